```python
import jax, jax.numpy as jnp
from jax import lax
import numpy as np

D_MODEL = 1024
BATCH = 8
SEQ = 8192
DEPTH = 2

EPS = 1e-6
NEG_INF = -1e30
CHUNK = 128
A_GROUPS = 4
A_WIDTH = D_MODEL
A_GROUP_DIM = A_WIDTH // A_GROUPS
B_WIDTH = D_MODEL
POOL_WINDOWS = (2, 4, 8, 16)
B_GROUPS = len(POOL_WINDOWS)
B_GROUP_DIM = B_WIDTH // B_GROUPS
MIX0_IN = 3 * A_WIDTH + 2 * B_WIDTH
MIX0_OUT = A_WIDTH + B_WIDTH
N_HEADS = 16
N_KV_HEADS = 4
HEAD_DIM = 64
GQA_GROUP = N_HEADS // N_KV_HEADS
C_WIDTH = N_HEADS * HEAD_DIM
KV_WIDTH = N_KV_HEADS * HEAD_DIM
MIX1_IN = 2 * C_WIDTH + 2 * KV_WIDTH
WINDOW = 128
ATTN_BLOCK = 128
ROPE_THETA = 500000.0
ROT_DIM = HEAD_DIM // 4

kernel_name = "hybrid_gmlp_pool_swa_encoder"


def rms_norm(x, g):
    xf = x.astype(jnp.float32)
    y = xf * lax.rsqrt(jnp.mean(xf * xf, axis=-1, keepdims=True) + EPS)
    return (y * g.astype(jnp.float32)).astype(x.dtype)


def spatial_gating_mixer(u, v, w_s, b_s, g_v):
    bn, s, _ = u.shape
    u = jax.nn.gelu(u)
    v = rms_norm(jax.nn.gelu(v), g_v)
    v = v.reshape(bn, s // CHUNK, CHUNK, A_GROUPS, A_GROUP_DIM)
    mixed = jnp.einsum('hij,bcjhd->bcihd', w_s, v) + b_s.T[None, None, :, :, None]
    return u * mixed.reshape(bn, s, A_WIDTH)


def multiscale_pool_mixer(xb, w_g, scale):
    bn, s, _ = xb.shape
    xf = xb.astype(jnp.float32)
    cs = jnp.concatenate([jnp.zeros((bn, 1, B_WIDTH), jnp.float32), lax.cumsum(xf, axis=1)], axis=1)
    t = jnp.arange(s)
    outs = []
    for gi, w in enumerate(POOL_WINDOWS):
        lo = jnp.clip(t - w // 2, 0, s)
        hi = jnp.clip(t + w // 2, 0, s)
        sl = slice(gi * B_GROUP_DIM, (gi + 1) * B_GROUP_DIM)
        csg = cs[:, :, sl]
        win_sum = jnp.take(csg, hi, axis=1) - jnp.take(csg, lo, axis=1)
        mean = win_sum / (hi - lo).astype(jnp.float32)[None, :, None]
        outs.append(mean - xf[:, :, sl])
    p = jnp.stack(outs, axis=2).astype(xb.dtype)
    y = jnp.einsum('bsgd,gde->bsge', p, w_g).reshape(bn, s, B_WIDTH)
    return y * scale


def partial_rope(x, pos):
    inv = ROPE_THETA ** (-jnp.arange(0, ROT_DIM, 2, dtype=jnp.float32) / ROT_DIM)
    ang = pos.astype(jnp.float32)[:, None] * inv[None, :]
    cos = jnp.cos(ang)[None, :, None, :]
    sin = jnp.sin(ang)[None, :, None, :]
    xr = x[..., :ROT_DIM].astype(jnp.float32)
    x1, x2 = xr[..., :ROT_DIM // 2], xr[..., ROT_DIM // 2:]
    rot = jnp.concatenate([x1 * cos - x2 * sin, x2 * cos + x1 * sin], axis=-1)
    return jnp.concatenate([rot.astype(x.dtype), x[..., ROT_DIM:]], axis=-1)


def windowed_gqa(q, k, v, sink):
    bn, s = q.shape[:2]
    nb = s // ATTN_BLOCK
    qb = q.reshape(bn, nb, ATTN_BLOCK, N_KV_HEADS, GQA_GROUP, HEAD_DIM)

    def band(t):
        tp = jnp.pad(t, ((0, 0), (ATTN_BLOCK, ATTN_BLOCK), (0, 0), (0, 0)))
        tp = tp.reshape(bn, nb + 2, ATTN_BLOCK, N_KV_HEADS, HEAD_DIM)
        return jnp.concatenate([tp[:, :nb], tp[:, 1:nb + 1], tp[:, 2:]], axis=2)

    kb, vb = band(k), band(v)
    scores = jnp.einsum('bnqkgd,bnskd->bnkgqs', qb, kb).astype(jnp.float32) * (HEAD_DIM ** -0.5)
    qi = jnp.arange(nb)[:, None, None] * ATTN_BLOCK + jnp.arange(ATTN_BLOCK)[None, :, None]
    kj = (jnp.arange(nb)[:, None, None] - 1) * ATTN_BLOCK + jnp.arange(3 * ATTN_BLOCK)[None, None, :]
    allowed = (jnp.abs(qi - kj) <= WINDOW) & (kj >= 0) & (kj < s)
    scores = jnp.where(allowed[None, :, None, None], scores, NEG_INF)
    sink_col = jnp.broadcast_to(sink.astype(jnp.float32).reshape(1, 1, N_KV_HEADS, GQA_GROUP, 1, 1),
                                scores.shape[:-1] + (1,))
    probs = jax.nn.softmax(jnp.concatenate([scores, sink_col], axis=-1), axis=-1)[..., :-1]
    out = jnp.einsum('bnkgqs,bnskd->bnqkgd', probs.astype(v.dtype), vb)
    return out.reshape(bn, s, C_WIDTH)


def _fwd_setup_inputs(seed: int = 0) -> dict:
    key = jax.random.key(seed)
    ks = jax.random.split(key, 16)
    f32 = jnp.float32
    nrm = lambda k, shp, sc: jax.random.normal(k, shp, f32) * sc
    return {
        "x": nrm(ks[0], (BATCH, SEQ, D_MODEL), 1.0),
        "norm_0": 1.0 + nrm(ks[1], (D_MODEL,), 0.05),
        "w_in_0": nrm(ks[2], (D_MODEL, MIX0_IN), D_MODEL ** -0.5),
        "a_v_norm_0": 1.0 + nrm(ks[3], (A_WIDTH,), 0.05),
        "a_spatial_w_0": nrm(ks[4], (A_GROUPS, CHUNK, CHUNK), CHUNK ** -0.5),
        "a_spatial_b_0": 1.0 + nrm(ks[5], (A_GROUPS, CHUNK), 0.1),
        "b_group_w_0": nrm(ks[6], (B_GROUPS, B_GROUP_DIM, B_GROUP_DIM), B_GROUP_DIM ** -0.5),
        "b_scale_0": 1.0 + nrm(ks[7], (B_WIDTH,), 0.05),
        "w_out_0": nrm(ks[8], (MIX0_OUT, D_MODEL), MIX0_OUT ** -0.5),
        "norm_1": 1.0 + nrm(ks[9], (D_MODEL,), 0.05),
        "w_in_1": nrm(ks[10], (D_MODEL, MIX1_IN), D_MODEL ** -0.5),
        "sink_1": nrm(ks[11], (N_HEADS,), 0.5),
        "w_out_1": nrm(ks[12], (C_WIDTH, D_MODEL), C_WIDTH ** -0.5),
        "final_norm": 1.0 + nrm(ks[13], (D_MODEL,), 0.05),
    }


def _fwd_reference(x, norm_0, w_in_0, a_v_norm_0, a_spatial_w_0, a_spatial_b_0, b_group_w_0, b_scale_0,
              w_out_0, norm_1, w_in_1, sink_1, w_out_1, final_norm):
    bn, s, _ = x.shape
    pos = jnp.arange(s)
    for layer in range(DEPTH):
        if layer % 2 == 0:
            h = rms_norm(x, norm_0)
            z = h @ w_in_0
            a_u, a_v, a_gate, b_x, b_gate = jnp.split(
                z, [A_WIDTH, 2 * A_WIDTH, 3 * A_WIDTH, 3 * A_WIDTH + B_WIDTH], axis=-1)
            ya = spatial_gating_mixer(a_u, a_v, a_spatial_w_0, a_spatial_b_0, a_v_norm_0) * jax.nn.silu(a_gate)
            yb = multiscale_pool_mixer(b_x, b_group_w_0, b_scale_0) * jax.nn.silu(b_gate)
            x = x + jnp.concatenate([ya, yb], axis=-1) @ w_out_0
        else:
            h = rms_norm(x, norm_1)
            z = h @ w_in_1
            q, k, v, gate = jnp.split(z, [C_WIDTH, C_WIDTH + KV_WIDTH, C_WIDTH + 2 * KV_WIDTH], axis=-1)
            q = partial_rope(q.reshape(bn, s, N_HEADS, HEAD_DIM), pos)
            k = partial_rope(k.reshape(bn, s, N_KV_HEADS, HEAD_DIM), pos)
            v = v.reshape(bn, s, N_KV_HEADS, HEAD_DIM)
            y = windowed_gqa(q, k, v, sink_1) * jax.nn.silu(gate)
            x = x + y @ w_out_1
    return rms_norm(x, final_norm)


import jax as _jax
import jax.numpy as _jnp

TWIN_FORMAT = 'train_step'
FWD_PARAMS = ['x', 'norm_0', 'w_in_0', 'a_v_norm_0', 'a_spatial_w_0', 'a_spatial_b_0', 'b_group_w_0', 'b_scale_0', 'w_out_0', 'norm_1', 'w_in_1', 'sink_1', 'w_out_1', 'final_norm']
TWIN_WEIGHTS = ['norm_0', 'w_in_0', 'a_v_norm_0', 'a_spatial_w_0', 'a_spatial_b_0', 'b_group_w_0', 'b_scale_0', 'w_out_0', 'norm_1', 'w_in_1', 'sink_1', 'w_out_1', 'final_norm']
TWIN_DIFF_INPUT = 'x'
TWIN_INPUTS = ['x', 'norm_0', 'w_in_0', 'a_v_norm_0', 'a_spatial_w_0', 'a_spatial_b_0', 'b_group_w_0', 'b_scale_0', 'w_out_0', 'norm_1', 'w_in_1', 'sink_1', 'w_out_1', 'final_norm', 'loss_target', 'm_norm_0', 'm_w_in_0', 'm_a_v_norm_0', 'm_a_spatial_w_0', 'm_a_spatial_b_0', 'm_b_group_w_0', 'm_b_scale_0', 'm_w_out_0', 'm_norm_1', 'm_w_in_1', 'm_sink_1', 'm_w_out_1', 'm_final_norm', 'v_norm_0', 'v_w_in_0', 'v_a_v_norm_0', 'v_a_spatial_w_0', 'v_a_spatial_b_0', 'v_b_group_w_0', 'v_b_scale_0', 'v_w_out_0', 'v_norm_1', 'v_w_in_1', 'v_sink_1', 'v_w_out_1', 'v_final_norm']
TWIN_OUTPUTS = ['loss', 'grad_x', 'grad_norm_0', 'grad_w_in_0', 'grad_a_v_norm_0', 'grad_a_spatial_w_0', 'grad_a_spatial_b_0', 'grad_b_group_w_0', 'grad_b_scale_0', 'grad_w_out_0', 'grad_norm_1', 'grad_w_in_1', 'grad_sink_1', 'grad_w_out_1', 'grad_final_norm', 'delta_norm_0', 'delta_w_in_0', 'delta_a_v_norm_0', 'delta_a_spatial_w_0', 'delta_a_spatial_b_0', 'delta_b_group_w_0', 'delta_b_scale_0', 'delta_w_out_0', 'delta_norm_1', 'delta_w_in_1', 'delta_sink_1', 'delta_w_out_1', 'delta_final_norm', 'new_m_norm_0', 'new_m_w_in_0', 'new_m_a_v_norm_0', 'new_m_a_spatial_w_0', 'new_m_a_spatial_b_0', 'new_m_b_group_w_0', 'new_m_b_scale_0', 'new_m_w_out_0', 'new_m_norm_1', 'new_m_w_in_1', 'new_m_sink_1', 'new_m_w_out_1', 'new_m_final_norm', 'new_v_norm_0', 'new_v_w_in_0', 'new_v_a_v_norm_0', 'new_v_a_spatial_w_0', 'new_v_a_spatial_b_0', 'new_v_b_group_w_0', 'new_v_b_scale_0', 'new_v_w_out_0', 'new_v_norm_1', 'new_v_w_in_1', 'new_v_sink_1', 'new_v_w_out_1', 'new_v_final_norm']
TWIN_LEAF_KINDS = {'loss': 'loss', 'grad_x': 'grad_x', 'grad_norm_0': 'grad_w', 'grad_w_in_0': 'grad_w', 'grad_a_v_norm_0': 'grad_w', 'grad_a_spatial_w_0': 'grad_w', 'grad_a_spatial_b_0': 'grad_w', 'grad_b_group_w_0': 'grad_w', 'grad_b_scale_0': 'grad_w', 'grad_w_out_0': 'grad_w', 'grad_norm_1': 'grad_w', 'grad_w_in_1': 'grad_w', 'grad_sink_1': 'grad_w', 'grad_w_out_1': 'grad_w', 'grad_final_norm': 'grad_w', 'delta_norm_0': 'delta_w', 'delta_w_in_0': 'delta_w', 'delta_a_v_norm_0': 'delta_w', 'delta_a_spatial_w_0': 'delta_w', 'delta_a_spatial_b_0': 'delta_w', 'delta_b_group_w_0': 'delta_w', 'delta_b_scale_0': 'delta_w', 'delta_w_out_0': 'delta_w', 'delta_norm_1': 'delta_w', 'delta_w_in_1': 'delta_w', 'delta_sink_1': 'delta_w', 'delta_w_out_1': 'delta_w', 'delta_final_norm': 'delta_w', 'new_m_norm_0': 'new_m', 'new_m_w_in_0': 'new_m', 'new_m_a_v_norm_0': 'new_m', 'new_m_a_spatial_w_0': 'new_m', 'new_m_a_spatial_b_0': 'new_m', 'new_m_b_group_w_0': 'new_m', 'new_m_b_scale_0': 'new_m', 'new_m_w_out_0': 'new_m', 'new_m_norm_1': 'new_m', 'new_m_w_in_1': 'new_m', 'new_m_sink_1': 'new_m', 'new_m_w_out_1': 'new_m', 'new_m_final_norm': 'new_m', 'new_v_norm_0': 'new_v', 'new_v_w_in_0': 'new_v', 'new_v_a_v_norm_0': 'new_v', 'new_v_a_spatial_w_0': 'new_v', 'new_v_a_spatial_b_0': 'new_v', 'new_v_b_group_w_0': 'new_v', 'new_v_b_scale_0': 'new_v', 'new_v_w_out_0': 'new_v', 'new_v_norm_1': 'new_v', 'new_v_w_in_1': 'new_v', 'new_v_sink_1': 'new_v', 'new_v_w_out_1': 'new_v', 'new_v_final_norm': 'new_v'}


def _forward(args):
    return _fwd_reference(*[args[k] for k in FWD_PARAMS])


def _output_shape():
    def fwd():
        inp = _fwd_setup_inputs(0)
        return _fwd_reference(*[inp[k] for k in FWD_PARAMS])
    out = _jax.eval_shape(fwd)
    return out.shape, out.dtype

N_MICROBATCH = 1
ADAM_LR = 0.001
ADAM_B1 = 0.9
ADAM_B2 = 0.999
ADAM_EPS = 1e-08
ADAM_WD = 0.01
ADAM_STEP = 10
PER_EXAMPLE_BATCH_AXIS = {'x': 0, 'loss_target': 0}
SHARED_INPUTS = []
_WEIGHT_DTYPES = {'norm_0': _jnp.float32, 'w_in_0': _jnp.float32, 'a_v_norm_0': _jnp.float32, 'a_spatial_w_0': _jnp.float32, 'a_spatial_b_0': _jnp.float32, 'b_group_w_0': _jnp.float32, 'b_scale_0': _jnp.float32, 'w_out_0': _jnp.float32, 'norm_1': _jnp.float32, 'w_in_1': _jnp.float32, 'sink_1': _jnp.float32, 'w_out_1': _jnp.float32, 'final_norm': _jnp.float32}
MOMENT_SCALE = {'norm_0': 1.843142e-01, 'w_in_0': 8.295204e-02, 'a_v_norm_0': 7.013180e-02, 'a_spatial_w_0': 8.711906e-02, 'a_spatial_b_0': 8.933377e-02, 'b_group_w_0': 8.317184e-02, 'b_scale_0': 8.969663e-02, 'w_out_0': 1.217709e-01, 'norm_1': 3.287364e-02, 'w_in_1': 2.083655e-02, 'sink_1': 6.293616e-04, 'w_out_1': 1.684219e-02, 'final_norm': 6.415082e+01}


def _to_microbatches(a, axis):
    t = _jnp.moveaxis(a, axis, 0)
    t = t.reshape((N_MICROBATCH, t.shape[0] // N_MICROBATCH) + t.shape[1:])
    return _jnp.moveaxis(t, 1, axis + 1)


def setup_inputs(seed: int = 0) -> dict:
    inp = _fwd_setup_inputs(seed)
    key = _jax.random.fold_in(_jax.random.key(seed), 7919)
    shape, _ = _output_shape()
    out = dict(inp)
    out["loss_target"] = _jax.random.normal(_jax.random.fold_in(key, 0), shape, _jnp.float32)
    for i, name in enumerate(TWIN_WEIGHTS):
        w = inp[name].astype(_jnp.float32)
        if MOMENT_SCALE is None:
            s = _jnp.sqrt(_jnp.mean(_jnp.square(w)) + 1e-30)
        else:
            s = MOMENT_SCALE[name]
        km, kv = _jax.random.split(_jax.random.fold_in(key, i + 1))
        out[name] = w
        out["m_" + name] = s * _jax.random.normal(km, w.shape, _jnp.float32)
        out["v_" + name] = (s * s) * _jax.random.uniform(kv, w.shape, _jnp.float32, 0.5, 1.5)
    if N_MICROBATCH > 1:
        for name, axis in PER_EXAMPLE_BATCH_AXIS.items():
            out[name] = _to_microbatches(out[name], axis)
    return {'x': out['x'], 'norm_0': out['norm_0'], 'w_in_0': out['w_in_0'], 'a_v_norm_0': out['a_v_norm_0'], 'a_spatial_w_0': out['a_spatial_w_0'], 'a_spatial_b_0': out['a_spatial_b_0'], 'b_group_w_0': out['b_group_w_0'], 'b_scale_0': out['b_scale_0'], 'w_out_0': out['w_out_0'], 'norm_1': out['norm_1'], 'w_in_1': out['w_in_1'], 'sink_1': out['sink_1'], 'w_out_1': out['w_out_1'], 'final_norm': out['final_norm'], 'loss_target': out['loss_target'], 'm_norm_0': out['m_norm_0'], 'm_w_in_0': out['m_w_in_0'], 'm_a_v_norm_0': out['m_a_v_norm_0'], 'm_a_spatial_w_0': out['m_a_spatial_w_0'], 'm_a_spatial_b_0': out['m_a_spatial_b_0'], 'm_b_group_w_0': out['m_b_group_w_0'], 'm_b_scale_0': out['m_b_scale_0'], 'm_w_out_0': out['m_w_out_0'], 'm_norm_1': out['m_norm_1'], 'm_w_in_1': out['m_w_in_1'], 'm_sink_1': out['m_sink_1'], 'm_w_out_1': out['m_w_out_1'], 'm_final_norm': out['m_final_norm'], 'v_norm_0': out['v_norm_0'], 'v_w_in_0': out['v_w_in_0'], 'v_a_v_norm_0': out['v_a_v_norm_0'], 'v_a_spatial_w_0': out['v_a_spatial_w_0'], 'v_a_spatial_b_0': out['v_a_spatial_b_0'], 'v_b_group_w_0': out['v_b_group_w_0'], 'v_b_scale_0': out['v_b_scale_0'], 'v_w_out_0': out['v_w_out_0'], 'v_norm_1': out['v_norm_1'], 'v_w_in_1': out['v_w_in_1'], 'v_sink_1': out['v_sink_1'], 'v_w_out_1': out['v_w_out_1'], 'v_final_norm': out['v_final_norm']}


def _loss(weights, diff, rest, loss_target):
    with _jax.named_scope("forward"):
        args = {**rest, TWIN_DIFF_INPUT: diff, **{k: w.astype(_WEIGHT_DTYPES[k]) for k, w in weights.items()}}
        y = _forward(args)
    with _jax.named_scope("loss_head"):
        err = _jnp.square(y.astype(_jnp.float32) - loss_target)
        return 0.5 * _jnp.sum(_jnp.mean(err, axis=-1)) if err.ndim else 0.5 * err


def _adamw(w, g, m, v):
    m = ADAM_B1 * m + (1.0 - ADAM_B1) * g
    v = ADAM_B2 * v + (1.0 - ADAM_B2) * _jnp.square(g)
    m_hat = m / (1.0 - ADAM_B1 ** ADAM_STEP)
    v_hat = v / (1.0 - ADAM_B2 ** ADAM_STEP)
    delta = -ADAM_LR * (m_hat / (_jnp.sqrt(v_hat) + ADAM_EPS) + ADAM_WD * w)
    return delta, m, v


def reference(x, norm_0, w_in_0, a_v_norm_0, a_spatial_w_0, a_spatial_b_0, b_group_w_0, b_scale_0, w_out_0, norm_1, w_in_1, sink_1, w_out_1, final_norm, loss_target, m_norm_0, m_w_in_0, m_a_v_norm_0, m_a_spatial_w_0, m_a_spatial_b_0, m_b_group_w_0, m_b_scale_0, m_w_out_0, m_norm_1, m_w_in_1, m_sink_1, m_w_out_1, m_final_norm, v_norm_0, v_w_in_0, v_a_v_norm_0, v_a_spatial_w_0, v_a_spatial_b_0, v_b_group_w_0, v_b_scale_0, v_w_out_0, v_norm_1, v_w_in_1, v_sink_1, v_w_out_1, v_final_norm):
    given = dict(x=x, norm_0=norm_0, w_in_0=w_in_0, a_v_norm_0=a_v_norm_0, a_spatial_w_0=a_spatial_w_0, a_spatial_b_0=a_spatial_b_0, b_group_w_0=b_group_w_0, b_scale_0=b_scale_0, w_out_0=w_out_0, norm_1=norm_1, w_in_1=w_in_1, sink_1=sink_1, w_out_1=w_out_1, final_norm=final_norm, loss_target=loss_target, m_norm_0=m_norm_0, m_w_in_0=m_w_in_0, m_a_v_norm_0=m_a_v_norm_0, m_a_spatial_w_0=m_a_spatial_w_0, m_a_spatial_b_0=m_a_spatial_b_0, m_b_group_w_0=m_b_group_w_0, m_b_scale_0=m_b_scale_0, m_w_out_0=m_w_out_0, m_norm_1=m_norm_1, m_w_in_1=m_w_in_1, m_sink_1=m_sink_1, m_w_out_1=m_w_out_1, m_final_norm=m_final_norm, v_norm_0=v_norm_0, v_w_in_0=v_w_in_0, v_a_v_norm_0=v_a_v_norm_0, v_a_spatial_w_0=v_a_spatial_w_0, v_a_spatial_b_0=v_a_spatial_b_0, v_b_group_w_0=v_b_group_w_0, v_b_scale_0=v_b_scale_0, v_w_out_0=v_w_out_0, v_norm_1=v_norm_1, v_w_in_1=v_w_in_1, v_sink_1=v_sink_1, v_w_out_1=v_w_out_1, v_final_norm=v_final_norm)
    weights = {n: given[n] for n in TWIN_WEIGHTS}
    shared = {n: given[n] for n in SHARED_INPUTS}
    per_example = {n: given[n] for n in ['x']}
    grad_fn = _jax.value_and_grad(_loss, argnums=(0, 1))

    def one_microbatch(ex, loss_target):
        ex = dict(ex)
        diff = ex.pop(TWIN_DIFF_INPUT)
        return grad_fn(weights, diff, {**shared, **ex}, loss_target)

    if N_MICROBATCH == 1:
        loss, (grad_w, grad_x) = one_microbatch(per_example, given["loss_target"])
    else:
        def body(carry, xs):
            loss_sum, grad_sum = carry
            l_k, (gw_k, gx_k) = one_microbatch(xs[0], xs[1])
            with _jax.named_scope("update"):
                return (loss_sum + l_k, _jax.tree.map(_jnp.add, grad_sum, gw_k)), gx_k

        init = (_jnp.zeros((), _jnp.float32), _jax.tree.map(_jnp.zeros_like, weights))
        (loss, grad_w), grad_x = _jax.lax.scan(body, init, (per_example, given["loss_target"]))
    with _jax.named_scope("update"):
        delta_w, new_m, new_v = {}, {}, {}
        for n in TWIN_WEIGHTS:
            delta_w[n], new_m[n], new_v[n] = _adamw(weights[n], grad_w[n], given["m_" + n], given["v_" + n])
    return (loss, grad_x, *[grad_w[n] for n in TWIN_WEIGHTS], *[delta_w[n] for n in TWIN_WEIGHTS],
            *[new_m[n] for n in TWIN_WEIGHTS], *[new_v[n] for n in TWIN_WEIGHTS])
```

```python
import functools

import jax
import jax.numpy as jnp
from jax import lax
from jax.experimental import pallas as pl
from jax.experimental.pallas import tpu as pltpu

F32 = jnp.float32
BF16 = jnp.bfloat16

D = 1024
EPS = 1e-6
NEG_INF = -1e30
CHUNK = 128
A_GROUPS = 4
POOL_WINDOWS = (2, 4, 8, 16)
POOL_HALO = 8
GDIM = 256
N_HEADS = 16
N_KV = 4
GQA = 4
HD = 64
BLK = 128
ROT_HALF = 8
ROPE_THETA = 500000.0
SCALE = HD ** -0.5
MIX0_IN = 5 * D
MIX1_IN = 2560

ADAM_LR = 0.001
ADAM_B1 = 0.9
ADAM_B2 = 0.999
ADAM_EPS = 1e-08
ADAM_WD = 0.01
ADAM_STEP = 10

N_DEV = 8
LANES = 128
MIB = 2 ** 20
MESH = pl.DeviceIdType.MESH

PACK_ROWS = (1024 * 640 // LANES, 4 * 32 * 256 // LANES, 256 * 1024 // LANES, 1024 * 320 // LANES, 128 * 1024 // LANES)
PACK_TOTAL = sum(PACK_ROWS)
SMALL_SIZES = (1024, 1024, 4 * 128 * 128, 4 * 128, 1024, 1024, 128, 1024)
SMALL_ROWS = 560


def _params(limit_mib, n_axes=1):
    return pltpu.CompilerParams(vmem_limit_bytes=limit_mib * MIB, dimension_semantics=("arbitrary",) * n_axes)


def _resident(shape):
    nd = len(shape)
    return pl.BlockSpec(shape, lambda *_: (0,) * nd, pipeline_mode=pl.Buffered(1))


def _gelu(x):
    k = 0.7978845608028654
    return 0.5 * x * (1.0 + jnp.tanh(k * (x + 0.044715 * x * x * x)))


def _gelu_and_grad(x):
    k = 0.7978845608028654
    x2 = x * x
    t = jnp.tanh(k * (x + 0.044715 * x * x2))
    g = 0.5 * x * (1.0 + t)
    dg = 0.5 * (1.0 + t) + 0.5 * x * (1.0 - t * t) * (k * (1.0 + 3.0 * 0.044715 * x2))
    return g, dg


def _silu_and_grad(x):
    s = jax.nn.sigmoid(x)
    return x * s, s * (1.0 + x * (1.0 - s))


def _nt(a, b):
    return lax.dot_general(a, b, (((1,), (1,)), ((), ())), preferred_element_type=F32)


def _tn(a, b):
    return lax.dot_general(a, b, (((0,), (0,)), ((), ())), preferred_element_type=F32)


def _mm(a, b):
    return jnp.dot(a, b, preferred_element_type=F32)


def _rope_apply(x, c, sa, sb):
    outs = []
    for s in range(x.shape[1] // LANES):
        xs = x[:, s * LANES:(s + 1) * LANES]
        outs.append(xs * c + pltpu.roll(xs, ROT_HALF, 1) * sa + pltpu.roll(xs, LANES - ROT_HALF, 1) * sb)
    return jnp.concatenate(outs, axis=1)


def _rope_transpose(dy, c, sa, sb):
    outs = []
    for s in range(dy.shape[1] // LANES):
        ds = dy[:, s * LANES:(s + 1) * LANES]
        outs.append(ds * c + pltpu.roll(ds * sa, LANES - ROT_HALF, 1) + pltpu.roll(ds * sb, ROT_HALF, 1))
    return jnp.concatenate(outs, axis=1)


def _rope_tables(seq):
    inv = ROPE_THETA ** (-jnp.arange(0, 2 * ROT_HALF, 2, dtype=F32) / (2 * ROT_HALF))
    ang = jnp.arange(seq, dtype=F32)[:, None] * inv[None, :]
    cos, sin = jnp.cos(ang), jnp.sin(ang)
    one = jnp.ones((seq, HD - 2 * ROT_HALF), F32)
    zero8 = jnp.zeros((seq, ROT_HALF), F32)
    zero = jnp.zeros((seq, HD - 2 * ROT_HALF), F32)
    c = jnp.concatenate([cos, cos, one], axis=1)
    sa = jnp.concatenate([zero8, sin, zero], axis=1)
    sb = jnp.concatenate([-sin, zero8, zero], axis=1)
    return tuple(jnp.tile(t, (1, LANES // HD)) for t in (c, sa, sb))


def _l0_in_proj(x, g0, w):
    seq = x.shape[0]
    tm = 512

    def body(x_ref, g_ref, w_ref, za_ref, bx_ref, bg_ref, ht_ref):
        xf = x_ref[...]
        r = lax.rsqrt(jnp.mean(xf * xf, axis=1, keepdims=True) + EPS)
        h = (xf * r * g_ref[...]).astype(BF16)
        ht_ref[...] = h.T
        for j in range(3):
            za_ref[:, j * D:(j + 1) * D] = _mm(h, w_ref[:, j * D:(j + 1) * D]).astype(BF16)
        bx_ref[...] = _mm(h, w_ref[:, 3 * D:4 * D])
        bg_ref[...] = _mm(h, w_ref[:, 4 * D:5 * D]).astype(BF16)

    return pl.pallas_call(
        body, grid=(seq // tm,), name="l0_in_proj",
        out_shape=(jax.ShapeDtypeStruct((seq, 3 * D), BF16), jax.ShapeDtypeStruct((seq, D), F32),
                   jax.ShapeDtypeStruct((seq, D), BF16), jax.ShapeDtypeStruct((D, seq), BF16)),
        in_specs=[pl.BlockSpec((tm, D), lambda i: (i, 0)), _resident((1, D)), _resident((D, MIX0_IN))],
        out_specs=(pl.BlockSpec((tm, 3 * D), lambda i: (i, 0)), pl.BlockSpec((tm, D), lambda i: (i, 0)),
                   pl.BlockSpec((tm, D), lambda i: (i, 0)), pl.BlockSpec((D, tm), lambda i: (0, i))),
        compiler_params=_params(48),
    )(x, g0, w)


def _fill_halo(ext_ref, cur, prev_ref, next_ref, i, n_tiles, ts):
    ext_ref[pl.ds(0, POOL_HALO), :] = jnp.where(i > 0, prev_ref[...], 0.0)
    ext_ref[pl.ds(POOL_HALO, ts), :] = cur
    ext_ref[pl.ds(POOL_HALO + ts, POOL_HALO), :] = jnp.where(i < n_tiles - 1, next_ref[...], 0.0)


def _pool_forward(xe_ref, ts, t0, seq):
    tg = t0 + lax.broadcasted_iota(jnp.int32, (ts, 1), 0)
    outs = []
    for gi, w in enumerate(POOL_WINDOWS):
        hw = w // 2
        cols = slice(gi * GDIM, (gi + 1) * GDIM)
        acc = xe_ref[pl.ds(POOL_HALO - hw, ts), cols]
        for k in range(-hw + 1, hw):
            acc = acc + xe_ref[pl.ds(POOL_HALO + k, ts), cols]
        cnt = (jnp.minimum(tg + hw, seq) - jnp.maximum(tg - hw, 0)).astype(F32)
        outs.append(acc / cnt - xe_ref[pl.ds(POOL_HALO, ts), cols])
    return jnp.concatenate(outs, axis=1)


def _spatial_mix(ws_ref, vnb, bias, ts):
    rows = []
    for c in range(ts // CHUNK):
        vc = vnb[c * CHUNK:(c + 1) * CHUNK, :]
        rows.append(jnp.concatenate(
            [_mm(ws_ref[h], vc[:, h * GDIM:(h + 1) * GDIM]) for h in range(A_GROUPS)], axis=1) + bias)
    return jnp.concatenate(rows, axis=0)


def _halo_specs(ts, seq, width):
    per = ts // POOL_HALO
    last = seq // POOL_HALO - 1
    prev = pl.BlockSpec((POOL_HALO, width), lambda i: (jnp.maximum(i * per - 1, 0), 0))
    nxt = pl.BlockSpec((POOL_HALO, width), lambda i: (jnp.minimum((i + 1) * per, last), 0))
    return prev, nxt


def _l0_mix_fwd(za, bx, bg, x, ws, bias, gv, wg, scale, wout):
    seq = x.shape[0]
    ts = 512
    n_tiles = seq // ts

    def body(za_ref, bx_ref, bxp_ref, bxn_ref, bg_ref, x_ref, ws_ref, bias_ref, gv_ref, wg_ref, sc_ref, wo_ref,
             x1_ref, xe_ref):
        i = pl.program_id(0)
        u = _gelu(za_ref[:, 0:D].astype(F32))
        vg = _gelu(za_ref[:, D:2 * D].astype(F32))
        rv = lax.rsqrt(jnp.mean(vg * vg, axis=1, keepdims=True) + EPS)
        vnb = (vg * rv * gv_ref[...]).astype(BF16)
        mixed = _spatial_mix(ws_ref, vnb, bias_ref[...], ts)
        ag = za_ref[:, 2 * D:3 * D].astype(F32)
        ya = (u * mixed * (ag * jax.nn.sigmoid(ag))).astype(BF16)

        _fill_halo(xe_ref, bx_ref[...], bxp_ref, bxn_ref, i, n_tiles, ts)
        pb = _pool_forward(xe_ref, ts, i * ts, seq).astype(BF16)
        y = jnp.concatenate([_mm(pb[:, g * GDIM:(g + 1) * GDIM], wg_ref[g]) for g in range(4)], axis=1) * sc_ref[...]
        bgf = bg_ref[...].astype(F32)
        yb = (y * (bgf * jax.nn.sigmoid(bgf))).astype(BF16)
        x1_ref[...] = x_ref[...] + _mm(ya, wo_ref[0:D, :]) + _mm(yb, wo_ref[D:2 * D, :])

    prev, nxt = _halo_specs(ts, seq, D)
    row = lambda w: pl.BlockSpec((ts, w), lambda i: (i, 0))
    return pl.pallas_call(
        body, grid=(n_tiles,), name="l0_mix_fwd",
        out_shape=jax.ShapeDtypeStruct((seq, D), F32),
        in_specs=[row(3 * D), row(D), prev, nxt, row(D), row(D), _resident((4, CHUNK, CHUNK)), _resident((CHUNK, D)),
                  _resident((1, D)), _resident((4, GDIM, GDIM)), _resident((1, D)), _resident((2 * D, D))],
        out_specs=row(D),
        scratch_shapes=[pltpu.VMEM((ts + 2 * POOL_HALO, D), F32)],
        compiler_params=_params(56),
    )(za, bx, bx, bx, bg, x, ws, bias, gv, wg, scale, wout)


def _l1_in_proj(x1, g1, w, rope_c, rope_sa, rope_sb):
    seq = x1.shape[0]
    tm = 512

    def body(x_ref, g_ref, w_ref, c_ref, sa_ref, sb_ref, z_ref, ht_ref):
        xf = x_ref[...]
        r = lax.rsqrt(jnp.mean(xf * xf, axis=1, keepdims=True) + EPS)
        h = (xf * r * g_ref[...]).astype(BF16)
        ht_ref[...] = h.T
        c, sa, sb = c_ref[...], sa_ref[...], sb_ref[...]
        z_ref[:, 0:D] = _rope_apply(_mm(h, w_ref[:, 0:D]), c, sa, sb).astype(BF16)
        z_ref[:, D:2 * D] = _mm(h, w_ref[:, D:2 * D]).astype(BF16)
        z_ref[:, 2 * D:2 * D + 256] = _rope_apply(_mm(h, w_ref[:, 2 * D:2 * D + 256]), c, sa, sb).astype(BF16)
        z_ref[:, 2 * D + 256:MIX1_IN] = _mm(h, w_ref[:, 2 * D + 256:MIX1_IN]).astype(BF16)

    row = lambda w_: pl.BlockSpec((tm, w_), lambda i: (i, 0))
    return pl.pallas_call(
        body, grid=(seq // tm,), name="l1_in_proj",
        out_shape=(jax.ShapeDtypeStruct((seq, MIX1_IN), BF16), jax.ShapeDtypeStruct((D, seq), BF16)),
        in_specs=[row(D), _resident((1, D)), _resident((D, MIX1_IN)), row(LANES), row(LANES), row(LANES)],
        out_specs=(row(MIX1_IN), pl.BlockSpec((D, tm), lambda i: (0, i))),
        compiler_params=_params(48),
    )(x1, g1, w, rope_c, rope_sa, rope_sb)


TQ = 512
KV_COL = 2 * D // 256


def _band_specs(nq, nb, clamp_i):
    def mk(col):
        cur = pl.BlockSpec((TQ, 256), lambda i: (clamp_i(i), col))
        prev = pl.BlockSpec((BLK, 256), lambda i: (jnp.maximum(clamp_i(i) * (TQ // BLK) - 1, 0), col))
        nxt = pl.BlockSpec((BLK, 256), lambda i: (jnp.minimum((clamp_i(i) + 1) * (TQ // BLK), nb - 1), col))
        return [prev, cur, nxt]
    return mk(KV_COL) + mk(KV_COL + 1)


def _band_mask(n, nb):
    r = lax.broadcasted_iota(jnp.int32, (BLK, 3 * BLK), 0)
    c = lax.broadcasted_iota(jnp.int32, (BLK, 3 * BLK), 1)
    m = (c >= r) & (c <= r + 2 * BLK) & ((c >= BLK) | (n > 0)) & ((c < 2 * BLK) | (n < nb - 1))
    return jnp.concatenate([m] * GQA, axis=0)


def _stack_heads(t, kv):
    return jnp.concatenate([t[:, (kv * GQA + g) * HD:(kv * GQA + g + 1) * HD] for g in range(GQA)], axis=0)


def _sink_col(sink_ref, kv):
    return jnp.concatenate([jnp.full((BLK, 1), sink_ref[kv * GQA + g], F32) for g in range(GQA)], axis=0)


def _l1_attn_fwd(z1, x1, tgt, wout, gf, sink):
    seq = x1.shape[0]
    nq, nb = seq // TQ, seq // BLK

    def body(q_ref, gate_ref, kp_ref, k_ref, kn_ref, vp_ref, v_ref, vn_ref, x1_ref, tgt_ref, wo_ref, gf_ref, sink_ref,
             dx2_ref, dx2b_ref, yt_ref, att_ref, lse_ref, loss_ref, dgf_ref, kbuf, vbuf, att_scr):
        i = pl.program_id(0)

        @pl.when(i == 0)
        def _():
            loss_ref[...] = jnp.zeros_like(loss_ref)
            dgf_ref[...] = jnp.zeros_like(dgf_ref)

        for buf, (p_, c_, n_) in ((kbuf, (kp_ref, k_ref, kn_ref)), (vbuf, (vp_ref, v_ref, vn_ref))):
            buf[pl.ds(0, BLK), :] = p_[...]
            buf[pl.ds(BLK, TQ), :] = c_[...]
            buf[pl.ds(BLK + TQ, BLK), :] = n_[...]
        lane = lax.broadcasted_iota(jnp.int32, (BLK, LANES), 1)

        def jbody(j, carry):
            r0 = pl.multiple_of(j * BLK, BLK)
            mask = _band_mask(i * (TQ // BLK) + j, nb)
            qj = q_ref[pl.ds(r0, BLK), :]
            kb = kbuf[pl.ds(r0, 3 * BLK), :]
            vb = vbuf[pl.ds(r0, 3 * BLK), :]
            outs = []
            lse_blk = jnp.zeros((BLK, LANES), F32)
            for kv in range(N_KV):
                q4 = _stack_heads(qj, kv)
                s = jnp.where(mask, _nt(q4, kb[:, kv * HD:(kv + 1) * HD]) * SCALE, NEG_INF)
                sk = _sink_col(sink_ref, kv)
                m = jnp.maximum(jnp.max(s, axis=1, keepdims=True), sk)
                p = jnp.exp(s - m)
                den = jnp.sum(p, axis=1, keepdims=True) + jnp.exp(sk - m)
                o = _mm(p.astype(BF16), vb[:, kv * HD:(kv + 1) * HD]) / den
                lse = m + jnp.log(den)
                for g in range(GQA):
                    outs.append(o[g * BLK:(g + 1) * BLK, :])
                    lse_blk = lse_blk + jnp.where(lane == kv * GQA + g, lse[g * BLK:(g + 1) * BLK, :], 0.0)
            att_scr[pl.ds(r0, BLK), :] = jnp.concatenate(outs, axis=1)
            lse_ref[pl.ds(r0, BLK), :] = lse_blk
            return carry

        lax.fori_loop(0, TQ // BLK, jbody, 0)

        att = att_scr[...]
        gate = gate_ref[...].astype(F32)
        y = (att * (gate * jax.nn.sigmoid(gate))).astype(BF16)
        yt_ref[...] = y.T
        att_ref[...] = att.astype(BF16)
        x2 = x1_ref[...] + _mm(y, wo_ref[...])
        r = lax.rsqrt(jnp.mean(x2 * x2, axis=1, keepdims=True) + EPS)
        xn = x2 * r
        diff = xn * gf_ref[...] - tgt_ref[...]
        loss_ref[...] += 0.5 * jnp.sum(jnp.mean(diff * diff, axis=1, keepdims=True), axis=0, keepdims=True)
        dout = diff * (1.0 / D)
        dgf_ref[...] += jnp.sum(dout * xn, axis=0, keepdims=True)
        dxn = dout * gf_ref[...]
        dx2 = r * (dxn - xn * jnp.mean(dxn * xn, axis=1, keepdims=True))
        dx2_ref[...] = dx2
        dx2b_ref[...] = dx2.astype(BF16)

    ident = lambda i: i
    row = lambda w_, col=0: pl.BlockSpec((TQ, w_), lambda i: (i, col))
    return pl.pallas_call(
        body, grid=(nq,), name="l1_attn_fwd",
        out_shape=(jax.ShapeDtypeStruct((seq, D), F32), jax.ShapeDtypeStruct((seq, D), BF16),
                   jax.ShapeDtypeStruct((D, seq), BF16), jax.ShapeDtypeStruct((seq, D), BF16),
                   jax.ShapeDtypeStruct((seq, LANES), F32), jax.ShapeDtypeStruct((1, 1), F32),
                   jax.ShapeDtypeStruct((1, D), F32)),
        in_specs=[row(D, 0), row(D, 1)] + _band_specs(nq, nb, ident) + [
            row(D), row(D), _resident((D, D)), _resident((1, D)), pl.BlockSpec(memory_space=pltpu.SMEM)],
        out_specs=(row(D), row(D), pl.BlockSpec((D, TQ), lambda i: (0, i)), row(D), row(LANES),
                   pl.BlockSpec((1, 1), lambda i: (0, 0)), pl.BlockSpec((1, D), lambda i: (0, 0))),
        scratch_shapes=[pltpu.VMEM((TQ + 2 * BLK, 256), BF16), pltpu.VMEM((TQ + 2 * BLK, 256), BF16),
                        pltpu.VMEM((TQ, D), F32)],
        compiler_params=_params(56),
    )(z1, z1, z1, z1, z1, z1, z1, z1, x1, tgt, wout, gf, sink)


def _l1_attn_bwd(dx2b, wout_t, z1, att, lse, sink):
    seq = dx2b.shape[0]
    nq, nb = seq // TQ, seq // BLK

    def body(dx_ref, wot_ref, q_ref, gate_ref, kp_ref, k_ref, kn_ref, vp_ref, v_ref, vn_ref, att_ref, lse_ref, sink_ref,
             dq_ref, dgate_ref, dk_ref, dv_ref, dsink_ref, kbuf, vbuf, dkacc, dvacc, dat_scr, dl_scr):
        i = pl.program_id(0)

        @pl.when(i == 0)
        def _():
            dkacc[...] = jnp.zeros_like(dkacc)
            dvacc[...] = jnp.zeros_like(dvacc)
            dsink_ref[...] = jnp.zeros_like(dsink_ref)

        @pl.when(i > 0)
        def _():
            for acc in (dkacc, dvacc):
                acc[pl.ds(0, 2 * BLK), :] = acc[pl.ds(TQ, 2 * BLK), :]
                acc[pl.ds(2 * BLK, TQ), :] = jnp.zeros((TQ, 256), F32)

        @pl.when(i < nq)
        def _():
            for buf, (p_, c_, n_) in ((kbuf, (kp_ref, k_ref, kn_ref)), (vbuf, (vp_ref, v_ref, vn_ref))):
                buf[pl.ds(0, BLK), :] = p_[...]
                buf[pl.ds(BLK, TQ), :] = c_[...]
                buf[pl.ds(BLK + TQ, BLK), :] = n_[...]
            dy = _mm(dx_ref[...], wot_ref[...])
            gate = gate_ref[...].astype(F32)
            sg, dsg = _silu_and_grad(gate)
            attf = att_ref[...].astype(F32)
            dat = dy * sg
            dat_scr[...] = dat.astype(BF16)
            dl_scr[...] = dat * attf
            dgate_ref[...] = (dy * attf * dsg).astype(BF16)
            lane = lax.broadcasted_iota(jnp.int32, (1, LANES), 1)

            def jbody(j, dsink):
                r0 = pl.multiple_of(j * BLK, BLK)
                mask = _band_mask(i * (TQ // BLK) + j, nb)
                qj = q_ref[pl.ds(r0, BLK), :]
                doj = dat_scr[pl.ds(r0, BLK), :]
                dlj = dl_scr[pl.ds(r0, BLK), :]
                lsej = lse_ref[pl.ds(r0, BLK), :]
                kb = kbuf[pl.ds(r0, 3 * BLK), :]
                vb = vbuf[pl.ds(r0, 3 * BLK), :]
                dq_parts, dk_parts, dv_parts = [], [], []
                for kv in range(N_KV):
                    q4 = _stack_heads(qj, kv)
                    do4 = _stack_heads(doj, kv)
                    delta4 = jnp.concatenate(
                        [jnp.sum(dlj[:, (kv * GQA + g) * HD:(kv * GQA + g + 1) * HD], axis=1, keepdims=True)
                         for g in range(GQA)], axis=0)
                    lse4 = jnp.concatenate([lsej[:, kv * GQA + g:kv * GQA + g + 1] for g in range(GQA)], axis=0)
                    kk = kb[:, kv * HD:(kv + 1) * HD]
                    vv = vb[:, kv * HD:(kv + 1) * HD]
                    s = _nt(q4, kk) * SCALE
                    p = jnp.where(mask, jnp.exp(s - lse4), 0.0)
                    dp = _nt(do4, vv)
                    ds = (p * (dp - delta4) * SCALE).astype(BF16)
                    dq4 = _mm(ds, kk)
                    dk_parts.append(_tn(ds, q4))
                    dv_parts.append(_tn(p.astype(BF16), do4))
                    dsk = -jnp.exp(_sink_col(sink_ref, kv) - lse4) * delta4
                    for g in range(GQA):
                        dq_parts.append(dq4[g * BLK:(g + 1) * BLK, :])
                        tot = jnp.sum(dsk[g * BLK:(g + 1) * BLK, :], axis=0, keepdims=True)
                        dsink = dsink + jnp.where(lane == kv * GQA + g, tot, 0.0)
                dq_ref[pl.ds(r0, BLK), :] = jnp.concatenate(dq_parts, axis=1).astype(BF16)
                dkacc[pl.ds(r0, 3 * BLK), :] += jnp.concatenate(dk_parts, axis=1)
                dvacc[pl.ds(r0, 3 * BLK), :] += jnp.concatenate(dv_parts, axis=1)
                return dsink

            dsink = lax.fori_loop(0, TQ // BLK, jbody, jnp.zeros((1, LANES), F32))
            dsink_ref[...] += dsink

        dk_ref[...] = dkacc[pl.ds(0, TQ), :].astype(BF16)
        dv_ref[...] = dvacc[pl.ds(0, TQ), :].astype(BF16)

    clamp = lambda i: jnp.minimum(i, nq - 1)
    row = lambda w_, col=0: pl.BlockSpec((TQ, w_), lambda i: (clamp(i), col))
    return pl.pallas_call(
        body, grid=(nq + 1,), name="l1_attn_bwd",
        out_shape=(jax.ShapeDtypeStruct((seq, D), BF16), jax.ShapeDtypeStruct((seq, D), BF16),
                   jax.ShapeDtypeStruct((seq + TQ, 256), BF16), jax.ShapeDtypeStruct((seq + TQ, 256), BF16),
                   jax.ShapeDtypeStruct((1, LANES), F32)),
        in_specs=[row(D), _resident((D, D)), row(D, 0), row(D, 1)] + _band_specs(nq, nb, clamp) + [
            row(D), row(LANES), pl.BlockSpec(memory_space=pltpu.SMEM)],
        out_specs=(row(D), row(D), pl.BlockSpec((TQ, 256), lambda i: (i, 0)), pl.BlockSpec((TQ, 256), lambda i: (i, 0)),
                   pl.BlockSpec((1, LANES), lambda i: (0, 0))),
        scratch_shapes=[pltpu.VMEM((TQ + 2 * BLK, 256), BF16), pltpu.VMEM((TQ + 2 * BLK, 256), BF16),
                        pltpu.VMEM((TQ + 2 * BLK, 256), F32), pltpu.VMEM((TQ + 2 * BLK, 256), F32),
                        pltpu.VMEM((TQ, D), BF16), pltpu.VMEM((TQ, D), F32)],
        compiler_params=_params(56),
    )(dx2b, wout_t, z1, z1, z1, z1, z1, z1, z1, z1, att, lse, sink)


def _l1_in_proj_bwd(dq_r, dgate, dk_r, dv, rope_c, rope_sa, rope_sb, w_t, x1, g1, dx2):
    seq = x1.shape[0]
    tm = 512

    def body(dq_ref, dg_ref, dk_ref, dv_ref, c_ref, sa_ref, sb_ref, wt_ref, x_ref, g_ref, dres_ref,
             dx_ref, dxb_ref, dqo_ref, dko_ref, dn_ref):
        @pl.when(pl.program_id(0) == 0)
        def _():
            dn_ref[...] = jnp.zeros_like(dn_ref)

        c, sa, sb = c_ref[...], sa_ref[...], sb_ref[...]
        dq = _rope_transpose(dq_ref[...].astype(F32), c, sa, sb).astype(BF16)
        dk = _rope_transpose(dk_ref[...].astype(F32), c, sa, sb).astype(BF16)
        dqo_ref[...] = dq
        dko_ref[...] = dk
        dh = (_mm(dq, wt_ref[0:D, :]) + _mm(dg_ref[...], wt_ref[D:2 * D, :])
              + _mm(dk, wt_ref[2 * D:2 * D + 256, :]) + _mm(dv_ref[...], wt_ref[2 * D + 256:MIX1_IN, :]))
        xf = x_ref[...]
        r = lax.rsqrt(jnp.mean(xf * xf, axis=1, keepdims=True) + EPS)
        xn = xf * r
        dn_ref[...] += jnp.sum(dh * xn, axis=0, keepdims=True)
        dxn = dh * g_ref[...]
        dx = dres_ref[...] + r * (dxn - xn * jnp.mean(dxn * xn, axis=1, keepdims=True))
        dx_ref[...] = dx
        dxb_ref[...] = dx.astype(BF16)

    row = lambda w_: pl.BlockSpec((tm, w_), lambda i: (i, 0))
    return pl.pallas_call(
        body, grid=(seq // tm,), name="l1_in_proj_bwd",
        out_shape=(jax.ShapeDtypeStruct((seq, D), F32), jax.ShapeDtypeStruct((seq, D), BF16),
                   jax.ShapeDtypeStruct((seq, D), BF16), jax.ShapeDtypeStruct((seq, 256), BF16),
                   jax.ShapeDtypeStruct((1, D), F32)),
        in_specs=[row(D), row(D), row(256), row(256), row(LANES), row(LANES), row(LANES), _resident((MIX1_IN, D)),
                  row(D), _resident((1, D)), row(D)],
        out_specs=(row(D), row(D), row(D), row(256), pl.BlockSpec((1, D), lambda i: (0, 0))),
        compiler_params=_params(48),
    )(dq_r, dgate, dk_r, dv, rope_c, rope_sa, rope_sb, w_t, x1, g1, dx2)


def _l0_mix_bwd(dx1b, wout_t, za, bx, bg, ws, ws_t, bias, gv, wg, wg_t, scale):
    seq = dx1b.shape[0]
    ts = 256
    n_tiles = seq // ts

    def body(dx_ref, wot_ref, za_ref, bx_ref, bxp_ref, bxn_ref, bg_ref, ws_ref, wst_ref, bias_ref, gv_ref, wg_ref,
             wgt_ref, sc_ref,
             dza_ref, dp_ref, dgb_ref, catt_ref, dws_ref, dbias_ref, dgv_ref, dsc_ref, dwg_ref, db_ref, xe_ref):
        i = pl.program_id(0)

        @pl.when(i == 0)
        def _():
            for r_ in (dws_ref, dbias_ref, dgv_ref, dsc_ref, dwg_ref, db_ref):
                r_[...] = jnp.zeros_like(r_)

        dxb = dx_ref[...]
        dya = _mm(dxb, wot_ref[:, 0:D])
        dyb = _mm(dxb, wot_ref[:, D:2 * D])

        u, du = _gelu_and_grad(za_ref[:, 0:D].astype(F32))
        vg, dvg_dz = _gelu_and_grad(za_ref[:, D:2 * D].astype(F32))
        rv = lax.rsqrt(jnp.mean(vg * vg, axis=1, keepdims=True) + EPS)
        vnorm = vg * rv
        gvw = gv_ref[...]
        vnb = (vnorm * gvw).astype(BF16)
        mixed = _spatial_mix(ws_ref, vnb, bias_ref[...], ts)
        sga, dsga = _silu_and_grad(za_ref[:, 2 * D:3 * D].astype(F32))
        um = u * mixed
        ya = (um * sga).astype(BF16)
        t = dya * sga
        dza_ref[:, 0:D] = (t * mixed * du).astype(BF16)
        dza_ref[:, 2 * D:3 * D] = (dya * um * dsga).astype(BF16)
        dmixed = t * u
        dmb = dmixed.astype(BF16)
        dvn_rows = []
        dbias = jnp.zeros((CHUNK, D), F32)
        for c in range(ts // CHUNK):
            rows = slice(c * CHUNK, (c + 1) * CHUNK)
            dbias = dbias + dmixed[rows, :]
            parts = []
            for h in range(A_GROUPS):
                cols = slice(h * GDIM, (h + 1) * GDIM)
                dws_ref[h] += _nt(dmb[rows, cols], vnb[rows, cols])
                parts.append(_mm(wst_ref[h], dmb[rows, cols]))
            dvn_rows.append(jnp.concatenate(parts, axis=1))
        dbias_ref[...] += dbias
        dvn = jnp.concatenate(dvn_rows, axis=0)
        dgv_ref[...] += jnp.sum(dvn * vnorm, axis=0, keepdims=True)
        dxn = dvn * gvw
        dvg = rv * (dxn - vnorm * jnp.mean(dxn * vnorm, axis=1, keepdims=True))
        dza_ref[:, D:2 * D] = (dvg * dvg_dz).astype(BF16)

        _fill_halo(xe_ref, bx_ref[...], bxp_ref, bxn_ref, i, n_tiles, ts)
        pb = _pool_forward(xe_ref, ts, i * ts, seq).astype(BF16)
        ypre = jnp.concatenate([_mm(pb[:, g * GDIM:(g + 1) * GDIM], wg_ref[g]) for g in range(4)], axis=1)
        sc = sc_ref[...]
        y = ypre * sc
        sgb, dsgb = _silu_and_grad(bg_ref[...].astype(F32))
        yb = (y * sgb).astype(BF16)
        dy_b = dyb * sgb
        dgb_ref[...] = (dyb * y * dsgb).astype(BF16)
        dsc_ref[...] += jnp.sum(dy_b * ypre, axis=0, keepdims=True)
        dypre = (dy_b * sc).astype(BF16)
        dps = []
        for g in range(4):
            cols = slice(g * GDIM, (g + 1) * GDIM)
            dwg_ref[g] += _tn(pb[:, cols], dypre[:, cols])
            dps.append(_mm(dypre[:, cols], wgt_ref[g]))
        dp_ref[...] = jnp.concatenate(dps, axis=1)
        catt_ref[...] = jnp.concatenate([ya, yb], axis=1).T

        @pl.when(i == n_tiles - 1)
        def _():
            for h in range(A_GROUPS):
                tot = jnp.sum(dbias_ref[:, h * GDIM:(h + 1) * GDIM].T, axis=0, keepdims=True)
                db_ref[pl.ds(h * 8, 8), :] = jnp.broadcast_to(tot, (8, CHUNK))

    prev, nxt = _halo_specs(ts, seq, D)
    row = lambda w_: pl.BlockSpec((ts, w_), lambda i: (i, 0))
    acc = lambda shape: pl.BlockSpec(shape, lambda i: (0,) * len(shape))
    return pl.pallas_call(
        body, grid=(n_tiles,), name="l0_mix_bwd",
        out_shape=(jax.ShapeDtypeStruct((seq, 3 * D), BF16), jax.ShapeDtypeStruct((seq, D), F32),
                   jax.ShapeDtypeStruct((seq, D), BF16), jax.ShapeDtypeStruct((2 * D, seq), BF16),
                   jax.ShapeDtypeStruct((4, CHUNK, CHUNK), F32), jax.ShapeDtypeStruct((CHUNK, D), F32),
                   jax.ShapeDtypeStruct((1, D), F32), jax.ShapeDtypeStruct((1, D), F32),
                   jax.ShapeDtypeStruct((4, GDIM, GDIM), F32), jax.ShapeDtypeStruct((32, CHUNK), F32)),
        in_specs=[row(D), _resident((D, 2 * D)), row(3 * D), row(D), prev, nxt, row(D), _resident((4, CHUNK, CHUNK)),
                  _resident((4, CHUNK, CHUNK)), _resident((CHUNK, D)), _resident((1, D)), _resident((4, GDIM, GDIM)),
                  _resident((4, GDIM, GDIM)), _resident((1, D))],
        out_specs=(row(3 * D), row(D), row(D), pl.BlockSpec((2 * D, ts), lambda i: (0, i)),
                   acc((4, CHUNK, CHUNK)), acc((CHUNK, D)), acc((1, D)), acc((1, D)), acc((4, GDIM, GDIM)),
                   acc((32, CHUNK))),
        scratch_shapes=[pltpu.VMEM((ts + 2 * POOL_HALO, D), F32)],
        compiler_params=_params(56),
    )(dx1b, wout_t, za, bx, bx, bx, bg, ws, ws_t, bias, gv, wg, wg_t, scale)


def _l0_pool_bwd(dp):
    seq = dp.shape[0]
    ts = 512
    n_tiles = seq // ts
    ext = ts + 2 * POOL_HALO

    def body(dp_ref, dpp_ref, dpn_ref, out_ref, qe_ref):
        i = pl.program_id(0)
        _fill_halo(qe_ref, dp_ref[...], dpp_ref, dpn_ref, i, n_tiles, ts)
        te = i * ts - POOL_HALO + lax.broadcasted_iota(jnp.int32, (ext, 1), 0)
        for gi, w in enumerate(POOL_WINDOWS):
            hw = w // 2
            cols = slice(gi * GDIM, (gi + 1) * GDIM)
            cnt = jnp.maximum(jnp.minimum(te + hw, seq) - jnp.maximum(te - hw, 0), 1).astype(F32)
            qe_ref[:, cols] = qe_ref[:, cols] / cnt
        outs = []
        for gi, w in enumerate(POOL_WINDOWS):
            hw = w // 2
            cols = slice(gi * GDIM, (gi + 1) * GDIM)
            acc = qe_ref[pl.ds(POOL_HALO - hw + 1, ts), cols]
            for k in range(-hw + 2, hw + 1):
                acc = acc + qe_ref[pl.ds(POOL_HALO + k, ts), cols]
            outs.append(acc - dp_ref[:, cols])
        out_ref[...] = jnp.concatenate(outs, axis=1).astype(BF16)

    prev, nxt = _halo_specs(ts, seq, D)
    row = pl.BlockSpec((ts, D), lambda i: (i, 0))
    return pl.pallas_call(
        body, grid=(n_tiles,), name="l0_pool_bwd",
        out_shape=jax.ShapeDtypeStruct((seq, D), BF16),
        in_specs=[row, prev, nxt], out_specs=row,
        scratch_shapes=[pltpu.VMEM((ext, D), F32)],
        compiler_params=_params(32),
    )(dp, dp, dp)


def _l0_in_proj_bwd(dza, dbx, dgb, w_t, x, g0, dx1):
    seq = x.shape[0]
    tm = 512

    def body(dza_ref, dbx_ref, dgb_ref, wt_ref, x_ref, g_ref, dres_ref, dx_ref, dn_ref):
        @pl.when(pl.program_id(0) == 0)
        def _():
            dn_ref[...] = jnp.zeros_like(dn_ref)

        dh = (_mm(dza_ref[...], wt_ref[0:3 * D, :]) + _mm(dbx_ref[...], wt_ref[3 * D:4 * D, :])
              + _mm(dgb_ref[...], wt_ref[4 * D:5 * D, :]))
        xf = x_ref[...]
        r = lax.rsqrt(jnp.mean(xf * xf, axis=1, keepdims=True) + EPS)
        xn = xf * r
        dn_ref[...] += jnp.sum(dh * xn, axis=0, keepdims=True)
        dxn = dh * g_ref[...]
        dx_ref[...] = dres_ref[...] + r * (dxn - xn * jnp.mean(dxn * xn, axis=1, keepdims=True))

    row = lambda w_: pl.BlockSpec((tm, w_), lambda i: (i, 0))
    return pl.pallas_call(
        body, grid=(seq // tm,), name="l0_in_proj_bwd",
        out_shape=(jax.ShapeDtypeStruct((seq, D), F32), jax.ShapeDtypeStruct((1, D), F32)),
        in_specs=[row(3 * D), row(D), row(D), _resident((MIX0_IN, D)), row(D), _resident((1, D)), row(D)],
        out_specs=(row(D), pl.BlockSpec((1, D), lambda i: (0, 0))),
        compiler_params=_params(56),
    )(dza, dbx, dgb, w_t, x, g0, dx1)


def _dw_matmul(a_t, b, name):
    k, seq = a_t.shape
    n = b.shape[1]
    tn = min(n, 1024)
    ts = 512

    def body(a_ref, b_ref, o_ref):
        @pl.when(pl.program_id(1) == 0)
        def _():
            o_ref[...] = jnp.zeros_like(o_ref)

        o_ref[...] += _mm(a_ref[...], b_ref[...])

    return pl.pallas_call(
        body, grid=(n // tn, seq // ts), name=name,
        out_shape=jax.ShapeDtypeStruct((k, n), F32),
        in_specs=[pl.BlockSpec((k, ts), lambda j, s: (0, s)), pl.BlockSpec((ts, tn), lambda j, s: (s, j))],
        out_specs=pl.BlockSpec((k, tn), lambda j, s: (0, j)),
        compiler_params=_params(48, 2),
    )(a_t, b)


def _cast_bf16(w_pack):
    rows = w_pack.shape[0]
    tr = rows // 8

    def body(w_ref, o_ref):
        o_ref[...] = w_ref[...].astype(BF16)

    spec = pl.BlockSpec((tr, LANES), lambda i: (i, 0))
    return pl.pallas_call(body, grid=(8,), name="cast_weights", out_shape=jax.ShapeDtypeStruct((rows, LANES), BF16),
                          in_specs=[spec], out_specs=spec, compiler_params=_params(32))(w_pack)


def _adamw_math(w, g, m, v):
    m2 = ADAM_B1 * m + (1.0 - ADAM_B1) * g
    v2 = ADAM_B2 * v + (1.0 - ADAM_B2) * (g * g)
    m_hat = m2 / (1.0 - ADAM_B1 ** ADAM_STEP)
    v_hat = v2 / (1.0 - ADAM_B2 ** ADAM_STEP)
    delta = -ADAM_LR * (m_hat / (jnp.sqrt(v_hat) + ADAM_EPS) + ADAM_WD * w)
    return delta, m2, v2


def _pair_sum(g_all, r1, c_idx):
    _, nchip, rows, _ = g_all.shape
    tr = rows // 8

    def body(c_ref, g_ref, r_ref, o_ref):
        o_ref[...] = g_ref[...] + r_ref[...]

    return pl.pallas_call(
        body, name="grad_pair_sum", out_shape=jax.ShapeDtypeStruct((nchip, rows, LANES), F32),
        grid_spec=pltpu.PrefetchScalarGridSpec(
            num_scalar_prefetch=1, grid=(nchip, 8),
            in_specs=[pl.BlockSpec((None, None, tr, LANES), lambda q, i, c: (c[0], q, i, 0)),
                      pl.BlockSpec((None, tr, LANES), lambda q, i, c: (q, i, 0))],
            out_specs=pl.BlockSpec((None, tr, LANES), lambda q, i, c: (q, i, 0))),
        compiler_params=_params(32, 2),
    )(c_idx, g_all, r1)


def _final_sum_adamw(p, r2, chip_idx, w, m, v):
    rows = w.shape[0]
    tr = rows // 8

    def body(q_ref, p_ref, r_ref, w_ref, m_ref, v_ref, g_out, d_out, m_out, v_out):
        g = p_ref[...] + r_ref[0] + r_ref[1] + r_ref[2]
        delta, m2, v2 = _adamw_math(w_ref[...], g, m_ref[...], v_ref[...])
        g_out[...] = g
        d_out[...] = delta
        m_out[...] = m2
        v_out[...] = v2

    flat = pl.BlockSpec((tr, LANES), lambda i, q: (i, 0))
    shp = jax.ShapeDtypeStruct((rows, LANES), F32)
    return pl.pallas_call(
        body, name="grad_sum_adamw", out_shape=(shp, shp, shp, shp),
        grid_spec=pltpu.PrefetchScalarGridSpec(
            num_scalar_prefetch=1, grid=(8,),
            in_specs=[pl.BlockSpec((None, tr, LANES), lambda i, q: (q[0], i, 0)),
                      pl.BlockSpec((3, tr, LANES), lambda i, q: (0, i, 0)), flat, flat, flat],
            out_specs=(flat, flat, flat, flat)),
        compiler_params=_params(32),
    )(chip_idx, p, r2, w, m, v)


def _small_pair_sum(gs, r1s):
    def body(g_ref, r_ref, o_ref):
        o_ref[...] = g_ref[...] + r_ref[...]

    return pl.pallas_call(body, name="small_pair_sum", out_shape=jax.ShapeDtypeStruct(gs.shape, F32))(gs, r1s)


def _small_sum_adamw(ps, r2s, chip_idx, w, m, v):
    def body(q_ref, p_ref, r_ref, w_ref, m_ref, v_ref, g_out, d_out, m_out, v_out, stack):
        stack[0] = p_ref[...]
        stack[2] = r_ref[0]
        stack[1] = r_ref[1]
        stack[3] = r_ref[2]
        me = q_ref[0]
        g = stack[me] + stack[me ^ 1]
        g = g + stack[me ^ 2]
        g = g + stack[me ^ 3]
        delta, m2, v2 = _adamw_math(w_ref[...], g, m_ref[...], v_ref[...])
        g_out[...] = g
        d_out[...] = delta
        m_out[...] = m2
        v_out[...] = v2

    shp = jax.ShapeDtypeStruct((SMALL_ROWS, LANES), F32)
    vm = pl.BlockSpec(memory_space=pltpu.VMEM)
    return pl.pallas_call(
        body, name="small_sum_adamw", out_shape=(shp, shp, shp, shp),
        in_specs=[pl.BlockSpec(memory_space=pltpu.SMEM), vm, vm, vm, vm, vm], out_specs=(vm, vm, vm, vm),
        scratch_shapes=[pltpu.VMEM((4, SMALL_ROWS, LANES), F32)],
    )(chip_idx, ps, r2s, w, m, v)


def _all_gather(blk):
    rows = blk.shape[0]

    def body(x_ref, out_ref, send_sems, recv_sems, local_sem):
        x, y, c = lax.axis_index("x"), lax.axis_index("y"), lax.axis_index("c")
        me, sibling = (x, y, c), (x, y, 1 - c)
        chips = [(1 - x, y), (x, 1 - y), (1 - x, 1 - y)]

        def slot(px, py, pc):
            return out_ref.at[4 * px + 2 * py + pc]

        def copy(k, block, to, src=None):
            return pltpu.make_async_remote_copy(
                src_ref=slot(*block) if src is None else src, dst_ref=slot(*block),
                send_sem=send_sems.at[k], recv_sem=recv_sems.at[k], device_id=to, device_id_type=MESH)

        mine = pltpu.make_async_copy(x_ref, slot(*me), local_sem)
        mine.start()
        first = [copy(0, me, sibling, src=x_ref)]
        first += [copy(1 + j, me, (*chip, c), src=x_ref) for j, chip in enumerate(chips)]
        for cp in first:
            cp.start()
        passed = [copy(4 + j, (*chip, c), sibling) for j, chip in enumerate(chips)]
        for j, chip in enumerate(chips):
            copy(1 + j, (*chip, c), me).wait_recv()
            passed[j].start()
        copy(0, sibling, me).wait_recv()
        for j, chip in enumerate(chips):
            copy(4 + j, (*chip, 1 - c), me).wait_recv()
        for cp in first + passed:
            cp.wait_send()
        mine.wait()

    any_spec = pl.BlockSpec(memory_space=pl.ANY)
    return pl.pallas_call(
        body, name="weights_all_gather", out_shape=jax.ShapeDtypeStruct((N_DEV, rows, LANES), blk.dtype),
        in_specs=[any_spec], out_specs=any_spec,
        scratch_shapes=[pltpu.SemaphoreType.DMA((7,)), pltpu.SemaphoreType.DMA((7,)), pltpu.SemaphoreType.DMA],
    )(blk)


def _sibling_exchange(g_all, gs):
    _, nchip, rows, _ = g_all.shape

    def body(g_ref, s_ref, r1_ref, r1s_ref, send_sems, recv_sems):
        x, y, c = lax.axis_index("x"), lax.axis_index("y"), lax.axis_index("c")
        sibling = (x, y, 1 - c)
        big = pltpu.make_async_remote_copy(src_ref=g_ref.at[1 - c], dst_ref=r1_ref, send_sem=send_sems.at[0],
                                           recv_sem=recv_sems.at[0], device_id=sibling, device_id_type=MESH)
        small = pltpu.make_async_remote_copy(src_ref=s_ref, dst_ref=r1s_ref, send_sem=send_sems.at[1],
                                             recv_sem=recv_sems.at[1], device_id=sibling, device_id_type=MESH)
        big.start()
        small.start()
        big.wait()
        small.wait()

    any_spec = pl.BlockSpec(memory_space=pl.ANY)
    return pl.pallas_call(
        body, name="grad_sibling_exchange",
        out_shape=(jax.ShapeDtypeStruct((nchip, rows, LANES), F32), jax.ShapeDtypeStruct(gs.shape, F32)),
        in_specs=[any_spec, any_spec], out_specs=(any_spec, any_spec),
        scratch_shapes=[pltpu.SemaphoreType.DMA((2,)), pltpu.SemaphoreType.DMA((2,))],
    )(g_all, gs)


def _chip_exchange(p, ps):
    nchip, rows, _ = p.shape

    def body(p_ref, s_ref, r2_ref, r2s_ref, send_sems, recv_sems):
        x, y, c = lax.axis_index("x"), lax.axis_index("y"), lax.axis_index("c")
        copies = []
        for j, (fx, fy) in enumerate(((1, 0), (0, 1), (1, 1))):
            tx = x ^ fx
            ty = y ^ fy
            to = (tx, ty, c)
            copies.append(pltpu.make_async_remote_copy(
                src_ref=p_ref.at[2 * tx + ty], dst_ref=r2_ref.at[j], send_sem=send_sems.at[j, 0],
                recv_sem=recv_sems.at[j, 0], device_id=to, device_id_type=MESH))
            copies.append(pltpu.make_async_remote_copy(
                src_ref=s_ref, dst_ref=r2s_ref.at[j], send_sem=send_sems.at[j, 1],
                recv_sem=recv_sems.at[j, 1], device_id=to, device_id_type=MESH))
        for cp in copies:
            cp.start()
        for cp in copies:
            cp.wait()

    any_spec = pl.BlockSpec(memory_space=pl.ANY)
    return pl.pallas_call(
        body, name="grad_chip_exchange",
        out_shape=(jax.ShapeDtypeStruct((3, rows, LANES), F32), jax.ShapeDtypeStruct((3,) + ps.shape, F32)),
        in_specs=[any_spec, any_spec], out_specs=(any_spec, any_spec),
        scratch_shapes=[pltpu.SemaphoreType.DMA((3, 2)), pltpu.SemaphoreType.DMA((3, 2))],
    )(p, ps)


def _pack_shards(w_in_0, b_group_w_0, w_out_0, w_in_1, w_out_1):
    return jnp.concatenate([t.reshape(-1, LANES) for t in (w_in_0, b_group_w_0, w_out_0, w_in_1, w_out_1)], axis=0)


def _unpack_shards(buf):
    shapes = ((1024, 640), (4, 32, 256), (256, 1024), (1024, 320), (128, 1024))
    outs, r0 = [], 0
    for shape, rows in zip(shapes, PACK_ROWS):
        outs.append(buf[r0:r0 + rows].reshape(shape))
        r0 += rows
    return outs


def _pack_small(norm_0, a_v_norm_0, a_spatial_w_0, a_spatial_b_0, b_scale_0, norm_1, sink_1, final_norm):
    sink = jnp.concatenate([sink_1.reshape(-1), jnp.zeros((LANES - N_HEADS,), F32)])
    parts = [norm_0, a_v_norm_0, a_spatial_w_0, a_spatial_b_0, b_scale_0, norm_1, sink, final_norm]
    flat = jnp.concatenate([t.reshape(-1) for t in parts])
    flat = jnp.concatenate([flat, jnp.zeros((SMALL_ROWS * LANES - flat.shape[0],), F32)])
    return flat.reshape(SMALL_ROWS, LANES)


def _unpack_small(buf):
    flat = buf.reshape(-1)
    shapes = ((1024,), (1024,), (4, 128, 128), (4, 128), (1024,), (1024,), (128,), (1024,))
    outs, o = [], 0
    for shape, size in zip(shapes, SMALL_SIZES):
        outs.append(flat[o:o + size].reshape(shape))
        o += size
    outs[6] = outs[6][:N_HEADS]
    return outs


def _blocks(t, axis):
    shape = t.shape
    t = t.reshape(shape[:axis] + (N_DEV, shape[axis] // N_DEV) + shape[axis + 1:])
    t = jnp.moveaxis(t, axis, 0)
    return t.reshape(N_DEV, -1, LANES)


def kernel(x, norm_0, w_in_0, a_v_norm_0, a_spatial_w_0, a_spatial_b_0, b_group_w_0, b_scale_0, w_out_0, norm_1, w_in_1, sink_1, w_out_1, final_norm, loss_target, m_norm_0, m_w_in_0, m_a_v_norm_0, m_a_spatial_w_0, m_a_spatial_b_0, m_b_group_w_0, m_b_scale_0, m_w_out_0, m_norm_1, m_w_in_1, m_sink_1, m_w_out_1, m_final_norm, v_norm_0, v_w_in_0, v_a_v_norm_0, v_a_spatial_w_0, v_a_spatial_b_0, v_b_group_w_0, v_b_scale_0, v_w_out_0, v_norm_1, v_w_in_1, v_sink_1, v_w_out_1, v_final_norm):
    seq = x.shape[1]
    xs = x.reshape(seq, D)
    tgt = loss_target.reshape(seq, D)
    ax, ay, ac = lax.axis_index("x"), lax.axis_index("y"), lax.axis_index("c")
    c_idx = jnp.reshape(ac, (1,)).astype(jnp.int32)
    chip_idx = jnp.reshape(2 * ax + ay, (1,)).astype(jnp.int32)

    w_pack = _pack_shards(w_in_0, b_group_w_0, w_out_0, w_in_1, w_out_1)
    gathered = _all_gather(_cast_bf16(w_pack))
    r0 = 0
    parts = []
    for rows in PACK_ROWS:
        parts.append(gathered[:, r0:r0 + rows])
        r0 += rows
    win0 = parts[0].reshape(N_DEV, D, 640).transpose(1, 0, 2).reshape(D, MIX0_IN)
    wg = parts[1].reshape(N_DEV, 4, 32, GDIM).transpose(1, 0, 2, 3).reshape(4, GDIM, GDIM)
    wout0 = parts[2].reshape(2 * D, D)
    win1 = parts[3].reshape(N_DEV, D, 320).transpose(1, 0, 2).reshape(D, MIX1_IN)
    wout1 = parts[4].reshape(D, D)
    loss_part, grad_x, d_win0, d_wg, d_wout0, d_win1, d_wout1, gs = _local_step(
        xs, tgt, win0, wg, wout0, win1, wout1, norm_0, a_v_norm_0, a_spatial_w_0, a_spatial_b_0, b_scale_0, norm_1, sink_1,
        final_norm)

    g_blocks = jnp.concatenate([_blocks(d_win0, 1), _blocks(d_wg, 1), _blocks(d_wout0, 0), _blocks(d_win1, 1),
                                _blocks(d_wout1, 0)], axis=1)
    g_all = g_blocks.reshape(4, 2, PACK_TOTAL, LANES).transpose(1, 0, 2, 3)
    r1, r1s = _sibling_exchange(g_all, gs)
    p = _pair_sum(g_all, r1, c_idx)
    ps = _small_pair_sum(gs, r1s)
    r2, r2s = _chip_exchange(p, ps)

    m_pack = _pack_shards(m_w_in_0, m_b_group_w_0, m_w_out_0, m_w_in_1, m_w_out_1)
    v_pack = _pack_shards(v_w_in_0, v_b_group_w_0, v_w_out_0, v_w_in_1, v_w_out_1)
    big = [_unpack_shards(t) for t in _final_sum_adamw(p, r2, chip_idx, w_pack, m_pack, v_pack)]
    ws_pack = _pack_small(norm_0, a_v_norm_0, a_spatial_w_0, a_spatial_b_0, b_scale_0, norm_1, sink_1, final_norm)
    ms_pack = _pack_small(m_norm_0, m_a_v_norm_0, m_a_spatial_w_0, m_a_spatial_b_0, m_b_scale_0, m_norm_1, m_sink_1,
                          m_final_norm)
    vs_pack = _pack_small(v_norm_0, v_a_v_norm_0, v_a_spatial_w_0, v_a_spatial_b_0, v_b_scale_0, v_norm_1, v_sink_1,
                          v_final_norm)
    small = [_unpack_small(t) for t in _small_sum_adamw(ps, r2s, chip_idx, ws_pack, ms_pack, vs_pack)]

    def in_order(kind):
        b, s = big[kind], small[kind]
        return [s[0], b[0], s[1], s[2], s[3], b[1], s[4], b[2], s[5], b[3], s[6], b[4], s[7]]

    loss = lax.psum(loss_part[0, 0], ("x", "y", "c"))
    return (loss, grad_x.reshape(1, seq, D), *in_order(0), *in_order(1), *in_order(2), *in_order(3))


def _local_step(xs, tgt, win0, wg, wout0, win1, wout1, norm_0, a_v_norm_0, a_spatial_w_0, a_spatial_b_0, b_scale_0, norm_1,
                sink_1, final_norm):
    seq = xs.shape[0]
    win1 = jnp.concatenate([win1[:, 0:D], win1[:, D + 512:MIX1_IN], win1[:, D:D + 512]], axis=1)
    win0_t, wout0_t, win1_t, wout1_t = win0.T, wout0.T, win1.T, wout1.T
    wg_t = jnp.swapaxes(wg, 1, 2)
    ws = a_spatial_w_0.astype(BF16)
    ws_t = jnp.swapaxes(ws, 1, 2)
    bias = jnp.repeat(a_spatial_b_0.T, GDIM, axis=1)
    g0, gv, scale, g1, gf = (t.reshape(1, D) for t in (norm_0, a_v_norm_0, b_scale_0, norm_1, final_norm))
    rope_c, rope_sa, rope_sb = _rope_tables(seq)

    za, bx, bg, h0_t = _l0_in_proj(xs, g0, win0)
    x1 = _l0_mix_fwd(za, bx, bg, xs, ws, bias, gv, wg, scale, wout0)
    z1, h1_t = _l1_in_proj(x1, g1, win1, rope_c, rope_sa, rope_sb)
    dx2, dx2b, y_t, att, lse, loss_part, d_gf = _l1_attn_fwd(z1, x1, tgt, wout1, gf, sink_1)

    d_wout1 = _dw_matmul(y_t, dx2b, "dw_out_1")
    dq_r, dgate, dk_pad, dv_pad, d_sink = _l1_attn_bwd(dx2b, wout1_t, z1, att, lse, sink_1)
    dk_r = dk_pad[BLK:BLK + seq]
    dv = dv_pad[BLK:BLK + seq]
    dx1, dx1b, dq, dk, d_g1 = _l1_in_proj_bwd(dq_r, dgate, dk_r, dv, rope_c, rope_sa, rope_sb, win1_t, x1, g1, dx2)
    d_win1 = jnp.concatenate([_dw_matmul(h1_t, dq, "dw_in_1_q"), _dw_matmul(h1_t, dk, "dw_in_1_k"),
                              _dw_matmul(h1_t, dv, "dw_in_1_v"), _dw_matmul(h1_t, dgate, "dw_in_1_gate")], axis=1)

    dza, dp, dgb, cat_t, d_ws, _, d_gv, d_scale, d_wg, d_b = _l0_mix_bwd(
        dx1b, wout0_t, za, bx, bg, ws, ws_t, bias, gv, wg, wg_t, scale)
    d_wout0 = _dw_matmul(cat_t, dx1b, "dw_out_0")
    dbx = _l0_pool_bwd(dp)
    grad_x, d_g0 = _l0_in_proj_bwd(dza, dbx, dgb, win0_t, xs, g0, dx1)
    d_win0 = jnp.concatenate([_dw_matmul(h0_t, dza, "dw_in_0_a"), _dw_matmul(h0_t, dbx, "dw_in_0_bx"),
                              _dw_matmul(h0_t, dgb, "dw_in_0_bg")], axis=1)

    d_bs = d_b.reshape(4, 8, CHUNK)[:, 0, :]
    gs = _pack_small(d_g0[0], d_gv[0], d_ws, d_bs, d_scale[0], d_g1[0], d_sink[0, :N_HEADS], d_gf[0])
    return loss_part, grad_x, d_win0, d_wg, d_wout0, d_win1, d_wout1, gs
```

```python
import jax
import jax.numpy as jnp
from jax import lax
from jax.experimental import pallas as pl
from jax.experimental.pallas import tpu as pltpu

F32 = jnp.float32
BF16 = jnp.bfloat16

D = 1024
EPS = 1e-6
NEG_INF = -1e30
CHUNK = 128
A_GROUPS = 4
POOL_WINDOWS = (2, 4, 8, 16)
POOL_HALO = 8
GDIM = 256
N_HEADS = 16
N_KV = 4
GQA = 4
HD = 64
BLK = 128
ROT_HALF = 8
ROPE_THETA = 500000.0
SCALE = HD ** -0.5
MIX0_IN = 5 * D
MIX1_IN = 2560
KV_W = N_KV * HD
Q_ROWS, K_ROWS, V_ROWS, G_ROWS = (0, D), (D, D + KV_W), (D + KV_W, D + 2 * KV_W), (D + 2 * KV_W, MIX1_IN)
TQ = 512

ADAM_LR = 0.001
ADAM_B1 = 0.9
ADAM_B2 = 0.999
ADAM_EPS = 1e-08
ADAM_WD = 0.01
ADAM_STEP = 10

N_DEV = 8
LANES = 128
MIB = 2 ** 20
MESH = pl.DeviceIdType.MESH

PACK_ROWS = (1024 * 640 // LANES, 4 * 32 * 256 // LANES, 256 * 1024 // LANES, 1024 * 320 // LANES, 128 * 1024 // LANES)
PACK_TOTAL = sum(PACK_ROWS)
SMALL_SIZES = (1024, 1024, 4 * 128 * 128, 4 * 128, 1024, 1024, 128, 1024)
SMALL_ROWS = 560


def _params(limit_mib, n_axes=1):
    return pltpu.CompilerParams(vmem_limit_bytes=limit_mib * MIB, dimension_semantics=("arbitrary",) * n_axes)


def _resident(shape):
    nd = len(shape)
    return pl.BlockSpec(shape, lambda *_: (0,) * nd, pipeline_mode=pl.Buffered(1))


def _gelu(x):
    k = 0.7978845608028654
    return 0.5 * x * (1.0 + jnp.tanh(k * (x + 0.044715 * x * x * x)))


def _gelu_and_grad(x):
    k = 0.7978845608028654
    x2 = x * x
    t = jnp.tanh(k * (x + 0.044715 * x * x2))
    g = 0.5 * x * (1.0 + t)
    dg = 0.5 * (1.0 + t) + 0.5 * x * (1.0 - t * t) * (k * (1.0 + 3.0 * 0.044715 * x2))
    return g, dg


def _silu_and_grad(x):
    s = jax.nn.sigmoid(x)
    return x * s, s * (1.0 + x * (1.0 - s))


def _nt(a, b):
    return lax.dot_general(a, b, (((1,), (1,)), ((), ())), preferred_element_type=F32)


def _tn(a, b):
    return lax.dot_general(a, b, (((0,), (0,)), ((), ())), preferred_element_type=F32)


def _mm(a, b):
    return jnp.dot(a, b, preferred_element_type=F32)


def _rope_tables_t(seq):
    inv = ROPE_THETA ** (-jnp.arange(0, 2 * ROT_HALF, 2, dtype=F32) / (2 * ROT_HALF))
    ang = inv[:, None] * jnp.arange(seq, dtype=F32)[None, :]
    return jnp.cos(ang), jnp.sin(ang)


def _rope_t(z, c, s, n_heads, sign):
    parts = []
    for h in range(n_heads):
        b = h * HD
        x1, x2 = z[b:b + ROT_HALF], z[b + ROT_HALF:b + 2 * ROT_HALF]
        if sign > 0:
            parts += [x1 * c - x2 * s, x2 * c + x1 * s]
        else:
            parts += [x1 * c + x2 * s, x2 * c - x1 * s]
        parts.append(z[b + 2 * ROT_HALF:b + HD])
    return jnp.concatenate(parts, axis=0)


def _l0_in_proj(x, g0, w):
    seq = x.shape[0]
    tm = 512

    def body(x_ref, g_ref, w_ref, za_ref, bx_ref, bg_ref, ht_ref):
        xf = x_ref[...]
        r = lax.rsqrt(jnp.mean(xf * xf, axis=1, keepdims=True) + EPS)
        h = (xf * r * g_ref[...]).astype(BF16)
        ht_ref[...] = h.T
        for j in range(3):
            za_ref[:, j * D:(j + 1) * D] = _mm(h, w_ref[:, j * D:(j + 1) * D]).astype(BF16)
        bx_ref[...] = _mm(h, w_ref[:, 3 * D:4 * D])
        bg_ref[...] = _mm(h, w_ref[:, 4 * D:5 * D]).astype(BF16)

    return pl.pallas_call(
        body, grid=(seq // tm,), name="l0_in_proj",
        out_shape=(jax.ShapeDtypeStruct((seq, 3 * D), BF16), jax.ShapeDtypeStruct((seq, D), F32),
                   jax.ShapeDtypeStruct((seq, D), BF16), jax.ShapeDtypeStruct((D, seq), BF16)),
        in_specs=[pl.BlockSpec((tm, D), lambda i: (i, 0)), _resident((1, D)), _resident((D, MIX0_IN))],
        out_specs=(pl.BlockSpec((tm, 3 * D), lambda i: (i, 0)), pl.BlockSpec((tm, D), lambda i: (i, 0)),
                   pl.BlockSpec((tm, D), lambda i: (i, 0)), pl.BlockSpec((D, tm), lambda i: (0, i))),
        compiler_params=_params(48),
    )(x, g0, w)


def _fill_halo(ext_ref, cur, prev_ref, next_ref, i, n_tiles, ts):
    ext_ref[pl.ds(0, POOL_HALO), :] = jnp.where(i > 0, prev_ref[...], 0.0)
    ext_ref[pl.ds(POOL_HALO, ts), :] = cur
    ext_ref[pl.ds(POOL_HALO + ts, POOL_HALO), :] = jnp.where(i < n_tiles - 1, next_ref[...], 0.0)


def _pool_forward(xe_ref, ts, t0, seq):
    tg = t0 + lax.broadcasted_iota(jnp.int32, (ts, 1), 0)
    outs = []
    for gi, w in enumerate(POOL_WINDOWS):
        hw = w // 2
        cols = slice(gi * GDIM, (gi + 1) * GDIM)
        acc = xe_ref[pl.ds(POOL_HALO - hw, ts), cols]
        for k in range(-hw + 1, hw):
            acc = acc + xe_ref[pl.ds(POOL_HALO + k, ts), cols]
        cnt = (jnp.minimum(tg + hw, seq) - jnp.maximum(tg - hw, 0)).astype(F32)
        outs.append(acc / cnt - xe_ref[pl.ds(POOL_HALO, ts), cols])
    return jnp.concatenate(outs, axis=1)


def _spatial_mix(ws_ref, vnb, bias, ts):
    rows = []
    for c in range(ts // CHUNK):
        vc = vnb[c * CHUNK:(c + 1) * CHUNK, :]
        rows.append(jnp.concatenate(
            [_mm(ws_ref[h], vc[:, h * GDIM:(h + 1) * GDIM]) for h in range(A_GROUPS)], axis=1) + bias)
    return jnp.concatenate(rows, axis=0)


def _halo_specs(ts, seq, width):
    per = ts // POOL_HALO
    last = seq // POOL_HALO - 1
    prev = pl.BlockSpec((POOL_HALO, width), lambda i: (jnp.maximum(i * per - 1, 0), 0))
    nxt = pl.BlockSpec((POOL_HALO, width), lambda i: (jnp.minimum((i + 1) * per, last), 0))
    return prev, nxt


def _l0_mix_fwd(za, bx, bg, x, ws, bias, gv, wg, scale, wout):
    seq = x.shape[0]
    ts = 512
    n_tiles = seq // ts

    def body(za_ref, bx_ref, bxp_ref, bxn_ref, bg_ref, x_ref, ws_ref, bias_ref, gv_ref, wg_ref, sc_ref, wo_ref,
             x1_ref, xe_ref):
        i = pl.program_id(0)
        u = _gelu(za_ref[:, 0:D].astype(F32))
        vg = _gelu(za_ref[:, D:2 * D].astype(F32))
        rv = lax.rsqrt(jnp.mean(vg * vg, axis=1, keepdims=True) + EPS)
        vnb = (vg * rv * gv_ref[...]).astype(BF16)
        mixed = _spatial_mix(ws_ref, vnb, bias_ref[...], ts)
        ag = za_ref[:, 2 * D:3 * D].astype(F32)
        ya = (u * mixed * (ag * jax.nn.sigmoid(ag))).astype(BF16)

        _fill_halo(xe_ref, bx_ref[...], bxp_ref, bxn_ref, i, n_tiles, ts)
        pb = _pool_forward(xe_ref, ts, i * ts, seq).astype(BF16)
        y = jnp.concatenate([_mm(pb[:, g * GDIM:(g + 1) * GDIM], wg_ref[g]) for g in range(4)], axis=1) * sc_ref[...]
        bgf = bg_ref[...].astype(F32)
        yb = (y * (bgf * jax.nn.sigmoid(bgf))).astype(BF16)
        x1_ref[...] = x_ref[...] + _mm(ya, wo_ref[0:D, :]) + _mm(yb, wo_ref[D:2 * D, :])

    prev, nxt = _halo_specs(ts, seq, D)
    row = lambda w: pl.BlockSpec((ts, w), lambda i: (i, 0))
    return pl.pallas_call(
        body, grid=(n_tiles,), name="l0_mix_fwd",
        out_shape=jax.ShapeDtypeStruct((seq, D), F32),
        in_specs=[row(3 * D), row(D), prev, nxt, row(D), row(D), _resident((4, CHUNK, CHUNK)), _resident((CHUNK, D)),
                  _resident((1, D)), _resident((4, GDIM, GDIM)), _resident((1, D)), _resident((2 * D, D))],
        out_specs=row(D),
        scratch_shapes=[pltpu.VMEM((ts + 2 * POOL_HALO, D), F32)],
        compiler_params=_params(56),
    )(za, bx, bx, bx, bg, x, ws, bias, gv, wg, scale, wout)


def _l1_in_proj(x1, g1, w_t, cos_t, sin_t):
    seq = x1.shape[0]
    tm = 512

    def body(x_ref, g_ref, wt_ref, c_ref, s_ref, q_ref, k_ref, v_ref, gate_ref, ht_ref):
        xf = x_ref[...]
        r = lax.rsqrt(jnp.mean(xf * xf, axis=1, keepdims=True) + EPS)
        ht = (xf * r * g_ref[...]).astype(BF16).T
        ht_ref[...] = ht
        c, s = c_ref[...], s_ref[...]
        q_ref[...] = _rope_t(_mm(wt_ref[Q_ROWS[0]:Q_ROWS[1], :], ht), c, s, N_HEADS, 1).astype(BF16)
        k_ref[...] = _rope_t(_mm(wt_ref[K_ROWS[0]:K_ROWS[1], :], ht), c, s, N_KV, 1).astype(BF16)
        v_ref[...] = _mm(wt_ref[V_ROWS[0]:V_ROWS[1], :], ht).astype(BF16)
        gate_ref[...] = _mm(wt_ref[G_ROWS[0]:G_ROWS[1], :], ht).astype(BF16)

    col = lambda rows: pl.BlockSpec((rows, tm), lambda i: (0, i))
    return pl.pallas_call(
        body, grid=(seq // tm,), name="l1_in_proj",
        out_shape=(jax.ShapeDtypeStruct((D, seq), BF16), jax.ShapeDtypeStruct((KV_W, seq), BF16),
                   jax.ShapeDtypeStruct((KV_W, seq), BF16), jax.ShapeDtypeStruct((D, seq), BF16),
                   jax.ShapeDtypeStruct((D, seq), BF16)),
        in_specs=[pl.BlockSpec((tm, D), lambda i: (i, 0)), _resident((1, D)), _resident((MIX1_IN, D)), col(ROT_HALF),
                  col(ROT_HALF)],
        out_specs=(col(D), col(KV_W), col(KV_W), col(D), col(D)),
        compiler_params=_params(48),
    )(x1, g1, w_t, cos_t, sin_t)


def _band_specs_t(nb, clamp_i):
    per = TQ // BLK
    prev = pl.BlockSpec((KV_W, BLK), lambda i: (0, jnp.maximum(clamp_i(i) * per - 1, 0)))
    cur = pl.BlockSpec((KV_W, TQ), lambda i: (0, clamp_i(i)))
    nxt = pl.BlockSpec((KV_W, BLK), lambda i: (0, jnp.minimum((clamp_i(i) + 1) * per, nb - 1)))
    return [prev, cur, nxt]


def _fill_band(buf, p_ref, c_ref, n_ref):
    buf[:, 0:BLK] = p_ref[...]
    buf[:, BLK:BLK + TQ] = c_ref[...]
    buf[:, BLK + TQ:2 * BLK + TQ] = n_ref[...]


def _band_bias_t(n, nb):
    c = lax.broadcasted_iota(jnp.int32, (3 * BLK, BLK), 0)
    r = lax.broadcasted_iota(jnp.int32, (3 * BLK, BLK), 1)
    ok = (c >= r) & (c <= r + 2 * BLK) & ((c >= BLK) | (n > 0)) & ((c < 2 * BLK) | (n < nb - 1))
    bias = jnp.where(ok, 0.0, NEG_INF).astype(F32)
    return jnp.concatenate([bias] * GQA, axis=1)


def _heads_t(ref, kv, c0):
    return jnp.concatenate([ref[(kv * GQA + g) * HD:(kv * GQA + g + 1) * HD, c0:c0 + BLK] for g in range(GQA)], axis=1)


def _row4(ref, kv, c0):
    return jnp.concatenate([ref[kv * GQA + g:kv * GQA + g + 1, c0:c0 + BLK] for g in range(GQA)], axis=1)


def _sink_row(sink_ref, kv):
    return jnp.concatenate([jnp.full((1, BLK), sink_ref[kv * GQA + g], F32) for g in range(GQA)], axis=1)


def _l1_attn_fwd(qt, kt, vt, gatet, x1, tgt, wout, gf, sink):
    seq = x1.shape[0]
    nq, nb = seq // TQ, seq // BLK

    def body(q_ref, gate_ref, kp_ref, k_ref, kn_ref, vp_ref, v_ref, vn_ref, x1_ref, tgt_ref, wo_ref, gf_ref, sink_ref,
             dx2_ref, dx2b_ref, yt_ref, att_ref, lse_ref, loss_ref, dgf_ref, kbuf, vbuf, att_scr):
        i = pl.program_id(0)

        @pl.when(i == 0)
        def _():
            loss_ref[...] = jnp.zeros_like(loss_ref)
            dgf_ref[...] = jnp.zeros_like(dgf_ref)

        _fill_band(kbuf, kp_ref, k_ref, kn_ref)
        _fill_band(vbuf, vp_ref, v_ref, vn_ref)
        for j in range(TQ // BLK):
            c0 = j * BLK
            bias = _band_bias_t(i * (TQ // BLK) + j, nb)
            lse_rows = []
            for kv in range(N_KV):
                rows = slice(kv * HD, (kv + 1) * HD)
                q4 = _heads_t(q_ref, kv, c0)
                st = _tn(kbuf[rows, c0:c0 + 3 * BLK], q4) * SCALE + bias
                sk = _sink_row(sink_ref, kv)
                m = jnp.maximum(jnp.max(st, axis=0, keepdims=True), sk)
                p = jnp.exp(st - m)
                den = jnp.sum(p, axis=0, keepdims=True) + jnp.exp(sk - m)
                ot = _mm(vbuf[rows, c0:c0 + 3 * BLK], p.astype(BF16)) / den
                lse = m + jnp.log(den)
                for g in range(GQA):
                    h = kv * GQA + g
                    att_scr[h * HD:(h + 1) * HD, c0:c0 + BLK] = ot[:, g * BLK:(g + 1) * BLK]
                    lse_rows.append(lse[:, g * BLK:(g + 1) * BLK])
            lse_ref[:, c0:c0 + BLK] = jnp.concatenate(lse_rows, axis=0)

        att = att_scr[...]
        gate = gate_ref[...].astype(F32)
        yt = (att * (gate * jax.nn.sigmoid(gate))).astype(BF16)
        yt_ref[...] = yt
        att_ref[...] = att.astype(BF16)
        x2 = x1_ref[...] + _mm(yt.T, wo_ref[...])
        r = lax.rsqrt(jnp.mean(x2 * x2, axis=1, keepdims=True) + EPS)
        xn = x2 * r
        diff = xn * gf_ref[...] - tgt_ref[...]
        loss_ref[...] += 0.5 * jnp.sum(jnp.mean(diff * diff, axis=1, keepdims=True), axis=0, keepdims=True)
        dout = diff * (1.0 / D)
        dgf_ref[...] += jnp.sum(dout * xn, axis=0, keepdims=True)
        dxn = dout * gf_ref[...]
        dx2 = r * (dxn - xn * jnp.mean(dxn * xn, axis=1, keepdims=True))
        dx2_ref[...] = dx2
        dx2b_ref[...] = dx2.astype(BF16)

    ident = lambda i: i
    row = pl.BlockSpec((TQ, D), lambda i: (i, 0))
    col = lambda rows: pl.BlockSpec((rows, TQ), lambda i: (0, i))
    return pl.pallas_call(
        body, grid=(nq,), name="l1_attn_fwd",
        out_shape=(jax.ShapeDtypeStruct((seq, D), F32), jax.ShapeDtypeStruct((seq, D), BF16),
                   jax.ShapeDtypeStruct((D, seq), BF16), jax.ShapeDtypeStruct((D, seq), BF16),
                   jax.ShapeDtypeStruct((N_HEADS, seq), F32), jax.ShapeDtypeStruct((1, 1), F32),
                   jax.ShapeDtypeStruct((1, D), F32)),
        in_specs=[col(D), col(D)] + _band_specs_t(nb, ident) + _band_specs_t(nb, ident) + [
            row, row, _resident((D, D)), _resident((1, D)), pl.BlockSpec(memory_space=pltpu.SMEM)],
        out_specs=(row, row, col(D), col(D), col(N_HEADS), pl.BlockSpec((1, 1), lambda i: (0, 0)),
                   pl.BlockSpec((1, D), lambda i: (0, 0))),
        scratch_shapes=[pltpu.VMEM((KV_W, TQ + 2 * BLK), BF16), pltpu.VMEM((KV_W, TQ + 2 * BLK), BF16),
                        pltpu.VMEM((D, TQ), F32)],
        compiler_params=_params(56),
    )(qt, gatet, kt, kt, kt, vt, vt, vt, x1, tgt, wout, gf, sink)


def _l1_attn_bwd(dx2b, wout, qt, kt, vt, gatet, att, lse, sink):
    seq = dx2b.shape[0]
    nq, nb = seq // TQ, seq // BLK

    def body(dx_ref, wo_ref, q_ref, gate_ref, kp_ref, k_ref, kn_ref, vp_ref, v_ref, vn_ref, att_ref, lse_ref, sink_ref,
             dq_ref, dgate_ref, dk_ref, dv_ref, dsink_ref, kbuf, vbuf, dkacc, dvacc, dat_scr, delta_scr, dsacc):
        i = pl.program_id(0)

        @pl.when(i == 0)
        def _():
            dkacc[...] = jnp.zeros_like(dkacc)
            dvacc[...] = jnp.zeros_like(dvacc)
            dsacc[...] = jnp.zeros_like(dsacc)

        @pl.when(i > 0)
        def _():
            for acc in (dkacc, dvacc):
                acc[:, 0:2 * BLK] = acc[:, TQ:TQ + 2 * BLK]
                acc[:, 2 * BLK:2 * BLK + TQ] = jnp.zeros((KV_W, TQ), F32)

        @pl.when(i < nq)
        def _():
            _fill_band(kbuf, kp_ref, k_ref, kn_ref)
            _fill_band(vbuf, vp_ref, v_ref, vn_ref)
            dyt = _nt(wo_ref[...], dx_ref[...])
            sg, dsg = _silu_and_grad(gate_ref[...].astype(F32))
            attf = att_ref[...].astype(F32)
            dat = dyt * sg
            dat_scr[...] = dat.astype(BF16)
            dgate_ref[...] = (dyt * attf * dsg).astype(BF16)
            dl = dat * attf
            delta_scr[...] = jnp.concatenate(
                [jnp.sum(dl[h * HD:(h + 1) * HD, :], axis=0, keepdims=True) for h in range(N_HEADS)], axis=0)
            for j in range(TQ // BLK):
                c0 = j * BLK
                bias = _band_bias_t(i * (TQ // BLK) + j, nb)
                for kv in range(N_KV):
                    rows = slice(kv * HD, (kv + 1) * HD)
                    q4 = _heads_t(q_ref, kv, c0)
                    do4 = _heads_t(dat_scr, kv, c0)
                    lse4 = _row4(lse_ref, kv, c0)
                    delta4 = _row4(delta_scr, kv, c0)
                    kth = kbuf[rows, c0:c0 + 3 * BLK]
                    vth = vbuf[rows, c0:c0 + 3 * BLK]
                    p = jnp.exp(_tn(kth, q4) * SCALE + bias - lse4)
                    dp = _tn(vth, do4)
                    ds = (p * (dp - delta4) * SCALE).astype(BF16)
                    dq4 = _mm(kth, ds)
                    dkacc[rows, c0:c0 + 3 * BLK] += _nt(q4, ds)
                    dvacc[rows, c0:c0 + 3 * BLK] += _nt(do4, p.astype(BF16))
                    dsk = -jnp.exp(_sink_row(sink_ref, kv) - lse4) * delta4
                    for g in range(GQA):
                        h = kv * GQA + g
                        dq_ref[h * HD:(h + 1) * HD, c0:c0 + BLK] = dq4[:, g * BLK:(g + 1) * BLK].astype(BF16)
                        dsacc[h:h + 1, :] += dsk[:, g * BLK:(g + 1) * BLK]

        dk_ref[...] = dkacc[:, 0:TQ].astype(BF16)
        dv_ref[...] = dvacc[:, 0:TQ].astype(BF16)

        @pl.when(i == nq)
        def _():
            dsink_ref[...] = jnp.broadcast_to(jnp.sum(dsacc[...], axis=1, keepdims=True), (N_HEADS, LANES))

    clamp = lambda i: jnp.minimum(i, nq - 1)
    row = pl.BlockSpec((TQ, D), lambda i: (clamp(i), 0))
    col = lambda rows: pl.BlockSpec((rows, TQ), lambda i: (0, clamp(i)))
    pad = pl.BlockSpec((KV_W, TQ), lambda i: (0, i))
    return pl.pallas_call(
        body, grid=(nq + 1,), name="l1_attn_bwd",
        out_shape=(jax.ShapeDtypeStruct((D, seq), BF16), jax.ShapeDtypeStruct((D, seq), BF16),
                   jax.ShapeDtypeStruct((KV_W, seq + TQ), BF16), jax.ShapeDtypeStruct((KV_W, seq + TQ), BF16),
                   jax.ShapeDtypeStruct((N_HEADS, LANES), F32)),
        in_specs=[row, _resident((D, D)), col(D), col(D)] + _band_specs_t(nb, clamp) + _band_specs_t(nb, clamp) + [
            col(D), col(N_HEADS), pl.BlockSpec(memory_space=pltpu.SMEM)],
        out_specs=(col(D), col(D), pad, pad, pl.BlockSpec((N_HEADS, LANES), lambda i: (0, 0))),
        scratch_shapes=[pltpu.VMEM((KV_W, TQ + 2 * BLK), BF16), pltpu.VMEM((KV_W, TQ + 2 * BLK), BF16),
                        pltpu.VMEM((KV_W, TQ + 2 * BLK), F32), pltpu.VMEM((KV_W, TQ + 2 * BLK), F32),
                        pltpu.VMEM((D, TQ), BF16), pltpu.VMEM((N_HEADS, TQ), F32), pltpu.VMEM((N_HEADS, LANES), F32)],
        compiler_params=_params(56),
    )(dx2b, wout, qt, gatet, kt, kt, kt, vt, vt, vt, att, lse, sink)


def _l1_in_proj_bwd(dq_r, dk_r, dv, dgate, cos_t, sin_t, w, x1, g1, dx2):
    seq = x1.shape[0]
    tm = 512

    def body(dq_ref, dk_ref, dv_ref, dg_ref, c_ref, s_ref, w_ref, x_ref, g_ref, dres_ref,
             dx_ref, dxb_ref, dqo_ref, dko_ref, dn_ref):
        @pl.when(pl.program_id(0) == 0)
        def _():
            dn_ref[...] = jnp.zeros_like(dn_ref)

        c, s = c_ref[...], s_ref[...]
        dq = _rope_t(dq_ref[...].astype(F32), c, s, N_HEADS, -1).astype(BF16)
        dk = _rope_t(dk_ref[...].astype(F32), c, s, N_KV, -1).astype(BF16)
        dqo_ref[...] = dq
        dko_ref[...] = dk
        dht = (_mm(w_ref[:, Q_ROWS[0]:Q_ROWS[1]], dq) + _mm(w_ref[:, K_ROWS[0]:K_ROWS[1]], dk)
               + _mm(w_ref[:, V_ROWS[0]:V_ROWS[1]], dv_ref[...]) + _mm(w_ref[:, G_ROWS[0]:G_ROWS[1]], dg_ref[...]))
        dh = dht.T
        xf = x_ref[...]
        r = lax.rsqrt(jnp.mean(xf * xf, axis=1, keepdims=True) + EPS)
        xn = xf * r
        dn_ref[...] += jnp.sum(dh * xn, axis=0, keepdims=True)
        dxn = dh * g_ref[...]
        dx = dres_ref[...] + r * (dxn - xn * jnp.mean(dxn * xn, axis=1, keepdims=True))
        dx_ref[...] = dx
        dxb_ref[...] = dx.astype(BF16)

    row = pl.BlockSpec((tm, D), lambda i: (i, 0))
    col = lambda rows: pl.BlockSpec((rows, tm), lambda i: (0, i))
    return pl.pallas_call(
        body, grid=(seq // tm,), name="l1_in_proj_bwd",
        out_shape=(jax.ShapeDtypeStruct((seq, D), F32), jax.ShapeDtypeStruct((seq, D), BF16),
                   jax.ShapeDtypeStruct((D, seq), BF16), jax.ShapeDtypeStruct((KV_W, seq), BF16),
                   jax.ShapeDtypeStruct((1, D), F32)),
        in_specs=[col(D), col(KV_W), col(KV_W), col(D), col(ROT_HALF), col(ROT_HALF), _resident((D, MIX1_IN)), row,
                  _resident((1, D)), row],
        out_specs=(row, row, col(D), col(KV_W), pl.BlockSpec((1, D), lambda i: (0, 0))),
        compiler_params=_params(48),
    )(dq_r, dk_r, dv, dgate, cos_t, sin_t, w, x1, g1, dx2)


def _l0_mix_bwd(dx1b, wout_t, za, bx, bg, ws, ws_t, bias, gv, wg, wg_t, scale):
    seq = dx1b.shape[0]
    ts = 256
    n_tiles = seq // ts

    def body(dx_ref, wot_ref, za_ref, bx_ref, bxp_ref, bxn_ref, bg_ref, ws_ref, wst_ref, bias_ref, gv_ref, wg_ref,
             wgt_ref, sc_ref,
             dza_ref, dp_ref, dgb_ref, catt_ref, dws_ref, dbias_ref, dgv_ref, dsc_ref, dwg_ref, db_ref, xe_ref):
        i = pl.program_id(0)

        @pl.when(i == 0)
        def _():
            for r_ in (dws_ref, dbias_ref, dgv_ref, dsc_ref, dwg_ref, db_ref):
                r_[...] = jnp.zeros_like(r_)

        dxb = dx_ref[...]
        dya = _mm(dxb, wot_ref[:, 0:D])
        dyb = _mm(dxb, wot_ref[:, D:2 * D])

        u, du = _gelu_and_grad(za_ref[:, 0:D].astype(F32))
        vg, dvg_dz = _gelu_and_grad(za_ref[:, D:2 * D].astype(F32))
        rv = lax.rsqrt(jnp.mean(vg * vg, axis=1, keepdims=True) + EPS)
        vnorm = vg * rv
        gvw = gv_ref[...]
        vnb = (vnorm * gvw).astype(BF16)
        mixed = _spatial_mix(ws_ref, vnb, bias_ref[...], ts)
        sga, dsga = _silu_and_grad(za_ref[:, 2 * D:3 * D].astype(F32))
        um = u * mixed
        ya = (um * sga).astype(BF16)
        t = dya * sga
        dza_ref[:, 0:D] = (t * mixed * du).astype(BF16)
        dza_ref[:, 2 * D:3 * D] = (dya * um * dsga).astype(BF16)
        dmixed = t * u
        dmb = dmixed.astype(BF16)
        dvn_rows = []
        dbias = jnp.zeros((CHUNK, D), F32)
        for c in range(ts // CHUNK):
            rows = slice(c * CHUNK, (c + 1) * CHUNK)
            dbias = dbias + dmixed[rows, :]
            parts = []
            for h in range(A_GROUPS):
                cols = slice(h * GDIM, (h + 1) * GDIM)
                dws_ref[h] += _nt(dmb[rows, cols], vnb[rows, cols])
                parts.append(_mm(wst_ref[h], dmb[rows, cols]))
            dvn_rows.append(jnp.concatenate(parts, axis=1))
        dbias_ref[...] += dbias
        dvn = jnp.concatenate(dvn_rows, axis=0)
        dgv_ref[...] += jnp.sum(dvn * vnorm, axis=0, keepdims=True)
        dxn = dvn * gvw
        dvg = rv * (dxn - vnorm * jnp.mean(dxn * vnorm, axis=1, keepdims=True))
        dza_ref[:, D:2 * D] = (dvg * dvg_dz).astype(BF16)

        _fill_halo(xe_ref, bx_ref[...], bxp_ref, bxn_ref, i, n_tiles, ts)
        pb = _pool_forward(xe_ref, ts, i * ts, seq).astype(BF16)
        ypre = jnp.concatenate([_mm(pb[:, g * GDIM:(g + 1) * GDIM], wg_ref[g]) for g in range(4)], axis=1)
        sc = sc_ref[...]
        y = ypre * sc
        sgb, dsgb = _silu_and_grad(bg_ref[...].astype(F32))
        yb = (y * sgb).astype(BF16)
        dy_b = dyb * sgb
        dgb_ref[...] = (dyb * y * dsgb).astype(BF16)
        dsc_ref[...] += jnp.sum(dy_b * ypre, axis=0, keepdims=True)
        dypre = (dy_b * sc).astype(BF16)
        dps = []
        for g in range(4):
            cols = slice(g * GDIM, (g + 1) * GDIM)
            dwg_ref[g] += _tn(pb[:, cols], dypre[:, cols])
            dps.append(_mm(dypre[:, cols], wgt_ref[g]))
        dp_ref[...] = jnp.concatenate(dps, axis=1)
        catt_ref[...] = jnp.concatenate([ya, yb], axis=1).T

        @pl.when(i == n_tiles - 1)
        def _():
            for h in range(A_GROUPS):
                tot = jnp.sum(dbias_ref[:, h * GDIM:(h + 1) * GDIM].T, axis=0, keepdims=True)
                db_ref[pl.ds(h * 8, 8), :] = jnp.broadcast_to(tot, (8, CHUNK))

    prev, nxt = _halo_specs(ts, seq, D)
    row = lambda w_: pl.BlockSpec((ts, w_), lambda i: (i, 0))
    acc = lambda shape: pl.BlockSpec(shape, lambda i: (0,) * len(shape))
    return pl.pallas_call(
        body, grid=(n_tiles,), name="l0_mix_bwd",
        out_shape=(jax.ShapeDtypeStruct((seq, 3 * D), BF16), jax.ShapeDtypeStruct((seq, D), F32),
                   jax.ShapeDtypeStruct((seq, D), BF16), jax.ShapeDtypeStruct((2 * D, seq), BF16),
                   jax.ShapeDtypeStruct((4, CHUNK, CHUNK), F32), jax.ShapeDtypeStruct((CHUNK, D), F32),
                   jax.ShapeDtypeStruct((1, D), F32), jax.ShapeDtypeStruct((1, D), F32),
                   jax.ShapeDtypeStruct((4, GDIM, GDIM), F32), jax.ShapeDtypeStruct((32, CHUNK), F32)),
        in_specs=[row(D), _resident((D, 2 * D)), row(3 * D), row(D), prev, nxt, row(D), _resident((4, CHUNK, CHUNK)),
                  _resident((4, CHUNK, CHUNK)), _resident((CHUNK, D)), _resident((1, D)), _resident((4, GDIM, GDIM)),
                  _resident((4, GDIM, GDIM)), _resident((1, D))],
        out_specs=(row(3 * D), row(D), row(D), pl.BlockSpec((2 * D, ts), lambda i: (0, i)),
                   acc((4, CHUNK, CHUNK)), acc((CHUNK, D)), acc((1, D)), acc((1, D)), acc((4, GDIM, GDIM)),
                   acc((32, CHUNK))),
        scratch_shapes=[pltpu.VMEM((ts + 2 * POOL_HALO, D), F32)],
        compiler_params=_params(56),
    )(dx1b, wout_t, za, bx, bx, bx, bg, ws, ws_t, bias, gv, wg, wg_t, scale)


def _l0_pool_bwd(dp):
    seq = dp.shape[0]
    ts = 512
    n_tiles = seq // ts
    ext = ts + 2 * POOL_HALO

    def body(dp_ref, dpp_ref, dpn_ref, out_ref, qe_ref):
        i = pl.program_id(0)
        _fill_halo(qe_ref, dp_ref[...], dpp_ref, dpn_ref, i, n_tiles, ts)
        te = i * ts - POOL_HALO + lax.broadcasted_iota(jnp.int32, (ext, 1), 0)
        for gi, w in enumerate(POOL_WINDOWS):
            hw = w // 2
            cols = slice(gi * GDIM, (gi + 1) * GDIM)
            cnt = jnp.maximum(jnp.minimum(te + hw, seq) - jnp.maximum(te - hw, 0), 1).astype(F32)
            qe_ref[:, cols] = qe_ref[:, cols] / cnt
        outs = []
        for gi, w in enumerate(POOL_WINDOWS):
            hw = w // 2
            cols = slice(gi * GDIM, (gi + 1) * GDIM)
            acc = qe_ref[pl.ds(POOL_HALO - hw + 1, ts), cols]
            for k in range(-hw + 2, hw + 1):
                acc = acc + qe_ref[pl.ds(POOL_HALO + k, ts), cols]
            outs.append(acc - dp_ref[:, cols])
        out_ref[...] = jnp.concatenate(outs, axis=1).astype(BF16)

    prev, nxt = _halo_specs(ts, seq, D)
    row = pl.BlockSpec((ts, D), lambda i: (i, 0))
    return pl.pallas_call(
        body, grid=(n_tiles,), name="l0_pool_bwd",
        out_shape=jax.ShapeDtypeStruct((seq, D), BF16),
        in_specs=[row, prev, nxt], out_specs=row,
        scratch_shapes=[pltpu.VMEM((ext, D), F32)],
        compiler_params=_params(32),
    )(dp, dp, dp)


def _l0_in_proj_bwd(dza, dbx, dgb, w_t, x, g0, dx1):
    seq = x.shape[0]
    tm = 512

    def body(dza_ref, dbx_ref, dgb_ref, wt_ref, x_ref, g_ref, dres_ref, dx_ref, dn_ref):
        @pl.when(pl.program_id(0) == 0)
        def _():
            dn_ref[...] = jnp.zeros_like(dn_ref)

        dh = (_mm(dza_ref[...], wt_ref[0:3 * D, :]) + _mm(dbx_ref[...], wt_ref[3 * D:4 * D, :])
              + _mm(dgb_ref[...], wt_ref[4 * D:5 * D, :]))
        xf = x_ref[...]
        r = lax.rsqrt(jnp.mean(xf * xf, axis=1, keepdims=True) + EPS)
        xn = xf * r
        dn_ref[...] += jnp.sum(dh * xn, axis=0, keepdims=True)
        dxn = dh * g_ref[...]
        dx_ref[...] = dres_ref[...] + r * (dxn - xn * jnp.mean(dxn * xn, axis=1, keepdims=True))

    row = lambda w_: pl.BlockSpec((tm, w_), lambda i: (i, 0))
    return pl.pallas_call(
        body, grid=(seq // tm,), name="l0_in_proj_bwd",
        out_shape=(jax.ShapeDtypeStruct((seq, D), F32), jax.ShapeDtypeStruct((1, D), F32)),
        in_specs=[row(3 * D), row(D), row(D), _resident((MIX0_IN, D)), row(D), _resident((1, D)), row(D)],
        out_specs=(row(D), pl.BlockSpec((1, D), lambda i: (0, 0))),
        compiler_params=_params(56),
    )(dza, dbx, dgb, w_t, x, g0, dx1)


def _dw_matmul(a_t, b, name, b_transposed=False):
    k, seq = a_t.shape
    n = b.shape[0] if b_transposed else b.shape[1]
    tn = min(n, 1024)
    ts = 512

    def body(a_ref, b_ref, o_ref):
        @pl.when(pl.program_id(1) == 0)
        def _():
            o_ref[...] = jnp.zeros_like(o_ref)

        o_ref[...] += _nt(a_ref[...], b_ref[...]) if b_transposed else _mm(a_ref[...], b_ref[...])

    b_spec = (pl.BlockSpec((tn, ts), lambda j, s: (j, s)) if b_transposed else pl.BlockSpec((ts, tn), lambda j, s: (s, j)))
    return pl.pallas_call(
        body, grid=(n // tn, seq // ts), name=name,
        out_shape=jax.ShapeDtypeStruct((k, n), F32),
        in_specs=[pl.BlockSpec((k, ts), lambda j, s: (0, s)), b_spec],
        out_specs=pl.BlockSpec((k, tn), lambda j, s: (0, j)),
        compiler_params=_params(48, 2),
    )(a_t, b)


def _cast_bf16(w_pack):
    rows = w_pack.shape[0]
    tr = rows // 8

    def body(w_ref, o_ref):
        o_ref[...] = w_ref[...].astype(BF16)

    spec = pl.BlockSpec((tr, LANES), lambda i: (i, 0))
    return pl.pallas_call(body, grid=(8,), name="cast_weights", out_shape=jax.ShapeDtypeStruct((rows, LANES), BF16),
                          in_specs=[spec], out_specs=spec, compiler_params=_params(32))(w_pack)


def _adamw_math(w, g, m, v):
    m2 = ADAM_B1 * m + (1.0 - ADAM_B1) * g
    v2 = ADAM_B2 * v + (1.0 - ADAM_B2) * (g * g)
    m_hat = m2 / (1.0 - ADAM_B1 ** ADAM_STEP)
    v_hat = v2 / (1.0 - ADAM_B2 ** ADAM_STEP)
    delta = -ADAM_LR * (m_hat / (jnp.sqrt(v_hat) + ADAM_EPS) + ADAM_WD * w)
    return delta, m2, v2


def _pair_sum(g_all, r1, c_idx):
    _, nchip, rows, _ = g_all.shape
    tr = rows // 8

    def body(c_ref, g_ref, r_ref, o_ref, ob_ref):
        p = g_ref[...] + r_ref[...]
        o_ref[...] = p
        ob_ref[...] = p.astype(BF16)

    blk = pl.BlockSpec((None, tr, LANES), lambda q, i, c: (q, i, 0))
    return pl.pallas_call(
        body, name="grad_pair_sum",
        out_shape=(jax.ShapeDtypeStruct((nchip, rows, LANES), F32), jax.ShapeDtypeStruct((nchip, rows, LANES), BF16)),
        grid_spec=pltpu.PrefetchScalarGridSpec(
            num_scalar_prefetch=1, grid=(nchip, 8),
            in_specs=[pl.BlockSpec((None, None, tr, LANES), lambda q, i, c: (c[0], q, i, 0)), blk],
            out_specs=(blk, blk)),
        compiler_params=_params(32, 2),
    )(c_idx, g_all, r1)


def _final_sum_adamw(p, r2, chip_idx, w, m, v):
    rows = w.shape[0]
    tr = rows // 8

    def body(q_ref, p_ref, r_ref, w_ref, m_ref, v_ref, g_out, d_out, m_out, v_out):
        g = p_ref[...] + r_ref[0].astype(F32) + r_ref[1].astype(F32) + r_ref[2].astype(F32)
        delta, m2, v2 = _adamw_math(w_ref[...], g, m_ref[...], v_ref[...])
        g_out[...] = g
        d_out[...] = delta
        m_out[...] = m2
        v_out[...] = v2

    flat = pl.BlockSpec((tr, LANES), lambda i, q: (i, 0))
    shp = jax.ShapeDtypeStruct((rows, LANES), F32)
    return pl.pallas_call(
        body, name="grad_sum_adamw", out_shape=(shp, shp, shp, shp),
        grid_spec=pltpu.PrefetchScalarGridSpec(
            num_scalar_prefetch=1, grid=(8,),
            in_specs=[pl.BlockSpec((None, tr, LANES), lambda i, q: (q[0], i, 0)),
                      pl.BlockSpec((3, tr, LANES), lambda i, q: (0, i, 0)), flat, flat, flat],
            out_specs=(flat, flat, flat, flat)),
        compiler_params=_params(32),
    )(chip_idx, p, r2, w, m, v)


def _small_pair_sum(gs, r1s):
    def body(g_ref, r_ref, o_ref):
        o_ref[...] = g_ref[...] + r_ref[...]

    return pl.pallas_call(body, name="small_pair_sum", out_shape=jax.ShapeDtypeStruct(gs.shape, F32))(gs, r1s)


def _small_sum_adamw(ps, r2s, chip_idx, w, m, v):
    def body(q_ref, p_ref, r_ref, w_ref, m_ref, v_ref, g_out, d_out, m_out, v_out, stack):
        stack[0] = p_ref[...]
        stack[2] = r_ref[0]
        stack[1] = r_ref[1]
        stack[3] = r_ref[2]
        me = q_ref[0]
        g = stack[me] + stack[me ^ 1]
        g = g + stack[me ^ 2]
        g = g + stack[me ^ 3]
        delta, m2, v2 = _adamw_math(w_ref[...], g, m_ref[...], v_ref[...])
        g_out[...] = g
        d_out[...] = delta
        m_out[...] = m2
        v_out[...] = v2

    shp = jax.ShapeDtypeStruct((SMALL_ROWS, LANES), F32)
    vm = pl.BlockSpec(memory_space=pltpu.VMEM)
    return pl.pallas_call(
        body, name="small_sum_adamw", out_shape=(shp, shp, shp, shp),
        in_specs=[pl.BlockSpec(memory_space=pltpu.SMEM), vm, vm, vm, vm, vm], out_specs=(vm, vm, vm, vm),
        scratch_shapes=[pltpu.VMEM((4, SMALL_ROWS, LANES), F32)],
    )(chip_idx, ps, r2s, w, m, v)


def _all_gather(blk):
    rows = blk.shape[0]

    def body(x_ref, out_ref, send_sems, recv_sems, local_sem):
        x, y, c = lax.axis_index("x"), lax.axis_index("y"), lax.axis_index("c")
        me, sibling = (x, y, c), (x, y, 1 - c)
        chips = [(1 - x, y), (x, 1 - y), (1 - x, 1 - y)]

        def slot(px, py, pc):
            return out_ref.at[4 * px + 2 * py + pc]

        def copy(k, block, to, src=None):
            return pltpu.make_async_remote_copy(
                src_ref=slot(*block) if src is None else src, dst_ref=slot(*block),
                send_sem=send_sems.at[k], recv_sem=recv_sems.at[k], device_id=to, device_id_type=MESH)

        mine = pltpu.make_async_copy(x_ref, slot(*me), local_sem)
        mine.start()
        first = [copy(0, me, sibling, src=x_ref)]
        first += [copy(1 + j, me, (*chip, c), src=x_ref) for j, chip in enumerate(chips)]
        for cp in first:
            cp.start()
        passed = [copy(4 + j, (*chip, c), sibling) for j, chip in enumerate(chips)]
        for j, chip in enumerate(chips):
            copy(1 + j, (*chip, c), me).wait_recv()
            passed[j].start()
        copy(0, sibling, me).wait_recv()
        for j, chip in enumerate(chips):
            copy(4 + j, (*chip, 1 - c), me).wait_recv()
        for cp in first + passed:
            cp.wait_send()
        mine.wait()

    any_spec = pl.BlockSpec(memory_space=pl.ANY)
    return pl.pallas_call(
        body, name="weights_all_gather", out_shape=jax.ShapeDtypeStruct((N_DEV, rows, LANES), blk.dtype),
        in_specs=[any_spec], out_specs=any_spec,
        scratch_shapes=[pltpu.SemaphoreType.DMA((7,)), pltpu.SemaphoreType.DMA((7,)), pltpu.SemaphoreType.DMA],
    )(blk)


def _sibling_exchange(g_all, gs):
    _, nchip, rows, _ = g_all.shape

    def body(g_ref, s_ref, r1_ref, r1s_ref, send_sems, recv_sems):
        x, y, c = lax.axis_index("x"), lax.axis_index("y"), lax.axis_index("c")
        sibling = (x, y, 1 - c)
        big = pltpu.make_async_remote_copy(src_ref=g_ref.at[1 - c], dst_ref=r1_ref, send_sem=send_sems.at[0],
                                           recv_sem=recv_sems.at[0], device_id=sibling, device_id_type=MESH)
        small = pltpu.make_async_remote_copy(src_ref=s_ref, dst_ref=r1s_ref, send_sem=send_sems.at[1],
                                             recv_sem=recv_sems.at[1], device_id=sibling, device_id_type=MESH)
        big.start()
        small.start()
        big.wait()
        small.wait()

    any_spec = pl.BlockSpec(memory_space=pl.ANY)
    return pl.pallas_call(
        body, name="grad_sibling_exchange",
        out_shape=(jax.ShapeDtypeStruct((nchip, rows, LANES), F32), jax.ShapeDtypeStruct(gs.shape, F32)),
        in_specs=[any_spec, any_spec], out_specs=(any_spec, any_spec),
        scratch_shapes=[pltpu.SemaphoreType.DMA((2,)), pltpu.SemaphoreType.DMA((2,))],
    )(g_all, gs)


def _chip_exchange(p, ps):
    nchip, rows, _ = p.shape

    def body(p_ref, s_ref, r2_ref, r2s_ref, send_sems, recv_sems):
        x, y, c = lax.axis_index("x"), lax.axis_index("y"), lax.axis_index("c")
        copies = []
        for j, (fx, fy) in enumerate(((1, 0), (0, 1), (1, 1))):
            tx = x ^ fx
            ty = y ^ fy
            to = (tx, ty, c)
            copies.append(pltpu.make_async_remote_copy(
                src_ref=p_ref.at[2 * tx + ty], dst_ref=r2_ref.at[j], send_sem=send_sems.at[j, 0],
                recv_sem=recv_sems.at[j, 0], device_id=to, device_id_type=MESH))
            copies.append(pltpu.make_async_remote_copy(
                src_ref=s_ref, dst_ref=r2s_ref.at[j], send_sem=send_sems.at[j, 1],
                recv_sem=recv_sems.at[j, 1], device_id=to, device_id_type=MESH))
        for cp in copies:
            cp.start()
        for cp in copies:
            cp.wait()

    any_spec = pl.BlockSpec(memory_space=pl.ANY)
    return pl.pallas_call(
        body, name="grad_chip_exchange",
        out_shape=(jax.ShapeDtypeStruct((3, rows, LANES), p.dtype), jax.ShapeDtypeStruct((3,) + ps.shape, F32)),
        in_specs=[any_spec, any_spec], out_specs=(any_spec, any_spec),
        scratch_shapes=[pltpu.SemaphoreType.DMA((3, 2)), pltpu.SemaphoreType.DMA((3, 2))],
    )(p, ps)


def _pack_shards(w_in_0, b_group_w_0, w_out_0, w_in_1, w_out_1):
    return jnp.concatenate([t.reshape(-1, LANES) for t in (w_in_0, b_group_w_0, w_out_0, w_in_1, w_out_1)], axis=0)


def _unpack_shards(buf):
    shapes = ((1024, 640), (4, 32, 256), (256, 1024), (1024, 320), (128, 1024))
    outs, r0 = [], 0
    for shape, rows in zip(shapes, PACK_ROWS):
        outs.append(buf[r0:r0 + rows].reshape(shape))
        r0 += rows
    return outs


def _pack_small(norm_0, a_v_norm_0, a_spatial_w_0, a_spatial_b_0, b_scale_0, norm_1, sink_1, final_norm):
    sink = jnp.concatenate([sink_1.reshape(-1), jnp.zeros((LANES - N_HEADS,), F32)])
    parts = [norm_0, a_v_norm_0, a_spatial_w_0, a_spatial_b_0, b_scale_0, norm_1, sink, final_norm]
    flat = jnp.concatenate([t.reshape(-1) for t in parts])
    flat = jnp.concatenate([flat, jnp.zeros((SMALL_ROWS * LANES - flat.shape[0],), F32)])
    return flat.reshape(SMALL_ROWS, LANES)


def _unpack_small(buf):
    flat = buf.reshape(-1)
    shapes = ((1024,), (1024,), (4, 128, 128), (4, 128), (1024,), (1024,), (128,), (1024,))
    outs, o = [], 0
    for shape, size in zip(shapes, SMALL_SIZES):
        outs.append(flat[o:o + size].reshape(shape))
        o += size
    outs[6] = outs[6][:N_HEADS]
    return outs


def _blocks(t, axis):
    shape = t.shape
    t = t.reshape(shape[:axis] + (N_DEV, shape[axis] // N_DEV) + shape[axis + 1:])
    t = jnp.moveaxis(t, axis, 0)
    return t.reshape(N_DEV, -1, LANES)


def kernel(x, norm_0, w_in_0, a_v_norm_0, a_spatial_w_0, a_spatial_b_0, b_group_w_0, b_scale_0, w_out_0, norm_1, w_in_1, sink_1, w_out_1, final_norm, loss_target, m_norm_0, m_w_in_0, m_a_v_norm_0, m_a_spatial_w_0, m_a_spatial_b_0, m_b_group_w_0, m_b_scale_0, m_w_out_0, m_norm_1, m_w_in_1, m_sink_1, m_w_out_1, m_final_norm, v_norm_0, v_w_in_0, v_a_v_norm_0, v_a_spatial_w_0, v_a_spatial_b_0, v_b_group_w_0, v_b_scale_0, v_w_out_0, v_norm_1, v_w_in_1, v_sink_1, v_w_out_1, v_final_norm):
    seq = x.shape[1]
    xs = x.reshape(seq, D)
    tgt = loss_target.reshape(seq, D)
    ax, ay, ac = lax.axis_index("x"), lax.axis_index("y"), lax.axis_index("c")
    c_idx = jnp.reshape(ac, (1,)).astype(jnp.int32)
    chip_idx = jnp.reshape(2 * ax + ay, (1,)).astype(jnp.int32)

    w_pack = _pack_shards(w_in_0, b_group_w_0, w_out_0, w_in_1, w_out_1)
    gathered = _all_gather(_cast_bf16(w_pack))
    r0 = 0
    parts = []
    for rows in PACK_ROWS:
        parts.append(gathered[:, r0:r0 + rows])
        r0 += rows
    win0 = parts[0].reshape(N_DEV, D, 640).transpose(1, 0, 2).reshape(D, MIX0_IN)
    wg = parts[1].reshape(N_DEV, 4, 32, GDIM).transpose(1, 0, 2, 3).reshape(4, GDIM, GDIM)
    wout0 = parts[2].reshape(2 * D, D)
    win1 = parts[3].reshape(N_DEV, D, 320).transpose(1, 0, 2).reshape(D, MIX1_IN)
    wout1 = parts[4].reshape(D, D)
    loss_part, grad_x, d_win0, d_wg, d_wout0, d_win1, d_wout1, gs = _local_step(
        xs, tgt, win0, wg, wout0, win1, wout1, norm_0, a_v_norm_0, a_spatial_w_0, a_spatial_b_0, b_scale_0, norm_1, sink_1,
        final_norm)

    g_blocks = jnp.concatenate([_blocks(d_win0, 1), _blocks(d_wg, 1), _blocks(d_wout0, 0), _blocks(d_win1, 1),
                                _blocks(d_wout1, 0)], axis=1)
    g_all = g_blocks.reshape(4, 2, PACK_TOTAL, LANES).transpose(1, 0, 2, 3)
    r1, r1s = _sibling_exchange(g_all, gs)
    p, p_wire = _pair_sum(g_all, r1, c_idx)
    ps = _small_pair_sum(gs, r1s)
    r2, r2s = _chip_exchange(p_wire, ps)

    m_pack = _pack_shards(m_w_in_0, m_b_group_w_0, m_w_out_0, m_w_in_1, m_w_out_1)
    v_pack = _pack_shards(v_w_in_0, v_b_group_w_0, v_w_out_0, v_w_in_1, v_w_out_1)
    big = [_unpack_shards(t) for t in _final_sum_adamw(p, r2, chip_idx, w_pack, m_pack, v_pack)]
    ws_pack = _pack_small(norm_0, a_v_norm_0, a_spatial_w_0, a_spatial_b_0, b_scale_0, norm_1, sink_1, final_norm)
    ms_pack = _pack_small(m_norm_0, m_a_v_norm_0, m_a_spatial_w_0, m_a_spatial_b_0, m_b_scale_0, m_norm_1, m_sink_1,
                          m_final_norm)
    vs_pack = _pack_small(v_norm_0, v_a_v_norm_0, v_a_spatial_w_0, v_a_spatial_b_0, v_b_scale_0, v_norm_1, v_sink_1,
                          v_final_norm)
    small = [_unpack_small(t) for t in _small_sum_adamw(ps, r2s, chip_idx, ws_pack, ms_pack, vs_pack)]

    def in_order(kind):
        b, s = big[kind], small[kind]
        return [s[0], b[0], s[1], s[2], s[3], b[1], s[4], b[2], s[5], b[3], s[6], b[4], s[7]]

    loss = lax.psum(loss_part[0, 0], ("x", "y", "c"))
    return (loss, grad_x.reshape(1, seq, D), *in_order(0), *in_order(1), *in_order(2), *in_order(3))


def _local_step(xs, tgt, win0, wg, wout0, win1, wout1, norm_0, a_v_norm_0, a_spatial_w_0, a_spatial_b_0, b_scale_0, norm_1,
                sink_1, final_norm):
    seq = xs.shape[0]
    win0_t, wout0_t, win1_t = win0.T, wout0.T, win1.T
    wg_t = jnp.swapaxes(wg, 1, 2)
    ws = a_spatial_w_0.astype(BF16)
    ws_t = jnp.swapaxes(ws, 1, 2)
    bias = jnp.repeat(a_spatial_b_0.T, GDIM, axis=1)
    g0, gv, scale, g1, gf = (t.reshape(1, D) for t in (norm_0, a_v_norm_0, b_scale_0, norm_1, final_norm))
    cos_t, sin_t = _rope_tables_t(seq)

    za, bx, bg, h0_t = _l0_in_proj(xs, g0, win0)
    x1 = _l0_mix_fwd(za, bx, bg, xs, ws, bias, gv, wg, scale, wout0)
    qt, kt, vt, gatet, h1_t = _l1_in_proj(x1, g1, win1_t, cos_t, sin_t)
    dx2, dx2b, y_t, att, lse, loss_part, d_gf = _l1_attn_fwd(qt, kt, vt, gatet, x1, tgt, wout1, gf, sink_1)

    d_wout1 = _dw_matmul(y_t, dx2b, "dw_out_1")
    dq_r, dgate, dk_pad, dv_pad, d_sink = _l1_attn_bwd(dx2b, wout1, qt, kt, vt, gatet, att, lse, sink_1)
    dk_r = dk_pad[:, BLK:BLK + seq]
    dv = dv_pad[:, BLK:BLK + seq]
    dx1, dx1b, dq, dk, d_g1 = _l1_in_proj_bwd(dq_r, dk_r, dv, dgate, cos_t, sin_t, win1, x1, g1, dx2)
    d_win1 = jnp.concatenate([_dw_matmul(h1_t, dq, "dw_in_1_q", True), _dw_matmul(h1_t, dk, "dw_in_1_k", True),
                              _dw_matmul(h1_t, dv, "dw_in_1_v", True), _dw_matmul(h1_t, dgate, "dw_in_1_gate", True)],
                             axis=1)

    dza, dp, dgb, cat_t, d_ws, _, d_gv, d_scale, d_wg, d_b = _l0_mix_bwd(
        dx1b, wout0_t, za, bx, bg, ws, ws_t, bias, gv, wg, wg_t, scale)
    d_wout0 = _dw_matmul(cat_t, dx1b, "dw_out_0")
    dbx = _l0_pool_bwd(dp)
    grad_x, d_g0 = _l0_in_proj_bwd(dza, dbx, dgb, win0_t, xs, g0, dx1)
    d_win0 = jnp.concatenate([_dw_matmul(h0_t, dza, "dw_in_0_a"), _dw_matmul(h0_t, dbx, "dw_in_0_bx"),
                              _dw_matmul(h0_t, dgb, "dw_in_0_bg")], axis=1)

    d_bs = d_b.reshape(4, 8, CHUNK)[:, 0, :]
    gs = _pack_small(d_g0[0], d_gv[0], d_ws, d_bs, d_scale[0], d_g1[0], d_sink[:, 0], d_gf[0])
    return loss_part, grad_x, d_win0, d_wg, d_wout0, d_win1, d_wout1, gs
```

```python
import jax
import jax.numpy as jnp
from jax import lax
from jax.experimental import pallas as pl
from jax.experimental.pallas import tpu as pltpu

F32 = jnp.float32
BF16 = jnp.bfloat16

D = 1024
EPS = 1e-6
NEG_INF = -1e30
CHUNK = 128
A_GROUPS = 4
POOL_WINDOWS = (2, 4, 8, 16)
POOL_HALO = 8
GDIM = 256
N_HEADS = 16
N_KV = 4
GQA = 4
HD = 64
BLK = 128
ROT_HALF = 8
ROPE_THETA = 500000.0
SCALE = HD ** -0.5
MIX0_IN = 5 * D
MIX1_IN = 2560
KV_W = N_KV * HD
Q_ROWS, K_ROWS, V_ROWS, G_ROWS = (0, D), (D, D + KV_W), (D + KV_W, D + 2 * KV_W), (D + 2 * KV_W, MIX1_IN)
TQ = 512

ADAM_LR = 0.001
ADAM_B1 = 0.9
ADAM_B2 = 0.999
ADAM_EPS = 1e-08
ADAM_WD = 0.01
ADAM_STEP = 10

N_DEV = 8
LANES = 128
MIB = 2 ** 20
MESH = pl.DeviceIdType.MESH


def _params(limit_mib, n_axes=1):
    return pltpu.CompilerParams(vmem_limit_bytes=limit_mib * MIB, dimension_semantics=("arbitrary",) * n_axes)


def _resident(shape):
    nd = len(shape)
    return pl.BlockSpec(shape, lambda *_: (0,) * nd, pipeline_mode=pl.Buffered(1))


def _gelu(x):
    k = 0.7978845608028654
    return 0.5 * x * (1.0 + jnp.tanh(k * (x + 0.044715 * x * x * x)))


def _gelu_and_grad(x):
    k = 0.7978845608028654
    x2 = x * x
    t = jnp.tanh(k * (x + 0.044715 * x * x2))
    g = 0.5 * x * (1.0 + t)
    dg = 0.5 * (1.0 + t) + 0.5 * x * (1.0 - t * t) * (k * (1.0 + 3.0 * 0.044715 * x2))
    return g, dg


def _silu_and_grad(x):
    s = jax.nn.sigmoid(x)
    return x * s, s * (1.0 + x * (1.0 - s))


def _nt(a, b):
    return lax.dot_general(a, b, (((1,), (1,)), ((), ())), preferred_element_type=F32)


def _tn(a, b):
    return lax.dot_general(a, b, (((0,), (0,)), ((), ())), preferred_element_type=F32)


def _mm(a, b):
    return jnp.dot(a, b, preferred_element_type=F32)


def _rope_tables_t(seq):
    inv = ROPE_THETA ** (-jnp.arange(0, 2 * ROT_HALF, 2, dtype=F32) / (2 * ROT_HALF))
    ang = inv[:, None] * jnp.arange(seq, dtype=F32)[None, :]
    return jnp.cos(ang), jnp.sin(ang)


def _rope_t(z, c, s, n_heads, sign):
    parts = []
    for h in range(n_heads):
        b = h * HD
        x1, x2 = z[b:b + ROT_HALF], z[b + ROT_HALF:b + 2 * ROT_HALF]
        if sign > 0:
            parts += [x1 * c - x2 * s, x2 * c + x1 * s]
        else:
            parts += [x1 * c + x2 * s, x2 * c - x1 * s]
        parts.append(z[b + 2 * ROT_HALF:b + HD])
    return jnp.concatenate(parts, axis=0)


def _l0_in_proj(x, g0, w):
    seq = x.shape[0]
    tm = 512

    def body(x_ref, g_ref, w_ref, za_ref, bx_ref, bg_ref, ht_ref):
        xf = x_ref[...]
        r = lax.rsqrt(jnp.mean(xf * xf, axis=1, keepdims=True) + EPS)
        h = (xf * r * g_ref[...]).astype(BF16)
        ht_ref[...] = h.T
        for j in range(3):
            za_ref[:, j * D:(j + 1) * D] = _mm(h, w_ref[:, j * D:(j + 1) * D]).astype(BF16)
        bx_ref[...] = _mm(h, w_ref[:, 3 * D:4 * D])
        bg_ref[...] = _mm(h, w_ref[:, 4 * D:5 * D]).astype(BF16)

    return pl.pallas_call(
        body, grid=(seq // tm,), name="l0_in_proj",
        out_shape=(jax.ShapeDtypeStruct((seq, 3 * D), BF16), jax.ShapeDtypeStruct((seq, D), F32),
                   jax.ShapeDtypeStruct((seq, D), BF16), jax.ShapeDtypeStruct((D, seq), BF16)),
        in_specs=[pl.BlockSpec((tm, D), lambda i: (i, 0)), _resident((1, D)), _resident((D, MIX0_IN))],
        out_specs=(pl.BlockSpec((tm, 3 * D), lambda i: (i, 0)), pl.BlockSpec((tm, D), lambda i: (i, 0)),
                   pl.BlockSpec((tm, D), lambda i: (i, 0)), pl.BlockSpec((D, tm), lambda i: (0, i))),
        compiler_params=_params(48),
    )(x, g0, w)


def _fill_halo(ext_ref, cur, prev_ref, next_ref, i, n_tiles, ts):
    ext_ref[pl.ds(0, POOL_HALO), :] = jnp.where(i > 0, prev_ref[...], 0.0)
    ext_ref[pl.ds(POOL_HALO, ts), :] = cur
    ext_ref[pl.ds(POOL_HALO + ts, POOL_HALO), :] = jnp.where(i < n_tiles - 1, next_ref[...], 0.0)


def _pool_forward(xe_ref, ts, t0, seq):
    tg = t0 + lax.broadcasted_iota(jnp.int32, (ts, 1), 0)
    outs = []
    for gi, w in enumerate(POOL_WINDOWS):
        hw = w // 2
        cols = slice(gi * GDIM, (gi + 1) * GDIM)
        acc = xe_ref[pl.ds(POOL_HALO - hw, ts), cols]
        for k in range(-hw + 1, hw):
            acc = acc + xe_ref[pl.ds(POOL_HALO + k, ts), cols]
        cnt = (jnp.minimum(tg + hw, seq) - jnp.maximum(tg - hw, 0)).astype(F32)
        outs.append(acc / cnt - xe_ref[pl.ds(POOL_HALO, ts), cols])
    return jnp.concatenate(outs, axis=1)


def _spatial_mix(ws_ref, vnb, bias, ts):
    rows = []
    for c in range(ts // CHUNK):
        vc = vnb[c * CHUNK:(c + 1) * CHUNK, :]
        rows.append(jnp.concatenate(
            [_mm(ws_ref[h], vc[:, h * GDIM:(h + 1) * GDIM]) for h in range(A_GROUPS)], axis=1) + bias)
    return jnp.concatenate(rows, axis=0)


def _halo_specs(ts, seq, width):
    per = ts // POOL_HALO
    last = seq // POOL_HALO - 1
    prev = pl.BlockSpec((POOL_HALO, width), lambda i: (jnp.maximum(i * per - 1, 0), 0))
    nxt = pl.BlockSpec((POOL_HALO, width), lambda i: (jnp.minimum((i + 1) * per, last), 0))
    return prev, nxt


def _l0_mix_fwd(za, bx, bg, x, ws, bias, gv, wg, scale, wout):
    seq = x.shape[0]
    ts = 512
    n_tiles = seq // ts

    def body(za_ref, bx_ref, bxp_ref, bxn_ref, bg_ref, x_ref, ws_ref, bias_ref, gv_ref, wg_ref, sc_ref, wo_ref,
             x1_ref, xe_ref):
        i = pl.program_id(0)
        u = _gelu(za_ref[:, 0:D].astype(F32))
        vg = _gelu(za_ref[:, D:2 * D].astype(F32))
        rv = lax.rsqrt(jnp.mean(vg * vg, axis=1, keepdims=True) + EPS)
        vnb = (vg * rv * gv_ref[...]).astype(BF16)
        mixed = _spatial_mix(ws_ref, vnb, bias_ref[...], ts)
        ag = za_ref[:, 2 * D:3 * D].astype(F32)
        ya = (u * mixed * (ag * jax.nn.sigmoid(ag))).astype(BF16)

        _fill_halo(xe_ref, bx_ref[...], bxp_ref, bxn_ref, i, n_tiles, ts)
        pb = _pool_forward(xe_ref, ts, i * ts, seq).astype(BF16)
        y = jnp.concatenate([_mm(pb[:, g * GDIM:(g + 1) * GDIM], wg_ref[g]) for g in range(4)], axis=1) * sc_ref[...]
        bgf = bg_ref[...].astype(F32)
        yb = (y * (bgf * jax.nn.sigmoid(bgf))).astype(BF16)
        x1_ref[...] = x_ref[...] + _mm(ya, wo_ref[0:D, :]) + _mm(yb, wo_ref[D:2 * D, :])

    prev, nxt = _halo_specs(ts, seq, D)
    row = lambda w: pl.BlockSpec((ts, w), lambda i: (i, 0))
    return pl.pallas_call(
        body, grid=(n_tiles,), name="l0_mix_fwd",
        out_shape=jax.ShapeDtypeStruct((seq, D), F32),
        in_specs=[row(3 * D), row(D), prev, nxt, row(D), row(D), _resident((4, CHUNK, CHUNK)), _resident((CHUNK, D)),
                  _resident((1, D)), _resident((4, GDIM, GDIM)), _resident((1, D)), _resident((2 * D, D))],
        out_specs=row(D),
        scratch_shapes=[pltpu.VMEM((ts + 2 * POOL_HALO, D), F32)],
        compiler_params=_params(56),
    )(za, bx, bx, bx, bg, x, ws, bias, gv, wg, scale, wout)


def _l1_in_proj(x1, g1, w_t, cos_t, sin_t):
    seq = x1.shape[0]
    tm = 512

    def body(x_ref, g_ref, wt_ref, c_ref, s_ref, q_ref, k_ref, v_ref, gate_ref, ht_ref):
        xf = x_ref[...]
        r = lax.rsqrt(jnp.mean(xf * xf, axis=1, keepdims=True) + EPS)
        ht = (xf * r * g_ref[...]).astype(BF16).T
        ht_ref[...] = ht
        c, s = c_ref[...], s_ref[...]
        q_ref[...] = _rope_t(_mm(wt_ref[Q_ROWS[0]:Q_ROWS[1], :], ht), c, s, N_HEADS, 1).astype(BF16)
        k_ref[...] = _rope_t(_mm(wt_ref[K_ROWS[0]:K_ROWS[1], :], ht), c, s, N_KV, 1).astype(BF16)
        v_ref[...] = _mm(wt_ref[V_ROWS[0]:V_ROWS[1], :], ht).astype(BF16)
        gate_ref[...] = _mm(wt_ref[G_ROWS[0]:G_ROWS[1], :], ht).astype(BF16)

    col = lambda rows: pl.BlockSpec((rows, tm), lambda i: (0, i))
    return pl.pallas_call(
        body, grid=(seq // tm,), name="l1_in_proj",
        out_shape=(jax.ShapeDtypeStruct((D, seq), BF16), jax.ShapeDtypeStruct((KV_W, seq), BF16),
                   jax.ShapeDtypeStruct((KV_W, seq), BF16), jax.ShapeDtypeStruct((D, seq), BF16),
                   jax.ShapeDtypeStruct((D, seq), BF16)),
        in_specs=[pl.BlockSpec((tm, D), lambda i: (i, 0)), _resident((1, D)), _resident((MIX1_IN, D)), col(ROT_HALF),
                  col(ROT_HALF)],
        out_specs=(col(D), col(KV_W), col(KV_W), col(D), col(D)),
        compiler_params=_params(48),
    )(x1, g1, w_t, cos_t, sin_t)


def _band_specs_t(nb, clamp_i):
    per = TQ // BLK
    prev = pl.BlockSpec((KV_W, BLK), lambda i: (0, jnp.maximum(clamp_i(i) * per - 1, 0)))
    cur = pl.BlockSpec((KV_W, TQ), lambda i: (0, clamp_i(i)))
    nxt = pl.BlockSpec((KV_W, BLK), lambda i: (0, jnp.minimum((clamp_i(i) + 1) * per, nb - 1)))
    return [prev, cur, nxt]


def _fill_band(buf, p_ref, c_ref, n_ref):
    buf[:, 0:BLK] = p_ref[...]
    buf[:, BLK:BLK + TQ] = c_ref[...]
    buf[:, BLK + TQ:2 * BLK + TQ] = n_ref[...]


def _band_bias_t(n, nb):
    c = lax.broadcasted_iota(jnp.int32, (3 * BLK, BLK), 0)
    r = lax.broadcasted_iota(jnp.int32, (3 * BLK, BLK), 1)
    ok = (c >= r) & (c <= r + 2 * BLK) & ((c >= BLK) | (n > 0)) & ((c < 2 * BLK) | (n < nb - 1))
    bias = jnp.where(ok, 0.0, NEG_INF).astype(F32)
    return jnp.concatenate([bias] * GQA, axis=1)


def _heads_t(ref, kv, c0):
    return jnp.concatenate([ref[(kv * GQA + g) * HD:(kv * GQA + g + 1) * HD, c0:c0 + BLK] for g in range(GQA)], axis=1)


def _row4(ref, kv, c0):
    return jnp.concatenate([ref[kv * GQA + g:kv * GQA + g + 1, c0:c0 + BLK] for g in range(GQA)], axis=1)


def _sink_row(sink_ref, kv):
    return jnp.concatenate([jnp.full((1, BLK), sink_ref[kv * GQA + g], F32) for g in range(GQA)], axis=1)


def _l1_attn_fwd(qt, kt, vt, gatet, x1, tgt, wout, gf, sink):
    seq = x1.shape[0]
    nq, nb = seq // TQ, seq // BLK

    def body(q_ref, gate_ref, kp_ref, k_ref, kn_ref, vp_ref, v_ref, vn_ref, x1_ref, tgt_ref, wo_ref, gf_ref, sink_ref,
             dx2_ref, dx2b_ref, yt_ref, att_ref, lse_ref, loss_ref, dgf_ref, kbuf, vbuf, att_scr):
        i = pl.program_id(0)

        @pl.when(i == 0)
        def _():
            loss_ref[...] = jnp.zeros_like(loss_ref)
            dgf_ref[...] = jnp.zeros_like(dgf_ref)

        _fill_band(kbuf, kp_ref, k_ref, kn_ref)
        _fill_band(vbuf, vp_ref, v_ref, vn_ref)
        for j in range(TQ // BLK):
            c0 = j * BLK
            bias = _band_bias_t(i * (TQ // BLK) + j, nb)
            lse_rows = []
            for kv in range(N_KV):
                rows = slice(kv * HD, (kv + 1) * HD)
                q4 = _heads_t(q_ref, kv, c0)
                st = _tn(kbuf[rows, c0:c0 + 3 * BLK], q4) * SCALE + bias
                sk = _sink_row(sink_ref, kv)
                m = jnp.maximum(jnp.max(st, axis=0, keepdims=True), sk)
                p = jnp.exp(st - m)
                den = jnp.sum(p, axis=0, keepdims=True) + jnp.exp(sk - m)
                ot = _mm(vbuf[rows, c0:c0 + 3 * BLK], p.astype(BF16)) / den
                lse = m + jnp.log(den)
                for g in range(GQA):
                    h = kv * GQA + g
                    att_scr[h * HD:(h + 1) * HD, c0:c0 + BLK] = ot[:, g * BLK:(g + 1) * BLK]
                    lse_rows.append(lse[:, g * BLK:(g + 1) * BLK])
            lse_ref[:, c0:c0 + BLK] = jnp.concatenate(lse_rows, axis=0)

        att = att_scr[...]
        gate = gate_ref[...].astype(F32)
        yt = (att * (gate * jax.nn.sigmoid(gate))).astype(BF16)
        yt_ref[...] = yt
        att_ref[...] = att.astype(BF16)
        x2 = x1_ref[...] + _mm(yt.T, wo_ref[...])
        r = lax.rsqrt(jnp.mean(x2 * x2, axis=1, keepdims=True) + EPS)
        xn = x2 * r
        diff = xn * gf_ref[...] - tgt_ref[...]
        loss_ref[...] += 0.5 * jnp.sum(jnp.mean(diff * diff, axis=1, keepdims=True), axis=0, keepdims=True)
        dout = diff * (1.0 / D)
        dgf_ref[...] += jnp.sum(dout * xn, axis=0, keepdims=True)
        dxn = dout * gf_ref[...]
        dx2 = r * (dxn - xn * jnp.mean(dxn * xn, axis=1, keepdims=True))
        dx2_ref[...] = dx2
        dx2b_ref[...] = dx2.astype(BF16)

    ident = lambda i: i
    row = pl.BlockSpec((TQ, D), lambda i: (i, 0))
    col = lambda rows: pl.BlockSpec((rows, TQ), lambda i: (0, i))
    return pl.pallas_call(
        body, grid=(nq,), name="l1_attn_fwd",
        out_shape=(jax.ShapeDtypeStruct((seq, D), F32), jax.ShapeDtypeStruct((seq, D), BF16),
                   jax.ShapeDtypeStruct((D, seq), BF16), jax.ShapeDtypeStruct((D, seq), BF16),
                   jax.ShapeDtypeStruct((N_HEADS, seq), F32), jax.ShapeDtypeStruct((1, 1), F32),
                   jax.ShapeDtypeStruct((1, D), F32)),
        in_specs=[col(D), col(D)] + _band_specs_t(nb, ident) + _band_specs_t(nb, ident) + [
            row, row, _resident((D, D)), _resident((1, D)), pl.BlockSpec(memory_space=pltpu.SMEM)],
        out_specs=(row, row, col(D), col(D), col(N_HEADS), pl.BlockSpec((1, 1), lambda i: (0, 0)),
                   pl.BlockSpec((1, D), lambda i: (0, 0))),
        scratch_shapes=[pltpu.VMEM((KV_W, TQ + 2 * BLK), BF16), pltpu.VMEM((KV_W, TQ + 2 * BLK), BF16),
                        pltpu.VMEM((D, TQ), F32)],
        compiler_params=_params(56),
    )(qt, gatet, kt, kt, kt, vt, vt, vt, x1, tgt, wout, gf, sink)


def _l1_attn_bwd(dx2b, wout, qt, kt, vt, gatet, att, lse, sink):
    seq = dx2b.shape[0]
    nq, nb = seq // TQ, seq // BLK

    def body(dx_ref, wo_ref, q_ref, gate_ref, kp_ref, k_ref, kn_ref, vp_ref, v_ref, vn_ref, att_ref, lse_ref, sink_ref,
             dq_ref, dgate_ref, dk_ref, dv_ref, dsink_ref, kbuf, vbuf, dkacc, dvacc, dat_scr, delta_scr, dsacc):
        i = pl.program_id(0)

        @pl.when(i == 0)
        def _():
            dkacc[...] = jnp.zeros_like(dkacc)
            dvacc[...] = jnp.zeros_like(dvacc)
            dsacc[...] = jnp.zeros_like(dsacc)

        @pl.when(i > 0)
        def _():
            for acc in (dkacc, dvacc):
                acc[:, 0:2 * BLK] = acc[:, TQ:TQ + 2 * BLK]
                acc[:, 2 * BLK:2 * BLK + TQ] = jnp.zeros((KV_W, TQ), F32)

        @pl.when(i < nq)
        def _():
            _fill_band(kbuf, kp_ref, k_ref, kn_ref)
            _fill_band(vbuf, vp_ref, v_ref, vn_ref)
            dyt = _nt(wo_ref[...], dx_ref[...])
            sg, dsg = _silu_and_grad(gate_ref[...].astype(F32))
            attf = att_ref[...].astype(F32)
            dat = dyt * sg
            dat_scr[...] = dat.astype(BF16)
            dgate_ref[...] = (dyt * attf * dsg).astype(BF16)
            dl = dat * attf
            delta_scr[...] = jnp.concatenate(
                [jnp.sum(dl[h * HD:(h + 1) * HD, :], axis=0, keepdims=True) for h in range(N_HEADS)], axis=0)
            for j in range(TQ // BLK):
                c0 = j * BLK
                bias = _band_bias_t(i * (TQ // BLK) + j, nb)
                for kv in range(N_KV):
                    rows = slice(kv * HD, (kv + 1) * HD)
                    q4 = _heads_t(q_ref, kv, c0)
                    do4 = _heads_t(dat_scr, kv, c0)
                    lse4 = _row4(lse_ref, kv, c0)
                    delta4 = _row4(delta_scr, kv, c0)
                    kth = kbuf[rows, c0:c0 + 3 * BLK]
                    vth = vbuf[rows, c0:c0 + 3 * BLK]
                    p = jnp.exp(_tn(kth, q4) * SCALE + bias - lse4)
                    dp = _tn(vth, do4)
                    ds = (p * (dp - delta4) * SCALE).astype(BF16)
                    dq4 = _mm(kth, ds)
                    dkacc[rows, c0:c0 + 3 * BLK] += _nt(q4, ds)
                    dvacc[rows, c0:c0 + 3 * BLK] += _nt(do4, p.astype(BF16))
                    dsk = -jnp.exp(_sink_row(sink_ref, kv) - lse4) * delta4
                    for g in range(GQA):
                        h = kv * GQA + g
                        dq_ref[h * HD:(h + 1) * HD, c0:c0 + BLK] = dq4[:, g * BLK:(g + 1) * BLK].astype(BF16)
                        dsacc[h:h + 1, :] += dsk[:, g * BLK:(g + 1) * BLK]

        dk_ref[...] = dkacc[:, 0:TQ].astype(BF16)
        dv_ref[...] = dvacc[:, 0:TQ].astype(BF16)

        @pl.when(i == nq)
        def _():
            dsink_ref[...] = jnp.broadcast_to(jnp.sum(dsacc[...], axis=1, keepdims=True), (N_HEADS, LANES))

    clamp = lambda i: jnp.minimum(i, nq - 1)
    row = pl.BlockSpec((TQ, D), lambda i: (clamp(i), 0))
    col = lambda rows: pl.BlockSpec((rows, TQ), lambda i: (0, clamp(i)))
    pad = pl.BlockSpec((KV_W, TQ), lambda i: (0, i))
    return pl.pallas_call(
        body, grid=(nq + 1,), name="l1_attn_bwd",
        out_shape=(jax.ShapeDtypeStruct((D, seq), BF16), jax.ShapeDtypeStruct((D, seq), BF16),
                   jax.ShapeDtypeStruct((KV_W, seq + TQ), BF16), jax.ShapeDtypeStruct((KV_W, seq + TQ), BF16),
                   jax.ShapeDtypeStruct((N_HEADS, LANES), F32)),
        in_specs=[row, _resident((D, D)), col(D), col(D)] + _band_specs_t(nb, clamp) + _band_specs_t(nb, clamp) + [
            col(D), col(N_HEADS), pl.BlockSpec(memory_space=pltpu.SMEM)],
        out_specs=(col(D), col(D), pad, pad, pl.BlockSpec((N_HEADS, LANES), lambda i: (0, 0))),
        scratch_shapes=[pltpu.VMEM((KV_W, TQ + 2 * BLK), BF16), pltpu.VMEM((KV_W, TQ + 2 * BLK), BF16),
                        pltpu.VMEM((KV_W, TQ + 2 * BLK), F32), pltpu.VMEM((KV_W, TQ + 2 * BLK), F32),
                        pltpu.VMEM((D, TQ), BF16), pltpu.VMEM((N_HEADS, TQ), F32), pltpu.VMEM((N_HEADS, LANES), F32)],
        compiler_params=_params(56),
    )(dx2b, wout, qt, gatet, kt, kt, kt, vt, vt, vt, att, lse, sink)


def _l1_in_proj_bwd(dq_r, dk_r, dv, dgate, cos_t, sin_t, w, x1, g1, dx2):
    seq = x1.shape[0]
    tm = 512

    def body(dq_ref, dk_ref, dv_ref, dg_ref, c_ref, s_ref, w_ref, x_ref, g_ref, dres_ref,
             dx_ref, dxb_ref, dz_ref, dn_ref):
        @pl.when(pl.program_id(0) == 0)
        def _():
            dn_ref[...] = jnp.zeros_like(dn_ref)

        c, s = c_ref[...], s_ref[...]
        dq = _rope_t(dq_ref[...].astype(F32), c, s, N_HEADS, -1).astype(BF16)
        dk = _rope_t(dk_ref[...].astype(F32), c, s, N_KV, -1).astype(BF16)
        dz = jnp.concatenate([dq, dk, dv_ref[...], dg_ref[...]], axis=0)
        dz_ref[...] = dz
        half = MIX1_IN // 2
        dh = (_mm(w_ref[:, 0:half], dz[0:half]) + _mm(w_ref[:, half:MIX1_IN], dz[half:MIX1_IN])).T
        xf = x_ref[...]
        r = lax.rsqrt(jnp.mean(xf * xf, axis=1, keepdims=True) + EPS)
        xn = xf * r
        dn_ref[...] += jnp.sum(dh * xn, axis=0, keepdims=True)
        dxn = dh * g_ref[...]
        dx = dres_ref[...] + r * (dxn - xn * jnp.mean(dxn * xn, axis=1, keepdims=True))
        dx_ref[...] = dx
        dxb_ref[...] = dx.astype(BF16)

    row = pl.BlockSpec((tm, D), lambda i: (i, 0))
    col = lambda rows: pl.BlockSpec((rows, tm), lambda i: (0, i))
    return pl.pallas_call(
        body, grid=(seq // tm,), name="l1_in_proj_bwd",
        out_shape=(jax.ShapeDtypeStruct((seq, D), F32), jax.ShapeDtypeStruct((seq, D), BF16),
                   jax.ShapeDtypeStruct((MIX1_IN, seq), BF16), jax.ShapeDtypeStruct((1, D), F32)),
        in_specs=[col(D), col(KV_W), col(KV_W), col(D), col(ROT_HALF), col(ROT_HALF), _resident((D, MIX1_IN)), row,
                  _resident((1, D)), row],
        out_specs=(row, row, col(MIX1_IN), pl.BlockSpec((1, D), lambda i: (0, 0))),
        compiler_params=_params(48),
    )(dq_r, dk_r, dv, dgate, cos_t, sin_t, w, x1, g1, dx2)


def _l0_mix_bwd(dx1b, wout_t, za, bx, bg, ws, ws_t, bias, gv, wg, wg_t, scale):
    seq = dx1b.shape[0]
    ts = 256
    n_tiles = seq // ts

    def body(dx_ref, wot_ref, za_ref, bx_ref, bxp_ref, bxn_ref, bg_ref, ws_ref, wst_ref, bias_ref, gv_ref, wg_ref,
             wgt_ref, sc_ref,
             dz_ref, dp_ref, catt_ref, dws_ref, dbias_ref, dgv_ref, dsc_ref, dwg_ref, db_ref, xe_ref):
        i = pl.program_id(0)

        @pl.when(i == 0)
        def _():
            for r_ in (dws_ref, dbias_ref, dgv_ref, dsc_ref, dwg_ref, db_ref):
                r_[...] = jnp.zeros_like(r_)

        dxb = dx_ref[...]
        dya = _mm(dxb, wot_ref[:, 0:D])
        dyb = _mm(dxb, wot_ref[:, D:2 * D])

        u, du = _gelu_and_grad(za_ref[:, 0:D].astype(F32))
        vg, dvg_dz = _gelu_and_grad(za_ref[:, D:2 * D].astype(F32))
        rv = lax.rsqrt(jnp.mean(vg * vg, axis=1, keepdims=True) + EPS)
        vnorm = vg * rv
        gvw = gv_ref[...]
        vnb = (vnorm * gvw).astype(BF16)
        mixed = _spatial_mix(ws_ref, vnb, bias_ref[...], ts)
        sga, dsga = _silu_and_grad(za_ref[:, 2 * D:3 * D].astype(F32))
        um = u * mixed
        ya = (um * sga).astype(BF16)
        t = dya * sga
        dz_ref[:, 0:D] = (t * mixed * du).astype(BF16)
        dz_ref[:, 2 * D:3 * D] = (dya * um * dsga).astype(BF16)
        dmixed = t * u
        dmb = dmixed.astype(BF16)
        dvn_rows = []
        dbias = jnp.zeros((CHUNK, D), F32)
        for c in range(ts // CHUNK):
            rows = slice(c * CHUNK, (c + 1) * CHUNK)
            dbias = dbias + dmixed[rows, :]
            parts = []
            for h in range(A_GROUPS):
                cols = slice(h * GDIM, (h + 1) * GDIM)
                dws_ref[h] += _nt(dmb[rows, cols], vnb[rows, cols])
                parts.append(_mm(wst_ref[h], dmb[rows, cols]))
            dvn_rows.append(jnp.concatenate(parts, axis=1))
        dbias_ref[...] += dbias
        dvn = jnp.concatenate(dvn_rows, axis=0)
        dgv_ref[...] += jnp.sum(dvn * vnorm, axis=0, keepdims=True)
        dxn = dvn * gvw
        dvg = rv * (dxn - vnorm * jnp.mean(dxn * vnorm, axis=1, keepdims=True))
        dz_ref[:, D:2 * D] = (dvg * dvg_dz).astype(BF16)

        _fill_halo(xe_ref, bx_ref[...], bxp_ref, bxn_ref, i, n_tiles, ts)
        pb = _pool_forward(xe_ref, ts, i * ts, seq).astype(BF16)
        ypre = jnp.concatenate([_mm(pb[:, g * GDIM:(g + 1) * GDIM], wg_ref[g]) for g in range(4)], axis=1)
        sc = sc_ref[...]
        y = ypre * sc
        sgb, dsgb = _silu_and_grad(bg_ref[...].astype(F32))
        yb = (y * sgb).astype(BF16)
        dy_b = dyb * sgb
        dz_ref[:, 3 * D:4 * D] = jnp.zeros((ts, D), BF16)
        dz_ref[:, 4 * D:5 * D] = (dyb * y * dsgb).astype(BF16)
        dsc_ref[...] += jnp.sum(dy_b * ypre, axis=0, keepdims=True)
        dypre = (dy_b * sc).astype(BF16)
        dps = []
        for g in range(4):
            cols = slice(g * GDIM, (g + 1) * GDIM)
            dwg_ref[g] += _tn(pb[:, cols], dypre[:, cols])
            dps.append(_mm(dypre[:, cols], wgt_ref[g]))
        dp_ref[...] = jnp.concatenate(dps, axis=1)
        catt_ref[...] = jnp.concatenate([ya, yb], axis=1).T

        @pl.when(i == n_tiles - 1)
        def _():
            for h in range(A_GROUPS):
                tot = jnp.sum(dbias_ref[:, h * GDIM:(h + 1) * GDIM].T, axis=0, keepdims=True)
                db_ref[pl.ds(h * 8, 8), :] = jnp.broadcast_to(tot, (8, CHUNK))

    prev, nxt = _halo_specs(ts, seq, D)
    row = lambda w_: pl.BlockSpec((ts, w_), lambda i: (i, 0))
    acc = lambda shape: pl.BlockSpec(shape, lambda i: (0,) * len(shape))
    return pl.pallas_call(
        body, grid=(n_tiles,), name="l0_mix_bwd",
        out_shape=(jax.ShapeDtypeStruct((seq, MIX0_IN), BF16), jax.ShapeDtypeStruct((seq, D), F32),
                   jax.ShapeDtypeStruct((2 * D, seq), BF16),
                   jax.ShapeDtypeStruct((4, CHUNK, CHUNK), F32), jax.ShapeDtypeStruct((CHUNK, D), F32),
                   jax.ShapeDtypeStruct((1, D), F32), jax.ShapeDtypeStruct((1, D), F32),
                   jax.ShapeDtypeStruct((4, GDIM, GDIM), F32), jax.ShapeDtypeStruct((32, CHUNK), F32)),
        in_specs=[row(D), _resident((D, 2 * D)), row(3 * D), row(D), prev, nxt, row(D), _resident((4, CHUNK, CHUNK)),
                  _resident((4, CHUNK, CHUNK)), _resident((CHUNK, D)), _resident((1, D)), _resident((4, GDIM, GDIM)),
                  _resident((4, GDIM, GDIM)), _resident((1, D))],
        out_specs=(row(MIX0_IN), row(D), pl.BlockSpec((2 * D, ts), lambda i: (0, i)),
                   acc((4, CHUNK, CHUNK)), acc((CHUNK, D)), acc((1, D)), acc((1, D)), acc((4, GDIM, GDIM)),
                   acc((32, CHUNK))),
        scratch_shapes=[pltpu.VMEM((ts + 2 * POOL_HALO, D), F32)],
        compiler_params=_params(56),
    )(dx1b, wout_t, za, bx, bx, bx, bg, ws, ws_t, bias, gv, wg, wg_t, scale)


def _l0_pool_bwd(dp, dz):
    seq = dp.shape[0]
    ts = 512
    n_tiles = seq // ts
    ext = ts + 2 * POOL_HALO

    def body(dp_ref, dpp_ref, dpn_ref, dz_ref, out_ref, qe_ref):
        i = pl.program_id(0)
        _fill_halo(qe_ref, dp_ref[...], dpp_ref, dpn_ref, i, n_tiles, ts)
        te = i * ts - POOL_HALO + lax.broadcasted_iota(jnp.int32, (ext, 1), 0)
        for gi, w in enumerate(POOL_WINDOWS):
            hw = w // 2
            cols = slice(gi * GDIM, (gi + 1) * GDIM)
            cnt = jnp.maximum(jnp.minimum(te + hw, seq) - jnp.maximum(te - hw, 0), 1).astype(F32)
            qe_ref[:, cols] = qe_ref[:, cols] / cnt
        outs = []
        for gi, w in enumerate(POOL_WINDOWS):
            hw = w // 2
            cols = slice(gi * GDIM, (gi + 1) * GDIM)
            acc = qe_ref[pl.ds(POOL_HALO - hw + 1, ts), cols]
            for k in range(-hw + 2, hw + 1):
                acc = acc + qe_ref[pl.ds(POOL_HALO + k, ts), cols]
            outs.append(acc - dp_ref[:, cols])
        out_ref[...] = jnp.concatenate(outs, axis=1).astype(BF16)

    prev, nxt = _halo_specs(ts, seq, D)
    row = pl.BlockSpec((ts, D), lambda i: (i, 0))
    return pl.pallas_call(
        body, grid=(n_tiles,), name="l0_pool_bwd",
        out_shape=jax.ShapeDtypeStruct(dz.shape, BF16),
        in_specs=[row, prev, nxt, pl.BlockSpec(memory_space=pl.ANY)],
        out_specs=pl.BlockSpec((ts, D), lambda i: (i, 3)),
        input_output_aliases={3: 0},
        scratch_shapes=[pltpu.VMEM((ext, D), F32)],
        compiler_params=_params(32),
    )(dp, dp, dp, dz)


def _l0_in_proj_bwd(dz, w_t, x, g0, dx1):
    seq = x.shape[0]
    tm = 512

    def body(dz_ref, wt_ref, x_ref, g_ref, dres_ref, dx_ref, dn_ref):
        @pl.when(pl.program_id(0) == 0)
        def _():
            dn_ref[...] = jnp.zeros_like(dn_ref)

        dh = _mm(dz_ref[...], wt_ref[...])
        xf = x_ref[...]
        r = lax.rsqrt(jnp.mean(xf * xf, axis=1, keepdims=True) + EPS)
        xn = xf * r
        dn_ref[...] += jnp.sum(dh * xn, axis=0, keepdims=True)
        dxn = dh * g_ref[...]
        dx_ref[...] = dres_ref[...] + r * (dxn - xn * jnp.mean(dxn * xn, axis=1, keepdims=True))

    row = lambda w_: pl.BlockSpec((tm, w_), lambda i: (i, 0))
    return pl.pallas_call(
        body, grid=(seq // tm,), name="l0_in_proj_bwd",
        out_shape=(jax.ShapeDtypeStruct((seq, D), F32), jax.ShapeDtypeStruct((1, D), F32)),
        in_specs=[row(MIX0_IN), _resident((MIX0_IN, D)), row(D), _resident((1, D)), row(D)],
        out_specs=(row(D), pl.BlockSpec((1, D), lambda i: (0, 0))),
        compiler_params=_params(56),
    )(dz, w_t, x, g0, dx1)


def _dw_matmul(a_t, b, name, b_transposed=False, tn=1024):
    k, seq = a_t.shape
    n = b.shape[0] if b_transposed else b.shape[1]
    tn = min(n, tn)
    ts = 512

    def body(a_ref, b_ref, o_ref):
        @pl.when(pl.program_id(1) == 0)
        def _():
            o_ref[...] = jnp.zeros_like(o_ref)

        o_ref[...] += _nt(a_ref[...], b_ref[...]) if b_transposed else _mm(a_ref[...], b_ref[...])

    b_spec = (pl.BlockSpec((tn, ts), lambda j, s: (j, s)) if b_transposed else pl.BlockSpec((ts, tn), lambda j, s: (s, j)))
    return pl.pallas_call(
        body, grid=(n // tn, seq // ts), name=name,
        out_shape=jax.ShapeDtypeStruct((k, n), F32),
        in_specs=[pl.BlockSpec((k, ts), lambda j, s: (0, s)), b_spec],
        out_specs=pl.BlockSpec((k, tn), lambda j, s: (0, j)),
        compiler_params=_params(48, 2),
    )(a_t, b)


ROW_TILES = 8


def _cast_shards(shards):
    n = len(shards)

    def body(*refs):
        for a in range(n):
            refs[n + a][...] = refs[a][...].astype(BF16)

    vm = pl.BlockSpec(memory_space=pltpu.VMEM)
    return pl.pallas_call(body, name="cast_weights", out_shape=[jax.ShapeDtypeStruct(t.shape, BF16) for t in shards],
                          in_specs=[vm] * n, out_specs=[vm] * n, compiler_params=_params(32, 0))(*shards)


def _adamw_math(w, g, m, v):
    m2 = ADAM_B1 * m + (1.0 - ADAM_B1) * g
    v2 = ADAM_B2 * v + (1.0 - ADAM_B2) * (g * g)
    m_hat = m2 / (1.0 - ADAM_B1 ** ADAM_STEP)
    v_hat = v2 / (1.0 - ADAM_B2 ** ADAM_STEP)
    delta = -ADAM_LR * (m_hat / (jnp.sqrt(v_hat) + ADAM_EPS) + ADAM_WD * w)
    return delta, m2, v2


def _pair_sum(g_list, r1_list, pos):
    n = len(g_list)

    def body(pos_ref, *refs):
        q = pl.program_id(1)
        for a in range(n):
            refs[2 * n + a][...] = (refs[a][...] + refs[n + a][...]).astype(BF16)

        @pl.when(q == pos_ref[1])
        def _():
            for a in range(n):
                refs[3 * n + a][...] = refs[a][...] + refs[n + a][...]

    g_specs, r_specs, own_specs, wire_shapes, own_shapes = [], [], [], [], []
    for g in g_list:
        _, nchip, rows, width = g.shape
        tr = rows // ROW_TILES
        g_specs.append(pl.BlockSpec((None, None, tr, width), lambda i, q, pos: (pos[0], q, i, 0)))
        r_specs.append(pl.BlockSpec((None, tr, width), lambda i, q, pos: (q, i, 0)))
        own_specs.append(pl.BlockSpec((tr, width), lambda i, q, pos: (i, 0)))
        wire_shapes.append(jax.ShapeDtypeStruct((nchip, rows, width), BF16))
        own_shapes.append(jax.ShapeDtypeStruct((rows, width), F32))
    out = pl.pallas_call(
        body, name="grad_pair_sum", out_shape=wire_shapes + own_shapes,
        grid_spec=pltpu.PrefetchScalarGridSpec(
            num_scalar_prefetch=1, grid=(ROW_TILES, 4),
            in_specs=g_specs + r_specs, out_specs=r_specs + own_specs),
        compiler_params=_params(32, 2),
    )(pos, *g_list, *r1_list)
    return out[:n], out[n:]


def _final_sum_adamw(own_list, r2_list, w_list, m_list, v_list):
    n = len(w_list)

    def body(*refs):
        own, r2, w, m, v = (refs[k * n:(k + 1) * n] for k in range(5))
        outs = [refs[(5 + k) * n:(6 + k) * n] for k in range(4)]
        for a in range(n):
            g = own[a][...] + r2[a][0].astype(F32) + r2[a][1].astype(F32) + r2[a][2].astype(F32)
            delta, m2, v2 = _adamw_math(w[a][...], g, m[a][...], v[a][...])
            for o_ref, val in zip((outs[0][a], outs[1][a], outs[2][a], outs[3][a]), (g, delta, m2, v2)):
                o_ref[...] = val

    flat, wire, shapes = [], [], []
    for t in w_list:
        rows, width = t.shape
        tr = rows // ROW_TILES
        flat.append(pl.BlockSpec((tr, width), lambda i: (i, 0)))
        wire.append(pl.BlockSpec((3, tr, width), lambda i: (0, i, 0)))
        shapes.append(jax.ShapeDtypeStruct((rows, width), F32))
    out = pl.pallas_call(
        body, name="grad_sum_adamw", grid=(ROW_TILES,), out_shape=shapes * 4,
        in_specs=flat + wire + flat * 3, out_specs=flat * 4, compiler_params=_params(32),
    )(*own_list, *r2_list, *w_list, *m_list, *v_list)
    return [out[k * n:(k + 1) * n] for k in range(4)]


def _small_pair_sum(gs, r1s):
    def body(g_ref, r_ref, o_ref):
        o_ref[...] = g_ref[...] + r_ref[...]

    return pl.pallas_call(body, name="small_pair_sum", out_shape=jax.ShapeDtypeStruct(gs.shape, F32))(gs, r1s)


SMALL_NAMES = ("norm_0", "a_v_norm_0", "b_scale_0", "norm_1", "final_norm", "a_spatial_w_0", "a_spatial_b_0", "sink_1")
SMALL_VIEWS = ((8, LANES),) * 5 + ((4 * CHUNK, LANES), (4, LANES), (1, N_HEADS))
SMALL_ROW0 = (0, 8, 16, 24, 32, 40, 552, 560)
SMALL_ROWS = 568


def _small_sum_adamw(ps, r2s, pos, w_list, m_list, v_list):
    n = len(w_list)

    def body(pos_ref, p_ref, r_ref, *refs):
        stack = refs[-1]
        stack[0] = p_ref[...]
        stack[2] = r_ref[0]
        stack[1] = r_ref[1]
        stack[3] = r_ref[2]
        me = pos_ref[1]
        gtot = stack[me] + stack[me ^ 1]
        gtot = gtot + stack[me ^ 2]
        gtot = gtot + stack[me ^ 3]
        for a, ((rows, width), r0) in enumerate(zip(SMALL_VIEWS, SMALL_ROW0)):
            g = gtot[r0:r0 + rows, 0:width]
            delta, m2, v2 = _adamw_math(refs[a][...], g, refs[n + a][...], refs[2 * n + a][...])
            for k, val in enumerate((g, delta, m2, v2)):
                refs[(3 + k) * n + a][...] = val

    vm = pl.BlockSpec(memory_space=pltpu.VMEM)
    shapes = [jax.ShapeDtypeStruct(s, F32) for s in SMALL_VIEWS]
    out = pl.pallas_call(
        body, name="small_sum_adamw", out_shape=shapes * 4,
        in_specs=[pl.BlockSpec(memory_space=pltpu.SMEM), vm, vm] + [vm] * (3 * n), out_specs=[vm] * (4 * n),
        scratch_shapes=[pltpu.VMEM((4, SMALL_ROWS, LANES), F32)],
    )(pos, ps, r2s, *w_list, *m_list, *v_list)
    return [out[k * n:(k + 1) * n] for k in range(4)]


def _all_gather(blks):
    n = len(blks)

    def body(*refs):
        ins, outs = refs[:n], refs[n:2 * n]
        send_sems, recv_sems, local_sems = refs[2 * n:]
        x, y, c = lax.axis_index("x"), lax.axis_index("y"), lax.axis_index("c")
        me, sibling = (x, y, c), (x, y, 1 - c)
        chips = [(1 - x, y), (x, 1 - y), (1 - x, 1 - y)]

        def slot(a, px, py, pc):
            return outs[a].at[4 * px + 2 * py + pc]

        def copy(k, a, block, to, from_input=False):
            return pltpu.make_async_remote_copy(
                src_ref=ins[a] if from_input else slot(a, *block), dst_ref=slot(a, *block),
                send_sem=send_sems.at[k, a], recv_sem=recv_sems.at[k, a], device_id=to, device_id_type=MESH)

        mine = [pltpu.make_async_copy(ins[a], slot(a, *me), local_sems.at[a]) for a in range(n)]
        first = []
        for a in range(n):
            first.append(copy(0, a, me, sibling, from_input=True))
            first += [copy(1 + j, a, me, (*chip, c), from_input=True) for j, chip in enumerate(chips)]
        for cp in mine + first:
            cp.start()
        passed = []
        for j, chip in enumerate(chips):
            for a in range(n):
                copy(1 + j, a, (*chip, c), me).wait_recv()
                passed.append(copy(4 + j, a, (*chip, c), sibling))
                passed[-1].start()
        for a in range(n):
            copy(0, a, sibling, me).wait_recv()
        for j, chip in enumerate(chips):
            for a in range(n):
                copy(4 + j, a, (*chip, 1 - c), me).wait_recv()
        for cp in first + passed:
            cp.wait_send()
        for cp in mine:
            cp.wait()

    any_spec = pl.BlockSpec(memory_space=pl.ANY)
    return pl.pallas_call(
        body, name="weights_all_gather", out_shape=[jax.ShapeDtypeStruct((N_DEV,) + t.shape, t.dtype) for t in blks],
        in_specs=[any_spec] * n, out_specs=[any_spec] * n,
        scratch_shapes=[pltpu.SemaphoreType.DMA((7, n)), pltpu.SemaphoreType.DMA((7, n)), pltpu.SemaphoreType.DMA((n,))],
    )(*blks)


def _sibling_exchange(g_list, gs):
    n = len(g_list)

    def body(*refs):
        ins, outs = refs[:n + 1], refs[n + 1:2 * n + 2]
        send_sems, recv_sems = refs[2 * n + 2:]
        x, y, c = lax.axis_index("x"), lax.axis_index("y"), lax.axis_index("c")
        copies = [pltpu.make_async_remote_copy(
            src_ref=ins[a].at[1 - c] if a < n else ins[a], dst_ref=outs[a], send_sem=send_sems.at[a],
            recv_sem=recv_sems.at[a], device_id=(x, y, 1 - c), device_id_type=MESH) for a in range(n + 1)]
        for cp in copies:
            cp.start()
        for cp in copies:
            cp.wait()

    any_spec = pl.BlockSpec(memory_space=pl.ANY)
    out = pl.pallas_call(
        body, name="grad_sibling_exchange",
        out_shape=[jax.ShapeDtypeStruct(g.shape[1:], F32) for g in g_list] + [jax.ShapeDtypeStruct(gs.shape, F32)],
        in_specs=[any_spec] * (n + 1), out_specs=[any_spec] * (n + 1),
        scratch_shapes=[pltpu.SemaphoreType.DMA((n + 1,)), pltpu.SemaphoreType.DMA((n + 1,))],
    )(*g_list, gs)
    return out[:n], out[n]


def _chip_exchange(p_list, ps):
    n = len(p_list)

    def body(*refs):
        ins, outs = refs[:n + 1], refs[n + 1:2 * n + 2]
        send_sems, recv_sems = refs[2 * n + 2:]
        x, y, c = lax.axis_index("x"), lax.axis_index("y"), lax.axis_index("c")
        copies = []
        for j, (fx, fy) in enumerate(((1, 0), (0, 1), (1, 1))):
            tx = x ^ fx
            ty = y ^ fy
            for a in range(n + 1):
                copies.append(pltpu.make_async_remote_copy(
                    src_ref=ins[a].at[2 * tx + ty] if a < n else ins[a], dst_ref=outs[a].at[j],
                    send_sem=send_sems.at[j, a], recv_sem=recv_sems.at[j, a], device_id=(tx, ty, c),
                    device_id_type=MESH))
        for cp in copies:
            cp.start()
        for cp in copies:
            cp.wait()

    any_spec = pl.BlockSpec(memory_space=pl.ANY)
    out = pl.pallas_call(
        body, name="grad_chip_exchange",
        out_shape=[jax.ShapeDtypeStruct((3,) + p.shape[1:], p.dtype) for p in p_list]
        + [jax.ShapeDtypeStruct((3,) + ps.shape, F32)],
        in_specs=[any_spec] * (n + 1), out_specs=[any_spec] * (n + 1),
        scratch_shapes=[pltpu.SemaphoreType.DMA((3, n + 1)), pltpu.SemaphoreType.DMA((3, n + 1))],
    )(*p_list, ps)
    return out[:n], out[n]


def _shard_views(w_in_0, b_group_w_0, w_out_0, w_in_1, w_out_1):
    return [w_in_0, b_group_w_0.reshape(4 * 32, GDIM), w_out_0, w_in_1, w_out_1]


def _small_views(named):
    return [named[name].reshape(view) for name, view in zip(SMALL_NAMES, SMALL_VIEWS)]


def _pack_small_grads(named):
    rows = []
    for name, (r, w) in zip(SMALL_NAMES, SMALL_VIEWS):
        t = named[name].reshape(r, w)
        pad_r = -r % 8
        rows.append(jnp.pad(t, ((0, pad_r), (0, LANES - w))))
    return jnp.concatenate(rows, axis=0)


def _owner_blocks(t, axis):
    shape = t.shape
    t = t.reshape(shape[:axis] + (4, 2, shape[axis] // N_DEV) + shape[axis + 1:])
    t = jnp.moveaxis(t, (axis + 1, axis), (0, 1))
    return t.reshape(2, 4, -1, shape[-1] if axis != len(shape) - 1 else shape[-1] // N_DEV)


def kernel(x, norm_0, w_in_0, a_v_norm_0, a_spatial_w_0, a_spatial_b_0, b_group_w_0, b_scale_0, w_out_0, norm_1, w_in_1, sink_1, w_out_1, final_norm, loss_target, m_norm_0, m_w_in_0, m_a_v_norm_0, m_a_spatial_w_0, m_a_spatial_b_0, m_b_group_w_0, m_b_scale_0, m_w_out_0, m_norm_1, m_w_in_1, m_sink_1, m_w_out_1, m_final_norm, v_norm_0, v_w_in_0, v_a_v_norm_0, v_a_spatial_w_0, v_a_spatial_b_0, v_b_group_w_0, v_b_scale_0, v_w_out_0, v_norm_1, v_w_in_1, v_sink_1, v_w_out_1, v_final_norm):
    seq = x.shape[1]
    xs = x.reshape(seq, D)
    tgt = loss_target.reshape(seq, D)
    ax, ay, ac = lax.axis_index("x"), lax.axis_index("y"), lax.axis_index("c")
    pos = jnp.stack([ac, 2 * ax + ay]).astype(jnp.int32)

    shards = _shard_views(w_in_0, b_group_w_0, w_out_0, w_in_1, w_out_1)
    gathered = _all_gather(_cast_shards(shards))
    win0 = gathered[0].transpose(1, 0, 2).reshape(D, MIX0_IN)
    wg = gathered[1].reshape(N_DEV, 4, 32, GDIM).transpose(1, 0, 2, 3).reshape(4, GDIM, GDIM)
    wout0 = gathered[2].reshape(2 * D, D)
    win1 = gathered[3].transpose(1, 0, 2).reshape(D, MIX1_IN)
    wout1 = gathered[4].reshape(D, D)
    loss_part, grad_x, d_win0, d_wg, d_wout0, d_win1, d_wout1, small_grads = _local_step(
        xs, tgt, win0, wg, wout0, win1, wout1, norm_0, a_v_norm_0, a_spatial_w_0, a_spatial_b_0, b_scale_0, norm_1, sink_1,
        final_norm)

    g_list = [_owner_blocks(d_win0, 1), _owner_blocks(d_wg, 1), _owner_blocks(d_wout0, 0), _owner_blocks(d_win1, 1),
              _owner_blocks(d_wout1, 0)]
    gs = _pack_small_grads(small_grads)
    r1, r1s = _sibling_exchange(g_list, gs)
    p_wire, p_own = _pair_sum(g_list, r1, pos)
    ps = _small_pair_sum(gs, r1s)
    r2, r2s = _chip_exchange(p_wire, ps)
    big = _final_sum_adamw(p_own, r2, shards, _shard_views(m_w_in_0, m_b_group_w_0, m_w_out_0, m_w_in_1, m_w_out_1),
                           _shard_views(v_w_in_0, v_b_group_w_0, v_w_out_0, v_w_in_1, v_w_out_1))
    weights = dict(norm_0=norm_0, a_v_norm_0=a_v_norm_0, a_spatial_w_0=a_spatial_w_0, a_spatial_b_0=a_spatial_b_0,
                   b_scale_0=b_scale_0, norm_1=norm_1, sink_1=sink_1, final_norm=final_norm)
    m_small = dict(norm_0=m_norm_0, a_v_norm_0=m_a_v_norm_0, a_spatial_w_0=m_a_spatial_w_0, a_spatial_b_0=m_a_spatial_b_0,
                   b_scale_0=m_b_scale_0, norm_1=m_norm_1, sink_1=m_sink_1, final_norm=m_final_norm)
    v_small = dict(norm_0=v_norm_0, a_v_norm_0=v_a_v_norm_0, a_spatial_w_0=v_a_spatial_w_0, a_spatial_b_0=v_a_spatial_b_0,
                   b_scale_0=v_b_scale_0, norm_1=v_norm_1, sink_1=v_sink_1, final_norm=v_final_norm)
    small = _small_sum_adamw(ps, r2s, pos, _small_views(weights), _small_views(m_small), _small_views(v_small))

    def in_order(kind):
        b = [b_.reshape(s_.shape) for b_, s_ in zip(big[kind], (w_in_0, b_group_w_0, w_out_0, w_in_1, w_out_1))]
        s = {name: t.reshape(weights[name].shape) for name, t in zip(SMALL_NAMES, small[kind])}
        return [s["norm_0"], b[0], s["a_v_norm_0"], s["a_spatial_w_0"], s["a_spatial_b_0"], b[1], s["b_scale_0"], b[2],
                s["norm_1"], b[3], s["sink_1"], b[4], s["final_norm"]]

    loss = lax.psum(loss_part[0, 0], ("x", "y", "c"))
    return (loss, grad_x.reshape(1, seq, D), *in_order(0), *in_order(1), *in_order(2), *in_order(3))


def _local_step(xs, tgt, win0, wg, wout0, win1, wout1, norm_0, a_v_norm_0, a_spatial_w_0, a_spatial_b_0, b_scale_0, norm_1,
                sink_1, final_norm):
    seq = xs.shape[0]
    win0_t, wout0_t, win1_t = win0.T, wout0.T, win1.T
    wg_t = jnp.swapaxes(wg, 1, 2)
    ws = a_spatial_w_0.astype(BF16)
    ws_t = jnp.swapaxes(ws, 1, 2)
    bias = jnp.repeat(a_spatial_b_0.T, GDIM, axis=1)
    g0, gv, scale, g1, gf = (t.reshape(1, D) for t in (norm_0, a_v_norm_0, b_scale_0, norm_1, final_norm))
    cos_t, sin_t = _rope_tables_t(seq)

    za, bx, bg, h0_t = _l0_in_proj(xs, g0, win0)
    x1 = _l0_mix_fwd(za, bx, bg, xs, ws, bias, gv, wg, scale, wout0)
    qt, kt, vt, gatet, h1_t = _l1_in_proj(x1, g1, win1_t, cos_t, sin_t)
    dx2, dx2b, y_t, att, lse, loss_part, d_gf = _l1_attn_fwd(qt, kt, vt, gatet, x1, tgt, wout1, gf, sink_1)

    d_wout1 = _dw_matmul(y_t, dx2b, "dw_out_1")
    dq_r, dgate, dk_pad, dv_pad, d_sink = _l1_attn_bwd(dx2b, wout1, qt, kt, vt, gatet, att, lse, sink_1)
    dk_r = dk_pad[:, BLK:BLK + seq]
    dv = dv_pad[:, BLK:BLK + seq]
    dx1, dx1b, dz1_t, d_g1 = _l1_in_proj_bwd(dq_r, dk_r, dv, dgate, cos_t, sin_t, win1, x1, g1, dx2)
    d_win1 = _dw_matmul(h1_t, dz1_t, "dw_in_1", b_transposed=True, tn=1280)

    dz0, dp, cat_t, d_ws, _, d_gv, d_scale, d_wg, d_b = _l0_mix_bwd(
        dx1b, wout0_t, za, bx, bg, ws, ws_t, bias, gv, wg, wg_t, scale)
    d_wout0 = _dw_matmul(cat_t, dx1b, "dw_out_0")
    dz0 = _l0_pool_bwd(dp, dz0)
    grad_x, d_g0 = _l0_in_proj_bwd(dz0, win0_t, xs, g0, dx1)
    d_win0 = _dw_matmul(h0_t, dz0, "dw_in_0")

    small_grads = dict(norm_0=d_g0, a_v_norm_0=d_gv, a_spatial_w_0=d_ws, a_spatial_b_0=d_b.reshape(4, 8, CHUNK)[:, 0, :],
                       b_scale_0=d_scale, norm_1=d_g1, sink_1=d_sink[:, 0], final_norm=d_gf)
    return loss_part, grad_x, d_win0, d_wg, d_wout0, d_win1, d_wout1, small_grads
```

```python
import jax
import jax.numpy as jnp
from jax import lax
from jax.experimental import pallas as pl
from jax.experimental.pallas import tpu as pltpu
from jax.experimental.pallas import tpu_sc as plsc

F32 = jnp.float32
BF16 = jnp.bfloat16

D = 1024
EPS = 1e-6
NEG_INF = -1e30
CHUNK = 128
A_GROUPS = 4
POOL_WINDOWS = (2, 4, 8, 16)
POOL_HALO = 8
GDIM = 256
N_HEADS = 16
N_KV = 4
GQA = 4
HD = 64
BLK = 128
ROT_HALF = 8
ROPE_THETA = 500000.0
SCALE = HD ** -0.5
MIX0_IN = 5 * D
MIX1_IN = 2560
KV_W = N_KV * HD
Q_ROWS, K_ROWS, V_ROWS, G_ROWS = (0, D), (D, D + KV_W), (D + KV_W, D + 2 * KV_W), (D + 2 * KV_W, MIX1_IN)
TQ = 512

ADAM_LR = 0.001
ADAM_B1 = 0.9
ADAM_B2 = 0.999
ADAM_EPS = 1e-08
ADAM_WD = 0.01
ADAM_STEP = 10

N_DEV = 8
LANES = 128
MIB = 2 ** 20
MESH = pl.DeviceIdType.MESH


def _params(limit_mib, n_axes=1):
    return pltpu.CompilerParams(vmem_limit_bytes=limit_mib * MIB, dimension_semantics=("arbitrary",) * n_axes)


def _resident(shape):
    nd = len(shape)
    return pl.BlockSpec(shape, lambda *_: (0,) * nd, pipeline_mode=pl.Buffered(1))


def _gelu(x):
    k = 0.7978845608028654
    return 0.5 * x * (1.0 + jnp.tanh(k * (x + 0.044715 * x * x * x)))


def _gelu_and_grad(x):
    k = 0.7978845608028654
    x2 = x * x
    t = jnp.tanh(k * (x + 0.044715 * x * x2))
    g = 0.5 * x * (1.0 + t)
    dg = 0.5 * (1.0 + t) + 0.5 * x * (1.0 - t * t) * (k * (1.0 + 3.0 * 0.044715 * x2))
    return g, dg


def _silu_and_grad(x):
    s = jax.nn.sigmoid(x)
    return x * s, s * (1.0 + x * (1.0 - s))


def _nt(a, b):
    return lax.dot_general(a, b, (((1,), (1,)), ((), ())), preferred_element_type=F32)


def _tn(a, b):
    return lax.dot_general(a, b, (((0,), (0,)), ((), ())), preferred_element_type=F32)


def _mm(a, b):
    return jnp.dot(a, b, preferred_element_type=F32)


def _rope_tables_t(seq):
    inv = ROPE_THETA ** (-jnp.arange(0, 2 * ROT_HALF, 2, dtype=F32) / (2 * ROT_HALF))
    ang = inv[:, None] * jnp.arange(seq, dtype=F32)[None, :]
    return jnp.cos(ang), jnp.sin(ang)


def _rope_t(z, c, s, n_heads, sign):
    parts = []
    for h in range(n_heads):
        b = h * HD
        x1, x2 = z[b:b + ROT_HALF], z[b + ROT_HALF:b + 2 * ROT_HALF]
        if sign > 0:
            parts += [x1 * c - x2 * s, x2 * c + x1 * s]
        else:
            parts += [x1 * c + x2 * s, x2 * c - x1 * s]
        parts.append(z[b + 2 * ROT_HALF:b + HD])
    return jnp.concatenate(parts, axis=0)


def _l0_in_proj(x, g0, w):
    seq = x.shape[0]
    tm = 512

    def body(x_ref, g_ref, w_ref, za_ref, bx_ref, bg_ref, ht_ref):
        xf = x_ref[...]
        r = lax.rsqrt(jnp.mean(xf * xf, axis=1, keepdims=True) + EPS)
        h = (xf * r * g_ref[...]).astype(BF16)
        ht_ref[...] = h.T
        for j in range(3):
            za_ref[:, j * D:(j + 1) * D] = _mm(h, w_ref[:, j * D:(j + 1) * D]).astype(BF16)
        bx_ref[...] = _mm(h, w_ref[:, 3 * D:4 * D])
        bg_ref[...] = _mm(h, w_ref[:, 4 * D:5 * D]).astype(BF16)

    return pl.pallas_call(
        body, grid=(seq // tm,), name="l0_in_proj",
        out_shape=(jax.ShapeDtypeStruct((seq, 3 * D), BF16), jax.ShapeDtypeStruct((seq, D), F32),
                   jax.ShapeDtypeStruct((seq, D), BF16), jax.ShapeDtypeStruct((D, seq), BF16)),
        in_specs=[pl.BlockSpec((tm, D), lambda i: (i, 0)), _resident((1, D)), _resident((D, MIX0_IN))],
        out_specs=(pl.BlockSpec((tm, 3 * D), lambda i: (i, 0)), pl.BlockSpec((tm, D), lambda i: (i, 0)),
                   pl.BlockSpec((tm, D), lambda i: (i, 0)), pl.BlockSpec((D, tm), lambda i: (0, i))),
        compiler_params=_params(48),
    )(x, g0, w)


def _fill_halo(ext_ref, cur, prev_ref, next_ref, i, n_tiles, ts):
    ext_ref[pl.ds(0, POOL_HALO), :] = jnp.where(i > 0, prev_ref[...], 0.0)
    ext_ref[pl.ds(POOL_HALO, ts), :] = cur
    ext_ref[pl.ds(POOL_HALO + ts, POOL_HALO), :] = jnp.where(i < n_tiles - 1, next_ref[...], 0.0)


def _pool_forward(xe_ref, ts, t0, seq):
    tg = t0 + lax.broadcasted_iota(jnp.int32, (ts, 1), 0)
    outs = []
    for gi, w in enumerate(POOL_WINDOWS):
        hw = w // 2
        cols = slice(gi * GDIM, (gi + 1) * GDIM)
        acc = xe_ref[pl.ds(POOL_HALO - hw, ts), cols]
        for k in range(-hw + 1, hw):
            acc = acc + xe_ref[pl.ds(POOL_HALO + k, ts), cols]
        cnt = (jnp.minimum(tg + hw, seq) - jnp.maximum(tg - hw, 0)).astype(F32)
        outs.append(acc / cnt - xe_ref[pl.ds(POOL_HALO, ts), cols])
    return jnp.concatenate(outs, axis=1)


def _spatial_mix(ws_ref, vnb, bias, ts):
    rows = []
    for c in range(ts // CHUNK):
        vc = vnb[c * CHUNK:(c + 1) * CHUNK, :]
        rows.append(jnp.concatenate(
            [_mm(ws_ref[h], vc[:, h * GDIM:(h + 1) * GDIM]) for h in range(A_GROUPS)], axis=1) + bias)
    return jnp.concatenate(rows, axis=0)


def _halo_specs(ts, seq, width):
    per = ts // POOL_HALO
    last = seq // POOL_HALO - 1
    prev = pl.BlockSpec((POOL_HALO, width), lambda i: (jnp.maximum(i * per - 1, 0), 0))
    nxt = pl.BlockSpec((POOL_HALO, width), lambda i: (jnp.minimum((i + 1) * per, last), 0))
    return prev, nxt


def _l0_mix_fwd(za, bx, bg, x, ws, bias, gv, wg, scale, wout):
    seq = x.shape[0]
    ts = 512
    n_tiles = seq // ts

    def body(za_ref, bx_ref, bxp_ref, bxn_ref, bg_ref, x_ref, ws_ref, bias_ref, gv_ref, wg_ref, sc_ref, wo_ref,
             x1_ref, xe_ref):
        i = pl.program_id(0)
        u = _gelu(za_ref[:, 0:D].astype(F32))
        vg = _gelu(za_ref[:, D:2 * D].astype(F32))
        rv = lax.rsqrt(jnp.mean(vg * vg, axis=1, keepdims=True) + EPS)
        vnb = (vg * rv * gv_ref[...]).astype(BF16)
        mixed = _spatial_mix(ws_ref, vnb, bias_ref[...], ts)
        ag = za_ref[:, 2 * D:3 * D].astype(F32)
        ya = (u * mixed * (ag * jax.nn.sigmoid(ag))).astype(BF16)

        _fill_halo(xe_ref, bx_ref[...], bxp_ref, bxn_ref, i, n_tiles, ts)
        pb = _pool_forward(xe_ref, ts, i * ts, seq).astype(BF16)
        y = jnp.concatenate([_mm(pb[:, g * GDIM:(g + 1) * GDIM], wg_ref[g]) for g in range(4)], axis=1) * sc_ref[...]
        bgf = bg_ref[...].astype(F32)
        yb = (y * (bgf * jax.nn.sigmoid(bgf))).astype(BF16)
        x1_ref[...] = x_ref[...] + _mm(ya, wo_ref[0:D, :]) + _mm(yb, wo_ref[D:2 * D, :])

    prev, nxt = _halo_specs(ts, seq, D)
    row = lambda w: pl.BlockSpec((ts, w), lambda i: (i, 0))
    return pl.pallas_call(
        body, grid=(n_tiles,), name="l0_mix_fwd",
        out_shape=jax.ShapeDtypeStruct((seq, D), F32),
        in_specs=[row(3 * D), row(D), prev, nxt, row(D), row(D), _resident((4, CHUNK, CHUNK)), _resident((CHUNK, D)),
                  _resident((1, D)), _resident((4, GDIM, GDIM)), _resident((1, D)), _resident((2 * D, D))],
        out_specs=row(D),
        scratch_shapes=[pltpu.VMEM((ts + 2 * POOL_HALO, D), F32)],
        compiler_params=_params(56),
    )(za, bx, bx, bx, bg, x, ws, bias, gv, wg, scale, wout)


def _l1_in_proj(x1, g1, w_t, cos_t, sin_t):
    seq = x1.shape[0]
    tm = 512

    def body(x_ref, g_ref, wt_ref, c_ref, s_ref, q_ref, k_ref, v_ref, gate_ref, ht_ref):
        xf = x_ref[...]
        r = lax.rsqrt(jnp.mean(xf * xf, axis=1, keepdims=True) + EPS)
        ht = (xf * r * g_ref[...]).astype(BF16).T
        ht_ref[...] = ht
        c, s = c_ref[...], s_ref[...]
        q_ref[...] = _rope_t(_mm(wt_ref[Q_ROWS[0]:Q_ROWS[1], :], ht), c, s, N_HEADS, 1).astype(BF16)
        k_ref[...] = _rope_t(_mm(wt_ref[K_ROWS[0]:K_ROWS[1], :], ht), c, s, N_KV, 1).astype(BF16)
        v_ref[...] = _mm(wt_ref[V_ROWS[0]:V_ROWS[1], :], ht).astype(BF16)
        gate_ref[...] = _mm(wt_ref[G_ROWS[0]:G_ROWS[1], :], ht).astype(BF16)

    col = lambda rows: pl.BlockSpec((rows, tm), lambda i: (0, i))
    return pl.pallas_call(
        body, grid=(seq // tm,), name="l1_in_proj",
        out_shape=(jax.ShapeDtypeStruct((D, seq), BF16), jax.ShapeDtypeStruct((KV_W, seq), BF16),
                   jax.ShapeDtypeStruct((KV_W, seq), BF16), jax.ShapeDtypeStruct((D, seq), BF16),
                   jax.ShapeDtypeStruct((D, seq), BF16)),
        in_specs=[pl.BlockSpec((tm, D), lambda i: (i, 0)), _resident((1, D)), _resident((MIX1_IN, D)), col(ROT_HALF),
                  col(ROT_HALF)],
        out_specs=(col(D), col(KV_W), col(KV_W), col(D), col(D)),
        compiler_params=_params(48),
    )(x1, g1, w_t, cos_t, sin_t)


def _band_specs_t(nb, clamp_i):
    per = TQ // BLK
    prev = pl.BlockSpec((KV_W, BLK), lambda i: (0, jnp.maximum(clamp_i(i) * per - 1, 0)))
    cur = pl.BlockSpec((KV_W, TQ), lambda i: (0, clamp_i(i)))
    nxt = pl.BlockSpec((KV_W, BLK), lambda i: (0, jnp.minimum((clamp_i(i) + 1) * per, nb - 1)))
    return [prev, cur, nxt]


def _fill_band(buf, p_ref, c_ref, n_ref):
    buf[:, 0:BLK] = p_ref[...]
    buf[:, BLK:BLK + TQ] = c_ref[...]
    buf[:, BLK + TQ:2 * BLK + TQ] = n_ref[...]


def _band_bias_t(n, nb):
    c = lax.broadcasted_iota(jnp.int32, (3 * BLK, BLK), 0)
    r = lax.broadcasted_iota(jnp.int32, (3 * BLK, BLK), 1)
    ok = (c >= r) & (c <= r + 2 * BLK) & ((c >= BLK) | (n > 0)) & ((c < 2 * BLK) | (n < nb - 1))
    bias = jnp.where(ok, 0.0, NEG_INF).astype(F32)
    return jnp.concatenate([bias] * GQA, axis=1)


def _heads_t(ref, kv, c0):
    return jnp.concatenate([ref[(kv * GQA + g) * HD:(kv * GQA + g + 1) * HD, c0:c0 + BLK] for g in range(GQA)], axis=1)


def _row4(ref, kv, c0):
    return jnp.concatenate([ref[kv * GQA + g:kv * GQA + g + 1, c0:c0 + BLK] for g in range(GQA)], axis=1)


def _sink_row(sink_ref, kv):
    return jnp.concatenate([jnp.full((1, BLK), sink_ref[kv * GQA + g], F32) for g in range(GQA)], axis=1)


def _l1_attn_fwd(qt, kt, vt, gatet, x1, tgt, wout, gf, sink):
    seq = x1.shape[0]
    nq, nb = seq // TQ, seq // BLK

    def body(q_ref, gate_ref, kp_ref, k_ref, kn_ref, vp_ref, v_ref, vn_ref, x1_ref, tgt_ref, wo_ref, gf_ref, sink_ref,
             dx2_ref, dx2b_ref, yt_ref, att_ref, lse_ref, loss_ref, dgf_ref, kbuf, vbuf, att_scr):
        i = pl.program_id(0)

        @pl.when(i == 0)
        def _():
            loss_ref[...] = jnp.zeros_like(loss_ref)
            dgf_ref[...] = jnp.zeros_like(dgf_ref)

        _fill_band(kbuf, kp_ref, k_ref, kn_ref)
        _fill_band(vbuf, vp_ref, v_ref, vn_ref)
        for j in range(TQ // BLK):
            c0 = j * BLK
            bias = _band_bias_t(i * (TQ // BLK) + j, nb)
            lse_rows = []
            for kv in range(N_KV):
                rows = slice(kv * HD, (kv + 1) * HD)
                q4 = _heads_t(q_ref, kv, c0)
                st = _tn(kbuf[rows, c0:c0 + 3 * BLK], q4) * SCALE + bias
                sk = _sink_row(sink_ref, kv)
                m = jnp.maximum(jnp.max(st, axis=0, keepdims=True), sk)
                p = jnp.exp(st - m)
                den = jnp.sum(p, axis=0, keepdims=True) + jnp.exp(sk - m)
                ot = _mm(vbuf[rows, c0:c0 + 3 * BLK], p.astype(BF16)) / den
                lse = m + jnp.log(den)
                for g in range(GQA):
                    h = kv * GQA + g
                    att_scr[h * HD:(h + 1) * HD, c0:c0 + BLK] = ot[:, g * BLK:(g + 1) * BLK]
                    lse_rows.append(lse[:, g * BLK:(g + 1) * BLK])
            lse_ref[:, c0:c0 + BLK] = jnp.concatenate(lse_rows, axis=0)

        att = att_scr[...]
        gate = gate_ref[...].astype(F32)
        yt = (att * (gate * jax.nn.sigmoid(gate))).astype(BF16)
        yt_ref[...] = yt
        att_ref[...] = att.astype(BF16)
        x2 = x1_ref[...] + _mm(yt.T, wo_ref[...])
        r = lax.rsqrt(jnp.mean(x2 * x2, axis=1, keepdims=True) + EPS)
        xn = x2 * r
        diff = xn * gf_ref[...] - tgt_ref[...]
        loss_ref[...] += 0.5 * jnp.sum(jnp.mean(diff * diff, axis=1, keepdims=True), axis=0, keepdims=True)
        dout = diff * (1.0 / D)
        dgf_ref[...] += jnp.sum(dout * xn, axis=0, keepdims=True)
        dxn = dout * gf_ref[...]
        dx2 = r * (dxn - xn * jnp.mean(dxn * xn, axis=1, keepdims=True))
        dx2_ref[...] = dx2
        dx2b_ref[...] = dx2.astype(BF16)

    ident = lambda i: i
    row = pl.BlockSpec((TQ, D), lambda i: (i, 0))
    col = lambda rows: pl.BlockSpec((rows, TQ), lambda i: (0, i))
    return pl.pallas_call(
        body, grid=(nq,), name="l1_attn_fwd",
        out_shape=(jax.ShapeDtypeStruct((seq, D), F32), jax.ShapeDtypeStruct((seq, D), BF16),
                   jax.ShapeDtypeStruct((D, seq), BF16), jax.ShapeDtypeStruct((D, seq), BF16),
                   jax.ShapeDtypeStruct((N_HEADS, seq), F32), jax.ShapeDtypeStruct((1, 1), F32),
                   jax.ShapeDtypeStruct((1, D), F32)),
        in_specs=[col(D), col(D)] + _band_specs_t(nb, ident) + _band_specs_t(nb, ident) + [
            row, row, _resident((D, D)), _resident((1, D)), pl.BlockSpec(memory_space=pltpu.SMEM)],
        out_specs=(row, row, col(D), col(D), col(N_HEADS), pl.BlockSpec((1, 1), lambda i: (0, 0)),
                   pl.BlockSpec((1, D), lambda i: (0, 0))),
        scratch_shapes=[pltpu.VMEM((KV_W, TQ + 2 * BLK), BF16), pltpu.VMEM((KV_W, TQ + 2 * BLK), BF16),
                        pltpu.VMEM((D, TQ), F32)],
        compiler_params=_params(56),
    )(qt, gatet, kt, kt, kt, vt, vt, vt, x1, tgt, wout, gf, sink)


def _l1_attn_bwd(dx2b, wout, qt, kt, vt, gatet, att, lse, sink):
    seq = dx2b.shape[0]
    nq, nb = seq // TQ, seq // BLK

    def body(dx_ref, wo_ref, q_ref, gate_ref, kp_ref, k_ref, kn_ref, vp_ref, v_ref, vn_ref, att_ref, lse_ref, sink_ref,
             dq_ref, dgate_ref, dk_ref, dv_ref, dsink_ref, kbuf, vbuf, dkacc, dvacc, dat_scr, delta_scr, dsacc):
        i = pl.program_id(0)

        @pl.when(i == 0)
        def _():
            dkacc[...] = jnp.zeros_like(dkacc)
            dvacc[...] = jnp.zeros_like(dvacc)
            dsacc[...] = jnp.zeros_like(dsacc)

        @pl.when(i > 0)
        def _():
            for acc in (dkacc, dvacc):
                acc[:, 0:2 * BLK] = acc[:, TQ:TQ + 2 * BLK]
                acc[:, 2 * BLK:2 * BLK + TQ] = jnp.zeros((KV_W, TQ), F32)

        @pl.when(i < nq)
        def _():
            _fill_band(kbuf, kp_ref, k_ref, kn_ref)
            _fill_band(vbuf, vp_ref, v_ref, vn_ref)
            dyt = _nt(wo_ref[...], dx_ref[...])
            sg, dsg = _silu_and_grad(gate_ref[...].astype(F32))
            attf = att_ref[...].astype(F32)
            dat = dyt * sg
            dat_scr[...] = dat.astype(BF16)
            dgate_ref[...] = (dyt * attf * dsg).astype(BF16)
            dl = dat * attf
            delta_scr[...] = jnp.concatenate(
                [jnp.sum(dl[h * HD:(h + 1) * HD, :], axis=0, keepdims=True) for h in range(N_HEADS)], axis=0)
            for j in range(TQ // BLK):
                c0 = j * BLK
                bias = _band_bias_t(i * (TQ // BLK) + j, nb)
                for kv in range(N_KV):
                    rows = slice(kv * HD, (kv + 1) * HD)
                    q4 = _heads_t(q_ref, kv, c0)
                    do4 = _heads_t(dat_scr, kv, c0)
                    lse4 = _row4(lse_ref, kv, c0)
                    delta4 = _row4(delta_scr, kv, c0)
                    kth = kbuf[rows, c0:c0 + 3 * BLK]
                    vth = vbuf[rows, c0:c0 + 3 * BLK]
                    p = jnp.exp(_tn(kth, q4) * SCALE + bias - lse4)
                    dp = _tn(vth, do4)
                    ds = (p * (dp - delta4) * SCALE).astype(BF16)
                    dq4 = _mm(kth, ds)
                    dkacc[rows, c0:c0 + 3 * BLK] += _nt(q4, ds)
                    dvacc[rows, c0:c0 + 3 * BLK] += _nt(do4, p.astype(BF16))
                    dsk = -jnp.exp(_sink_row(sink_ref, kv) - lse4) * delta4
                    for g in range(GQA):
                        h = kv * GQA + g
                        dq_ref[h * HD:(h + 1) * HD, c0:c0 + BLK] = dq4[:, g * BLK:(g + 1) * BLK].astype(BF16)
                        dsacc[h:h + 1, :] += dsk[:, g * BLK:(g + 1) * BLK]

        dk_ref[...] = dkacc[:, 0:TQ].astype(BF16)
        dv_ref[...] = dvacc[:, 0:TQ].astype(BF16)

        @pl.when(i == nq)
        def _():
            dsink_ref[...] = jnp.broadcast_to(jnp.sum(dsacc[...], axis=1, keepdims=True), (N_HEADS, LANES))

    clamp = lambda i: jnp.minimum(i, nq - 1)
    row = pl.BlockSpec((TQ, D), lambda i: (clamp(i), 0))
    col = lambda rows: pl.BlockSpec((rows, TQ), lambda i: (0, clamp(i)))
    pad = pl.BlockSpec((KV_W, TQ), lambda i: (0, i))
    return pl.pallas_call(
        body, grid=(nq + 1,), name="l1_attn_bwd",
        out_shape=(jax.ShapeDtypeStruct((D, seq), BF16), jax.ShapeDtypeStruct((D, seq), BF16),
                   jax.ShapeDtypeStruct((KV_W, seq + TQ), BF16), jax.ShapeDtypeStruct((KV_W, seq + TQ), BF16),
                   jax.ShapeDtypeStruct((N_HEADS, LANES), F32)),
        in_specs=[row, _resident((D, D)), col(D), col(D)] + _band_specs_t(nb, clamp) + _band_specs_t(nb, clamp) + [
            col(D), col(N_HEADS), pl.BlockSpec(memory_space=pltpu.SMEM)],
        out_specs=(col(D), col(D), pad, pad, pl.BlockSpec((N_HEADS, LANES), lambda i: (0, 0))),
        scratch_shapes=[pltpu.VMEM((KV_W, TQ + 2 * BLK), BF16), pltpu.VMEM((KV_W, TQ + 2 * BLK), BF16),
                        pltpu.VMEM((KV_W, TQ + 2 * BLK), F32), pltpu.VMEM((KV_W, TQ + 2 * BLK), F32),
                        pltpu.VMEM((D, TQ), BF16), pltpu.VMEM((N_HEADS, TQ), F32), pltpu.VMEM((N_HEADS, LANES), F32)],
        compiler_params=_params(56),
    )(dx2b, wout, qt, gatet, kt, kt, kt, vt, vt, vt, att, lse, sink)


def _l1_in_proj_bwd(dq_r, dk_r, dv, dgate, cos_t, sin_t, w, x1, g1, dx2):
    seq = x1.shape[0]
    tm = 512

    def body(dq_ref, dk_ref, dv_ref, dg_ref, c_ref, s_ref, w_ref, x_ref, g_ref, dres_ref,
             dx_ref, dxb_ref, dz_ref, dn_ref):
        @pl.when(pl.program_id(0) == 0)
        def _():
            dn_ref[...] = jnp.zeros_like(dn_ref)

        c, s = c_ref[...], s_ref[...]
        dq = _rope_t(dq_ref[...].astype(F32), c, s, N_HEADS, -1).astype(BF16)
        dk = _rope_t(dk_ref[...].astype(F32), c, s, N_KV, -1).astype(BF16)
        dz = jnp.concatenate([dq, dk, dv_ref[...], dg_ref[...]], axis=0)
        dz_ref[...] = dz
        half = MIX1_IN // 2
        dh = (_mm(w_ref[:, 0:half], dz[0:half]) + _mm(w_ref[:, half:MIX1_IN], dz[half:MIX1_IN])).T
        xf = x_ref[...]
        r = lax.rsqrt(jnp.mean(xf * xf, axis=1, keepdims=True) + EPS)
        xn = xf * r
        dn_ref[...] += jnp.sum(dh * xn, axis=0, keepdims=True)
        dxn = dh * g_ref[...]
        dx = dres_ref[...] + r * (dxn - xn * jnp.mean(dxn * xn, axis=1, keepdims=True))
        dx_ref[...] = dx
        dxb_ref[...] = dx.astype(BF16)

    row = pl.BlockSpec((tm, D), lambda i: (i, 0))
    col = lambda rows: pl.BlockSpec((rows, tm), lambda i: (0, i))
    return pl.pallas_call(
        body, grid=(seq // tm,), name="l1_in_proj_bwd",
        out_shape=(jax.ShapeDtypeStruct((seq, D), F32), jax.ShapeDtypeStruct((seq, D), BF16),
                   jax.ShapeDtypeStruct((MIX1_IN, seq), BF16), jax.ShapeDtypeStruct((1, D), F32)),
        in_specs=[col(D), col(KV_W), col(KV_W), col(D), col(ROT_HALF), col(ROT_HALF), _resident((D, MIX1_IN)), row,
                  _resident((1, D)), row],
        out_specs=(row, row, col(MIX1_IN), pl.BlockSpec((1, D), lambda i: (0, 0))),
        compiler_params=_params(48),
    )(dq_r, dk_r, dv, dgate, cos_t, sin_t, w, x1, g1, dx2)


def _l0_mix_bwd(dx1b, wout_t, za, bx, bg, ws, ws_t, bias, gv, wg, wg_t, scale):
    seq = dx1b.shape[0]
    ts = 256
    n_tiles = seq // ts

    def body(dx_ref, wot_ref, za_ref, bx_ref, bxp_ref, bxn_ref, bg_ref, ws_ref, wst_ref, bias_ref, gv_ref, wg_ref,
             wgt_ref, sc_ref,
             dz_ref, dp_ref, catt_ref, dws_ref, dbias_ref, dgv_ref, dsc_ref, dwg_ref, db_ref, xe_ref):
        i = pl.program_id(0)

        @pl.when(i == 0)
        def _():
            for r_ in (dws_ref, dbias_ref, dgv_ref, dsc_ref, dwg_ref, db_ref):
                r_[...] = jnp.zeros_like(r_)

        dxb = dx_ref[...]
        dya = _mm(dxb, wot_ref[:, 0:D])
        dyb = _mm(dxb, wot_ref[:, D:2 * D])

        u, du = _gelu_and_grad(za_ref[:, 0:D].astype(F32))
        vg, dvg_dz = _gelu_and_grad(za_ref[:, D:2 * D].astype(F32))
        rv = lax.rsqrt(jnp.mean(vg * vg, axis=1, keepdims=True) + EPS)
        vnorm = vg * rv
        gvw = gv_ref[...]
        vnb = (vnorm * gvw).astype(BF16)
        mixed = _spatial_mix(ws_ref, vnb, bias_ref[...], ts)
        sga, dsga = _silu_and_grad(za_ref[:, 2 * D:3 * D].astype(F32))
        um = u * mixed
        ya = (um * sga).astype(BF16)
        t = dya * sga
        dz_ref[:, 0:D] = (t * mixed * du).astype(BF16)
        dz_ref[:, 2 * D:3 * D] = (dya * um * dsga).astype(BF16)
        dmixed = t * u
        dmb = dmixed.astype(BF16)
        dvn_rows = []
        dbias = jnp.zeros((CHUNK, D), F32)
        for c in range(ts // CHUNK):
            rows = slice(c * CHUNK, (c + 1) * CHUNK)
            dbias = dbias + dmixed[rows, :]
            parts = []
            for h in range(A_GROUPS):
                cols = slice(h * GDIM, (h + 1) * GDIM)
                dws_ref[h] += _nt(dmb[rows, cols], vnb[rows, cols])
                parts.append(_mm(wst_ref[h], dmb[rows, cols]))
            dvn_rows.append(jnp.concatenate(parts, axis=1))
        dbias_ref[...] += dbias
        dvn = jnp.concatenate(dvn_rows, axis=0)
        dgv_ref[...] += jnp.sum(dvn * vnorm, axis=0, keepdims=True)
        dxn = dvn * gvw
        dvg = rv * (dxn - vnorm * jnp.mean(dxn * vnorm, axis=1, keepdims=True))
        dz_ref[:, D:2 * D] = (dvg * dvg_dz).astype(BF16)

        _fill_halo(xe_ref, bx_ref[...], bxp_ref, bxn_ref, i, n_tiles, ts)
        pb = _pool_forward(xe_ref, ts, i * ts, seq).astype(BF16)
        ypre = jnp.concatenate([_mm(pb[:, g * GDIM:(g + 1) * GDIM], wg_ref[g]) for g in range(4)], axis=1)
        sc = sc_ref[...]
        y = ypre * sc
        sgb, dsgb = _silu_and_grad(bg_ref[...].astype(F32))
        yb = (y * sgb).astype(BF16)
        dy_b = dyb * sgb
        dz_ref[:, 3 * D:4 * D] = jnp.zeros((ts, D), BF16)
        dz_ref[:, 4 * D:5 * D] = (dyb * y * dsgb).astype(BF16)
        dsc_ref[...] += jnp.sum(dy_b * ypre, axis=0, keepdims=True)
        dypre = (dy_b * sc).astype(BF16)
        dps = []
        for g in range(4):
            cols = slice(g * GDIM, (g + 1) * GDIM)
            dwg_ref[g] += _tn(pb[:, cols], dypre[:, cols])
            dps.append(_mm(dypre[:, cols], wgt_ref[g]))
        dp_ref[...] = jnp.concatenate(dps, axis=1)
        catt_ref[...] = jnp.concatenate([ya, yb], axis=1).T

        @pl.when(i == n_tiles - 1)
        def _():
            for h in range(A_GROUPS):
                tot = jnp.sum(dbias_ref[:, h * GDIM:(h + 1) * GDIM].T, axis=0, keepdims=True)
                db_ref[pl.ds(h * 8, 8), :] = jnp.broadcast_to(tot, (8, CHUNK))

    prev, nxt = _halo_specs(ts, seq, D)
    row = lambda w_: pl.BlockSpec((ts, w_), lambda i: (i, 0))
    acc = lambda shape: pl.BlockSpec(shape, lambda i: (0,) * len(shape))
    return pl.pallas_call(
        body, grid=(n_tiles,), name="l0_mix_bwd",
        out_shape=(jax.ShapeDtypeStruct((seq, MIX0_IN), BF16), jax.ShapeDtypeStruct((seq, D), F32),
                   jax.ShapeDtypeStruct((2 * D, seq), BF16),
                   jax.ShapeDtypeStruct((4, CHUNK, CHUNK), F32), jax.ShapeDtypeStruct((CHUNK, D), F32),
                   jax.ShapeDtypeStruct((1, D), F32), jax.ShapeDtypeStruct((1, D), F32),
                   jax.ShapeDtypeStruct((4, GDIM, GDIM), F32), jax.ShapeDtypeStruct((32, CHUNK), F32)),
        in_specs=[row(D), _resident((D, 2 * D)), row(3 * D), row(D), prev, nxt, row(D), _resident((4, CHUNK, CHUNK)),
                  _resident((4, CHUNK, CHUNK)), _resident((CHUNK, D)), _resident((1, D)), _resident((4, GDIM, GDIM)),
                  _resident((4, GDIM, GDIM)), _resident((1, D))],
        out_specs=(row(MIX0_IN), row(D), pl.BlockSpec((2 * D, ts), lambda i: (0, i)),
                   acc((4, CHUNK, CHUNK)), acc((CHUNK, D)), acc((1, D)), acc((1, D)), acc((4, GDIM, GDIM)),
                   acc((32, CHUNK))),
        scratch_shapes=[pltpu.VMEM((ts + 2 * POOL_HALO, D), F32)],
        compiler_params=_params(56),
    )(dx1b, wout_t, za, bx, bx, bx, bg, ws, ws_t, bias, gv, wg, wg_t, scale)


def _l0_pool_bwd(dp, dz):
    seq = dp.shape[0]
    ts = 512
    n_tiles = seq // ts
    ext = ts + 2 * POOL_HALO

    def body(dp_ref, dpp_ref, dpn_ref, dz_ref, out_ref, qe_ref):
        i = pl.program_id(0)
        _fill_halo(qe_ref, dp_ref[...], dpp_ref, dpn_ref, i, n_tiles, ts)
        te = i * ts - POOL_HALO + lax.broadcasted_iota(jnp.int32, (ext, 1), 0)
        for gi, w in enumerate(POOL_WINDOWS):
            hw = w // 2
            cols = slice(gi * GDIM, (gi + 1) * GDIM)
            cnt = jnp.maximum(jnp.minimum(te + hw, seq) - jnp.maximum(te - hw, 0), 1).astype(F32)
            qe_ref[:, cols] = qe_ref[:, cols] / cnt
        outs = []
        for gi, w in enumerate(POOL_WINDOWS):
            hw = w // 2
            cols = slice(gi * GDIM, (gi + 1) * GDIM)
            acc = qe_ref[pl.ds(POOL_HALO - hw + 1, ts), cols]
            for k in range(-hw + 2, hw + 1):
                acc = acc + qe_ref[pl.ds(POOL_HALO + k, ts), cols]
            outs.append(acc - dp_ref[:, cols])
        out_ref[...] = jnp.concatenate(outs, axis=1).astype(BF16)

    prev, nxt = _halo_specs(ts, seq, D)
    row = pl.BlockSpec((ts, D), lambda i: (i, 0))
    return pl.pallas_call(
        body, grid=(n_tiles,), name="l0_pool_bwd",
        out_shape=jax.ShapeDtypeStruct(dz.shape, BF16),
        in_specs=[row, prev, nxt, pl.BlockSpec(memory_space=pl.ANY)],
        out_specs=pl.BlockSpec((ts, D), lambda i: (i, 3)),
        input_output_aliases={3: 0},
        scratch_shapes=[pltpu.VMEM((ext, D), F32)],
        compiler_params=_params(32),
    )(dp, dp, dp, dz)


def _l0_in_proj_bwd(dz, w_t, x, g0, dx1):
    seq = x.shape[0]
    tm = 512

    def body(dz_ref, wt_ref, x_ref, g_ref, dres_ref, dx_ref, dn_ref):
        @pl.when(pl.program_id(0) == 0)
        def _():
            dn_ref[...] = jnp.zeros_like(dn_ref)

        dh = _mm(dz_ref[...], wt_ref[...])
        xf = x_ref[...]
        r = lax.rsqrt(jnp.mean(xf * xf, axis=1, keepdims=True) + EPS)
        xn = xf * r
        dn_ref[...] += jnp.sum(dh * xn, axis=0, keepdims=True)
        dxn = dh * g_ref[...]
        dx_ref[...] = dres_ref[...] + r * (dxn - xn * jnp.mean(dxn * xn, axis=1, keepdims=True))

    row = lambda w_: pl.BlockSpec((tm, w_), lambda i: (i, 0))
    return pl.pallas_call(
        body, grid=(seq // tm,), name="l0_in_proj_bwd",
        out_shape=(jax.ShapeDtypeStruct((seq, D), F32), jax.ShapeDtypeStruct((1, D), F32)),
        in_specs=[row(MIX0_IN), _resident((MIX0_IN, D)), row(D), _resident((1, D)), row(D)],
        out_specs=(row(D), pl.BlockSpec((1, D), lambda i: (0, 0))),
        compiler_params=_params(56),
    )(dz, w_t, x, g0, dx1)


def _dw_matmul(a_t, b, name, b_transposed=False, tn=1024):
    k, seq = a_t.shape
    n = b.shape[0] if b_transposed else b.shape[1]
    tn = min(n, tn)
    ts = 512

    def body(a_ref, b_ref, o_ref):
        @pl.when(pl.program_id(1) == 0)
        def _():
            o_ref[...] = jnp.zeros_like(o_ref)

        o_ref[...] += _nt(a_ref[...], b_ref[...]) if b_transposed else _mm(a_ref[...], b_ref[...])

    b_spec = (pl.BlockSpec((tn, ts), lambda j, s: (j, s)) if b_transposed else pl.BlockSpec((ts, tn), lambda j, s: (s, j)))
    return pl.pallas_call(
        body, grid=(n // tn, seq // ts), name=name,
        out_shape=jax.ShapeDtypeStruct((k, n), F32),
        in_specs=[pl.BlockSpec((k, ts), lambda j, s: (0, s)), b_spec],
        out_specs=pl.BlockSpec((k, tn), lambda j, s: (0, j)),
        compiler_params=_params(48, 2),
    )(a_t, b)


ROW_TILES = 8


def _cast_shards(shards):
    n = len(shards)

    def body(*refs):
        for a in range(n):
            refs[n + a][...] = refs[a][...].astype(BF16)

    vm = pl.BlockSpec(memory_space=pltpu.VMEM)
    return pl.pallas_call(body, name="cast_weights", out_shape=[jax.ShapeDtypeStruct(t.shape, BF16) for t in shards],
                          in_specs=[vm] * n, out_specs=[vm] * n, compiler_params=_params(32, 0))(*shards)


def _adamw_math(w, g, m, v):
    m2 = ADAM_B1 * m + (1.0 - ADAM_B1) * g
    v2 = ADAM_B2 * v + (1.0 - ADAM_B2) * (g * g)
    m_hat = m2 / (1.0 - ADAM_B1 ** ADAM_STEP)
    v_hat = v2 / (1.0 - ADAM_B2 ** ADAM_STEP)
    delta = -ADAM_LR * (m_hat / (jnp.sqrt(v_hat) + ADAM_EPS) + ADAM_WD * w)
    return delta, m2, v2


def _pair_sum(g_list, r1_list, pos):
    n = len(g_list)

    def body(pos_ref, *refs):
        q = pl.program_id(1)
        for a in range(n):
            refs[2 * n + a][...] = (refs[a][...] + refs[n + a][...]).astype(BF16)

        @pl.when(q == pos_ref[1])
        def _():
            for a in range(n):
                refs[3 * n + a][...] = refs[a][...] + refs[n + a][...]

    g_specs, r_specs, own_specs, wire_shapes, own_shapes = [], [], [], [], []
    for g in g_list:
        _, nchip, rows, width = g.shape
        tr = rows // ROW_TILES
        g_specs.append(pl.BlockSpec((None, None, tr, width), lambda i, q, pos: (pos[0], q, i, 0)))
        r_specs.append(pl.BlockSpec((None, tr, width), lambda i, q, pos: (q, i, 0)))
        own_specs.append(pl.BlockSpec((tr, width), lambda i, q, pos: (i, 0)))
        wire_shapes.append(jax.ShapeDtypeStruct((nchip, rows, width), BF16))
        own_shapes.append(jax.ShapeDtypeStruct((rows, width), F32))
    out = pl.pallas_call(
        body, name="grad_pair_sum", out_shape=wire_shapes + own_shapes,
        grid_spec=pltpu.PrefetchScalarGridSpec(
            num_scalar_prefetch=1, grid=(ROW_TILES, 4),
            in_specs=g_specs + r_specs, out_specs=r_specs + own_specs),
        compiler_params=_params(32, 2),
    )(pos, *g_list, *r1_list)
    return out[:n], out[n:]


def _final_sum_adamw(own_list, r2_list, w_list, m_list, v_list):
    n = len(w_list)

    def body(*refs):
        own, r2, w, m, v = (refs[k * n:(k + 1) * n] for k in range(5))
        outs = [refs[(5 + k) * n:(6 + k) * n] for k in range(4)]
        for a in range(n):
            g = own[a][...] + r2[a][0].astype(F32) + r2[a][1].astype(F32) + r2[a][2].astype(F32)
            delta, m2, v2 = _adamw_math(w[a][...], g, m[a][...], v[a][...])
            for o_ref, val in zip((outs[0][a], outs[1][a], outs[2][a], outs[3][a]), (g, delta, m2, v2)):
                o_ref[...] = val

    flat, wire, shapes = [], [], []
    for t in w_list:
        rows, width = t.shape
        tr = rows // ROW_TILES
        flat.append(pl.BlockSpec((tr, width), lambda i: (i, 0)))
        wire.append(pl.BlockSpec((3, tr, width), lambda i: (0, i, 0)))
        shapes.append(jax.ShapeDtypeStruct((rows, width), F32))
    out = pl.pallas_call(
        body, name="grad_sum_adamw", grid=(ROW_TILES,), out_shape=shapes * 4,
        in_specs=flat + wire + flat * 3, out_specs=flat * 4, compiler_params=_params(32),
    )(*own_list, *r2_list, *w_list, *m_list, *v_list)
    return [out[k * n:(k + 1) * n] for k in range(4)]


def _small_pair_sum(gs, r1s):
    def body(g_ref, r_ref, o_ref):
        o_ref[...] = g_ref[...] + r_ref[...]

    return pl.pallas_call(body, name="small_pair_sum", out_shape=jax.ShapeDtypeStruct(gs.shape, F32))(gs, r1s)


SMALL_NAMES = ("norm_0", "a_v_norm_0", "b_scale_0", "norm_1", "final_norm", "a_spatial_w_0", "a_spatial_b_0", "sink_1")
SMALL_VIEWS = ((8, LANES),) * 5 + ((4 * CHUNK, LANES), (4, LANES), (1, N_HEADS))
SMALL_ROW0 = (0, 8, 16, 24, 32, 40, 552, 560)
SMALL_ROWS = 568


def _small_sum_adamw(ps, r2s, pos, w_list, m_list, v_list):
    n = len(w_list)

    def body(pos_ref, p_ref, r_ref, *refs):
        stack = refs[-1]
        stack[0] = p_ref[...]
        stack[2] = r_ref[0]
        stack[1] = r_ref[1]
        stack[3] = r_ref[2]
        me = pos_ref[1]
        gtot = stack[me] + stack[me ^ 1]
        gtot = gtot + stack[me ^ 2]
        gtot = gtot + stack[me ^ 3]
        for a, ((rows, width), r0) in enumerate(zip(SMALL_VIEWS, SMALL_ROW0)):
            g = gtot[r0:r0 + rows, 0:width]
            delta, m2, v2 = _adamw_math(refs[a][...], g, refs[n + a][...], refs[2 * n + a][...])
            for k, val in enumerate((g, delta, m2, v2)):
                refs[(3 + k) * n + a][...] = val

    vm = pl.BlockSpec(memory_space=pltpu.VMEM)
    shapes = [jax.ShapeDtypeStruct(s, F32) for s in SMALL_VIEWS]
    out = pl.pallas_call(
        body, name="small_sum_adamw", out_shape=shapes * 4,
        in_specs=[pl.BlockSpec(memory_space=pltpu.SMEM), vm, vm] + [vm] * (3 * n), out_specs=[vm] * (4 * n),
        scratch_shapes=[pltpu.VMEM((4, SMALL_ROWS, LANES), F32)],
    )(pos, ps, r2s, *w_list, *m_list, *v_list)
    return [out[k * n:(k + 1) * n] for k in range(4)]


def _all_gather(blks):
    n = len(blks)

    def body(*refs):
        ins, outs = refs[:n], refs[n:2 * n]
        send_sems, recv_sems, local_sems = refs[2 * n:]
        x, y, c = lax.axis_index("x"), lax.axis_index("y"), lax.axis_index("c")
        me, sibling = (x, y, c), (x, y, 1 - c)
        chips = [(1 - x, y), (x, 1 - y), (1 - x, 1 - y)]

        def slot(a, px, py, pc):
            return outs[a].at[4 * px + 2 * py + pc]

        def copy(k, a, block, to, from_input=False):
            return pltpu.make_async_remote_copy(
                src_ref=ins[a] if from_input else slot(a, *block), dst_ref=slot(a, *block),
                send_sem=send_sems.at[k, a], recv_sem=recv_sems.at[k, a], device_id=to, device_id_type=MESH)

        mine = [pltpu.make_async_copy(ins[a], slot(a, *me), local_sems.at[a]) for a in range(n)]
        first = []
        for a in range(n):
            first.append(copy(0, a, me, sibling, from_input=True))
            first += [copy(1 + j, a, me, (*chip, c), from_input=True) for j, chip in enumerate(chips)]
        for cp in mine + first:
            cp.start()
        passed = []
        for j, chip in enumerate(chips):
            for a in range(n):
                copy(1 + j, a, (*chip, c), me).wait_recv()
                passed.append(copy(4 + j, a, (*chip, c), sibling))
                passed[-1].start()
        for a in range(n):
            copy(0, a, sibling, me).wait_recv()
        for j, chip in enumerate(chips):
            for a in range(n):
                copy(4 + j, a, (*chip, 1 - c), me).wait_recv()
        for cp in first + passed:
            cp.wait_send()
        for cp in mine:
            cp.wait()

    any_spec = pl.BlockSpec(memory_space=pl.ANY)
    return pl.pallas_call(
        body, name="weights_all_gather", out_shape=[jax.ShapeDtypeStruct((N_DEV,) + t.shape, t.dtype) for t in blks],
        in_specs=[any_spec] * n, out_specs=[any_spec] * n,
        scratch_shapes=[pltpu.SemaphoreType.DMA((7, n)), pltpu.SemaphoreType.DMA((7, n)), pltpu.SemaphoreType.DMA((n,))],
    )(*blks)


PEER_FLIPS = tuple((fx, fy, fc) for fx in (0, 1) for fy in (0, 1) for fc in (0, 1))[1:]


def _sequencer_all_gather(blks, name, collective_id):
    n = len(blks)

    def body(*refs):
        ins, outs = refs[:n], refs[n:2 * n]
        send_sems, recv_sems, local_sems = refs[2 * n:]
        x, y, c = lax.axis_index("x"), lax.axis_index("y"), lax.axis_index("c")
        peers = [(x ^ fx, y ^ fy, c ^ fc) for fx, fy, fc in PEER_FLIPS]
        barrier = pltpu.get_barrier_semaphore()
        for peer in peers:
            pl.semaphore_signal(barrier, inc=1, device_id=peer, device_id_type=MESH)
        pl.semaphore_wait(barrier, len(peers))
        me = 4 * x + 2 * y + c
        copies = [pltpu.make_async_remote_copy(
            src_ref=ins[a], dst_ref=outs[a].at[me], send_sem=send_sems.at[k, a], recv_sem=recv_sems.at[k, a],
            device_id=peer, device_id_type=MESH) for k, peer in enumerate(peers) for a in range(n)]
        mine = [pltpu.make_async_copy(ins[a], outs[a].at[me], local_sems.at[a]) for a in range(n)]
        for cp in copies + mine:
            cp.start()
        for cp in copies + mine:
            cp.wait()

    return pl.kernel(
        body, out_type=[jax.ShapeDtypeStruct((N_DEV,) + t.shape, t.dtype) for t in blks],
        mesh=plsc.ScalarSubcoreMesh(axis_name="sequencer", num_cores=1), name=name,
        scratch_types=[pltpu.SemaphoreType.DMA((7, n)), pltpu.SemaphoreType.DMA((7, n)), pltpu.SemaphoreType.DMA((n,))],
        compiler_params=pltpu.CompilerParams(collective_id=collective_id),
    )(*blks)


def _sibling_exchange(g_list, gs):
    n = len(g_list)

    def body(*refs):
        ins, outs = refs[:n + 1], refs[n + 1:2 * n + 2]
        send_sems, recv_sems = refs[2 * n + 2:]
        x, y, c = lax.axis_index("x"), lax.axis_index("y"), lax.axis_index("c")
        copies = [pltpu.make_async_remote_copy(
            src_ref=ins[a].at[1 - c] if a < n else ins[a], dst_ref=outs[a], send_sem=send_sems.at[a],
            recv_sem=recv_sems.at[a], device_id=(x, y, 1 - c), device_id_type=MESH) for a in range(n + 1)]
        for cp in copies:
            cp.start()
        for cp in copies:
            cp.wait()

    any_spec = pl.BlockSpec(memory_space=pl.ANY)
    out = pl.pallas_call(
        body, name="grad_sibling_exchange",
        out_shape=[jax.ShapeDtypeStruct(g.shape[1:], F32) for g in g_list] + [jax.ShapeDtypeStruct(gs.shape, F32)],
        in_specs=[any_spec] * (n + 1), out_specs=[any_spec] * (n + 1),
        scratch_shapes=[pltpu.SemaphoreType.DMA((n + 1,)), pltpu.SemaphoreType.DMA((n + 1,))],
    )(*g_list, gs)
    return out[:n], out[n]


def _chip_exchange(p_list, ps):
    n = len(p_list)

    def body(*refs):
        ins, outs = refs[:n + 1], refs[n + 1:2 * n + 2]
        send_sems, recv_sems = refs[2 * n + 2:]
        x, y, c = lax.axis_index("x"), lax.axis_index("y"), lax.axis_index("c")
        copies = []
        for j, (fx, fy) in enumerate(((1, 0), (0, 1), (1, 1))):
            tx = x ^ fx
            ty = y ^ fy
            for a in range(n + 1):
                copies.append(pltpu.make_async_remote_copy(
                    src_ref=ins[a].at[2 * tx + ty] if a < n else ins[a], dst_ref=outs[a].at[j],
                    send_sem=send_sems.at[j, a], recv_sem=recv_sems.at[j, a], device_id=(tx, ty, c),
                    device_id_type=MESH))
        for cp in copies:
            cp.start()
        for cp in copies:
            cp.wait()

    any_spec = pl.BlockSpec(memory_space=pl.ANY)
    out = pl.pallas_call(
        body, name="grad_chip_exchange",
        out_shape=[jax.ShapeDtypeStruct((3,) + p.shape[1:], p.dtype) for p in p_list]
        + [jax.ShapeDtypeStruct((3,) + ps.shape, F32)],
        in_specs=[any_spec] * (n + 1), out_specs=[any_spec] * (n + 1),
        scratch_shapes=[pltpu.SemaphoreType.DMA((3, n + 1)), pltpu.SemaphoreType.DMA((3, n + 1))],
    )(*p_list, ps)
    return out[:n], out[n]


def _shard_views(w_in_0, b_group_w_0, w_out_0, w_in_1, w_out_1):
    return [w_in_0, b_group_w_0.reshape(4 * 32, GDIM), w_out_0, w_in_1, w_out_1]


def _small_views(named):
    return [named[name].reshape(view) for name, view in zip(SMALL_NAMES, SMALL_VIEWS)]


def _pack_small_grads(named):
    rows = []
    for name, (r, w) in zip(SMALL_NAMES, SMALL_VIEWS):
        t = named[name].reshape(r, w)
        pad_r = -r % 8
        rows.append(jnp.pad(t, ((0, pad_r), (0, LANES - w))))
    return jnp.concatenate(rows, axis=0)


def _owner_blocks(t, axis):
    shape = t.shape
    t = t.reshape(shape[:axis] + (4, 2, shape[axis] // N_DEV) + shape[axis + 1:])
    t = jnp.moveaxis(t, (axis + 1, axis), (0, 1))
    return t.reshape(2, 4, -1, shape[-1] if axis != len(shape) - 1 else shape[-1] // N_DEV)


def kernel(x, norm_0, w_in_0, a_v_norm_0, a_spatial_w_0, a_spatial_b_0, b_group_w_0, b_scale_0, w_out_0, norm_1, w_in_1, sink_1, w_out_1, final_norm, loss_target, m_norm_0, m_w_in_0, m_a_v_norm_0, m_a_spatial_w_0, m_a_spatial_b_0, m_b_group_w_0, m_b_scale_0, m_w_out_0, m_norm_1, m_w_in_1, m_sink_1, m_w_out_1, m_final_norm, v_norm_0, v_w_in_0, v_a_v_norm_0, v_a_spatial_w_0, v_a_spatial_b_0, v_b_group_w_0, v_b_scale_0, v_w_out_0, v_norm_1, v_w_in_1, v_sink_1, v_w_out_1, v_final_norm):
    seq = x.shape[1]
    xs = x.reshape(seq, D)
    tgt = loss_target.reshape(seq, D)
    ax, ay, ac = lax.axis_index("x"), lax.axis_index("y"), lax.axis_index("c")
    pos = jnp.stack([ac, 2 * ax + ay]).astype(jnp.int32)

    shards = _shard_views(w_in_0, b_group_w_0, w_out_0, w_in_1, w_out_1)
    cast = _cast_shards(shards)
    gathered = (list(_all_gather(cast[0:1])) + list(_sequencer_all_gather(cast[1:3], "weights_gather_a", 1))
                + list(_sequencer_all_gather(cast[3:5], "weights_gather_b", 2)))
    win0 = gathered[0].transpose(1, 0, 2).reshape(D, MIX0_IN)
    wg = gathered[1].reshape(N_DEV, 4, 32, GDIM).transpose(1, 0, 2, 3).reshape(4, GDIM, GDIM)
    wout0 = gathered[2].reshape(2 * D, D)
    win1 = gathered[3].transpose(1, 0, 2).reshape(D, MIX1_IN)
    wout1 = gathered[4].reshape(D, D)
    loss_part, grad_x, d_win0, d_wg, d_wout0, d_win1, d_wout1, small_grads = _local_step(
        xs, tgt, win0, wg, wout0, win1, wout1, norm_0, a_v_norm_0, a_spatial_w_0, a_spatial_b_0, b_scale_0, norm_1, sink_1,
        final_norm)

    g_list = [_owner_blocks(d_win0, 1), _owner_blocks(d_wg, 1), _owner_blocks(d_wout0, 0), _owner_blocks(d_win1, 1),
              _owner_blocks(d_wout1, 0)]
    gs = _pack_small_grads(small_grads)
    r1, r1s = _sibling_exchange(g_list, gs)
    p_wire, p_own = _pair_sum(g_list, r1, pos)
    ps = _small_pair_sum(gs, r1s)
    r2, r2s = _chip_exchange(p_wire, ps)
    big = _final_sum_adamw(p_own, r2, shards, _shard_views(m_w_in_0, m_b_group_w_0, m_w_out_0, m_w_in_1, m_w_out_1),
                           _shard_views(v_w_in_0, v_b_group_w_0, v_w_out_0, v_w_in_1, v_w_out_1))
    weights = dict(norm_0=norm_0, a_v_norm_0=a_v_norm_0, a_spatial_w_0=a_spatial_w_0, a_spatial_b_0=a_spatial_b_0,
                   b_scale_0=b_scale_0, norm_1=norm_1, sink_1=sink_1, final_norm=final_norm)
    m_small = dict(norm_0=m_norm_0, a_v_norm_0=m_a_v_norm_0, a_spatial_w_0=m_a_spatial_w_0, a_spatial_b_0=m_a_spatial_b_0,
                   b_scale_0=m_b_scale_0, norm_1=m_norm_1, sink_1=m_sink_1, final_norm=m_final_norm)
    v_small = dict(norm_0=v_norm_0, a_v_norm_0=v_a_v_norm_0, a_spatial_w_0=v_a_spatial_w_0, a_spatial_b_0=v_a_spatial_b_0,
                   b_scale_0=v_b_scale_0, norm_1=v_norm_1, sink_1=v_sink_1, final_norm=v_final_norm)
    small = _small_sum_adamw(ps, r2s, pos, _small_views(weights), _small_views(m_small), _small_views(v_small))

    def in_order(kind):
        b = [b_.reshape(s_.shape) for b_, s_ in zip(big[kind], (w_in_0, b_group_w_0, w_out_0, w_in_1, w_out_1))]
        s = {name: t.reshape(weights[name].shape) for name, t in zip(SMALL_NAMES, small[kind])}
        return [s["norm_0"], b[0], s["a_v_norm_0"], s["a_spatial_w_0"], s["a_spatial_b_0"], b[1], s["b_scale_0"], b[2],
                s["norm_1"], b[3], s["sink_1"], b[4], s["final_norm"]]

    loss = lax.psum(loss_part[0, 0], ("x", "y", "c"))
    return (loss, grad_x.reshape(1, seq, D), *in_order(0), *in_order(1), *in_order(2), *in_order(3))


def _local_step(xs, tgt, win0, wg, wout0, win1, wout1, norm_0, a_v_norm_0, a_spatial_w_0, a_spatial_b_0, b_scale_0, norm_1,
                sink_1, final_norm):
    seq = xs.shape[0]
    win0_t, wout0_t, win1_t = win0.T, wout0.T, win1.T
    wg_t = jnp.swapaxes(wg, 1, 2)
    ws = a_spatial_w_0.astype(BF16)
    ws_t = jnp.swapaxes(ws, 1, 2)
    bias = jnp.repeat(a_spatial_b_0.T, GDIM, axis=1)
    g0, gv, scale, g1, gf = (t.reshape(1, D) for t in (norm_0, a_v_norm_0, b_scale_0, norm_1, final_norm))
    cos_t, sin_t = _rope_tables_t(seq)

    za, bx, bg, h0_t = _l0_in_proj(xs, g0, win0)
    x1 = _l0_mix_fwd(za, bx, bg, xs, ws, bias, gv, wg, scale, wout0)
    qt, kt, vt, gatet, h1_t = _l1_in_proj(x1, g1, win1_t, cos_t, sin_t)
    dx2, dx2b, y_t, att, lse, loss_part, d_gf = _l1_attn_fwd(qt, kt, vt, gatet, x1, tgt, wout1, gf, sink_1)

    d_wout1 = _dw_matmul(y_t, dx2b, "dw_out_1")
    dq_r, dgate, dk_pad, dv_pad, d_sink = _l1_attn_bwd(dx2b, wout1, qt, kt, vt, gatet, att, lse, sink_1)
    dk_r = dk_pad[:, BLK:BLK + seq]
    dv = dv_pad[:, BLK:BLK + seq]
    dx1, dx1b, dz1_t, d_g1 = _l1_in_proj_bwd(dq_r, dk_r, dv, dgate, cos_t, sin_t, win1, x1, g1, dx2)
    d_win1 = _dw_matmul(h1_t, dz1_t, "dw_in_1", b_transposed=True, tn=1280)

    dz0, dp, cat_t, d_ws, _, d_gv, d_scale, d_wg, d_b = _l0_mix_bwd(
        dx1b, wout0_t, za, bx, bg, ws, ws_t, bias, gv, wg, wg_t, scale)
    d_wout0 = _dw_matmul(cat_t, dx1b, "dw_out_0")
    dz0 = _l0_pool_bwd(dp, dz0)
    grad_x, d_g0 = _l0_in_proj_bwd(dz0, win0_t, xs, g0, dx1)
    d_win0 = _dw_matmul(h0_t, dz0, "dw_in_0")

    small_grads = dict(norm_0=d_g0, a_v_norm_0=d_gv, a_spatial_w_0=d_ws, a_spatial_b_0=d_b.reshape(4, 8, CHUNK)[:, 0, :],
                       b_scale_0=d_scale, norm_1=d_g1, sink_1=d_sink[:, 0], final_norm=d_gf)
    return loss_part, grad_x, d_win0, d_wg, d_wout0, d_win1, d_wout1, small_grads
```

```python
import jax
import jax.numpy as jnp
from jax import lax
from jax.experimental import pallas as pl
from jax.experimental.pallas import tpu as pltpu
from jax.experimental.pallas import tpu_sc as plsc

F32 = jnp.float32
BF16 = jnp.bfloat16

D = 1024
EPS = 1e-6
NEG_INF = -1e30
CHUNK = 128
A_GROUPS = 4
POOL_WINDOWS = (2, 4, 8, 16)
POOL_HALO = 8
GDIM = 256
N_HEADS = 16
N_KV = 4
GQA = 4
HD = 64
BLK = 128
ROT_HALF = 8
ROPE_THETA = 500000.0
SCALE = HD ** -0.5
MIX0_IN = 5 * D
MIX1_IN = 2560
KV_W = N_KV * HD
Q_ROWS, K_ROWS, V_ROWS, G_ROWS = (0, D), (D, D + KV_W), (D + KV_W, D + 2 * KV_W), (D + 2 * KV_W, MIX1_IN)
TQ = 512

ADAM_LR = 0.001
ADAM_B1 = 0.9
ADAM_B2 = 0.999
ADAM_EPS = 1e-08
ADAM_WD = 0.01
ADAM_STEP = 10

N_DEV = 8
LANES = 128
MIB = 2 ** 20
MESH = pl.DeviceIdType.MESH


def _params(limit_mib, n_axes=1):
    return pltpu.CompilerParams(vmem_limit_bytes=limit_mib * MIB, dimension_semantics=("arbitrary",) * n_axes)


def _resident(shape):
    nd = len(shape)
    return pl.BlockSpec(shape, lambda *_: (0,) * nd, pipeline_mode=pl.Buffered(1))


def _gelu(x):
    k = 0.7978845608028654
    return 0.5 * x * (1.0 + jnp.tanh(k * (x + 0.044715 * x * x * x)))


def _gelu_and_grad(x):
    k = 0.7978845608028654
    x2 = x * x
    t = jnp.tanh(k * (x + 0.044715 * x * x2))
    g = 0.5 * x * (1.0 + t)
    dg = 0.5 * (1.0 + t) + 0.5 * x * (1.0 - t * t) * (k * (1.0 + 3.0 * 0.044715 * x2))
    return g, dg


def _silu_and_grad(x):
    s = jax.nn.sigmoid(x)
    return x * s, s * (1.0 + x * (1.0 - s))


def _nt(a, b):
    return lax.dot_general(a, b, (((1,), (1,)), ((), ())), preferred_element_type=F32)


def _tn(a, b):
    return lax.dot_general(a, b, (((0,), (0,)), ((), ())), preferred_element_type=F32)


def _mm(a, b):
    return jnp.dot(a, b, preferred_element_type=F32)


def _rope_tables_t(seq):
    inv = ROPE_THETA ** (-jnp.arange(0, 2 * ROT_HALF, 2, dtype=F32) / (2 * ROT_HALF))
    ang = inv[:, None] * jnp.arange(seq, dtype=F32)[None, :]
    return jnp.cos(ang), jnp.sin(ang)


def _rope_t(z, c, s, n_heads, sign):
    parts = []
    for h in range(n_heads):
        b = h * HD
        x1, x2 = z[b:b + ROT_HALF], z[b + ROT_HALF:b + 2 * ROT_HALF]
        if sign > 0:
            parts += [x1 * c - x2 * s, x2 * c + x1 * s]
        else:
            parts += [x1 * c + x2 * s, x2 * c - x1 * s]
        parts.append(z[b + 2 * ROT_HALF:b + HD])
    return jnp.concatenate(parts, axis=0)


def _l0_in_proj(x, g0, w):
    seq = x.shape[0]
    tm = 512

    def body(x_ref, g_ref, w_ref, za_ref, bx_ref, bg_ref, ht_ref):
        xf = x_ref[...]
        r = lax.rsqrt(jnp.mean(xf * xf, axis=1, keepdims=True) + EPS)
        h = (xf * r * g_ref[...]).astype(BF16)
        ht_ref[...] = h.T
        for j in range(3):
            za_ref[:, j * D:(j + 1) * D] = _mm(h, w_ref[:, j * D:(j + 1) * D]).astype(BF16)
        bx_ref[...] = _mm(h, w_ref[:, 3 * D:4 * D])
        bg_ref[...] = _mm(h, w_ref[:, 4 * D:5 * D]).astype(BF16)

    return pl.pallas_call(
        body, grid=(seq // tm,), name="l0_in_proj",
        out_shape=(jax.ShapeDtypeStruct((seq, 3 * D), BF16), jax.ShapeDtypeStruct((seq, D), F32),
                   jax.ShapeDtypeStruct((seq, D), BF16), jax.ShapeDtypeStruct((D, seq), BF16)),
        in_specs=[pl.BlockSpec((tm, D), lambda i: (i, 0)), _resident((1, D)), _resident((D, MIX0_IN))],
        out_specs=(pl.BlockSpec((tm, 3 * D), lambda i: (i, 0)), pl.BlockSpec((tm, D), lambda i: (i, 0)),
                   pl.BlockSpec((tm, D), lambda i: (i, 0)), pl.BlockSpec((D, tm), lambda i: (0, i))),
        compiler_params=_params(48),
    )(x, g0, w)


def _fill_halo(ext_ref, cur, prev_ref, next_ref, i, n_tiles, ts):
    ext_ref[pl.ds(0, POOL_HALO), :] = jnp.where(i > 0, prev_ref[...], 0.0)
    ext_ref[pl.ds(POOL_HALO, ts), :] = cur
    ext_ref[pl.ds(POOL_HALO + ts, POOL_HALO), :] = jnp.where(i < n_tiles - 1, next_ref[...], 0.0)


def _pool_forward(xe_ref, ts, t0, seq):
    tg = t0 + lax.broadcasted_iota(jnp.int32, (ts, 1), 0)
    outs = []
    for gi, w in enumerate(POOL_WINDOWS):
        hw = w // 2
        cols = slice(gi * GDIM, (gi + 1) * GDIM)
        acc = xe_ref[pl.ds(POOL_HALO - hw, ts), cols]
        for k in range(-hw + 1, hw):
            acc = acc + xe_ref[pl.ds(POOL_HALO + k, ts), cols]
        cnt = (jnp.minimum(tg + hw, seq) - jnp.maximum(tg - hw, 0)).astype(F32)
        outs.append(acc / cnt - xe_ref[pl.ds(POOL_HALO, ts), cols])
    return jnp.concatenate(outs, axis=1)


def _spatial_mix(ws_ref, vnb, bias, ts):
    rows = []
    for c in range(ts // CHUNK):
        vc = vnb[c * CHUNK:(c + 1) * CHUNK, :]
        rows.append(jnp.concatenate(
            [_mm(ws_ref[h], vc[:, h * GDIM:(h + 1) * GDIM]) for h in range(A_GROUPS)], axis=1) + bias)
    return jnp.concatenate(rows, axis=0)


def _halo_specs(ts, seq, width):
    per = ts // POOL_HALO
    last = seq // POOL_HALO - 1
    prev = pl.BlockSpec((POOL_HALO, width), lambda i: (jnp.maximum(i * per - 1, 0), 0))
    nxt = pl.BlockSpec((POOL_HALO, width), lambda i: (jnp.minimum((i + 1) * per, last), 0))
    return prev, nxt


def _l0_mix_fwd(za, bx, bg, x, ws, bias, gv, wg, scale, wout):
    seq = x.shape[0]
    ts = 512
    n_tiles = seq // ts

    def body(za_ref, bx_ref, bxp_ref, bxn_ref, bg_ref, x_ref, ws_ref, bias_ref, gv_ref, wg_ref, sc_ref, wo_ref,
             x1_ref, xe_ref):
        i = pl.program_id(0)
        u = _gelu(za_ref[:, 0:D].astype(F32))
        vg = _gelu(za_ref[:, D:2 * D].astype(F32))
        rv = lax.rsqrt(jnp.mean(vg * vg, axis=1, keepdims=True) + EPS)
        vnb = (vg * rv * gv_ref[...]).astype(BF16)
        mixed = _spatial_mix(ws_ref, vnb, bias_ref[...], ts)
        ag = za_ref[:, 2 * D:3 * D].astype(F32)
        ya = (u * mixed * (ag * jax.nn.sigmoid(ag))).astype(BF16)

        _fill_halo(xe_ref, bx_ref[...], bxp_ref, bxn_ref, i, n_tiles, ts)
        pb = _pool_forward(xe_ref, ts, i * ts, seq).astype(BF16)
        y = jnp.concatenate([_mm(pb[:, g * GDIM:(g + 1) * GDIM], wg_ref[g]) for g in range(4)], axis=1) * sc_ref[...]
        bgf = bg_ref[...].astype(F32)
        yb = (y * (bgf * jax.nn.sigmoid(bgf))).astype(BF16)
        x1_ref[...] = x_ref[...] + _mm(ya, wo_ref[0:D, :]) + _mm(yb, wo_ref[D:2 * D, :])

    prev, nxt = _halo_specs(ts, seq, D)
    row = lambda w: pl.BlockSpec((ts, w), lambda i: (i, 0))
    return pl.pallas_call(
        body, grid=(n_tiles,), name="l0_mix_fwd",
        out_shape=jax.ShapeDtypeStruct((seq, D), F32),
        in_specs=[row(3 * D), row(D), prev, nxt, row(D), row(D), _resident((4, CHUNK, CHUNK)), _resident((CHUNK, D)),
                  _resident((1, D)), _resident((4, GDIM, GDIM)), _resident((1, D)), _resident((2 * D, D))],
        out_specs=row(D),
        scratch_shapes=[pltpu.VMEM((ts + 2 * POOL_HALO, D), F32)],
        compiler_params=_params(56),
    )(za, bx, bx, bx, bg, x, ws, bias, gv, wg, scale, wout)


def _l1_in_proj(x1, g1, w_t, cos_t, sin_t):
    seq = x1.shape[0]
    tm = 512

    def body(x_ref, g_ref, wt_ref, c_ref, s_ref, q_ref, k_ref, v_ref, gate_ref, ht_ref):
        xf = x_ref[...]
        r = lax.rsqrt(jnp.mean(xf * xf, axis=1, keepdims=True) + EPS)
        ht = (xf * r * g_ref[...]).astype(BF16).T
        ht_ref[...] = ht
        c, s = c_ref[...], s_ref[...]
        q_ref[...] = _rope_t(_mm(wt_ref[Q_ROWS[0]:Q_ROWS[1], :], ht), c, s, N_HEADS, 1).astype(BF16)
        k_ref[...] = _rope_t(_mm(wt_ref[K_ROWS[0]:K_ROWS[1], :], ht), c, s, N_KV, 1).astype(BF16)
        v_ref[...] = _mm(wt_ref[V_ROWS[0]:V_ROWS[1], :], ht).astype(BF16)
        gate_ref[...] = _mm(wt_ref[G_ROWS[0]:G_ROWS[1], :], ht).astype(BF16)

    col = lambda rows: pl.BlockSpec((rows, tm), lambda i: (0, i))
    return pl.pallas_call(
        body, grid=(seq // tm,), name="l1_in_proj",
        out_shape=(jax.ShapeDtypeStruct((D, seq), BF16), jax.ShapeDtypeStruct((KV_W, seq), BF16),
                   jax.ShapeDtypeStruct((KV_W, seq), BF16), jax.ShapeDtypeStruct((D, seq), BF16),
                   jax.ShapeDtypeStruct((D, seq), BF16)),
        in_specs=[pl.BlockSpec((tm, D), lambda i: (i, 0)), _resident((1, D)), _resident((MIX1_IN, D)), col(ROT_HALF),
                  col(ROT_HALF)],
        out_specs=(col(D), col(KV_W), col(KV_W), col(D), col(D)),
        compiler_params=_params(48),
    )(x1, g1, w_t, cos_t, sin_t)


def _band_specs_t(nb, clamp_i):
    per = TQ // BLK
    prev = pl.BlockSpec((KV_W, BLK), lambda i: (0, jnp.maximum(clamp_i(i) * per - 1, 0)))
    cur = pl.BlockSpec((KV_W, TQ), lambda i: (0, clamp_i(i)))
    nxt = pl.BlockSpec((KV_W, BLK), lambda i: (0, jnp.minimum((clamp_i(i) + 1) * per, nb - 1)))
    return [prev, cur, nxt]


def _fill_band(buf, p_ref, c_ref, n_ref):
    buf[:, 0:BLK] = p_ref[...]
    buf[:, BLK:BLK + TQ] = c_ref[...]
    buf[:, BLK + TQ:2 * BLK + TQ] = n_ref[...]


def _band_bias_t(n, nb):
    c = lax.broadcasted_iota(jnp.int32, (3 * BLK, BLK), 0)
    r = lax.broadcasted_iota(jnp.int32, (3 * BLK, BLK), 1)
    ok = (c >= r) & (c <= r + 2 * BLK) & ((c >= BLK) | (n > 0)) & ((c < 2 * BLK) | (n < nb - 1))
    bias = jnp.where(ok, 0.0, NEG_INF).astype(F32)
    return jnp.concatenate([bias] * GQA, axis=1)


def _heads_t(ref, kv, c0):
    return jnp.concatenate([ref[(kv * GQA + g) * HD:(kv * GQA + g + 1) * HD, c0:c0 + BLK] for g in range(GQA)], axis=1)


def _row4(ref, kv, c0):
    return jnp.concatenate([ref[kv * GQA + g:kv * GQA + g + 1, c0:c0 + BLK] for g in range(GQA)], axis=1)


def _sink_row(sink_ref, kv):
    return jnp.concatenate([jnp.full((1, BLK), sink_ref[kv * GQA + g], F32) for g in range(GQA)], axis=1)


def _l1_attn_fwd(qt, kt, vt, gatet, x1, tgt, wout, gf, sink):
    seq = x1.shape[0]
    nq, nb = seq // TQ, seq // BLK

    def body(q_ref, gate_ref, kp_ref, k_ref, kn_ref, vp_ref, v_ref, vn_ref, x1_ref, tgt_ref, wo_ref, gf_ref, sink_ref,
             dx2_ref, dx2b_ref, yt_ref, att_ref, lse_ref, loss_ref, dgf_ref, kbuf, vbuf, att_scr):
        i = pl.program_id(0)

        @pl.when(i == 0)
        def _():
            loss_ref[...] = jnp.zeros_like(loss_ref)
            dgf_ref[...] = jnp.zeros_like(dgf_ref)

        _fill_band(kbuf, kp_ref, k_ref, kn_ref)
        _fill_band(vbuf, vp_ref, v_ref, vn_ref)
        for j in range(TQ // BLK):
            c0 = j * BLK
            bias = _band_bias_t(i * (TQ // BLK) + j, nb)
            lse_rows = []
            for kv in range(N_KV):
                rows = slice(kv * HD, (kv + 1) * HD)
                q4 = _heads_t(q_ref, kv, c0)
                st = _tn(kbuf[rows, c0:c0 + 3 * BLK], q4) * SCALE + bias
                sk = _sink_row(sink_ref, kv)
                m = jnp.maximum(jnp.max(st, axis=0, keepdims=True), sk)
                p = jnp.exp(st - m)
                den = jnp.sum(p, axis=0, keepdims=True) + jnp.exp(sk - m)
                ot = _mm(vbuf[rows, c0:c0 + 3 * BLK], p.astype(BF16)) / den
                lse = m + jnp.log(den)
                for g in range(GQA):
                    h = kv * GQA + g
                    att_scr[h * HD:(h + 1) * HD, c0:c0 + BLK] = ot[:, g * BLK:(g + 1) * BLK]
                    lse_rows.append(lse[:, g * BLK:(g + 1) * BLK])
            lse_ref[:, c0:c0 + BLK] = jnp.concatenate(lse_rows, axis=0)

        att = att_scr[...]
        gate = gate_ref[...].astype(F32)
        yt = (att * (gate * jax.nn.sigmoid(gate))).astype(BF16)
        yt_ref[...] = yt
        att_ref[...] = att.astype(BF16)
        x2 = x1_ref[...] + _mm(yt.T, wo_ref[...])
        r = lax.rsqrt(jnp.mean(x2 * x2, axis=1, keepdims=True) + EPS)
        xn = x2 * r
        diff = xn * gf_ref[...] - tgt_ref[...]
        loss_ref[...] += 0.5 * jnp.sum(jnp.mean(diff * diff, axis=1, keepdims=True), axis=0, keepdims=True)
        dout = diff * (1.0 / D)
        dgf_ref[...] += jnp.sum(dout * xn, axis=0, keepdims=True)
        dxn = dout * gf_ref[...]
        dx2 = r * (dxn - xn * jnp.mean(dxn * xn, axis=1, keepdims=True))
        dx2_ref[...] = dx2
        dx2b_ref[...] = dx2.astype(BF16)

    ident = lambda i: i
    row = pl.BlockSpec((TQ, D), lambda i: (i, 0))
    col = lambda rows: pl.BlockSpec((rows, TQ), lambda i: (0, i))
    return pl.pallas_call(
        body, grid=(nq,), name="l1_attn_fwd",
        out_shape=(jax.ShapeDtypeStruct((seq, D), F32), jax.ShapeDtypeStruct((seq, D), BF16),
                   jax.ShapeDtypeStruct((D, seq), BF16), jax.ShapeDtypeStruct((D, seq), BF16),
                   jax.ShapeDtypeStruct((N_HEADS, seq), F32), jax.ShapeDtypeStruct((1, 1), F32),
                   jax.ShapeDtypeStruct((1, D), F32)),
        in_specs=[col(D), col(D)] + _band_specs_t(nb, ident) + _band_specs_t(nb, ident) + [
            row, row, _resident((D, D)), _resident((1, D)), pl.BlockSpec(memory_space=pltpu.SMEM)],
        out_specs=(row, row, col(D), col(D), col(N_HEADS), pl.BlockSpec((1, 1), lambda i: (0, 0)),
                   pl.BlockSpec((1, D), lambda i: (0, 0))),
        scratch_shapes=[pltpu.VMEM((KV_W, TQ + 2 * BLK), BF16), pltpu.VMEM((KV_W, TQ + 2 * BLK), BF16),
                        pltpu.VMEM((D, TQ), F32)],
        compiler_params=_params(56),
    )(qt, gatet, kt, kt, kt, vt, vt, vt, x1, tgt, wout, gf, sink)


def _l1_attn_bwd(dx2b, wout, qt, kt, vt, gatet, att, lse, sink):
    seq = dx2b.shape[0]
    nq, nb = seq // TQ, seq // BLK

    def body(dx_ref, wo_ref, q_ref, gate_ref, kp_ref, k_ref, kn_ref, vp_ref, v_ref, vn_ref, att_ref, lse_ref, sink_ref,
             dq_ref, dgate_ref, dk_ref, dv_ref, dsink_ref, kbuf, vbuf, dkacc, dvacc, dat_scr, delta_scr, dsacc):
        i = pl.program_id(0)

        @pl.when(i == 0)
        def _():
            dkacc[...] = jnp.zeros_like(dkacc)
            dvacc[...] = jnp.zeros_like(dvacc)
            dsacc[...] = jnp.zeros_like(dsacc)

        @pl.when(i > 0)
        def _():
            for acc in (dkacc, dvacc):
                acc[:, 0:2 * BLK] = acc[:, TQ:TQ + 2 * BLK]
                acc[:, 2 * BLK:2 * BLK + TQ] = jnp.zeros((KV_W, TQ), F32)

        @pl.when(i < nq)
        def _():
            _fill_band(kbuf, kp_ref, k_ref, kn_ref)
            _fill_band(vbuf, vp_ref, v_ref, vn_ref)
            dyt = _nt(wo_ref[...], dx_ref[...])
            sg, dsg = _silu_and_grad(gate_ref[...].astype(F32))
            attf = att_ref[...].astype(F32)
            dat = dyt * sg
            dat_scr[...] = dat.astype(BF16)
            dgate_ref[...] = (dyt * attf * dsg).astype(BF16)
            dl = dat * attf
            delta_scr[...] = jnp.concatenate(
                [jnp.sum(dl[h * HD:(h + 1) * HD, :], axis=0, keepdims=True) for h in range(N_HEADS)], axis=0)
            for j in range(TQ // BLK):
                c0 = j * BLK
                bias = _band_bias_t(i * (TQ // BLK) + j, nb)
                for kv in range(N_KV):
                    rows = slice(kv * HD, (kv + 1) * HD)
                    q4 = _heads_t(q_ref, kv, c0)
                    do4 = _heads_t(dat_scr, kv, c0)
                    lse4 = _row4(lse_ref, kv, c0)
                    delta4 = _row4(delta_scr, kv, c0)
                    kth = kbuf[rows, c0:c0 + 3 * BLK]
                    vth = vbuf[rows, c0:c0 + 3 * BLK]
                    p = jnp.exp(_tn(kth, q4) * SCALE + bias - lse4)
                    dp = _tn(vth, do4)
                    ds = (p * (dp - delta4) * SCALE).astype(BF16)
                    dq4 = _mm(kth, ds)
                    dkacc[rows, c0:c0 + 3 * BLK] += _nt(q4, ds)
                    dvacc[rows, c0:c0 + 3 * BLK] += _nt(do4, p.astype(BF16))
                    dsk = -jnp.exp(_sink_row(sink_ref, kv) - lse4) * delta4
                    for g in range(GQA):
                        h = kv * GQA + g
                        dq_ref[h * HD:(h + 1) * HD, c0:c0 + BLK] = dq4[:, g * BLK:(g + 1) * BLK].astype(BF16)
                        dsacc[h:h + 1, :] += dsk[:, g * BLK:(g + 1) * BLK]

        dk_ref[...] = dkacc[:, 0:TQ].astype(BF16)
        dv_ref[...] = dvacc[:, 0:TQ].astype(BF16)

        @pl.when(i == nq)
        def _():
            dsink_ref[...] = jnp.broadcast_to(jnp.sum(dsacc[...], axis=1, keepdims=True), (N_HEADS, LANES))

    clamp = lambda i: jnp.minimum(i, nq - 1)
    row = pl.BlockSpec((TQ, D), lambda i: (clamp(i), 0))
    col = lambda rows: pl.BlockSpec((rows, TQ), lambda i: (0, clamp(i)))
    pad = pl.BlockSpec((KV_W, TQ), lambda i: (0, i))
    return pl.pallas_call(
        body, grid=(nq + 1,), name="l1_attn_bwd",
        out_shape=(jax.ShapeDtypeStruct((D, seq), BF16), jax.ShapeDtypeStruct((D, seq), BF16),
                   jax.ShapeDtypeStruct((KV_W, seq + TQ), BF16), jax.ShapeDtypeStruct((KV_W, seq + TQ), BF16),
                   jax.ShapeDtypeStruct((N_HEADS, LANES), F32)),
        in_specs=[row, _resident((D, D)), col(D), col(D)] + _band_specs_t(nb, clamp) + _band_specs_t(nb, clamp) + [
            col(D), col(N_HEADS), pl.BlockSpec(memory_space=pltpu.SMEM)],
        out_specs=(col(D), col(D), pad, pad, pl.BlockSpec((N_HEADS, LANES), lambda i: (0, 0))),
        scratch_shapes=[pltpu.VMEM((KV_W, TQ + 2 * BLK), BF16), pltpu.VMEM((KV_W, TQ + 2 * BLK), BF16),
                        pltpu.VMEM((KV_W, TQ + 2 * BLK), F32), pltpu.VMEM((KV_W, TQ + 2 * BLK), F32),
                        pltpu.VMEM((D, TQ), BF16), pltpu.VMEM((N_HEADS, TQ), F32), pltpu.VMEM((N_HEADS, LANES), F32)],
        compiler_params=_params(56),
    )(dx2b, wout, qt, gatet, kt, kt, kt, vt, vt, vt, att, lse, sink)


def _l1_in_proj_bwd(dq_r, dk_r, dv, dgate, cos_t, sin_t, w, x1, g1, dx2):
    seq = x1.shape[0]
    tm = 512

    def body(dq_ref, dk_ref, dv_ref, dg_ref, c_ref, s_ref, w_ref, x_ref, g_ref, dres_ref,
             dx_ref, dxb_ref, dz_ref, dn_ref):
        @pl.when(pl.program_id(0) == 0)
        def _():
            dn_ref[...] = jnp.zeros_like(dn_ref)

        c, s = c_ref[...], s_ref[...]
        dq = _rope_t(dq_ref[...].astype(F32), c, s, N_HEADS, -1).astype(BF16)
        dk = _rope_t(dk_ref[...].astype(F32), c, s, N_KV, -1).astype(BF16)
        dz = jnp.concatenate([dq, dk, dv_ref[...], dg_ref[...]], axis=0)
        dz_ref[...] = dz
        half = MIX1_IN // 2
        dh = (_mm(w_ref[:, 0:half], dz[0:half]) + _mm(w_ref[:, half:MIX1_IN], dz[half:MIX1_IN])).T
        xf = x_ref[...]
        r = lax.rsqrt(jnp.mean(xf * xf, axis=1, keepdims=True) + EPS)
        xn = xf * r
        dn_ref[...] += jnp.sum(dh * xn, axis=0, keepdims=True)
        dxn = dh * g_ref[...]
        dx = dres_ref[...] + r * (dxn - xn * jnp.mean(dxn * xn, axis=1, keepdims=True))
        dx_ref[...] = dx
        dxb_ref[...] = dx.astype(BF16)

    row = pl.BlockSpec((tm, D), lambda i: (i, 0))
    col = lambda rows: pl.BlockSpec((rows, tm), lambda i: (0, i))
    return pl.pallas_call(
        body, grid=(seq // tm,), name="l1_in_proj_bwd",
        out_shape=(jax.ShapeDtypeStruct((seq, D), F32), jax.ShapeDtypeStruct((seq, D), BF16),
                   jax.ShapeDtypeStruct((MIX1_IN, seq), BF16), jax.ShapeDtypeStruct((1, D), F32)),
        in_specs=[col(D), col(KV_W), col(KV_W), col(D), col(ROT_HALF), col(ROT_HALF), _resident((D, MIX1_IN)), row,
                  _resident((1, D)), row],
        out_specs=(row, row, col(MIX1_IN), pl.BlockSpec((1, D), lambda i: (0, 0))),
        compiler_params=_params(48),
    )(dq_r, dk_r, dv, dgate, cos_t, sin_t, w, x1, g1, dx2)


def _l0_mix_bwd(dx1b, wout_t, za, bx, bg, ws, ws_t, bias, gv, wg, wg_t, scale):
    seq = dx1b.shape[0]
    ts = 256
    n_tiles = seq // ts

    def body(dx_ref, wot_ref, za_ref, bx_ref, bxp_ref, bxn_ref, bg_ref, ws_ref, wst_ref, bias_ref, gv_ref, wg_ref,
             wgt_ref, sc_ref,
             dz_ref, dp_ref, catt_ref, dws_ref, dbias_ref, dgv_ref, dsc_ref, dwg_ref, db_ref, xe_ref):
        i = pl.program_id(0)

        @pl.when(i == 0)
        def _():
            for r_ in (dws_ref, dbias_ref, dgv_ref, dsc_ref, dwg_ref, db_ref):
                r_[...] = jnp.zeros_like(r_)

        dxb = dx_ref[...]
        dya = _mm(dxb, wot_ref[:, 0:D])
        dyb = _mm(dxb, wot_ref[:, D:2 * D])

        u, du = _gelu_and_grad(za_ref[:, 0:D].astype(F32))
        vg, dvg_dz = _gelu_and_grad(za_ref[:, D:2 * D].astype(F32))
        rv = lax.rsqrt(jnp.mean(vg * vg, axis=1, keepdims=True) + EPS)
        vnorm = vg * rv
        gvw = gv_ref[...]
        vnb = (vnorm * gvw).astype(BF16)
        mixed = _spatial_mix(ws_ref, vnb, bias_ref[...], ts)
        sga, dsga = _silu_and_grad(za_ref[:, 2 * D:3 * D].astype(F32))
        um = u * mixed
        ya = (um * sga).astype(BF16)
        t = dya * sga
        dz_ref[:, 0:D] = (t * mixed * du).astype(BF16)
        dz_ref[:, 2 * D:3 * D] = (dya * um * dsga).astype(BF16)
        dmixed = t * u
        dmb = dmixed.astype(BF16)
        dvn_rows = []
        dbias = jnp.zeros((CHUNK, D), F32)
        for c in range(ts // CHUNK):
            rows = slice(c * CHUNK, (c + 1) * CHUNK)
            dbias = dbias + dmixed[rows, :]
            parts = []
            for h in range(A_GROUPS):
                cols = slice(h * GDIM, (h + 1) * GDIM)
                dws_ref[h] += _nt(dmb[rows, cols], vnb[rows, cols])
                parts.append(_mm(wst_ref[h], dmb[rows, cols]))
            dvn_rows.append(jnp.concatenate(parts, axis=1))
        dbias_ref[...] += dbias
        dvn = jnp.concatenate(dvn_rows, axis=0)
        dgv_ref[...] += jnp.sum(dvn * vnorm, axis=0, keepdims=True)
        dxn = dvn * gvw
        dvg = rv * (dxn - vnorm * jnp.mean(dxn * vnorm, axis=1, keepdims=True))
        dz_ref[:, D:2 * D] = (dvg * dvg_dz).astype(BF16)

        _fill_halo(xe_ref, bx_ref[...], bxp_ref, bxn_ref, i, n_tiles, ts)
        pb = _pool_forward(xe_ref, ts, i * ts, seq).astype(BF16)
        ypre = jnp.concatenate([_mm(pb[:, g * GDIM:(g + 1) * GDIM], wg_ref[g]) for g in range(4)], axis=1)
        sc = sc_ref[...]
        y = ypre * sc
        sgb, dsgb = _silu_and_grad(bg_ref[...].astype(F32))
        yb = (y * sgb).astype(BF16)
        dy_b = dyb * sgb
        dz_ref[:, 3 * D:4 * D] = jnp.zeros((ts, D), BF16)
        dz_ref[:, 4 * D:5 * D] = (dyb * y * dsgb).astype(BF16)
        dsc_ref[...] += jnp.sum(dy_b * ypre, axis=0, keepdims=True)
        dypre = (dy_b * sc).astype(BF16)
        dps = []
        for g in range(4):
            cols = slice(g * GDIM, (g + 1) * GDIM)
            dwg_ref[g] += _tn(pb[:, cols], dypre[:, cols])
            dps.append(_mm(dypre[:, cols], wgt_ref[g]))
        dp_ref[...] = jnp.concatenate(dps, axis=1)
        catt_ref[...] = jnp.concatenate([ya, yb], axis=1).T

        @pl.when(i == n_tiles - 1)
        def _():
            for h in range(A_GROUPS):
                tot = jnp.sum(dbias_ref[:, h * GDIM:(h + 1) * GDIM].T, axis=0, keepdims=True)
                db_ref[pl.ds(h * 8, 8), :] = jnp.broadcast_to(tot, (8, CHUNK))

    prev, nxt = _halo_specs(ts, seq, D)
    row = lambda w_: pl.BlockSpec((ts, w_), lambda i: (i, 0))
    acc = lambda shape: pl.BlockSpec(shape, lambda i: (0,) * len(shape))
    return pl.pallas_call(
        body, grid=(n_tiles,), name="l0_mix_bwd",
        out_shape=(jax.ShapeDtypeStruct((seq, MIX0_IN), BF16), jax.ShapeDtypeStruct((seq, D), F32),
                   jax.ShapeDtypeStruct((2 * D, seq), BF16),
                   jax.ShapeDtypeStruct((4, CHUNK, CHUNK), F32), jax.ShapeDtypeStruct((CHUNK, D), F32),
                   jax.ShapeDtypeStruct((1, D), F32), jax.ShapeDtypeStruct((1, D), F32),
                   jax.ShapeDtypeStruct((4, GDIM, GDIM), F32), jax.ShapeDtypeStruct((32, CHUNK), F32)),
        in_specs=[row(D), _resident((D, 2 * D)), row(3 * D), row(D), prev, nxt, row(D), _resident((4, CHUNK, CHUNK)),
                  _resident((4, CHUNK, CHUNK)), _resident((CHUNK, D)), _resident((1, D)), _resident((4, GDIM, GDIM)),
                  _resident((4, GDIM, GDIM)), _resident((1, D))],
        out_specs=(row(MIX0_IN), row(D), pl.BlockSpec((2 * D, ts), lambda i: (0, i)),
                   acc((4, CHUNK, CHUNK)), acc((CHUNK, D)), acc((1, D)), acc((1, D)), acc((4, GDIM, GDIM)),
                   acc((32, CHUNK))),
        scratch_shapes=[pltpu.VMEM((ts + 2 * POOL_HALO, D), F32)],
        compiler_params=_params(56),
    )(dx1b, wout_t, za, bx, bx, bx, bg, ws, ws_t, bias, gv, wg, wg_t, scale)


def _l0_pool_bwd(dp, dz):
    seq = dp.shape[0]
    ts = 512
    n_tiles = seq // ts
    ext = ts + 2 * POOL_HALO

    def body(dp_ref, dpp_ref, dpn_ref, dz_ref, out_ref, qe_ref):
        i = pl.program_id(0)
        _fill_halo(qe_ref, dp_ref[...], dpp_ref, dpn_ref, i, n_tiles, ts)
        te = i * ts - POOL_HALO + lax.broadcasted_iota(jnp.int32, (ext, 1), 0)
        for gi, w in enumerate(POOL_WINDOWS):
            hw = w // 2
            cols = slice(gi * GDIM, (gi + 1) * GDIM)
            cnt = jnp.maximum(jnp.minimum(te + hw, seq) - jnp.maximum(te - hw, 0), 1).astype(F32)
            qe_ref[:, cols] = qe_ref[:, cols] / cnt
        outs = []
        for gi, w in enumerate(POOL_WINDOWS):
            hw = w // 2
            cols = slice(gi * GDIM, (gi + 1) * GDIM)
            acc = qe_ref[pl.ds(POOL_HALO - hw + 1, ts), cols]
            for k in range(-hw + 2, hw + 1):
                acc = acc + qe_ref[pl.ds(POOL_HALO + k, ts), cols]
            outs.append(acc - dp_ref[:, cols])
        out_ref[...] = jnp.concatenate(outs, axis=1).astype(BF16)

    prev, nxt = _halo_specs(ts, seq, D)
    row = pl.BlockSpec((ts, D), lambda i: (i, 0))
    return pl.pallas_call(
        body, grid=(n_tiles,), name="l0_pool_bwd",
        out_shape=jax.ShapeDtypeStruct(dz.shape, BF16),
        in_specs=[row, prev, nxt, pl.BlockSpec(memory_space=pl.ANY)],
        out_specs=pl.BlockSpec((ts, D), lambda i: (i, 3)),
        input_output_aliases={3: 0},
        scratch_shapes=[pltpu.VMEM((ext, D), F32)],
        compiler_params=_params(32),
    )(dp, dp, dp, dz)


def _l0_in_proj_bwd(dz, w_t, x, g0, dx1):
    seq = x.shape[0]
    tm = 512

    def body(dz_ref, wt_ref, x_ref, g_ref, dres_ref, dx_ref, dn_ref):
        @pl.when(pl.program_id(0) == 0)
        def _():
            dn_ref[...] = jnp.zeros_like(dn_ref)

        dh = _mm(dz_ref[...], wt_ref[...])
        xf = x_ref[...]
        r = lax.rsqrt(jnp.mean(xf * xf, axis=1, keepdims=True) + EPS)
        xn = xf * r
        dn_ref[...] += jnp.sum(dh * xn, axis=0, keepdims=True)
        dxn = dh * g_ref[...]
        dx_ref[...] = dres_ref[...] + r * (dxn - xn * jnp.mean(dxn * xn, axis=1, keepdims=True))

    row = lambda w_: pl.BlockSpec((tm, w_), lambda i: (i, 0))
    return pl.pallas_call(
        body, grid=(seq // tm,), name="l0_in_proj_bwd",
        out_shape=(jax.ShapeDtypeStruct((seq, D), F32), jax.ShapeDtypeStruct((1, D), F32)),
        in_specs=[row(MIX0_IN), _resident((MIX0_IN, D)), row(D), _resident((1, D)), row(D)],
        out_specs=(row(D), pl.BlockSpec((1, D), lambda i: (0, 0))),
        compiler_params=_params(56),
    )(dz, w_t, x, g0, dx1)


def _dw_matmul(a_t, b, name, b_transposed=False, tn=1024):
    k, seq = a_t.shape
    n = b.shape[0] if b_transposed else b.shape[1]
    tn = min(n, tn)
    ts = 512

    def body(a_ref, b_ref, o_ref):
        @pl.when(pl.program_id(1) == 0)
        def _():
            o_ref[...] = jnp.zeros_like(o_ref)

        o_ref[...] += _nt(a_ref[...], b_ref[...]) if b_transposed else _mm(a_ref[...], b_ref[...])

    b_spec = (pl.BlockSpec((tn, ts), lambda j, s: (j, s)) if b_transposed else pl.BlockSpec((ts, tn), lambda j, s: (s, j)))
    return pl.pallas_call(
        body, grid=(n // tn, seq // ts), name=name,
        out_shape=jax.ShapeDtypeStruct((k, n), F32),
        in_specs=[pl.BlockSpec((k, ts), lambda j, s: (0, s)), b_spec],
        out_specs=pl.BlockSpec((k, tn), lambda j, s: (0, j)),
        compiler_params=_params(48, 2),
    )(a_t, b)


ROW_TILES = 8


def _cast_shards(shards):
    n = len(shards)

    def body(*refs):
        for a in range(n):
            refs[n + a][...] = refs[a][...].astype(BF16)

    vm = pl.BlockSpec(memory_space=pltpu.VMEM)
    return pl.pallas_call(body, name="cast_weights", out_shape=[jax.ShapeDtypeStruct(t.shape, BF16) for t in shards],
                          in_specs=[vm] * n, out_specs=[vm] * n, compiler_params=_params(32, 0))(*shards)


def _adamw_math(w, g, m, v):
    m2 = ADAM_B1 * m + (1.0 - ADAM_B1) * g
    v2 = ADAM_B2 * v + (1.0 - ADAM_B2) * (g * g)
    m_hat = m2 / (1.0 - ADAM_B1 ** ADAM_STEP)
    v_hat = v2 / (1.0 - ADAM_B2 ** ADAM_STEP)
    delta = -ADAM_LR * (m_hat / (jnp.sqrt(v_hat) + ADAM_EPS) + ADAM_WD * w)
    return delta, m2, v2


def _cast_blocks(g_list, name):
    n = len(g_list)

    def body(*refs):
        for a in range(n):
            refs[n + a][...] = refs[a][...].astype(BF16)

    specs = [pl.BlockSpec((None,) + g.shape[1:], lambda b: (b, 0, 0)) for g in g_list]
    return pl.pallas_call(body, name=name, grid=(N_DEV,), out_shape=[jax.ShapeDtypeStruct(g.shape, BF16) for g in g_list],
                          in_specs=specs, out_specs=specs, compiler_params=_params(32))(*g_list)


def _final_sum_adamw(g_list, recv_list, me, w_list, m_list, v_list):
    n = len(w_list)

    def body(me_ref, *refs):
        own, recv, w, m, v = (refs[k * n:(k + 1) * n] for k in range(5))
        outs = [refs[(5 + k) * n:(6 + k) * n] for k in range(4)]
        for a in range(n):
            g = own[a][...]
            for k in range(N_DEV - 1):
                g = g + recv[a][k].astype(F32)
            delta, m2, v2 = _adamw_math(w[a][...], g, m[a][...], v[a][...])
            for o_ref, val in zip((outs[0][a], outs[1][a], outs[2][a], outs[3][a]), (g, delta, m2, v2)):
                o_ref[...] = val

    own_specs, flat, wire, shapes = [], [], [], []
    for t in w_list:
        rows, width = t.shape
        tr = rows // ROW_TILES
        own_specs.append(pl.BlockSpec((None, tr, width), lambda i, me: (me[0], i, 0)))
        flat.append(pl.BlockSpec((tr, width), lambda i, me: (i, 0)))
        wire.append(pl.BlockSpec((N_DEV - 1, tr, width), lambda i, me: (0, i, 0)))
        shapes.append(jax.ShapeDtypeStruct((rows, width), F32))
    out = pl.pallas_call(
        body, name="grad_sum_adamw", out_shape=shapes * 4,
        grid_spec=pltpu.PrefetchScalarGridSpec(
            num_scalar_prefetch=1, grid=(ROW_TILES,), in_specs=own_specs + wire + flat * 3, out_specs=flat * 4),
        compiler_params=_params(40),
    )(me, *g_list, *recv_list, *w_list, *m_list, *v_list)
    return [out[k * n:(k + 1) * n] for k in range(4)]


SMALL_NAMES = ("norm_0", "a_v_norm_0", "b_scale_0", "norm_1", "final_norm", "a_spatial_w_0", "a_spatial_b_0", "sink_1")
SMALL_VIEWS = ((8, LANES),) * 5 + ((4 * CHUNK, LANES), (4, LANES), (1, N_HEADS))
SMALL_ROW0 = (0, 8, 16, 24, 32, 40, 552, 560)
SMALL_ROWS = 568


def _small_sum_adamw(gs, rs, me, w_list, m_list, v_list):
    n = len(w_list)

    def body(me_ref, g_ref, r_ref, *refs):
        stack = refs[-1]
        stack[0] = g_ref[...]
        for k in range(N_DEV - 1):
            stack[k + 1] = r_ref[k]
        me = me_ref[0]
        gtot = stack[me]
        for a in range(1, N_DEV):
            gtot = gtot + stack[me ^ a]
        for a, ((rows, width), r0) in enumerate(zip(SMALL_VIEWS, SMALL_ROW0)):
            g = gtot[r0:r0 + rows, 0:width]
            delta, m2, v2 = _adamw_math(refs[a][...], g, refs[n + a][...], refs[2 * n + a][...])
            for k, val in enumerate((g, delta, m2, v2)):
                refs[(3 + k) * n + a][...] = val

    vm = pl.BlockSpec(memory_space=pltpu.VMEM)
    shapes = [jax.ShapeDtypeStruct(s, F32) for s in SMALL_VIEWS]
    out = pl.pallas_call(
        body, name="small_sum_adamw", out_shape=shapes * 4,
        in_specs=[pl.BlockSpec(memory_space=pltpu.SMEM), vm, vm] + [vm] * (3 * n), out_specs=[vm] * (4 * n),
        scratch_shapes=[pltpu.VMEM((N_DEV, SMALL_ROWS, LANES), F32)],
    )(me, gs, rs, *w_list, *m_list, *v_list)
    return [out[k * n:(k + 1) * n] for k in range(4)]


def _all_gather(blks):
    n = len(blks)

    def body(*refs):
        ins, outs = refs[:n], refs[n:2 * n]
        send_sems, recv_sems, local_sems = refs[2 * n:]
        x, y, c = lax.axis_index("x"), lax.axis_index("y"), lax.axis_index("c")
        me, sibling = (x, y, c), (x, y, 1 - c)
        chips = [(1 - x, y), (x, 1 - y), (1 - x, 1 - y)]

        def slot(a, px, py, pc):
            return outs[a].at[4 * px + 2 * py + pc]

        def copy(k, a, block, to, from_input=False):
            return pltpu.make_async_remote_copy(
                src_ref=ins[a] if from_input else slot(a, *block), dst_ref=slot(a, *block),
                send_sem=send_sems.at[k, a], recv_sem=recv_sems.at[k, a], device_id=to, device_id_type=MESH)

        mine = [pltpu.make_async_copy(ins[a], slot(a, *me), local_sems.at[a]) for a in range(n)]
        first = []
        for a in range(n):
            first.append(copy(0, a, me, sibling, from_input=True))
            first += [copy(1 + j, a, me, (*chip, c), from_input=True) for j, chip in enumerate(chips)]
        for cp in mine + first:
            cp.start()
        passed = []
        for j, chip in enumerate(chips):
            for a in range(n):
                copy(1 + j, a, (*chip, c), me).wait_recv()
                passed.append(copy(4 + j, a, (*chip, c), sibling))
                passed[-1].start()
        for a in range(n):
            copy(0, a, sibling, me).wait_recv()
        for j, chip in enumerate(chips):
            for a in range(n):
                copy(4 + j, a, (*chip, 1 - c), me).wait_recv()
        for cp in first + passed:
            cp.wait_send()
        for cp in mine:
            cp.wait()

    any_spec = pl.BlockSpec(memory_space=pl.ANY)
    return pl.pallas_call(
        body, name="weights_all_gather", out_shape=[jax.ShapeDtypeStruct((N_DEV,) + t.shape, t.dtype) for t in blks],
        in_specs=[any_spec] * n, out_specs=[any_spec] * n,
        scratch_shapes=[pltpu.SemaphoreType.DMA((7, n)), pltpu.SemaphoreType.DMA((7, n)), pltpu.SemaphoreType.DMA((n,))],
    )(*blks)


PEER_FLIPS = tuple((fx, fy, fc) for fx in (0, 1) for fy in (0, 1) for fc in (0, 1))[1:]


def _sequencer_all_gather(blks, name, collective_id):
    n = len(blks)

    def body(*refs):
        ins, outs = refs[:n], refs[n:2 * n]
        send_sems, recv_sems, local_sems = refs[2 * n:]
        x, y, c = lax.axis_index("x"), lax.axis_index("y"), lax.axis_index("c")
        peers = [(x ^ fx, y ^ fy, c ^ fc) for fx, fy, fc in PEER_FLIPS]
        barrier = pltpu.get_barrier_semaphore()
        for peer in peers:
            pl.semaphore_signal(barrier, inc=1, device_id=peer, device_id_type=MESH)
        pl.semaphore_wait(barrier, len(peers))
        me = 4 * x + 2 * y + c
        copies = [pltpu.make_async_remote_copy(
            src_ref=ins[a], dst_ref=outs[a].at[me], send_sem=send_sems.at[k, a], recv_sem=recv_sems.at[k, a],
            device_id=peer, device_id_type=MESH) for k, peer in enumerate(peers) for a in range(n)]
        mine = [pltpu.make_async_copy(ins[a], outs[a].at[me], local_sems.at[a]) for a in range(n)]
        for cp in copies + mine:
            cp.start()
        for cp in copies + mine:
            cp.wait()

    return pl.kernel(
        body, out_type=[jax.ShapeDtypeStruct((N_DEV,) + t.shape, t.dtype) for t in blks],
        mesh=plsc.ScalarSubcoreMesh(axis_name="sequencer", num_cores=1), name=name,
        scratch_types=[pltpu.SemaphoreType.DMA((7, n)), pltpu.SemaphoreType.DMA((7, n)), pltpu.SemaphoreType.DMA((n,))],
        compiler_params=pltpu.CompilerParams(collective_id=collective_id),
    )(*blks)


def _sequencer_scatter(g_list, name, collective_id):
    n = len(g_list)

    def body(*refs):
        ins, outs = refs[:n], refs[n:2 * n]
        send_sems, recv_sems = refs[2 * n:]
        x, y, c = lax.axis_index("x"), lax.axis_index("y"), lax.axis_index("c")
        peers = [(x ^ fx, y ^ fy, c ^ fc) for fx, fy, fc in PEER_FLIPS]
        barrier = pltpu.get_barrier_semaphore()
        for peer in peers:
            pl.semaphore_signal(barrier, inc=1, device_id=peer, device_id_type=MESH)
        pl.semaphore_wait(barrier, len(peers))
        copies = [pltpu.make_async_remote_copy(
            src_ref=ins[a].at[4 * px + 2 * py + pc], dst_ref=outs[a].at[k], send_sem=send_sems.at[k, a],
            recv_sem=recv_sems.at[k, a], device_id=(px, py, pc), device_id_type=MESH)
            for k, (px, py, pc) in enumerate(peers) for a in range(n)]
        for cp in copies:
            cp.start()
        for cp in copies:
            cp.wait()

    return pl.kernel(
        body, out_type=[jax.ShapeDtypeStruct((N_DEV - 1,) + g.shape[1:], g.dtype) for g in g_list],
        mesh=plsc.ScalarSubcoreMesh(axis_name="sequencer", num_cores=1), name=name,
        scratch_types=[pltpu.SemaphoreType.DMA((7, n)), pltpu.SemaphoreType.DMA((7, n))],
        compiler_params=pltpu.CompilerParams(collective_id=collective_id),
    )(*g_list)


def _small_exchange(gs):
    def body(g_ref, out_ref, send_sems, recv_sems):
        x, y, c = lax.axis_index("x"), lax.axis_index("y"), lax.axis_index("c")
        copies = [pltpu.make_async_remote_copy(
            src_ref=g_ref, dst_ref=out_ref.at[k], send_sem=send_sems.at[k], recv_sem=recv_sems.at[k],
            device_id=(x ^ fx, y ^ fy, c ^ fc), device_id_type=MESH) for k, (fx, fy, fc) in enumerate(PEER_FLIPS)]
        for cp in copies:
            cp.start()
        for cp in copies:
            cp.wait()

    any_spec = pl.BlockSpec(memory_space=pl.ANY)
    return pl.pallas_call(
        body, name="small_grad_exchange", out_shape=jax.ShapeDtypeStruct((N_DEV - 1,) + gs.shape, F32),
        in_specs=[any_spec], out_specs=any_spec,
        scratch_shapes=[pltpu.SemaphoreType.DMA((7,)), pltpu.SemaphoreType.DMA((7,))],
    )(gs)


def _shard_views(w_in_0, b_group_w_0, w_out_0, w_in_1, w_out_1):
    return [w_in_0, b_group_w_0.reshape(4 * 32, GDIM), w_out_0, w_in_1, w_out_1]


def _small_views(named):
    return [named[name].reshape(view) for name, view in zip(SMALL_NAMES, SMALL_VIEWS)]


def _pack_small_grads(named):
    rows = []
    for name, (r, w) in zip(SMALL_NAMES, SMALL_VIEWS):
        t = named[name].reshape(r, w)
        pad_r = -r % 8
        rows.append(jnp.pad(t, ((0, pad_r), (0, LANES - w))))
    return jnp.concatenate(rows, axis=0)


def _device_blocks(t, axis):
    shape = t.shape
    t = t.reshape(shape[:axis] + (N_DEV, shape[axis] // N_DEV) + shape[axis + 1:])
    t = jnp.moveaxis(t, axis, 0)
    return t.reshape(N_DEV, -1, shape[-1] if axis != len(shape) - 1 else shape[-1] // N_DEV)


def kernel(x, norm_0, w_in_0, a_v_norm_0, a_spatial_w_0, a_spatial_b_0, b_group_w_0, b_scale_0, w_out_0, norm_1, w_in_1, sink_1, w_out_1, final_norm, loss_target, m_norm_0, m_w_in_0, m_a_v_norm_0, m_a_spatial_w_0, m_a_spatial_b_0, m_b_group_w_0, m_b_scale_0, m_w_out_0, m_norm_1, m_w_in_1, m_sink_1, m_w_out_1, m_final_norm, v_norm_0, v_w_in_0, v_a_v_norm_0, v_a_spatial_w_0, v_a_spatial_b_0, v_b_group_w_0, v_b_scale_0, v_w_out_0, v_norm_1, v_w_in_1, v_sink_1, v_w_out_1, v_final_norm):
    seq = x.shape[1]
    xs = x.reshape(seq, D)
    tgt = loss_target.reshape(seq, D)
    ax, ay, ac = lax.axis_index("x"), lax.axis_index("y"), lax.axis_index("c")
    me = jnp.reshape(4 * ax + 2 * ay + ac, (1,)).astype(jnp.int32)

    shards = _shard_views(w_in_0, b_group_w_0, w_out_0, w_in_1, w_out_1)
    cast = _cast_shards(shards)
    gathered = (list(_all_gather(cast[0:1])) + list(_sequencer_all_gather(cast[1:3], "weights_gather_a", 1))
                + list(_sequencer_all_gather(cast[3:5], "weights_gather_b", 2)))

    blocks, received = {}, {}

    def scatter(tag, grads):
        if tag == "l1":
            d_win1, d_wout1 = grads
            blocks[tag] = [_device_blocks(d_win1, 1), _device_blocks(d_wout1, 0)]
            received[tag] = _sequencer_scatter(blocks[tag], "grad_scatter_l1", 3)
        elif tag == "in0":
            d_win0, d_wg = grads
            blocks[tag] = [_device_blocks(d_win0, 1), _device_blocks(d_wg, 1)]
            received[tag] = _sequencer_scatter(_cast_blocks(blocks[tag], "cast_grad_in0"), "grad_scatter_in0", 4)
        else:
            blocks[tag] = [_device_blocks(grads[0], 0)]
            received[tag] = _sequencer_scatter(blocks[tag], "grad_scatter_out0", 5)

    loss_part, grad_x, small_grads = _local_step(
        xs, tgt, gathered, norm_0, a_v_norm_0, a_spatial_w_0, a_spatial_b_0, b_scale_0, norm_1, sink_1, final_norm, scatter)

    order = (("in0", 0), ("in0", 1), ("out0", 0), ("l1", 0), ("l1", 1))
    gs = _pack_small_grads(small_grads)
    rs = _small_exchange(gs)
    shards_late, _ = lax.optimization_barrier((shards, grad_x))
    big = _final_sum_adamw([blocks[t][i] for t, i in order], [received[t][i] for t, i in order], me, shards_late,
                           _shard_views(m_w_in_0, m_b_group_w_0, m_w_out_0, m_w_in_1, m_w_out_1),
                           _shard_views(v_w_in_0, v_b_group_w_0, v_w_out_0, v_w_in_1, v_w_out_1))
    weights = dict(norm_0=norm_0, a_v_norm_0=a_v_norm_0, a_spatial_w_0=a_spatial_w_0, a_spatial_b_0=a_spatial_b_0,
                   b_scale_0=b_scale_0, norm_1=norm_1, sink_1=sink_1, final_norm=final_norm)
    m_small = dict(norm_0=m_norm_0, a_v_norm_0=m_a_v_norm_0, a_spatial_w_0=m_a_spatial_w_0, a_spatial_b_0=m_a_spatial_b_0,
                   b_scale_0=m_b_scale_0, norm_1=m_norm_1, sink_1=m_sink_1, final_norm=m_final_norm)
    v_small = dict(norm_0=v_norm_0, a_v_norm_0=v_a_v_norm_0, a_spatial_w_0=v_a_spatial_w_0, a_spatial_b_0=v_a_spatial_b_0,
                   b_scale_0=v_b_scale_0, norm_1=v_norm_1, sink_1=v_sink_1, final_norm=v_final_norm)
    small = _small_sum_adamw(gs, rs, me, _small_views(weights), _small_views(m_small), _small_views(v_small))

    def in_order(kind):
        b = [b_.reshape(s_.shape) for b_, s_ in zip(big[kind], (w_in_0, b_group_w_0, w_out_0, w_in_1, w_out_1))]
        s = {name: t.reshape(weights[name].shape) for name, t in zip(SMALL_NAMES, small[kind])}
        return [s["norm_0"], b[0], s["a_v_norm_0"], s["a_spatial_w_0"], s["a_spatial_b_0"], b[1], s["b_scale_0"], b[2],
                s["norm_1"], b[3], s["sink_1"], b[4], s["final_norm"]]

    loss = lax.psum(loss_part[0, 0], ("x", "y", "c"))
    return (loss, grad_x.reshape(1, seq, D), *in_order(0), *in_order(1), *in_order(2), *in_order(3))


def _local_step(xs, tgt, gathered, norm_0, a_v_norm_0, a_spatial_w_0, a_spatial_b_0, b_scale_0, norm_1, sink_1, final_norm,
                scatter):
    seq = xs.shape[0]
    ws = a_spatial_w_0.astype(BF16)
    ws_t = jnp.swapaxes(ws, 1, 2)
    bias = jnp.repeat(a_spatial_b_0.T, GDIM, axis=1)
    g0, gv, scale, g1, gf = (t.reshape(1, D) for t in (norm_0, a_v_norm_0, b_scale_0, norm_1, final_norm))
    cos_t, sin_t = _rope_tables_t(seq)

    win0 = gathered[0].transpose(1, 0, 2).reshape(D, MIX0_IN)
    win0_t = win0.T
    za, bx, bg, h0_t = _l0_in_proj(xs, g0, win0)
    g_wg, g_wout0, za = lax.optimization_barrier((gathered[1], gathered[2], za))
    wg = g_wg.reshape(N_DEV, 4, 32, GDIM).transpose(1, 0, 2, 3).reshape(4, GDIM, GDIM)
    wg_t = jnp.swapaxes(wg, 1, 2)
    wout0 = g_wout0.reshape(2 * D, D)
    wout0_t = wout0.T
    x1 = _l0_mix_fwd(za, bx, bg, xs, ws, bias, gv, wg, scale, wout0)
    g_win1, g_wout1, x1 = lax.optimization_barrier((gathered[3], gathered[4], x1))
    win1 = g_win1.transpose(1, 0, 2).reshape(D, MIX1_IN)
    win1_t = win1.T
    wout1 = g_wout1.reshape(D, D)
    qt, kt, vt, gatet, h1_t = _l1_in_proj(x1, g1, win1_t, cos_t, sin_t)
    dx2, dx2b, y_t, att, lse, loss_part, d_gf = _l1_attn_fwd(qt, kt, vt, gatet, x1, tgt, wout1, gf, sink_1)

    d_wout1 = _dw_matmul(y_t, dx2b, "dw_out_1")
    dq_r, dgate, dk_pad, dv_pad, d_sink = _l1_attn_bwd(dx2b, wout1, qt, kt, vt, gatet, att, lse, sink_1)
    dk_r = dk_pad[:, BLK:BLK + seq]
    dv = dv_pad[:, BLK:BLK + seq]
    dx1, dx1b, dz1_t, d_g1 = _l1_in_proj_bwd(dq_r, dk_r, dv, dgate, cos_t, sin_t, win1, x1, g1, dx2)
    d_win1 = _dw_matmul(h1_t, dz1_t, "dw_in_1", b_transposed=True, tn=1280)
    scatter("l1", (d_win1, d_wout1))

    dz0, dp, cat_t, d_ws, _, d_gv, d_scale, d_wg, d_b = _l0_mix_bwd(
        dx1b, wout0_t, za, bx, bg, ws, ws_t, bias, gv, wg, wg_t, scale)
    dz0 = _l0_pool_bwd(dp, dz0)
    d_win0 = _dw_matmul(h0_t, dz0, "dw_in_0")
    scatter("in0", (d_win0, d_wg))
    cat_t, _ = lax.optimization_barrier((cat_t, d_win0))
    d_wout0 = _dw_matmul(cat_t, dx1b, "dw_out_0")
    scatter("out0", (d_wout0,))
    dz0, _ = lax.optimization_barrier((dz0, d_wout0))
    grad_x, d_g0 = _l0_in_proj_bwd(dz0, win0_t, xs, g0, dx1)

    small_grads = dict(norm_0=d_g0, a_v_norm_0=d_gv, a_spatial_w_0=d_ws, a_spatial_b_0=d_b.reshape(4, 8, CHUNK)[:, 0, :],
                       b_scale_0=d_scale, norm_1=d_g1, sink_1=d_sink[:, 0], final_norm=d_gf)
    return loss_part, grad_x, small_grads
```

```python
import jax
import jax.numpy as jnp
from jax import lax
from jax.experimental import pallas as pl
from jax.experimental.pallas import tpu as pltpu
from jax.experimental.pallas import tpu_sc as plsc

F32 = jnp.float32
BF16 = jnp.bfloat16

D = 1024
EPS = 1e-6
NEG_INF = -1e30
CHUNK = 128
A_GROUPS = 4
POOL_WINDOWS = (2, 4, 8, 16)
POOL_HALO = 8
GDIM = 256
N_HEADS = 16
N_KV = 4
GQA = 4
HD = 64
BLK = 128
ROT_HALF = 8
ROPE_THETA = 500000.0
SCALE = HD ** -0.5
MIX0_IN = 5 * D
MIX1_IN = 2560
KV_W = N_KV * HD
Q_ROWS, K_ROWS, V_ROWS, G_ROWS = (0, D), (D, D + KV_W), (D + KV_W, D + 2 * KV_W), (D + 2 * KV_W, MIX1_IN)
TQ = 512

ADAM_LR = 0.001
ADAM_B1 = 0.9
ADAM_B2 = 0.999
ADAM_EPS = 1e-08
ADAM_WD = 0.01
ADAM_STEP = 10

N_DEV = 8
LANES = 128
MIB = 2 ** 20
MESH = pl.DeviceIdType.MESH


def _params(limit_mib, n_axes=1):
    return pltpu.CompilerParams(vmem_limit_bytes=limit_mib * MIB, dimension_semantics=("arbitrary",) * n_axes)


def _resident(shape):
    nd = len(shape)
    return pl.BlockSpec(shape, lambda *_: (0,) * nd, pipeline_mode=pl.Buffered(1))


def _gelu(x):
    k = 0.7978845608028654
    return 0.5 * x * (1.0 + jnp.tanh(k * (x + 0.044715 * x * x * x)))


def _gelu_and_grad(x):
    k = 0.7978845608028654
    x2 = x * x
    t = jnp.tanh(k * (x + 0.044715 * x * x2))
    g = 0.5 * x * (1.0 + t)
    dg = 0.5 * (1.0 + t) + 0.5 * x * (1.0 - t * t) * (k * (1.0 + 3.0 * 0.044715 * x2))
    return g, dg


def _silu_and_grad(x):
    s = jax.nn.sigmoid(x)
    return x * s, s * (1.0 + x * (1.0 - s))


def _nt(a, b):
    return lax.dot_general(a, b, (((1,), (1,)), ((), ())), preferred_element_type=F32)


def _tn(a, b):
    return lax.dot_general(a, b, (((0,), (0,)), ((), ())), preferred_element_type=F32)


def _mm(a, b):
    return jnp.dot(a, b, preferred_element_type=F32)


def _rope_tables_t(seq):
    inv = ROPE_THETA ** (-jnp.arange(0, 2 * ROT_HALF, 2, dtype=F32) / (2 * ROT_HALF))
    ang = inv[:, None] * jnp.arange(seq, dtype=F32)[None, :]
    return jnp.cos(ang), jnp.sin(ang)


def _rope_t(z, c, s, n_heads, sign):
    parts = []
    for h in range(n_heads):
        b = h * HD
        x1, x2 = z[b:b + ROT_HALF], z[b + ROT_HALF:b + 2 * ROT_HALF]
        if sign > 0:
            parts += [x1 * c - x2 * s, x2 * c + x1 * s]
        else:
            parts += [x1 * c + x2 * s, x2 * c - x1 * s]
        parts.append(z[b + 2 * ROT_HALF:b + HD])
    return jnp.concatenate(parts, axis=0)


def _l0_in_proj(x, g0, w):
    seq = x.shape[0]
    tm = 512

    def body(x_ref, g_ref, w_ref, za_ref, bx_ref, bg_ref, ht_ref):
        xf = x_ref[...]
        r = lax.rsqrt(jnp.mean(xf * xf, axis=1, keepdims=True) + EPS)
        h = (xf * r * g_ref[...]).astype(BF16)
        ht_ref[...] = h.T
        for j in range(3):
            za_ref[:, j * D:(j + 1) * D] = _mm(h, w_ref[:, j * D:(j + 1) * D]).astype(BF16)
        bx_ref[...] = _mm(h, w_ref[:, 3 * D:4 * D])
        bg_ref[...] = _mm(h, w_ref[:, 4 * D:5 * D]).astype(BF16)

    return pl.pallas_call(
        body, grid=(seq // tm,), name="l0_in_proj",
        out_shape=(jax.ShapeDtypeStruct((seq, 3 * D), BF16), jax.ShapeDtypeStruct((seq, D), F32),
                   jax.ShapeDtypeStruct((seq, D), BF16), jax.ShapeDtypeStruct((D, seq), BF16)),
        in_specs=[pl.BlockSpec((tm, D), lambda i: (i, 0)), _resident((1, D)), _resident((D, MIX0_IN))],
        out_specs=(pl.BlockSpec((tm, 3 * D), lambda i: (i, 0)), pl.BlockSpec((tm, D), lambda i: (i, 0)),
                   pl.BlockSpec((tm, D), lambda i: (i, 0)), pl.BlockSpec((D, tm), lambda i: (0, i))),
        compiler_params=_params(48),
    )(x, g0, w)


def _fill_halo(ext_ref, cur, prev_ref, next_ref, i, n_tiles, ts):
    ext_ref[pl.ds(0, POOL_HALO), :] = jnp.where(i > 0, prev_ref[...], 0.0)
    ext_ref[pl.ds(POOL_HALO, ts), :] = cur
    ext_ref[pl.ds(POOL_HALO + ts, POOL_HALO), :] = jnp.where(i < n_tiles - 1, next_ref[...], 0.0)


def _pool_forward(xe_ref, ts, t0, seq):
    tg = t0 + lax.broadcasted_iota(jnp.int32, (ts, 1), 0)
    outs = []
    for gi, w in enumerate(POOL_WINDOWS):
        hw = w // 2
        cols = slice(gi * GDIM, (gi + 1) * GDIM)
        acc = xe_ref[pl.ds(POOL_HALO - hw, ts), cols]
        for k in range(-hw + 1, hw):
            acc = acc + xe_ref[pl.ds(POOL_HALO + k, ts), cols]
        cnt = (jnp.minimum(tg + hw, seq) - jnp.maximum(tg - hw, 0)).astype(F32)
        outs.append(acc / cnt - xe_ref[pl.ds(POOL_HALO, ts), cols])
    return jnp.concatenate(outs, axis=1)


def _spatial_mix(ws_ref, vnb, bias, ts):
    rows = []
    for c in range(ts // CHUNK):
        vc = vnb[c * CHUNK:(c + 1) * CHUNK, :]
        rows.append(jnp.concatenate(
            [_mm(ws_ref[h], vc[:, h * GDIM:(h + 1) * GDIM]) for h in range(A_GROUPS)], axis=1) + bias)
    return jnp.concatenate(rows, axis=0)


def _halo_specs(ts, seq, width):
    per = ts // POOL_HALO
    last = seq // POOL_HALO - 1
    prev = pl.BlockSpec((POOL_HALO, width), lambda i: (jnp.maximum(i * per - 1, 0), 0))
    nxt = pl.BlockSpec((POOL_HALO, width), lambda i: (jnp.minimum((i + 1) * per, last), 0))
    return prev, nxt


def _l0_mix_fwd(za, bx, bg, x, ws, bias, gv, wg, scale, wout):
    seq = x.shape[0]
    ts = 512
    n_tiles = seq // ts

    def body(za_ref, bx_ref, bxp_ref, bxn_ref, bg_ref, x_ref, ws_ref, bias_ref, gv_ref, wg_ref, sc_ref, wo_ref,
             x1_ref, xe_ref):
        i = pl.program_id(0)
        u = _gelu(za_ref[:, 0:D].astype(F32))
        vg = _gelu(za_ref[:, D:2 * D].astype(F32))
        rv = lax.rsqrt(jnp.mean(vg * vg, axis=1, keepdims=True) + EPS)
        vnb = (vg * rv * gv_ref[...]).astype(BF16)
        mixed = _spatial_mix(ws_ref, vnb, bias_ref[...], ts)
        ag = za_ref[:, 2 * D:3 * D].astype(F32)
        ya = (u * mixed * (ag * jax.nn.sigmoid(ag))).astype(BF16)

        _fill_halo(xe_ref, bx_ref[...], bxp_ref, bxn_ref, i, n_tiles, ts)
        pb = _pool_forward(xe_ref, ts, i * ts, seq).astype(BF16)
        y = jnp.concatenate([_mm(pb[:, g * GDIM:(g + 1) * GDIM], wg_ref[g]) for g in range(4)], axis=1) * sc_ref[...]
        bgf = bg_ref[...].astype(F32)
        yb = (y * (bgf * jax.nn.sigmoid(bgf))).astype(BF16)
        x1_ref[...] = x_ref[...] + _mm(ya, wo_ref[0:D, :]) + _mm(yb, wo_ref[D:2 * D, :])

    prev, nxt = _halo_specs(ts, seq, D)
    row = lambda w: pl.BlockSpec((ts, w), lambda i: (i, 0))
    return pl.pallas_call(
        body, grid=(n_tiles,), name="l0_mix_fwd",
        out_shape=jax.ShapeDtypeStruct((seq, D), F32),
        in_specs=[row(3 * D), row(D), prev, nxt, row(D), row(D), _resident((4, CHUNK, CHUNK)), _resident((CHUNK, D)),
                  _resident((1, D)), _resident((4, GDIM, GDIM)), _resident((1, D)), _resident((2 * D, D))],
        out_specs=row(D),
        scratch_shapes=[pltpu.VMEM((ts + 2 * POOL_HALO, D), F32)],
        compiler_params=_params(56),
    )(za, bx, bx, bx, bg, x, ws, bias, gv, wg, scale, wout)


def _l1_in_proj(x1, g1, w_t, cos_t, sin_t):
    seq = x1.shape[0]
    tm = 512

    def body(x_ref, g_ref, wt_ref, c_ref, s_ref, q_ref, k_ref, v_ref, gate_ref, ht_ref):
        xf = x_ref[...]
        r = lax.rsqrt(jnp.mean(xf * xf, axis=1, keepdims=True) + EPS)
        ht = (xf * r * g_ref[...]).astype(BF16).T
        ht_ref[...] = ht
        c, s = c_ref[...], s_ref[...]
        q_ref[...] = _rope_t(_mm(wt_ref[Q_ROWS[0]:Q_ROWS[1], :], ht), c, s, N_HEADS, 1).astype(BF16)
        k_ref[...] = _rope_t(_mm(wt_ref[K_ROWS[0]:K_ROWS[1], :], ht), c, s, N_KV, 1).astype(BF16)
        v_ref[...] = _mm(wt_ref[V_ROWS[0]:V_ROWS[1], :], ht).astype(BF16)
        gate_ref[...] = _mm(wt_ref[G_ROWS[0]:G_ROWS[1], :], ht).astype(BF16)

    col = lambda rows: pl.BlockSpec((rows, tm), lambda i: (0, i))
    return pl.pallas_call(
        body, grid=(seq // tm,), name="l1_in_proj",
        out_shape=(jax.ShapeDtypeStruct((D, seq), BF16), jax.ShapeDtypeStruct((KV_W, seq), BF16),
                   jax.ShapeDtypeStruct((KV_W, seq), BF16), jax.ShapeDtypeStruct((D, seq), BF16),
                   jax.ShapeDtypeStruct((D, seq), BF16)),
        in_specs=[pl.BlockSpec((tm, D), lambda i: (i, 0)), _resident((1, D)), _resident((MIX1_IN, D)), col(ROT_HALF),
                  col(ROT_HALF)],
        out_specs=(col(D), col(KV_W), col(KV_W), col(D), col(D)),
        compiler_params=_params(48),
    )(x1, g1, w_t, cos_t, sin_t)


def _band_specs_t(nb, clamp_i):
    per = TQ // BLK
    prev = pl.BlockSpec((KV_W, BLK), lambda i: (0, jnp.maximum(clamp_i(i) * per - 1, 0)))
    cur = pl.BlockSpec((KV_W, TQ), lambda i: (0, clamp_i(i)))
    nxt = pl.BlockSpec((KV_W, BLK), lambda i: (0, jnp.minimum((clamp_i(i) + 1) * per, nb - 1)))
    return [prev, cur, nxt]


def _fill_band(buf, p_ref, c_ref, n_ref):
    buf[:, 0:BLK] = p_ref[...]
    buf[:, BLK:BLK + TQ] = c_ref[...]
    buf[:, BLK + TQ:2 * BLK + TQ] = n_ref[...]


def _band_bias_t(n, nb):
    c = lax.broadcasted_iota(jnp.int32, (3 * BLK, BLK), 0)
    r = lax.broadcasted_iota(jnp.int32, (3 * BLK, BLK), 1)
    ok = (c >= r) & (c <= r + 2 * BLK) & ((c >= BLK) | (n > 0)) & ((c < 2 * BLK) | (n < nb - 1))
    bias = jnp.where(ok, 0.0, NEG_INF).astype(F32)
    return jnp.concatenate([bias] * GQA, axis=1)


def _heads_t(ref, kv, c0):
    return jnp.concatenate([ref[(kv * GQA + g) * HD:(kv * GQA + g + 1) * HD, c0:c0 + BLK] for g in range(GQA)], axis=1)


def _row4(ref, kv, c0):
    return jnp.concatenate([ref[kv * GQA + g:kv * GQA + g + 1, c0:c0 + BLK] for g in range(GQA)], axis=1)


def _sink_row(sink_ref, kv):
    return jnp.concatenate([jnp.full((1, BLK), sink_ref[kv * GQA + g], F32) for g in range(GQA)], axis=1)


def _l1_attn_fwd(qt, kt, vt, gatet, x1, tgt, wout, gf, sink):
    seq = x1.shape[0]
    nq, nb = seq // TQ, seq // BLK

    def body(q_ref, gate_ref, kp_ref, k_ref, kn_ref, vp_ref, v_ref, vn_ref, x1_ref, tgt_ref, wo_ref, gf_ref, sink_ref,
             dx2_ref, dx2b_ref, yt_ref, att_ref, lse_ref, loss_ref, dgf_ref, kbuf, vbuf, att_scr):
        i = pl.program_id(0)

        @pl.when(i == 0)
        def _():
            loss_ref[...] = jnp.zeros_like(loss_ref)
            dgf_ref[...] = jnp.zeros_like(dgf_ref)

        _fill_band(kbuf, kp_ref, k_ref, kn_ref)
        _fill_band(vbuf, vp_ref, v_ref, vn_ref)
        for j in range(TQ // BLK):
            c0 = j * BLK
            bias = _band_bias_t(i * (TQ // BLK) + j, nb)
            lse_rows = []
            for kv in range(N_KV):
                rows = slice(kv * HD, (kv + 1) * HD)
                q4 = _heads_t(q_ref, kv, c0)
                st = _tn(kbuf[rows, c0:c0 + 3 * BLK], q4) * SCALE + bias
                sk = _sink_row(sink_ref, kv)
                m = jnp.maximum(jnp.max(st, axis=0, keepdims=True), sk)
                p = jnp.exp(st - m)
                den = jnp.sum(p, axis=0, keepdims=True) + jnp.exp(sk - m)
                ot = _mm(vbuf[rows, c0:c0 + 3 * BLK], p.astype(BF16)) / den
                lse = m + jnp.log(den)
                for g in range(GQA):
                    h = kv * GQA + g
                    att_scr[h * HD:(h + 1) * HD, c0:c0 + BLK] = ot[:, g * BLK:(g + 1) * BLK]
                    lse_rows.append(lse[:, g * BLK:(g + 1) * BLK])
            lse_ref[:, c0:c0 + BLK] = jnp.concatenate(lse_rows, axis=0)

        att = att_scr[...]
        gate = gate_ref[...].astype(F32)
        yt = (att * (gate * jax.nn.sigmoid(gate))).astype(BF16)
        yt_ref[...] = yt
        att_ref[...] = att.astype(BF16)
        x2 = x1_ref[...] + _mm(yt.T, wo_ref[...])
        r = lax.rsqrt(jnp.mean(x2 * x2, axis=1, keepdims=True) + EPS)
        xn = x2 * r
        diff = xn * gf_ref[...] - tgt_ref[...]
        loss_ref[...] += 0.5 * jnp.sum(jnp.mean(diff * diff, axis=1, keepdims=True), axis=0, keepdims=True)
        dout = diff * (1.0 / D)
        dgf_ref[...] += jnp.sum(dout * xn, axis=0, keepdims=True)
        dxn = dout * gf_ref[...]
        dx2 = r * (dxn - xn * jnp.mean(dxn * xn, axis=1, keepdims=True))
        dx2_ref[...] = dx2
        dx2b_ref[...] = dx2.astype(BF16)

    ident = lambda i: i
    row = pl.BlockSpec((TQ, D), lambda i: (i, 0))
    col = lambda rows: pl.BlockSpec((rows, TQ), lambda i: (0, i))
    return pl.pallas_call(
        body, grid=(nq,), name="l1_attn_fwd",
        out_shape=(jax.ShapeDtypeStruct((seq, D), F32), jax.ShapeDtypeStruct((seq, D), BF16),
                   jax.ShapeDtypeStruct((D, seq), BF16), jax.ShapeDtypeStruct((D, seq), BF16),
                   jax.ShapeDtypeStruct((N_HEADS, seq), F32), jax.ShapeDtypeStruct((1, 1), F32),
                   jax.ShapeDtypeStruct((1, D), F32)),
        in_specs=[col(D), col(D)] + _band_specs_t(nb, ident) + _band_specs_t(nb, ident) + [
            row, row, _resident((D, D)), _resident((1, D)), pl.BlockSpec(memory_space=pltpu.SMEM)],
        out_specs=(row, row, col(D), col(D), col(N_HEADS), pl.BlockSpec((1, 1), lambda i: (0, 0)),
                   pl.BlockSpec((1, D), lambda i: (0, 0))),
        scratch_shapes=[pltpu.VMEM((KV_W, TQ + 2 * BLK), BF16), pltpu.VMEM((KV_W, TQ + 2 * BLK), BF16),
                        pltpu.VMEM((D, TQ), F32)],
        compiler_params=_params(56),
    )(qt, gatet, kt, kt, kt, vt, vt, vt, x1, tgt, wout, gf, sink)


def _l1_attn_bwd(dx2b, wout, qt, kt, vt, gatet, att, lse, sink):
    seq = dx2b.shape[0]
    nq, nb = seq // TQ, seq // BLK

    def body(dx_ref, wo_ref, q_ref, gate_ref, kp_ref, k_ref, kn_ref, vp_ref, v_ref, vn_ref, att_ref, lse_ref, sink_ref,
             dq_ref, dgate_ref, dk_ref, dv_ref, dsink_ref, kbuf, vbuf, dkacc, dvacc, dat_scr, delta_scr, dsacc):
        i = pl.program_id(0)

        @pl.when(i == 0)
        def _():
            dkacc[...] = jnp.zeros_like(dkacc)
            dvacc[...] = jnp.zeros_like(dvacc)
            dsacc[...] = jnp.zeros_like(dsacc)

        @pl.when(i > 0)
        def _():
            for acc in (dkacc, dvacc):
                acc[:, 0:2 * BLK] = acc[:, TQ:TQ + 2 * BLK]
                acc[:, 2 * BLK:2 * BLK + TQ] = jnp.zeros((KV_W, TQ), F32)

        @pl.when(i < nq)
        def _():
            _fill_band(kbuf, kp_ref, k_ref, kn_ref)
            _fill_band(vbuf, vp_ref, v_ref, vn_ref)
            dyt = _nt(wo_ref[...], dx_ref[...])
            sg, dsg = _silu_and_grad(gate_ref[...].astype(F32))
            attf = att_ref[...].astype(F32)
            dat = dyt * sg
            dat_scr[...] = dat.astype(BF16)
            dgate_ref[...] = (dyt * attf * dsg).astype(BF16)
            dl = dat * attf
            delta_scr[...] = jnp.concatenate(
                [jnp.sum(dl[h * HD:(h + 1) * HD, :], axis=0, keepdims=True) for h in range(N_HEADS)], axis=0)
            for j in range(TQ // BLK):
                c0 = j * BLK
                bias = _band_bias_t(i * (TQ // BLK) + j, nb)
                for kv in range(N_KV):
                    rows = slice(kv * HD, (kv + 1) * HD)
                    q4 = _heads_t(q_ref, kv, c0)
                    do4 = _heads_t(dat_scr, kv, c0)
                    lse4 = _row4(lse_ref, kv, c0)
                    delta4 = _row4(delta_scr, kv, c0)
                    kth = kbuf[rows, c0:c0 + 3 * BLK]
                    vth = vbuf[rows, c0:c0 + 3 * BLK]
                    p = jnp.exp(_tn(kth, q4) * SCALE + bias - lse4)
                    dp = _tn(vth, do4)
                    ds = (p * (dp - delta4) * SCALE).astype(BF16)
                    dq4 = _mm(kth, ds)
                    dkacc[rows, c0:c0 + 3 * BLK] += _nt(q4, ds)
                    dvacc[rows, c0:c0 + 3 * BLK] += _nt(do4, p.astype(BF16))
                    dsk = -jnp.exp(_sink_row(sink_ref, kv) - lse4) * delta4
                    for g in range(GQA):
                        h = kv * GQA + g
                        dq_ref[h * HD:(h + 1) * HD, c0:c0 + BLK] = dq4[:, g * BLK:(g + 1) * BLK].astype(BF16)
                        dsacc[h:h + 1, :] += dsk[:, g * BLK:(g + 1) * BLK]

        dk_ref[...] = dkacc[:, 0:TQ].astype(BF16)
        dv_ref[...] = dvacc[:, 0:TQ].astype(BF16)

        @pl.when(i == nq)
        def _():
            dsink_ref[...] = jnp.broadcast_to(jnp.sum(dsacc[...], axis=1, keepdims=True), (N_HEADS, LANES))

    clamp = lambda i: jnp.minimum(i, nq - 1)
    row = pl.BlockSpec((TQ, D), lambda i: (clamp(i), 0))
    col = lambda rows: pl.BlockSpec((rows, TQ), lambda i: (0, clamp(i)))
    pad = pl.BlockSpec((KV_W, TQ), lambda i: (0, i))
    return pl.pallas_call(
        body, grid=(nq + 1,), name="l1_attn_bwd",
        out_shape=(jax.ShapeDtypeStruct((D, seq), BF16), jax.ShapeDtypeStruct((D, seq), BF16),
                   jax.ShapeDtypeStruct((KV_W, seq + TQ), BF16), jax.ShapeDtypeStruct((KV_W, seq + TQ), BF16),
                   jax.ShapeDtypeStruct((N_HEADS, LANES), F32)),
        in_specs=[row, _resident((D, D)), col(D), col(D)] + _band_specs_t(nb, clamp) + _band_specs_t(nb, clamp) + [
            col(D), col(N_HEADS), pl.BlockSpec(memory_space=pltpu.SMEM)],
        out_specs=(col(D), col(D), pad, pad, pl.BlockSpec((N_HEADS, LANES), lambda i: (0, 0))),
        scratch_shapes=[pltpu.VMEM((KV_W, TQ + 2 * BLK), BF16), pltpu.VMEM((KV_W, TQ + 2 * BLK), BF16),
                        pltpu.VMEM((KV_W, TQ + 2 * BLK), F32), pltpu.VMEM((KV_W, TQ + 2 * BLK), F32),
                        pltpu.VMEM((D, TQ), BF16), pltpu.VMEM((N_HEADS, TQ), F32), pltpu.VMEM((N_HEADS, LANES), F32)],
        compiler_params=_params(56),
    )(dx2b, wout, qt, gatet, kt, kt, kt, vt, vt, vt, att, lse, sink)


def _l1_in_proj_bwd(dq_r, dk_r, dv, dgate, cos_t, sin_t, w, x1, g1, dx2):
    seq = x1.shape[0]
    tm = 512

    def body(dq_ref, dk_ref, dv_ref, dg_ref, c_ref, s_ref, w_ref, x_ref, g_ref, dres_ref,
             dx_ref, dxb_ref, dz_ref, dn_ref):
        @pl.when(pl.program_id(0) == 0)
        def _():
            dn_ref[...] = jnp.zeros_like(dn_ref)

        c, s = c_ref[...], s_ref[...]
        dq = _rope_t(dq_ref[...].astype(F32), c, s, N_HEADS, -1).astype(BF16)
        dk = _rope_t(dk_ref[...].astype(F32), c, s, N_KV, -1).astype(BF16)
        dz = jnp.concatenate([dq, dk, dv_ref[...], dg_ref[...]], axis=0)
        dz_ref[...] = dz
        half = MIX1_IN // 2
        dh = (_mm(w_ref[:, 0:half], dz[0:half]) + _mm(w_ref[:, half:MIX1_IN], dz[half:MIX1_IN])).T
        xf = x_ref[...]
        r = lax.rsqrt(jnp.mean(xf * xf, axis=1, keepdims=True) + EPS)
        xn = xf * r
        dn_ref[...] += jnp.sum(dh * xn, axis=0, keepdims=True)
        dxn = dh * g_ref[...]
        dx = dres_ref[...] + r * (dxn - xn * jnp.mean(dxn * xn, axis=1, keepdims=True))
        dx_ref[...] = dx
        dxb_ref[...] = dx.astype(BF16)

    row = pl.BlockSpec((tm, D), lambda i: (i, 0))
    col = lambda rows: pl.BlockSpec((rows, tm), lambda i: (0, i))
    return pl.pallas_call(
        body, grid=(seq // tm,), name="l1_in_proj_bwd",
        out_shape=(jax.ShapeDtypeStruct((seq, D), F32), jax.ShapeDtypeStruct((seq, D), BF16),
                   jax.ShapeDtypeStruct((MIX1_IN, seq), BF16), jax.ShapeDtypeStruct((1, D), F32)),
        in_specs=[col(D), col(KV_W), col(KV_W), col(D), col(ROT_HALF), col(ROT_HALF), _resident((D, MIX1_IN)), row,
                  _resident((1, D)), row],
        out_specs=(row, row, col(MIX1_IN), pl.BlockSpec((1, D), lambda i: (0, 0))),
        compiler_params=_params(48),
    )(dq_r, dk_r, dv, dgate, cos_t, sin_t, w, x1, g1, dx2)


def _l0_mix_bwd(dx1b, wout, za, bx, bg, ws, ws_t, bias, gv, wg, wg_t, scale):
    seq = dx1b.shape[0]
    ts = 256
    n_tiles = seq // ts

    def body(dx_ref, wo_ref, za_ref, bx_ref, bxp_ref, bxn_ref, bg_ref, ws_ref, wst_ref, bias_ref, gv_ref, wg_ref,
             wgt_ref, sc_ref,
             dz_ref, dp_ref, catt_ref, dws_ref, dbias_ref, dgv_ref, dsc_ref, dwg_ref, db_ref, xe_ref):
        i = pl.program_id(0)

        @pl.when(i == 0)
        def _():
            for r_ in (dws_ref, dbias_ref, dgv_ref, dsc_ref, dwg_ref, db_ref):
                r_[...] = jnp.zeros_like(r_)

        dxb = dx_ref[...]
        dya = _nt(dxb, wo_ref[0:D, :])
        dyb = _nt(dxb, wo_ref[D:2 * D, :])

        u, du = _gelu_and_grad(za_ref[:, 0:D].astype(F32))
        vg, dvg_dz = _gelu_and_grad(za_ref[:, D:2 * D].astype(F32))
        rv = lax.rsqrt(jnp.mean(vg * vg, axis=1, keepdims=True) + EPS)
        vnorm = vg * rv
        gvw = gv_ref[...]
        vnb = (vnorm * gvw).astype(BF16)
        mixed = _spatial_mix(ws_ref, vnb, bias_ref[...], ts)
        sga, dsga = _silu_and_grad(za_ref[:, 2 * D:3 * D].astype(F32))
        um = u * mixed
        ya = (um * sga).astype(BF16)
        t = dya * sga
        dz_ref[:, 0:D] = (t * mixed * du).astype(BF16)
        dz_ref[:, 2 * D:3 * D] = (dya * um * dsga).astype(BF16)
        dmixed = t * u
        dmb = dmixed.astype(BF16)
        dvn_rows = []
        dbias = jnp.zeros((CHUNK, D), F32)
        for c in range(ts // CHUNK):
            rows = slice(c * CHUNK, (c + 1) * CHUNK)
            dbias = dbias + dmixed[rows, :]
            parts = []
            for h in range(A_GROUPS):
                cols = slice(h * GDIM, (h + 1) * GDIM)
                dws_ref[h] += _nt(dmb[rows, cols], vnb[rows, cols])
                parts.append(_mm(wst_ref[h], dmb[rows, cols]))
            dvn_rows.append(jnp.concatenate(parts, axis=1))
        dbias_ref[...] += dbias
        dvn = jnp.concatenate(dvn_rows, axis=0)
        dgv_ref[...] += jnp.sum(dvn * vnorm, axis=0, keepdims=True)
        dxn = dvn * gvw
        dvg = rv * (dxn - vnorm * jnp.mean(dxn * vnorm, axis=1, keepdims=True))
        dz_ref[:, D:2 * D] = (dvg * dvg_dz).astype(BF16)

        _fill_halo(xe_ref, bx_ref[...], bxp_ref, bxn_ref, i, n_tiles, ts)
        pb = _pool_forward(xe_ref, ts, i * ts, seq).astype(BF16)
        ypre = jnp.concatenate([_mm(pb[:, g * GDIM:(g + 1) * GDIM], wg_ref[g]) for g in range(4)], axis=1)
        sc = sc_ref[...]
        y = ypre * sc
        sgb, dsgb = _silu_and_grad(bg_ref[...].astype(F32))
        yb = (y * sgb).astype(BF16)
        dy_b = dyb * sgb
        dz_ref[:, 3 * D:4 * D] = jnp.zeros((ts, D), BF16)
        dz_ref[:, 4 * D:5 * D] = (dyb * y * dsgb).astype(BF16)
        dsc_ref[...] += jnp.sum(dy_b * ypre, axis=0, keepdims=True)
        dypre = (dy_b * sc).astype(BF16)
        dps = []
        for g in range(4):
            cols = slice(g * GDIM, (g + 1) * GDIM)
            dwg_ref[g] += _tn(pb[:, cols], dypre[:, cols])
            dps.append(_mm(dypre[:, cols], wgt_ref[g]))
        dp_ref[...] = jnp.concatenate(dps, axis=1)
        catt_ref[...] = jnp.concatenate([ya, yb], axis=1).T

        @pl.when(i == n_tiles - 1)
        def _():
            for h in range(A_GROUPS):
                tot = jnp.sum(dbias_ref[:, h * GDIM:(h + 1) * GDIM].T, axis=0, keepdims=True)
                db_ref[pl.ds(h * 8, 8), :] = jnp.broadcast_to(tot, (8, CHUNK))

    prev, nxt = _halo_specs(ts, seq, D)
    row = lambda w_: pl.BlockSpec((ts, w_), lambda i: (i, 0))
    acc = lambda shape: pl.BlockSpec(shape, lambda i: (0,) * len(shape))
    return pl.pallas_call(
        body, grid=(n_tiles,), name="l0_mix_bwd",
        out_shape=(jax.ShapeDtypeStruct((seq, MIX0_IN), BF16), jax.ShapeDtypeStruct((seq, D), F32),
                   jax.ShapeDtypeStruct((2 * D, seq), BF16),
                   jax.ShapeDtypeStruct((4, CHUNK, CHUNK), F32), jax.ShapeDtypeStruct((CHUNK, D), F32),
                   jax.ShapeDtypeStruct((1, D), F32), jax.ShapeDtypeStruct((1, D), F32),
                   jax.ShapeDtypeStruct((4, GDIM, GDIM), F32), jax.ShapeDtypeStruct((32, CHUNK), F32)),
        in_specs=[row(D), _resident((2 * D, D)), row(3 * D), row(D), prev, nxt, row(D), _resident((4, CHUNK, CHUNK)),
                  _resident((4, CHUNK, CHUNK)), _resident((CHUNK, D)), _resident((1, D)), _resident((4, GDIM, GDIM)),
                  _resident((4, GDIM, GDIM)), _resident((1, D))],
        out_specs=(row(MIX0_IN), row(D), pl.BlockSpec((2 * D, ts), lambda i: (0, i)),
                   acc((4, CHUNK, CHUNK)), acc((CHUNK, D)), acc((1, D)), acc((1, D)), acc((4, GDIM, GDIM)),
                   acc((32, CHUNK))),
        scratch_shapes=[pltpu.VMEM((ts + 2 * POOL_HALO, D), F32)],
        compiler_params=_params(56),
    )(dx1b, wout, za, bx, bx, bx, bg, ws, ws_t, bias, gv, wg, wg_t, scale)


def _l0_pool_bwd(dp, dz):
    seq = dp.shape[0]
    ts = 512
    n_tiles = seq // ts
    ext = ts + 2 * POOL_HALO

    def body(dp_ref, dpp_ref, dpn_ref, dz_ref, out_ref, qe_ref):
        i = pl.program_id(0)
        _fill_halo(qe_ref, dp_ref[...], dpp_ref, dpn_ref, i, n_tiles, ts)
        te = i * ts - POOL_HALO + lax.broadcasted_iota(jnp.int32, (ext, 1), 0)
        for gi, w in enumerate(POOL_WINDOWS):
            hw = w // 2
            cols = slice(gi * GDIM, (gi + 1) * GDIM)
            cnt = jnp.maximum(jnp.minimum(te + hw, seq) - jnp.maximum(te - hw, 0), 1).astype(F32)
            qe_ref[:, cols] = qe_ref[:, cols] / cnt
        outs = []
        for gi, w in enumerate(POOL_WINDOWS):
            hw = w // 2
            cols = slice(gi * GDIM, (gi + 1) * GDIM)
            acc = qe_ref[pl.ds(POOL_HALO - hw + 1, ts), cols]
            for k in range(-hw + 2, hw + 1):
                acc = acc + qe_ref[pl.ds(POOL_HALO + k, ts), cols]
            outs.append(acc - dp_ref[:, cols])
        out_ref[...] = jnp.concatenate(outs, axis=1).astype(BF16)

    prev, nxt = _halo_specs(ts, seq, D)
    row = pl.BlockSpec((ts, D), lambda i: (i, 0))
    return pl.pallas_call(
        body, grid=(n_tiles,), name="l0_pool_bwd",
        out_shape=jax.ShapeDtypeStruct(dz.shape, BF16),
        in_specs=[row, prev, nxt, pl.BlockSpec(memory_space=pl.ANY)],
        out_specs=pl.BlockSpec((ts, D), lambda i: (i, 3)),
        input_output_aliases={3: 0},
        scratch_shapes=[pltpu.VMEM((ext, D), F32)],
        compiler_params=_params(32),
    )(dp, dp, dp, dz)


def _l0_in_proj_bwd(dz, w, x, g0, dx1):
    seq = x.shape[0]
    tm = 512

    def body(dz_ref, w_ref, x_ref, g_ref, dres_ref, dx_ref, dn_ref):
        @pl.when(pl.program_id(0) == 0)
        def _():
            dn_ref[...] = jnp.zeros_like(dn_ref)

        dh = _nt(dz_ref[...], w_ref[...])
        xf = x_ref[...]
        r = lax.rsqrt(jnp.mean(xf * xf, axis=1, keepdims=True) + EPS)
        xn = xf * r
        dn_ref[...] += jnp.sum(dh * xn, axis=0, keepdims=True)
        dxn = dh * g_ref[...]
        dx_ref[...] = dres_ref[...] + r * (dxn - xn * jnp.mean(dxn * xn, axis=1, keepdims=True))

    row = lambda w_: pl.BlockSpec((tm, w_), lambda i: (i, 0))
    return pl.pallas_call(
        body, grid=(seq // tm,), name="l0_in_proj_bwd",
        out_shape=(jax.ShapeDtypeStruct((seq, D), F32), jax.ShapeDtypeStruct((1, D), F32)),
        in_specs=[row(MIX0_IN), _resident((D, MIX0_IN)), row(D), _resident((1, D)), row(D)],
        out_specs=(row(D), pl.BlockSpec((1, D), lambda i: (0, 0))),
        compiler_params=_params(56),
    )(dz, w, x, g0, dx1)


def _dw_matmul(a_t, b, name, b_transposed=False, tn=1024, ts=1024, col_block=None):
    k, seq = a_t.shape
    n = b.shape[0] if b_transposed else b.shape[1]
    tn = min(n, tn)
    assert seq % ts == 0 and n % tn == 0 and (col_block is None or tn % col_block == 0)
    n_s = seq // ts
    per = 1 if col_block is None else tn // col_block

    def body(a_ref, b_ref, o_ref, ob_ref, acc_ref):
        s = pl.program_id(1)

        @pl.when(s == 0)
        def _():
            acc_ref[...] = jnp.zeros_like(acc_ref)

        acc_ref[...] += _nt(a_ref[...], b_ref[...]) if b_transposed else _mm(a_ref[...], b_ref[...])

        @pl.when(s == n_s - 1)
        def _():
            acc = acc_ref[...]
            if col_block is None:
                o_ref[...] = acc
                ob_ref[...] = acc.astype(BF16)
            else:
                for i in range(per):
                    piece = acc[:, i * col_block:(i + 1) * col_block]
                    o_ref[i] = piece
                    ob_ref[i] = piece.astype(BF16)

    b_spec = (pl.BlockSpec((tn, ts), lambda j, s: (j, s)) if b_transposed else pl.BlockSpec((ts, tn), lambda j, s: (s, j)))
    if col_block is None:
        shape, o_spec = (k, n), pl.BlockSpec((k, tn), lambda j, s: (0, j))
    else:
        shape, o_spec = (n // col_block, k, col_block), pl.BlockSpec((per, k, col_block), lambda j, s: (j, 0, 0))
    return pl.pallas_call(
        body, grid=(n // tn, n_s), name=name,
        out_shape=(jax.ShapeDtypeStruct(shape, F32), jax.ShapeDtypeStruct(shape, BF16)),
        in_specs=[pl.BlockSpec((k, ts), lambda j, s: (0, s)), b_spec],
        out_specs=(o_spec, o_spec),
        scratch_shapes=[pltpu.VMEM((k, tn), F32)],
        compiler_params=_params(56, 2),
    )(a_t, b)


ROW_TILES = 8


def _cast_shards(shards):
    n = len(shards)

    def body(*refs):
        for a in range(n):
            refs[n + a][...] = refs[a][...].astype(BF16)

    vm = pl.BlockSpec(memory_space=pltpu.VMEM)
    return pl.pallas_call(body, name="cast_weights", out_shape=[jax.ShapeDtypeStruct(t.shape, BF16) for t in shards],
                          in_specs=[vm] * n, out_specs=[vm] * n, compiler_params=_params(32, 0))(*shards)


def _adamw_math(w, g, m, v):
    m2 = ADAM_B1 * m + (1.0 - ADAM_B1) * g
    v2 = ADAM_B2 * v + (1.0 - ADAM_B2) * (g * g)
    m_hat = m2 / (1.0 - ADAM_B1 ** ADAM_STEP)
    v_hat = v2 / (1.0 - ADAM_B2 ** ADAM_STEP)
    delta = -ADAM_LR * (m_hat / (jnp.sqrt(v_hat) + ADAM_EPS) + ADAM_WD * w)
    return delta, m2, v2


def _final_sum_adamw(g_list, recv_list, me, w_list, m_list, v_list):
    n = len(w_list)

    def body(me_ref, *refs):
        own, recv, w, m, v = (refs[k * n:(k + 1) * n] for k in range(5))
        outs = [refs[(5 + k) * n:(6 + k) * n] for k in range(4)]
        for a in range(n):
            g = own[a][...]
            for k in range(N_DEV - 1):
                g = g + recv[a][k].astype(F32)
            delta, m2, v2 = _adamw_math(w[a][...], g, m[a][...], v[a][...])
            for o_ref, val in zip((outs[0][a], outs[1][a], outs[2][a], outs[3][a]), (g, delta, m2, v2)):
                o_ref[...] = val

    own_specs, flat, wire, shapes = [], [], [], []
    for t in w_list:
        rows, width = t.shape
        tr = rows // ROW_TILES
        own_specs.append(pl.BlockSpec((None, tr, width), lambda i, me: (me[0], i, 0)))
        flat.append(pl.BlockSpec((tr, width), lambda i, me: (i, 0)))
        wire.append(pl.BlockSpec((N_DEV - 1, tr, width), lambda i, me: (0, i, 0)))
        shapes.append(jax.ShapeDtypeStruct((rows, width), F32))
    out = pl.pallas_call(
        body, name="grad_sum_adamw", out_shape=shapes * 4,
        grid_spec=pltpu.PrefetchScalarGridSpec(
            num_scalar_prefetch=1, grid=(ROW_TILES,), in_specs=own_specs + wire + flat * 3, out_specs=flat * 4),
        compiler_params=_params(40),
    )(me, *g_list, *recv_list, *w_list, *m_list, *v_list)
    return [out[k * n:(k + 1) * n] for k in range(4)]


SMALL_NAMES = ("norm_0", "a_v_norm_0", "b_scale_0", "norm_1", "final_norm", "a_spatial_w_0", "a_spatial_b_0", "sink_1")
SMALL_VIEWS = ((8, LANES),) * 5 + ((4 * CHUNK, LANES), (4, LANES), (1, N_HEADS))
SMALL_ROW0 = (0, 8, 16, 24, 32, 40, 552, 560)
SMALL_ROWS = 568


def _small_sum_adamw(early, late, w_list, m_list, v_list):
    n = len(w_list)

    def body(e_ref, l_ref, *refs):
        gtot, first = e_ref[0], l_ref[0]
        for d in range(1, N_DEV):
            gtot = gtot + e_ref[d]
            first = first + l_ref[d]
        for a, ((rows, width), r0) in enumerate(zip(SMALL_VIEWS, SMALL_ROW0)):
            g = first if SMALL_NAMES[a] == "norm_0" else gtot[r0:r0 + rows, 0:width]
            delta, m2, v2 = _adamw_math(refs[a][...], g, refs[n + a][...], refs[2 * n + a][...])
            for k, val in enumerate((g, delta, m2, v2)):
                refs[(3 + k) * n + a][...] = val

    vm = pl.BlockSpec(memory_space=pltpu.VMEM)
    shapes = [jax.ShapeDtypeStruct(s, F32) for s in SMALL_VIEWS]
    out = pl.pallas_call(
        body, name="small_sum_adamw", out_shape=shapes * 4,
        in_specs=[vm, vm] + [vm] * (3 * n), out_specs=[vm] * (4 * n),
    )(early, late, *w_list, *m_list, *v_list)
    return [out[k * n:(k + 1) * n] for k in range(4)]


def _all_gather_columns(blks):
    n = len(blks)

    def body(*refs):
        ins, outs = refs[:n], refs[n:2 * n]
        send_sems, recv_sems, local_sems = refs[2 * n:]
        x, y, c = lax.axis_index("x"), lax.axis_index("y"), lax.axis_index("c")
        me, sibling = (x, y, c), (x, y, 1 - c)
        chips = [(1 - x, y), (x, 1 - y), (1 - x, 1 - y)]

        def slot(a, px, py, pc):
            width = blks[a].shape[1]
            return outs[a].at[:, pl.ds(pl.multiple_of((4 * px + 2 * py + pc) * width, LANES), width)]

        def copy(k, a, block, to, from_input=False):
            return pltpu.make_async_remote_copy(
                src_ref=ins[a] if from_input else slot(a, *block), dst_ref=slot(a, *block),
                send_sem=send_sems.at[k, a], recv_sem=recv_sems.at[k, a], device_id=to, device_id_type=MESH)

        mine = [pltpu.make_async_copy(ins[a], slot(a, *me), local_sems.at[a]) for a in range(n)]
        first = []
        for a in range(n):
            first.append(copy(0, a, me, sibling, from_input=True))
            first += [copy(1 + j, a, me, (*chip, c), from_input=True) for j, chip in enumerate(chips)]
        for cp in mine + first:
            cp.start()
        passed = []
        for j, chip in enumerate(chips):
            for a in range(n):
                copy(1 + j, a, (*chip, c), me).wait_recv()
                passed.append(copy(4 + j, a, (*chip, c), sibling))
                passed[-1].start()
        for a in range(n):
            copy(0, a, sibling, me).wait_recv()
        for j, chip in enumerate(chips):
            for a in range(n):
                copy(4 + j, a, (*chip, 1 - c), me).wait_recv()
        for cp in first + passed:
            cp.wait_send()
        for cp in mine:
            cp.wait()

    any_spec = pl.BlockSpec(memory_space=pl.ANY)
    return pl.pallas_call(
        body, name="weights_all_gather",
        out_shape=[jax.ShapeDtypeStruct((t.shape[0], N_DEV * t.shape[1]), t.dtype) for t in blks],
        in_specs=[any_spec] * n, out_specs=[any_spec] * n,
        scratch_shapes=[pltpu.SemaphoreType.DMA((7, n)), pltpu.SemaphoreType.DMA((7, n)), pltpu.SemaphoreType.DMA((n,))],
    )(*blks)


PEER_FLIPS = tuple((fx, fy, fc) for fx in (0, 1) for fy in (0, 1) for fc in (0, 1))[1:]


def _sequencer_all_gather(blks, name, collective_id):
    n = len(blks)

    def body(*refs):
        ins, outs = refs[:n], refs[n:2 * n]
        send_sems, recv_sems, local_sems = refs[2 * n:]
        x, y, c = lax.axis_index("x"), lax.axis_index("y"), lax.axis_index("c")
        peers = [(x ^ fx, y ^ fy, c ^ fc) for fx, fy, fc in PEER_FLIPS]
        barrier = pltpu.get_barrier_semaphore()
        for peer in peers:
            pl.semaphore_signal(barrier, inc=1, device_id=peer, device_id_type=MESH)
        pl.semaphore_wait(barrier, len(peers))
        me = 4 * x + 2 * y + c
        copies = [pltpu.make_async_remote_copy(
            src_ref=ins[a], dst_ref=outs[a].at[me], send_sem=send_sems.at[k, a], recv_sem=recv_sems.at[k, a],
            device_id=peer, device_id_type=MESH) for k, peer in enumerate(peers) for a in range(n)]
        mine = [pltpu.make_async_copy(ins[a], outs[a].at[me], local_sems.at[a]) for a in range(n)]
        for cp in copies + mine:
            cp.start()
        for cp in copies + mine:
            cp.wait()

    return pl.kernel(
        body, out_type=[jax.ShapeDtypeStruct((N_DEV,) + t.shape, t.dtype) for t in blks],
        mesh=plsc.ScalarSubcoreMesh(axis_name="sequencer", num_cores=1), name=name,
        scratch_types=[pltpu.SemaphoreType.DMA((7, n)), pltpu.SemaphoreType.DMA((7, n)), pltpu.SemaphoreType.DMA((n,))],
        compiler_params=pltpu.CompilerParams(collective_id=collective_id),
    )(*blks)


def _sequencer_scatter(g_list, name, collective_id):
    n = len(g_list)

    def body(*refs):
        ins, outs = refs[:n], refs[n:2 * n]
        send_sems, recv_sems = refs[2 * n:]
        x, y, c = lax.axis_index("x"), lax.axis_index("y"), lax.axis_index("c")
        peers = [(x ^ fx, y ^ fy, c ^ fc) for fx, fy, fc in PEER_FLIPS]
        barrier = pltpu.get_barrier_semaphore()
        for peer in peers:
            pl.semaphore_signal(barrier, inc=1, device_id=peer, device_id_type=MESH)
        pl.semaphore_wait(barrier, len(peers))
        copies = [pltpu.make_async_remote_copy(
            src_ref=ins[a].at[4 * px + 2 * py + pc], dst_ref=outs[a].at[k], send_sem=send_sems.at[k, a],
            recv_sem=recv_sems.at[k, a], device_id=(px, py, pc), device_id_type=MESH)
            for k, (px, py, pc) in enumerate(peers) for a in range(n)]
        for cp in copies:
            cp.start()
        for cp in copies:
            cp.wait()

    return pl.kernel(
        body, out_type=[jax.ShapeDtypeStruct((N_DEV - 1,) + g.shape[1:], g.dtype) for g in g_list],
        mesh=plsc.ScalarSubcoreMesh(axis_name="sequencer", num_cores=1), name=name,
        scratch_types=[pltpu.SemaphoreType.DMA((7, n)), pltpu.SemaphoreType.DMA((7, n))],
        compiler_params=pltpu.CompilerParams(collective_id=collective_id),
    )(*g_list)


def _direct_all_gather(blk, name):
    def body(g_ref, out_ref, send_sems, recv_sems, local_sem):
        x, y, c = lax.axis_index("x"), lax.axis_index("y"), lax.axis_index("c")
        me = 4 * x + 2 * y + c
        copies = [pltpu.make_async_remote_copy(
            src_ref=g_ref, dst_ref=out_ref.at[me], send_sem=send_sems.at[k], recv_sem=recv_sems.at[k],
            device_id=(x ^ fx, y ^ fy, c ^ fc), device_id_type=MESH) for k, (fx, fy, fc) in enumerate(PEER_FLIPS)]
        copies.append(pltpu.make_async_copy(g_ref, out_ref.at[me], local_sem))
        for cp in copies:
            cp.start()
        for cp in copies:
            cp.wait()

    any_spec = pl.BlockSpec(memory_space=pl.ANY)
    return pl.pallas_call(
        body, name=name, out_shape=jax.ShapeDtypeStruct((N_DEV,) + blk.shape, blk.dtype),
        in_specs=[any_spec], out_specs=any_spec,
        scratch_shapes=[pltpu.SemaphoreType.DMA((7,)), pltpu.SemaphoreType.DMA((7,)), pltpu.SemaphoreType.DMA],
    )(blk)


def _shard_views(w_in_0, b_group_w_0, w_out_0, w_in_1, w_out_1):
    return [w_in_0, b_group_w_0.reshape(4 * 32, GDIM), w_out_0, w_in_1, w_out_1]


def _small_views(named):
    return [named[name].reshape(view) for name, view in zip(SMALL_NAMES, SMALL_VIEWS)]


def _pack_small_grads(named):
    rows = []
    for name, (r, w) in zip(SMALL_NAMES, SMALL_VIEWS):
        pad_r = -r % 8
        if name in named:
            rows.append(jnp.pad(named[name].reshape(r, w), ((0, pad_r), (0, LANES - w))))
        else:
            rows.append(jnp.zeros((r + pad_r, LANES), F32))
    return jnp.concatenate(rows, axis=0)


def _device_blocks(t, axis):
    shape = t.shape
    t = t.reshape(shape[:axis] + (N_DEV, shape[axis] // N_DEV) + shape[axis + 1:])
    t = jnp.moveaxis(t, axis, 0)
    return t.reshape(N_DEV, -1, shape[-1] if axis != len(shape) - 1 else shape[-1] // N_DEV)


def kernel(x, norm_0, w_in_0, a_v_norm_0, a_spatial_w_0, a_spatial_b_0, b_group_w_0, b_scale_0, w_out_0, norm_1, w_in_1, sink_1, w_out_1, final_norm, loss_target, m_norm_0, m_w_in_0, m_a_v_norm_0, m_a_spatial_w_0, m_a_spatial_b_0, m_b_group_w_0, m_b_scale_0, m_w_out_0, m_norm_1, m_w_in_1, m_sink_1, m_w_out_1, m_final_norm, v_norm_0, v_w_in_0, v_a_v_norm_0, v_a_spatial_w_0, v_a_spatial_b_0, v_b_group_w_0, v_b_scale_0, v_w_out_0, v_norm_1, v_w_in_1, v_sink_1, v_w_out_1, v_final_norm):
    seq = x.shape[1]
    xs = x.reshape(seq, D)
    tgt = loss_target.reshape(seq, D)
    ax, ay, ac = lax.axis_index("x"), lax.axis_index("y"), lax.axis_index("c")
    me = jnp.reshape(4 * ax + 2 * ay + ac, (1,)).astype(jnp.int32)

    shards = _shard_views(w_in_0, b_group_w_0, w_out_0, w_in_1, w_out_1)
    cast = _cast_shards(shards)
    win0 = _all_gather_columns(cast[0:1])[0]
    gathered = (list(_sequencer_all_gather(cast[1:3], "weights_gather_a", 1))
                + list(_sequencer_all_gather(cast[3:5], "weights_gather_b", 2)))

    blocks, received, early = {}, {}, {}
    collective_ids = {"l1": 3, "out0": 4, "in0": 5}

    def after_forward(loss_part, dx2b):
        early["loss"], dx2b = lax.optimization_barrier((lax.psum(loss_part[0, 0], ("x", "y", "c")), dx2b))
        return dx2b

    def scatter(tag, own_blocks, wire_blocks):
        blocks[tag] = own_blocks
        received[tag] = _sequencer_scatter(wire_blocks, "grad_scatter_" + tag, collective_ids[tag])

    def small_early(named):
        early["small"] = _sequencer_all_gather([_pack_small_grads(named)], "small_grad_gather", 6)[0]

    grad_x, d_norm_0 = _local_step(xs, tgt, win0, gathered, norm_0, a_v_norm_0, a_spatial_w_0, a_spatial_b_0, b_scale_0,
                                   norm_1, sink_1, final_norm, after_forward, scatter, small_early)

    order = (("in0", 0), ("in0", 1), ("out0", 0), ("l1", 0), ("l1", 1))
    late = _direct_all_gather(d_norm_0.reshape(8, LANES), "norm_grad_gather")
    shards_late, _ = lax.optimization_barrier((shards, grad_x))
    big = _final_sum_adamw([blocks[t][i] for t, i in order], [received[t][i] for t, i in order], me, shards_late,
                           _shard_views(m_w_in_0, m_b_group_w_0, m_w_out_0, m_w_in_1, m_w_out_1),
                           _shard_views(v_w_in_0, v_b_group_w_0, v_w_out_0, v_w_in_1, v_w_out_1))
    weights = dict(norm_0=norm_0, a_v_norm_0=a_v_norm_0, a_spatial_w_0=a_spatial_w_0, a_spatial_b_0=a_spatial_b_0,
                   b_scale_0=b_scale_0, norm_1=norm_1, sink_1=sink_1, final_norm=final_norm)
    m_small = dict(norm_0=m_norm_0, a_v_norm_0=m_a_v_norm_0, a_spatial_w_0=m_a_spatial_w_0, a_spatial_b_0=m_a_spatial_b_0,
                   b_scale_0=m_b_scale_0, norm_1=m_norm_1, sink_1=m_sink_1, final_norm=m_final_norm)
    v_small = dict(norm_0=v_norm_0, a_v_norm_0=v_a_v_norm_0, a_spatial_w_0=v_a_spatial_w_0, a_spatial_b_0=v_a_spatial_b_0,
                   b_scale_0=v_b_scale_0, norm_1=v_norm_1, sink_1=v_sink_1, final_norm=v_final_norm)
    small = _small_sum_adamw(early["small"], late, _small_views(weights), _small_views(m_small), _small_views(v_small))

    def in_order(kind):
        b = [b_.reshape(s_.shape) for b_, s_ in zip(big[kind], (w_in_0, b_group_w_0, w_out_0, w_in_1, w_out_1))]
        s = {name: t.reshape(weights[name].shape) for name, t in zip(SMALL_NAMES, small[kind])}
        return [s["norm_0"], b[0], s["a_v_norm_0"], s["a_spatial_w_0"], s["a_spatial_b_0"], b[1], s["b_scale_0"], b[2],
                s["norm_1"], b[3], s["sink_1"], b[4], s["final_norm"]]

    return (early["loss"], grad_x.reshape(1, seq, D), *in_order(0), *in_order(1), *in_order(2), *in_order(3))


def _local_step(xs, tgt, win0, gathered, norm_0, a_v_norm_0, a_spatial_w_0, a_spatial_b_0, b_scale_0, norm_1, sink_1,
                final_norm, after_forward, scatter, small_early):
    seq = xs.shape[0]
    ws = a_spatial_w_0.astype(BF16)
    ws_t = jnp.swapaxes(ws, 1, 2)
    bias = jnp.repeat(a_spatial_b_0.T, GDIM, axis=1)
    g0, gv, scale, g1, gf = (t.reshape(1, D) for t in (norm_0, a_v_norm_0, b_scale_0, norm_1, final_norm))
    cos_t, sin_t = _rope_tables_t(seq)

    za, bx, bg, h0_t = _l0_in_proj(xs, g0, win0)
    g_wg, g_wout0, za = lax.optimization_barrier((gathered[0], gathered[1], za))
    wg = g_wg.reshape(N_DEV, 4, 32, GDIM).transpose(1, 0, 2, 3).reshape(4, GDIM, GDIM)
    wg_t = jnp.swapaxes(wg, 1, 2)
    wout0 = g_wout0.reshape(2 * D, D)
    x1 = _l0_mix_fwd(za, bx, bg, xs, ws, bias, gv, wg, scale, wout0)
    g_win1, g_wout1, x1 = lax.optimization_barrier((gathered[2], gathered[3], x1))
    win1 = g_win1.transpose(1, 0, 2).reshape(D, MIX1_IN)
    win1_t = win1.T
    wout1 = g_wout1.reshape(D, D)
    qt, kt, vt, gatet, h1_t = _l1_in_proj(x1, g1, win1_t, cos_t, sin_t)
    dx2, dx2b, y_t, att, lse, loss_part, d_gf = _l1_attn_fwd(qt, kt, vt, gatet, x1, tgt, wout1, gf, sink_1)
    dx2b = after_forward(loss_part, dx2b)

    d_wout1, d_wout1_wire = _dw_matmul(y_t, dx2b, "dw_out_1")
    dx2b, _ = lax.optimization_barrier((dx2b, d_wout1))
    dq_r, dgate, dk_pad, dv_pad, d_sink = _l1_attn_bwd(dx2b, wout1, qt, kt, vt, gatet, att, lse, sink_1)
    dk_r = dk_pad[:, BLK:BLK + seq]
    dv = dv_pad[:, BLK:BLK + seq]
    dx1, dx1b, dz1_t, d_g1 = _l1_in_proj_bwd(dq_r, dk_r, dv, dgate, cos_t, sin_t, win1, x1, g1, dx2)
    d_win1, d_win1_wire = _dw_matmul(h1_t, dz1_t, "dw_in_1", b_transposed=True, tn=1280, col_block=MIX1_IN // N_DEV)
    rows = lambda t: t.reshape(N_DEV, t.shape[0] // N_DEV, t.shape[1])
    scatter("l1", [d_win1, rows(d_wout1)], [d_win1_wire, rows(d_wout1_wire)])

    dz0, dp, cat_t, d_ws, _, d_gv, d_scale, d_wg, d_b = _l0_mix_bwd(
        dx1b, wout0, za, bx, bg, ws, ws_t, bias, gv, wg, wg_t, scale)
    d_wout0, d_wout0_wire = _dw_matmul(cat_t, dx1b, "dw_out_0", ts=512)
    scatter("out0", [rows(d_wout0)], [rows(d_wout0_wire)])
    small_early(dict(a_v_norm_0=d_gv, a_spatial_w_0=d_ws, a_spatial_b_0=d_b.reshape(4, 8, CHUNK)[:, 0, :],
                     b_scale_0=d_scale, norm_1=d_g1, sink_1=d_sink[:, 0], final_norm=d_gf))
    dp, _ = lax.optimization_barrier((dp, d_wout0))
    dz0 = _l0_pool_bwd(dp, dz0)
    d_win0, d_win0_wire = _dw_matmul(h0_t, dz0, "dw_in_0", tn=1280, col_block=MIX0_IN // N_DEV)
    d_wg_blocks = _device_blocks(d_wg, 1)
    scatter("in0", [d_win0, d_wg_blocks], [d_win0_wire, d_wg_blocks])
    dz0, _ = lax.optimization_barrier((dz0, d_win0))
    return _l0_in_proj_bwd(dz0, win0, xs, g0, dx1)
```

```python
import jax
import jax.numpy as jnp
from jax import lax
from jax.experimental import pallas as pl
from jax.experimental.pallas import tpu as pltpu
from jax.experimental.pallas import tpu_sc as plsc

F32 = jnp.float32
BF16 = jnp.bfloat16

D = 1024
EPS = 1e-6
NEG_INF = -1e30
CHUNK = 128
A_GROUPS = 4
POOL_WINDOWS = (2, 4, 8, 16)
POOL_HALO = 8
GDIM = 256
N_HEADS = 16
N_KV = 4
GQA = 4
HD = 64
BLK = 128
ROT_HALF = 8
ROPE_THETA = 500000.0
SCALE = HD ** -0.5
MIX0_IN = 5 * D
MIX1_IN = 2560
KV_W = N_KV * HD
Q_ROWS, K_ROWS, V_ROWS, G_ROWS = (0, D), (D, D + KV_W), (D + KV_W, D + 2 * KV_W), (D + 2 * KV_W, MIX1_IN)
TQ = 512

ADAM_LR = 0.001
ADAM_B1 = 0.9
ADAM_B2 = 0.999
ADAM_EPS = 1e-08
ADAM_WD = 0.01
ADAM_STEP = 10

N_DEV = 8
LANES = 128
MIB = 2 ** 20
MESH = pl.DeviceIdType.MESH


def _params(limit_mib, n_axes=1):
    return pltpu.CompilerParams(vmem_limit_bytes=limit_mib * MIB, dimension_semantics=("arbitrary",) * n_axes)


def _resident(shape):
    nd = len(shape)
    return pl.BlockSpec(shape, lambda *_: (0,) * nd, pipeline_mode=pl.Buffered(1))


def _gelu(x):
    k = 0.7978845608028654
    return 0.5 * x * (1.0 + jnp.tanh(k * (x + 0.044715 * x * x * x)))


def _gelu_and_grad(x):
    k = 0.7978845608028654
    x2 = x * x
    t = jnp.tanh(k * (x + 0.044715 * x * x2))
    g = 0.5 * x * (1.0 + t)
    dg = 0.5 * (1.0 + t) + 0.5 * x * (1.0 - t * t) * (k * (1.0 + 3.0 * 0.044715 * x2))
    return g, dg


def _silu_and_grad(x):
    s = jax.nn.sigmoid(x)
    return x * s, s * (1.0 + x * (1.0 - s))


def _nt(a, b):
    return lax.dot_general(a, b, (((1,), (1,)), ((), ())), preferred_element_type=F32)


def _tn(a, b):
    return lax.dot_general(a, b, (((0,), (0,)), ((), ())), preferred_element_type=F32)


def _mm(a, b):
    return jnp.dot(a, b, preferred_element_type=F32)


def _rope_tables_t(seq):
    inv = ROPE_THETA ** (-jnp.arange(0, 2 * ROT_HALF, 2, dtype=F32) / (2 * ROT_HALF))
    ang = inv[:, None] * jnp.arange(seq, dtype=F32)[None, :]
    return jnp.cos(ang), jnp.sin(ang)


def _rope_t(z, c, s, n_heads, sign):
    parts = []
    for h in range(n_heads):
        b = h * HD
        x1, x2 = z[b:b + ROT_HALF], z[b + ROT_HALF:b + 2 * ROT_HALF]
        if sign > 0:
            parts += [x1 * c - x2 * s, x2 * c + x1 * s]
        else:
            parts += [x1 * c + x2 * s, x2 * c - x1 * s]
        parts.append(z[b + 2 * ROT_HALF:b + HD])
    return jnp.concatenate(parts, axis=0)


def _l0_in_proj(x, g0, w):
    seq = x.shape[0]
    tm = 512

    def body(x_ref, g_ref, w_ref, za_ref, bx_ref, bg_ref, ht_ref):
        xf = x_ref[...]
        r = lax.rsqrt(jnp.mean(xf * xf, axis=1, keepdims=True) + EPS)
        h = (xf * r * g_ref[...]).astype(BF16)
        ht_ref[...] = h.T
        for j in range(3):
            za_ref[:, j * D:(j + 1) * D] = _mm(h, w_ref[:, j * D:(j + 1) * D]).astype(BF16)
        bx_ref[...] = _mm(h, w_ref[:, 3 * D:4 * D])
        bg_ref[...] = _mm(h, w_ref[:, 4 * D:5 * D]).astype(BF16)

    return pl.pallas_call(
        body, grid=(seq // tm,), name="l0_in_proj",
        out_shape=(jax.ShapeDtypeStruct((seq, 3 * D), BF16), jax.ShapeDtypeStruct((seq, D), F32),
                   jax.ShapeDtypeStruct((seq, D), BF16), jax.ShapeDtypeStruct((D, seq), BF16)),
        in_specs=[pl.BlockSpec((tm, D), lambda i: (i, 0)), _resident((1, D)), _resident((D, MIX0_IN))],
        out_specs=(pl.BlockSpec((tm, 3 * D), lambda i: (i, 0)), pl.BlockSpec((tm, D), lambda i: (i, 0)),
                   pl.BlockSpec((tm, D), lambda i: (i, 0)), pl.BlockSpec((D, tm), lambda i: (0, i))),
        compiler_params=_params(48),
    )(x, g0, w)


POOL_EXT = 40


def _fill_halo(ext_ref, cur, prev_ref, next_ref, i, n_tiles, ts):
    ext_ref[pl.ds(0, POOL_HALO), :] = jnp.where(i > 0, prev_ref[...], 0.0)
    ext_ref[pl.ds(POOL_HALO, ts), :] = cur
    ext_ref[pl.ds(POOL_HALO + ts, POOL_HALO), :] = jnp.where(i < n_tiles - 1, next_ref[...], 0.0)
    ext_ref[pl.ds(2 * POOL_HALO + ts, POOL_EXT - 2 * POOL_HALO), :] = jnp.zeros((POOL_EXT - 2 * POOL_HALO, D), F32)


def _window_sums(src_ref, tmp_refs, ts, cols, w, shift):
    if w == 2:
        return src_ref[pl.ds(POOL_HALO - 1 + shift, ts), cols] + src_ref[pl.ds(POOL_HALO + shift, ts), cols]
    d2, d4, d8 = tmp_refs
    n2, n4, n8 = ts + 32, ts + 24, ts + 16
    d2[pl.ds(0, n2), :] = src_ref[pl.ds(0, n2), cols] + src_ref[pl.ds(1, n2), cols]
    if w == 4:
        return d2[pl.ds(POOL_HALO - 2 + shift, ts), :] + d2[pl.ds(POOL_HALO + shift, ts), :]
    d4[pl.ds(0, n4), :] = d2[pl.ds(0, n4), :] + d2[pl.ds(2, n4), :]
    if w == 8:
        return d4[pl.ds(POOL_HALO - 4 + shift, ts), :] + d4[pl.ds(POOL_HALO + shift, ts), :]
    d8[pl.ds(0, n8), :] = d4[pl.ds(0, n8), :] + d4[pl.ds(4, n8), :]
    return d8[pl.ds(shift, ts), :] + d8[pl.ds(POOL_HALO + shift, ts), :]


def _pool_scratch(ts):
    return [pltpu.VMEM((ts + POOL_EXT, D), F32)] + [pltpu.VMEM((ts + POOL_EXT, GDIM), F32)] * 3


def _pool_forward(xe_ref, tmp_refs, ts, t0, seq):
    tg = t0 + lax.broadcasted_iota(jnp.int32, (ts, 1), 0)
    outs = []
    for gi, w in enumerate(POOL_WINDOWS):
        hw = w // 2
        cols = slice(gi * GDIM, (gi + 1) * GDIM)
        cnt = (jnp.minimum(tg + hw, seq) - jnp.maximum(tg - hw, 0)).astype(F32)
        outs.append(_window_sums(xe_ref, tmp_refs, ts, cols, w, 0) / cnt - xe_ref[pl.ds(POOL_HALO, ts), cols])
    return jnp.concatenate(outs, axis=1)


def _spatial_mix(ws_ref, vnb, bias, ts):
    rows = []
    for c in range(ts // CHUNK):
        vc = vnb[c * CHUNK:(c + 1) * CHUNK, :]
        rows.append(jnp.concatenate(
            [_mm(ws_ref[h], vc[:, h * GDIM:(h + 1) * GDIM]) for h in range(A_GROUPS)], axis=1) + bias)
    return jnp.concatenate(rows, axis=0)


def _halo_specs(ts, seq, width):
    per = ts // POOL_HALO
    last = seq // POOL_HALO - 1
    prev = pl.BlockSpec((POOL_HALO, width), lambda i: (jnp.maximum(i * per - 1, 0), 0))
    nxt = pl.BlockSpec((POOL_HALO, width), lambda i: (jnp.minimum((i + 1) * per, last), 0))
    return prev, nxt


def _l0_mix_fwd(za, bx, bg, x, ws, bias, gv, wg, scale, wout):
    seq = x.shape[0]
    ts = 512
    n_tiles = seq // ts

    def body(za_ref, bx_ref, bxp_ref, bxn_ref, bg_ref, x_ref, ws_ref, bias_ref, gv_ref, wg_ref, sc_ref, wo_ref,
             x1_ref, xe_ref, *tmp_refs):
        i = pl.program_id(0)
        u = _gelu(za_ref[:, 0:D].astype(F32))
        vg = _gelu(za_ref[:, D:2 * D].astype(F32))
        rv = lax.rsqrt(jnp.mean(vg * vg, axis=1, keepdims=True) + EPS)
        vnb = (vg * rv * gv_ref[...]).astype(BF16)
        mixed = _spatial_mix(ws_ref, vnb, bias_ref[...], ts)
        ag = za_ref[:, 2 * D:3 * D].astype(F32)
        ya = (u * mixed * (ag * jax.nn.sigmoid(ag))).astype(BF16)

        _fill_halo(xe_ref, bx_ref[...], bxp_ref, bxn_ref, i, n_tiles, ts)
        pb = _pool_forward(xe_ref, tmp_refs, ts, i * ts, seq).astype(BF16)
        y = jnp.concatenate([_mm(pb[:, g * GDIM:(g + 1) * GDIM], wg_ref[g]) for g in range(4)], axis=1) * sc_ref[...]
        bgf = bg_ref[...].astype(F32)
        yb = (y * (bgf * jax.nn.sigmoid(bgf))).astype(BF16)
        x1_ref[...] = x_ref[...] + _mm(ya, wo_ref[0:D, :]) + _mm(yb, wo_ref[D:2 * D, :])

    prev, nxt = _halo_specs(ts, seq, D)
    row = lambda w: pl.BlockSpec((ts, w), lambda i: (i, 0))
    return pl.pallas_call(
        body, grid=(n_tiles,), name="l0_mix_fwd",
        out_shape=jax.ShapeDtypeStruct((seq, D), F32),
        in_specs=[row(3 * D), row(D), prev, nxt, row(D), row(D), _resident((4, CHUNK, CHUNK)), _resident((CHUNK, D)),
                  _resident((1, D)), _resident((4, GDIM, GDIM)), _resident((1, D)), _resident((2 * D, D))],
        out_specs=row(D),
        scratch_shapes=_pool_scratch(ts),
        compiler_params=_params(56),
    )(za, bx, bx, bx, bg, x, ws, bias, gv, wg, scale, wout)


def _l1_in_proj(x1, g1, w_t, cos_t, sin_t):
    seq = x1.shape[0]
    tm = 512

    def body(x_ref, g_ref, wt_ref, c_ref, s_ref, q_ref, k_ref, v_ref, gate_ref, ht_ref):
        xf = x_ref[...]
        r = lax.rsqrt(jnp.mean(xf * xf, axis=1, keepdims=True) + EPS)
        ht = (xf * r * g_ref[...]).astype(BF16).T
        ht_ref[...] = ht
        c, s = c_ref[...], s_ref[...]
        q_ref[...] = _rope_t(_mm(wt_ref[Q_ROWS[0]:Q_ROWS[1], :], ht), c, s, N_HEADS, 1).astype(BF16)
        k_ref[...] = _rope_t(_mm(wt_ref[K_ROWS[0]:K_ROWS[1], :], ht), c, s, N_KV, 1).astype(BF16)
        v_ref[...] = _mm(wt_ref[V_ROWS[0]:V_ROWS[1], :], ht).astype(BF16)
        gate_ref[...] = _mm(wt_ref[G_ROWS[0]:G_ROWS[1], :], ht).astype(BF16)

    col = lambda rows: pl.BlockSpec((rows, tm), lambda i: (0, i))
    return pl.pallas_call(
        body, grid=(seq // tm,), name="l1_in_proj",
        out_shape=(jax.ShapeDtypeStruct((D, seq), BF16), jax.ShapeDtypeStruct((KV_W, seq), BF16),
                   jax.ShapeDtypeStruct((KV_W, seq), BF16), jax.ShapeDtypeStruct((D, seq), BF16),
                   jax.ShapeDtypeStruct((D, seq), BF16)),
        in_specs=[pl.BlockSpec((tm, D), lambda i: (i, 0)), _resident((1, D)), _resident((MIX1_IN, D)), col(ROT_HALF),
                  col(ROT_HALF)],
        out_specs=(col(D), col(KV_W), col(KV_W), col(D), col(D)),
        compiler_params=_params(48),
    )(x1, g1, w_t, cos_t, sin_t)


def _band_specs_t(nb, clamp_i):
    per = TQ // BLK
    prev = pl.BlockSpec((KV_W, BLK), lambda i: (0, jnp.maximum(clamp_i(i) * per - 1, 0)))
    cur = pl.BlockSpec((KV_W, TQ), lambda i: (0, clamp_i(i)))
    nxt = pl.BlockSpec((KV_W, BLK), lambda i: (0, jnp.minimum((clamp_i(i) + 1) * per, nb - 1)))
    return [prev, cur, nxt]


def _fill_band(buf, p_ref, c_ref, n_ref):
    buf[:, 0:BLK] = p_ref[...]
    buf[:, BLK:BLK + TQ] = c_ref[...]
    buf[:, BLK + TQ:2 * BLK + TQ] = n_ref[...]


def _band_bias_t(n, nb):
    c = lax.broadcasted_iota(jnp.int32, (3 * BLK, BLK), 0)
    r = lax.broadcasted_iota(jnp.int32, (3 * BLK, BLK), 1)
    ok = (c >= r) & (c <= r + 2 * BLK) & ((c >= BLK) | (n > 0)) & ((c < 2 * BLK) | (n < nb - 1))
    bias = jnp.where(ok, 0.0, NEG_INF).astype(F32)
    return jnp.concatenate([bias] * GQA, axis=1)


def _heads_t(ref, kv, c0):
    return jnp.concatenate([ref[(kv * GQA + g) * HD:(kv * GQA + g + 1) * HD, c0:c0 + BLK] for g in range(GQA)], axis=1)


def _row4(ref, kv, c0):
    return jnp.concatenate([ref[kv * GQA + g:kv * GQA + g + 1, c0:c0 + BLK] for g in range(GQA)], axis=1)


def _sink_row(sink_ref, kv):
    return jnp.concatenate([jnp.full((1, BLK), sink_ref[kv * GQA + g], F32) for g in range(GQA)], axis=1)


def _l1_attn_fwd(qt, kt, vt, gatet, x1, tgt, wout, gf, sink):
    seq = x1.shape[0]
    nq, nb = seq // TQ, seq // BLK

    def body(q_ref, gate_ref, kp_ref, k_ref, kn_ref, vp_ref, v_ref, vn_ref, x1_ref, tgt_ref, wo_ref, gf_ref, sink_ref,
             dx2_ref, dx2b_ref, yt_ref, att_ref, lse_ref, loss_ref, dgf_ref, kbuf, vbuf, att_scr):
        i = pl.program_id(0)

        @pl.when(i == 0)
        def _():
            loss_ref[...] = jnp.zeros_like(loss_ref)
            dgf_ref[...] = jnp.zeros_like(dgf_ref)

        _fill_band(kbuf, kp_ref, k_ref, kn_ref)
        _fill_band(vbuf, vp_ref, v_ref, vn_ref)
        for j in range(TQ // BLK):
            c0 = j * BLK
            bias = _band_bias_t(i * (TQ // BLK) + j, nb)
            lse_rows = []
            for kv in range(N_KV):
                rows = slice(kv * HD, (kv + 1) * HD)
                q4 = _heads_t(q_ref, kv, c0)
                st = _tn(kbuf[rows, c0:c0 + 3 * BLK], q4) * SCALE + bias
                sk = _sink_row(sink_ref, kv)
                m = jnp.maximum(jnp.max(st, axis=0, keepdims=True), sk)
                p = jnp.exp(st - m)
                den = jnp.sum(p, axis=0, keepdims=True) + jnp.exp(sk - m)
                ot = _mm(vbuf[rows, c0:c0 + 3 * BLK], p.astype(BF16)) / den
                lse = m + jnp.log(den)
                for g in range(GQA):
                    h = kv * GQA + g
                    att_scr[h * HD:(h + 1) * HD, c0:c0 + BLK] = ot[:, g * BLK:(g + 1) * BLK]
                    lse_rows.append(lse[:, g * BLK:(g + 1) * BLK])
            lse_ref[:, c0:c0 + BLK] = jnp.concatenate(lse_rows, axis=0)

        att = att_scr[...]
        gate = gate_ref[...].astype(F32)
        yt = (att * (gate * jax.nn.sigmoid(gate))).astype(BF16)
        yt_ref[...] = yt
        att_ref[...] = att.astype(BF16)
        x2 = x1_ref[...] + _mm(yt.T, wo_ref[...])
        r = lax.rsqrt(jnp.mean(x2 * x2, axis=1, keepdims=True) + EPS)
        xn = x2 * r
        diff = xn * gf_ref[...] - tgt_ref[...]
        loss_ref[...] += 0.5 * jnp.sum(jnp.mean(diff * diff, axis=1, keepdims=True), axis=0, keepdims=True)
        dout = diff * (1.0 / D)
        dgf_ref[...] += jnp.sum(dout * xn, axis=0, keepdims=True)
        dxn = dout * gf_ref[...]
        dx2 = r * (dxn - xn * jnp.mean(dxn * xn, axis=1, keepdims=True))
        dx2_ref[...] = dx2
        dx2b_ref[...] = dx2.astype(BF16)

    ident = lambda i: i
    row = pl.BlockSpec((TQ, D), lambda i: (i, 0))
    col = lambda rows: pl.BlockSpec((rows, TQ), lambda i: (0, i))
    return pl.pallas_call(
        body, grid=(nq,), name="l1_attn_fwd",
        out_shape=(jax.ShapeDtypeStruct((seq, D), F32), jax.ShapeDtypeStruct((seq, D), BF16),
                   jax.ShapeDtypeStruct((D, seq), BF16), jax.ShapeDtypeStruct((D, seq), BF16),
                   jax.ShapeDtypeStruct((N_HEADS, seq), F32), jax.ShapeDtypeStruct((1, 1), F32),
                   jax.ShapeDtypeStruct((1, D), F32)),
        in_specs=[col(D), col(D)] + _band_specs_t(nb, ident) + _band_specs_t(nb, ident) + [
            row, row, _resident((D, D)), _resident((1, D)), pl.BlockSpec(memory_space=pltpu.SMEM)],
        out_specs=(row, row, col(D), col(D), col(N_HEADS), pl.BlockSpec((1, 1), lambda i: (0, 0)),
                   pl.BlockSpec((1, D), lambda i: (0, 0))),
        scratch_shapes=[pltpu.VMEM((KV_W, TQ + 2 * BLK), BF16), pltpu.VMEM((KV_W, TQ + 2 * BLK), BF16),
                        pltpu.VMEM((D, TQ), F32)],
        compiler_params=_params(56),
    )(qt, gatet, kt, kt, kt, vt, vt, vt, x1, tgt, wout, gf, sink)


def _l1_attn_bwd(dx2b, wout, qt, kt, vt, gatet, att, lse, sink):
    seq = dx2b.shape[0]
    nq, nb = seq // TQ, seq // BLK

    def body(dx_ref, wo_ref, q_ref, gate_ref, kp_ref, k_ref, kn_ref, vp_ref, v_ref, vn_ref, att_ref, lse_ref, sink_ref,
             dq_ref, dgate_ref, dk_ref, dv_ref, dsink_ref, kbuf, vbuf, dkacc, dvacc, dat_scr, delta_scr, dsacc):
        i = pl.program_id(0)

        @pl.when(i == 0)
        def _():
            dkacc[...] = jnp.zeros_like(dkacc)
            dvacc[...] = jnp.zeros_like(dvacc)
            dsacc[...] = jnp.zeros_like(dsacc)

        @pl.when(i > 0)
        def _():
            for acc in (dkacc, dvacc):
                acc[:, 0:2 * BLK] = acc[:, TQ:TQ + 2 * BLK]
                acc[:, 2 * BLK:2 * BLK + TQ] = jnp.zeros((KV_W, TQ), F32)

        @pl.when(i < nq)
        def _():
            _fill_band(kbuf, kp_ref, k_ref, kn_ref)
            _fill_band(vbuf, vp_ref, v_ref, vn_ref)
            dyt = _nt(wo_ref[...], dx_ref[...])
            sg, dsg = _silu_and_grad(gate_ref[...].astype(F32))
            attf = att_ref[...].astype(F32)
            dat = dyt * sg
            dat_scr[...] = dat.astype(BF16)
            dgate_ref[...] = (dyt * attf * dsg).astype(BF16)
            dl = dat * attf
            delta_scr[...] = jnp.concatenate(
                [jnp.sum(dl[h * HD:(h + 1) * HD, :], axis=0, keepdims=True) for h in range(N_HEADS)], axis=0)
            for j in range(TQ // BLK):
                c0 = j * BLK
                bias = _band_bias_t(i * (TQ // BLK) + j, nb)
                for kv in range(N_KV):
                    rows = slice(kv * HD, (kv + 1) * HD)
                    q4 = _heads_t(q_ref, kv, c0)
                    do4 = _heads_t(dat_scr, kv, c0)
                    lse4 = _row4(lse_ref, kv, c0)
                    delta4 = _row4(delta_scr, kv, c0)
                    kth = kbuf[rows, c0:c0 + 3 * BLK]
                    vth = vbuf[rows, c0:c0 + 3 * BLK]
                    p = jnp.exp(_tn(kth, q4) * SCALE + bias - lse4)
                    dp = _tn(vth, do4)
                    ds = (p * (dp - delta4) * SCALE).astype(BF16)
                    dq4 = _mm(kth, ds)
                    dkacc[rows, c0:c0 + 3 * BLK] += _nt(q4, ds)
                    dvacc[rows, c0:c0 + 3 * BLK] += _nt(do4, p.astype(BF16))
                    dsk = -jnp.exp(_sink_row(sink_ref, kv) - lse4) * delta4
                    for g in range(GQA):
                        h = kv * GQA + g
                        dq_ref[h * HD:(h + 1) * HD, c0:c0 + BLK] = dq4[:, g * BLK:(g + 1) * BLK].astype(BF16)
                        dsacc[h:h + 1, :] += dsk[:, g * BLK:(g + 1) * BLK]

        dk_ref[...] = dkacc[:, 0:TQ].astype(BF16)
        dv_ref[...] = dvacc[:, 0:TQ].astype(BF16)

        @pl.when(i == nq)
        def _():
            dsink_ref[...] = jnp.broadcast_to(jnp.sum(dsacc[...], axis=1, keepdims=True), (N_HEADS, LANES))

    clamp = lambda i: jnp.minimum(i, nq - 1)
    row = pl.BlockSpec((TQ, D), lambda i: (clamp(i), 0))
    col = lambda rows: pl.BlockSpec((rows, TQ), lambda i: (0, clamp(i)))
    pad = pl.BlockSpec((KV_W, TQ), lambda i: (0, i))
    return pl.pallas_call(
        body, grid=(nq + 1,), name="l1_attn_bwd",
        out_shape=(jax.ShapeDtypeStruct((D, seq), BF16), jax.ShapeDtypeStruct((D, seq), BF16),
                   jax.ShapeDtypeStruct((KV_W, seq + TQ), BF16), jax.ShapeDtypeStruct((KV_W, seq + TQ), BF16),
                   jax.ShapeDtypeStruct((N_HEADS, LANES), F32)),
        in_specs=[row, _resident((D, D)), col(D), col(D)] + _band_specs_t(nb, clamp) + _band_specs_t(nb, clamp) + [
            col(D), col(N_HEADS), pl.BlockSpec(memory_space=pltpu.SMEM)],
        out_specs=(col(D), col(D), pad, pad, pl.BlockSpec((N_HEADS, LANES), lambda i: (0, 0))),
        scratch_shapes=[pltpu.VMEM((KV_W, TQ + 2 * BLK), BF16), pltpu.VMEM((KV_W, TQ + 2 * BLK), BF16),
                        pltpu.VMEM((KV_W, TQ + 2 * BLK), F32), pltpu.VMEM((KV_W, TQ + 2 * BLK), F32),
                        pltpu.VMEM((D, TQ), BF16), pltpu.VMEM((N_HEADS, TQ), F32), pltpu.VMEM((N_HEADS, LANES), F32)],
        compiler_params=_params(56),
    )(dx2b, wout, qt, gatet, kt, kt, kt, vt, vt, vt, att, lse, sink)


def _l1_in_proj_bwd(dq_r, dk_r, dv, dgate, cos_t, sin_t, w, x1, g1, dx2):
    seq = x1.shape[0]
    tm = 512

    def body(dq_ref, dk_ref, dv_ref, dg_ref, c_ref, s_ref, w_ref, x_ref, g_ref, dres_ref,
             dx_ref, dxb_ref, dz_ref, dn_ref):
        @pl.when(pl.program_id(0) == 0)
        def _():
            dn_ref[...] = jnp.zeros_like(dn_ref)

        c, s = c_ref[...], s_ref[...]
        dq = _rope_t(dq_ref[...].astype(F32), c, s, N_HEADS, -1).astype(BF16)
        dk = _rope_t(dk_ref[...].astype(F32), c, s, N_KV, -1).astype(BF16)
        dz = jnp.concatenate([dq, dk, dv_ref[...], dg_ref[...]], axis=0)
        dz_ref[...] = dz
        half = MIX1_IN // 2
        dh = (_mm(w_ref[:, 0:half], dz[0:half]) + _mm(w_ref[:, half:MIX1_IN], dz[half:MIX1_IN])).T
        xf = x_ref[...]
        r = lax.rsqrt(jnp.mean(xf * xf, axis=1, keepdims=True) + EPS)
        xn = xf * r
        dn_ref[...] += jnp.sum(dh * xn, axis=0, keepdims=True)
        dxn = dh * g_ref[...]
        dx = dres_ref[...] + r * (dxn - xn * jnp.mean(dxn * xn, axis=1, keepdims=True))
        dx_ref[...] = dx
        dxb_ref[...] = dx.astype(BF16)

    row = pl.BlockSpec((tm, D), lambda i: (i, 0))
    col = lambda rows: pl.BlockSpec((rows, tm), lambda i: (0, i))
    return pl.pallas_call(
        body, grid=(seq // tm,), name="l1_in_proj_bwd",
        out_shape=(jax.ShapeDtypeStruct((seq, D), F32), jax.ShapeDtypeStruct((seq, D), BF16),
                   jax.ShapeDtypeStruct((MIX1_IN, seq), BF16), jax.ShapeDtypeStruct((1, D), F32)),
        in_specs=[col(D), col(KV_W), col(KV_W), col(D), col(ROT_HALF), col(ROT_HALF), _resident((D, MIX1_IN)), row,
                  _resident((1, D)), row],
        out_specs=(row, row, col(MIX1_IN), pl.BlockSpec((1, D), lambda i: (0, 0))),
        compiler_params=_params(48),
    )(dq_r, dk_r, dv, dgate, cos_t, sin_t, w, x1, g1, dx2)


def _l0_mix_bwd(dx1b, wout, za, bx, bg, ws, ws_t, bias, gv, wg, wg_t, scale):
    seq = dx1b.shape[0]
    ts = 256
    n_tiles = seq // ts

    def body(dx_ref, wo_ref, za_ref, bx_ref, bxp_ref, bxn_ref, bg_ref, ws_ref, wst_ref, bias_ref, gv_ref, wg_ref,
             wgt_ref, sc_ref,
             dz_ref, dp_ref, catt_ref, dws_ref, dbias_ref, dgv_ref, dsc_ref, dwg_ref, db_ref, xe_ref, *tmp_refs):
        i = pl.program_id(0)

        @pl.when(i == 0)
        def _():
            for r_ in (dws_ref, dbias_ref, dgv_ref, dsc_ref, dwg_ref, db_ref):
                r_[...] = jnp.zeros_like(r_)

        dxb = dx_ref[...]
        dya = _nt(dxb, wo_ref[0:D, :])
        dyb = _nt(dxb, wo_ref[D:2 * D, :])

        u, du = _gelu_and_grad(za_ref[:, 0:D].astype(F32))
        vg, dvg_dz = _gelu_and_grad(za_ref[:, D:2 * D].astype(F32))
        rv = lax.rsqrt(jnp.mean(vg * vg, axis=1, keepdims=True) + EPS)
        vnorm = vg * rv
        gvw = gv_ref[...]
        vnb = (vnorm * gvw).astype(BF16)
        mixed = _spatial_mix(ws_ref, vnb, bias_ref[...], ts)
        sga, dsga = _silu_and_grad(za_ref[:, 2 * D:3 * D].astype(F32))
        um = u * mixed
        ya = (um * sga).astype(BF16)
        t = dya * sga
        dz_ref[:, 0:D] = (t * mixed * du).astype(BF16)
        dz_ref[:, 2 * D:3 * D] = (dya * um * dsga).astype(BF16)
        dmixed = t * u
        dmb = dmixed.astype(BF16)
        dvn_rows = []
        dbias = jnp.zeros((CHUNK, D), F32)
        for c in range(ts // CHUNK):
            rows = slice(c * CHUNK, (c + 1) * CHUNK)
            dbias = dbias + dmixed[rows, :]
            parts = []
            for h in range(A_GROUPS):
                cols = slice(h * GDIM, (h + 1) * GDIM)
                dws_ref[h] += _nt(dmb[rows, cols], vnb[rows, cols])
                parts.append(_mm(wst_ref[h], dmb[rows, cols]))
            dvn_rows.append(jnp.concatenate(parts, axis=1))
        dbias_ref[...] += dbias
        dvn = jnp.concatenate(dvn_rows, axis=0)
        dgv_ref[...] += jnp.sum(dvn * vnorm, axis=0, keepdims=True)
        dxn = dvn * gvw
        dvg = rv * (dxn - vnorm * jnp.mean(dxn * vnorm, axis=1, keepdims=True))
        dz_ref[:, D:2 * D] = (dvg * dvg_dz).astype(BF16)

        _fill_halo(xe_ref, bx_ref[...], bxp_ref, bxn_ref, i, n_tiles, ts)
        pb = _pool_forward(xe_ref, tmp_refs, ts, i * ts, seq).astype(BF16)
        ypre = jnp.concatenate([_mm(pb[:, g * GDIM:(g + 1) * GDIM], wg_ref[g]) for g in range(4)], axis=1)
        sc = sc_ref[...]
        y = ypre * sc
        sgb, dsgb = _silu_and_grad(bg_ref[...].astype(F32))
        yb = (y * sgb).astype(BF16)
        dy_b = dyb * sgb
        dz_ref[:, 3 * D:4 * D] = jnp.zeros((ts, D), BF16)
        dz_ref[:, 4 * D:5 * D] = (dyb * y * dsgb).astype(BF16)
        dsc_ref[...] += jnp.sum(dy_b * ypre, axis=0, keepdims=True)
        dypre = (dy_b * sc).astype(BF16)
        dps = []
        for g in range(4):
            cols = slice(g * GDIM, (g + 1) * GDIM)
            dwg_ref[g] += _tn(pb[:, cols], dypre[:, cols])
            dps.append(_mm(dypre[:, cols], wgt_ref[g]))
        dp_ref[...] = jnp.concatenate(dps, axis=1)
        catt_ref[...] = jnp.concatenate([ya, yb], axis=1).T

        @pl.when(i == n_tiles - 1)
        def _():
            for h in range(A_GROUPS):
                tot = jnp.sum(dbias_ref[:, h * GDIM:(h + 1) * GDIM].T, axis=0, keepdims=True)
                db_ref[pl.ds(h * 8, 8), :] = jnp.broadcast_to(tot, (8, CHUNK))

    prev, nxt = _halo_specs(ts, seq, D)
    row = lambda w_: pl.BlockSpec((ts, w_), lambda i: (i, 0))
    acc = lambda shape: pl.BlockSpec(shape, lambda i: (0,) * len(shape))
    return pl.pallas_call(
        body, grid=(n_tiles,), name="l0_mix_bwd",
        out_shape=(jax.ShapeDtypeStruct((seq, MIX0_IN), BF16), jax.ShapeDtypeStruct((seq, D), F32),
                   jax.ShapeDtypeStruct((2 * D, seq), BF16),
                   jax.ShapeDtypeStruct((4, CHUNK, CHUNK), F32), jax.ShapeDtypeStruct((CHUNK, D), F32),
                   jax.ShapeDtypeStruct((1, D), F32), jax.ShapeDtypeStruct((1, D), F32),
                   jax.ShapeDtypeStruct((4, GDIM, GDIM), F32), jax.ShapeDtypeStruct((32, CHUNK), F32)),
        in_specs=[row(D), _resident((2 * D, D)), row(3 * D), row(D), prev, nxt, row(D), _resident((4, CHUNK, CHUNK)),
                  _resident((4, CHUNK, CHUNK)), _resident((CHUNK, D)), _resident((1, D)), _resident((4, GDIM, GDIM)),
                  _resident((4, GDIM, GDIM)), _resident((1, D))],
        out_specs=(row(MIX0_IN), row(D), pl.BlockSpec((2 * D, ts), lambda i: (0, i)),
                   acc((4, CHUNK, CHUNK)), acc((CHUNK, D)), acc((1, D)), acc((1, D)), acc((4, GDIM, GDIM)),
                   acc((32, CHUNK))),
        scratch_shapes=_pool_scratch(ts),
        compiler_params=_params(56),
    )(dx1b, wout, za, bx, bx, bx, bg, ws, ws_t, bias, gv, wg, wg_t, scale)


def _l0_pool_bwd(dp, dz):
    seq = dp.shape[0]
    ts = 512
    n_tiles = seq // ts
    ext = ts + 2 * POOL_HALO

    def body(dp_ref, dpp_ref, dpn_ref, dz_ref, out_ref, qe_ref, *tmp_refs):
        i = pl.program_id(0)
        _fill_halo(qe_ref, dp_ref[...], dpp_ref, dpn_ref, i, n_tiles, ts)
        te = i * ts - POOL_HALO + lax.broadcasted_iota(jnp.int32, (ext, 1), 0)
        for gi, w in enumerate(POOL_WINDOWS):
            hw = w // 2
            cols = slice(gi * GDIM, (gi + 1) * GDIM)
            cnt = jnp.maximum(jnp.minimum(te + hw, seq) - jnp.maximum(te - hw, 0), 1).astype(F32)
            qe_ref[pl.ds(0, ext), cols] = qe_ref[pl.ds(0, ext), cols] / cnt
        outs = []
        for gi, w in enumerate(POOL_WINDOWS):
            cols = slice(gi * GDIM, (gi + 1) * GDIM)
            outs.append(_window_sums(qe_ref, tmp_refs, ts, cols, w, 1) - dp_ref[:, cols])
        out_ref[...] = jnp.concatenate(outs, axis=1).astype(BF16)

    prev, nxt = _halo_specs(ts, seq, D)
    row = pl.BlockSpec((ts, D), lambda i: (i, 0))
    return pl.pallas_call(
        body, grid=(n_tiles,), name="l0_pool_bwd",
        out_shape=jax.ShapeDtypeStruct(dz.shape, BF16),
        in_specs=[row, prev, nxt, pl.BlockSpec(memory_space=pl.ANY)],
        out_specs=pl.BlockSpec((ts, D), lambda i: (i, 3)),
        input_output_aliases={3: 0},
        scratch_shapes=_pool_scratch(ts),
        compiler_params=_params(32),
    )(dp, dp, dp, dz)


def _l0_in_proj_bwd(dz, w, x, g0, dx1):
    seq = x.shape[0]
    tm = 512

    def body(dz_ref, w_ref, x_ref, g_ref, dres_ref, dx_ref, dn_ref):
        @pl.when(pl.program_id(0) == 0)
        def _():
            dn_ref[...] = jnp.zeros_like(dn_ref)

        dh = _nt(dz_ref[...], w_ref[...])
        xf = x_ref[...]
        r = lax.rsqrt(jnp.mean(xf * xf, axis=1, keepdims=True) + EPS)
        xn = xf * r
        dn_ref[...] += jnp.sum(dh * xn, axis=0, keepdims=True)
        dxn = dh * g_ref[...]
        dx_ref[...] = dres_ref[...] + r * (dxn - xn * jnp.mean(dxn * xn, axis=1, keepdims=True))

    row = lambda w_: pl.BlockSpec((tm, w_), lambda i: (i, 0))
    return pl.pallas_call(
        body, grid=(seq // tm,), name="l0_in_proj_bwd",
        out_shape=(jax.ShapeDtypeStruct((seq, D), F32), jax.ShapeDtypeStruct((1, D), F32)),
        in_specs=[row(MIX0_IN), _resident((D, MIX0_IN)), row(D), _resident((1, D)), row(D)],
        out_specs=(row(D), pl.BlockSpec((1, D), lambda i: (0, 0))),
        compiler_params=_params(56),
    )(dz, w, x, g0, dx1)


def _dw_matmul(a_t, b, name, b_transposed=False, tn=1024, ts=1024, col_block=None):
    k, seq = a_t.shape
    n = b.shape[0] if b_transposed else b.shape[1]
    tn = min(n, tn)
    assert seq % ts == 0 and n % tn == 0 and (col_block is None or tn % col_block == 0)
    n_s = seq // ts
    per = 1 if col_block is None else tn // col_block

    def body(a_ref, b_ref, o_ref, ob_ref, acc_ref):
        s = pl.program_id(1)

        @pl.when(s == 0)
        def _():
            acc_ref[...] = jnp.zeros_like(acc_ref)

        acc_ref[...] += _nt(a_ref[...], b_ref[...]) if b_transposed else _mm(a_ref[...], b_ref[...])

        @pl.when(s == n_s - 1)
        def _():
            acc = acc_ref[...]
            if col_block is None:
                o_ref[...] = acc
                ob_ref[...] = acc.astype(BF16)
            else:
                for i in range(per):
                    piece = acc[:, i * col_block:(i + 1) * col_block]
                    o_ref[i] = piece
                    ob_ref[i] = piece.astype(BF16)

    b_spec = (pl.BlockSpec((tn, ts), lambda j, s: (j, s)) if b_transposed else pl.BlockSpec((ts, tn), lambda j, s: (s, j)))
    if col_block is None:
        shape, o_spec = (k, n), pl.BlockSpec((k, tn), lambda j, s: (0, j))
    else:
        shape, o_spec = (n // col_block, k, col_block), pl.BlockSpec((per, k, col_block), lambda j, s: (j, 0, 0))
    return pl.pallas_call(
        body, grid=(n // tn, n_s), name=name,
        out_shape=(jax.ShapeDtypeStruct(shape, F32), jax.ShapeDtypeStruct(shape, BF16)),
        in_specs=[pl.BlockSpec((k, ts), lambda j, s: (0, s)), b_spec],
        out_specs=(o_spec, o_spec),
        scratch_shapes=[pltpu.VMEM((k, tn), F32)],
        compiler_params=_params(56, 2),
    )(a_t, b)


ROW_TILES = 8


def _cast_shards(shards):
    n = len(shards)

    def body(*refs):
        for a in range(n):
            refs[n + a][...] = refs[a][...].astype(BF16)

    vm = pl.BlockSpec(memory_space=pltpu.VMEM)
    return pl.pallas_call(body, name="cast_weights", out_shape=[jax.ShapeDtypeStruct(t.shape, BF16) for t in shards],
                          in_specs=[vm] * n, out_specs=[vm] * n, compiler_params=_params(32, 0))(*shards)


def _adamw_math(w, g, m, v):
    m2 = ADAM_B1 * m + (1.0 - ADAM_B1) * g
    v2 = ADAM_B2 * v + (1.0 - ADAM_B2) * (g * g)
    m_hat = m2 / (1.0 - ADAM_B1 ** ADAM_STEP)
    v_hat = v2 / (1.0 - ADAM_B2 ** ADAM_STEP)
    delta = -ADAM_LR * (m_hat / (jnp.sqrt(v_hat) + ADAM_EPS) + ADAM_WD * w)
    return delta, m2, v2


def _final_sum_adamw(g_list, recv_list, me, w_list, m_list, v_list):
    n = len(w_list)

    def body(me_ref, *refs):
        own, recv, w, m, v = (refs[k * n:(k + 1) * n] for k in range(5))
        outs = [refs[(5 + k) * n:(6 + k) * n] for k in range(4)]
        for a in range(n):
            g = own[a][...]
            for k in range(N_DEV - 1):
                g = g + recv[a][k].astype(F32)
            delta, m2, v2 = _adamw_math(w[a][...], g, m[a][...], v[a][...])
            for o_ref, val in zip((outs[0][a], outs[1][a], outs[2][a], outs[3][a]), (g, delta, m2, v2)):
                o_ref[...] = val

    own_specs, flat, wire, shapes = [], [], [], []
    for t in w_list:
        rows, width = t.shape
        tr = rows // ROW_TILES
        own_specs.append(pl.BlockSpec((None, tr, width), lambda i, me: (me[0], i, 0)))
        flat.append(pl.BlockSpec((tr, width), lambda i, me: (i, 0)))
        wire.append(pl.BlockSpec((N_DEV - 1, tr, width), lambda i, me: (0, i, 0)))
        shapes.append(jax.ShapeDtypeStruct((rows, width), F32))
    out = pl.pallas_call(
        body, name="grad_sum_adamw", out_shape=shapes * 4,
        grid_spec=pltpu.PrefetchScalarGridSpec(
            num_scalar_prefetch=1, grid=(ROW_TILES,), in_specs=own_specs + wire + flat * 3, out_specs=flat * 4),
        compiler_params=_params(40),
    )(me, *g_list, *recv_list, *w_list, *m_list, *v_list)
    return [out[k * n:(k + 1) * n] for k in range(4)]


SMALL_NAMES = ("norm_0", "a_v_norm_0", "b_scale_0", "norm_1", "final_norm", "a_spatial_w_0", "a_spatial_b_0", "sink_1")
SMALL_VIEWS = ((8, LANES),) * 5 + ((4 * CHUNK, LANES), (4, LANES), (1, N_HEADS))
SMALL_ROW0 = (0, 8, 16, 24, 32, 40, 552, 560)
SMALL_ROWS = 568


def _small_sum_adamw(early, late, w_list, m_list, v_list):
    n = len(w_list)

    def body(e_ref, l_ref, *refs):
        gtot, first = e_ref[0], l_ref[0]
        for d in range(1, N_DEV):
            gtot = gtot + e_ref[d]
            first = first + l_ref[d]
        for a, ((rows, width), r0) in enumerate(zip(SMALL_VIEWS, SMALL_ROW0)):
            g = first if SMALL_NAMES[a] == "norm_0" else gtot[r0:r0 + rows, 0:width]
            delta, m2, v2 = _adamw_math(refs[a][...], g, refs[n + a][...], refs[2 * n + a][...])
            for k, val in enumerate((g, delta, m2, v2)):
                refs[(3 + k) * n + a][...] = val

    vm = pl.BlockSpec(memory_space=pltpu.VMEM)
    shapes = [jax.ShapeDtypeStruct(s, F32) for s in SMALL_VIEWS]
    out = pl.pallas_call(
        body, name="small_sum_adamw", out_shape=shapes * 4,
        in_specs=[vm, vm] + [vm] * (3 * n), out_specs=[vm] * (4 * n),
    )(early, late, *w_list, *m_list, *v_list)
    return [out[k * n:(k + 1) * n] for k in range(4)]


def _all_gather_columns(blks):
    n = len(blks)

    def body(*refs):
        ins, outs = refs[:n], refs[n:2 * n]
        send_sems, recv_sems, local_sems = refs[2 * n:]
        x, y, c = lax.axis_index("x"), lax.axis_index("y"), lax.axis_index("c")
        me, sibling = (x, y, c), (x, y, 1 - c)
        chips = [(1 - x, y), (x, 1 - y), (1 - x, 1 - y)]

        def slot(a, px, py, pc):
            width = blks[a].shape[1]
            return outs[a].at[:, pl.ds(pl.multiple_of((4 * px + 2 * py + pc) * width, LANES), width)]

        def copy(k, a, block, to, from_input=False):
            return pltpu.make_async_remote_copy(
                src_ref=ins[a] if from_input else slot(a, *block), dst_ref=slot(a, *block),
                send_sem=send_sems.at[k, a], recv_sem=recv_sems.at[k, a], device_id=to, device_id_type=MESH)

        mine = [pltpu.make_async_copy(ins[a], slot(a, *me), local_sems.at[a]) for a in range(n)]
        first = []
        for a in range(n):
            first.append(copy(0, a, me, sibling, from_input=True))
            first += [copy(1 + j, a, me, (*chip, c), from_input=True) for j, chip in enumerate(chips)]
        for cp in mine + first:
            cp.start()
        passed = []
        for j, chip in enumerate(chips):
            for a in range(n):
                copy(1 + j, a, (*chip, c), me).wait_recv()
                passed.append(copy(4 + j, a, (*chip, c), sibling))
                passed[-1].start()
        for a in range(n):
            copy(0, a, sibling, me).wait_recv()
        for j, chip in enumerate(chips):
            for a in range(n):
                copy(4 + j, a, (*chip, 1 - c), me).wait_recv()
        for cp in first + passed:
            cp.wait_send()
        for cp in mine:
            cp.wait()

    any_spec = pl.BlockSpec(memory_space=pl.ANY)
    return pl.pallas_call(
        body, name="weights_all_gather",
        out_shape=[jax.ShapeDtypeStruct((t.shape[0], N_DEV * t.shape[1]), t.dtype) for t in blks],
        in_specs=[any_spec] * n, out_specs=[any_spec] * n,
        scratch_shapes=[pltpu.SemaphoreType.DMA((7, n)), pltpu.SemaphoreType.DMA((7, n)), pltpu.SemaphoreType.DMA((n,))],
    )(*blks)


PEER_FLIPS = tuple((fx, fy, fc) for fx in (0, 1) for fy in (0, 1) for fc in (0, 1))[1:]


def _sequencer_all_gather(blks, name, collective_id):
    n = len(blks)

    def body(*refs):
        ins, outs = refs[:n], refs[n:2 * n]
        send_sems, recv_sems, local_sems = refs[2 * n:]
        x, y, c = lax.axis_index("x"), lax.axis_index("y"), lax.axis_index("c")
        peers = [(x ^ fx, y ^ fy, c ^ fc) for fx, fy, fc in PEER_FLIPS]
        barrier = pltpu.get_barrier_semaphore()
        for peer in peers:
            pl.semaphore_signal(barrier, inc=1, device_id=peer, device_id_type=MESH)
        pl.semaphore_wait(barrier, len(peers))
        me = 4 * x + 2 * y + c
        copies = [pltpu.make_async_remote_copy(
            src_ref=ins[a], dst_ref=outs[a].at[me], send_sem=send_sems.at[k, a], recv_sem=recv_sems.at[k, a],
            device_id=peer, device_id_type=MESH) for k, peer in enumerate(peers) for a in range(n)]
        mine = [pltpu.make_async_copy(ins[a], outs[a].at[me], local_sems.at[a]) for a in range(n)]
        for cp in copies + mine:
            cp.start()
        for cp in copies + mine:
            cp.wait()

    return pl.kernel(
        body, out_type=[jax.ShapeDtypeStruct((N_DEV,) + t.shape, t.dtype) for t in blks],
        mesh=plsc.ScalarSubcoreMesh(axis_name="sequencer", num_cores=1), name=name,
        scratch_types=[pltpu.SemaphoreType.DMA((7, n)), pltpu.SemaphoreType.DMA((7, n)), pltpu.SemaphoreType.DMA((n,))],
        compiler_params=pltpu.CompilerParams(collective_id=collective_id),
    )(*blks)


def _sequencer_scatter(g_list, name, collective_id):
    n = len(g_list)

    def body(*refs):
        ins, outs = refs[:n], refs[n:2 * n]
        send_sems, recv_sems = refs[2 * n:]
        x, y, c = lax.axis_index("x"), lax.axis_index("y"), lax.axis_index("c")
        peers = [(x ^ fx, y ^ fy, c ^ fc) for fx, fy, fc in PEER_FLIPS]
        barrier = pltpu.get_barrier_semaphore()
        for peer in peers:
            pl.semaphore_signal(barrier, inc=1, device_id=peer, device_id_type=MESH)
        pl.semaphore_wait(barrier, len(peers))
        copies = [pltpu.make_async_remote_copy(
            src_ref=ins[a].at[4 * px + 2 * py + pc], dst_ref=outs[a].at[k], send_sem=send_sems.at[k, a],
            recv_sem=recv_sems.at[k, a], device_id=(px, py, pc), device_id_type=MESH)
            for k, (px, py, pc) in enumerate(peers) for a in range(n)]
        for cp in copies:
            cp.start()
        for cp in copies:
            cp.wait()

    return pl.kernel(
        body, out_type=[jax.ShapeDtypeStruct((N_DEV - 1,) + g.shape[1:], g.dtype) for g in g_list],
        mesh=plsc.ScalarSubcoreMesh(axis_name="sequencer", num_cores=1), name=name,
        scratch_types=[pltpu.SemaphoreType.DMA((7, n)), pltpu.SemaphoreType.DMA((7, n))],
        compiler_params=pltpu.CompilerParams(collective_id=collective_id),
    )(*g_list)


def _direct_all_gather(blk, name):
    def body(g_ref, out_ref, send_sems, recv_sems, local_sem):
        x, y, c = lax.axis_index("x"), lax.axis_index("y"), lax.axis_index("c")
        me = 4 * x + 2 * y + c
        copies = [pltpu.make_async_remote_copy(
            src_ref=g_ref, dst_ref=out_ref.at[me], send_sem=send_sems.at[k], recv_sem=recv_sems.at[k],
            device_id=(x ^ fx, y ^ fy, c ^ fc), device_id_type=MESH) for k, (fx, fy, fc) in enumerate(PEER_FLIPS)]
        copies.append(pltpu.make_async_copy(g_ref, out_ref.at[me], local_sem))
        for cp in copies:
            cp.start()
        for cp in copies:
            cp.wait()

    any_spec = pl.BlockSpec(memory_space=pl.ANY)
    return pl.pallas_call(
        body, name=name, out_shape=jax.ShapeDtypeStruct((N_DEV,) + blk.shape, blk.dtype),
        in_specs=[any_spec], out_specs=any_spec,
        scratch_shapes=[pltpu.SemaphoreType.DMA((7,)), pltpu.SemaphoreType.DMA((7,)), pltpu.SemaphoreType.DMA],
    )(blk)


def _shard_views(w_in_0, b_group_w_0, w_out_0, w_in_1, w_out_1):
    return [w_in_0, b_group_w_0.reshape(4 * 32, GDIM), w_out_0, w_in_1, w_out_1]


def _small_views(named):
    return [named[name].reshape(view) for name, view in zip(SMALL_NAMES, SMALL_VIEWS)]


def _pack_small_grads(named):
    rows = []
    for name, (r, w) in zip(SMALL_NAMES, SMALL_VIEWS):
        pad_r = -r % 8
        if name in named:
            rows.append(jnp.pad(named[name].reshape(r, w), ((0, pad_r), (0, LANES - w))))
        else:
            rows.append(jnp.zeros((r + pad_r, LANES), F32))
    return jnp.concatenate(rows, axis=0)


def _device_blocks(t, axis):
    shape = t.shape
    t = t.reshape(shape[:axis] + (N_DEV, shape[axis] // N_DEV) + shape[axis + 1:])
    t = jnp.moveaxis(t, axis, 0)
    return t.reshape(N_DEV, -1, shape[-1] if axis != len(shape) - 1 else shape[-1] // N_DEV)


def kernel(x, norm_0, w_in_0, a_v_norm_0, a_spatial_w_0, a_spatial_b_0, b_group_w_0, b_scale_0, w_out_0, norm_1, w_in_1, sink_1, w_out_1, final_norm, loss_target, m_norm_0, m_w_in_0, m_a_v_norm_0, m_a_spatial_w_0, m_a_spatial_b_0, m_b_group_w_0, m_b_scale_0, m_w_out_0, m_norm_1, m_w_in_1, m_sink_1, m_w_out_1, m_final_norm, v_norm_0, v_w_in_0, v_a_v_norm_0, v_a_spatial_w_0, v_a_spatial_b_0, v_b_group_w_0, v_b_scale_0, v_w_out_0, v_norm_1, v_w_in_1, v_sink_1, v_w_out_1, v_final_norm):
    seq = x.shape[1]
    xs = x.reshape(seq, D)
    tgt = loss_target.reshape(seq, D)
    ax, ay, ac = lax.axis_index("x"), lax.axis_index("y"), lax.axis_index("c")
    me = jnp.reshape(4 * ax + 2 * ay + ac, (1,)).astype(jnp.int32)

    shards = _shard_views(w_in_0, b_group_w_0, w_out_0, w_in_1, w_out_1)
    cast = _cast_shards(shards)
    win0 = _all_gather_columns(cast[0:1])[0]
    rest, win0 = lax.optimization_barrier((cast[1:5], win0))
    gathered = (list(_sequencer_all_gather(rest[0:2], "weights_gather_a", 1))
                + list(_sequencer_all_gather(rest[2:4], "weights_gather_b", 2)))

    blocks, received, early = {}, {}, {}
    collective_ids = {"l1": 3, "out0": 4, "in0": 5}

    def after_forward(loss_part, dx2b):
        early["loss"], dx2b = lax.optimization_barrier((lax.psum(loss_part[0, 0], ("x", "y", "c")), dx2b))
        return dx2b

    def scatter(tag, own_blocks, wire_blocks):
        blocks[tag] = own_blocks
        received[tag] = _sequencer_scatter(wire_blocks, "grad_scatter_" + tag, collective_ids[tag])

    def small_early(named):
        early["small"] = _sequencer_all_gather([_pack_small_grads(named)], "small_grad_gather", 6)[0]

    grad_x, d_norm_0 = _local_step(xs, tgt, win0, gathered, norm_0, a_v_norm_0, a_spatial_w_0, a_spatial_b_0, b_scale_0,
                                   norm_1, sink_1, final_norm, after_forward, scatter, small_early)

    order = (("in0", 0), ("in0", 1), ("out0", 0), ("l1", 0), ("l1", 1))
    late = _direct_all_gather(d_norm_0.reshape(8, LANES), "norm_grad_gather")
    shards_late, _ = lax.optimization_barrier((shards, grad_x))
    big = _final_sum_adamw([blocks[t][i] for t, i in order], [received[t][i] for t, i in order], me, shards_late,
                           _shard_views(m_w_in_0, m_b_group_w_0, m_w_out_0, m_w_in_1, m_w_out_1),
                           _shard_views(v_w_in_0, v_b_group_w_0, v_w_out_0, v_w_in_1, v_w_out_1))
    weights = dict(norm_0=norm_0, a_v_norm_0=a_v_norm_0, a_spatial_w_0=a_spatial_w_0, a_spatial_b_0=a_spatial_b_0,
                   b_scale_0=b_scale_0, norm_1=norm_1, sink_1=sink_1, final_norm=final_norm)
    m_small = dict(norm_0=m_norm_0, a_v_norm_0=m_a_v_norm_0, a_spatial_w_0=m_a_spatial_w_0, a_spatial_b_0=m_a_spatial_b_0,
                   b_scale_0=m_b_scale_0, norm_1=m_norm_1, sink_1=m_sink_1, final_norm=m_final_norm)
    v_small = dict(norm_0=v_norm_0, a_v_norm_0=v_a_v_norm_0, a_spatial_w_0=v_a_spatial_w_0, a_spatial_b_0=v_a_spatial_b_0,
                   b_scale_0=v_b_scale_0, norm_1=v_norm_1, sink_1=v_sink_1, final_norm=v_final_norm)
    small = _small_sum_adamw(early["small"], late, _small_views(weights), _small_views(m_small), _small_views(v_small))

    def in_order(kind):
        b = [b_.reshape(s_.shape) for b_, s_ in zip(big[kind], (w_in_0, b_group_w_0, w_out_0, w_in_1, w_out_1))]
        s = {name: t.reshape(weights[name].shape) for name, t in zip(SMALL_NAMES, small[kind])}
        return [s["norm_0"], b[0], s["a_v_norm_0"], s["a_spatial_w_0"], s["a_spatial_b_0"], b[1], s["b_scale_0"], b[2],
                s["norm_1"], b[3], s["sink_1"], b[4], s["final_norm"]]

    return (early["loss"], grad_x.reshape(1, seq, D), *in_order(0), *in_order(1), *in_order(2), *in_order(3))


def _local_step(xs, tgt, win0, gathered, norm_0, a_v_norm_0, a_spatial_w_0, a_spatial_b_0, b_scale_0, norm_1, sink_1,
                final_norm, after_forward, scatter, small_early):
    seq = xs.shape[0]
    ws = a_spatial_w_0.astype(BF16)
    ws_t = jnp.swapaxes(ws, 1, 2)
    bias = jnp.repeat(a_spatial_b_0.T, GDIM, axis=1)
    g0, gv, scale, g1, gf = (t.reshape(1, D) for t in (norm_0, a_v_norm_0, b_scale_0, norm_1, final_norm))
    cos_t, sin_t = _rope_tables_t(seq)

    za, bx, bg, h0_t = _l0_in_proj(xs, g0, win0)
    g_wg, g_wout0, za = lax.optimization_barrier((gathered[0], gathered[1], za))
    wg = g_wg.reshape(N_DEV, 4, 32, GDIM).transpose(1, 0, 2, 3).reshape(4, GDIM, GDIM)
    wg_t = jnp.swapaxes(wg, 1, 2)
    wout0 = g_wout0.reshape(2 * D, D)
    x1 = _l0_mix_fwd(za, bx, bg, xs, ws, bias, gv, wg, scale, wout0)
    g_win1, g_wout1, x1 = lax.optimization_barrier((gathered[2], gathered[3], x1))
    win1 = g_win1.transpose(1, 0, 2).reshape(D, MIX1_IN)
    win1_t = win1.T
    wout1 = g_wout1.reshape(D, D)
    qt, kt, vt, gatet, h1_t = _l1_in_proj(x1, g1, win1_t, cos_t, sin_t)
    dx2, dx2b, y_t, att, lse, loss_part, d_gf = _l1_attn_fwd(qt, kt, vt, gatet, x1, tgt, wout1, gf, sink_1)
    dx2b = after_forward(loss_part, dx2b)

    d_wout1, d_wout1_wire = _dw_matmul(y_t, dx2b, "dw_out_1")
    dx2b, _ = lax.optimization_barrier((dx2b, d_wout1))
    dq_r, dgate, dk_pad, dv_pad, d_sink = _l1_attn_bwd(dx2b, wout1, qt, kt, vt, gatet, att, lse, sink_1)
    dk_r = dk_pad[:, BLK:BLK + seq]
    dv = dv_pad[:, BLK:BLK + seq]
    dx1, dx1b, dz1_t, d_g1 = _l1_in_proj_bwd(dq_r, dk_r, dv, dgate, cos_t, sin_t, win1, x1, g1, dx2)
    d_win1, d_win1_wire = _dw_matmul(h1_t, dz1_t, "dw_in_1", b_transposed=True, tn=1280, col_block=MIX1_IN // N_DEV)
    rows = lambda t: t.reshape(N_DEV, t.shape[0] // N_DEV, t.shape[1])
    scatter("l1", [d_win1, rows(d_wout1)], [d_win1_wire, rows(d_wout1_wire)])

    dz0, dp, cat_t, d_ws, _, d_gv, d_scale, d_wg, d_b = _l0_mix_bwd(
        dx1b, wout0, za, bx, bg, ws, ws_t, bias, gv, wg, wg_t, scale)
    dz0 = _l0_pool_bwd(dp, dz0)
    d_win0, d_win0_wire = _dw_matmul(h0_t, dz0, "dw_in_0", tn=1280, col_block=MIX0_IN // N_DEV)
    d_wg_blocks = _device_blocks(d_wg, 1)
    scatter("in0", [d_win0, d_wg_blocks], [d_win0_wire, d_wg_blocks])
    cat_t, _ = lax.optimization_barrier((cat_t, d_win0))
    d_wout0, d_wout0_wire = _dw_matmul(cat_t, dx1b, "dw_out_0", ts=512)
    scatter("out0", [rows(d_wout0)], [rows(d_wout0_wire)])
    small_early(dict(a_v_norm_0=d_gv, a_spatial_w_0=d_ws, a_spatial_b_0=d_b.reshape(4, 8, CHUNK)[:, 0, :],
                     b_scale_0=d_scale, norm_1=d_g1, sink_1=d_sink[:, 0], final_norm=d_gf))
    dz0, _ = lax.optimization_barrier((dz0, d_wout0))
    return _l0_in_proj_bwd(dz0, win0, xs, g0, dx1)
```

```python
import jax
import jax.numpy as jnp
from jax import lax
from jax.experimental import pallas as pl
from jax.experimental.pallas import tpu as pltpu
from jax.experimental.pallas import tpu_sc as plsc

F32 = jnp.float32
BF16 = jnp.bfloat16

D = 1024
EPS = 1e-6
NEG_INF = -1e30
CHUNK = 128
A_GROUPS = 4
POOL_WINDOWS = (2, 4, 8, 16)
POOL_HALO = 8
GDIM = 256
N_HEADS = 16
N_KV = 4
GQA = 4
HD = 64
BLK = 128
ROT_HALF = 8
ROPE_THETA = 500000.0
SCALE = HD ** -0.5
MIX0_IN = 5 * D
MIX1_IN = 2560
KV_W = N_KV * HD
Q_ROWS, K_ROWS, V_ROWS, G_ROWS = (0, D), (D, D + KV_W), (D + KV_W, D + 2 * KV_W), (D + 2 * KV_W, MIX1_IN)
TQ = 512

ADAM_LR = 0.001
ADAM_B1 = 0.9
ADAM_B2 = 0.999
ADAM_EPS = 1e-08
ADAM_WD = 0.01
ADAM_STEP = 10

N_DEV = 8
LANES = 128
MIB = 2 ** 20
MESH = pl.DeviceIdType.MESH


def _params(limit_mib, n_axes=1):
    return pltpu.CompilerParams(vmem_limit_bytes=limit_mib * MIB, dimension_semantics=("arbitrary",) * n_axes)


def _resident(shape):
    nd = len(shape)
    return pl.BlockSpec(shape, lambda *_: (0,) * nd, pipeline_mode=pl.Buffered(1))


def _gelu(x):
    k = 0.7978845608028654
    return 0.5 * x * (1.0 + jnp.tanh(k * (x + 0.044715 * x * x * x)))


def _gelu_and_grad(x):
    k = 0.7978845608028654
    x2 = x * x
    t = jnp.tanh(k * (x + 0.044715 * x * x2))
    g = 0.5 * x * (1.0 + t)
    dg = 0.5 * (1.0 + t) + 0.5 * x * (1.0 - t * t) * (k * (1.0 + 3.0 * 0.044715 * x2))
    return g, dg


def _silu_and_grad(x):
    s = jax.nn.sigmoid(x)
    return x * s, s * (1.0 + x * (1.0 - s))


def _nt(a, b):
    return lax.dot_general(a, b, (((1,), (1,)), ((), ())), preferred_element_type=F32)


def _tn(a, b):
    return lax.dot_general(a, b, (((0,), (0,)), ((), ())), preferred_element_type=F32)


def _mm(a, b):
    return jnp.dot(a, b, preferred_element_type=F32)


def _rope_tables_t(seq):
    inv = ROPE_THETA ** (-jnp.arange(0, 2 * ROT_HALF, 2, dtype=F32) / (2 * ROT_HALF))
    ang = inv[:, None] * jnp.arange(seq, dtype=F32)[None, :]
    return jnp.cos(ang), jnp.sin(ang)


def _rope_t(z, c, s, n_heads, sign):
    parts = []
    for h in range(n_heads):
        b = h * HD
        x1, x2 = z[b:b + ROT_HALF], z[b + ROT_HALF:b + 2 * ROT_HALF]
        if sign > 0:
            parts += [x1 * c - x2 * s, x2 * c + x1 * s]
        else:
            parts += [x1 * c + x2 * s, x2 * c - x1 * s]
        parts.append(z[b + 2 * ROT_HALF:b + HD])
    return jnp.concatenate(parts, axis=0)


def _l0_in_proj(x, g0, w):
    seq = x.shape[0]
    tm = 512

    def body(x_ref, g_ref, w_ref, za_ref, bx_ref, bg_ref, ht_ref):
        xf = x_ref[...]
        r = lax.rsqrt(jnp.mean(xf * xf, axis=1, keepdims=True) + EPS)
        h = (xf * r * g_ref[...]).astype(BF16)
        ht_ref[...] = h.T
        for j in range(3):
            za_ref[:, j * D:(j + 1) * D] = _mm(h, w_ref[:, j * D:(j + 1) * D]).astype(BF16)
        bx_ref[...] = _mm(h, w_ref[:, 3 * D:4 * D])
        bg_ref[...] = _mm(h, w_ref[:, 4 * D:5 * D]).astype(BF16)

    return pl.pallas_call(
        body, grid=(seq // tm,), name="l0_in_proj",
        out_shape=(jax.ShapeDtypeStruct((seq, 3 * D), BF16), jax.ShapeDtypeStruct((seq, D), F32),
                   jax.ShapeDtypeStruct((seq, D), BF16), jax.ShapeDtypeStruct((D, seq), BF16)),
        in_specs=[pl.BlockSpec((tm, D), lambda i: (i, 0)), _resident((1, D)), _resident((D, MIX0_IN))],
        out_specs=(pl.BlockSpec((tm, 3 * D), lambda i: (i, 0)), pl.BlockSpec((tm, D), lambda i: (i, 0)),
                   pl.BlockSpec((tm, D), lambda i: (i, 0)), pl.BlockSpec((D, tm), lambda i: (0, i))),
        compiler_params=_params(48),
    )(x, g0, w)


POOL_EXT = 40


def _fill_halo(ext_ref, cur, prev_ref, next_ref, i, n_tiles, ts):
    ext_ref[pl.ds(0, POOL_HALO), :] = jnp.where(i > 0, prev_ref[...], 0.0)
    ext_ref[pl.ds(POOL_HALO, ts), :] = cur
    ext_ref[pl.ds(POOL_HALO + ts, POOL_HALO), :] = jnp.where(i < n_tiles - 1, next_ref[...], 0.0)
    ext_ref[pl.ds(2 * POOL_HALO + ts, POOL_EXT - 2 * POOL_HALO), :] = jnp.zeros((POOL_EXT - 2 * POOL_HALO, D), F32)


def _window_sums(src_ref, tmp_refs, ts, cols, w, shift):
    if w == 2:
        return src_ref[pl.ds(POOL_HALO - 1 + shift, ts), cols] + src_ref[pl.ds(POOL_HALO + shift, ts), cols]
    d2, d4, d8 = tmp_refs
    n2, n4, n8 = ts + 32, ts + 24, ts + 16
    d2[pl.ds(0, n2), :] = src_ref[pl.ds(0, n2), cols] + src_ref[pl.ds(1, n2), cols]
    if w == 4:
        return d2[pl.ds(POOL_HALO - 2 + shift, ts), :] + d2[pl.ds(POOL_HALO + shift, ts), :]
    d4[pl.ds(0, n4), :] = d2[pl.ds(0, n4), :] + d2[pl.ds(2, n4), :]
    if w == 8:
        return d4[pl.ds(POOL_HALO - 4 + shift, ts), :] + d4[pl.ds(POOL_HALO + shift, ts), :]
    d8[pl.ds(0, n8), :] = d4[pl.ds(0, n8), :] + d4[pl.ds(4, n8), :]
    return d8[pl.ds(shift, ts), :] + d8[pl.ds(POOL_HALO + shift, ts), :]


def _pool_scratch(ts):
    return [pltpu.VMEM((ts + POOL_EXT, D), F32)] + [pltpu.VMEM((ts + POOL_EXT, GDIM), F32)] * 3


def _pool_forward(xe_ref, tmp_refs, ts, t0, seq):
    tg = t0 + lax.broadcasted_iota(jnp.int32, (ts, 1), 0)
    outs = []
    for gi, w in enumerate(POOL_WINDOWS):
        hw = w // 2
        cols = slice(gi * GDIM, (gi + 1) * GDIM)
        cnt = (jnp.minimum(tg + hw, seq) - jnp.maximum(tg - hw, 0)).astype(F32)
        outs.append(_window_sums(xe_ref, tmp_refs, ts, cols, w, 0) / cnt - xe_ref[pl.ds(POOL_HALO, ts), cols])
    return jnp.concatenate(outs, axis=1)


def _spatial_mix(ws_ref, vnb, bias, ts):
    rows = []
    for c in range(ts // CHUNK):
        vc = vnb[c * CHUNK:(c + 1) * CHUNK, :]
        rows.append(jnp.concatenate(
            [_mm(ws_ref[h], vc[:, h * GDIM:(h + 1) * GDIM]) for h in range(A_GROUPS)], axis=1) + bias)
    return jnp.concatenate(rows, axis=0)


def _halo_specs(ts, seq, width):
    per = ts // POOL_HALO
    last = seq // POOL_HALO - 1
    prev = pl.BlockSpec((POOL_HALO, width), lambda i: (jnp.maximum(i * per - 1, 0), 0))
    nxt = pl.BlockSpec((POOL_HALO, width), lambda i: (jnp.minimum((i + 1) * per, last), 0))
    return prev, nxt


def _l0_mix_fwd(za, bx, bg, x, ws, bias, gv, wg, scale, wout):
    seq = x.shape[0]
    ts = 512
    n_tiles = seq // ts

    def body(za_ref, bx_ref, bxp_ref, bxn_ref, bg_ref, x_ref, ws_ref, bias_ref, gv_ref, wg_ref, sc_ref, wo_ref,
             x1_ref, xe_ref, *tmp_refs):
        i = pl.program_id(0)
        u = _gelu(za_ref[:, 0:D].astype(F32))
        vg = _gelu(za_ref[:, D:2 * D].astype(F32))
        rv = lax.rsqrt(jnp.mean(vg * vg, axis=1, keepdims=True) + EPS)
        vnb = (vg * rv * gv_ref[...]).astype(BF16)
        mixed = _spatial_mix(ws_ref, vnb, bias_ref[...], ts)
        ag = za_ref[:, 2 * D:3 * D].astype(F32)
        ya = (u * mixed * (ag * jax.nn.sigmoid(ag))).astype(BF16)

        _fill_halo(xe_ref, bx_ref[...], bxp_ref, bxn_ref, i, n_tiles, ts)
        pb = _pool_forward(xe_ref, tmp_refs, ts, i * ts, seq).astype(BF16)
        y = jnp.concatenate([_mm(pb[:, g * GDIM:(g + 1) * GDIM], wg_ref[g]) for g in range(4)], axis=1) * sc_ref[...]
        bgf = bg_ref[...].astype(F32)
        yb = (y * (bgf * jax.nn.sigmoid(bgf))).astype(BF16)
        x1_ref[...] = x_ref[...] + _mm(ya, wo_ref[0:D, :]) + _mm(yb, wo_ref[D:2 * D, :])

    prev, nxt = _halo_specs(ts, seq, D)
    row = lambda w: pl.BlockSpec((ts, w), lambda i: (i, 0))
    return pl.pallas_call(
        body, grid=(n_tiles,), name="l0_mix_fwd",
        out_shape=jax.ShapeDtypeStruct((seq, D), F32),
        in_specs=[row(3 * D), row(D), prev, nxt, row(D), row(D), _resident((4, CHUNK, CHUNK)), _resident((CHUNK, D)),
                  _resident((1, D)), _resident((4, GDIM, GDIM)), _resident((1, D)), _resident((2 * D, D))],
        out_specs=row(D),
        scratch_shapes=_pool_scratch(ts),
        compiler_params=_params(56),
    )(za, bx, bx, bx, bg, x, ws, bias, gv, wg, scale, wout)


def _l1_in_proj(x1, g1, w_t, cos_t, sin_t):
    seq = x1.shape[0]
    tm = 512

    def body(x_ref, g_ref, wt_ref, c_ref, s_ref, q_ref, k_ref, v_ref, gate_ref, ht_ref):
        xf = x_ref[...]
        r = lax.rsqrt(jnp.mean(xf * xf, axis=1, keepdims=True) + EPS)
        ht = (xf * r * g_ref[...]).astype(BF16).T
        ht_ref[...] = ht
        c, s = c_ref[...], s_ref[...]
        q_ref[...] = (_rope_t(_mm(wt_ref[Q_ROWS[0]:Q_ROWS[1], :], ht), c, s, N_HEADS, 1) * SCALE).astype(BF16)
        k_ref[...] = _rope_t(_mm(wt_ref[K_ROWS[0]:K_ROWS[1], :], ht), c, s, N_KV, 1).astype(BF16)
        v_ref[...] = _mm(wt_ref[V_ROWS[0]:V_ROWS[1], :], ht).astype(BF16)
        gate_ref[...] = _mm(wt_ref[G_ROWS[0]:G_ROWS[1], :], ht).astype(BF16)

    col = lambda rows: pl.BlockSpec((rows, tm), lambda i: (0, i))
    return pl.pallas_call(
        body, grid=(seq // tm,), name="l1_in_proj",
        out_shape=(jax.ShapeDtypeStruct((D, seq), BF16), jax.ShapeDtypeStruct((KV_W, seq), BF16),
                   jax.ShapeDtypeStruct((KV_W, seq), BF16), jax.ShapeDtypeStruct((D, seq), BF16),
                   jax.ShapeDtypeStruct((D, seq), BF16)),
        in_specs=[pl.BlockSpec((tm, D), lambda i: (i, 0)), _resident((1, D)), _resident((MIX1_IN, D)), col(ROT_HALF),
                  col(ROT_HALF)],
        out_specs=(col(D), col(KV_W), col(KV_W), col(D), col(D)),
        compiler_params=_params(48),
    )(x1, g1, w_t, cos_t, sin_t)


def _band_specs_t(nb, clamp_i):
    per = TQ // BLK
    prev = pl.BlockSpec((KV_W, BLK), lambda i: (0, jnp.maximum(clamp_i(i) * per - 1, 0)))
    cur = pl.BlockSpec((KV_W, TQ), lambda i: (0, clamp_i(i)))
    nxt = pl.BlockSpec((KV_W, BLK), lambda i: (0, jnp.minimum((clamp_i(i) + 1) * per, nb - 1)))
    return [prev, cur, nxt]


def _fill_band(buf, p_ref, c_ref, n_ref):
    buf[:, 0:BLK] = p_ref[...]
    buf[:, BLK:BLK + TQ] = c_ref[...]
    buf[:, BLK + TQ:2 * BLK + TQ] = n_ref[...]


def _band_bias_t(n, nb):
    c = lax.broadcasted_iota(jnp.int32, (3 * BLK, BLK), 0)
    r = lax.broadcasted_iota(jnp.int32, (3 * BLK, BLK), 1)
    ok = (c >= r) & (c <= r + 2 * BLK) & ((c >= BLK) | (n > 0)) & ((c < 2 * BLK) | (n < nb - 1))
    bias = jnp.where(ok, 0.0, NEG_INF).astype(F32)
    return jnp.concatenate([bias] * GQA, axis=1)


def _heads_t(ref, kv, c0):
    return jnp.concatenate([ref[(kv * GQA + g) * HD:(kv * GQA + g + 1) * HD, c0:c0 + BLK] for g in range(GQA)], axis=1)


def _row4(ref, kv, c0):
    return jnp.concatenate([ref[kv * GQA + g:kv * GQA + g + 1, c0:c0 + BLK] for g in range(GQA)], axis=1)


def _sink_row(sink_ref, kv):
    return jnp.concatenate([jnp.full((1, BLK), sink_ref[kv * GQA + g], F32) for g in range(GQA)], axis=1)


def _l1_attn_fwd(qt, kt, vt, gatet, x1, tgt, wout, gf, sink):
    seq = x1.shape[0]
    nq, nb = seq // TQ, seq // BLK

    def body(q_ref, gate_ref, kp_ref, k_ref, kn_ref, vp_ref, v_ref, vn_ref, x1_ref, tgt_ref, wo_ref, gf_ref, sink_ref,
             dx2_ref, dx2b_ref, att_ref, lse_ref, loss_ref, dgf_ref, dwo_ref, dwo_wire_ref, kbuf, vbuf, att_scr):
        i = pl.program_id(0)

        @pl.when(i == 0)
        def _():
            loss_ref[...] = jnp.zeros_like(loss_ref)
            dgf_ref[...] = jnp.zeros_like(dgf_ref)
            dwo_ref[...] = jnp.zeros_like(dwo_ref)

        _fill_band(kbuf, kp_ref, k_ref, kn_ref)
        _fill_band(vbuf, vp_ref, v_ref, vn_ref)
        for j in range(TQ // BLK):
            c0 = j * BLK
            bias = _band_bias_t(i * (TQ // BLK) + j, nb)
            lse_rows = []
            for kv in range(N_KV):
                rows = slice(kv * HD, (kv + 1) * HD)
                q4 = _heads_t(q_ref, kv, c0)
                st = _tn(kbuf[rows, c0:c0 + 3 * BLK], q4) + bias
                sk = _sink_row(sink_ref, kv)
                m = jnp.maximum(jnp.max(st, axis=0, keepdims=True), sk)
                p = jnp.exp(st - m)
                den = jnp.sum(p, axis=0, keepdims=True) + jnp.exp(sk - m)
                ot = _mm(vbuf[rows, c0:c0 + 3 * BLK], p.astype(BF16)) / den
                lse = m + jnp.log(den)
                for g in range(GQA):
                    h = kv * GQA + g
                    att_scr[h * HD:(h + 1) * HD, c0:c0 + BLK] = ot[:, g * BLK:(g + 1) * BLK]
                    lse_rows.append(lse[:, g * BLK:(g + 1) * BLK])
            lse_ref[:, c0:c0 + BLK] = jnp.concatenate(lse_rows, axis=0)

        att = att_scr[...]
        gate = gate_ref[...].astype(F32)
        yt = (att * (gate * jax.nn.sigmoid(gate))).astype(BF16)
        att_ref[...] = att.astype(BF16)
        x2 = x1_ref[...] + _mm(yt.T, wo_ref[...])
        r = lax.rsqrt(jnp.mean(x2 * x2, axis=1, keepdims=True) + EPS)
        xn = x2 * r
        diff = xn * gf_ref[...] - tgt_ref[...]
        loss_ref[...] += 0.5 * jnp.sum(jnp.mean(diff * diff, axis=1, keepdims=True), axis=0, keepdims=True)
        dout = diff * (1.0 / D)
        dgf_ref[...] += jnp.sum(dout * xn, axis=0, keepdims=True)
        dxn = dout * gf_ref[...]
        dx2 = r * (dxn - xn * jnp.mean(dxn * xn, axis=1, keepdims=True))
        dx2_ref[...] = dx2
        dx2b = dx2.astype(BF16)
        dx2b_ref[...] = dx2b
        dwo_ref[...] += _mm(yt, dx2b)

        @pl.when(i == nq - 1)
        def _():
            dwo_wire_ref[...] = dwo_ref[...].astype(BF16)

    ident = lambda i: i
    row = pl.BlockSpec((TQ, D), lambda i: (i, 0))
    col = lambda rows: pl.BlockSpec((rows, TQ), lambda i: (0, i))
    whole = pl.BlockSpec((D, D), lambda i: (0, 0))
    return pl.pallas_call(
        body, grid=(nq,), name="l1_attn_fwd",
        out_shape=(jax.ShapeDtypeStruct((seq, D), F32), jax.ShapeDtypeStruct((seq, D), BF16),
                   jax.ShapeDtypeStruct((D, seq), BF16),
                   jax.ShapeDtypeStruct((N_HEADS, seq), F32), jax.ShapeDtypeStruct((1, 1), F32),
                   jax.ShapeDtypeStruct((1, D), F32), jax.ShapeDtypeStruct((D, D), F32), jax.ShapeDtypeStruct((D, D), BF16)),
        in_specs=[col(D), col(D)] + _band_specs_t(nb, ident) + _band_specs_t(nb, ident) + [
            row, row, _resident((D, D)), _resident((1, D)), pl.BlockSpec(memory_space=pltpu.SMEM)],
        out_specs=(row, row, col(D), col(N_HEADS), pl.BlockSpec((1, 1), lambda i: (0, 0)),
                   pl.BlockSpec((1, D), lambda i: (0, 0)), whole, whole),
        scratch_shapes=[pltpu.VMEM((KV_W, TQ + 2 * BLK), BF16), pltpu.VMEM((KV_W, TQ + 2 * BLK), BF16),
                        pltpu.VMEM((D, TQ), F32)],
        compiler_params=_params(56),
    )(qt, gatet, kt, kt, kt, vt, vt, vt, x1, tgt, wout, gf, sink)


def _l1_attn_bwd(dx2b, wout, qt, kt, vt, gatet, att, lse, sink):
    seq = dx2b.shape[0]
    nq, nb = seq // TQ, seq // BLK

    def body(dx_ref, wo_ref, q_ref, gate_ref, kp_ref, k_ref, kn_ref, vp_ref, v_ref, vn_ref, att_ref, lse_ref, sink_ref,
             dq_ref, dgate_ref, dk_ref, dv_ref, dsink_ref, kbuf, vbuf, dkacc, dvacc, dat_scr, delta_scr, dsacc):
        i = pl.program_id(0)

        @pl.when(i == 0)
        def _():
            dkacc[...] = jnp.zeros_like(dkacc)
            dvacc[...] = jnp.zeros_like(dvacc)
            dsacc[...] = jnp.zeros_like(dsacc)

        @pl.when(i > 0)
        def _():
            for acc in (dkacc, dvacc):
                acc[:, 0:2 * BLK] = acc[:, TQ:TQ + 2 * BLK]
                acc[:, 2 * BLK:2 * BLK + TQ] = jnp.zeros((KV_W, TQ), F32)

        @pl.when(i < nq)
        def _():
            _fill_band(kbuf, kp_ref, k_ref, kn_ref)
            _fill_band(vbuf, vp_ref, v_ref, vn_ref)
            dyt = _nt(wo_ref[...], dx_ref[...])
            sg, dsg = _silu_and_grad(gate_ref[...].astype(F32))
            attf = att_ref[...].astype(F32)
            dat = dyt * sg
            dat_scr[...] = dat.astype(BF16)
            dgate_ref[...] = (dyt * attf * dsg).astype(BF16)
            dl = dat * attf
            delta_scr[...] = jnp.concatenate(
                [jnp.sum(dl[h * HD:(h + 1) * HD, :], axis=0, keepdims=True) for h in range(N_HEADS)], axis=0)
            for j in range(TQ // BLK):
                c0 = j * BLK
                bias = _band_bias_t(i * (TQ // BLK) + j, nb)
                for kv in range(N_KV):
                    rows = slice(kv * HD, (kv + 1) * HD)
                    q4 = _heads_t(q_ref, kv, c0)
                    do4 = _heads_t(dat_scr, kv, c0)
                    lse4 = _row4(lse_ref, kv, c0)
                    delta4 = _row4(delta_scr, kv, c0)
                    kth = kbuf[rows, c0:c0 + 3 * BLK]
                    vth = vbuf[rows, c0:c0 + 3 * BLK]
                    p = jnp.exp(_tn(kth, q4) + bias - lse4)
                    dp = _tn(vth, do4)
                    ds = (p * (dp - delta4)).astype(BF16)
                    dq4 = _mm(kth, ds) * SCALE
                    dkacc[rows, c0:c0 + 3 * BLK] += _nt(q4, ds)
                    dvacc[rows, c0:c0 + 3 * BLK] += _nt(do4, p.astype(BF16))
                    dsk = -jnp.exp(_sink_row(sink_ref, kv) - lse4) * delta4
                    for g in range(GQA):
                        h = kv * GQA + g
                        dq_ref[h * HD:(h + 1) * HD, c0:c0 + BLK] = dq4[:, g * BLK:(g + 1) * BLK].astype(BF16)
                        dsacc[h:h + 1, :] += dsk[:, g * BLK:(g + 1) * BLK]

        dk_ref[...] = dkacc[:, 0:TQ].astype(BF16)
        dv_ref[...] = dvacc[:, 0:TQ].astype(BF16)

        @pl.when(i == nq)
        def _():
            dsink_ref[...] = jnp.broadcast_to(jnp.sum(dsacc[...], axis=1, keepdims=True), (N_HEADS, LANES))

    clamp = lambda i: jnp.minimum(i, nq - 1)
    row = pl.BlockSpec((TQ, D), lambda i: (clamp(i), 0))
    col = lambda rows: pl.BlockSpec((rows, TQ), lambda i: (0, clamp(i)))
    pad = pl.BlockSpec((KV_W, TQ), lambda i: (0, i))
    return pl.pallas_call(
        body, grid=(nq + 1,), name="l1_attn_bwd",
        out_shape=(jax.ShapeDtypeStruct((D, seq), BF16), jax.ShapeDtypeStruct((D, seq), BF16),
                   jax.ShapeDtypeStruct((KV_W, seq + TQ), BF16), jax.ShapeDtypeStruct((KV_W, seq + TQ), BF16),
                   jax.ShapeDtypeStruct((N_HEADS, LANES), F32)),
        in_specs=[row, _resident((D, D)), col(D), col(D)] + _band_specs_t(nb, clamp) + _band_specs_t(nb, clamp) + [
            col(D), col(N_HEADS), pl.BlockSpec(memory_space=pltpu.SMEM)],
        out_specs=(col(D), col(D), pad, pad, pl.BlockSpec((N_HEADS, LANES), lambda i: (0, 0))),
        scratch_shapes=[pltpu.VMEM((KV_W, TQ + 2 * BLK), BF16), pltpu.VMEM((KV_W, TQ + 2 * BLK), BF16),
                        pltpu.VMEM((KV_W, TQ + 2 * BLK), F32), pltpu.VMEM((KV_W, TQ + 2 * BLK), F32),
                        pltpu.VMEM((D, TQ), BF16), pltpu.VMEM((N_HEADS, TQ), F32), pltpu.VMEM((N_HEADS, LANES), F32)],
        compiler_params=_params(56),
    )(dx2b, wout, qt, gatet, kt, kt, kt, vt, vt, vt, att, lse, sink)


def _l1_in_proj_bwd(dq_r, dk_r, dv, dgate, cos_t, sin_t, w_t, x1, g1, dx2):
    seq = x1.shape[0]
    tm = 512

    def body(dq_ref, dk_ref, dv_ref, dg_ref, c_ref, s_ref, w_ref, x_ref, g_ref, dres_ref,
             dx_ref, dxb_ref, dz_ref, dn_ref):
        @pl.when(pl.program_id(0) == 0)
        def _():
            dn_ref[...] = jnp.zeros_like(dn_ref)

        c, s = c_ref[...], s_ref[...]
        dq = _rope_t(dq_ref[...].astype(F32), c, s, N_HEADS, -1).astype(BF16)
        dk = _rope_t(dk_ref[...].astype(F32), c, s, N_KV, -1).astype(BF16)
        dz = jnp.concatenate([dq, dk, dv_ref[...], dg_ref[...]], axis=0)
        dz_ref[...] = dz
        dh = _tn(dz, w_ref[...])
        xf = x_ref[...]
        r = lax.rsqrt(jnp.mean(xf * xf, axis=1, keepdims=True) + EPS)
        xn = xf * r
        dn_ref[...] += jnp.sum(dh * xn, axis=0, keepdims=True)
        dxn = dh * g_ref[...]
        dx = dres_ref[...] + r * (dxn - xn * jnp.mean(dxn * xn, axis=1, keepdims=True))
        dx_ref[...] = dx
        dxb_ref[...] = dx.astype(BF16)

    row = pl.BlockSpec((tm, D), lambda i: (i, 0))
    col = lambda rows: pl.BlockSpec((rows, tm), lambda i: (0, i))
    return pl.pallas_call(
        body, grid=(seq // tm,), name="l1_in_proj_bwd",
        out_shape=(jax.ShapeDtypeStruct((seq, D), F32), jax.ShapeDtypeStruct((seq, D), BF16),
                   jax.ShapeDtypeStruct((MIX1_IN, seq), BF16), jax.ShapeDtypeStruct((1, D), F32)),
        in_specs=[col(D), col(KV_W), col(KV_W), col(D), col(ROT_HALF), col(ROT_HALF), _resident((MIX1_IN, D)), row,
                  _resident((1, D)), row],
        out_specs=(row, row, col(MIX1_IN), pl.BlockSpec((1, D), lambda i: (0, 0))),
        compiler_params=_params(48),
    )(dq_r, dk_r, dv, dgate, cos_t, sin_t, w_t, x1, g1, dx2)


def _l0_mix_bwd(dx1b, wout, za, bx, bg, ws, ws_t, bias, gv, wg, wg_t, scale):
    seq = dx1b.shape[0]
    ts = 256
    n_tiles = seq // ts

    def body(dx_ref, wo_ref, za_ref, bx_ref, bxp_ref, bxn_ref, bg_ref, ws_ref, wst_ref, bias_ref, gv_ref, wg_ref,
             wgt_ref, sc_ref,
             dz_ref, dp_ref, catt_ref, dws_ref, dbias_ref, dgv_ref, dsc_ref, dwg_ref, db_ref, xe_ref, *tmp_refs):
        i = pl.program_id(0)

        @pl.when(i == 0)
        def _():
            for r_ in (dws_ref, dbias_ref, dgv_ref, dsc_ref, dwg_ref, db_ref):
                r_[...] = jnp.zeros_like(r_)

        dxb = dx_ref[...]
        dya = _nt(dxb, wo_ref[0:D, :])
        dyb = _nt(dxb, wo_ref[D:2 * D, :])

        u, du = _gelu_and_grad(za_ref[:, 0:D].astype(F32))
        vg, dvg_dz = _gelu_and_grad(za_ref[:, D:2 * D].astype(F32))
        rv = lax.rsqrt(jnp.mean(vg * vg, axis=1, keepdims=True) + EPS)
        vnorm = vg * rv
        gvw = gv_ref[...]
        vnb = (vnorm * gvw).astype(BF16)
        mixed = _spatial_mix(ws_ref, vnb, bias_ref[...], ts)
        sga, dsga = _silu_and_grad(za_ref[:, 2 * D:3 * D].astype(F32))
        um = u * mixed
        ya = (um * sga).astype(BF16)
        t = dya * sga
        dz_ref[:, 0:D] = (t * mixed * du).astype(BF16)
        dz_ref[:, 2 * D:3 * D] = (dya * um * dsga).astype(BF16)
        dmixed = t * u
        dmb = dmixed.astype(BF16)
        dvn_rows = []
        dbias = jnp.zeros((CHUNK, D), F32)
        for c in range(ts // CHUNK):
            rows = slice(c * CHUNK, (c + 1) * CHUNK)
            dbias = dbias + dmixed[rows, :]
            parts = []
            for h in range(A_GROUPS):
                cols = slice(h * GDIM, (h + 1) * GDIM)
                dws_ref[h] += _nt(dmb[rows, cols], vnb[rows, cols])
                parts.append(_mm(wst_ref[h], dmb[rows, cols]))
            dvn_rows.append(jnp.concatenate(parts, axis=1))
        dbias_ref[...] += dbias
        dvn = jnp.concatenate(dvn_rows, axis=0)
        dgv_ref[...] += jnp.sum(dvn * vnorm, axis=0, keepdims=True)
        dxn = dvn * gvw
        dvg = rv * (dxn - vnorm * jnp.mean(dxn * vnorm, axis=1, keepdims=True))
        dz_ref[:, D:2 * D] = (dvg * dvg_dz).astype(BF16)

        _fill_halo(xe_ref, bx_ref[...], bxp_ref, bxn_ref, i, n_tiles, ts)
        pb = _pool_forward(xe_ref, tmp_refs, ts, i * ts, seq).astype(BF16)
        ypre = jnp.concatenate([_mm(pb[:, g * GDIM:(g + 1) * GDIM], wg_ref[g]) for g in range(4)], axis=1)
        sc = sc_ref[...]
        y = ypre * sc
        sgb, dsgb = _silu_and_grad(bg_ref[...].astype(F32))
        yb = (y * sgb).astype(BF16)
        dy_b = dyb * sgb
        dz_ref[:, 3 * D:4 * D] = jnp.zeros((ts, D), BF16)
        dz_ref[:, 4 * D:5 * D] = (dyb * y * dsgb).astype(BF16)
        dsc_ref[...] += jnp.sum(dy_b * ypre, axis=0, keepdims=True)
        dypre = (dy_b * sc).astype(BF16)
        dps = []
        for g in range(4):
            cols = slice(g * GDIM, (g + 1) * GDIM)
            dwg_ref[g] += _tn(pb[:, cols], dypre[:, cols])
            dps.append(_mm(dypre[:, cols], wgt_ref[g]))
        dp_ref[...] = jnp.concatenate(dps, axis=1)
        catt_ref[...] = jnp.concatenate([ya, yb], axis=1).T

        @pl.when(i == n_tiles - 1)
        def _():
            for h in range(A_GROUPS):
                tot = jnp.sum(dbias_ref[:, h * GDIM:(h + 1) * GDIM].T, axis=0, keepdims=True)
                db_ref[pl.ds(h * 8, 8), :] = jnp.broadcast_to(tot, (8, CHUNK))

    prev, nxt = _halo_specs(ts, seq, D)
    row = lambda w_: pl.BlockSpec((ts, w_), lambda i: (i, 0))
    acc = lambda shape: pl.BlockSpec(shape, lambda i: (0,) * len(shape))
    return pl.pallas_call(
        body, grid=(n_tiles,), name="l0_mix_bwd",
        out_shape=(jax.ShapeDtypeStruct((seq, MIX0_IN), BF16), jax.ShapeDtypeStruct((seq, D), F32),
                   jax.ShapeDtypeStruct((2 * D, seq), BF16),
                   jax.ShapeDtypeStruct((4, CHUNK, CHUNK), F32), jax.ShapeDtypeStruct((CHUNK, D), F32),
                   jax.ShapeDtypeStruct((1, D), F32), jax.ShapeDtypeStruct((1, D), F32),
                   jax.ShapeDtypeStruct((4, GDIM, GDIM), F32), jax.ShapeDtypeStruct((32, CHUNK), F32)),
        in_specs=[row(D), _resident((2 * D, D)), row(3 * D), row(D), prev, nxt, row(D), _resident((4, CHUNK, CHUNK)),
                  _resident((4, CHUNK, CHUNK)), _resident((CHUNK, D)), _resident((1, D)), _resident((4, GDIM, GDIM)),
                  _resident((4, GDIM, GDIM)), _resident((1, D))],
        out_specs=(row(MIX0_IN), row(D), pl.BlockSpec((2 * D, ts), lambda i: (0, i)),
                   acc((4, CHUNK, CHUNK)), acc((CHUNK, D)), acc((1, D)), acc((1, D)), acc((4, GDIM, GDIM)),
                   acc((32, CHUNK))),
        scratch_shapes=_pool_scratch(ts),
        compiler_params=_params(56),
    )(dx1b, wout, za, bx, bx, bx, bg, ws, ws_t, bias, gv, wg, wg_t, scale)


def _l0_pool_bwd(dp, dz):
    seq = dp.shape[0]
    ts = 512
    n_tiles = seq // ts
    ext = ts + 2 * POOL_HALO

    def body(dp_ref, dpp_ref, dpn_ref, dz_ref, out_ref, qe_ref, *tmp_refs):
        i = pl.program_id(0)
        _fill_halo(qe_ref, dp_ref[...], dpp_ref, dpn_ref, i, n_tiles, ts)
        te = i * ts - POOL_HALO + lax.broadcasted_iota(jnp.int32, (ext, 1), 0)
        for gi, w in enumerate(POOL_WINDOWS):
            hw = w // 2
            cols = slice(gi * GDIM, (gi + 1) * GDIM)
            cnt = jnp.maximum(jnp.minimum(te + hw, seq) - jnp.maximum(te - hw, 0), 1).astype(F32)
            qe_ref[pl.ds(0, ext), cols] = qe_ref[pl.ds(0, ext), cols] / cnt
        outs = []
        for gi, w in enumerate(POOL_WINDOWS):
            cols = slice(gi * GDIM, (gi + 1) * GDIM)
            outs.append(_window_sums(qe_ref, tmp_refs, ts, cols, w, 1) - dp_ref[:, cols])
        out_ref[...] = jnp.concatenate(outs, axis=1).astype(BF16)

    prev, nxt = _halo_specs(ts, seq, D)
    row = pl.BlockSpec((ts, D), lambda i: (i, 0))
    return pl.pallas_call(
        body, grid=(n_tiles,), name="l0_pool_bwd",
        out_shape=jax.ShapeDtypeStruct(dz.shape, BF16),
        in_specs=[row, prev, nxt, pl.BlockSpec(memory_space=pl.ANY)],
        out_specs=pl.BlockSpec((ts, D), lambda i: (i, 3)),
        input_output_aliases={3: 0},
        scratch_shapes=_pool_scratch(ts),
        compiler_params=_params(32),
    )(dp, dp, dp, dz)


def _l0_in_proj_bwd(dz, w, x, g0, dx1):
    seq = x.shape[0]
    tm = 512

    def body(dz_ref, w_ref, x_ref, g_ref, dres_ref, dx_ref, dn_ref):
        @pl.when(pl.program_id(0) == 0)
        def _():
            dn_ref[...] = jnp.zeros_like(dn_ref)

        dh = _nt(dz_ref[...], w_ref[...])
        xf = x_ref[...]
        r = lax.rsqrt(jnp.mean(xf * xf, axis=1, keepdims=True) + EPS)
        xn = xf * r
        dn_ref[...] += jnp.sum(dh * xn, axis=0, keepdims=True)
        dxn = dh * g_ref[...]
        dx_ref[...] = dres_ref[...] + r * (dxn - xn * jnp.mean(dxn * xn, axis=1, keepdims=True))

    row = lambda w_: pl.BlockSpec((tm, w_), lambda i: (i, 0))
    return pl.pallas_call(
        body, grid=(seq // tm,), name="l0_in_proj_bwd",
        out_shape=(jax.ShapeDtypeStruct((seq, D), F32), jax.ShapeDtypeStruct((1, D), F32)),
        in_specs=[row(MIX0_IN), _resident((D, MIX0_IN)), row(D), _resident((1, D)), row(D)],
        out_specs=(row(D), pl.BlockSpec((1, D), lambda i: (0, 0))),
        compiler_params=_params(56),
    )(dz, w, x, g0, dx1)


def _dw_matmul(a_t, b, name, b_transposed=False, tn=1024, ts=1024, col_block=None):
    k, seq = a_t.shape
    n = b.shape[0] if b_transposed else b.shape[1]
    tn = min(n, tn)
    assert seq % ts == 0 and n % tn == 0 and (col_block is None or tn % col_block == 0)
    n_s = seq // ts
    per = 1 if col_block is None else tn // col_block

    def body(a_ref, b_ref, o_ref, ob_ref, acc_ref):
        s = pl.program_id(1)

        @pl.when(s == 0)
        def _():
            acc_ref[...] = jnp.zeros_like(acc_ref)

        acc_ref[...] += _nt(a_ref[...], b_ref[...]) if b_transposed else _mm(a_ref[...], b_ref[...])

        @pl.when(s == n_s - 1)
        def _():
            acc = acc_ref[...]
            if col_block is None:
                o_ref[...] = acc
                ob_ref[...] = acc.astype(BF16)
            else:
                for i in range(per):
                    piece = acc[:, i * col_block:(i + 1) * col_block]
                    o_ref[i] = piece
                    ob_ref[i] = piece.astype(BF16)

    b_spec = (pl.BlockSpec((tn, ts), lambda j, s: (j, s)) if b_transposed else pl.BlockSpec((ts, tn), lambda j, s: (s, j)))
    if col_block is None:
        shape, o_spec = (k, n), pl.BlockSpec((k, tn), lambda j, s: (0, j))
    else:
        shape, o_spec = (n // col_block, k, col_block), pl.BlockSpec((per, k, col_block), lambda j, s: (j, 0, 0))
    return pl.pallas_call(
        body, grid=(n // tn, n_s), name=name,
        out_shape=(jax.ShapeDtypeStruct(shape, F32), jax.ShapeDtypeStruct(shape, BF16)),
        in_specs=[pl.BlockSpec((k, ts), lambda j, s: (0, s)), b_spec],
        out_specs=(o_spec, o_spec),
        scratch_shapes=[pltpu.VMEM((k, tn), F32)],
        compiler_params=_params(56, 2),
    )(a_t, b)


ROW_TILES = 8


def _cast_shards(shards):
    n = len(shards)

    def body(*refs):
        for a in range(n):
            refs[n + a][...] = refs[a][...].astype(BF16)

    vm = pl.BlockSpec(memory_space=pltpu.VMEM)
    return pl.pallas_call(body, name="cast_weights", out_shape=[jax.ShapeDtypeStruct(t.shape, BF16) for t in shards],
                          in_specs=[vm] * n, out_specs=[vm] * n, compiler_params=_params(32, 0))(*shards)


def _adamw_math(w, g, m, v):
    m2 = ADAM_B1 * m + (1.0 - ADAM_B1) * g
    v2 = ADAM_B2 * v + (1.0 - ADAM_B2) * (g * g)
    m_hat = m2 / (1.0 - ADAM_B1 ** ADAM_STEP)
    v_hat = v2 / (1.0 - ADAM_B2 ** ADAM_STEP)
    delta = -ADAM_LR * (m_hat / (jnp.sqrt(v_hat) + ADAM_EPS) + ADAM_WD * w)
    return delta, m2, v2


def _final_sum_adamw(g_list, recv_list, me, w_list, m_list, v_list):
    n = len(w_list)

    def body(me_ref, *refs):
        own, recv, w, m, v = (refs[k * n:(k + 1) * n] for k in range(5))
        outs = [refs[(5 + k) * n:(6 + k) * n] for k in range(4)]
        for a in range(n):
            g = own[a][...]
            for k in range(N_DEV - 1):
                g = g + recv[a][k].astype(F32)
            delta, m2, v2 = _adamw_math(w[a][...], g, m[a][...], v[a][...])
            for o_ref, val in zip((outs[0][a], outs[1][a], outs[2][a], outs[3][a]), (g, delta, m2, v2)):
                o_ref[...] = val

    own_specs, flat, wire, shapes = [], [], [], []
    for t in w_list:
        rows, width = t.shape
        tr = rows // ROW_TILES
        own_specs.append(pl.BlockSpec((None, tr, width), lambda i, me: (me[0], i, 0)))
        flat.append(pl.BlockSpec((tr, width), lambda i, me: (i, 0)))
        wire.append(pl.BlockSpec((N_DEV - 1, tr, width), lambda i, me: (0, i, 0)))
        shapes.append(jax.ShapeDtypeStruct((rows, width), F32))
    out = pl.pallas_call(
        body, name="grad_sum_adamw", out_shape=shapes * 4,
        grid_spec=pltpu.PrefetchScalarGridSpec(
            num_scalar_prefetch=1, grid=(ROW_TILES,), in_specs=own_specs + wire + flat * 3, out_specs=flat * 4),
        compiler_params=_params(40),
    )(me, *g_list, *recv_list, *w_list, *m_list, *v_list)
    return [out[k * n:(k + 1) * n] for k in range(4)]


SMALL_NAMES = ("norm_0", "a_v_norm_0", "b_scale_0", "norm_1", "final_norm", "a_spatial_w_0", "a_spatial_b_0", "sink_1")
SMALL_VIEWS = ((8, LANES),) * 5 + ((4 * CHUNK, LANES), (4, LANES), (1, N_HEADS))
SMALL_ROW0 = (0, 8, 16, 24, 32, 40, 552, 560)
SMALL_ROWS = 568


def _small_sum_adamw(early, late, w_list, m_list, v_list):
    n = len(w_list)

    def body(e_ref, l_ref, *refs):
        gtot, first = e_ref[0], l_ref[0]
        for d in range(1, N_DEV):
            gtot = gtot + e_ref[d]
            first = first + l_ref[d]
        for a, ((rows, width), r0) in enumerate(zip(SMALL_VIEWS, SMALL_ROW0)):
            g = first if SMALL_NAMES[a] == "norm_0" else gtot[r0:r0 + rows, 0:width]
            delta, m2, v2 = _adamw_math(refs[a][...], g, refs[n + a][...], refs[2 * n + a][...])
            for k, val in enumerate((g, delta, m2, v2)):
                refs[(3 + k) * n + a][...] = val
        refs[7 * n][...] = gtot[LOSS_ROW:LOSS_ROW + 1, LOSS_LANE:LOSS_LANE + 1]

    vm = pl.BlockSpec(memory_space=pltpu.VMEM)
    shapes = [jax.ShapeDtypeStruct(s, F32) for s in SMALL_VIEWS]
    out = pl.pallas_call(
        body, name="small_sum_adamw", out_shape=shapes * 4 + [jax.ShapeDtypeStruct((1, 1), F32)],
        in_specs=[vm, vm] + [vm] * (3 * n), out_specs=[vm] * (4 * n + 1),
    )(early, late, *w_list, *m_list, *v_list)
    return [out[k * n:(k + 1) * n] for k in range(4)], out[4 * n]


def _all_gather_columns(blks):
    n = len(blks)

    def body(*refs):
        ins, outs = refs[:n], refs[n:2 * n]
        send_sems, recv_sems, local_sems = refs[2 * n:]
        x, y, c = lax.axis_index("x"), lax.axis_index("y"), lax.axis_index("c")
        me, sibling = (x, y, c), (x, y, 1 - c)
        chips = [(1 - x, y), (x, 1 - y), (1 - x, 1 - y)]

        def slot(a, px, py, pc):
            width = blks[a].shape[1]
            return outs[a].at[:, pl.ds(pl.multiple_of((4 * px + 2 * py + pc) * width, LANES), width)]

        def copy(k, a, block, to, from_input=False):
            return pltpu.make_async_remote_copy(
                src_ref=ins[a] if from_input else slot(a, *block), dst_ref=slot(a, *block),
                send_sem=send_sems.at[k, a], recv_sem=recv_sems.at[k, a], device_id=to, device_id_type=MESH)

        mine = [pltpu.make_async_copy(ins[a], slot(a, *me), local_sems.at[a]) for a in range(n)]
        first = []
        for a in range(n):
            first.append(copy(0, a, me, sibling, from_input=True))
            first += [copy(1 + j, a, me, (*chip, c), from_input=True) for j, chip in enumerate(chips)]
        for cp in mine + first:
            cp.start()
        passed = []
        for j, chip in enumerate(chips):
            for a in range(n):
                copy(1 + j, a, (*chip, c), me).wait_recv()
                passed.append(copy(4 + j, a, (*chip, c), sibling))
                passed[-1].start()
        for a in range(n):
            copy(0, a, sibling, me).wait_recv()
        for j, chip in enumerate(chips):
            for a in range(n):
                copy(4 + j, a, (*chip, 1 - c), me).wait_recv()
        for cp in first + passed:
            cp.wait_send()
        for cp in mine:
            cp.wait()

    any_spec = pl.BlockSpec(memory_space=pl.ANY)
    return pl.pallas_call(
        body, name="weights_all_gather",
        out_shape=[jax.ShapeDtypeStruct((t.shape[0], N_DEV * t.shape[1]), t.dtype) for t in blks],
        in_specs=[any_spec] * n, out_specs=[any_spec] * n,
        scratch_shapes=[pltpu.SemaphoreType.DMA((7, n)), pltpu.SemaphoreType.DMA((7, n)), pltpu.SemaphoreType.DMA((n,))],
    )(*blks)


PEER_FLIPS = tuple((fx, fy, fc) for fx in (0, 1) for fy in (0, 1) for fc in (0, 1))[1:]


def _sequencer_all_gather(blks, name, collective_id, concat_rows=False):
    n = len(blks)

    def body(*refs):
        ins, outs = refs[:n], refs[n:2 * n]
        send_sems, recv_sems, local_sems = refs[2 * n:]
        x, y, c = lax.axis_index("x"), lax.axis_index("y"), lax.axis_index("c")
        peers = [(x ^ fx, y ^ fy, c ^ fc) for fx, fy, fc in PEER_FLIPS]
        barrier = pltpu.get_barrier_semaphore()
        for peer in peers:
            pl.semaphore_signal(barrier, inc=1, device_id=peer, device_id_type=MESH)
        pl.semaphore_wait(barrier, len(peers))
        me = 4 * x + 2 * y + c

        def slot(a):
            rows = blks[a].shape[0]
            return outs[a].at[pl.ds(pl.multiple_of(me * rows, 16), rows)] if concat_rows else outs[a].at[me]

        copies = [pltpu.make_async_remote_copy(
            src_ref=ins[a], dst_ref=slot(a), send_sem=send_sems.at[k, a], recv_sem=recv_sems.at[k, a],
            device_id=peer, device_id_type=MESH) for k, peer in enumerate(peers) for a in range(n)]
        mine = [pltpu.make_async_copy(ins[a], slot(a), local_sems.at[a]) for a in range(n)]
        for cp in copies + mine:
            cp.start()
        for cp in copies + mine:
            cp.wait()

    out_shape = (lambda t: (N_DEV * t.shape[0],) + t.shape[1:]) if concat_rows else (lambda t: (N_DEV,) + t.shape)
    return pl.kernel(
        body, out_type=[jax.ShapeDtypeStruct(out_shape(t), t.dtype) for t in blks],
        mesh=plsc.ScalarSubcoreMesh(axis_name="sequencer", num_cores=1), name=name,
        scratch_types=[pltpu.SemaphoreType.DMA((7, n)), pltpu.SemaphoreType.DMA((7, n)), pltpu.SemaphoreType.DMA((n,))],
        compiler_params=pltpu.CompilerParams(collective_id=collective_id),
    )(*blks)


def _sequencer_scatter(g_list, name, collective_id):
    n = len(g_list)

    def body(*refs):
        ins, outs = refs[:n], refs[n:2 * n]
        send_sems, recv_sems = refs[2 * n:]
        x, y, c = lax.axis_index("x"), lax.axis_index("y"), lax.axis_index("c")
        peers = [(x ^ fx, y ^ fy, c ^ fc) for fx, fy, fc in PEER_FLIPS]
        barrier = pltpu.get_barrier_semaphore()
        for peer in peers:
            pl.semaphore_signal(barrier, inc=1, device_id=peer, device_id_type=MESH)
        pl.semaphore_wait(barrier, len(peers))
        copies = [pltpu.make_async_remote_copy(
            src_ref=ins[a].at[4 * px + 2 * py + pc], dst_ref=outs[a].at[k], send_sem=send_sems.at[k, a],
            recv_sem=recv_sems.at[k, a], device_id=(px, py, pc), device_id_type=MESH)
            for k, (px, py, pc) in enumerate(peers) for a in range(n)]
        for cp in copies:
            cp.start()
        for cp in copies:
            cp.wait()

    return pl.kernel(
        body, out_type=[jax.ShapeDtypeStruct((N_DEV - 1,) + g.shape[1:], g.dtype) for g in g_list],
        mesh=plsc.ScalarSubcoreMesh(axis_name="sequencer", num_cores=1), name=name,
        scratch_types=[pltpu.SemaphoreType.DMA((7, n)), pltpu.SemaphoreType.DMA((7, n))],
        compiler_params=pltpu.CompilerParams(collective_id=collective_id),
    )(*g_list)


def _direct_all_gather(blk, name):
    def body(g_ref, out_ref, send_sems, recv_sems, local_sem):
        x, y, c = lax.axis_index("x"), lax.axis_index("y"), lax.axis_index("c")
        me = 4 * x + 2 * y + c
        copies = [pltpu.make_async_remote_copy(
            src_ref=g_ref, dst_ref=out_ref.at[me], send_sem=send_sems.at[k], recv_sem=recv_sems.at[k],
            device_id=(x ^ fx, y ^ fy, c ^ fc), device_id_type=MESH) for k, (fx, fy, fc) in enumerate(PEER_FLIPS)]
        copies.append(pltpu.make_async_copy(g_ref, out_ref.at[me], local_sem))
        for cp in copies:
            cp.start()
        for cp in copies:
            cp.wait()

    any_spec = pl.BlockSpec(memory_space=pl.ANY)
    return pl.pallas_call(
        body, name=name, out_shape=jax.ShapeDtypeStruct((N_DEV,) + blk.shape, blk.dtype),
        in_specs=[any_spec], out_specs=any_spec,
        scratch_shapes=[pltpu.SemaphoreType.DMA((7,)), pltpu.SemaphoreType.DMA((7,)), pltpu.SemaphoreType.DMA],
    )(blk)


def _shard_views(w_in_0, b_group_w_0, w_out_0, w_in_1, w_out_1):
    return [w_in_0, b_group_w_0.reshape(4 * 32, GDIM), w_out_0, w_in_1, w_out_1]


def _small_views(named):
    return [named[name].reshape(view) for name, view in zip(SMALL_NAMES, SMALL_VIEWS)]


LOSS_ROW, LOSS_LANE = 560, N_HEADS


def _pack_small_grads(named, loss_part):
    rows = []
    for name, (r, w) in zip(SMALL_NAMES, SMALL_VIEWS):
        pad_r = -r % 8
        if name == "sink_1":
            t = jnp.concatenate([named[name].reshape(r, w), loss_part], axis=1)
            rows.append(jnp.pad(t, ((0, pad_r), (0, LANES - w - 1))))
        elif name in named:
            rows.append(jnp.pad(named[name].reshape(r, w), ((0, pad_r), (0, LANES - w))))
        else:
            rows.append(jnp.zeros((r + pad_r, LANES), F32))
    return jnp.concatenate(rows, axis=0)


def _device_blocks(t, axis):
    shape = t.shape
    t = t.reshape(shape[:axis] + (N_DEV, shape[axis] // N_DEV) + shape[axis + 1:])
    t = jnp.moveaxis(t, axis, 0)
    return t.reshape(N_DEV, -1, shape[-1] if axis != len(shape) - 1 else shape[-1] // N_DEV)


def kernel(x, norm_0, w_in_0, a_v_norm_0, a_spatial_w_0, a_spatial_b_0, b_group_w_0, b_scale_0, w_out_0, norm_1, w_in_1, sink_1, w_out_1, final_norm, loss_target, m_norm_0, m_w_in_0, m_a_v_norm_0, m_a_spatial_w_0, m_a_spatial_b_0, m_b_group_w_0, m_b_scale_0, m_w_out_0, m_norm_1, m_w_in_1, m_sink_1, m_w_out_1, m_final_norm, v_norm_0, v_w_in_0, v_a_v_norm_0, v_a_spatial_w_0, v_a_spatial_b_0, v_b_group_w_0, v_b_scale_0, v_w_out_0, v_norm_1, v_w_in_1, v_sink_1, v_w_out_1, v_final_norm):
    seq = x.shape[1]
    xs = x.reshape(seq, D)
    tgt = loss_target.reshape(seq, D)
    ax, ay, ac = lax.axis_index("x"), lax.axis_index("y"), lax.axis_index("c")
    me = jnp.reshape(4 * ax + 2 * ay + ac, (1,)).astype(jnp.int32)

    shards = _shard_views(w_in_0, b_group_w_0, w_out_0, w_in_1, w_out_1)
    cast = _cast_shards([shards[0], shards[1], shards[2], w_in_1.T, shards[4]])
    win0 = _all_gather_columns(cast[0:1])[0]
    rest, win0 = lax.optimization_barrier((cast[1:5], win0))
    g_wg, wout0 = _sequencer_all_gather(rest[0:2], "weights_gather_a", 1, concat_rows=True)
    win1_t, wout1 = _sequencer_all_gather(rest[2:4], "weights_gather_b", 2, concat_rows=True)

    blocks, received, early = {}, {}, {}
    collective_ids = {"l1": 3, "out0": 4, "in0": 5}

    def scatter(tag, own_blocks, wire_blocks):
        blocks[tag] = own_blocks
        received[tag] = _sequencer_scatter(wire_blocks, "grad_scatter_" + tag, collective_ids[tag])

    def small_early(named, loss_part):
        early["small"] = _sequencer_all_gather([_pack_small_grads(named, loss_part)], "small_grad_gather", 6)[0]

    grad_x, d_norm_0 = _local_step(xs, tgt, win0, g_wg, wout0, win1_t, wout1, norm_0, a_v_norm_0, a_spatial_w_0,
                                   a_spatial_b_0, b_scale_0, norm_1, sink_1, final_norm, scatter, small_early)

    order = (("in0", 0), ("in0", 1), ("out0", 0), ("l1", 0), ("l1", 1))
    late = _direct_all_gather(d_norm_0.reshape(8, LANES), "norm_grad_gather")
    shards_late, _ = lax.optimization_barrier((shards, grad_x))
    big = _final_sum_adamw([blocks[t][i] for t, i in order], [received[t][i] for t, i in order], me, shards_late,
                           _shard_views(m_w_in_0, m_b_group_w_0, m_w_out_0, m_w_in_1, m_w_out_1),
                           _shard_views(v_w_in_0, v_b_group_w_0, v_w_out_0, v_w_in_1, v_w_out_1))
    weights = dict(norm_0=norm_0, a_v_norm_0=a_v_norm_0, a_spatial_w_0=a_spatial_w_0, a_spatial_b_0=a_spatial_b_0,
                   b_scale_0=b_scale_0, norm_1=norm_1, sink_1=sink_1, final_norm=final_norm)
    m_small = dict(norm_0=m_norm_0, a_v_norm_0=m_a_v_norm_0, a_spatial_w_0=m_a_spatial_w_0, a_spatial_b_0=m_a_spatial_b_0,
                   b_scale_0=m_b_scale_0, norm_1=m_norm_1, sink_1=m_sink_1, final_norm=m_final_norm)
    v_small = dict(norm_0=v_norm_0, a_v_norm_0=v_a_v_norm_0, a_spatial_w_0=v_a_spatial_w_0, a_spatial_b_0=v_a_spatial_b_0,
                   b_scale_0=v_b_scale_0, norm_1=v_norm_1, sink_1=v_sink_1, final_norm=v_final_norm)
    small, loss = _small_sum_adamw(early["small"], late, _small_views(weights), _small_views(m_small),
                                   _small_views(v_small))

    def in_order(kind):
        b = [b_.reshape(s_.shape) for b_, s_ in zip(big[kind], (w_in_0, b_group_w_0, w_out_0, w_in_1, w_out_1))]
        s = {name: t.reshape(weights[name].shape) for name, t in zip(SMALL_NAMES, small[kind])}
        return [s["norm_0"], b[0], s["a_v_norm_0"], s["a_spatial_w_0"], s["a_spatial_b_0"], b[1], s["b_scale_0"], b[2],
                s["norm_1"], b[3], s["sink_1"], b[4], s["final_norm"]]

    return (loss[0, 0], grad_x.reshape(1, seq, D), *in_order(0), *in_order(1), *in_order(2), *in_order(3))


def _local_step(xs, tgt, win0, g_wg, wout0, win1_t, wout1, norm_0, a_v_norm_0, a_spatial_w_0, a_spatial_b_0, b_scale_0,
                norm_1, sink_1, final_norm, scatter, small_early):
    seq = xs.shape[0]
    ws = a_spatial_w_0.astype(BF16)
    ws_t = jnp.swapaxes(ws, 1, 2)
    bias = jnp.repeat(a_spatial_b_0.T, GDIM, axis=1)
    g0, gv, scale, g1, gf = (t.reshape(1, D) for t in (norm_0, a_v_norm_0, b_scale_0, norm_1, final_norm))
    cos_t, sin_t = _rope_tables_t(seq)

    za, bx, bg, h0_t = _l0_in_proj(xs, g0, win0)
    g_wg, wout0, za = lax.optimization_barrier((g_wg, wout0, za))
    wg = g_wg.reshape(N_DEV, 4, 32, GDIM).transpose(1, 0, 2, 3).reshape(4, GDIM, GDIM)
    wg_t = jnp.swapaxes(wg, 1, 2)
    x1 = _l0_mix_fwd(za, bx, bg, xs, ws, bias, gv, wg, scale, wout0)
    win1_t, wout1, x1 = lax.optimization_barrier((win1_t, wout1, x1))
    qt, kt, vt, gatet, h1_t = _l1_in_proj(x1, g1, win1_t, cos_t, sin_t)
    dx2, dx2b, att, lse, loss_part, d_gf, d_wout1, d_wout1_wire = _l1_attn_fwd(
        qt, kt, vt, gatet, x1, tgt, wout1, gf, sink_1)

    dq_r, dgate, dk_pad, dv_pad, d_sink = _l1_attn_bwd(dx2b, wout1, qt, kt, vt, gatet, att, lse, sink_1)
    dk_r = dk_pad[:, BLK:BLK + seq]
    dv = dv_pad[:, BLK:BLK + seq]
    dx1, dx1b, dz1_t, d_g1 = _l1_in_proj_bwd(dq_r, dk_r, dv, dgate, cos_t, sin_t, win1_t, x1, g1, dx2)
    d_win1, d_win1_wire = _dw_matmul(h1_t, dz1_t, "dw_in_1", b_transposed=True, tn=1280, col_block=MIX1_IN // N_DEV)
    rows = lambda t: t.reshape(N_DEV, t.shape[0] // N_DEV, t.shape[1])
    scatter("l1", [d_win1, rows(d_wout1)], [d_win1_wire, rows(d_wout1_wire)])

    dz0, dp, cat_t, d_ws, _, d_gv, d_scale, d_wg, d_b = _l0_mix_bwd(
        dx1b, wout0, za, bx, bg, ws, ws_t, bias, gv, wg, wg_t, scale)
    dz0 = _l0_pool_bwd(dp, dz0)
    d_win0, d_win0_wire = _dw_matmul(h0_t, dz0, "dw_in_0", tn=1280, col_block=MIX0_IN // N_DEV)
    d_wg_blocks = _device_blocks(d_wg, 1)
    scatter("in0", [d_win0, d_wg_blocks], [d_win0_wire, d_wg_blocks])
    cat_t, _ = lax.optimization_barrier((cat_t, d_win0))
    d_wout0, d_wout0_wire = _dw_matmul(cat_t, dx1b, "dw_out_0")
    scatter("out0", [rows(d_wout0)], [rows(d_wout0_wire)])
    small_early(dict(a_v_norm_0=d_gv, a_spatial_w_0=d_ws, a_spatial_b_0=d_b.reshape(4, 8, CHUNK)[:, 0, :],
                     b_scale_0=d_scale, norm_1=d_g1, sink_1=d_sink[:, 0], final_norm=d_gf), loss_part)
    dz0, _ = lax.optimization_barrier((dz0, d_wout0))
    return _l0_in_proj_bwd(dz0, win0, xs, g0, dx1)
```

```python
import jax
import jax.numpy as jnp
from jax import lax
from jax.experimental import pallas as pl
from jax.experimental.pallas import tpu as pltpu
from jax.experimental.pallas import tpu_sc as plsc

F32 = jnp.float32
BF16 = jnp.bfloat16

D = 1024
EPS = 1e-6
NEG_INF = -1e30
CHUNK = 128
A_GROUPS = 4
POOL_WINDOWS = (2, 4, 8, 16)
POOL_HALO = 8
GDIM = 256
N_HEADS = 16
N_KV = 4
GQA = 4
HD = 64
BLK = 128
ROT_HALF = 8
ROPE_THETA = 500000.0
SCALE = HD ** -0.5
MIX0_IN = 5 * D
MIX1_IN = 2560
KV_W = N_KV * HD
Q_ROWS, K_ROWS, V_ROWS, G_ROWS = (0, D), (D, D + KV_W), (D + KV_W, D + 2 * KV_W), (D + 2 * KV_W, MIX1_IN)
TQ = 512

ADAM_LR = 0.001
ADAM_B1 = 0.9
ADAM_B2 = 0.999
ADAM_EPS = 1e-08
ADAM_WD = 0.01
ADAM_STEP = 10

N_DEV = 8
LANES = 128
MIB = 2 ** 20
MESH = pl.DeviceIdType.MESH


def _params(limit_mib, n_axes=1):
    return pltpu.CompilerParams(vmem_limit_bytes=limit_mib * MIB, dimension_semantics=("arbitrary",) * n_axes)


def _resident(shape):
    nd = len(shape)
    return pl.BlockSpec(shape, lambda *_: (0,) * nd, pipeline_mode=pl.Buffered(1))


def _gelu(x):
    k = 0.7978845608028654
    return 0.5 * x * (1.0 + jnp.tanh(k * (x + 0.044715 * x * x * x)))


def _gelu_and_grad(x):
    k = 0.7978845608028654
    x2 = x * x
    t = jnp.tanh(k * (x + 0.044715 * x * x2))
    g = 0.5 * x * (1.0 + t)
    dg = 0.5 * (1.0 + t) + 0.5 * x * (1.0 - t * t) * (k * (1.0 + 3.0 * 0.044715 * x2))
    return g, dg


def _silu_and_grad(x):
    s = jax.nn.sigmoid(x)
    return x * s, s * (1.0 + x * (1.0 - s))


def _nt(a, b):
    return lax.dot_general(a, b, (((1,), (1,)), ((), ())), preferred_element_type=F32)


def _tn(a, b):
    return lax.dot_general(a, b, (((0,), (0,)), ((), ())), preferred_element_type=F32)


def _mm(a, b):
    return jnp.dot(a, b, preferred_element_type=F32)


def _rope_tables_t(seq):
    inv = ROPE_THETA ** (-jnp.arange(0, 2 * ROT_HALF, 2, dtype=F32) / (2 * ROT_HALF))
    ang = inv[:, None] * jnp.arange(seq, dtype=F32)[None, :]
    return jnp.cos(ang), jnp.sin(ang)


def _rope_t(z, c, s, n_heads, sign):
    parts = []
    for h in range(n_heads):
        b = h * HD
        x1, x2 = z[b:b + ROT_HALF], z[b + ROT_HALF:b + 2 * ROT_HALF]
        if sign > 0:
            parts += [x1 * c - x2 * s, x2 * c + x1 * s]
        else:
            parts += [x1 * c + x2 * s, x2 * c - x1 * s]
        parts.append(z[b + 2 * ROT_HALF:b + HD])
    return jnp.concatenate(parts, axis=0)


def _l0_in_proj(x, g0, w):
    seq = x.shape[0]
    tm = 512

    def body(x_ref, g_ref, w_ref, za_ref, bx_ref, bg_ref, ht_ref):
        xf = x_ref[...]
        r = lax.rsqrt(jnp.mean(xf * xf, axis=1, keepdims=True) + EPS)
        h = (xf * r * g_ref[...]).astype(BF16)
        ht_ref[...] = h.T
        for j in range(3):
            za_ref[:, j * D:(j + 1) * D] = _mm(h, w_ref[:, j * D:(j + 1) * D]).astype(BF16)
        bx_ref[...] = _mm(h, w_ref[:, 3 * D:4 * D])
        bg_ref[...] = _mm(h, w_ref[:, 4 * D:5 * D]).astype(BF16)

    return pl.pallas_call(
        body, grid=(seq // tm,), name="l0_in_proj",
        out_shape=(jax.ShapeDtypeStruct((seq, 3 * D), BF16), jax.ShapeDtypeStruct((seq, D), F32),
                   jax.ShapeDtypeStruct((seq, D), BF16), jax.ShapeDtypeStruct((D, seq), BF16)),
        in_specs=[pl.BlockSpec((tm, D), lambda i: (i, 0)), _resident((1, D)), _resident((D, MIX0_IN))],
        out_specs=(pl.BlockSpec((tm, 3 * D), lambda i: (i, 0)), pl.BlockSpec((tm, D), lambda i: (i, 0)),
                   pl.BlockSpec((tm, D), lambda i: (i, 0)), pl.BlockSpec((D, tm), lambda i: (0, i))),
        compiler_params=_params(48),
    )(x, g0, w)


POOL_EXT = 40


def _fill_halo(ext_ref, cur, prev_ref, next_ref, i, n_tiles, ts):
    ext_ref[pl.ds(0, POOL_HALO), :] = jnp.where(i > 0, prev_ref[...], 0.0)
    ext_ref[pl.ds(POOL_HALO, ts), :] = cur
    ext_ref[pl.ds(POOL_HALO + ts, POOL_HALO), :] = jnp.where(i < n_tiles - 1, next_ref[...], 0.0)
    ext_ref[pl.ds(2 * POOL_HALO + ts, POOL_EXT - 2 * POOL_HALO), :] = jnp.zeros((POOL_EXT - 2 * POOL_HALO, D), F32)


def _window_sums(src_ref, tmp_refs, ts, cols, w, shift):
    if w == 2:
        return src_ref[pl.ds(POOL_HALO - 1 + shift, ts), cols] + src_ref[pl.ds(POOL_HALO + shift, ts), cols]
    d2, d4, d8 = tmp_refs
    n2, n4, n8 = ts + 32, ts + 24, ts + 16
    d2[pl.ds(0, n2), :] = src_ref[pl.ds(0, n2), cols] + src_ref[pl.ds(1, n2), cols]
    if w == 4:
        return d2[pl.ds(POOL_HALO - 2 + shift, ts), :] + d2[pl.ds(POOL_HALO + shift, ts), :]
    d4[pl.ds(0, n4), :] = d2[pl.ds(0, n4), :] + d2[pl.ds(2, n4), :]
    if w == 8:
        return d4[pl.ds(POOL_HALO - 4 + shift, ts), :] + d4[pl.ds(POOL_HALO + shift, ts), :]
    d8[pl.ds(0, n8), :] = d4[pl.ds(0, n8), :] + d4[pl.ds(4, n8), :]
    return d8[pl.ds(shift, ts), :] + d8[pl.ds(POOL_HALO + shift, ts), :]


def _pool_scratch(ts):
    return [pltpu.VMEM((ts + POOL_EXT, D), F32)] + [pltpu.VMEM((ts + POOL_EXT, GDIM), F32)] * 3


def _pool_forward(xe_ref, tmp_refs, ts, t0, seq):
    tg = t0 + lax.broadcasted_iota(jnp.int32, (ts, 1), 0)
    outs = []
    for gi, w in enumerate(POOL_WINDOWS):
        hw = w // 2
        cols = slice(gi * GDIM, (gi + 1) * GDIM)
        cnt = (jnp.minimum(tg + hw, seq) - jnp.maximum(tg - hw, 0)).astype(F32)
        outs.append(_window_sums(xe_ref, tmp_refs, ts, cols, w, 0) / cnt - xe_ref[pl.ds(POOL_HALO, ts), cols])
    return jnp.concatenate(outs, axis=1)


def _spatial_mix(ws_ref, vnb, bias, ts):
    rows = []
    for c in range(ts // CHUNK):
        vc = vnb[c * CHUNK:(c + 1) * CHUNK, :]
        rows.append(jnp.concatenate(
            [_mm(ws_ref[h], vc[:, h * GDIM:(h + 1) * GDIM]) for h in range(A_GROUPS)], axis=1) + bias)
    return jnp.concatenate(rows, axis=0)


def _halo_specs(ts, seq, width):
    per = ts // POOL_HALO
    last = seq // POOL_HALO - 1
    prev = pl.BlockSpec((POOL_HALO, width), lambda i: (jnp.maximum(i * per - 1, 0), 0))
    nxt = pl.BlockSpec((POOL_HALO, width), lambda i: (jnp.minimum((i + 1) * per, last), 0))
    return prev, nxt


def _l0_mix_fwd(za, bx, bg, x, ws, bias, gv, wg, scale, wout):
    seq = x.shape[0]
    ts = 512
    n_tiles = seq // ts

    def body(za_ref, bx_ref, bxp_ref, bxn_ref, bg_ref, x_ref, ws_ref, bias_ref, gv_ref, wg_ref, sc_ref, wo_ref,
             x1_ref, xe_ref, *tmp_refs):
        i = pl.program_id(0)
        u = _gelu(za_ref[:, 0:D].astype(F32))
        vg = _gelu(za_ref[:, D:2 * D].astype(F32))
        rv = lax.rsqrt(jnp.mean(vg * vg, axis=1, keepdims=True) + EPS)
        vnb = (vg * rv * gv_ref[...]).astype(BF16)
        mixed = _spatial_mix(ws_ref, vnb, bias_ref[...], ts)
        ag = za_ref[:, 2 * D:3 * D].astype(F32)
        ya = (u * mixed * (ag * jax.nn.sigmoid(ag))).astype(BF16)

        _fill_halo(xe_ref, bx_ref[...], bxp_ref, bxn_ref, i, n_tiles, ts)
        pb = _pool_forward(xe_ref, tmp_refs, ts, i * ts, seq).astype(BF16)
        y = jnp.concatenate([_mm(pb[:, g * GDIM:(g + 1) * GDIM], wg_ref[g]) for g in range(4)], axis=1) * sc_ref[...]
        bgf = bg_ref[...].astype(F32)
        yb = (y * (bgf * jax.nn.sigmoid(bgf))).astype(BF16)
        x1_ref[...] = x_ref[...] + _mm(ya, wo_ref[0:D, :]) + _mm(yb, wo_ref[D:2 * D, :])

    prev, nxt = _halo_specs(ts, seq, D)
    row = lambda w: pl.BlockSpec((ts, w), lambda i: (i, 0))
    return pl.pallas_call(
        body, grid=(n_tiles,), name="l0_mix_fwd",
        out_shape=jax.ShapeDtypeStruct((seq, D), F32),
        in_specs=[row(3 * D), row(D), prev, nxt, row(D), row(D), _resident((4, CHUNK, CHUNK)), _resident((CHUNK, D)),
                  _resident((1, D)), _resident((4, GDIM, GDIM)), _resident((1, D)), _resident((2 * D, D))],
        out_specs=row(D),
        scratch_shapes=_pool_scratch(ts),
        compiler_params=_params(56),
    )(za, bx, bx, bx, bg, x, ws, bias, gv, wg, scale, wout)


def _l1_in_proj(x1, g1, w_t, cos_t, sin_t):
    seq = x1.shape[0]
    tm = 512

    def body(x_ref, g_ref, wt_ref, c_ref, s_ref, q_ref, k_ref, v_ref, gate_ref, ht_ref):
        xf = x_ref[...]
        r = lax.rsqrt(jnp.mean(xf * xf, axis=1, keepdims=True) + EPS)
        ht = (xf * r * g_ref[...]).astype(BF16).T
        ht_ref[...] = ht
        c, s = c_ref[...], s_ref[...]
        q_ref[...] = (_rope_t(_mm(wt_ref[Q_ROWS[0]:Q_ROWS[1], :], ht), c, s, N_HEADS, 1) * SCALE).astype(BF16)
        k_ref[...] = _rope_t(_mm(wt_ref[K_ROWS[0]:K_ROWS[1], :], ht), c, s, N_KV, 1).astype(BF16)
        v_ref[...] = _mm(wt_ref[V_ROWS[0]:V_ROWS[1], :], ht).astype(BF16)
        gate_ref[...] = _mm(wt_ref[G_ROWS[0]:G_ROWS[1], :], ht).astype(BF16)

    col = lambda rows: pl.BlockSpec((rows, tm), lambda i: (0, i))
    return pl.pallas_call(
        body, grid=(seq // tm,), name="l1_in_proj",
        out_shape=(jax.ShapeDtypeStruct((D, seq), BF16), jax.ShapeDtypeStruct((KV_W, seq), BF16),
                   jax.ShapeDtypeStruct((KV_W, seq), BF16), jax.ShapeDtypeStruct((D, seq), BF16),
                   jax.ShapeDtypeStruct((D, seq), BF16)),
        in_specs=[pl.BlockSpec((tm, D), lambda i: (i, 0)), _resident((1, D)), _resident((MIX1_IN, D)), col(ROT_HALF),
                  col(ROT_HALF)],
        out_specs=(col(D), col(KV_W), col(KV_W), col(D), col(D)),
        compiler_params=_params(48),
    )(x1, g1, w_t, cos_t, sin_t)


def _band_specs_t(nb, clamp_i):
    per = TQ // BLK
    prev = pl.BlockSpec((KV_W, BLK), lambda i: (0, jnp.maximum(clamp_i(i) * per - 1, 0)))
    cur = pl.BlockSpec((KV_W, TQ), lambda i: (0, clamp_i(i)))
    nxt = pl.BlockSpec((KV_W, BLK), lambda i: (0, jnp.minimum((clamp_i(i) + 1) * per, nb - 1)))
    return [prev, cur, nxt]


def _fill_band(buf, p_ref, c_ref, n_ref):
    buf[:, 0:BLK] = p_ref[...]
    buf[:, BLK:BLK + TQ] = c_ref[...]
    buf[:, BLK + TQ:2 * BLK + TQ] = n_ref[...]


def _band_bias_t(n, nb):
    c = lax.broadcasted_iota(jnp.int32, (BLK, BLK), 0)
    r = lax.broadcasted_iota(jnp.int32, (BLK, BLK), 1)
    first = jnp.where((c >= r) & (n > 0), 0.0, NEG_INF).astype(F32)
    last = jnp.where((c <= r) & (n < nb - 1), 0.0, NEG_INF).astype(F32)
    return jnp.concatenate([first] * HPP, axis=1), jnp.concatenate([last] * HPP, axis=1)


def _masked(st, bias):
    first, last = bias
    return jnp.concatenate([st[0:BLK] + first, st[BLK:2 * BLK], st[2 * BLK:3 * BLK] + last], axis=0)


AUG = 16


def _ones_rows(n_ones, width):
    return (lax.broadcasted_iota(jnp.int32, (AUG, width), 0) < n_ones).astype(BF16)


def _minus_rows(vec):
    hi = vec.astype(BF16).astype(F32)
    lo = vec - hi
    return jnp.concatenate([-hi, -lo, jnp.zeros((AUG - 2, vec.shape[1]), F32)], axis=0).astype(BF16)


HPP = GQA
FWD_GROUP, BWD_GROUP = 4, 2


def _heads_t(ref, h0, c0):
    return jnp.concatenate([ref[(h0 + g) * HD:(h0 + g + 1) * HD, c0:c0 + BLK] for g in range(HPP)], axis=1)


def _row4(ref, h0, c0):
    return jnp.concatenate([ref[h0 + g:h0 + g + 1, c0:c0 + BLK] for g in range(HPP)], axis=1)


def _sink_row(sink_ref, h0):
    return jnp.concatenate([jnp.full((1, BLK), sink_ref[h0 + g], F32) for g in range(HPP)], axis=1)


def _l1_attn_fwd(qt, kt, vt, gatet, x1, tgt, wout, gf, sink):
    seq = x1.shape[0]
    nq, nb = seq // TQ, seq // BLK

    def body(q_ref, gate_ref, kp_ref, k_ref, kn_ref, vp_ref, v_ref, vn_ref, x1_ref, tgt_ref, wo_ref, gf_ref, sink_ref,
             dx2_ref, dx2b_ref, att_ref, lse_ref, loss_ref, dgf_ref, dwo_ref, dwo_wire_ref, kbuf, vbuf, att_scr):
        i = pl.program_id(0)

        @pl.when(i == 0)
        def _():
            loss_ref[...] = jnp.zeros_like(loss_ref)
            dgf_ref[...] = jnp.zeros_like(dgf_ref)
            dwo_ref[...] = jnp.zeros_like(dwo_ref)

        _fill_band(kbuf, kp_ref, k_ref, kn_ref)
        _fill_band(vbuf, vp_ref, v_ref, vn_ref)
        ones_row = _ones_rows(1, 3 * BLK)
        groups = [list(range(0, N_HEADS, HPP))[g:g + FWD_GROUP] for g in range(0, N_HEADS // HPP, FWD_GROUP)]
        for j, passes in ((j, grp) for j in range(TQ // BLK) for grp in groups):
            c0 = j * BLK
            bias = _band_bias_t(i * (TQ // BLK) + j, nb)
            lse_rows = []
            kv_rows = [slice(h0 // GQA * HD, (h0 // GQA + 1) * HD) for h0 in passes]
            sts = [_masked(_tn(kbuf[rows, c0:c0 + 3 * BLK], _heads_t(q_ref, h0, c0)), bias)
                   for h0, rows in zip(passes, kv_rows)]
            ms, ps, sks = [], [], []
            for h0, st in zip(passes, sts):
                sk = _sink_row(sink_ref, h0)
                m = jnp.maximum(jnp.max(st, axis=0, keepdims=True), sk)
                ms.append(m)
                sks.append(sk)
                ps.append(jnp.exp(st - m).astype(BF16))
            pvs = [_mm(jnp.concatenate([vbuf[rows, c0:c0 + 3 * BLK], ones_row], axis=0), p)
                   for rows, p in zip(kv_rows, ps)]
            for h0, pv, m, sk in zip(passes, pvs, ms, sks):
                den = pv[HD:HD + 1, :] + jnp.exp(sk - m)
                ot = pv[0:HD, :] / den
                lse = m + jnp.log(den)
                for g in range(HPP):
                    h = h0 + g
                    att_scr[h * HD:(h + 1) * HD, c0:c0 + BLK] = ot[:, g * BLK:(g + 1) * BLK]
                    lse_rows.append(lse[:, g * BLK:(g + 1) * BLK])
            lse_ref[passes[0]:passes[0] + len(lse_rows), c0:c0 + BLK] = jnp.concatenate(lse_rows, axis=0)

        att = att_scr[...]
        gate = gate_ref[...].astype(F32)
        yt = (att * (gate * jax.nn.sigmoid(gate))).astype(BF16)
        att_ref[...] = att.astype(BF16)
        x2 = x1_ref[...] + _mm(yt.T, wo_ref[...])
        r = lax.rsqrt(jnp.mean(x2 * x2, axis=1, keepdims=True) + EPS)
        xn = x2 * r
        diff = xn * gf_ref[...] - tgt_ref[...]
        loss_ref[...] += 0.5 * jnp.sum(jnp.mean(diff * diff, axis=1, keepdims=True), axis=0, keepdims=True)
        dout = diff * (1.0 / D)
        dgf_ref[...] += jnp.sum(dout * xn, axis=0, keepdims=True)
        dxn = dout * gf_ref[...]
        dx2 = r * (dxn - xn * jnp.mean(dxn * xn, axis=1, keepdims=True))
        dx2_ref[...] = dx2
        dx2b = dx2.astype(BF16)
        dx2b_ref[...] = dx2b
        dwo_ref[...] += _mm(yt, dx2b)

        @pl.when(i == nq - 1)
        def _():
            dwo_wire_ref[...] = dwo_ref[...].astype(BF16)

    ident = lambda i: i
    row = pl.BlockSpec((TQ, D), lambda i: (i, 0))
    col = lambda rows: pl.BlockSpec((rows, TQ), lambda i: (0, i))
    whole = pl.BlockSpec((D, D), lambda i: (0, 0))
    return pl.pallas_call(
        body, grid=(nq,), name="l1_attn_fwd",
        out_shape=(jax.ShapeDtypeStruct((seq, D), F32), jax.ShapeDtypeStruct((seq, D), BF16),
                   jax.ShapeDtypeStruct((D, seq), BF16),
                   jax.ShapeDtypeStruct((N_HEADS, seq), F32), jax.ShapeDtypeStruct((1, 1), F32),
                   jax.ShapeDtypeStruct((1, D), F32), jax.ShapeDtypeStruct((D, D), F32), jax.ShapeDtypeStruct((D, D), BF16)),
        in_specs=[col(D), col(D)] + _band_specs_t(nb, ident) + _band_specs_t(nb, ident) + [
            row, row, _resident((D, D)), _resident((1, D)), pl.BlockSpec(memory_space=pltpu.SMEM)],
        out_specs=(row, row, col(D), col(N_HEADS), pl.BlockSpec((1, 1), lambda i: (0, 0)),
                   pl.BlockSpec((1, D), lambda i: (0, 0)), whole, whole),
        scratch_shapes=[pltpu.VMEM((KV_W, TQ + 2 * BLK), BF16), pltpu.VMEM((KV_W, TQ + 2 * BLK), BF16),
                        pltpu.VMEM((D, TQ), F32)],
        compiler_params=_params(56),
    )(qt, gatet, kt, kt, kt, vt, vt, vt, x1, tgt, wout, gf, sink)


def _l1_attn_bwd(dx2b, wout, qt, kt, vt, gatet, att, lse, sink):
    seq = dx2b.shape[0]
    nq, nb = seq // TQ, seq // BLK

    def body(dx_ref, wo_ref, q_ref, gate_ref, kp_ref, k_ref, kn_ref, vp_ref, v_ref, vn_ref, att_ref, lse_ref, sink_ref,
             dq_ref, dgate_ref, dk_ref, dv_ref, dsink_ref, kbuf, vbuf, dkacc, dvacc, dat_scr, delta_scr, dsacc):
        i = pl.program_id(0)

        @pl.when(i == 0)
        def _():
            dkacc[...] = jnp.zeros_like(dkacc)
            dvacc[...] = jnp.zeros_like(dvacc)
            dsacc[...] = jnp.zeros_like(dsacc)

        @pl.when(i > 0)
        def _():
            for acc in (dkacc, dvacc):
                acc[:, 0:2 * BLK] = acc[:, TQ:TQ + 2 * BLK]
                acc[:, 2 * BLK:2 * BLK + TQ] = jnp.zeros((KV_W, TQ), F32)

        @pl.when(i < nq)
        def _():
            _fill_band(kbuf, kp_ref, k_ref, kn_ref)
            _fill_band(vbuf, vp_ref, v_ref, vn_ref)
            dyt = _nt(wo_ref[...], dx_ref[...])
            sg, dsg = _silu_and_grad(gate_ref[...].astype(F32))
            attf = att_ref[...].astype(F32)
            dat = dyt * sg
            dat_scr[...] = dat.astype(BF16)
            dgate_ref[...] = (dyt * attf * dsg).astype(BF16)
            dl = dat * attf
            delta_scr[...] = jnp.concatenate(
                [jnp.sum(dl[h * HD:(h + 1) * HD, :], axis=0, keepdims=True) for h in range(N_HEADS)], axis=0)
            ones_rows = _ones_rows(2, 3 * BLK)
            groups = [list(range(0, N_HEADS, HPP))[g:g + BWD_GROUP] for g in range(0, N_HEADS // HPP, BWD_GROUP)]
            for j, passes in ((j, grp) for j in range(TQ // BLK) for grp in groups):
                c0 = j * BLK
                bias = _band_bias_t(i * (TQ // BLK) + j, nb)
                kv_rows = [slice(h0 // GQA * HD, (h0 // GQA + 1) * HD) for h0 in passes]
                q4s = [_heads_t(q_ref, h0, c0) for h0 in passes]
                do4s = [_heads_t(dat_scr, h0, c0) for h0 in passes]
                lse4s = [_row4(lse_ref, h0, c0) for h0 in passes]
                delta4s = [_row4(delta_scr, h0, c0) for h0 in passes]
                kths = [kbuf[rows, c0:c0 + 3 * BLK] for rows in kv_rows]
                sts = [_tn(jnp.concatenate([kth, ones_rows], axis=0), jnp.concatenate([q4, _minus_rows(lse4)], axis=0))
                       for kth, q4, lse4 in zip(kths, q4s, lse4s)]
                dpds = [_tn(jnp.concatenate([vbuf[rows, c0:c0 + 3 * BLK], ones_rows], axis=0),
                            jnp.concatenate([do4, _minus_rows(delta4)], axis=0))
                        for rows, do4, delta4 in zip(kv_rows, do4s, delta4s)]
                ps = [jnp.exp(_masked(st, bias)) for st in sts]
                dss = [(p * dpd).astype(BF16) for p, dpd in zip(ps, dpds)]
                dq4s = [_mm(kth, ds) * SCALE for kth, ds in zip(kths, dss)]
                dks = [_nt(q4, ds) for q4, ds in zip(q4s, dss)]
                dvs = [_nt(do4, p.astype(BF16)) for do4, p in zip(do4s, ps)]
                for h0, rows, dq4, dk, dv, lse4, delta4 in zip(passes, kv_rows, dq4s, dks, dvs, lse4s, delta4s):
                    dkacc[rows, c0:c0 + 3 * BLK] += dk
                    dvacc[rows, c0:c0 + 3 * BLK] += dv
                    dsk = -jnp.exp(_sink_row(sink_ref, h0) - lse4) * delta4
                    for g in range(HPP):
                        h = h0 + g
                        dq_ref[h * HD:(h + 1) * HD, c0:c0 + BLK] = dq4[:, g * BLK:(g + 1) * BLK].astype(BF16)
                        dsacc[h:h + 1, :] += dsk[:, g * BLK:(g + 1) * BLK]

        dk_ref[...] = dkacc[:, 0:TQ].astype(BF16)
        dv_ref[...] = dvacc[:, 0:TQ].astype(BF16)

        @pl.when(i == nq)
        def _():
            dsink_ref[...] = jnp.broadcast_to(jnp.sum(dsacc[...], axis=1, keepdims=True), (N_HEADS, LANES))

    clamp = lambda i: jnp.minimum(i, nq - 1)
    row = pl.BlockSpec((TQ, D), lambda i: (clamp(i), 0))
    col = lambda rows: pl.BlockSpec((rows, TQ), lambda i: (0, clamp(i)))
    pad = pl.BlockSpec((KV_W, TQ), lambda i: (0, i))
    return pl.pallas_call(
        body, grid=(nq + 1,), name="l1_attn_bwd",
        out_shape=(jax.ShapeDtypeStruct((D, seq), BF16), jax.ShapeDtypeStruct((D, seq), BF16),
                   jax.ShapeDtypeStruct((KV_W, seq + TQ), BF16), jax.ShapeDtypeStruct((KV_W, seq + TQ), BF16),
                   jax.ShapeDtypeStruct((N_HEADS, LANES), F32)),
        in_specs=[row, _resident((D, D)), col(D), col(D)] + _band_specs_t(nb, clamp) + _band_specs_t(nb, clamp) + [
            col(D), col(N_HEADS), pl.BlockSpec(memory_space=pltpu.SMEM)],
        out_specs=(col(D), col(D), pad, pad, pl.BlockSpec((N_HEADS, LANES), lambda i: (0, 0))),
        scratch_shapes=[pltpu.VMEM((KV_W, TQ + 2 * BLK), BF16), pltpu.VMEM((KV_W, TQ + 2 * BLK), BF16),
                        pltpu.VMEM((KV_W, TQ + 2 * BLK), F32), pltpu.VMEM((KV_W, TQ + 2 * BLK), F32),
                        pltpu.VMEM((D, TQ), BF16), pltpu.VMEM((N_HEADS, TQ), F32), pltpu.VMEM((N_HEADS, LANES), F32)],
        compiler_params=_params(56),
    )(dx2b, wout, qt, gatet, kt, kt, kt, vt, vt, vt, att, lse, sink)


def _l1_in_proj_bwd(dq_r, dk_r, dv, dgate, cos_t, sin_t, w_t, x1, g1, dx2):
    seq = x1.shape[0]
    tm = 512

    def body(dq_ref, dk_ref, dv_ref, dg_ref, c_ref, s_ref, w_ref, x_ref, g_ref, dres_ref,
             dx_ref, dxb_ref, dz_ref, dn_ref):
        @pl.when(pl.program_id(0) == 0)
        def _():
            dn_ref[...] = jnp.zeros_like(dn_ref)

        c, s = c_ref[...], s_ref[...]
        dq = _rope_t(dq_ref[...].astype(F32), c, s, N_HEADS, -1).astype(BF16)
        dk = _rope_t(dk_ref[...].astype(F32), c, s, N_KV, -1).astype(BF16)
        dz = jnp.concatenate([dq, dk, dv_ref[...], dg_ref[...]], axis=0)
        dz_ref[...] = dz
        dh = _tn(dz, w_ref[...])
        xf = x_ref[...]
        r = lax.rsqrt(jnp.mean(xf * xf, axis=1, keepdims=True) + EPS)
        xn = xf * r
        dn_ref[...] += jnp.sum(dh * xn, axis=0, keepdims=True)
        dxn = dh * g_ref[...]
        dx = dres_ref[...] + r * (dxn - xn * jnp.mean(dxn * xn, axis=1, keepdims=True))
        dx_ref[...] = dx
        dxb_ref[...] = dx.astype(BF16)

    row = pl.BlockSpec((tm, D), lambda i: (i, 0))
    col = lambda rows: pl.BlockSpec((rows, tm), lambda i: (0, i))
    return pl.pallas_call(
        body, grid=(seq // tm,), name="l1_in_proj_bwd",
        out_shape=(jax.ShapeDtypeStruct((seq, D), F32), jax.ShapeDtypeStruct((seq, D), BF16),
                   jax.ShapeDtypeStruct((MIX1_IN, seq), BF16), jax.ShapeDtypeStruct((1, D), F32)),
        in_specs=[col(D), col(KV_W), col(KV_W), col(D), col(ROT_HALF), col(ROT_HALF), _resident((MIX1_IN, D)), row,
                  _resident((1, D)), row],
        out_specs=(row, row, col(MIX1_IN), pl.BlockSpec((1, D), lambda i: (0, 0))),
        compiler_params=_params(48),
    )(dq_r, dk_r, dv, dgate, cos_t, sin_t, w_t, x1, g1, dx2)


def _l0_mix_bwd(dx1b, wout, za, bx, bg, ws, ws_t, bias, gv, wg, wg_t, scale):
    seq = dx1b.shape[0]
    ts = 256
    n_tiles = seq // ts

    def body(dx_ref, wo_ref, za_ref, bx_ref, bxp_ref, bxn_ref, bg_ref, ws_ref, wst_ref, bias_ref, gv_ref, wg_ref,
             wgt_ref, sc_ref,
             dz_ref, dp_ref, catt_ref, dws_ref, dbias_ref, dgv_ref, dsc_ref, dwg_ref, db_ref, xe_ref, *tmp_refs):
        i = pl.program_id(0)

        @pl.when(i == 0)
        def _():
            for r_ in (dws_ref, dbias_ref, dgv_ref, dsc_ref, dwg_ref, db_ref):
                r_[...] = jnp.zeros_like(r_)

        dxb = dx_ref[...]
        dya = _nt(dxb, wo_ref[0:D, :])
        dyb = _nt(dxb, wo_ref[D:2 * D, :])

        u, du = _gelu_and_grad(za_ref[:, 0:D].astype(F32))
        vg, dvg_dz = _gelu_and_grad(za_ref[:, D:2 * D].astype(F32))
        rv = lax.rsqrt(jnp.mean(vg * vg, axis=1, keepdims=True) + EPS)
        vnorm = vg * rv
        gvw = gv_ref[...]
        vnb = (vnorm * gvw).astype(BF16)
        mixed = _spatial_mix(ws_ref, vnb, bias_ref[...], ts)
        sga, dsga = _silu_and_grad(za_ref[:, 2 * D:3 * D].astype(F32))
        um = u * mixed
        ya = (um * sga).astype(BF16)
        t = dya * sga
        dz_ref[:, 0:D] = (t * mixed * du).astype(BF16)
        dz_ref[:, 2 * D:3 * D] = (dya * um * dsga).astype(BF16)
        dmixed = t * u
        dmb = dmixed.astype(BF16)
        dvn_rows = []
        dbias = jnp.zeros((CHUNK, D), F32)
        for c in range(ts // CHUNK):
            rows = slice(c * CHUNK, (c + 1) * CHUNK)
            dbias = dbias + dmixed[rows, :]
            parts = []
            for h in range(A_GROUPS):
                cols = slice(h * GDIM, (h + 1) * GDIM)
                dws_ref[h] += _nt(dmb[rows, cols], vnb[rows, cols])
                parts.append(_mm(wst_ref[h], dmb[rows, cols]))
            dvn_rows.append(jnp.concatenate(parts, axis=1))
        dbias_ref[...] += dbias
        dvn = jnp.concatenate(dvn_rows, axis=0)
        dgv_ref[...] += jnp.sum(dvn * vnorm, axis=0, keepdims=True)
        dxn = dvn * gvw
        dvg = rv * (dxn - vnorm * jnp.mean(dxn * vnorm, axis=1, keepdims=True))
        dz_ref[:, D:2 * D] = (dvg * dvg_dz).astype(BF16)

        _fill_halo(xe_ref, bx_ref[...], bxp_ref, bxn_ref, i, n_tiles, ts)
        pb = _pool_forward(xe_ref, tmp_refs, ts, i * ts, seq).astype(BF16)
        ypre = jnp.concatenate([_mm(pb[:, g * GDIM:(g + 1) * GDIM], wg_ref[g]) for g in range(4)], axis=1)
        sc = sc_ref[...]
        y = ypre * sc
        sgb, dsgb = _silu_and_grad(bg_ref[...].astype(F32))
        yb = (y * sgb).astype(BF16)
        dy_b = dyb * sgb
        dz_ref[:, 3 * D:4 * D] = jnp.zeros((ts, D), BF16)
        dz_ref[:, 4 * D:5 * D] = (dyb * y * dsgb).astype(BF16)
        dsc_ref[...] += jnp.sum(dy_b * ypre, axis=0, keepdims=True)
        dypre = (dy_b * sc).astype(BF16)
        dps = []
        for g in range(4):
            cols = slice(g * GDIM, (g + 1) * GDIM)
            dwg_ref[g] += _tn(pb[:, cols], dypre[:, cols])
            dps.append(_mm(dypre[:, cols], wgt_ref[g]))
        dp_ref[...] = jnp.concatenate(dps, axis=1)
        catt_ref[...] = jnp.concatenate([ya, yb], axis=1).T

        @pl.when(i == n_tiles - 1)
        def _():
            for h in range(A_GROUPS):
                tot = jnp.sum(dbias_ref[:, h * GDIM:(h + 1) * GDIM].T, axis=0, keepdims=True)
                db_ref[pl.ds(h * 8, 8), :] = jnp.broadcast_to(tot, (8, CHUNK))

    prev, nxt = _halo_specs(ts, seq, D)
    row = lambda w_: pl.BlockSpec((ts, w_), lambda i: (i, 0))
    acc = lambda shape: pl.BlockSpec(shape, lambda i: (0,) * len(shape))
    return pl.pallas_call(
        body, grid=(n_tiles,), name="l0_mix_bwd",
        out_shape=(jax.ShapeDtypeStruct((seq, MIX0_IN), BF16), jax.ShapeDtypeStruct((seq, D), F32),
                   jax.ShapeDtypeStruct((2 * D, seq), BF16),
                   jax.ShapeDtypeStruct((4, CHUNK, CHUNK), F32), jax.ShapeDtypeStruct((CHUNK, D), F32),
                   jax.ShapeDtypeStruct((1, D), F32), jax.ShapeDtypeStruct((1, D), F32),
                   jax.ShapeDtypeStruct((4, GDIM, GDIM), F32), jax.ShapeDtypeStruct((32, CHUNK), F32)),
        in_specs=[row(D), _resident((2 * D, D)), row(3 * D), row(D), prev, nxt, row(D), _resident((4, CHUNK, CHUNK)),
                  _resident((4, CHUNK, CHUNK)), _resident((CHUNK, D)), _resident((1, D)), _resident((4, GDIM, GDIM)),
                  _resident((4, GDIM, GDIM)), _resident((1, D))],
        out_specs=(row(MIX0_IN), row(D), pl.BlockSpec((2 * D, ts), lambda i: (0, i)),
                   acc((4, CHUNK, CHUNK)), acc((CHUNK, D)), acc((1, D)), acc((1, D)), acc((4, GDIM, GDIM)),
                   acc((32, CHUNK))),
        scratch_shapes=_pool_scratch(ts),
        compiler_params=_params(56),
    )(dx1b, wout, za, bx, bx, bx, bg, ws, ws_t, bias, gv, wg, wg_t, scale)


def _l0_pool_bwd(dp, dz):
    seq = dp.shape[0]
    ts = 512
    n_tiles = seq // ts
    ext = ts + 2 * POOL_HALO

    def body(dp_ref, dpp_ref, dpn_ref, dz_ref, out_ref, qe_ref, *tmp_refs):
        i = pl.program_id(0)
        _fill_halo(qe_ref, dp_ref[...], dpp_ref, dpn_ref, i, n_tiles, ts)
        te = i * ts - POOL_HALO + lax.broadcasted_iota(jnp.int32, (ext, 1), 0)
        for gi, w in enumerate(POOL_WINDOWS):
            hw = w // 2
            cols = slice(gi * GDIM, (gi + 1) * GDIM)
            cnt = jnp.maximum(jnp.minimum(te + hw, seq) - jnp.maximum(te - hw, 0), 1).astype(F32)
            qe_ref[pl.ds(0, ext), cols] = qe_ref[pl.ds(0, ext), cols] / cnt
        outs = []
        for gi, w in enumerate(POOL_WINDOWS):
            cols = slice(gi * GDIM, (gi + 1) * GDIM)
            outs.append(_window_sums(qe_ref, tmp_refs, ts, cols, w, 1) - dp_ref[:, cols])
        out_ref[...] = jnp.concatenate(outs, axis=1).astype(BF16)

    prev, nxt = _halo_specs(ts, seq, D)
    row = pl.BlockSpec((ts, D), lambda i: (i, 0))
    return pl.pallas_call(
        body, grid=(n_tiles,), name="l0_pool_bwd",
        out_shape=jax.ShapeDtypeStruct(dz.shape, BF16),
        in_specs=[row, prev, nxt, pl.BlockSpec(memory_space=pl.ANY)],
        out_specs=pl.BlockSpec((ts, D), lambda i: (i, 3)),
        input_output_aliases={3: 0},
        scratch_shapes=_pool_scratch(ts),
        compiler_params=_params(32),
    )(dp, dp, dp, dz)


def _l0_in_proj_bwd(dz, w, x, g0, dx1):
    seq = x.shape[0]
    tm = 512

    def body(dz_ref, w_ref, x_ref, g_ref, dres_ref, dx_ref, dn_ref):
        @pl.when(pl.program_id(0) == 0)
        def _():
            dn_ref[...] = jnp.zeros_like(dn_ref)

        dh = _nt(dz_ref[...], w_ref[...])
        xf = x_ref[...]
        r = lax.rsqrt(jnp.mean(xf * xf, axis=1, keepdims=True) + EPS)
        xn = xf * r
        dn_ref[...] += jnp.sum(dh * xn, axis=0, keepdims=True)
        dxn = dh * g_ref[...]
        dx_ref[...] = dres_ref[...] + r * (dxn - xn * jnp.mean(dxn * xn, axis=1, keepdims=True))

    row = lambda w_: pl.BlockSpec((tm, w_), lambda i: (i, 0))
    return pl.pallas_call(
        body, grid=(seq // tm,), name="l0_in_proj_bwd",
        out_shape=(jax.ShapeDtypeStruct((seq, D), F32), jax.ShapeDtypeStruct((1, D), F32)),
        in_specs=[row(MIX0_IN), _resident((D, MIX0_IN)), row(D), _resident((1, D)), row(D)],
        out_specs=(row(D), pl.BlockSpec((1, D), lambda i: (0, 0))),
        compiler_params=_params(56),
    )(dz, w, x, g0, dx1)


def _dw_matmul(a_t, b, name, b_transposed=False, tn=1024, ts=1024, col_block=None):
    k, seq = a_t.shape
    n = b.shape[0] if b_transposed else b.shape[1]
    tn = min(n, tn)
    assert seq % ts == 0 and n % tn == 0 and (col_block is None or tn % col_block == 0)
    n_s = seq // ts
    per = 1 if col_block is None else tn // col_block

    def body(a_ref, b_ref, o_ref, ob_ref, acc_ref):
        s = pl.program_id(1)

        @pl.when(s == 0)
        def _():
            acc_ref[...] = jnp.zeros_like(acc_ref)

        acc_ref[...] += _nt(a_ref[...], b_ref[...]) if b_transposed else _mm(a_ref[...], b_ref[...])

        @pl.when(s == n_s - 1)
        def _():
            acc = acc_ref[...]
            if col_block is None:
                o_ref[...] = acc
                ob_ref[...] = acc.astype(BF16)
            else:
                for i in range(per):
                    piece = acc[:, i * col_block:(i + 1) * col_block]
                    o_ref[i] = piece
                    ob_ref[i] = piece.astype(BF16)

    b_spec = (pl.BlockSpec((tn, ts), lambda j, s: (j, s)) if b_transposed else pl.BlockSpec((ts, tn), lambda j, s: (s, j)))
    if col_block is None:
        shape, o_spec = (k, n), pl.BlockSpec((k, tn), lambda j, s: (0, j))
    else:
        shape, o_spec = (n // col_block, k, col_block), pl.BlockSpec((per, k, col_block), lambda j, s: (j, 0, 0))
    return pl.pallas_call(
        body, grid=(n // tn, n_s), name=name,
        out_shape=(jax.ShapeDtypeStruct(shape, F32), jax.ShapeDtypeStruct(shape, BF16)),
        in_specs=[pl.BlockSpec((k, ts), lambda j, s: (0, s)), b_spec],
        out_specs=(o_spec, o_spec),
        scratch_shapes=[pltpu.VMEM((k, tn), F32)],
        compiler_params=_params(56, 2),
    )(a_t, b)


ROW_TILES = 8


def _cast_shards(shards):
    n = len(shards)

    def body(*refs):
        for a in range(n):
            refs[n + a][...] = refs[a][...].astype(BF16)

    vm = pl.BlockSpec(memory_space=pltpu.VMEM)
    return pl.pallas_call(body, name="cast_weights", out_shape=[jax.ShapeDtypeStruct(t.shape, BF16) for t in shards],
                          in_specs=[vm] * n, out_specs=[vm] * n, compiler_params=_params(32, 0))(*shards)


def _adamw_math(w, g, m, v):
    m2 = ADAM_B1 * m + (1.0 - ADAM_B1) * g
    v2 = ADAM_B2 * v + (1.0 - ADAM_B2) * (g * g)
    m_hat = m2 / (1.0 - ADAM_B1 ** ADAM_STEP)
    v_hat = v2 / (1.0 - ADAM_B2 ** ADAM_STEP)
    delta = -ADAM_LR * (m_hat / (jnp.sqrt(v_hat) + ADAM_EPS) + ADAM_WD * w)
    return delta, m2, v2


def _final_sum_adamw(g_list, recv_list, me, w_list, m_list, v_list):
    n = len(w_list)

    def body(me_ref, *refs):
        own, recv, w, m, v = (refs[k * n:(k + 1) * n] for k in range(5))
        outs = [refs[(5 + k) * n:(6 + k) * n] for k in range(4)]
        for a in range(n):
            g = own[a][...]
            for k in range(N_DEV - 1):
                g = g + recv[a][k].astype(F32)
            delta, m2, v2 = _adamw_math(w[a][...], g, m[a][...], v[a][...])
            for o_ref, val in zip((outs[0][a], outs[1][a], outs[2][a], outs[3][a]), (g, delta, m2, v2)):
                o_ref[...] = val

    own_specs, flat, wire, shapes = [], [], [], []
    for t in w_list:
        rows, width = t.shape
        tr = rows // ROW_TILES
        own_specs.append(pl.BlockSpec((None, tr, width), lambda i, me: (me[0], i, 0)))
        flat.append(pl.BlockSpec((tr, width), lambda i, me: (i, 0)))
        wire.append(pl.BlockSpec((N_DEV - 1, tr, width), lambda i, me: (0, i, 0)))
        shapes.append(jax.ShapeDtypeStruct((rows, width), F32))
    out = pl.pallas_call(
        body, name="grad_sum_adamw", out_shape=shapes * 4,
        grid_spec=pltpu.PrefetchScalarGridSpec(
            num_scalar_prefetch=1, grid=(ROW_TILES,), in_specs=own_specs + wire + flat * 3, out_specs=flat * 4),
        compiler_params=_params(40),
    )(me, *g_list, *recv_list, *w_list, *m_list, *v_list)
    return [out[k * n:(k + 1) * n] for k in range(4)]


SMALL_NAMES = ("norm_0", "a_v_norm_0", "b_scale_0", "norm_1", "final_norm", "a_spatial_w_0", "a_spatial_b_0", "sink_1")
SMALL_VIEWS = ((8, LANES),) * 5 + ((4 * CHUNK, LANES), (4, LANES), (1, N_HEADS))
SMALL_ROW0 = (0, 8, 16, 24, 32, 40, 552, 560)
SMALL_ROWS = 568


def _small_sum_adamw(early, late, w_list, m_list, v_list):
    n = len(w_list)

    def body(e_ref, l_ref, *refs):
        gtot, first = e_ref[0], l_ref[0]
        for d in range(1, N_DEV):
            gtot = gtot + e_ref[d]
            first = first + l_ref[d]
        for a, ((rows, width), r0) in enumerate(zip(SMALL_VIEWS, SMALL_ROW0)):
            g = first if SMALL_NAMES[a] == "norm_0" else gtot[r0:r0 + rows, 0:width]
            delta, m2, v2 = _adamw_math(refs[a][...], g, refs[n + a][...], refs[2 * n + a][...])
            for k, val in enumerate((g, delta, m2, v2)):
                refs[(3 + k) * n + a][...] = val
        refs[7 * n][...] = gtot[LOSS_ROW:LOSS_ROW + 1, LOSS_LANE:LOSS_LANE + 1]

    vm = pl.BlockSpec(memory_space=pltpu.VMEM)
    shapes = [jax.ShapeDtypeStruct(s, F32) for s in SMALL_VIEWS]
    out = pl.pallas_call(
        body, name="small_sum_adamw", out_shape=shapes * 4 + [jax.ShapeDtypeStruct((1, 1), F32)],
        in_specs=[vm, vm] + [vm] * (3 * n), out_specs=[vm] * (4 * n + 1),
    )(early, late, *w_list, *m_list, *v_list)
    return [out[k * n:(k + 1) * n] for k in range(4)], out[4 * n]


def _all_gather_columns(blks):
    n = len(blks)

    def body(*refs):
        ins, outs = refs[:n], refs[n:2 * n]
        send_sems, recv_sems, local_sems = refs[2 * n:]
        x, y, c = lax.axis_index("x"), lax.axis_index("y"), lax.axis_index("c")
        me, sibling = (x, y, c), (x, y, 1 - c)
        chips = [(1 - x, y), (x, 1 - y), (1 - x, 1 - y)]

        def slot(a, px, py, pc):
            width = blks[a].shape[1]
            return outs[a].at[:, pl.ds(pl.multiple_of((4 * px + 2 * py + pc) * width, LANES), width)]

        def copy(k, a, block, to, from_input=False):
            return pltpu.make_async_remote_copy(
                src_ref=ins[a] if from_input else slot(a, *block), dst_ref=slot(a, *block),
                send_sem=send_sems.at[k, a], recv_sem=recv_sems.at[k, a], device_id=to, device_id_type=MESH)

        mine = [pltpu.make_async_copy(ins[a], slot(a, *me), local_sems.at[a]) for a in range(n)]
        first = []
        for a in range(n):
            first.append(copy(0, a, me, sibling, from_input=True))
            first += [copy(1 + j, a, me, (*chip, c), from_input=True) for j, chip in enumerate(chips)]
        for cp in mine + first:
            cp.start()
        passed = []
        for j, chip in enumerate(chips):
            for a in range(n):
                copy(1 + j, a, (*chip, c), me).wait_recv()
                passed.append(copy(4 + j, a, (*chip, c), sibling))
                passed[-1].start()
        for a in range(n):
            copy(0, a, sibling, me).wait_recv()
        for j, chip in enumerate(chips):
            for a in range(n):
                copy(4 + j, a, (*chip, 1 - c), me).wait_recv()
        for cp in first + passed:
            cp.wait_send()
        for cp in mine:
            cp.wait()

    any_spec = pl.BlockSpec(memory_space=pl.ANY)
    return pl.pallas_call(
        body, name="weights_all_gather",
        out_shape=[jax.ShapeDtypeStruct((t.shape[0], N_DEV * t.shape[1]), t.dtype) for t in blks],
        in_specs=[any_spec] * n, out_specs=[any_spec] * n,
        scratch_shapes=[pltpu.SemaphoreType.DMA((7, n)), pltpu.SemaphoreType.DMA((7, n)), pltpu.SemaphoreType.DMA((n,))],
    )(*blks)


PEER_FLIPS = tuple((fx, fy, fc) for fx in (0, 1) for fy in (0, 1) for fc in (0, 1))[1:]


def _sequencer_all_gather(blks, name, collective_id, concat_rows=False):
    n = len(blks)

    def body(*refs):
        ins, outs = refs[:n], refs[n:2 * n]
        send_sems, recv_sems, local_sems = refs[2 * n:]
        x, y, c = lax.axis_index("x"), lax.axis_index("y"), lax.axis_index("c")
        peers = [(x ^ fx, y ^ fy, c ^ fc) for fx, fy, fc in PEER_FLIPS]
        barrier = pltpu.get_barrier_semaphore()
        for peer in peers:
            pl.semaphore_signal(barrier, inc=1, device_id=peer, device_id_type=MESH)
        pl.semaphore_wait(barrier, len(peers))
        me = 4 * x + 2 * y + c

        def slot(a):
            rows = blks[a].shape[0]
            return outs[a].at[pl.ds(pl.multiple_of(me * rows, 16), rows)] if concat_rows else outs[a].at[me]

        copies = [pltpu.make_async_remote_copy(
            src_ref=ins[a], dst_ref=slot(a), send_sem=send_sems.at[k, a], recv_sem=recv_sems.at[k, a],
            device_id=peer, device_id_type=MESH) for k, peer in enumerate(peers) for a in range(n)]
        mine = [pltpu.make_async_copy(ins[a], slot(a), local_sems.at[a]) for a in range(n)]
        for cp in copies + mine:
            cp.start()
        for cp in copies + mine:
            cp.wait()

    out_shape = (lambda t: (N_DEV * t.shape[0],) + t.shape[1:]) if concat_rows else (lambda t: (N_DEV,) + t.shape)
    return pl.kernel(
        body, out_type=[jax.ShapeDtypeStruct(out_shape(t), t.dtype) for t in blks],
        mesh=plsc.ScalarSubcoreMesh(axis_name="sequencer", num_cores=1), name=name,
        scratch_types=[pltpu.SemaphoreType.DMA((7, n)), pltpu.SemaphoreType.DMA((7, n)), pltpu.SemaphoreType.DMA((n,))],
        compiler_params=pltpu.CompilerParams(collective_id=collective_id),
    )(*blks)


def _sequencer_scatter(g_list, name, collective_id):
    n = len(g_list)

    def body(*refs):
        ins, outs = refs[:n], refs[n:2 * n]
        send_sems, recv_sems = refs[2 * n:]
        x, y, c = lax.axis_index("x"), lax.axis_index("y"), lax.axis_index("c")
        peers = [(x ^ fx, y ^ fy, c ^ fc) for fx, fy, fc in PEER_FLIPS]
        barrier = pltpu.get_barrier_semaphore()
        for peer in peers:
            pl.semaphore_signal(barrier, inc=1, device_id=peer, device_id_type=MESH)
        pl.semaphore_wait(barrier, len(peers))
        copies = [pltpu.make_async_remote_copy(
            src_ref=ins[a].at[4 * px + 2 * py + pc], dst_ref=outs[a].at[k], send_sem=send_sems.at[k, a],
            recv_sem=recv_sems.at[k, a], device_id=(px, py, pc), device_id_type=MESH)
            for k, (px, py, pc) in enumerate(peers) for a in range(n)]
        for cp in copies:
            cp.start()
        for cp in copies:
            cp.wait()

    return pl.kernel(
        body, out_type=[jax.ShapeDtypeStruct((N_DEV - 1,) + g.shape[1:], g.dtype) for g in g_list],
        mesh=plsc.ScalarSubcoreMesh(axis_name="sequencer", num_cores=1), name=name,
        scratch_types=[pltpu.SemaphoreType.DMA((7, n)), pltpu.SemaphoreType.DMA((7, n))],
        compiler_params=pltpu.CompilerParams(collective_id=collective_id),
    )(*g_list)


def _direct_all_gather(blk, name):
    def body(g_ref, out_ref, send_sems, recv_sems, local_sem):
        x, y, c = lax.axis_index("x"), lax.axis_index("y"), lax.axis_index("c")
        me = 4 * x + 2 * y + c
        copies = [pltpu.make_async_remote_copy(
            src_ref=g_ref, dst_ref=out_ref.at[me], send_sem=send_sems.at[k], recv_sem=recv_sems.at[k],
            device_id=(x ^ fx, y ^ fy, c ^ fc), device_id_type=MESH) for k, (fx, fy, fc) in enumerate(PEER_FLIPS)]
        copies.append(pltpu.make_async_copy(g_ref, out_ref.at[me], local_sem))
        for cp in copies:
            cp.start()
        for cp in copies:
            cp.wait()

    any_spec = pl.BlockSpec(memory_space=pl.ANY)
    return pl.pallas_call(
        body, name=name, out_shape=jax.ShapeDtypeStruct((N_DEV,) + blk.shape, blk.dtype),
        in_specs=[any_spec], out_specs=any_spec,
        scratch_shapes=[pltpu.SemaphoreType.DMA((7,)), pltpu.SemaphoreType.DMA((7,)), pltpu.SemaphoreType.DMA],
    )(blk)


def _shard_views(w_in_0, b_group_w_0, w_out_0, w_in_1, w_out_1):
    return [w_in_0, b_group_w_0.reshape(4 * 32, GDIM), w_out_0, w_in_1, w_out_1]


def _small_views(named):
    return [named[name].reshape(view) for name, view in zip(SMALL_NAMES, SMALL_VIEWS)]


LOSS_ROW, LOSS_LANE = 560, N_HEADS


def _pack_small_grads(named, loss_part):
    rows = []
    for name, (r, w) in zip(SMALL_NAMES, SMALL_VIEWS):
        pad_r = -r % 8
        if name == "sink_1":
            t = jnp.concatenate([named[name].reshape(r, w), loss_part], axis=1)
            rows.append(jnp.pad(t, ((0, pad_r), (0, LANES - w - 1))))
        elif name in named:
            rows.append(jnp.pad(named[name].reshape(r, w), ((0, pad_r), (0, LANES - w))))
        else:
            rows.append(jnp.zeros((r + pad_r, LANES), F32))
    return jnp.concatenate(rows, axis=0)


def _device_blocks(t, axis):
    shape = t.shape
    t = t.reshape(shape[:axis] + (N_DEV, shape[axis] // N_DEV) + shape[axis + 1:])
    t = jnp.moveaxis(t, axis, 0)
    return t.reshape(N_DEV, -1, shape[-1] if axis != len(shape) - 1 else shape[-1] // N_DEV)


def kernel(x, norm_0, w_in_0, a_v_norm_0, a_spatial_w_0, a_spatial_b_0, b_group_w_0, b_scale_0, w_out_0, norm_1, w_in_1, sink_1, w_out_1, final_norm, loss_target, m_norm_0, m_w_in_0, m_a_v_norm_0, m_a_spatial_w_0, m_a_spatial_b_0, m_b_group_w_0, m_b_scale_0, m_w_out_0, m_norm_1, m_w_in_1, m_sink_1, m_w_out_1, m_final_norm, v_norm_0, v_w_in_0, v_a_v_norm_0, v_a_spatial_w_0, v_a_spatial_b_0, v_b_group_w_0, v_b_scale_0, v_w_out_0, v_norm_1, v_w_in_1, v_sink_1, v_w_out_1, v_final_norm):
    seq = x.shape[1]
    xs = x.reshape(seq, D)
    tgt = loss_target.reshape(seq, D)
    ax, ay, ac = lax.axis_index("x"), lax.axis_index("y"), lax.axis_index("c")
    me = jnp.reshape(4 * ax + 2 * ay + ac, (1,)).astype(jnp.int32)

    shards = _shard_views(w_in_0, b_group_w_0, w_out_0, w_in_1, w_out_1)
    cast = _cast_shards([shards[0], shards[1], shards[2], w_in_1.T, shards[4]])
    win0 = _all_gather_columns(cast[0:1])[0]
    rest, win0 = lax.optimization_barrier((cast[1:5], win0))
    g_wg, wout0 = _sequencer_all_gather(rest[0:2], "weights_gather_a", 1, concat_rows=True)
    win1_t, wout1 = _sequencer_all_gather(rest[2:4], "weights_gather_b", 2, concat_rows=True)

    blocks, received, early = {}, {}, {}
    collective_ids = {"l1": 3, "out0": 4, "in0": 5}

    def scatter(tag, own_blocks, wire_blocks):
        blocks[tag] = own_blocks
        received[tag] = _sequencer_scatter(wire_blocks, "grad_scatter_" + tag, collective_ids[tag])

    def small_early(named, loss_part):
        early["small"] = _sequencer_all_gather([_pack_small_grads(named, loss_part)], "small_grad_gather", 6)[0]

    grad_x, d_norm_0 = _local_step(xs, tgt, win0, g_wg, wout0, win1_t, wout1, norm_0, a_v_norm_0, a_spatial_w_0,
                                   a_spatial_b_0, b_scale_0, norm_1, sink_1, final_norm, scatter, small_early)

    order = (("in0", 0), ("in0", 1), ("out0", 0), ("l1", 0), ("l1", 1))
    late = _direct_all_gather(d_norm_0.reshape(8, LANES), "norm_grad_gather")
    shards_late, _ = lax.optimization_barrier((shards, grad_x))
    big = _final_sum_adamw([blocks[t][i] for t, i in order], [received[t][i] for t, i in order], me, shards_late,
                           _shard_views(m_w_in_0, m_b_group_w_0, m_w_out_0, m_w_in_1, m_w_out_1),
                           _shard_views(v_w_in_0, v_b_group_w_0, v_w_out_0, v_w_in_1, v_w_out_1))
    weights = dict(norm_0=norm_0, a_v_norm_0=a_v_norm_0, a_spatial_w_0=a_spatial_w_0, a_spatial_b_0=a_spatial_b_0,
                   b_scale_0=b_scale_0, norm_1=norm_1, sink_1=sink_1, final_norm=final_norm)
    m_small = dict(norm_0=m_norm_0, a_v_norm_0=m_a_v_norm_0, a_spatial_w_0=m_a_spatial_w_0, a_spatial_b_0=m_a_spatial_b_0,
                   b_scale_0=m_b_scale_0, norm_1=m_norm_1, sink_1=m_sink_1, final_norm=m_final_norm)
    v_small = dict(norm_0=v_norm_0, a_v_norm_0=v_a_v_norm_0, a_spatial_w_0=v_a_spatial_w_0, a_spatial_b_0=v_a_spatial_b_0,
                   b_scale_0=v_b_scale_0, norm_1=v_norm_1, sink_1=v_sink_1, final_norm=v_final_norm)
    small, loss = _small_sum_adamw(early["small"], late, _small_views(weights), _small_views(m_small),
                                   _small_views(v_small))

    def in_order(kind):
        b = [b_.reshape(s_.shape) for b_, s_ in zip(big[kind], (w_in_0, b_group_w_0, w_out_0, w_in_1, w_out_1))]
        s = {name: t.reshape(weights[name].shape) for name, t in zip(SMALL_NAMES, small[kind])}
        return [s["norm_0"], b[0], s["a_v_norm_0"], s["a_spatial_w_0"], s["a_spatial_b_0"], b[1], s["b_scale_0"], b[2],
                s["norm_1"], b[3], s["sink_1"], b[4], s["final_norm"]]

    return (loss[0, 0], grad_x.reshape(1, seq, D), *in_order(0), *in_order(1), *in_order(2), *in_order(3))


def _local_step(xs, tgt, win0, g_wg, wout0, win1_t, wout1, norm_0, a_v_norm_0, a_spatial_w_0, a_spatial_b_0, b_scale_0,
                norm_1, sink_1, final_norm, scatter, small_early):
    seq = xs.shape[0]
    ws = a_spatial_w_0.astype(BF16)
    ws_t = jnp.swapaxes(ws, 1, 2)
    bias = jnp.repeat(a_spatial_b_0.T, GDIM, axis=1)
    g0, gv, scale, g1, gf = (t.reshape(1, D) for t in (norm_0, a_v_norm_0, b_scale_0, norm_1, final_norm))
    cos_t, sin_t = _rope_tables_t(seq)

    za, bx, bg, h0_t = _l0_in_proj(xs, g0, win0)
    g_wg, wout0, za = lax.optimization_barrier((g_wg, wout0, za))
    wg = g_wg.reshape(N_DEV, 4, 32, GDIM).transpose(1, 0, 2, 3).reshape(4, GDIM, GDIM)
    wg_t = jnp.swapaxes(wg, 1, 2)
    x1 = _l0_mix_fwd(za, bx, bg, xs, ws, bias, gv, wg, scale, wout0)
    win1_t, wout1, x1 = lax.optimization_barrier((win1_t, wout1, x1))
    qt, kt, vt, gatet, h1_t = _l1_in_proj(x1, g1, win1_t, cos_t, sin_t)
    dx2, dx2b, att, lse, loss_part, d_gf, d_wout1, d_wout1_wire = _l1_attn_fwd(
        qt, kt, vt, gatet, x1, tgt, wout1, gf, sink_1)

    dq_r, dgate, dk_pad, dv_pad, d_sink = _l1_attn_bwd(dx2b, wout1, qt, kt, vt, gatet, att, lse, sink_1)
    dk_r = dk_pad[:, BLK:BLK + seq]
    dv = dv_pad[:, BLK:BLK + seq]
    dx1, dx1b, dz1_t, d_g1 = _l1_in_proj_bwd(dq_r, dk_r, dv, dgate, cos_t, sin_t, win1_t, x1, g1, dx2)
    d_win1, d_win1_wire = _dw_matmul(h1_t, dz1_t, "dw_in_1", b_transposed=True, tn=1280, col_block=MIX1_IN // N_DEV)
    rows = lambda t: t.reshape(N_DEV, t.shape[0] // N_DEV, t.shape[1])
    scatter("l1", [d_win1, rows(d_wout1)], [d_win1_wire, rows(d_wout1_wire)])

    dz0, dp, cat_t, d_ws, _, d_gv, d_scale, d_wg, d_b = _l0_mix_bwd(
        dx1b, wout0, za, bx, bg, ws, ws_t, bias, gv, wg, wg_t, scale)
    dz0 = _l0_pool_bwd(dp, dz0)
    d_win0, d_win0_wire = _dw_matmul(h0_t, dz0, "dw_in_0", tn=1280, col_block=MIX0_IN // N_DEV)
    d_wg_blocks = _device_blocks(d_wg, 1)
    scatter("in0", [d_win0, d_wg_blocks], [d_win0_wire, d_wg_blocks])
    cat_t, _ = lax.optimization_barrier((cat_t, d_win0))
    d_wout0, d_wout0_wire = _dw_matmul(cat_t, dx1b, "dw_out_0")
    scatter("out0", [rows(d_wout0)], [rows(d_wout0_wire)])
    small_early(dict(a_v_norm_0=d_gv, a_spatial_w_0=d_ws, a_spatial_b_0=d_b.reshape(4, 8, CHUNK)[:, 0, :],
                     b_scale_0=d_scale, norm_1=d_g1, sink_1=d_sink[:, 0], final_norm=d_gf), loss_part)
    dz0, _ = lax.optimization_barrier((dz0, d_wout0))
    return _l0_in_proj_bwd(dz0, win0, xs, g0, dx1)
```

```python
import jax
import jax.numpy as jnp
from jax import lax
from jax.experimental import pallas as pl
from jax.experimental.pallas import tpu as pltpu
from jax.experimental.pallas import tpu_sc as plsc

F32 = jnp.float32
BF16 = jnp.bfloat16

D = 1024
EPS = 1e-6
NEG_INF = -1e30
CHUNK = 128
A_GROUPS = 4
POOL_WINDOWS = (2, 4, 8, 16)
POOL_HALO = 8
GDIM = 256
N_HEADS = 16
N_KV = 4
GQA = 4
HD = 64
BLK = 128
ROT_HALF = 8
ROPE_THETA = 500000.0
SCALE = HD ** -0.5
MIX0_IN = 5 * D
MIX1_IN = 2560
KV_W = N_KV * HD
Q_ROWS, K_ROWS, V_ROWS, G_ROWS = (0, D), (D, D + KV_W), (D + KV_W, D + 2 * KV_W), (D + 2 * KV_W, MIX1_IN)
TQ = 512

ADAM_LR = 0.001
ADAM_B1 = 0.9
ADAM_B2 = 0.999
ADAM_EPS = 1e-08
ADAM_WD = 0.01
ADAM_STEP = 10

N_DEV = 8
LANES = 128
MIB = 2 ** 20
MESH = pl.DeviceIdType.MESH


def _params(limit_mib, n_axes=1):
    return pltpu.CompilerParams(vmem_limit_bytes=limit_mib * MIB, dimension_semantics=("arbitrary",) * n_axes)


def _resident(shape):
    nd = len(shape)
    return pl.BlockSpec(shape, lambda *_: (0,) * nd, pipeline_mode=pl.Buffered(1))


def _gelu(x):
    k = 0.7978845608028654
    return 0.5 * x * (1.0 + jnp.tanh(k * (x + 0.044715 * x * x * x)))


def _gelu_and_grad(x):
    k = 0.7978845608028654
    x2 = x * x
    t = jnp.tanh(k * (x + 0.044715 * x * x2))
    g = 0.5 * x * (1.0 + t)
    dg = 0.5 * (1.0 + t) + 0.5 * x * (1.0 - t * t) * (k * (1.0 + 3.0 * 0.044715 * x2))
    return g, dg


def _silu_and_grad(x):
    s = jax.nn.sigmoid(x)
    return x * s, s * (1.0 + x * (1.0 - s))


def _nt(a, b):
    return lax.dot_general(a, b, (((1,), (1,)), ((), ())), preferred_element_type=F32)


def _tn(a, b):
    return lax.dot_general(a, b, (((0,), (0,)), ((), ())), preferred_element_type=F32)


def _mm(a, b):
    return jnp.dot(a, b, preferred_element_type=F32)


def _rope_tables_t(seq):
    inv = ROPE_THETA ** (-jnp.arange(0, 2 * ROT_HALF, 2, dtype=F32) / (2 * ROT_HALF))
    ang = inv[:, None] * jnp.arange(seq, dtype=F32)[None, :]
    return jnp.cos(ang), jnp.sin(ang)


def _rope_t(z, c, s, n_heads, sign):
    parts = []
    for h in range(n_heads):
        b = h * HD
        x1, x2 = z[b:b + ROT_HALF], z[b + ROT_HALF:b + 2 * ROT_HALF]
        if sign > 0:
            parts += [x1 * c - x2 * s, x2 * c + x1 * s]
        else:
            parts += [x1 * c + x2 * s, x2 * c - x1 * s]
        parts.append(z[b + 2 * ROT_HALF:b + HD])
    return jnp.concatenate(parts, axis=0)


def _l0_in_proj(x, g0, w):
    seq = x.shape[0]
    tm = 512

    def body(x_ref, g_ref, w_ref, za_ref, bx_ref, bg_ref, ht_ref):
        xf = x_ref[...]
        r = lax.rsqrt(jnp.mean(xf * xf, axis=1, keepdims=True) + EPS)
        h = (xf * r * g_ref[...]).astype(BF16)
        ht_ref[...] = h.T
        for j in range(3):
            za_ref[:, j * D:(j + 1) * D] = _mm(h, w_ref[:, j * D:(j + 1) * D]).astype(BF16)
        bx_ref[...] = _mm(h, w_ref[:, 3 * D:4 * D])
        bg_ref[...] = _mm(h, w_ref[:, 4 * D:5 * D]).astype(BF16)

    return pl.pallas_call(
        body, grid=(seq // tm,), name="l0_in_proj",
        out_shape=(jax.ShapeDtypeStruct((seq, 3 * D), BF16), jax.ShapeDtypeStruct((seq, D), F32),
                   jax.ShapeDtypeStruct((seq, D), BF16), jax.ShapeDtypeStruct((D, seq), BF16)),
        in_specs=[pl.BlockSpec((tm, D), lambda i: (i, 0)), _resident((1, D)), _resident((D, MIX0_IN))],
        out_specs=(pl.BlockSpec((tm, 3 * D), lambda i: (i, 0)), pl.BlockSpec((tm, D), lambda i: (i, 0)),
                   pl.BlockSpec((tm, D), lambda i: (i, 0)), pl.BlockSpec((D, tm), lambda i: (0, i))),
        compiler_params=_params(48),
    )(x, g0, w)


POOL_EXT = 40


def _fill_halo(ext_ref, cur, prev_ref, next_ref, i, n_tiles, ts):
    ext_ref[pl.ds(0, POOL_HALO), :] = jnp.where(i > 0, prev_ref[...], 0.0)
    ext_ref[pl.ds(POOL_HALO, ts), :] = cur
    ext_ref[pl.ds(POOL_HALO + ts, POOL_HALO), :] = jnp.where(i < n_tiles - 1, next_ref[...], 0.0)
    ext_ref[pl.ds(2 * POOL_HALO + ts, POOL_EXT - 2 * POOL_HALO), :] = jnp.zeros((POOL_EXT - 2 * POOL_HALO, D), F32)


def _window_sums(src_ref, tmp_refs, ts, cols, w, shift):
    if w == 2:
        return src_ref[pl.ds(POOL_HALO - 1 + shift, ts), cols] + src_ref[pl.ds(POOL_HALO + shift, ts), cols]
    d2, d4, d8 = tmp_refs
    n2, n4, n8 = ts + 32, ts + 24, ts + 16
    d2[pl.ds(0, n2), :] = src_ref[pl.ds(0, n2), cols] + src_ref[pl.ds(1, n2), cols]
    if w == 4:
        return d2[pl.ds(POOL_HALO - 2 + shift, ts), :] + d2[pl.ds(POOL_HALO + shift, ts), :]
    d4[pl.ds(0, n4), :] = d2[pl.ds(0, n4), :] + d2[pl.ds(2, n4), :]
    if w == 8:
        return d4[pl.ds(POOL_HALO - 4 + shift, ts), :] + d4[pl.ds(POOL_HALO + shift, ts), :]
    d8[pl.ds(0, n8), :] = d4[pl.ds(0, n8), :] + d4[pl.ds(4, n8), :]
    return d8[pl.ds(shift, ts), :] + d8[pl.ds(POOL_HALO + shift, ts), :]


def _pool_scratch(ts):
    return [pltpu.VMEM((ts + POOL_EXT, D), F32)] + [pltpu.VMEM((ts + POOL_EXT, GDIM), F32)] * 3


def _pool_forward(xe_ref, tmp_refs, ts, t0, seq):
    tg = t0 + lax.broadcasted_iota(jnp.int32, (ts, 1), 0)
    outs = []
    for gi, w in enumerate(POOL_WINDOWS):
        hw = w // 2
        cols = slice(gi * GDIM, (gi + 1) * GDIM)
        cnt = (jnp.minimum(tg + hw, seq) - jnp.maximum(tg - hw, 0)).astype(F32)
        outs.append(_window_sums(xe_ref, tmp_refs, ts, cols, w, 0) / cnt - xe_ref[pl.ds(POOL_HALO, ts), cols])
    return jnp.concatenate(outs, axis=1)


def _spatial_mix(ws_ref, vnb, bias, ts):
    rows = []
    for c in range(ts // CHUNK):
        vc = vnb[c * CHUNK:(c + 1) * CHUNK, :]
        rows.append(jnp.concatenate(
            [_mm(ws_ref[h], vc[:, h * GDIM:(h + 1) * GDIM]) for h in range(A_GROUPS)], axis=1) + bias)
    return jnp.concatenate(rows, axis=0)


def _halo_specs(ts, seq, width):
    per = ts // POOL_HALO
    last = seq // POOL_HALO - 1
    prev = pl.BlockSpec((POOL_HALO, width), lambda i: (jnp.maximum(i * per - 1, 0), 0))
    nxt = pl.BlockSpec((POOL_HALO, width), lambda i: (jnp.minimum((i + 1) * per, last), 0))
    return prev, nxt


def _l0_mix_fwd(za, bx, bg, x, ws, bias, gv, wg, scale, wout):
    seq = x.shape[0]
    ts = 512
    n_tiles = seq // ts

    def body(za_ref, bx_ref, bxp_ref, bxn_ref, bg_ref, x_ref, ws_ref, bias_ref, gv_ref, wg_ref, sc_ref, wo_ref,
             x1_ref, xe_ref, *tmp_refs):
        i = pl.program_id(0)
        vg = _gelu(za_ref[:, D:2 * D].astype(F32))
        rv = lax.rsqrt(jnp.mean(vg * vg, axis=1, keepdims=True) + EPS)
        vnb = (vg * rv * gv_ref[...]).astype(BF16)
        mixed = _spatial_mix(ws_ref, vnb, bias_ref[...], ts)

        _fill_halo(xe_ref, bx_ref[...], bxp_ref, bxn_ref, i, n_tiles, ts)
        pb = _pool_forward(xe_ref, tmp_refs, ts, i * ts, seq).astype(BF16)
        ypre = jnp.concatenate([_mm(pb[:, g * GDIM:(g + 1) * GDIM], wg_ref[g]) for g in range(4)], axis=1)

        u = _gelu(za_ref[:, 0:D].astype(F32))
        ag = za_ref[:, 2 * D:3 * D].astype(F32)
        ya = (u * mixed * (ag * jax.nn.sigmoid(ag))).astype(BF16)
        out_a = _mm(ya, wo_ref[0:D, :])

        bgf = bg_ref[...].astype(F32)
        yb = (ypre * sc_ref[...] * (bgf * jax.nn.sigmoid(bgf))).astype(BF16)
        x1_ref[...] = x_ref[...] + out_a + _mm(yb, wo_ref[D:2 * D, :])

    prev, nxt = _halo_specs(ts, seq, D)
    row = lambda w: pl.BlockSpec((ts, w), lambda i: (i, 0))
    return pl.pallas_call(
        body, grid=(n_tiles,), name="l0_mix_fwd",
        out_shape=jax.ShapeDtypeStruct((seq, D), F32),
        in_specs=[row(3 * D), row(D), prev, nxt, row(D), row(D), _resident((4, CHUNK, CHUNK)), _resident((CHUNK, D)),
                  _resident((1, D)), _resident((4, GDIM, GDIM)), _resident((1, D)), _resident((2 * D, D))],
        out_specs=row(D),
        scratch_shapes=_pool_scratch(ts),
        compiler_params=_params(56),
    )(za, bx, bx, bx, bg, x, ws, bias, gv, wg, scale, wout)


def _l1_in_proj(x1, g1, w_t, cos_t, sin_t):
    seq = x1.shape[0]
    tm = 512

    def body(x_ref, g_ref, wt_ref, c_ref, s_ref, q_ref, k_ref, v_ref, gate_ref, ht_ref):
        xf = x_ref[...]
        r = lax.rsqrt(jnp.mean(xf * xf, axis=1, keepdims=True) + EPS)
        ht = (xf * r * g_ref[...]).astype(BF16).T
        ht_ref[...] = ht
        c, s = c_ref[...], s_ref[...]
        q_ref[...] = (_rope_t(_mm(wt_ref[Q_ROWS[0]:Q_ROWS[1], :], ht), c, s, N_HEADS, 1) * SCALE).astype(BF16)
        k_ref[...] = _rope_t(_mm(wt_ref[K_ROWS[0]:K_ROWS[1], :], ht), c, s, N_KV, 1).astype(BF16)
        v_ref[...] = _mm(wt_ref[V_ROWS[0]:V_ROWS[1], :], ht).astype(BF16)
        gate_ref[...] = _mm(wt_ref[G_ROWS[0]:G_ROWS[1], :], ht).astype(BF16)

    col = lambda rows: pl.BlockSpec((rows, tm), lambda i: (0, i))
    return pl.pallas_call(
        body, grid=(seq // tm,), name="l1_in_proj",
        out_shape=(jax.ShapeDtypeStruct((D, seq), BF16), jax.ShapeDtypeStruct((KV_W, seq), BF16),
                   jax.ShapeDtypeStruct((KV_W, seq), BF16), jax.ShapeDtypeStruct((D, seq), BF16),
                   jax.ShapeDtypeStruct((D, seq), BF16)),
        in_specs=[pl.BlockSpec((tm, D), lambda i: (i, 0)), _resident((1, D)), _resident((MIX1_IN, D)), col(ROT_HALF),
                  col(ROT_HALF)],
        out_specs=(col(D), col(KV_W), col(KV_W), col(D), col(D)),
        compiler_params=_params(48),
    )(x1, g1, w_t, cos_t, sin_t)


def _band_specs_t(nb, clamp_i):
    per = TQ // BLK
    prev = pl.BlockSpec((KV_W, BLK), lambda i: (0, jnp.maximum(clamp_i(i) * per - 1, 0)))
    cur = pl.BlockSpec((KV_W, TQ), lambda i: (0, clamp_i(i)))
    nxt = pl.BlockSpec((KV_W, BLK), lambda i: (0, jnp.minimum((clamp_i(i) + 1) * per, nb - 1)))
    return [prev, cur, nxt]


def _fill_band(buf, p_ref, c_ref, n_ref):
    buf[:, 0:BLK] = p_ref[...]
    buf[:, BLK:BLK + TQ] = c_ref[...]
    buf[:, BLK + TQ:2 * BLK + TQ] = n_ref[...]


def _band_bias_t(n, nb):
    c = lax.broadcasted_iota(jnp.int32, (BLK, BLK), 0)
    r = lax.broadcasted_iota(jnp.int32, (BLK, BLK), 1)
    first = jnp.where((c >= r) & (n > 0), 0.0, NEG_INF).astype(F32)
    last = jnp.where((c <= r) & (n < nb - 1), 0.0, NEG_INF).astype(F32)
    return jnp.concatenate([first] * HPP, axis=1), jnp.concatenate([last] * HPP, axis=1)


def _masked(st, bias):
    first, last = bias
    return jnp.concatenate([st[0:BLK] + first, st[BLK:2 * BLK], st[2 * BLK:3 * BLK] + last], axis=0)


AUG = 16


def _ones_rows(n_ones, width):
    return (lax.broadcasted_iota(jnp.int32, (AUG, width), 0) < n_ones).astype(BF16)


def _minus_rows(vec):
    hi = vec.astype(BF16).astype(F32)
    lo = vec - hi
    return jnp.concatenate([-hi, -lo, jnp.zeros((AUG - 2, vec.shape[1]), F32)], axis=0).astype(BF16)


HPP = GQA
FWD_GROUP, BWD_GROUP = 2, 1
BWD_AHEAD = 1


def _heads_t(ref, h0, c0):
    return jnp.concatenate([ref[(h0 + g) * HD:(h0 + g + 1) * HD, c0:c0 + BLK] for g in range(HPP)], axis=1)


def _row4(ref, h0, c0):
    return jnp.concatenate([ref[h0 + g:h0 + g + 1, c0:c0 + BLK] for g in range(HPP)], axis=1)


def _sink_row(sink_ref, h0):
    return jnp.concatenate([jnp.full((1, BLK), sink_ref[h0 + g], F32) for g in range(HPP)], axis=1)


def _l1_attn_fwd(qt, kt, vt, gatet, x1, tgt, wout, gf, sink):
    seq = x1.shape[0]
    nq, nb = seq // TQ, seq // BLK

    def body(q_ref, gate_ref, kp_ref, k_ref, kn_ref, vp_ref, v_ref, vn_ref, x1_ref, tgt_ref, wo_ref, gf_ref, sink_ref,
             dx2_ref, dx2b_ref, att_ref, lse_ref, loss_ref, dgf_ref, dwo_ref, dwo_wire_ref, kbuf, vbuf, att_scr):
        i = pl.program_id(0)

        @pl.when(i == 0)
        def _():
            loss_ref[...] = jnp.zeros_like(loss_ref)
            dgf_ref[...] = jnp.zeros_like(dgf_ref)
            dwo_ref[...] = jnp.zeros_like(dwo_ref)

        _fill_band(kbuf, kp_ref, k_ref, kn_ref)
        _fill_band(vbuf, vp_ref, v_ref, vn_ref)
        ones_row = _ones_rows(1, 3 * BLK)
        groups = [list(range(0, N_HEADS, HPP))[g:g + FWD_GROUP] for g in range(0, N_HEADS // HPP, FWD_GROUP)]
        work = [(j, grp) for j in range(TQ // BLK) for grp in groups]

        def scores(j, passes):
            c0 = j * BLK
            bias = _band_bias_t(i * (TQ // BLK) + j, nb)
            st = dict(c0=c0, passes=passes)
            st["kv_rows"] = [slice(h0 // GQA * HD, (h0 // GQA + 1) * HD) for h0 in passes]
            st["sts"] = [_masked(_tn(kbuf[rows, c0:c0 + 3 * BLK], _heads_t(q_ref, h0, c0)), bias)
                         for h0, rows in zip(passes, st["kv_rows"])]
            return st

        def softmaxes(st):
            st["sks"] = [_sink_row(sink_ref, h0) for h0 in st["passes"]]
            st["ms"] = [jnp.maximum(jnp.max(s_, axis=0, keepdims=True), sk) for s_, sk in zip(st["sts"], st["sks"])]
            st["ps"] = [jnp.exp(s_ - m).astype(BF16) for s_, m in zip(st["sts"], st["ms"])]

        def values(st):
            c0, passes = st["c0"], st["passes"]
            pvs = [_mm(jnp.concatenate([vbuf[rows, c0:c0 + 3 * BLK], ones_row], axis=0), p)
                   for rows, p in zip(st["kv_rows"], st["ps"])]
            lse_rows = []
            for h0, pv, m, sk in zip(passes, pvs, st["ms"], st["sks"]):
                den = pv[HD:HD + 1, :] + jnp.exp(sk - m)
                ot = pv[0:HD, :] / den
                lse = m + jnp.log(den)
                for g in range(HPP):
                    h = h0 + g
                    att_scr[h * HD:(h + 1) * HD, c0:c0 + BLK] = ot[:, g * BLK:(g + 1) * BLK]
                    lse_rows.append(lse[:, g * BLK:(g + 1) * BLK])
            lse_ref[passes[0]:passes[0] + len(lse_rows), c0:c0 + BLK] = jnp.concatenate(lse_rows, axis=0)

        state = scores(*work[0])
        for nxt in work[1:] + [None]:
            following = scores(*nxt) if nxt is not None else None
            softmaxes(state)
            values(state)
            state = following

        att = att_scr[...]
        gate = gate_ref[...].astype(F32)
        yt = (att * (gate * jax.nn.sigmoid(gate))).astype(BF16)
        att_ref[...] = att.astype(BF16)
        x2 = x1_ref[...] + _mm(yt.T, wo_ref[...])
        r = lax.rsqrt(jnp.mean(x2 * x2, axis=1, keepdims=True) + EPS)
        xn = x2 * r
        diff = xn * gf_ref[...] - tgt_ref[...]
        loss_ref[...] += 0.5 * jnp.sum(jnp.mean(diff * diff, axis=1, keepdims=True), axis=0, keepdims=True)
        dout = diff * (1.0 / D)
        dgf_ref[...] += jnp.sum(dout * xn, axis=0, keepdims=True)
        dxn = dout * gf_ref[...]
        dx2 = r * (dxn - xn * jnp.mean(dxn * xn, axis=1, keepdims=True))
        dx2_ref[...] = dx2
        dx2b = dx2.astype(BF16)
        dx2b_ref[...] = dx2b
        dwo_ref[...] += _mm(yt, dx2b)

        @pl.when(i == nq - 1)
        def _():
            dwo_wire_ref[...] = dwo_ref[...].astype(BF16)

    ident = lambda i: i
    row = pl.BlockSpec((TQ, D), lambda i: (i, 0))
    col = lambda rows: pl.BlockSpec((rows, TQ), lambda i: (0, i))
    whole = pl.BlockSpec((D, D), lambda i: (0, 0))
    return pl.pallas_call(
        body, grid=(nq,), name="l1_attn_fwd",
        out_shape=(jax.ShapeDtypeStruct((seq, D), F32), jax.ShapeDtypeStruct((seq, D), BF16),
                   jax.ShapeDtypeStruct((D, seq), BF16),
                   jax.ShapeDtypeStruct((N_HEADS, seq), F32), jax.ShapeDtypeStruct((1, 1), F32),
                   jax.ShapeDtypeStruct((1, D), F32), jax.ShapeDtypeStruct((D, D), F32), jax.ShapeDtypeStruct((D, D), BF16)),
        in_specs=[col(D), col(D)] + _band_specs_t(nb, ident) + _band_specs_t(nb, ident) + [
            row, row, _resident((D, D)), _resident((1, D)), pl.BlockSpec(memory_space=pltpu.SMEM)],
        out_specs=(row, row, col(D), col(N_HEADS), pl.BlockSpec((1, 1), lambda i: (0, 0)),
                   pl.BlockSpec((1, D), lambda i: (0, 0)), whole, whole),
        scratch_shapes=[pltpu.VMEM((KV_W, TQ + 2 * BLK), BF16), pltpu.VMEM((KV_W, TQ + 2 * BLK), BF16),
                        pltpu.VMEM((D, TQ), F32)],
        compiler_params=_params(56),
    )(qt, gatet, kt, kt, kt, vt, vt, vt, x1, tgt, wout, gf, sink)


def _l1_attn_bwd(dx2b, wout, qt, kt, vt, gatet, att, lse, sink):
    seq = dx2b.shape[0]
    nq, nb = seq // TQ, seq // BLK

    def body(dx_ref, wo_ref, q_ref, gate_ref, kp_ref, k_ref, kn_ref, vp_ref, v_ref, vn_ref, att_ref, lse_ref, sink_ref,
             dq_ref, dgate_ref, dk_ref, dv_ref, dsink_ref, kbuf, vbuf, dkacc, dvacc, dat_scr, delta_scr, dsacc):
        i = pl.program_id(0)

        @pl.when(i == 0)
        def _():
            dkacc[...] = jnp.zeros_like(dkacc)
            dvacc[...] = jnp.zeros_like(dvacc)
            dsacc[...] = jnp.zeros_like(dsacc)

        @pl.when(i > 0)
        def _():
            for acc in (dkacc, dvacc):
                acc[:, 0:2 * BLK] = acc[:, TQ:TQ + 2 * BLK]
                acc[:, 2 * BLK:2 * BLK + TQ] = jnp.zeros((KV_W, TQ), F32)

        @pl.when(i < nq)
        def _():
            _fill_band(kbuf, kp_ref, k_ref, kn_ref)
            _fill_band(vbuf, vp_ref, v_ref, vn_ref)
            dyt = _nt(wo_ref[...], dx_ref[...])
            sg, dsg = _silu_and_grad(gate_ref[...].astype(F32))
            attf = att_ref[...].astype(F32)
            dat = dyt * sg
            dat_scr[...] = dat.astype(BF16)
            dgate_ref[...] = (dyt * attf * dsg).astype(BF16)
            dl = dat * attf
            delta_scr[...] = jnp.concatenate(
                [jnp.sum(dl[h * HD:(h + 1) * HD, :], axis=0, keepdims=True) for h in range(N_HEADS)], axis=0)
            ones_rows = _ones_rows(2, 3 * BLK)
            groups = [list(range(0, N_HEADS, HPP))[g:g + BWD_GROUP] for g in range(0, N_HEADS // HPP, BWD_GROUP)]
            work = [(j, grp) for j in range(TQ // BLK) for grp in groups]

            def scores(j, passes):
                c0 = j * BLK
                st = dict(c0=c0, passes=passes, bias=_band_bias_t(i * (TQ // BLK) + j, nb))
                st["kv_rows"] = [slice(h0 // GQA * HD, (h0 // GQA + 1) * HD) for h0 in passes]
                st["q4s"] = [_heads_t(q_ref, h0, c0) for h0 in passes]
                st["do4s"] = [_heads_t(dat_scr, h0, c0) for h0 in passes]
                st["lse4s"] = [_row4(lse_ref, h0, c0) for h0 in passes]
                st["delta4s"] = [_row4(delta_scr, h0, c0) for h0 in passes]
                st["kths"] = [kbuf[rows, c0:c0 + 3 * BLK] for rows in st["kv_rows"]]
                st["sts"] = [_tn(jnp.concatenate([kth, ones_rows], axis=0),
                                 jnp.concatenate([q4, _minus_rows(lse4)], axis=0))
                             for kth, q4, lse4 in zip(st["kths"], st["q4s"], st["lse4s"])]
                st["dpds"] = [_tn(jnp.concatenate([vbuf[rows, c0:c0 + 3 * BLK], ones_rows], axis=0),
                                  jnp.concatenate([do4, _minus_rows(delta4)], axis=0))
                              for rows, do4, delta4 in zip(st["kv_rows"], st["do4s"], st["delta4s"])]
                return st

            def elementwise(st):
                st["ps"] = [jnp.exp(_masked(s_, st["bias"])) for s_ in st["sts"]]
                st["dss"] = [(p * dpd).astype(BF16) for p, dpd in zip(st["ps"], st["dpds"])]

            def gradients(st):
                c0 = st["c0"]
                dq4s = [_mm(kth, ds) * SCALE for kth, ds in zip(st["kths"], st["dss"])]
                dks = [_nt(q4, ds) for q4, ds in zip(st["q4s"], st["dss"])]
                dvs = [_nt(do4, p.astype(BF16)) for do4, p in zip(st["do4s"], st["ps"])]
                for h0, rows, dq4, dk, dv, lse4, delta4 in zip(st["passes"], st["kv_rows"], dq4s, dks, dvs, st["lse4s"],
                                                               st["delta4s"]):
                    dkacc[rows, c0:c0 + 3 * BLK] += dk
                    dvacc[rows, c0:c0 + 3 * BLK] += dv
                    dsk = -jnp.exp(_sink_row(sink_ref, h0) - lse4) * delta4
                    for g in range(HPP):
                        h = h0 + g
                        dq_ref[h * HD:(h + 1) * HD, c0:c0 + BLK] = dq4[:, g * BLK:(g + 1) * BLK].astype(BF16)
                        dsacc[h:h + 1, :] += dsk[:, g * BLK:(g + 1) * BLK]

            ahead = [scores(*w) for w in work[:BWD_AHEAD]]
            for n in range(len(work)):
                if n + BWD_AHEAD < len(work):
                    ahead.append(scores(*work[n + BWD_AHEAD]))
                state = ahead.pop(0)
                elementwise(state)
                gradients(state)

        dk_ref[...] = dkacc[:, 0:TQ].astype(BF16)
        dv_ref[...] = dvacc[:, 0:TQ].astype(BF16)

        @pl.when(i == nq)
        def _():
            dsink_ref[...] = jnp.broadcast_to(jnp.sum(dsacc[...], axis=1, keepdims=True), (N_HEADS, LANES))

    clamp = lambda i: jnp.minimum(i, nq - 1)
    row = pl.BlockSpec((TQ, D), lambda i: (clamp(i), 0))
    col = lambda rows: pl.BlockSpec((rows, TQ), lambda i: (0, clamp(i)))
    pad = pl.BlockSpec((KV_W, TQ), lambda i: (0, i))
    return pl.pallas_call(
        body, grid=(nq + 1,), name="l1_attn_bwd",
        out_shape=(jax.ShapeDtypeStruct((D, seq), BF16), jax.ShapeDtypeStruct((D, seq), BF16),
                   jax.ShapeDtypeStruct((KV_W, seq + TQ), BF16), jax.ShapeDtypeStruct((KV_W, seq + TQ), BF16),
                   jax.ShapeDtypeStruct((N_HEADS, LANES), F32)),
        in_specs=[row, _resident((D, D)), col(D), col(D)] + _band_specs_t(nb, clamp) + _band_specs_t(nb, clamp) + [
            col(D), col(N_HEADS), pl.BlockSpec(memory_space=pltpu.SMEM)],
        out_specs=(col(D), col(D), pad, pad, pl.BlockSpec((N_HEADS, LANES), lambda i: (0, 0))),
        scratch_shapes=[pltpu.VMEM((KV_W, TQ + 2 * BLK), BF16), pltpu.VMEM((KV_W, TQ + 2 * BLK), BF16),
                        pltpu.VMEM((KV_W, TQ + 2 * BLK), F32), pltpu.VMEM((KV_W, TQ + 2 * BLK), F32),
                        pltpu.VMEM((D, TQ), BF16), pltpu.VMEM((N_HEADS, TQ), F32), pltpu.VMEM((N_HEADS, LANES), F32)],
        compiler_params=_params(56),
    )(dx2b, wout, qt, gatet, kt, kt, kt, vt, vt, vt, att, lse, sink)


def _l1_in_proj_bwd(dq_r, dk_r, dv, dgate, cos_t, sin_t, w_t, x1, g1, dx2):
    seq = x1.shape[0]
    tm = 512

    def body(dq_ref, dk_ref, dv_ref, dg_ref, c_ref, s_ref, w_ref, x_ref, g_ref, dres_ref,
             dx_ref, dxb_ref, dz_ref, dn_ref):
        @pl.when(pl.program_id(0) == 0)
        def _():
            dn_ref[...] = jnp.zeros_like(dn_ref)

        c, s = c_ref[...], s_ref[...]
        dq = _rope_t(dq_ref[...].astype(F32), c, s, N_HEADS, -1).astype(BF16)
        dk = _rope_t(dk_ref[...].astype(F32), c, s, N_KV, -1).astype(BF16)
        dz = jnp.concatenate([dq, dk, dv_ref[...], dg_ref[...]], axis=0)
        dz_ref[...] = dz
        dh = _tn(dz, w_ref[...])
        xf = x_ref[...]
        r = lax.rsqrt(jnp.mean(xf * xf, axis=1, keepdims=True) + EPS)
        xn = xf * r
        dn_ref[...] += jnp.sum(dh * xn, axis=0, keepdims=True)
        dxn = dh * g_ref[...]
        dx = dres_ref[...] + r * (dxn - xn * jnp.mean(dxn * xn, axis=1, keepdims=True))
        dx_ref[...] = dx
        dxb_ref[...] = dx.astype(BF16)

    row = pl.BlockSpec((tm, D), lambda i: (i, 0))
    col = lambda rows: pl.BlockSpec((rows, tm), lambda i: (0, i))
    return pl.pallas_call(
        body, grid=(seq // tm,), name="l1_in_proj_bwd",
        out_shape=(jax.ShapeDtypeStruct((seq, D), F32), jax.ShapeDtypeStruct((seq, D), BF16),
                   jax.ShapeDtypeStruct((MIX1_IN, seq), BF16), jax.ShapeDtypeStruct((1, D), F32)),
        in_specs=[col(D), col(KV_W), col(KV_W), col(D), col(ROT_HALF), col(ROT_HALF), _resident((MIX1_IN, D)), row,
                  _resident((1, D)), row],
        out_specs=(row, row, col(MIX1_IN), pl.BlockSpec((1, D), lambda i: (0, 0))),
        compiler_params=_params(48),
    )(dq_r, dk_r, dv, dgate, cos_t, sin_t, w_t, x1, g1, dx2)


def _l0_mix_bwd(dx1b, wout, za, bx, bg, ws, ws_t, bias, gv, wg, wg_t, scale):
    seq = dx1b.shape[0]
    ts = 256
    n_tiles = seq // ts

    def body(dx_ref, wo_ref, za_ref, bx_ref, bxp_ref, bxn_ref, bg_ref, ws_ref, wst_ref, bias_ref, gv_ref, wg_ref,
             wgt_ref, sc_ref,
             dz_ref, dp_ref, catt_ref, dws_ref, dbias_ref, dgv_ref, dsc_ref, dwg_ref, db_ref, xe_ref, *tmp_refs):
        i = pl.program_id(0)

        @pl.when(i == 0)
        def _():
            for r_ in (dws_ref, dbias_ref, dgv_ref, dsc_ref, dwg_ref, db_ref):
                r_[...] = jnp.zeros_like(r_)

        dxb = dx_ref[...]
        dya = _nt(dxb, wo_ref[0:D, :])
        dyb = _nt(dxb, wo_ref[D:2 * D, :])

        vg, dvg_dz = _gelu_and_grad(za_ref[:, D:2 * D].astype(F32))
        rv = lax.rsqrt(jnp.mean(vg * vg, axis=1, keepdims=True) + EPS)
        vnorm = vg * rv
        gvw = gv_ref[...]
        vnb = (vnorm * gvw).astype(BF16)
        mixed = _spatial_mix(ws_ref, vnb, bias_ref[...], ts)

        _fill_halo(xe_ref, bx_ref[...], bxp_ref, bxn_ref, i, n_tiles, ts)
        pb = _pool_forward(xe_ref, tmp_refs, ts, i * ts, seq).astype(BF16)
        ypre = jnp.concatenate([_mm(pb[:, g * GDIM:(g + 1) * GDIM], wg_ref[g]) for g in range(4)], axis=1)

        u, du = _gelu_and_grad(za_ref[:, 0:D].astype(F32))
        sga, dsga = _silu_and_grad(za_ref[:, 2 * D:3 * D].astype(F32))
        um = u * mixed
        ya = (um * sga).astype(BF16)
        t = dya * sga
        dz_ref[:, 0:D] = (t * mixed * du).astype(BF16)
        dz_ref[:, 2 * D:3 * D] = (dya * um * dsga).astype(BF16)
        dmixed = t * u
        dmb = dmixed.astype(BF16)
        dvn_rows = []
        for c in range(ts // CHUNK):
            rows = slice(c * CHUNK, (c + 1) * CHUNK)
            parts = []
            for h in range(A_GROUPS):
                cols = slice(h * GDIM, (h + 1) * GDIM)
                dws_ref[h] += _nt(dmb[rows, cols], vnb[rows, cols])
                parts.append(_mm(wst_ref[h], dmb[rows, cols]))
            dvn_rows.append(jnp.concatenate(parts, axis=1))

        sc = sc_ref[...]
        y = ypre * sc
        sgb, dsgb = _silu_and_grad(bg_ref[...].astype(F32))
        yb = (y * sgb).astype(BF16)
        dy_b = dyb * sgb
        dz_ref[:, 3 * D:4 * D] = jnp.zeros((ts, D), BF16)
        dz_ref[:, 4 * D:5 * D] = (dyb * y * dsgb).astype(BF16)
        dsc_ref[...] += jnp.sum(dy_b * ypre, axis=0, keepdims=True)
        dypre = (dy_b * sc).astype(BF16)
        dps = []
        for g in range(4):
            cols = slice(g * GDIM, (g + 1) * GDIM)
            dwg_ref[g] += _tn(pb[:, cols], dypre[:, cols])
            dps.append(_mm(dypre[:, cols], wgt_ref[g]))

        dbias = dmixed[0:CHUNK, :]
        for c in range(1, ts // CHUNK):
            dbias = dbias + dmixed[c * CHUNK:(c + 1) * CHUNK, :]
        dbias_ref[...] += dbias
        dvn = jnp.concatenate(dvn_rows, axis=0)
        dgv_ref[...] += jnp.sum(dvn * vnorm, axis=0, keepdims=True)
        dxn = dvn * gvw
        dvg = rv * (dxn - vnorm * jnp.mean(dxn * vnorm, axis=1, keepdims=True))
        dz_ref[:, D:2 * D] = (dvg * dvg_dz).astype(BF16)

        dp_ref[...] = jnp.concatenate(dps, axis=1)
        catt_ref[...] = jnp.concatenate([ya, yb], axis=1).T

        @pl.when(i == n_tiles - 1)
        def _():
            for h in range(A_GROUPS):
                tot = jnp.sum(dbias_ref[:, h * GDIM:(h + 1) * GDIM].T, axis=0, keepdims=True)
                db_ref[pl.ds(h * 8, 8), :] = jnp.broadcast_to(tot, (8, CHUNK))

    prev, nxt = _halo_specs(ts, seq, D)
    row = lambda w_: pl.BlockSpec((ts, w_), lambda i: (i, 0))
    acc = lambda shape: pl.BlockSpec(shape, lambda i: (0,) * len(shape))
    return pl.pallas_call(
        body, grid=(n_tiles,), name="l0_mix_bwd",
        out_shape=(jax.ShapeDtypeStruct((seq, MIX0_IN), BF16), jax.ShapeDtypeStruct((seq, D), F32),
                   jax.ShapeDtypeStruct((2 * D, seq), BF16),
                   jax.ShapeDtypeStruct((4, CHUNK, CHUNK), F32), jax.ShapeDtypeStruct((CHUNK, D), F32),
                   jax.ShapeDtypeStruct((1, D), F32), jax.ShapeDtypeStruct((1, D), F32),
                   jax.ShapeDtypeStruct((4, GDIM, GDIM), F32), jax.ShapeDtypeStruct((32, CHUNK), F32)),
        in_specs=[row(D), _resident((2 * D, D)), row(3 * D), row(D), prev, nxt, row(D), _resident((4, CHUNK, CHUNK)),
                  _resident((4, CHUNK, CHUNK)), _resident((CHUNK, D)), _resident((1, D)), _resident((4, GDIM, GDIM)),
                  _resident((4, GDIM, GDIM)), _resident((1, D))],
        out_specs=(row(MIX0_IN), row(D), pl.BlockSpec((2 * D, ts), lambda i: (0, i)),
                   acc((4, CHUNK, CHUNK)), acc((CHUNK, D)), acc((1, D)), acc((1, D)), acc((4, GDIM, GDIM)),
                   acc((32, CHUNK))),
        scratch_shapes=_pool_scratch(ts),
        compiler_params=_params(56),
    )(dx1b, wout, za, bx, bx, bx, bg, ws, ws_t, bias, gv, wg, wg_t, scale)


def _l0_pool_bwd(dp, dz):
    seq = dp.shape[0]
    ts = 512
    n_tiles = seq // ts
    ext = ts + 2 * POOL_HALO

    def body(dp_ref, dpp_ref, dpn_ref, dz_ref, out_ref, qe_ref, *tmp_refs):
        i = pl.program_id(0)
        _fill_halo(qe_ref, dp_ref[...], dpp_ref, dpn_ref, i, n_tiles, ts)
        te = i * ts - POOL_HALO + lax.broadcasted_iota(jnp.int32, (ext, 1), 0)
        for gi, w in enumerate(POOL_WINDOWS):
            hw = w // 2
            cols = slice(gi * GDIM, (gi + 1) * GDIM)
            cnt = jnp.maximum(jnp.minimum(te + hw, seq) - jnp.maximum(te - hw, 0), 1).astype(F32)
            qe_ref[pl.ds(0, ext), cols] = qe_ref[pl.ds(0, ext), cols] / cnt
        outs = []
        for gi, w in enumerate(POOL_WINDOWS):
            cols = slice(gi * GDIM, (gi + 1) * GDIM)
            outs.append(_window_sums(qe_ref, tmp_refs, ts, cols, w, 1) - dp_ref[:, cols])
        out_ref[...] = jnp.concatenate(outs, axis=1).astype(BF16)

    prev, nxt = _halo_specs(ts, seq, D)
    row = pl.BlockSpec((ts, D), lambda i: (i, 0))
    return pl.pallas_call(
        body, grid=(n_tiles,), name="l0_pool_bwd",
        out_shape=jax.ShapeDtypeStruct(dz.shape, BF16),
        in_specs=[row, prev, nxt, pl.BlockSpec(memory_space=pl.ANY)],
        out_specs=pl.BlockSpec((ts, D), lambda i: (i, 3)),
        input_output_aliases={3: 0},
        scratch_shapes=_pool_scratch(ts),
        compiler_params=_params(32),
    )(dp, dp, dp, dz)


def _l0_in_proj_bwd(dz, w, x, g0, dx1):
    seq = x.shape[0]
    tm = 512

    def body(dz_ref, w_ref, x_ref, g_ref, dres_ref, dx_ref, dn_ref):
        @pl.when(pl.program_id(0) == 0)
        def _():
            dn_ref[...] = jnp.zeros_like(dn_ref)

        dh = _nt(dz_ref[...], w_ref[...])
        xf = x_ref[...]
        r = lax.rsqrt(jnp.mean(xf * xf, axis=1, keepdims=True) + EPS)
        xn = xf * r
        dn_ref[...] += jnp.sum(dh * xn, axis=0, keepdims=True)
        dxn = dh * g_ref[...]
        dx_ref[...] = dres_ref[...] + r * (dxn - xn * jnp.mean(dxn * xn, axis=1, keepdims=True))

    row = lambda w_: pl.BlockSpec((tm, w_), lambda i: (i, 0))
    return pl.pallas_call(
        body, grid=(seq // tm,), name="l0_in_proj_bwd",
        out_shape=(jax.ShapeDtypeStruct((seq, D), F32), jax.ShapeDtypeStruct((1, D), F32)),
        in_specs=[row(MIX0_IN), _resident((D, MIX0_IN)), row(D), _resident((1, D)), row(D)],
        out_specs=(row(D), pl.BlockSpec((1, D), lambda i: (0, 0))),
        compiler_params=_params(56),
    )(dz, w, x, g0, dx1)


def _dw_matmul(a_t, b, name, b_transposed=False, tn=1024, ts=1024, col_block=None):
    k, seq = a_t.shape
    n = b.shape[0] if b_transposed else b.shape[1]
    tn = min(n, tn)
    assert seq % ts == 0 and n % tn == 0 and (col_block is None or tn % col_block == 0)
    n_s = seq // ts
    per = 1 if col_block is None else tn // col_block

    def body(a_ref, b_ref, o_ref, ob_ref, acc_ref):
        s = pl.program_id(1)

        @pl.when(s == 0)
        def _():
            acc_ref[...] = jnp.zeros_like(acc_ref)

        acc_ref[...] += _nt(a_ref[...], b_ref[...]) if b_transposed else _mm(a_ref[...], b_ref[...])

        @pl.when(s == n_s - 1)
        def _():
            acc = acc_ref[...]
            if col_block is None:
                o_ref[...] = acc
                ob_ref[...] = acc.astype(BF16)
            else:
                for i in range(per):
                    piece = acc[:, i * col_block:(i + 1) * col_block]
                    o_ref[i] = piece
                    ob_ref[i] = piece.astype(BF16)

    b_spec = (pl.BlockSpec((tn, ts), lambda j, s: (j, s)) if b_transposed else pl.BlockSpec((ts, tn), lambda j, s: (s, j)))
    if col_block is None:
        shape, o_spec = (k, n), pl.BlockSpec((k, tn), lambda j, s: (0, j))
    else:
        shape, o_spec = (n // col_block, k, col_block), pl.BlockSpec((per, k, col_block), lambda j, s: (j, 0, 0))
    return pl.pallas_call(
        body, grid=(n // tn, n_s), name=name,
        out_shape=(jax.ShapeDtypeStruct(shape, F32), jax.ShapeDtypeStruct(shape, BF16)),
        in_specs=[pl.BlockSpec((k, ts), lambda j, s: (0, s)), b_spec],
        out_specs=(o_spec, o_spec),
        scratch_shapes=[pltpu.VMEM((k, tn), F32)],
        compiler_params=_params(56, 2),
    )(a_t, b)


ROW_TILES = 8


def _cast_shards(shards):
    n = len(shards)

    def body(*refs):
        for a in range(n):
            refs[n + a][...] = refs[a][...].astype(BF16)

    vm = pl.BlockSpec(memory_space=pltpu.VMEM)
    return pl.pallas_call(body, name="cast_weights", out_shape=[jax.ShapeDtypeStruct(t.shape, BF16) for t in shards],
                          in_specs=[vm] * n, out_specs=[vm] * n, compiler_params=_params(32, 0))(*shards)


def _adamw_math(w, g, m, v):
    m2 = ADAM_B1 * m + (1.0 - ADAM_B1) * g
    v2 = ADAM_B2 * v + (1.0 - ADAM_B2) * (g * g)
    m_hat = m2 / (1.0 - ADAM_B1 ** ADAM_STEP)
    v_hat = v2 / (1.0 - ADAM_B2 ** ADAM_STEP)
    delta = -ADAM_LR * (m_hat / (jnp.sqrt(v_hat) + ADAM_EPS) + ADAM_WD * w)
    return delta, m2, v2


def _final_sum_adamw(g_list, recv_list, me, w_list, m_list, v_list):
    n = len(w_list)

    def body(me_ref, *refs):
        own, recv, w, m, v = (refs[k * n:(k + 1) * n] for k in range(5))
        outs = [refs[(5 + k) * n:(6 + k) * n] for k in range(4)]
        for a in range(n):
            g = own[a][...]
            for k in range(N_DEV - 1):
                g = g + recv[a][k].astype(F32)
            delta, m2, v2 = _adamw_math(w[a][...], g, m[a][...], v[a][...])
            for o_ref, val in zip((outs[0][a], outs[1][a], outs[2][a], outs[3][a]), (g, delta, m2, v2)):
                o_ref[...] = val

    own_specs, flat, wire, shapes = [], [], [], []
    for t in w_list:
        rows, width = t.shape
        tr = rows // ROW_TILES
        own_specs.append(pl.BlockSpec((None, tr, width), lambda i, me: (me[0], i, 0)))
        flat.append(pl.BlockSpec((tr, width), lambda i, me: (i, 0)))
        wire.append(pl.BlockSpec((N_DEV - 1, tr, width), lambda i, me: (0, i, 0)))
        shapes.append(jax.ShapeDtypeStruct((rows, width), F32))
    out = pl.pallas_call(
        body, name="grad_sum_adamw", out_shape=shapes * 4,
        grid_spec=pltpu.PrefetchScalarGridSpec(
            num_scalar_prefetch=1, grid=(ROW_TILES,), in_specs=own_specs + wire + flat * 3, out_specs=flat * 4),
        compiler_params=_params(40),
    )(me, *g_list, *recv_list, *w_list, *m_list, *v_list)
    return [out[k * n:(k + 1) * n] for k in range(4)]


SMALL_NAMES = ("norm_0", "a_v_norm_0", "b_scale_0", "norm_1", "final_norm", "a_spatial_w_0", "a_spatial_b_0", "sink_1")
SMALL_VIEWS = ((8, LANES),) * 5 + ((4 * CHUNK, LANES), (4, LANES), (1, N_HEADS))
SMALL_ROW0 = (0, 8, 16, 24, 32, 40, 552, 560)
SMALL_ROWS = 568


def _small_sum_adamw(early, late, w_list, m_list, v_list):
    n = len(w_list)

    def body(e_ref, l_ref, *refs):
        gtot, first = e_ref[0], l_ref[0]
        for d in range(1, N_DEV):
            gtot = gtot + e_ref[d]
            first = first + l_ref[d]
        for a, ((rows, width), r0) in enumerate(zip(SMALL_VIEWS, SMALL_ROW0)):
            g = first if SMALL_NAMES[a] == "norm_0" else gtot[r0:r0 + rows, 0:width]
            delta, m2, v2 = _adamw_math(refs[a][...], g, refs[n + a][...], refs[2 * n + a][...])
            for k, val in enumerate((g, delta, m2, v2)):
                refs[(3 + k) * n + a][...] = val
        refs[7 * n][...] = gtot[LOSS_ROW:LOSS_ROW + 1, LOSS_LANE:LOSS_LANE + 1]

    vm = pl.BlockSpec(memory_space=pltpu.VMEM)
    shapes = [jax.ShapeDtypeStruct(s, F32) for s in SMALL_VIEWS]
    out = pl.pallas_call(
        body, name="small_sum_adamw", out_shape=shapes * 4 + [jax.ShapeDtypeStruct((1, 1), F32)],
        in_specs=[vm, vm] + [vm] * (3 * n), out_specs=[vm] * (4 * n + 1),
    )(early, late, *w_list, *m_list, *v_list)
    return [out[k * n:(k + 1) * n] for k in range(4)], out[4 * n]


def _all_gather_columns(blks):
    n = len(blks)

    def body(*refs):
        ins, outs = refs[:n], refs[n:2 * n]
        send_sems, recv_sems, local_sems = refs[2 * n:]
        x, y, c = lax.axis_index("x"), lax.axis_index("y"), lax.axis_index("c")
        me, sibling = (x, y, c), (x, y, 1 - c)
        chips = [(1 - x, y), (x, 1 - y), (1 - x, 1 - y)]

        def slot(a, px, py, pc):
            width = blks[a].shape[1]
            return outs[a].at[:, pl.ds(pl.multiple_of((4 * px + 2 * py + pc) * width, LANES), width)]

        def copy(k, a, block, to, from_input=False):
            return pltpu.make_async_remote_copy(
                src_ref=ins[a] if from_input else slot(a, *block), dst_ref=slot(a, *block),
                send_sem=send_sems.at[k, a], recv_sem=recv_sems.at[k, a], device_id=to, device_id_type=MESH)

        mine = [pltpu.make_async_copy(ins[a], slot(a, *me), local_sems.at[a]) for a in range(n)]
        first = []
        for a in range(n):
            first.append(copy(0, a, me, sibling, from_input=True))
            first += [copy(1 + j, a, me, (*chip, c), from_input=True) for j, chip in enumerate(chips)]
        for cp in mine + first:
            cp.start()
        passed = []
        for j, chip in enumerate(chips):
            for a in range(n):
                copy(1 + j, a, (*chip, c), me).wait_recv()
                passed.append(copy(4 + j, a, (*chip, c), sibling))
                passed[-1].start()
        for a in range(n):
            copy(0, a, sibling, me).wait_recv()
        for j, chip in enumerate(chips):
            for a in range(n):
                copy(4 + j, a, (*chip, 1 - c), me).wait_recv()
        for cp in first + passed:
            cp.wait_send()
        for cp in mine:
            cp.wait()

    any_spec = pl.BlockSpec(memory_space=pl.ANY)
    return pl.pallas_call(
        body, name="weights_all_gather",
        out_shape=[jax.ShapeDtypeStruct((t.shape[0], N_DEV * t.shape[1]), t.dtype) for t in blks],
        in_specs=[any_spec] * n, out_specs=[any_spec] * n,
        scratch_shapes=[pltpu.SemaphoreType.DMA((7, n)), pltpu.SemaphoreType.DMA((7, n)), pltpu.SemaphoreType.DMA((n,))],
    )(*blks)


PEER_FLIPS = tuple((fx, fy, fc) for fx in (0, 1) for fy in (0, 1) for fc in (0, 1))[1:]


def _sequencer_all_gather(blks, name, collective_id, concat_rows=False):
    n = len(blks)

    def body(*refs):
        ins, outs = refs[:n], refs[n:2 * n]
        send_sems, recv_sems, local_sems = refs[2 * n:]
        x, y, c = lax.axis_index("x"), lax.axis_index("y"), lax.axis_index("c")
        peers = [(x ^ fx, y ^ fy, c ^ fc) for fx, fy, fc in PEER_FLIPS]
        barrier = pltpu.get_barrier_semaphore()
        for peer in peers:
            pl.semaphore_signal(barrier, inc=1, device_id=peer, device_id_type=MESH)
        pl.semaphore_wait(barrier, len(peers))
        me = 4 * x + 2 * y + c

        def slot(a):
            rows = blks[a].shape[0]
            return outs[a].at[pl.ds(pl.multiple_of(me * rows, 16), rows)] if concat_rows else outs[a].at[me]

        copies = [pltpu.make_async_remote_copy(
            src_ref=ins[a], dst_ref=slot(a), send_sem=send_sems.at[k, a], recv_sem=recv_sems.at[k, a],
            device_id=peer, device_id_type=MESH) for k, peer in enumerate(peers) for a in range(n)]
        mine = [pltpu.make_async_copy(ins[a], slot(a), local_sems.at[a]) for a in range(n)]
        for cp in copies + mine:
            cp.start()
        for cp in copies + mine:
            cp.wait()

    out_shape = (lambda t: (N_DEV * t.shape[0],) + t.shape[1:]) if concat_rows else (lambda t: (N_DEV,) + t.shape)
    return pl.kernel(
        body, out_type=[jax.ShapeDtypeStruct(out_shape(t), t.dtype) for t in blks],
        mesh=plsc.ScalarSubcoreMesh(axis_name="sequencer", num_cores=1), name=name,
        scratch_types=[pltpu.SemaphoreType.DMA((7, n)), pltpu.SemaphoreType.DMA((7, n)), pltpu.SemaphoreType.DMA((n,))],
        compiler_params=pltpu.CompilerParams(collective_id=collective_id),
    )(*blks)


def _sequencer_scatter(g_list, name, collective_id):
    n = len(g_list)

    def body(*refs):
        ins, outs = refs[:n], refs[n:2 * n]
        send_sems, recv_sems = refs[2 * n:]
        x, y, c = lax.axis_index("x"), lax.axis_index("y"), lax.axis_index("c")
        peers = [(x ^ fx, y ^ fy, c ^ fc) for fx, fy, fc in PEER_FLIPS]
        barrier = pltpu.get_barrier_semaphore()
        for peer in peers:
            pl.semaphore_signal(barrier, inc=1, device_id=peer, device_id_type=MESH)
        pl.semaphore_wait(barrier, len(peers))
        copies = [pltpu.make_async_remote_copy(
            src_ref=ins[a].at[4 * px + 2 * py + pc], dst_ref=outs[a].at[k], send_sem=send_sems.at[k, a],
            recv_sem=recv_sems.at[k, a], device_id=(px, py, pc), device_id_type=MESH)
            for k, (px, py, pc) in enumerate(peers) for a in range(n)]
        for cp in copies:
            cp.start()
        for cp in copies:
            cp.wait()

    return pl.kernel(
        body, out_type=[jax.ShapeDtypeStruct((N_DEV - 1,) + g.shape[1:], g.dtype) for g in g_list],
        mesh=plsc.ScalarSubcoreMesh(axis_name="sequencer", num_cores=1), name=name,
        scratch_types=[pltpu.SemaphoreType.DMA((7, n)), pltpu.SemaphoreType.DMA((7, n))],
        compiler_params=pltpu.CompilerParams(collective_id=collective_id),
    )(*g_list)


def _direct_all_gather(blk, name):
    def body(g_ref, out_ref, send_sems, recv_sems, local_sem):
        x, y, c = lax.axis_index("x"), lax.axis_index("y"), lax.axis_index("c")
        me = 4 * x + 2 * y + c
        copies = [pltpu.make_async_remote_copy(
            src_ref=g_ref, dst_ref=out_ref.at[me], send_sem=send_sems.at[k], recv_sem=recv_sems.at[k],
            device_id=(x ^ fx, y ^ fy, c ^ fc), device_id_type=MESH) for k, (fx, fy, fc) in enumerate(PEER_FLIPS)]
        copies.append(pltpu.make_async_copy(g_ref, out_ref.at[me], local_sem))
        for cp in copies:
            cp.start()
        for cp in copies:
            cp.wait()

    any_spec = pl.BlockSpec(memory_space=pl.ANY)
    return pl.pallas_call(
        body, name=name, out_shape=jax.ShapeDtypeStruct((N_DEV,) + blk.shape, blk.dtype),
        in_specs=[any_spec], out_specs=any_spec,
        scratch_shapes=[pltpu.SemaphoreType.DMA((7,)), pltpu.SemaphoreType.DMA((7,)), pltpu.SemaphoreType.DMA],
    )(blk)


def _shard_views(w_in_0, b_group_w_0, w_out_0, w_in_1, w_out_1):
    return [w_in_0, b_group_w_0.reshape(4 * 32, GDIM), w_out_0, w_in_1, w_out_1]


def _small_views(named):
    return [named[name].reshape(view) for name, view in zip(SMALL_NAMES, SMALL_VIEWS)]


LOSS_ROW, LOSS_LANE = 560, N_HEADS


def _pack_small_grads(named, loss_part):
    rows = []
    for name, (r, w) in zip(SMALL_NAMES, SMALL_VIEWS):
        pad_r = -r % 8
        if name == "sink_1":
            t = jnp.concatenate([named[name].reshape(r, w), loss_part], axis=1)
            rows.append(jnp.pad(t, ((0, pad_r), (0, LANES - w - 1))))
        elif name in named:
            rows.append(jnp.pad(named[name].reshape(r, w), ((0, pad_r), (0, LANES - w))))
        else:
            rows.append(jnp.zeros((r + pad_r, LANES), F32))
    return jnp.concatenate(rows, axis=0)


def _device_blocks(t, axis):
    shape = t.shape
    t = t.reshape(shape[:axis] + (N_DEV, shape[axis] // N_DEV) + shape[axis + 1:])
    t = jnp.moveaxis(t, axis, 0)
    return t.reshape(N_DEV, -1, shape[-1] if axis != len(shape) - 1 else shape[-1] // N_DEV)


def kernel(x, norm_0, w_in_0, a_v_norm_0, a_spatial_w_0, a_spatial_b_0, b_group_w_0, b_scale_0, w_out_0, norm_1, w_in_1, sink_1, w_out_1, final_norm, loss_target, m_norm_0, m_w_in_0, m_a_v_norm_0, m_a_spatial_w_0, m_a_spatial_b_0, m_b_group_w_0, m_b_scale_0, m_w_out_0, m_norm_1, m_w_in_1, m_sink_1, m_w_out_1, m_final_norm, v_norm_0, v_w_in_0, v_a_v_norm_0, v_a_spatial_w_0, v_a_spatial_b_0, v_b_group_w_0, v_b_scale_0, v_w_out_0, v_norm_1, v_w_in_1, v_sink_1, v_w_out_1, v_final_norm):
    seq = x.shape[1]
    xs = x.reshape(seq, D)
    tgt = loss_target.reshape(seq, D)
    ax, ay, ac = lax.axis_index("x"), lax.axis_index("y"), lax.axis_index("c")
    me = jnp.reshape(4 * ax + 2 * ay + ac, (1,)).astype(jnp.int32)

    shards = _shard_views(w_in_0, b_group_w_0, w_out_0, w_in_1, w_out_1)
    cast = _cast_shards([shards[0], shards[1], shards[2], w_in_1.T, shards[4]])
    win0 = _all_gather_columns(cast[0:1])[0]
    rest, win0 = lax.optimization_barrier((cast[1:5], win0))
    g_wg, wout0 = _sequencer_all_gather(rest[0:2], "weights_gather_a", 1, concat_rows=True)
    win1_t, wout1 = _sequencer_all_gather(rest[2:4], "weights_gather_b", 2, concat_rows=True)

    blocks, received, early = {}, {}, {}
    collective_ids = {"l1": 3, "out0": 4, "in0": 5}

    def scatter(tag, own_blocks, wire_blocks):
        blocks[tag] = own_blocks
        received[tag] = _sequencer_scatter(wire_blocks, "grad_scatter_" + tag, collective_ids[tag])

    def small_early(named, loss_part):
        early["small"] = _sequencer_all_gather([_pack_small_grads(named, loss_part)], "small_grad_gather", 6)[0]

    grad_x, d_norm_0 = _local_step(xs, tgt, win0, g_wg, wout0, win1_t, wout1, norm_0, a_v_norm_0, a_spatial_w_0,
                                   a_spatial_b_0, b_scale_0, norm_1, sink_1, final_norm, scatter, small_early)

    order = (("in0", 0), ("in0", 1), ("out0", 0), ("l1", 0), ("l1", 1))
    late = _direct_all_gather(d_norm_0.reshape(8, LANES), "norm_grad_gather")
    shards_late, _ = lax.optimization_barrier((shards, grad_x))
    big = _final_sum_adamw([blocks[t][i] for t, i in order], [received[t][i] for t, i in order], me, shards_late,
                           _shard_views(m_w_in_0, m_b_group_w_0, m_w_out_0, m_w_in_1, m_w_out_1),
                           _shard_views(v_w_in_0, v_b_group_w_0, v_w_out_0, v_w_in_1, v_w_out_1))
    weights = dict(norm_0=norm_0, a_v_norm_0=a_v_norm_0, a_spatial_w_0=a_spatial_w_0, a_spatial_b_0=a_spatial_b_0,
                   b_scale_0=b_scale_0, norm_1=norm_1, sink_1=sink_1, final_norm=final_norm)
    m_small = dict(norm_0=m_norm_0, a_v_norm_0=m_a_v_norm_0, a_spatial_w_0=m_a_spatial_w_0, a_spatial_b_0=m_a_spatial_b_0,
                   b_scale_0=m_b_scale_0, norm_1=m_norm_1, sink_1=m_sink_1, final_norm=m_final_norm)
    v_small = dict(norm_0=v_norm_0, a_v_norm_0=v_a_v_norm_0, a_spatial_w_0=v_a_spatial_w_0, a_spatial_b_0=v_a_spatial_b_0,
                   b_scale_0=v_b_scale_0, norm_1=v_norm_1, sink_1=v_sink_1, final_norm=v_final_norm)
    small, loss = _small_sum_adamw(early["small"], late, _small_views(weights), _small_views(m_small),
                                   _small_views(v_small))

    def in_order(kind):
        b = [b_.reshape(s_.shape) for b_, s_ in zip(big[kind], (w_in_0, b_group_w_0, w_out_0, w_in_1, w_out_1))]
        s = {name: t.reshape(weights[name].shape) for name, t in zip(SMALL_NAMES, small[kind])}
        return [s["norm_0"], b[0], s["a_v_norm_0"], s["a_spatial_w_0"], s["a_spatial_b_0"], b[1], s["b_scale_0"], b[2],
                s["norm_1"], b[3], s["sink_1"], b[4], s["final_norm"]]

    return (loss[0, 0], grad_x.reshape(1, seq, D), *in_order(0), *in_order(1), *in_order(2), *in_order(3))


def _local_step(xs, tgt, win0, g_wg, wout0, win1_t, wout1, norm_0, a_v_norm_0, a_spatial_w_0, a_spatial_b_0, b_scale_0,
                norm_1, sink_1, final_norm, scatter, small_early):
    seq = xs.shape[0]
    ws = a_spatial_w_0.astype(BF16)
    ws_t = jnp.swapaxes(ws, 1, 2)
    bias = jnp.repeat(a_spatial_b_0.T, GDIM, axis=1)
    g0, gv, scale, g1, gf = (t.reshape(1, D) for t in (norm_0, a_v_norm_0, b_scale_0, norm_1, final_norm))
    cos_t, sin_t = _rope_tables_t(seq)

    za, bx, bg, h0_t = _l0_in_proj(xs, g0, win0)
    g_wg, wout0, za = lax.optimization_barrier((g_wg, wout0, za))
    wg = g_wg.reshape(N_DEV, 4, 32, GDIM).transpose(1, 0, 2, 3).reshape(4, GDIM, GDIM)
    wg_t = jnp.swapaxes(wg, 1, 2)
    x1 = _l0_mix_fwd(za, bx, bg, xs, ws, bias, gv, wg, scale, wout0)
    win1_t, wout1, x1 = lax.optimization_barrier((win1_t, wout1, x1))
    qt, kt, vt, gatet, h1_t = _l1_in_proj(x1, g1, win1_t, cos_t, sin_t)
    dx2, dx2b, att, lse, loss_part, d_gf, d_wout1, d_wout1_wire = _l1_attn_fwd(
        qt, kt, vt, gatet, x1, tgt, wout1, gf, sink_1)

    dq_r, dgate, dk_pad, dv_pad, d_sink = _l1_attn_bwd(dx2b, wout1, qt, kt, vt, gatet, att, lse, sink_1)
    dk_r = dk_pad[:, BLK:BLK + seq]
    dv = dv_pad[:, BLK:BLK + seq]
    dx1, dx1b, dz1_t, d_g1 = _l1_in_proj_bwd(dq_r, dk_r, dv, dgate, cos_t, sin_t, win1_t, x1, g1, dx2)
    d_win1, d_win1_wire = _dw_matmul(h1_t, dz1_t, "dw_in_1", b_transposed=True, tn=1280, col_block=MIX1_IN // N_DEV)
    rows = lambda t: t.reshape(N_DEV, t.shape[0] // N_DEV, t.shape[1])
    scatter("l1", [d_win1, rows(d_wout1)], [d_win1_wire, rows(d_wout1_wire)])

    dz0, dp, cat_t, d_ws, _, d_gv, d_scale, d_wg, d_b = _l0_mix_bwd(
        dx1b, wout0, za, bx, bg, ws, ws_t, bias, gv, wg, wg_t, scale)
    dz0 = _l0_pool_bwd(dp, dz0)
    d_win0, d_win0_wire = _dw_matmul(h0_t, dz0, "dw_in_0", tn=1280, col_block=MIX0_IN // N_DEV)
    d_wg_blocks = _device_blocks(d_wg, 1)
    scatter("in0", [d_win0, d_wg_blocks], [d_win0_wire, d_wg_blocks])
    cat_t, _ = lax.optimization_barrier((cat_t, d_win0))
    d_wout0, d_wout0_wire = _dw_matmul(cat_t, dx1b, "dw_out_0")
    scatter("out0", [rows(d_wout0)], [rows(d_wout0_wire)])
    small_early(dict(a_v_norm_0=d_gv, a_spatial_w_0=d_ws, a_spatial_b_0=d_b.reshape(4, 8, CHUNK)[:, 0, :],
                     b_scale_0=d_scale, norm_1=d_g1, sink_1=d_sink[:, 0], final_norm=d_gf), loss_part)
    dz0, _ = lax.optimization_barrier((dz0, d_wout0))
    return _l0_in_proj_bwd(dz0, win0, xs, g0, dx1)
```

```python
import jax
import jax.numpy as jnp
from jax import lax
from jax.experimental import pallas as pl
from jax.experimental.pallas import tpu as pltpu
from jax.experimental.pallas import tpu_sc as plsc

F32 = jnp.float32
BF16 = jnp.bfloat16

D = 1024
EPS = 1e-6
NEG_INF = -1e30
CHUNK = 128
A_GROUPS = 4
POOL_WINDOWS = (2, 4, 8, 16)
POOL_HALO = 8
GDIM = 256
N_HEADS = 16
N_KV = 4
GQA = 4
HD = 64
BLK = 128
ROT_HALF = 8
ROPE_THETA = 500000.0
SCALE = HD ** -0.5
MIX0_IN = 5 * D
MIX1_IN = 2560
KV_W = N_KV * HD
Q_ROWS, K_ROWS, V_ROWS, G_ROWS = (0, D), (D, D + KV_W), (D + KV_W, D + 2 * KV_W), (D + 2 * KV_W, MIX1_IN)
TQ = 512

ADAM_LR = 0.001
ADAM_B1 = 0.9
ADAM_B2 = 0.999
ADAM_EPS = 1e-08
ADAM_WD = 0.01
ADAM_STEP = 10

N_DEV = 8
LANES = 128
MIB = 2 ** 20
MESH = pl.DeviceIdType.MESH


def _params(limit_mib, n_axes=1):
    return pltpu.CompilerParams(vmem_limit_bytes=limit_mib * MIB, dimension_semantics=("arbitrary",) * n_axes)


def _resident(shape):
    nd = len(shape)
    return pl.BlockSpec(shape, lambda *_: (0,) * nd, pipeline_mode=pl.Buffered(1))


def _gelu(x):
    k = 0.7978845608028654
    return 0.5 * x * (1.0 + jnp.tanh(k * (x + 0.044715 * x * x * x)))


def _gelu_and_grad(x):
    k = 0.7978845608028654
    x2 = x * x
    t = jnp.tanh(k * (x + 0.044715 * x * x2))
    g = 0.5 * x * (1.0 + t)
    dg = 0.5 * (1.0 + t) + 0.5 * x * (1.0 - t * t) * (k * (1.0 + 3.0 * 0.044715 * x2))
    return g, dg


def _silu_and_grad(x):
    s = jax.nn.sigmoid(x)
    return x * s, s * (1.0 + x * (1.0 - s))


def _nt(a, b):
    return lax.dot_general(a, b, (((1,), (1,)), ((), ())), preferred_element_type=F32)


def _tn(a, b):
    return lax.dot_general(a, b, (((0,), (0,)), ((), ())), preferred_element_type=F32)


def _mm(a, b):
    return jnp.dot(a, b, preferred_element_type=F32)


def _rope_tables_t(seq):
    inv = ROPE_THETA ** (-jnp.arange(0, 2 * ROT_HALF, 2, dtype=F32) / (2 * ROT_HALF))
    ang = inv[:, None] * jnp.arange(seq, dtype=F32)[None, :]
    return jnp.cos(ang), jnp.sin(ang)


def _rope_t(z, c, s, n_heads, sign):
    parts = []
    for h in range(n_heads):
        b = h * HD
        x1, x2 = z[b:b + ROT_HALF], z[b + ROT_HALF:b + 2 * ROT_HALF]
        if sign > 0:
            parts += [x1 * c - x2 * s, x2 * c + x1 * s]
        else:
            parts += [x1 * c + x2 * s, x2 * c - x1 * s]
        parts.append(z[b + 2 * ROT_HALF:b + HD])
    return jnp.concatenate(parts, axis=0)


N_CHIPS = 4
CHIP_COLS = MIX0_IN // N_CHIPS
IN_PROJ_PIECES = (
    ((0, 0, CHIP_COLS, 0),),
    ((0, CHIP_COLS, CHIP_COLS, 0),),
    ((0, 2 * CHIP_COLS, 3 * D - 2 * CHIP_COLS, 0), (1, 0, 3 * CHIP_COLS - 3 * D, 3 * D - 2 * CHIP_COLS)),
    ((1, 3 * CHIP_COLS - 3 * D, 4 * D - 3 * CHIP_COLS, 0), (2, 0, D, 4 * D - 3 * CHIP_COLS)),
)


def _l0_in_proj(x, g0, w_shard, later_shards):
    seq = x.shape[0]
    tm = 512
    n = seq // tm
    shard_cols = w_shard.shape[1]
    n_arr = 1 + len(later_shards)
    later = range(1, n_arr)
    assert 2 * shard_cols == CHIP_COLS and seq % tm == 0 and n >= 4

    def body(*refs):
        x_ref, g_ref = refs[:2]
        ins = refs[2:2 + n_arr]
        za_ref, bx_ref, bg_ref, ht_ref = refs[2 + n_arr:6 + n_arr]
        gathered = refs[6 + n_arr:6 + 2 * n_arr]
        h_all, w_buf, z32, z16, send_sems, recv_sems, local_sems, load_sems, out_sems = refs[6 + 2 * n_arr:]
        p, i = pl.program_id(0), pl.program_id(1)
        ax, ay, ac = lax.axis_index("x"), lax.axis_index("y"), lax.axis_index("c")
        me, sibling = (ax, ay, ac), (ax, ay, 1 - ac)
        chips = [(ax, ay), (1 - ax, ay), (ax, 1 - ay), (1 - ax, 1 - ay)]
        outs = (za_ref, bx_ref, bg_ref)

        def slot(a, px, py, pc):
            dev = 4 * px + 2 * py + pc
            if a == 0:
                return gathered[0].at[:, pl.ds(pl.multiple_of(dev * shard_cols, LANES), shard_cols)]
            rows = later_shards[a - 1].shape[0]
            return gathered[a].at[pl.ds(pl.multiple_of(dev * rows, 16), rows)]

        def copy(k, a, block, to, from_input=False):
            return pltpu.make_async_remote_copy(
                src_ref=ins[a] if from_input else slot(a, *block), dst_ref=slot(a, *block),
                send_sem=send_sems.at[k, a], recv_sem=recv_sems.at[k, a], device_id=to, device_id_type=MESH)

        def to_sibling(a):
            return copy(0, a, me, sibling, from_input=True)

        def send(j, a):
            return copy(j, a, me, (*chips[j], ac), from_input=True)

        def landed(j, a):
            return copy(j, a, (*chips[j], ac), me)

        def forward(j, a):
            return copy(3 + j, a, (*chips[j], ac), sibling)

        def forwarded(j, a):
            return copy(3 + j, a, (*chips[j], 1 - ac), me)

        def mine(a):
            return pltpu.make_async_copy(ins[a], slot(a, *me), local_sems.at[a])

        def load(chip, q):
            px, py = chip
            cols = pl.ds(pl.multiple_of((2 * px + py) * CHIP_COLS, LANES), CHIP_COLS)
            return pltpu.make_async_copy(gathered[0].at[:, cols], w_buf.at[q % 2], load_sems.at[q % 2])

        def out_copies(q, tile, stage):
            cps = []
            for k, (o, c0, width, z0) in enumerate(IN_PROJ_PIECES[q]):
                src = z32.at[stage, :, pl.ds(z0, width)] if o == 1 else z16.at[stage, :, pl.ds(z0, width)]
                dst = outs[o].at[pl.ds(pl.multiple_of(tile * tm, tm), tm), pl.ds(c0, width)]
                cps.append(pltpu.make_async_copy(src, dst, out_sems.at[stage, k]))
            return cps

        @pl.when((p == 0) & (i == 0))
        def _():
            for a in range(n_arr):
                mine(a).start()
                to_sibling(a).start()
            send(1, 0).start()
            send(2, 0).start()
            copy(0, 0, sibling, me).wait_recv()
            mine(0).wait()
            load(chips[0], 0).start()

        for j in range(1, N_CHIPS):
            @pl.when((p == j - 1) & (i == n - 2))
            def _(j=j):
                landed(j, 0).wait_recv()
                forward(j, 0).start()
                forwarded(j, 0).wait_recv()
                if j == 1:
                    send(1, 0).wait_send()
                    send(2, 0).wait_send()
                    send(3, 0).start()
                    for a in later:
                        for jj in range(1, N_CHIPS):
                            send(jj, a).start()
                load(chips[j], j).start()

        @pl.when((p == N_CHIPS - 1) & (i == n - 4))
        def _():
            for jj in range(1, N_CHIPS):
                for a in later:
                    landed(jj, a).wait_recv()
                    forward(jj, a).start()

        @pl.when(i == 0)
        def _():
            load(chips[0], p).wait()

        @pl.when(p == 0)
        def _():
            xf = x_ref[...]
            r = lax.rsqrt(jnp.mean(xf * xf, axis=1, keepdims=True) + EPS)
            h = (xf * r * g_ref[...]).astype(BF16)
            ht_ref[...] = h.T
            h_all[pl.ds(pl.multiple_of(i * tm, tm), tm), :] = h

        def chip_of_pass(pp):
            return (2 * ax + ay) ^ ((pp >> 1) | ((pp & 1) << 1))

        step = p * n + i
        stage = step % 2
        for q in range(N_CHIPS):
            @pl.when((step >= 2) & (chip_of_pass((step - 2) // n) == q))
            def _(q=q):
                for cp in out_copies(q, (step - 2) % n, stage):
                    cp.wait()

        z32[stage] = _mm(h_all[pl.ds(pl.multiple_of(i * tm, tm), tm), :], w_buf[p % 2])
        z16[stage] = z32[stage].astype(BF16)
        for q in range(N_CHIPS):
            @pl.when(chip_of_pass(p) == q)
            def _(q=q):
                for cp in out_copies(q, i, stage):
                    cp.start()

        last = (p == N_CHIPS - 1) & (i == n - 1)
        for q in range(N_CHIPS):
            @pl.when(last & (chip_of_pass(p) == q))
            def _(q=q):
                for cp in out_copies(q, n - 2, 1 - stage) + out_copies(q, n - 1, stage):
                    cp.wait()

        @pl.when(last)
        def _():
            for a in later:
                copy(0, a, sibling, me).wait_recv()
                for jj in range(1, N_CHIPS):
                    forwarded(jj, a).wait_recv()
                    send(jj, a).wait_send()
                mine(a).wait()
            send(3, 0).wait_send()
            for a in range(n_arr):
                to_sibling(a).wait_send()
                for jj in range(1, N_CHIPS):
                    forward(jj, a).wait_send()

    any_spec = pl.BlockSpec(memory_space=pl.ANY)
    first_pass_tile = lambda p, i: jnp.where(p == 0, i, n - 1)
    return pl.pallas_call(
        body, grid=(N_CHIPS, n), name="l0_in_proj",
        out_shape=[jax.ShapeDtypeStruct((seq, 3 * D), BF16), jax.ShapeDtypeStruct((seq, D), F32),
                   jax.ShapeDtypeStruct((seq, D), BF16), jax.ShapeDtypeStruct((D, seq), BF16),
                   jax.ShapeDtypeStruct((D, MIX0_IN), BF16)]
        + [jax.ShapeDtypeStruct((N_DEV * t.shape[0], t.shape[1]), t.dtype) for t in later_shards],
        in_specs=[pl.BlockSpec((tm, D), lambda p, i: (first_pass_tile(p, i), 0)), _resident((1, D))] + [any_spec] * n_arr,
        out_specs=[any_spec, any_spec, any_spec, pl.BlockSpec((D, tm), lambda p, i: (0, first_pass_tile(p, i)))]
        + [any_spec] * n_arr,
        scratch_shapes=[pltpu.VMEM((seq, D), BF16), pltpu.VMEM((2, D, CHIP_COLS), BF16),
                        pltpu.VMEM((2, tm, CHIP_COLS), F32), pltpu.VMEM((2, tm, CHIP_COLS), BF16),
                        pltpu.SemaphoreType.DMA((7, n_arr)), pltpu.SemaphoreType.DMA((7, n_arr)),
                        pltpu.SemaphoreType.DMA((n_arr,)), pltpu.SemaphoreType.DMA((2,)), pltpu.SemaphoreType.DMA((2, 2))],
        compiler_params=_params(48, 2),
    )(x, g0, w_shard, *later_shards)


POOL_EXT = 40


def _fill_halo(ext_ref, cur, prev_ref, next_ref, i, n_tiles, ts):
    ext_ref[pl.ds(0, POOL_HALO), :] = jnp.where(i > 0, prev_ref[...], 0.0)
    ext_ref[pl.ds(POOL_HALO, ts), :] = cur
    ext_ref[pl.ds(POOL_HALO + ts, POOL_HALO), :] = jnp.where(i < n_tiles - 1, next_ref[...], 0.0)
    ext_ref[pl.ds(2 * POOL_HALO + ts, POOL_EXT - 2 * POOL_HALO), :] = jnp.zeros((POOL_EXT - 2 * POOL_HALO, D), F32)


def _window_sums(src_ref, tmp_refs, ts, cols, w, shift):
    if w == 2:
        return src_ref[pl.ds(POOL_HALO - 1 + shift, ts), cols] + src_ref[pl.ds(POOL_HALO + shift, ts), cols]
    d2, d4, d8 = tmp_refs
    n2, n4, n8 = ts + 32, ts + 24, ts + 16
    d2[pl.ds(0, n2), :] = src_ref[pl.ds(0, n2), cols] + src_ref[pl.ds(1, n2), cols]
    if w == 4:
        return d2[pl.ds(POOL_HALO - 2 + shift, ts), :] + d2[pl.ds(POOL_HALO + shift, ts), :]
    d4[pl.ds(0, n4), :] = d2[pl.ds(0, n4), :] + d2[pl.ds(2, n4), :]
    if w == 8:
        return d4[pl.ds(POOL_HALO - 4 + shift, ts), :] + d4[pl.ds(POOL_HALO + shift, ts), :]
    d8[pl.ds(0, n8), :] = d4[pl.ds(0, n8), :] + d4[pl.ds(4, n8), :]
    return d8[pl.ds(shift, ts), :] + d8[pl.ds(POOL_HALO + shift, ts), :]


def _pool_scratch(ts):
    return [pltpu.VMEM((ts + POOL_EXT, D), F32)] + [pltpu.VMEM((ts + POOL_EXT, GDIM), F32)] * 3


def _pool_forward(xe_ref, tmp_refs, ts, t0, seq):
    tg = t0 + lax.broadcasted_iota(jnp.int32, (ts, 1), 0)
    outs = []
    for gi, w in enumerate(POOL_WINDOWS):
        hw = w // 2
        cols = slice(gi * GDIM, (gi + 1) * GDIM)
        cnt = (jnp.minimum(tg + hw, seq) - jnp.maximum(tg - hw, 0)).astype(F32)
        outs.append(_window_sums(xe_ref, tmp_refs, ts, cols, w, 0) / cnt - xe_ref[pl.ds(POOL_HALO, ts), cols])
    return jnp.concatenate(outs, axis=1)


def _spatial_mix(ws_ref, vnb, bias, ts):
    rows = []
    for c in range(ts // CHUNK):
        vc = vnb[c * CHUNK:(c + 1) * CHUNK, :]
        rows.append(jnp.concatenate(
            [_mm(ws_ref[h], vc[:, h * GDIM:(h + 1) * GDIM]) for h in range(A_GROUPS)], axis=1) + bias)
    return jnp.concatenate(rows, axis=0)


def _halo_specs(ts, seq, width):
    per = ts // POOL_HALO
    last = seq // POOL_HALO - 1
    prev = pl.BlockSpec((POOL_HALO, width), lambda i: (jnp.maximum(i * per - 1, 0), 0))
    nxt = pl.BlockSpec((POOL_HALO, width), lambda i: (jnp.minimum((i + 1) * per, last), 0))
    return prev, nxt


def _l0_mix_fwd(za, bx, bg, x, ws, bias, gv, wg, scale, wout):
    seq = x.shape[0]
    ts = 512
    n_tiles = seq // ts

    def body(za_ref, bx_ref, bxp_ref, bxn_ref, bg_ref, x_ref, ws_ref, bias_ref, gv_ref, wg_ref, sc_ref, wo_ref,
             x1_ref, xe_ref, *tmp_refs):
        i = pl.program_id(0)
        vg = _gelu(za_ref[:, D:2 * D].astype(F32))
        rv = lax.rsqrt(jnp.mean(vg * vg, axis=1, keepdims=True) + EPS)
        vnb = (vg * rv * gv_ref[...]).astype(BF16)
        mixed = _spatial_mix(ws_ref, vnb, bias_ref[...], ts)

        _fill_halo(xe_ref, bx_ref[...], bxp_ref, bxn_ref, i, n_tiles, ts)
        pb = _pool_forward(xe_ref, tmp_refs, ts, i * ts, seq).astype(BF16)
        ypre = jnp.concatenate([_mm(pb[:, g * GDIM:(g + 1) * GDIM], wg_ref[g]) for g in range(4)], axis=1)

        u = _gelu(za_ref[:, 0:D].astype(F32))
        ag = za_ref[:, 2 * D:3 * D].astype(F32)
        ya = (u * mixed * (ag * jax.nn.sigmoid(ag))).astype(BF16)
        out_a = _mm(ya, wo_ref[0:D, :])

        bgf = bg_ref[...].astype(F32)
        yb = (ypre * sc_ref[...] * (bgf * jax.nn.sigmoid(bgf))).astype(BF16)
        x1_ref[...] = x_ref[...] + out_a + _mm(yb, wo_ref[D:2 * D, :])

    prev, nxt = _halo_specs(ts, seq, D)
    row = lambda w: pl.BlockSpec((ts, w), lambda i: (i, 0))
    return pl.pallas_call(
        body, grid=(n_tiles,), name="l0_mix_fwd",
        out_shape=jax.ShapeDtypeStruct((seq, D), F32),
        in_specs=[row(3 * D), row(D), prev, nxt, row(D), row(D), _resident((4, CHUNK, CHUNK)), _resident((CHUNK, D)),
                  _resident((1, D)), _resident((4, GDIM, GDIM)), _resident((1, D)), _resident((2 * D, D))],
        out_specs=row(D),
        scratch_shapes=_pool_scratch(ts),
        compiler_params=_params(56),
    )(za, bx, bx, bx, bg, x, ws, bias, gv, wg, scale, wout)


def _l1_in_proj(x1, g1, w_t, cos_t, sin_t):
    seq = x1.shape[0]
    tm = 512

    def body(x_ref, g_ref, wt_ref, c_ref, s_ref, q_ref, k_ref, v_ref, gate_ref, ht_ref):
        xf = x_ref[...]
        r = lax.rsqrt(jnp.mean(xf * xf, axis=1, keepdims=True) + EPS)
        ht = (xf * r * g_ref[...]).astype(BF16).T
        ht_ref[...] = ht
        c, s = c_ref[...], s_ref[...]
        q_ref[...] = (_rope_t(_mm(wt_ref[Q_ROWS[0]:Q_ROWS[1], :], ht), c, s, N_HEADS, 1) * SCALE).astype(BF16)
        k_ref[...] = _rope_t(_mm(wt_ref[K_ROWS[0]:K_ROWS[1], :], ht), c, s, N_KV, 1).astype(BF16)
        v_ref[...] = _mm(wt_ref[V_ROWS[0]:V_ROWS[1], :], ht).astype(BF16)
        gate_ref[...] = _mm(wt_ref[G_ROWS[0]:G_ROWS[1], :], ht).astype(BF16)

    col = lambda rows: pl.BlockSpec((rows, tm), lambda i: (0, i))
    return pl.pallas_call(
        body, grid=(seq // tm,), name="l1_in_proj",
        out_shape=(jax.ShapeDtypeStruct((D, seq), BF16), jax.ShapeDtypeStruct((KV_W, seq), BF16),
                   jax.ShapeDtypeStruct((KV_W, seq), BF16), jax.ShapeDtypeStruct((D, seq), BF16),
                   jax.ShapeDtypeStruct((D, seq), BF16)),
        in_specs=[pl.BlockSpec((tm, D), lambda i: (i, 0)), _resident((1, D)), _resident((MIX1_IN, D)), col(ROT_HALF),
                  col(ROT_HALF)],
        out_specs=(col(D), col(KV_W), col(KV_W), col(D), col(D)),
        compiler_params=_params(48),
    )(x1, g1, w_t, cos_t, sin_t)


def _band_specs_t(nb, clamp_i):
    per = TQ // BLK
    prev = pl.BlockSpec((KV_W, BLK), lambda i: (0, jnp.maximum(clamp_i(i) * per - 1, 0)))
    cur = pl.BlockSpec((KV_W, TQ), lambda i: (0, clamp_i(i)))
    nxt = pl.BlockSpec((KV_W, BLK), lambda i: (0, jnp.minimum((clamp_i(i) + 1) * per, nb - 1)))
    return [prev, cur, nxt]


def _fill_band(buf, p_ref, c_ref, n_ref):
    buf[:, 0:BLK] = p_ref[...]
    buf[:, BLK:BLK + TQ] = c_ref[...]
    buf[:, BLK + TQ:2 * BLK + TQ] = n_ref[...]


def _band_bias_t(n, nb):
    c = lax.broadcasted_iota(jnp.int32, (BLK, BLK), 0)
    r = lax.broadcasted_iota(jnp.int32, (BLK, BLK), 1)
    first = jnp.where((c >= r) & (n > 0), 0.0, NEG_INF).astype(F32)
    last = jnp.where((c <= r) & (n < nb - 1), 0.0, NEG_INF).astype(F32)
    return jnp.concatenate([first] * HPP, axis=1), jnp.concatenate([last] * HPP, axis=1)


def _masked(st, bias):
    first, last = bias
    return jnp.concatenate([st[0:BLK] + first, st[BLK:2 * BLK], st[2 * BLK:3 * BLK] + last], axis=0)


AUG = 16


def _ones_rows(n_ones, width):
    return (lax.broadcasted_iota(jnp.int32, (AUG, width), 0) < n_ones).astype(BF16)


def _minus_rows(vec):
    hi = vec.astype(BF16).astype(F32)
    lo = vec - hi
    return jnp.concatenate([-hi, -lo, jnp.zeros((AUG - 2, vec.shape[1]), F32)], axis=0).astype(BF16)


HPP = GQA
FWD_GROUP, BWD_GROUP = 2, 1
BWD_AHEAD = 1


def _heads_t(ref, h0, c0):
    return jnp.concatenate([ref[(h0 + g) * HD:(h0 + g + 1) * HD, c0:c0 + BLK] for g in range(HPP)], axis=1)


def _row4(ref, h0, c0):
    return jnp.concatenate([ref[h0 + g:h0 + g + 1, c0:c0 + BLK] for g in range(HPP)], axis=1)


def _sink_row(sink_ref, h0):
    return jnp.concatenate([jnp.full((1, BLK), sink_ref[h0 + g], F32) for g in range(HPP)], axis=1)


def _l1_attn_fwd(qt, kt, vt, gatet, x1, tgt, wout, gf, sink):
    seq = x1.shape[0]
    nq, nb = seq // TQ, seq // BLK

    def body(q_ref, gate_ref, kp_ref, k_ref, kn_ref, vp_ref, v_ref, vn_ref, x1_ref, tgt_ref, wo_ref, gf_ref, sink_ref,
             dx2_ref, dx2b_ref, att_ref, lse_ref, loss_ref, dgf_ref, dwo_ref, dwo_wire_ref, kbuf, vbuf, att_scr):
        i = pl.program_id(0)

        @pl.when(i == 0)
        def _():
            loss_ref[...] = jnp.zeros_like(loss_ref)
            dgf_ref[...] = jnp.zeros_like(dgf_ref)
            dwo_ref[...] = jnp.zeros_like(dwo_ref)

        _fill_band(kbuf, kp_ref, k_ref, kn_ref)
        _fill_band(vbuf, vp_ref, v_ref, vn_ref)
        ones_row = _ones_rows(1, 3 * BLK)
        groups = [list(range(0, N_HEADS, HPP))[g:g + FWD_GROUP] for g in range(0, N_HEADS // HPP, FWD_GROUP)]
        work = [(j, grp) for j in range(TQ // BLK) for grp in groups]

        def scores(j, passes):
            c0 = j * BLK
            bias = _band_bias_t(i * (TQ // BLK) + j, nb)
            st = dict(c0=c0, passes=passes)
            st["kv_rows"] = [slice(h0 // GQA * HD, (h0 // GQA + 1) * HD) for h0 in passes]
            st["sts"] = [_masked(_tn(kbuf[rows, c0:c0 + 3 * BLK], _heads_t(q_ref, h0, c0)), bias)
                         for h0, rows in zip(passes, st["kv_rows"])]
            return st

        def softmaxes(st):
            st["sks"] = [_sink_row(sink_ref, h0) for h0 in st["passes"]]
            st["ms"] = [jnp.maximum(jnp.max(s_, axis=0, keepdims=True), sk) for s_, sk in zip(st["sts"], st["sks"])]
            st["ps"] = [jnp.exp(s_ - m).astype(BF16) for s_, m in zip(st["sts"], st["ms"])]

        def values(st):
            c0, passes = st["c0"], st["passes"]
            pvs = [_mm(jnp.concatenate([vbuf[rows, c0:c0 + 3 * BLK], ones_row], axis=0), p)
                   for rows, p in zip(st["kv_rows"], st["ps"])]
            lse_rows = []
            for h0, pv, m, sk in zip(passes, pvs, st["ms"], st["sks"]):
                den = pv[HD:HD + 1, :] + jnp.exp(sk - m)
                ot = pv[0:HD, :] / den
                lse = m + jnp.log(den)
                for g in range(HPP):
                    h = h0 + g
                    att_scr[h * HD:(h + 1) * HD, c0:c0 + BLK] = ot[:, g * BLK:(g + 1) * BLK]
                    lse_rows.append(lse[:, g * BLK:(g + 1) * BLK])
            lse_ref[passes[0]:passes[0] + len(lse_rows), c0:c0 + BLK] = jnp.concatenate(lse_rows, axis=0)

        state = scores(*work[0])
        for nxt in work[1:] + [None]:
            following = scores(*nxt) if nxt is not None else None
            softmaxes(state)
            values(state)
            state = following

        att = att_scr[...]
        gate = gate_ref[...].astype(F32)
        yt = (att * (gate * jax.nn.sigmoid(gate))).astype(BF16)
        att_ref[...] = att.astype(BF16)
        x2 = x1_ref[...] + _mm(yt.T, wo_ref[...])
        r = lax.rsqrt(jnp.mean(x2 * x2, axis=1, keepdims=True) + EPS)
        xn = x2 * r
        diff = xn * gf_ref[...] - tgt_ref[...]
        loss_ref[...] += 0.5 * jnp.sum(jnp.mean(diff * diff, axis=1, keepdims=True), axis=0, keepdims=True)
        dout = diff * (1.0 / D)
        dgf_ref[...] += jnp.sum(dout * xn, axis=0, keepdims=True)
        dxn = dout * gf_ref[...]
        dx2 = r * (dxn - xn * jnp.mean(dxn * xn, axis=1, keepdims=True))
        dx2_ref[...] = dx2
        dx2b = dx2.astype(BF16)
        dx2b_ref[...] = dx2b
        dwo_ref[...] += _mm(yt, dx2b)

        @pl.when(i == nq - 1)
        def _():
            dwo_wire_ref[...] = dwo_ref[...].astype(BF16)

    ident = lambda i: i
    row = pl.BlockSpec((TQ, D), lambda i: (i, 0))
    col = lambda rows: pl.BlockSpec((rows, TQ), lambda i: (0, i))
    whole = pl.BlockSpec((D, D), lambda i: (0, 0))
    return pl.pallas_call(
        body, grid=(nq,), name="l1_attn_fwd",
        out_shape=(jax.ShapeDtypeStruct((seq, D), F32), jax.ShapeDtypeStruct((seq, D), BF16),
                   jax.ShapeDtypeStruct((D, seq), BF16),
                   jax.ShapeDtypeStruct((N_HEADS, seq), F32), jax.ShapeDtypeStruct((1, 1), F32),
                   jax.ShapeDtypeStruct((1, D), F32), jax.ShapeDtypeStruct((D, D), F32), jax.ShapeDtypeStruct((D, D), BF16)),
        in_specs=[col(D), col(D)] + _band_specs_t(nb, ident) + _band_specs_t(nb, ident) + [
            row, row, _resident((D, D)), _resident((1, D)), pl.BlockSpec(memory_space=pltpu.SMEM)],
        out_specs=(row, row, col(D), col(N_HEADS), pl.BlockSpec((1, 1), lambda i: (0, 0)),
                   pl.BlockSpec((1, D), lambda i: (0, 0)), whole, whole),
        scratch_shapes=[pltpu.VMEM((KV_W, TQ + 2 * BLK), BF16), pltpu.VMEM((KV_W, TQ + 2 * BLK), BF16),
                        pltpu.VMEM((D, TQ), F32)],
        compiler_params=_params(56),
    )(qt, gatet, kt, kt, kt, vt, vt, vt, x1, tgt, wout, gf, sink)


def _l1_attn_bwd(dx2b, wout, qt, kt, vt, gatet, att, lse, sink):
    seq = dx2b.shape[0]
    nq, nb = seq // TQ, seq // BLK

    def body(dx_ref, wo_ref, q_ref, gate_ref, kp_ref, k_ref, kn_ref, vp_ref, v_ref, vn_ref, att_ref, lse_ref, sink_ref,
             dq_ref, dgate_ref, dk_ref, dv_ref, dsink_ref, kbuf, vbuf, dkacc, dvacc, dat_scr, delta_scr, dsacc):
        i = pl.program_id(0)

        @pl.when(i == 0)
        def _():
            dkacc[...] = jnp.zeros_like(dkacc)
            dvacc[...] = jnp.zeros_like(dvacc)
            dsacc[...] = jnp.zeros_like(dsacc)

        @pl.when(i > 0)
        def _():
            for acc in (dkacc, dvacc):
                acc[:, 0:2 * BLK] = acc[:, TQ:TQ + 2 * BLK]
                acc[:, 2 * BLK:2 * BLK + TQ] = jnp.zeros((KV_W, TQ), F32)

        @pl.when(i < nq)
        def _():
            _fill_band(kbuf, kp_ref, k_ref, kn_ref)
            _fill_band(vbuf, vp_ref, v_ref, vn_ref)
            dyt = _nt(wo_ref[...], dx_ref[...])
            sg, dsg = _silu_and_grad(gate_ref[...].astype(F32))
            attf = att_ref[...].astype(F32)
            dat = dyt * sg
            dat_scr[...] = dat.astype(BF16)
            dgate_ref[...] = (dyt * attf * dsg).astype(BF16)
            dl = dat * attf
            delta_scr[...] = jnp.concatenate(
                [jnp.sum(dl[h * HD:(h + 1) * HD, :], axis=0, keepdims=True) for h in range(N_HEADS)], axis=0)
            ones_rows = _ones_rows(2, 3 * BLK)
            groups = [list(range(0, N_HEADS, HPP))[g:g + BWD_GROUP] for g in range(0, N_HEADS // HPP, BWD_GROUP)]
            work = [(j, grp) for j in range(TQ // BLK) for grp in groups]

            def scores(j, passes):
                c0 = j * BLK
                st = dict(c0=c0, passes=passes, bias=_band_bias_t(i * (TQ // BLK) + j, nb))
                st["kv_rows"] = [slice(h0 // GQA * HD, (h0 // GQA + 1) * HD) for h0 in passes]
                st["q4s"] = [_heads_t(q_ref, h0, c0) for h0 in passes]
                st["do4s"] = [_heads_t(dat_scr, h0, c0) for h0 in passes]
                st["lse4s"] = [_row4(lse_ref, h0, c0) for h0 in passes]
                st["delta4s"] = [_row4(delta_scr, h0, c0) for h0 in passes]
                st["kths"] = [kbuf[rows, c0:c0 + 3 * BLK] for rows in st["kv_rows"]]
                st["sts"] = [_tn(jnp.concatenate([kth, ones_rows], axis=0),
                                 jnp.concatenate([q4, _minus_rows(lse4)], axis=0))
                             for kth, q4, lse4 in zip(st["kths"], st["q4s"], st["lse4s"])]
                st["dpds"] = [_tn(jnp.concatenate([vbuf[rows, c0:c0 + 3 * BLK], ones_rows], axis=0),
                                  jnp.concatenate([do4, _minus_rows(delta4)], axis=0))
                              for rows, do4, delta4 in zip(st["kv_rows"], st["do4s"], st["delta4s"])]
                return st

            def elementwise(st):
                st["ps"] = [jnp.exp(_masked(s_, st["bias"])) for s_ in st["sts"]]
                st["dss"] = [(p * dpd).astype(BF16) for p, dpd in zip(st["ps"], st["dpds"])]

            def gradients(st):
                c0 = st["c0"]
                dq4s = [_mm(kth, ds) * SCALE for kth, ds in zip(st["kths"], st["dss"])]
                dks = [_nt(q4, ds) for q4, ds in zip(st["q4s"], st["dss"])]
                dvs = [_nt(do4, p.astype(BF16)) for do4, p in zip(st["do4s"], st["ps"])]
                for h0, rows, dq4, dk, dv, lse4, delta4 in zip(st["passes"], st["kv_rows"], dq4s, dks, dvs, st["lse4s"],
                                                               st["delta4s"]):
                    dkacc[rows, c0:c0 + 3 * BLK] += dk
                    dvacc[rows, c0:c0 + 3 * BLK] += dv
                    dsk = -jnp.exp(_sink_row(sink_ref, h0) - lse4) * delta4
                    for g in range(HPP):
                        h = h0 + g
                        dq_ref[h * HD:(h + 1) * HD, c0:c0 + BLK] = dq4[:, g * BLK:(g + 1) * BLK].astype(BF16)
                        dsacc[h:h + 1, :] += dsk[:, g * BLK:(g + 1) * BLK]

            ahead = [scores(*w) for w in work[:BWD_AHEAD]]
            for n in range(len(work)):
                if n + BWD_AHEAD < len(work):
                    ahead.append(scores(*work[n + BWD_AHEAD]))
                state = ahead.pop(0)
                elementwise(state)
                gradients(state)

        dk_ref[...] = dkacc[:, 0:TQ].astype(BF16)
        dv_ref[...] = dvacc[:, 0:TQ].astype(BF16)

        @pl.when(i == nq)
        def _():
            dsink_ref[...] = jnp.broadcast_to(jnp.sum(dsacc[...], axis=1, keepdims=True), (N_HEADS, LANES))

    clamp = lambda i: jnp.minimum(i, nq - 1)
    row = pl.BlockSpec((TQ, D), lambda i: (clamp(i), 0))
    col = lambda rows: pl.BlockSpec((rows, TQ), lambda i: (0, clamp(i)))
    pad = pl.BlockSpec((KV_W, TQ), lambda i: (0, i))
    return pl.pallas_call(
        body, grid=(nq + 1,), name="l1_attn_bwd",
        out_shape=(jax.ShapeDtypeStruct((D, seq), BF16), jax.ShapeDtypeStruct((D, seq), BF16),
                   jax.ShapeDtypeStruct((KV_W, seq + TQ), BF16), jax.ShapeDtypeStruct((KV_W, seq + TQ), BF16),
                   jax.ShapeDtypeStruct((N_HEADS, LANES), F32)),
        in_specs=[row, _resident((D, D)), col(D), col(D)] + _band_specs_t(nb, clamp) + _band_specs_t(nb, clamp) + [
            col(D), col(N_HEADS), pl.BlockSpec(memory_space=pltpu.SMEM)],
        out_specs=(col(D), col(D), pad, pad, pl.BlockSpec((N_HEADS, LANES), lambda i: (0, 0))),
        scratch_shapes=[pltpu.VMEM((KV_W, TQ + 2 * BLK), BF16), pltpu.VMEM((KV_W, TQ + 2 * BLK), BF16),
                        pltpu.VMEM((KV_W, TQ + 2 * BLK), F32), pltpu.VMEM((KV_W, TQ + 2 * BLK), F32),
                        pltpu.VMEM((D, TQ), BF16), pltpu.VMEM((N_HEADS, TQ), F32), pltpu.VMEM((N_HEADS, LANES), F32)],
        compiler_params=_params(56),
    )(dx2b, wout, qt, gatet, kt, kt, kt, vt, vt, vt, att, lse, sink)


def _l1_in_proj_bwd(dq_r, dk_r, dv, dgate, cos_t, sin_t, w_t, x1, g1, dx2):
    seq = x1.shape[0]
    tm = 512

    def body(dq_ref, dk_ref, dv_ref, dg_ref, c_ref, s_ref, w_ref, x_ref, g_ref, dres_ref,
             dx_ref, dxb_ref, dz_ref, dn_ref):
        @pl.when(pl.program_id(0) == 0)
        def _():
            dn_ref[...] = jnp.zeros_like(dn_ref)

        c, s = c_ref[...], s_ref[...]
        dq = _rope_t(dq_ref[...].astype(F32), c, s, N_HEADS, -1).astype(BF16)
        dk = _rope_t(dk_ref[...].astype(F32), c, s, N_KV, -1).astype(BF16)
        dz = jnp.concatenate([dq, dk, dv_ref[...], dg_ref[...]], axis=0)
        dz_ref[...] = dz
        dh = _tn(dz, w_ref[...])
        xf = x_ref[...]
        r = lax.rsqrt(jnp.mean(xf * xf, axis=1, keepdims=True) + EPS)
        xn = xf * r
        dn_ref[...] += jnp.sum(dh * xn, axis=0, keepdims=True)
        dxn = dh * g_ref[...]
        dx = dres_ref[...] + r * (dxn - xn * jnp.mean(dxn * xn, axis=1, keepdims=True))
        dx_ref[...] = dx
        dxb_ref[...] = dx.astype(BF16)

    row = pl.BlockSpec((tm, D), lambda i: (i, 0))
    col = lambda rows: pl.BlockSpec((rows, tm), lambda i: (0, i))
    return pl.pallas_call(
        body, grid=(seq // tm,), name="l1_in_proj_bwd",
        out_shape=(jax.ShapeDtypeStruct((seq, D), F32), jax.ShapeDtypeStruct((seq, D), BF16),
                   jax.ShapeDtypeStruct((MIX1_IN, seq), BF16), jax.ShapeDtypeStruct((1, D), F32)),
        in_specs=[col(D), col(KV_W), col(KV_W), col(D), col(ROT_HALF), col(ROT_HALF), _resident((MIX1_IN, D)), row,
                  _resident((1, D)), row],
        out_specs=(row, row, col(MIX1_IN), pl.BlockSpec((1, D), lambda i: (0, 0))),
        compiler_params=_params(48),
    )(dq_r, dk_r, dv, dgate, cos_t, sin_t, w_t, x1, g1, dx2)


def _l0_mix_bwd(dx1b, wout, za, bx, bg, ws, ws_t, bias, gv, wg, wg_t, scale):
    seq = dx1b.shape[0]
    ts = 256
    n_tiles = seq // ts

    def body(dx_ref, wo_ref, za_ref, bx_ref, bxp_ref, bxn_ref, bg_ref, ws_ref, wst_ref, bias_ref, gv_ref, wg_ref,
             wgt_ref, sc_ref,
             dz_ref, dp_ref, catt_ref, dws_ref, dbias_ref, dgv_ref, dsc_ref, dwg_ref, db_ref, xe_ref, *tmp_refs):
        i = pl.program_id(0)

        @pl.when(i == 0)
        def _():
            for r_ in (dws_ref, dbias_ref, dgv_ref, dsc_ref, dwg_ref, db_ref):
                r_[...] = jnp.zeros_like(r_)

        dxb = dx_ref[...]
        dya = _nt(dxb, wo_ref[0:D, :])
        dyb = _nt(dxb, wo_ref[D:2 * D, :])

        vg, dvg_dz = _gelu_and_grad(za_ref[:, D:2 * D].astype(F32))
        rv = lax.rsqrt(jnp.mean(vg * vg, axis=1, keepdims=True) + EPS)
        vnorm = vg * rv
        gvw = gv_ref[...]
        vnb = (vnorm * gvw).astype(BF16)
        mixed = _spatial_mix(ws_ref, vnb, bias_ref[...], ts)

        _fill_halo(xe_ref, bx_ref[...], bxp_ref, bxn_ref, i, n_tiles, ts)
        pb = _pool_forward(xe_ref, tmp_refs, ts, i * ts, seq).astype(BF16)
        ypre = jnp.concatenate([_mm(pb[:, g * GDIM:(g + 1) * GDIM], wg_ref[g]) for g in range(4)], axis=1)

        u, du = _gelu_and_grad(za_ref[:, 0:D].astype(F32))
        sga, dsga = _silu_and_grad(za_ref[:, 2 * D:3 * D].astype(F32))
        um = u * mixed
        ya = (um * sga).astype(BF16)
        t = dya * sga
        dz_ref[:, 0:D] = (t * mixed * du).astype(BF16)
        dz_ref[:, 2 * D:3 * D] = (dya * um * dsga).astype(BF16)
        dmixed = t * u
        dmb = dmixed.astype(BF16)
        dvn_rows = []
        for c in range(ts // CHUNK):
            rows = slice(c * CHUNK, (c + 1) * CHUNK)
            parts = []
            for h in range(A_GROUPS):
                cols = slice(h * GDIM, (h + 1) * GDIM)
                dws_ref[h] += _nt(dmb[rows, cols], vnb[rows, cols])
                parts.append(_mm(wst_ref[h], dmb[rows, cols]))
            dvn_rows.append(jnp.concatenate(parts, axis=1))

        sc = sc_ref[...]
        y = ypre * sc
        sgb, dsgb = _silu_and_grad(bg_ref[...].astype(F32))
        yb = (y * sgb).astype(BF16)
        dy_b = dyb * sgb
        dz_ref[:, 3 * D:4 * D] = jnp.zeros((ts, D), BF16)
        dz_ref[:, 4 * D:5 * D] = (dyb * y * dsgb).astype(BF16)
        dsc_ref[...] += jnp.sum(dy_b * ypre, axis=0, keepdims=True)
        dypre = (dy_b * sc).astype(BF16)
        dps = []
        for g in range(4):
            cols = slice(g * GDIM, (g + 1) * GDIM)
            dwg_ref[g] += _tn(pb[:, cols], dypre[:, cols])
            dps.append(_mm(dypre[:, cols], wgt_ref[g]))

        dbias = dmixed[0:CHUNK, :]
        for c in range(1, ts // CHUNK):
            dbias = dbias + dmixed[c * CHUNK:(c + 1) * CHUNK, :]
        dbias_ref[...] += dbias
        dvn = jnp.concatenate(dvn_rows, axis=0)
        dgv_ref[...] += jnp.sum(dvn * vnorm, axis=0, keepdims=True)
        dxn = dvn * gvw
        dvg = rv * (dxn - vnorm * jnp.mean(dxn * vnorm, axis=1, keepdims=True))
        dz_ref[:, D:2 * D] = (dvg * dvg_dz).astype(BF16)

        dp_ref[...] = jnp.concatenate(dps, axis=1)
        catt_ref[...] = jnp.concatenate([ya, yb], axis=1).T

        @pl.when(i == n_tiles - 1)
        def _():
            for h in range(A_GROUPS):
                tot = jnp.sum(dbias_ref[:, h * GDIM:(h + 1) * GDIM].T, axis=0, keepdims=True)
                db_ref[pl.ds(h * 8, 8), :] = jnp.broadcast_to(tot, (8, CHUNK))

    prev, nxt = _halo_specs(ts, seq, D)
    row = lambda w_: pl.BlockSpec((ts, w_), lambda i: (i, 0))
    acc = lambda shape: pl.BlockSpec(shape, lambda i: (0,) * len(shape))
    return pl.pallas_call(
        body, grid=(n_tiles,), name="l0_mix_bwd",
        out_shape=(jax.ShapeDtypeStruct((seq, MIX0_IN), BF16), jax.ShapeDtypeStruct((seq, D), F32),
                   jax.ShapeDtypeStruct((2 * D, seq), BF16),
                   jax.ShapeDtypeStruct((4, CHUNK, CHUNK), F32), jax.ShapeDtypeStruct((CHUNK, D), F32),
                   jax.ShapeDtypeStruct((1, D), F32), jax.ShapeDtypeStruct((1, D), F32),
                   jax.ShapeDtypeStruct((4, GDIM, GDIM), F32), jax.ShapeDtypeStruct((32, CHUNK), F32)),
        in_specs=[row(D), _resident((2 * D, D)), row(3 * D), row(D), prev, nxt, row(D), _resident((4, CHUNK, CHUNK)),
                  _resident((4, CHUNK, CHUNK)), _resident((CHUNK, D)), _resident((1, D)), _resident((4, GDIM, GDIM)),
                  _resident((4, GDIM, GDIM)), _resident((1, D))],
        out_specs=(row(MIX0_IN), row(D), pl.BlockSpec((2 * D, ts), lambda i: (0, i)),
                   acc((4, CHUNK, CHUNK)), acc((CHUNK, D)), acc((1, D)), acc((1, D)), acc((4, GDIM, GDIM)),
                   acc((32, CHUNK))),
        scratch_shapes=_pool_scratch(ts),
        compiler_params=_params(56),
    )(dx1b, wout, za, bx, bx, bx, bg, ws, ws_t, bias, gv, wg, wg_t, scale)


def _l0_pool_bwd(dp, dz):
    seq = dp.shape[0]
    ts = 512
    n_tiles = seq // ts
    ext = ts + 2 * POOL_HALO

    def body(dp_ref, dpp_ref, dpn_ref, dz_ref, out_ref, qe_ref, *tmp_refs):
        i = pl.program_id(0)
        _fill_halo(qe_ref, dp_ref[...], dpp_ref, dpn_ref, i, n_tiles, ts)
        te = i * ts - POOL_HALO + lax.broadcasted_iota(jnp.int32, (ext, 1), 0)
        for gi, w in enumerate(POOL_WINDOWS):
            hw = w // 2
            cols = slice(gi * GDIM, (gi + 1) * GDIM)
            cnt = jnp.maximum(jnp.minimum(te + hw, seq) - jnp.maximum(te - hw, 0), 1).astype(F32)
            qe_ref[pl.ds(0, ext), cols] = qe_ref[pl.ds(0, ext), cols] / cnt
        outs = []
        for gi, w in enumerate(POOL_WINDOWS):
            cols = slice(gi * GDIM, (gi + 1) * GDIM)
            outs.append(_window_sums(qe_ref, tmp_refs, ts, cols, w, 1) - dp_ref[:, cols])
        out_ref[...] = jnp.concatenate(outs, axis=1).astype(BF16)

    prev, nxt = _halo_specs(ts, seq, D)
    row = pl.BlockSpec((ts, D), lambda i: (i, 0))
    return pl.pallas_call(
        body, grid=(n_tiles,), name="l0_pool_bwd",
        out_shape=jax.ShapeDtypeStruct(dz.shape, BF16),
        in_specs=[row, prev, nxt, pl.BlockSpec(memory_space=pl.ANY)],
        out_specs=pl.BlockSpec((ts, D), lambda i: (i, 3)),
        input_output_aliases={3: 0},
        scratch_shapes=_pool_scratch(ts),
        compiler_params=_params(32),
    )(dp, dp, dp, dz)


def _l0_in_proj_bwd(dz, w, x, g0, dx1):
    seq = x.shape[0]
    tm = 512

    def body(dz_ref, w_ref, x_ref, g_ref, dres_ref, dx_ref, dn_ref):
        @pl.when(pl.program_id(0) == 0)
        def _():
            dn_ref[...] = jnp.zeros_like(dn_ref)

        dh = _nt(dz_ref[...], w_ref[...])
        xf = x_ref[...]
        r = lax.rsqrt(jnp.mean(xf * xf, axis=1, keepdims=True) + EPS)
        xn = xf * r
        dn_ref[...] += jnp.sum(dh * xn, axis=0, keepdims=True)
        dxn = dh * g_ref[...]
        dx_ref[...] = dres_ref[...] + r * (dxn - xn * jnp.mean(dxn * xn, axis=1, keepdims=True))

    row = lambda w_: pl.BlockSpec((tm, w_), lambda i: (i, 0))
    return pl.pallas_call(
        body, grid=(seq // tm,), name="l0_in_proj_bwd",
        out_shape=(jax.ShapeDtypeStruct((seq, D), F32), jax.ShapeDtypeStruct((1, D), F32)),
        in_specs=[row(MIX0_IN), _resident((D, MIX0_IN)), row(D), _resident((1, D)), row(D)],
        out_specs=(row(D), pl.BlockSpec((1, D), lambda i: (0, 0))),
        compiler_params=_params(56),
    )(dz, w, x, g0, dx1)


def _dw_matmul(a_t, b, name, b_transposed=False, tn=1024, ts=1024, col_block=None):
    k, seq = a_t.shape
    n = b.shape[0] if b_transposed else b.shape[1]
    tn = min(n, tn)
    assert seq % ts == 0 and n % tn == 0 and (col_block is None or tn % col_block == 0)
    n_s = seq // ts
    per = 1 if col_block is None else tn // col_block

    def body(a_ref, b_ref, o_ref, ob_ref, acc_ref):
        s = pl.program_id(1)

        @pl.when(s == 0)
        def _():
            acc_ref[...] = jnp.zeros_like(acc_ref)

        acc_ref[...] += _nt(a_ref[...], b_ref[...]) if b_transposed else _mm(a_ref[...], b_ref[...])

        @pl.when(s == n_s - 1)
        def _():
            acc = acc_ref[...]
            if col_block is None:
                o_ref[...] = acc
                ob_ref[...] = acc.astype(BF16)
            else:
                for i in range(per):
                    piece = acc[:, i * col_block:(i + 1) * col_block]
                    o_ref[i] = piece
                    ob_ref[i] = piece.astype(BF16)

    b_spec = (pl.BlockSpec((tn, ts), lambda j, s: (j, s)) if b_transposed else pl.BlockSpec((ts, tn), lambda j, s: (s, j)))
    if col_block is None:
        shape, o_spec = (k, n), pl.BlockSpec((k, tn), lambda j, s: (0, j))
    else:
        shape, o_spec = (n // col_block, k, col_block), pl.BlockSpec((per, k, col_block), lambda j, s: (j, 0, 0))
    return pl.pallas_call(
        body, grid=(n // tn, n_s), name=name,
        out_shape=(jax.ShapeDtypeStruct(shape, F32), jax.ShapeDtypeStruct(shape, BF16)),
        in_specs=[pl.BlockSpec((k, ts), lambda j, s: (0, s)), b_spec],
        out_specs=(o_spec, o_spec),
        scratch_shapes=[pltpu.VMEM((k, tn), F32)],
        compiler_params=_params(56, 2),
    )(a_t, b)


ROW_TILES = 8


def _cast_shards(shards):
    n = len(shards)

    def body(*refs):
        for a in range(n):
            refs[n + a][...] = refs[a][...].astype(BF16)

    vm = pl.BlockSpec(memory_space=pltpu.VMEM)
    return pl.pallas_call(body, name="cast_weights", out_shape=[jax.ShapeDtypeStruct(t.shape, BF16) for t in shards],
                          in_specs=[vm] * n, out_specs=[vm] * n, compiler_params=_params(32, 0))(*shards)


def _adamw_math(w, g, m, v):
    m2 = ADAM_B1 * m + (1.0 - ADAM_B1) * g
    v2 = ADAM_B2 * v + (1.0 - ADAM_B2) * (g * g)
    m_hat = m2 / (1.0 - ADAM_B1 ** ADAM_STEP)
    v_hat = v2 / (1.0 - ADAM_B2 ** ADAM_STEP)
    delta = -ADAM_LR * (m_hat / (jnp.sqrt(v_hat) + ADAM_EPS) + ADAM_WD * w)
    return delta, m2, v2


def _final_sum_adamw(g_list, recv_list, me, w_list, m_list, v_list):
    n = len(w_list)

    def body(me_ref, *refs):
        own, recv, w, m, v = (refs[k * n:(k + 1) * n] for k in range(5))
        outs = [refs[(5 + k) * n:(6 + k) * n] for k in range(4)]
        for a in range(n):
            g = own[a][...]
            for k in range(N_DEV - 1):
                g = g + recv[a][k].astype(F32)
            delta, m2, v2 = _adamw_math(w[a][...], g, m[a][...], v[a][...])
            for o_ref, val in zip((outs[0][a], outs[1][a], outs[2][a], outs[3][a]), (g, delta, m2, v2)):
                o_ref[...] = val

    own_specs, flat, wire, shapes = [], [], [], []
    for t in w_list:
        rows, width = t.shape
        tr = rows // ROW_TILES
        own_specs.append(pl.BlockSpec((None, tr, width), lambda i, me: (me[0], i, 0)))
        flat.append(pl.BlockSpec((tr, width), lambda i, me: (i, 0)))
        wire.append(pl.BlockSpec((N_DEV - 1, tr, width), lambda i, me: (0, i, 0)))
        shapes.append(jax.ShapeDtypeStruct((rows, width), F32))
    out = pl.pallas_call(
        body, name="grad_sum_adamw", out_shape=shapes * 4,
        grid_spec=pltpu.PrefetchScalarGridSpec(
            num_scalar_prefetch=1, grid=(ROW_TILES,), in_specs=own_specs + wire + flat * 3, out_specs=flat * 4),
        compiler_params=_params(40),
    )(me, *g_list, *recv_list, *w_list, *m_list, *v_list)
    return [out[k * n:(k + 1) * n] for k in range(4)]


SMALL_NAMES = ("norm_0", "a_v_norm_0", "b_scale_0", "norm_1", "final_norm", "a_spatial_w_0", "a_spatial_b_0", "sink_1")
SMALL_VIEWS = ((8, LANES),) * 5 + ((4 * CHUNK, LANES), (4, LANES), (1, N_HEADS))
SMALL_ROW0 = (0, 8, 16, 24, 32, 40, 552, 560)
SMALL_ROWS = 568


def _small_sum_adamw(early, late, w_list, m_list, v_list):
    n = len(w_list)

    def body(e_ref, l_ref, *refs):
        gtot, first = e_ref[0], l_ref[0]
        for d in range(1, N_DEV):
            gtot = gtot + e_ref[d]
            first = first + l_ref[d]
        for a, ((rows, width), r0) in enumerate(zip(SMALL_VIEWS, SMALL_ROW0)):
            g = first if SMALL_NAMES[a] == "norm_0" else gtot[r0:r0 + rows, 0:width]
            delta, m2, v2 = _adamw_math(refs[a][...], g, refs[n + a][...], refs[2 * n + a][...])
            for k, val in enumerate((g, delta, m2, v2)):
                refs[(3 + k) * n + a][...] = val
        refs[7 * n][...] = gtot[LOSS_ROW:LOSS_ROW + 1, LOSS_LANE:LOSS_LANE + 1]

    vm = pl.BlockSpec(memory_space=pltpu.VMEM)
    shapes = [jax.ShapeDtypeStruct(s, F32) for s in SMALL_VIEWS]
    out = pl.pallas_call(
        body, name="small_sum_adamw", out_shape=shapes * 4 + [jax.ShapeDtypeStruct((1, 1), F32)],
        in_specs=[vm, vm] + [vm] * (3 * n), out_specs=[vm] * (4 * n + 1),
    )(early, late, *w_list, *m_list, *v_list)
    return [out[k * n:(k + 1) * n] for k in range(4)], out[4 * n]


PEER_FLIPS = tuple((fx, fy, fc) for fx in (0, 1) for fy in (0, 1) for fc in (0, 1))[1:]


def _sequencer_all_gather(blks, name, collective_id, concat_rows=False):
    n = len(blks)

    def body(*refs):
        ins, outs = refs[:n], refs[n:2 * n]
        send_sems, recv_sems, local_sems = refs[2 * n:]
        x, y, c = lax.axis_index("x"), lax.axis_index("y"), lax.axis_index("c")
        peers = [(x ^ fx, y ^ fy, c ^ fc) for fx, fy, fc in PEER_FLIPS]
        barrier = pltpu.get_barrier_semaphore()
        for peer in peers:
            pl.semaphore_signal(barrier, inc=1, device_id=peer, device_id_type=MESH)
        pl.semaphore_wait(barrier, len(peers))
        me = 4 * x + 2 * y + c

        def slot(a):
            rows = blks[a].shape[0]
            return outs[a].at[pl.ds(pl.multiple_of(me * rows, 16), rows)] if concat_rows else outs[a].at[me]

        copies = [pltpu.make_async_remote_copy(
            src_ref=ins[a], dst_ref=slot(a), send_sem=send_sems.at[k, a], recv_sem=recv_sems.at[k, a],
            device_id=peer, device_id_type=MESH) for k, peer in enumerate(peers) for a in range(n)]
        mine = [pltpu.make_async_copy(ins[a], slot(a), local_sems.at[a]) for a in range(n)]
        for cp in copies + mine:
            cp.start()
        for cp in copies + mine:
            cp.wait()

    out_shape = (lambda t: (N_DEV * t.shape[0],) + t.shape[1:]) if concat_rows else (lambda t: (N_DEV,) + t.shape)
    return pl.kernel(
        body, out_type=[jax.ShapeDtypeStruct(out_shape(t), t.dtype) for t in blks],
        mesh=plsc.ScalarSubcoreMesh(axis_name="sequencer", num_cores=1), name=name,
        scratch_types=[pltpu.SemaphoreType.DMA((7, n)), pltpu.SemaphoreType.DMA((7, n)), pltpu.SemaphoreType.DMA((n,))],
        compiler_params=pltpu.CompilerParams(collective_id=collective_id),
    )(*blks)


def _sequencer_scatter(g_list, name, collective_id):
    n = len(g_list)

    def body(*refs):
        ins, outs = refs[:n], refs[n:2 * n]
        send_sems, recv_sems = refs[2 * n:]
        x, y, c = lax.axis_index("x"), lax.axis_index("y"), lax.axis_index("c")
        peers = [(x ^ fx, y ^ fy, c ^ fc) for fx, fy, fc in PEER_FLIPS]
        barrier = pltpu.get_barrier_semaphore()
        for peer in peers:
            pl.semaphore_signal(barrier, inc=1, device_id=peer, device_id_type=MESH)
        pl.semaphore_wait(barrier, len(peers))
        copies = [pltpu.make_async_remote_copy(
            src_ref=ins[a].at[4 * px + 2 * py + pc], dst_ref=outs[a].at[k], send_sem=send_sems.at[k, a],
            recv_sem=recv_sems.at[k, a], device_id=(px, py, pc), device_id_type=MESH)
            for k, (px, py, pc) in enumerate(peers) for a in range(n)]
        for cp in copies:
            cp.start()
        for cp in copies:
            cp.wait()

    return pl.kernel(
        body, out_type=[jax.ShapeDtypeStruct((N_DEV - 1,) + g.shape[1:], g.dtype) for g in g_list],
        mesh=plsc.ScalarSubcoreMesh(axis_name="sequencer", num_cores=1), name=name,
        scratch_types=[pltpu.SemaphoreType.DMA((7, n)), pltpu.SemaphoreType.DMA((7, n))],
        compiler_params=pltpu.CompilerParams(collective_id=collective_id),
    )(*g_list)


def _direct_all_gather(blk, name):
    def body(g_ref, out_ref, send_sems, recv_sems, local_sem):
        x, y, c = lax.axis_index("x"), lax.axis_index("y"), lax.axis_index("c")
        me = 4 * x + 2 * y + c
        copies = [pltpu.make_async_remote_copy(
            src_ref=g_ref, dst_ref=out_ref.at[me], send_sem=send_sems.at[k], recv_sem=recv_sems.at[k],
            device_id=(x ^ fx, y ^ fy, c ^ fc), device_id_type=MESH) for k, (fx, fy, fc) in enumerate(PEER_FLIPS)]
        copies.append(pltpu.make_async_copy(g_ref, out_ref.at[me], local_sem))
        for cp in copies:
            cp.start()
        for cp in copies:
            cp.wait()

    any_spec = pl.BlockSpec(memory_space=pl.ANY)
    return pl.pallas_call(
        body, name=name, out_shape=jax.ShapeDtypeStruct((N_DEV,) + blk.shape, blk.dtype),
        in_specs=[any_spec], out_specs=any_spec,
        scratch_shapes=[pltpu.SemaphoreType.DMA((7,)), pltpu.SemaphoreType.DMA((7,)), pltpu.SemaphoreType.DMA],
    )(blk)


def _shard_views(w_in_0, b_group_w_0, w_out_0, w_in_1, w_out_1):
    return [w_in_0, b_group_w_0.reshape(4 * 32, GDIM), w_out_0, w_in_1, w_out_1]


def _small_views(named):
    return [named[name].reshape(view) for name, view in zip(SMALL_NAMES, SMALL_VIEWS)]


LOSS_ROW, LOSS_LANE = 560, N_HEADS


def _pack_small_grads(named, loss_part):
    rows = []
    for name, (r, w) in zip(SMALL_NAMES, SMALL_VIEWS):
        pad_r = -r % 8
        if name == "sink_1":
            t = jnp.concatenate([named[name].reshape(r, w), loss_part], axis=1)
            rows.append(jnp.pad(t, ((0, pad_r), (0, LANES - w - 1))))
        elif name in named:
            rows.append(jnp.pad(named[name].reshape(r, w), ((0, pad_r), (0, LANES - w))))
        else:
            rows.append(jnp.zeros((r + pad_r, LANES), F32))
    return jnp.concatenate(rows, axis=0)


def _device_blocks(t, axis):
    shape = t.shape
    t = t.reshape(shape[:axis] + (N_DEV, shape[axis] // N_DEV) + shape[axis + 1:])
    t = jnp.moveaxis(t, axis, 0)
    return t.reshape(N_DEV, -1, shape[-1] if axis != len(shape) - 1 else shape[-1] // N_DEV)


def kernel(x, norm_0, w_in_0, a_v_norm_0, a_spatial_w_0, a_spatial_b_0, b_group_w_0, b_scale_0, w_out_0, norm_1, w_in_1, sink_1, w_out_1, final_norm, loss_target, m_norm_0, m_w_in_0, m_a_v_norm_0, m_a_spatial_w_0, m_a_spatial_b_0, m_b_group_w_0, m_b_scale_0, m_w_out_0, m_norm_1, m_w_in_1, m_sink_1, m_w_out_1, m_final_norm, v_norm_0, v_w_in_0, v_a_v_norm_0, v_a_spatial_w_0, v_a_spatial_b_0, v_b_group_w_0, v_b_scale_0, v_w_out_0, v_norm_1, v_w_in_1, v_sink_1, v_w_out_1, v_final_norm):
    seq = x.shape[1]
    xs = x.reshape(seq, D)
    tgt = loss_target.reshape(seq, D)
    ax, ay, ac = lax.axis_index("x"), lax.axis_index("y"), lax.axis_index("c")
    me = jnp.reshape(4 * ax + 2 * ay + ac, (1,)).astype(jnp.int32)

    shards = _shard_views(w_in_0, b_group_w_0, w_out_0, w_in_1, w_out_1)
    cast = _cast_shards([shards[0], shards[1], shards[2], w_in_1.T, shards[4]])

    def l1_weights(after):
        blks, _ = lax.optimization_barrier((cast[3:5], after))
        return _sequencer_all_gather(blks, "weights_gather_l1", 2, concat_rows=True)

    blocks, received, early = {}, {}, {}
    collective_ids = {"l1": 3, "out0": 4, "in0": 5}

    def scatter(tag, own_blocks, wire_blocks):
        blocks[tag] = own_blocks
        received[tag] = _sequencer_scatter(wire_blocks, "grad_scatter_" + tag, collective_ids[tag])

    def small_early(named, loss_part):
        early["small"] = _sequencer_all_gather([_pack_small_grads(named, loss_part)], "small_grad_gather", 6)[0]

    grad_x, d_norm_0 = _local_step(xs, tgt, cast[0], cast[1:3], l1_weights, norm_0, a_v_norm_0, a_spatial_w_0,
                                   a_spatial_b_0, b_scale_0, norm_1, sink_1, final_norm, scatter, small_early)

    order = (("in0", 0), ("in0", 1), ("out0", 0), ("l1", 0), ("l1", 1))
    late = _direct_all_gather(d_norm_0.reshape(8, LANES), "norm_grad_gather")
    shards_late, _ = lax.optimization_barrier((shards, grad_x))
    big = _final_sum_adamw([blocks[t][i] for t, i in order], [received[t][i] for t, i in order], me, shards_late,
                           _shard_views(m_w_in_0, m_b_group_w_0, m_w_out_0, m_w_in_1, m_w_out_1),
                           _shard_views(v_w_in_0, v_b_group_w_0, v_w_out_0, v_w_in_1, v_w_out_1))
    weights = dict(norm_0=norm_0, a_v_norm_0=a_v_norm_0, a_spatial_w_0=a_spatial_w_0, a_spatial_b_0=a_spatial_b_0,
                   b_scale_0=b_scale_0, norm_1=norm_1, sink_1=sink_1, final_norm=final_norm)
    m_small = dict(norm_0=m_norm_0, a_v_norm_0=m_a_v_norm_0, a_spatial_w_0=m_a_spatial_w_0, a_spatial_b_0=m_a_spatial_b_0,
                   b_scale_0=m_b_scale_0, norm_1=m_norm_1, sink_1=m_sink_1, final_norm=m_final_norm)
    v_small = dict(norm_0=v_norm_0, a_v_norm_0=v_a_v_norm_0, a_spatial_w_0=v_a_spatial_w_0, a_spatial_b_0=v_a_spatial_b_0,
                   b_scale_0=v_b_scale_0, norm_1=v_norm_1, sink_1=v_sink_1, final_norm=v_final_norm)
    small, loss = _small_sum_adamw(early["small"], late, _small_views(weights), _small_views(m_small),
                                   _small_views(v_small))

    def in_order(kind):
        b = [b_.reshape(s_.shape) for b_, s_ in zip(big[kind], (w_in_0, b_group_w_0, w_out_0, w_in_1, w_out_1))]
        s = {name: t.reshape(weights[name].shape) for name, t in zip(SMALL_NAMES, small[kind])}
        return [s["norm_0"], b[0], s["a_v_norm_0"], s["a_spatial_w_0"], s["a_spatial_b_0"], b[1], s["b_scale_0"], b[2],
                s["norm_1"], b[3], s["sink_1"], b[4], s["final_norm"]]

    return (loss[0, 0], grad_x.reshape(1, seq, D), *in_order(0), *in_order(1), *in_order(2), *in_order(3))


def _local_step(xs, tgt, win0_shard, l0_shards, l1_weights, norm_0, a_v_norm_0, a_spatial_w_0, a_spatial_b_0, b_scale_0,
                norm_1, sink_1, final_norm, scatter, small_early):
    seq = xs.shape[0]
    ws = a_spatial_w_0.astype(BF16)
    ws_t = jnp.swapaxes(ws, 1, 2)
    bias = jnp.repeat(a_spatial_b_0.T, GDIM, axis=1)
    g0, gv, scale, g1, gf = (t.reshape(1, D) for t in (norm_0, a_v_norm_0, b_scale_0, norm_1, final_norm))
    cos_t, sin_t = _rope_tables_t(seq)

    za, bx, bg, h0_t, win0, g_wg, wout0 = _l0_in_proj(xs, g0, win0_shard, l0_shards)
    win1_t, wout1 = l1_weights(za)
    wg = g_wg.reshape(N_DEV, 4, 32, GDIM).transpose(1, 0, 2, 3).reshape(4, GDIM, GDIM)
    wg_t = jnp.swapaxes(wg, 1, 2)
    x1 = _l0_mix_fwd(za, bx, bg, xs, ws, bias, gv, wg, scale, wout0)
    win1_t, wout1, x1 = lax.optimization_barrier((win1_t, wout1, x1))
    qt, kt, vt, gatet, h1_t = _l1_in_proj(x1, g1, win1_t, cos_t, sin_t)
    dx2, dx2b, att, lse, loss_part, d_gf, d_wout1, d_wout1_wire = _l1_attn_fwd(
        qt, kt, vt, gatet, x1, tgt, wout1, gf, sink_1)

    dq_r, dgate, dk_pad, dv_pad, d_sink = _l1_attn_bwd(dx2b, wout1, qt, kt, vt, gatet, att, lse, sink_1)
    dk_r = dk_pad[:, BLK:BLK + seq]
    dv = dv_pad[:, BLK:BLK + seq]
    dx1, dx1b, dz1_t, d_g1 = _l1_in_proj_bwd(dq_r, dk_r, dv, dgate, cos_t, sin_t, win1_t, x1, g1, dx2)
    d_win1, d_win1_wire = _dw_matmul(h1_t, dz1_t, "dw_in_1", b_transposed=True, tn=1280, col_block=MIX1_IN // N_DEV)
    rows = lambda t: t.reshape(N_DEV, t.shape[0] // N_DEV, t.shape[1])
    scatter("l1", [d_win1, rows(d_wout1)], [d_win1_wire, rows(d_wout1_wire)])

    dz0, dp, cat_t, d_ws, _, d_gv, d_scale, d_wg, d_b = _l0_mix_bwd(
        dx1b, wout0, za, bx, bg, ws, ws_t, bias, gv, wg, wg_t, scale)
    dz0 = _l0_pool_bwd(dp, dz0)
    d_win0, d_win0_wire = _dw_matmul(h0_t, dz0, "dw_in_0", tn=1280, col_block=MIX0_IN // N_DEV)
    d_wg_blocks = _device_blocks(d_wg, 1)
    scatter("in0", [d_win0, d_wg_blocks], [d_win0_wire, d_wg_blocks])
    cat_t, _ = lax.optimization_barrier((cat_t, d_win0))
    d_wout0, d_wout0_wire = _dw_matmul(cat_t, dx1b, "dw_out_0")
    scatter("out0", [rows(d_wout0)], [rows(d_wout0_wire)])
    small_early(dict(a_v_norm_0=d_gv, a_spatial_w_0=d_ws, a_spatial_b_0=d_b.reshape(4, 8, CHUNK)[:, 0, :],
                     b_scale_0=d_scale, norm_1=d_g1, sink_1=d_sink[:, 0], final_norm=d_gf), loss_part)
    dz0, _ = lax.optimization_barrier((dz0, d_wout0))
    return _l0_in_proj_bwd(dz0, win0, xs, g0, dx1)
```

```python
import jax
import jax.numpy as jnp
from jax import lax
from jax.experimental import pallas as pl
from jax.experimental.pallas import tpu as pltpu
from jax.experimental.pallas import tpu_sc as plsc

F32 = jnp.float32
BF16 = jnp.bfloat16

D = 1024
EPS = 1e-6
NEG_INF = -1e30
CHUNK = 128
A_GROUPS = 4
POOL_WINDOWS = (2, 4, 8, 16)
POOL_HALO = 8
GDIM = 256
N_HEADS = 16
N_KV = 4
GQA = 4
HD = 64
BLK = 128
ROT_HALF = 8
ROPE_THETA = 500000.0
SCALE = HD ** -0.5
MIX0_IN = 5 * D
MIX1_IN = 2560
KV_W = N_KV * HD
Q_ROWS, K_ROWS, V_ROWS, G_ROWS = (0, D), (D, D + KV_W), (D + KV_W, D + 2 * KV_W), (D + 2 * KV_W, MIX1_IN)
TQ = 512

ADAM_LR = 0.001
ADAM_B1 = 0.9
ADAM_B2 = 0.999
ADAM_EPS = 1e-08
ADAM_WD = 0.01
ADAM_STEP = 10

N_DEV = 8
LANES = 128
MIB = 2 ** 20
MESH = pl.DeviceIdType.MESH


def _params(limit_mib, n_axes=1):
    return pltpu.CompilerParams(vmem_limit_bytes=limit_mib * MIB, dimension_semantics=("arbitrary",) * n_axes)


def _resident(shape):
    nd = len(shape)
    return pl.BlockSpec(shape, lambda *_: (0,) * nd, pipeline_mode=pl.Buffered(1))


def _gelu(x):
    k = 0.7978845608028654
    return 0.5 * x * (1.0 + jnp.tanh(k * (x + 0.044715 * x * x * x)))


def _gelu_and_grad(x):
    k = 0.7978845608028654
    x2 = x * x
    t = jnp.tanh(k * (x + 0.044715 * x * x2))
    g = 0.5 * x * (1.0 + t)
    dg = 0.5 * (1.0 + t) + 0.5 * x * (1.0 - t * t) * (k * (1.0 + 3.0 * 0.044715 * x2))
    return g, dg


def _silu_and_grad(x):
    s = jax.nn.sigmoid(x)
    return x * s, s * (1.0 + x * (1.0 - s))


def _nt(a, b):
    return lax.dot_general(a, b, (((1,), (1,)), ((), ())), preferred_element_type=F32)


def _tn(a, b):
    return lax.dot_general(a, b, (((0,), (0,)), ((), ())), preferred_element_type=F32)


def _mm(a, b):
    return jnp.dot(a, b, preferred_element_type=F32)


def _rope_tables_t(seq):
    inv = ROPE_THETA ** (-jnp.arange(0, 2 * ROT_HALF, 2, dtype=F32) / (2 * ROT_HALF))
    ang = inv[:, None] * jnp.arange(seq, dtype=F32)[None, :]
    return jnp.cos(ang), jnp.sin(ang)


def _rope_t(z, c, s, n_heads, sign):
    parts = []
    for h in range(n_heads):
        b = h * HD
        x1, x2 = z[b:b + ROT_HALF], z[b + ROT_HALF:b + 2 * ROT_HALF]
        if sign > 0:
            parts += [x1 * c - x2 * s, x2 * c + x1 * s]
        else:
            parts += [x1 * c + x2 * s, x2 * c - x1 * s]
        parts.append(z[b + 2 * ROT_HALF:b + HD])
    return jnp.concatenate(parts, axis=0)


N_CHIPS = 4
CHIP_COLS = MIX0_IN // N_CHIPS
IN_PROJ_ROWS, IN_PROJ_CHUNK = 512, 256
IN_PROJ_PIECES = (
    ((0, 0, CHIP_COLS, 0),),
    ((0, CHIP_COLS, CHIP_COLS, 0),),
    ((0, 2 * CHIP_COLS, 3 * D - 2 * CHIP_COLS, 0), (1, 0, 3 * CHIP_COLS - 3 * D, 3 * D - 2 * CHIP_COLS)),
    ((1, 3 * CHIP_COLS - 3 * D, 4 * D - 3 * CHIP_COLS, 0), (2, 0, D, 4 * D - 3 * CHIP_COLS)),
)


def _l0_in_proj(x, g0, w_shard, later_shards):
    seq = x.shape[0]
    tm = 1024
    n = seq // tm
    f32_cols = max(width for pieces in IN_PROJ_PIECES for o, _, width, _ in pieces if o == 1)
    shard_cols = w_shard.shape[1]
    n_arr = 1 + len(later_shards)
    later = range(1, n_arr)
    assert 2 * shard_cols == CHIP_COLS and seq % tm == 0 and n >= 4

    def body(*refs):
        x_ref, g_ref = refs[:2]
        ins = refs[2:2 + n_arr]
        za_ref, bx_ref, bg_ref, ht_ref = refs[2 + n_arr:6 + n_arr]
        gathered = refs[6 + n_arr:6 + 2 * n_arr]
        h_all, w_buf, z32, z16, send_sems, recv_sems, local_sems, load_sems, out_sems = refs[6 + 2 * n_arr:]
        p, i = pl.program_id(0), pl.program_id(1)
        ax, ay, ac = lax.axis_index("x"), lax.axis_index("y"), lax.axis_index("c")
        me, sibling = (ax, ay, ac), (ax, ay, 1 - ac)
        chips = [(ax, ay), (1 - ax, ay), (ax, 1 - ay), (1 - ax, 1 - ay)]
        outs = (za_ref, bx_ref, bg_ref)

        def slot(a, px, py, pc):
            dev = 4 * px + 2 * py + pc
            if a == 0:
                return gathered[0].at[:, pl.ds(pl.multiple_of(dev * shard_cols, LANES), shard_cols)]
            rows = later_shards[a - 1].shape[0]
            return gathered[a].at[pl.ds(pl.multiple_of(dev * rows, 16), rows)]

        def copy(k, a, block, to, from_input=False):
            return pltpu.make_async_remote_copy(
                src_ref=ins[a] if from_input else slot(a, *block), dst_ref=slot(a, *block),
                send_sem=send_sems.at[k, a], recv_sem=recv_sems.at[k, a], device_id=to, device_id_type=MESH)

        def to_sibling(a):
            return copy(0, a, me, sibling, from_input=True)

        def send(j, a):
            return copy(j, a, me, (*chips[j], ac), from_input=True)

        def landed(j, a):
            return copy(j, a, (*chips[j], ac), me)

        def forward(j, a):
            return copy(3 + j, a, (*chips[j], ac), sibling)

        def forwarded(j, a):
            return copy(3 + j, a, (*chips[j], 1 - ac), me)

        def mine(a):
            return pltpu.make_async_copy(ins[a], slot(a, *me), local_sems.at[a])

        def load(chip, q):
            px, py = chip
            cols = pl.ds(pl.multiple_of((2 * px + py) * CHIP_COLS, LANES), CHIP_COLS)
            return pltpu.make_async_copy(gathered[0].at[:, cols], w_buf.at[q % 2], load_sems.at[q % 2])

        def out_copies(q, tile, stage):
            cps = []
            for k, (o, c0, width, z0) in enumerate(IN_PROJ_PIECES[q]):
                src = z32.at[stage, :, pl.ds(0, width)] if o == 1 else z16.at[stage, :, pl.ds(z0, width)]
                dst = outs[o].at[pl.ds(pl.multiple_of(tile * tm, tm), tm), pl.ds(c0, width)]
                cps.append(pltpu.make_async_copy(src, dst, out_sems.at[stage, k]))
            return cps

        @pl.when((p == 0) & (i == 0))
        def _():
            for a in range(n_arr):
                mine(a).start()
                to_sibling(a).start()
            send(1, 0).start()
            send(2, 0).start()
            copy(0, 0, sibling, me).wait_recv()
            mine(0).wait()
            load(chips[0], 0).start()

        for j in range(1, N_CHIPS):
            @pl.when((p == j - 1) & (i == n - 2))
            def _(j=j):
                landed(j, 0).wait_recv()
                forward(j, 0).start()
                if j == 1:
                    send(1, 0).wait_send()
                    send(2, 0).wait_send()
                    send(3, 0).start()
                    for a in later:
                        for jj in range(1, N_CHIPS):
                            send(jj, a).start()

            @pl.when((p == j - 1) & (i == n - 1))
            def _(j=j):
                forwarded(j, 0).wait_recv()
                load(chips[j], j).start()

        @pl.when((p == N_CHIPS - 1) & (i == n - 4))
        def _():
            for jj in range(1, N_CHIPS):
                for a in later:
                    landed(jj, a).wait_recv()
                    forward(jj, a).start()

        @pl.when(i == 0)
        def _():
            load(chips[0], p).wait()

        @pl.when(p == 0)
        def _():
            xf = x_ref[...]
            r = lax.rsqrt(jnp.mean(xf * xf, axis=1, keepdims=True) + EPS)
            h = (xf * r * g_ref[...]).astype(BF16)
            ht_ref[...] = h.T
            h_all[pl.ds(pl.multiple_of(i * tm, tm), tm), :] = h

        def chip_of_pass(pp):
            return (2 * ax + ay) ^ ((pp >> 1) | ((pp & 1) << 1))

        step = p * n + i
        stage = step % 2
        for q in range(N_CHIPS):
            @pl.when((step >= 2) & (chip_of_pass((step - 2) // n) == q))
            def _(q=q):
                for cp in out_copies(q, (step - 2) % n, stage):
                    cp.wait()

        for q in range(N_CHIPS):
            @pl.when(chip_of_pass(p) == q)
            def _(q=q):
                f32_from = [(z0, width) for o, _, width, z0 in IN_PROJ_PIECES[q] if o == 1]
                for r0 in range(0, tm, IN_PROJ_ROWS):
                    rows = pl.ds(r0, IN_PROJ_ROWS)
                    h = h_all[pl.ds(pl.multiple_of(i * tm + r0, IN_PROJ_ROWS), IN_PROJ_ROWS), :]
                    for c0 in range(0, CHIP_COLS, IN_PROJ_CHUNK):
                        z = _mm(h, w_buf[p % 2, :, pl.ds(c0, IN_PROJ_CHUNK)])
                        z16[stage, rows, pl.ds(c0, IN_PROJ_CHUNK)] = z.astype(BF16)
                        for z0, width in f32_from:
                            if z0 <= c0 < z0 + width:
                                z32[stage, rows, pl.ds(c0 - z0, IN_PROJ_CHUNK)] = z
                for cp in out_copies(q, i, stage):
                    cp.start()

        last = (p == N_CHIPS - 1) & (i == n - 1)
        for q in range(N_CHIPS):
            @pl.when(last & (chip_of_pass(p) == q))
            def _(q=q):
                for cp in out_copies(q, n - 2, 1 - stage) + out_copies(q, n - 1, stage):
                    cp.wait()

        @pl.when(last)
        def _():
            for a in later:
                copy(0, a, sibling, me).wait_recv()
                for jj in range(1, N_CHIPS):
                    forwarded(jj, a).wait_recv()
                    send(jj, a).wait_send()
                mine(a).wait()
            send(3, 0).wait_send()
            for a in range(n_arr):
                to_sibling(a).wait_send()
                for jj in range(1, N_CHIPS):
                    forward(jj, a).wait_send()

    any_spec = pl.BlockSpec(memory_space=pl.ANY)
    first_pass_tile = lambda p, i: jnp.where(p == 0, i, n - 1)
    return pl.pallas_call(
        body, grid=(N_CHIPS, n), name="l0_in_proj",
        out_shape=[jax.ShapeDtypeStruct((seq, 3 * D), BF16), jax.ShapeDtypeStruct((seq, D), F32),
                   jax.ShapeDtypeStruct((seq, D), BF16), jax.ShapeDtypeStruct((D, seq), BF16),
                   jax.ShapeDtypeStruct((D, MIX0_IN), BF16)]
        + [jax.ShapeDtypeStruct((N_DEV * t.shape[0], t.shape[1]), t.dtype) for t in later_shards],
        in_specs=[pl.BlockSpec((tm, D), lambda p, i: (first_pass_tile(p, i), 0)), _resident((1, D))] + [any_spec] * n_arr,
        out_specs=[any_spec, any_spec, any_spec, pl.BlockSpec((D, tm), lambda p, i: (0, first_pass_tile(p, i)))]
        + [any_spec] * n_arr,
        scratch_shapes=[pltpu.VMEM((seq, D), BF16), pltpu.VMEM((2, D, CHIP_COLS), BF16),
                        pltpu.VMEM((2, tm, f32_cols), F32), pltpu.VMEM((2, tm, CHIP_COLS), BF16),
                        pltpu.SemaphoreType.DMA((7, n_arr)), pltpu.SemaphoreType.DMA((7, n_arr)),
                        pltpu.SemaphoreType.DMA((n_arr,)), pltpu.SemaphoreType.DMA((2,)), pltpu.SemaphoreType.DMA((2, 2))],
        compiler_params=_params(56, 2),
    )(x, g0, w_shard, *later_shards)


POOL_EXT = 40


def _fill_halo(ext_ref, cur, prev_ref, next_ref, i, n_tiles, ts):
    ext_ref[pl.ds(0, POOL_HALO), :] = jnp.where(i > 0, prev_ref[...], 0.0)
    ext_ref[pl.ds(POOL_HALO, ts), :] = cur
    ext_ref[pl.ds(POOL_HALO + ts, POOL_HALO), :] = jnp.where(i < n_tiles - 1, next_ref[...], 0.0)
    ext_ref[pl.ds(2 * POOL_HALO + ts, POOL_EXT - 2 * POOL_HALO), :] = jnp.zeros((POOL_EXT - 2 * POOL_HALO, D), F32)


def _window_sums(src_ref, tmp_refs, ts, cols, w, shift):
    if w == 2:
        return src_ref[pl.ds(POOL_HALO - 1 + shift, ts), cols] + src_ref[pl.ds(POOL_HALO + shift, ts), cols]
    d2, d4, d8 = tmp_refs
    n2, n4, n8 = ts + 32, ts + 24, ts + 16
    d2[pl.ds(0, n2), :] = src_ref[pl.ds(0, n2), cols] + src_ref[pl.ds(1, n2), cols]
    if w == 4:
        return d2[pl.ds(POOL_HALO - 2 + shift, ts), :] + d2[pl.ds(POOL_HALO + shift, ts), :]
    d4[pl.ds(0, n4), :] = d2[pl.ds(0, n4), :] + d2[pl.ds(2, n4), :]
    if w == 8:
        return d4[pl.ds(POOL_HALO - 4 + shift, ts), :] + d4[pl.ds(POOL_HALO + shift, ts), :]
    d8[pl.ds(0, n8), :] = d4[pl.ds(0, n8), :] + d4[pl.ds(4, n8), :]
    return d8[pl.ds(shift, ts), :] + d8[pl.ds(POOL_HALO + shift, ts), :]


def _pool_scratch(ts):
    return [pltpu.VMEM((ts + POOL_EXT, D), F32)] + [pltpu.VMEM((ts + POOL_EXT, GDIM), F32)] * 3


def _pool_forward(xe_ref, tmp_refs, ts, t0, seq):
    tg = t0 + lax.broadcasted_iota(jnp.int32, (ts, 1), 0)
    outs = []
    for gi, w in enumerate(POOL_WINDOWS):
        hw = w // 2
        cols = slice(gi * GDIM, (gi + 1) * GDIM)
        cnt = (jnp.minimum(tg + hw, seq) - jnp.maximum(tg - hw, 0)).astype(F32)
        outs.append(_window_sums(xe_ref, tmp_refs, ts, cols, w, 0) / cnt - xe_ref[pl.ds(POOL_HALO, ts), cols])
    return jnp.concatenate(outs, axis=1)


def _spatial_mix(ws_ref, vnb, bias, ts):
    rows = []
    for c in range(ts // CHUNK):
        vc = vnb[c * CHUNK:(c + 1) * CHUNK, :]
        rows.append(jnp.concatenate(
            [_mm(ws_ref[h], vc[:, h * GDIM:(h + 1) * GDIM]) for h in range(A_GROUPS)], axis=1) + bias)
    return jnp.concatenate(rows, axis=0)


def _halo_specs(ts, seq, width):
    per = ts // POOL_HALO
    last = seq // POOL_HALO - 1
    prev = pl.BlockSpec((POOL_HALO, width), lambda i: (jnp.maximum(i * per - 1, 0), 0))
    nxt = pl.BlockSpec((POOL_HALO, width), lambda i: (jnp.minimum((i + 1) * per, last), 0))
    return prev, nxt


def _l0_mix_fwd(za, bx, bg, x, ws, bias, gv, wg, scale, wout):
    seq = x.shape[0]
    ts = 512
    n_tiles = seq // ts

    def body(za_ref, bx_ref, bxp_ref, bxn_ref, bg_ref, x_ref, ws_ref, bias_ref, gv_ref, wg_ref, sc_ref, wo_ref,
             x1_ref, xe_ref, *tmp_refs):
        i = pl.program_id(0)
        vg = _gelu(za_ref[:, D:2 * D].astype(F32))
        rv = lax.rsqrt(jnp.mean(vg * vg, axis=1, keepdims=True) + EPS)
        vnb = (vg * rv * gv_ref[...]).astype(BF16)
        mixed = _spatial_mix(ws_ref, vnb, bias_ref[...], ts)

        _fill_halo(xe_ref, bx_ref[...], bxp_ref, bxn_ref, i, n_tiles, ts)
        pb = _pool_forward(xe_ref, tmp_refs, ts, i * ts, seq).astype(BF16)
        ypre = jnp.concatenate([_mm(pb[:, g * GDIM:(g + 1) * GDIM], wg_ref[g]) for g in range(4)], axis=1)

        u = _gelu(za_ref[:, 0:D].astype(F32))
        ag = za_ref[:, 2 * D:3 * D].astype(F32)
        ya = (u * mixed * (ag * jax.nn.sigmoid(ag))).astype(BF16)
        out_a = _mm(ya, wo_ref[0:D, :])

        bgf = bg_ref[...].astype(F32)
        yb = (ypre * sc_ref[...] * (bgf * jax.nn.sigmoid(bgf))).astype(BF16)
        x1_ref[...] = x_ref[...] + out_a + _mm(yb, wo_ref[D:2 * D, :])

    prev, nxt = _halo_specs(ts, seq, D)
    row = lambda w: pl.BlockSpec((ts, w), lambda i: (i, 0))
    return pl.pallas_call(
        body, grid=(n_tiles,), name="l0_mix_fwd",
        out_shape=jax.ShapeDtypeStruct((seq, D), F32),
        in_specs=[row(3 * D), row(D), prev, nxt, row(D), row(D), _resident((4, CHUNK, CHUNK)), _resident((CHUNK, D)),
                  _resident((1, D)), _resident((4, GDIM, GDIM)), _resident((1, D)), _resident((2 * D, D))],
        out_specs=row(D),
        scratch_shapes=_pool_scratch(ts),
        compiler_params=_params(56),
    )(za, bx, bx, bx, bg, x, ws, bias, gv, wg, scale, wout)


def _l1_in_proj(x1, g1, w_t, cos_t, sin_t):
    seq = x1.shape[0]
    tm = 512

    def body(x_ref, g_ref, wt_ref, c_ref, s_ref, q_ref, k_ref, v_ref, gate_ref, ht_ref):
        xf = x_ref[...]
        r = lax.rsqrt(jnp.mean(xf * xf, axis=1, keepdims=True) + EPS)
        ht = (xf * r * g_ref[...]).astype(BF16).T
        ht_ref[...] = ht
        c, s = c_ref[...], s_ref[...]
        q_ref[...] = (_rope_t(_mm(wt_ref[Q_ROWS[0]:Q_ROWS[1], :], ht), c, s, N_HEADS, 1) * SCALE).astype(BF16)
        k_ref[...] = _rope_t(_mm(wt_ref[K_ROWS[0]:K_ROWS[1], :], ht), c, s, N_KV, 1).astype(BF16)
        v_ref[...] = _mm(wt_ref[V_ROWS[0]:V_ROWS[1], :], ht).astype(BF16)
        gate_ref[...] = _mm(wt_ref[G_ROWS[0]:G_ROWS[1], :], ht).astype(BF16)

    col = lambda rows: pl.BlockSpec((rows, tm), lambda i: (0, i))
    return pl.pallas_call(
        body, grid=(seq // tm,), name="l1_in_proj",
        out_shape=(jax.ShapeDtypeStruct((D, seq), BF16), jax.ShapeDtypeStruct((KV_W, seq), BF16),
                   jax.ShapeDtypeStruct((KV_W, seq), BF16), jax.ShapeDtypeStruct((D, seq), BF16),
                   jax.ShapeDtypeStruct((D, seq), BF16)),
        in_specs=[pl.BlockSpec((tm, D), lambda i: (i, 0)), _resident((1, D)), _resident((MIX1_IN, D)), col(ROT_HALF),
                  col(ROT_HALF)],
        out_specs=(col(D), col(KV_W), col(KV_W), col(D), col(D)),
        compiler_params=_params(48),
    )(x1, g1, w_t, cos_t, sin_t)


def _band_specs_t(nb, clamp_i):
    per = TQ // BLK
    prev = pl.BlockSpec((KV_W, BLK), lambda i: (0, jnp.maximum(clamp_i(i) * per - 1, 0)))
    cur = pl.BlockSpec((KV_W, TQ), lambda i: (0, clamp_i(i)))
    nxt = pl.BlockSpec((KV_W, BLK), lambda i: (0, jnp.minimum((clamp_i(i) + 1) * per, nb - 1)))
    return [prev, cur, nxt]


def _fill_band(buf, p_ref, c_ref, n_ref):
    buf[:, 0:BLK] = p_ref[...]
    buf[:, BLK:BLK + TQ] = c_ref[...]
    buf[:, BLK + TQ:2 * BLK + TQ] = n_ref[...]


def _band_bias_t(n, nb):
    c = lax.broadcasted_iota(jnp.int32, (BLK, BLK), 0)
    r = lax.broadcasted_iota(jnp.int32, (BLK, BLK), 1)
    first = jnp.where((c >= r) & (n > 0), 0.0, NEG_INF).astype(F32)
    last = jnp.where((c <= r) & (n < nb - 1), 0.0, NEG_INF).astype(F32)
    return jnp.concatenate([first] * HPP, axis=1), jnp.concatenate([last] * HPP, axis=1)


def _masked(st, bias):
    first, last = bias
    return jnp.concatenate([st[0:BLK] + first, st[BLK:2 * BLK], st[2 * BLK:3 * BLK] + last], axis=0)


AUG = 16


def _ones_rows(n_ones, width):
    return (lax.broadcasted_iota(jnp.int32, (AUG, width), 0) < n_ones).astype(BF16)


def _minus_rows(vec):
    hi = vec.astype(BF16).astype(F32)
    lo = vec - hi
    return jnp.concatenate([-hi, -lo, jnp.zeros((AUG - 2, vec.shape[1]), F32)], axis=0).astype(BF16)


HPP = GQA
FWD_GROUP, BWD_GROUP = 2, 1
BWD_AHEAD = 1


def _heads_t(ref, h0, c0):
    return jnp.concatenate([ref[(h0 + g) * HD:(h0 + g + 1) * HD, c0:c0 + BLK] for g in range(HPP)], axis=1)


def _row4(ref, h0, c0):
    return jnp.concatenate([ref[h0 + g:h0 + g + 1, c0:c0 + BLK] for g in range(HPP)], axis=1)


def _sink_row(sink_ref, h0):
    return jnp.concatenate([jnp.full((1, BLK), sink_ref[h0 + g], F32) for g in range(HPP)], axis=1)


def _l1_attn_fwd(qt, kt, vt, gatet, x1, tgt, wout, gf, sink):
    seq = x1.shape[0]
    nq, nb = seq // TQ, seq // BLK

    def body(q_ref, gate_ref, kp_ref, k_ref, kn_ref, vp_ref, v_ref, vn_ref, x1_ref, tgt_ref, wo_ref, gf_ref, sink_ref,
             dx2_ref, dx2b_ref, att_ref, lse_ref, loss_ref, dgf_ref, dwo_ref, dwo_wire_ref, kbuf, vbuf, att_scr):
        i = pl.program_id(0)

        @pl.when(i == 0)
        def _():
            loss_ref[...] = jnp.zeros_like(loss_ref)
            dgf_ref[...] = jnp.zeros_like(dgf_ref)
            dwo_ref[...] = jnp.zeros_like(dwo_ref)

        _fill_band(kbuf, kp_ref, k_ref, kn_ref)
        _fill_band(vbuf, vp_ref, v_ref, vn_ref)
        ones_row = _ones_rows(1, 3 * BLK)
        groups = [list(range(0, N_HEADS, HPP))[g:g + FWD_GROUP] for g in range(0, N_HEADS // HPP, FWD_GROUP)]
        work = [(j, grp) for j in range(TQ // BLK) for grp in groups]

        def scores(j, passes):
            c0 = j * BLK
            bias = _band_bias_t(i * (TQ // BLK) + j, nb)
            st = dict(c0=c0, passes=passes)
            st["kv_rows"] = [slice(h0 // GQA * HD, (h0 // GQA + 1) * HD) for h0 in passes]
            st["sts"] = [_masked(_tn(kbuf[rows, c0:c0 + 3 * BLK], _heads_t(q_ref, h0, c0)), bias)
                         for h0, rows in zip(passes, st["kv_rows"])]
            return st

        def softmaxes(st):
            st["sks"] = [_sink_row(sink_ref, h0) for h0 in st["passes"]]
            st["ms"] = [jnp.maximum(jnp.max(s_, axis=0, keepdims=True), sk) for s_, sk in zip(st["sts"], st["sks"])]
            st["ps"] = [jnp.exp(s_ - m).astype(BF16) for s_, m in zip(st["sts"], st["ms"])]

        def values(st):
            c0, passes = st["c0"], st["passes"]
            pvs = [_mm(jnp.concatenate([vbuf[rows, c0:c0 + 3 * BLK], ones_row], axis=0), p)
                   for rows, p in zip(st["kv_rows"], st["ps"])]
            lse_rows = []
            for h0, pv, m, sk in zip(passes, pvs, st["ms"], st["sks"]):
                den = pv[HD:HD + 1, :] + jnp.exp(sk - m)
                ot = pv[0:HD, :] / den
                lse = m + jnp.log(den)
                for g in range(HPP):
                    h = h0 + g
                    att_scr[h * HD:(h + 1) * HD, c0:c0 + BLK] = ot[:, g * BLK:(g + 1) * BLK]
                    lse_rows.append(lse[:, g * BLK:(g + 1) * BLK])
            lse_ref[passes[0]:passes[0] + len(lse_rows), c0:c0 + BLK] = jnp.concatenate(lse_rows, axis=0)

        state = scores(*work[0])
        for nxt in work[1:] + [None]:
            following = scores(*nxt) if nxt is not None else None
            softmaxes(state)
            values(state)
            state = following

        att = att_scr[...]
        gate = gate_ref[...].astype(F32)
        yt = (att * (gate * jax.nn.sigmoid(gate))).astype(BF16)
        att_ref[...] = att.astype(BF16)
        x2 = x1_ref[...] + _mm(yt.T, wo_ref[...])
        r = lax.rsqrt(jnp.mean(x2 * x2, axis=1, keepdims=True) + EPS)
        xn = x2 * r
        diff = xn * gf_ref[...] - tgt_ref[...]
        loss_ref[...] += 0.5 * jnp.sum(jnp.mean(diff * diff, axis=1, keepdims=True), axis=0, keepdims=True)
        dout = diff * (1.0 / D)
        dgf_ref[...] += jnp.sum(dout * xn, axis=0, keepdims=True)
        dxn = dout * gf_ref[...]
        dx2 = r * (dxn - xn * jnp.mean(dxn * xn, axis=1, keepdims=True))
        dx2_ref[...] = dx2
        dx2b = dx2.astype(BF16)
        dx2b_ref[...] = dx2b
        dwo_ref[...] += _mm(yt, dx2b)

        @pl.when(i == nq - 1)
        def _():
            dwo_wire_ref[...] = dwo_ref[...].astype(BF16)

    ident = lambda i: i
    row = pl.BlockSpec((TQ, D), lambda i: (i, 0))
    col = lambda rows: pl.BlockSpec((rows, TQ), lambda i: (0, i))
    whole = pl.BlockSpec((D, D), lambda i: (0, 0))
    return pl.pallas_call(
        body, grid=(nq,), name="l1_attn_fwd",
        out_shape=(jax.ShapeDtypeStruct((seq, D), F32), jax.ShapeDtypeStruct((seq, D), BF16),
                   jax.ShapeDtypeStruct((D, seq), BF16),
                   jax.ShapeDtypeStruct((N_HEADS, seq), F32), jax.ShapeDtypeStruct((1, 1), F32),
                   jax.ShapeDtypeStruct((1, D), F32), jax.ShapeDtypeStruct((D, D), F32), jax.ShapeDtypeStruct((D, D), BF16)),
        in_specs=[col(D), col(D)] + _band_specs_t(nb, ident) + _band_specs_t(nb, ident) + [
            row, row, _resident((D, D)), _resident((1, D)), pl.BlockSpec(memory_space=pltpu.SMEM)],
        out_specs=(row, row, col(D), col(N_HEADS), pl.BlockSpec((1, 1), lambda i: (0, 0)),
                   pl.BlockSpec((1, D), lambda i: (0, 0)), whole, whole),
        scratch_shapes=[pltpu.VMEM((KV_W, TQ + 2 * BLK), BF16), pltpu.VMEM((KV_W, TQ + 2 * BLK), BF16),
                        pltpu.VMEM((D, TQ), F32)],
        compiler_params=_params(56),
    )(qt, gatet, kt, kt, kt, vt, vt, vt, x1, tgt, wout, gf, sink)


def _l1_attn_bwd(dx2b, wout, qt, kt, vt, gatet, att, lse, sink):
    seq = dx2b.shape[0]
    nq, nb = seq // TQ, seq // BLK

    def body(dx_ref, wo_ref, q_ref, gate_ref, kp_ref, k_ref, kn_ref, vp_ref, v_ref, vn_ref, att_ref, lse_ref, sink_ref,
             dq_ref, dgate_ref, dk_ref, dv_ref, dsink_ref, kbuf, vbuf, dkacc, dvacc, dat_scr, delta_scr, dsacc):
        i = pl.program_id(0)

        @pl.when(i == 0)
        def _():
            dkacc[...] = jnp.zeros_like(dkacc)
            dvacc[...] = jnp.zeros_like(dvacc)
            dsacc[...] = jnp.zeros_like(dsacc)

        @pl.when(i > 0)
        def _():
            for acc in (dkacc, dvacc):
                acc[:, 0:2 * BLK] = acc[:, TQ:TQ + 2 * BLK]
                acc[:, 2 * BLK:2 * BLK + TQ] = jnp.zeros((KV_W, TQ), F32)

        @pl.when(i < nq)
        def _():
            _fill_band(kbuf, kp_ref, k_ref, kn_ref)
            _fill_band(vbuf, vp_ref, v_ref, vn_ref)
            dyt = _nt(wo_ref[...], dx_ref[...])
            sg, dsg = _silu_and_grad(gate_ref[...].astype(F32))
            attf = att_ref[...].astype(F32)
            dat = dyt * sg
            dat_scr[...] = dat.astype(BF16)
            dgate_ref[...] = (dyt * attf * dsg).astype(BF16)
            dl = dat * attf
            delta_scr[...] = jnp.concatenate(
                [jnp.sum(dl[h * HD:(h + 1) * HD, :], axis=0, keepdims=True) for h in range(N_HEADS)], axis=0)
            ones_rows = _ones_rows(2, 3 * BLK)
            groups = [list(range(0, N_HEADS, HPP))[g:g + BWD_GROUP] for g in range(0, N_HEADS // HPP, BWD_GROUP)]
            work = [(j, grp) for j in range(TQ // BLK) for grp in groups]

            def scores(j, passes):
                c0 = j * BLK
                st = dict(c0=c0, passes=passes, bias=_band_bias_t(i * (TQ // BLK) + j, nb))
                st["kv_rows"] = [slice(h0 // GQA * HD, (h0 // GQA + 1) * HD) for h0 in passes]
                st["q4s"] = [_heads_t(q_ref, h0, c0) for h0 in passes]
                st["do4s"] = [_heads_t(dat_scr, h0, c0) for h0 in passes]
                st["lse4s"] = [_row4(lse_ref, h0, c0) for h0 in passes]
                st["delta4s"] = [_row4(delta_scr, h0, c0) for h0 in passes]
                st["kths"] = [kbuf[rows, c0:c0 + 3 * BLK] for rows in st["kv_rows"]]
                st["sts"] = [_tn(jnp.concatenate([kth, ones_rows], axis=0),
                                 jnp.concatenate([q4, _minus_rows(lse4)], axis=0))
                             for kth, q4, lse4 in zip(st["kths"], st["q4s"], st["lse4s"])]
                st["dpds"] = [_tn(jnp.concatenate([vbuf[rows, c0:c0 + 3 * BLK], ones_rows], axis=0),
                                  jnp.concatenate([do4, _minus_rows(delta4)], axis=0))
                              for rows, do4, delta4 in zip(st["kv_rows"], st["do4s"], st["delta4s"])]
                return st

            def elementwise(st):
                st["ps"] = [jnp.exp(_masked(s_, st["bias"])) for s_ in st["sts"]]
                st["dss"] = [(p * dpd).astype(BF16) for p, dpd in zip(st["ps"], st["dpds"])]

            def gradients(st):
                c0 = st["c0"]
                dq4s = [_mm(kth, ds) * SCALE for kth, ds in zip(st["kths"], st["dss"])]
                dks = [_nt(q4, ds) for q4, ds in zip(st["q4s"], st["dss"])]
                dvs = [_nt(do4, p.astype(BF16)) for do4, p in zip(st["do4s"], st["ps"])]
                for h0, rows, dq4, dk, dv, lse4, delta4 in zip(st["passes"], st["kv_rows"], dq4s, dks, dvs, st["lse4s"],
                                                               st["delta4s"]):
                    dkacc[rows, c0:c0 + 3 * BLK] += dk
                    dvacc[rows, c0:c0 + 3 * BLK] += dv
                    dsk = -jnp.exp(_sink_row(sink_ref, h0) - lse4) * delta4
                    for g in range(HPP):
                        h = h0 + g
                        dq_ref[h * HD:(h + 1) * HD, c0:c0 + BLK] = dq4[:, g * BLK:(g + 1) * BLK].astype(BF16)
                        dsacc[h:h + 1, :] += dsk[:, g * BLK:(g + 1) * BLK]

            ahead = [scores(*w) for w in work[:BWD_AHEAD]]
            for n in range(len(work)):
                if n + BWD_AHEAD < len(work):
                    ahead.append(scores(*work[n + BWD_AHEAD]))
                state = ahead.pop(0)
                elementwise(state)
                gradients(state)

        dk_ref[...] = dkacc[:, 0:TQ].astype(BF16)
        dv_ref[...] = dvacc[:, 0:TQ].astype(BF16)

        @pl.when(i == nq)
        def _():
            dsink_ref[...] = jnp.broadcast_to(jnp.sum(dsacc[...], axis=1, keepdims=True), (N_HEADS, LANES))

    clamp = lambda i: jnp.minimum(i, nq - 1)
    row = pl.BlockSpec((TQ, D), lambda i: (clamp(i), 0))
    col = lambda rows: pl.BlockSpec((rows, TQ), lambda i: (0, clamp(i)))
    pad = pl.BlockSpec((KV_W, TQ), lambda i: (0, i))
    return pl.pallas_call(
        body, grid=(nq + 1,), name="l1_attn_bwd",
        out_shape=(jax.ShapeDtypeStruct((D, seq), BF16), jax.ShapeDtypeStruct((D, seq), BF16),
                   jax.ShapeDtypeStruct((KV_W, seq + TQ), BF16), jax.ShapeDtypeStruct((KV_W, seq + TQ), BF16),
                   jax.ShapeDtypeStruct((N_HEADS, LANES), F32)),
        in_specs=[row, _resident((D, D)), col(D), col(D)] + _band_specs_t(nb, clamp) + _band_specs_t(nb, clamp) + [
            col(D), col(N_HEADS), pl.BlockSpec(memory_space=pltpu.SMEM)],
        out_specs=(col(D), col(D), pad, pad, pl.BlockSpec((N_HEADS, LANES), lambda i: (0, 0))),
        scratch_shapes=[pltpu.VMEM((KV_W, TQ + 2 * BLK), BF16), pltpu.VMEM((KV_W, TQ + 2 * BLK), BF16),
                        pltpu.VMEM((KV_W, TQ + 2 * BLK), F32), pltpu.VMEM((KV_W, TQ + 2 * BLK), F32),
                        pltpu.VMEM((D, TQ), BF16), pltpu.VMEM((N_HEADS, TQ), F32), pltpu.VMEM((N_HEADS, LANES), F32)],
        compiler_params=_params(56),
    )(dx2b, wout, qt, gatet, kt, kt, kt, vt, vt, vt, att, lse, sink)


def _l1_in_proj_bwd(dq_r, dk_r, dv, dgate, cos_t, sin_t, w_t, x1, g1, dx2):
    seq = x1.shape[0]
    tm = 512

    def body(dq_ref, dk_ref, dv_ref, dg_ref, c_ref, s_ref, w_ref, x_ref, g_ref, dres_ref,
             dx_ref, dxb_ref, dz_ref, dn_ref):
        @pl.when(pl.program_id(0) == 0)
        def _():
            dn_ref[...] = jnp.zeros_like(dn_ref)

        c, s = c_ref[...], s_ref[...]
        dq = _rope_t(dq_ref[...].astype(F32), c, s, N_HEADS, -1).astype(BF16)
        dk = _rope_t(dk_ref[...].astype(F32), c, s, N_KV, -1).astype(BF16)
        dz = jnp.concatenate([dq, dk, dv_ref[...], dg_ref[...]], axis=0)
        dz_ref[...] = dz
        dh = _tn(dz, w_ref[...])
        xf = x_ref[...]
        r = lax.rsqrt(jnp.mean(xf * xf, axis=1, keepdims=True) + EPS)
        xn = xf * r
        dn_ref[...] += jnp.sum(dh * xn, axis=0, keepdims=True)
        dxn = dh * g_ref[...]
        dx = dres_ref[...] + r * (dxn - xn * jnp.mean(dxn * xn, axis=1, keepdims=True))
        dx_ref[...] = dx
        dxb_ref[...] = dx.astype(BF16)

    row = pl.BlockSpec((tm, D), lambda i: (i, 0))
    col = lambda rows: pl.BlockSpec((rows, tm), lambda i: (0, i))
    return pl.pallas_call(
        body, grid=(seq // tm,), name="l1_in_proj_bwd",
        out_shape=(jax.ShapeDtypeStruct((seq, D), F32), jax.ShapeDtypeStruct((seq, D), BF16),
                   jax.ShapeDtypeStruct((MIX1_IN, seq), BF16), jax.ShapeDtypeStruct((1, D), F32)),
        in_specs=[col(D), col(KV_W), col(KV_W), col(D), col(ROT_HALF), col(ROT_HALF), _resident((MIX1_IN, D)), row,
                  _resident((1, D)), row],
        out_specs=(row, row, col(MIX1_IN), pl.BlockSpec((1, D), lambda i: (0, 0))),
        compiler_params=_params(48),
    )(dq_r, dk_r, dv, dgate, cos_t, sin_t, w_t, x1, g1, dx2)


def _l0_mix_bwd(dx1b, wout, za, bx, bg, ws, ws_t, bias, gv, wg, wg_t, scale):
    seq = dx1b.shape[0]
    ts = 256
    n_tiles = seq // ts

    def body(dx_ref, wo_ref, za_ref, bx_ref, bxp_ref, bxn_ref, bg_ref, ws_ref, wst_ref, bias_ref, gv_ref, wg_ref,
             wgt_ref, sc_ref,
             dz_ref, dp_ref, catt_ref, dws_ref, dbias_ref, dgv_ref, dsc_ref, dwg_ref, db_ref, xe_ref, *tmp_refs):
        i = pl.program_id(0)

        @pl.when(i == 0)
        def _():
            for r_ in (dws_ref, dbias_ref, dgv_ref, dsc_ref, dwg_ref, db_ref):
                r_[...] = jnp.zeros_like(r_)

        dxb = dx_ref[...]
        dya = _nt(dxb, wo_ref[0:D, :])
        dyb = _nt(dxb, wo_ref[D:2 * D, :])

        vg, dvg_dz = _gelu_and_grad(za_ref[:, D:2 * D].astype(F32))
        rv = lax.rsqrt(jnp.mean(vg * vg, axis=1, keepdims=True) + EPS)
        vnorm = vg * rv
        gvw = gv_ref[...]
        vnb = (vnorm * gvw).astype(BF16)
        mixed = _spatial_mix(ws_ref, vnb, bias_ref[...], ts)

        _fill_halo(xe_ref, bx_ref[...], bxp_ref, bxn_ref, i, n_tiles, ts)
        pb = _pool_forward(xe_ref, tmp_refs, ts, i * ts, seq).astype(BF16)
        ypre = jnp.concatenate([_mm(pb[:, g * GDIM:(g + 1) * GDIM], wg_ref[g]) for g in range(4)], axis=1)

        u, du = _gelu_and_grad(za_ref[:, 0:D].astype(F32))
        sga, dsga = _silu_and_grad(za_ref[:, 2 * D:3 * D].astype(F32))
        um = u * mixed
        ya = (um * sga).astype(BF16)
        t = dya * sga
        dz_ref[:, 0:D] = (t * mixed * du).astype(BF16)
        dz_ref[:, 2 * D:3 * D] = (dya * um * dsga).astype(BF16)
        dmixed = t * u
        dmb = dmixed.astype(BF16)
        dvn_rows = []
        for c in range(ts // CHUNK):
            rows = slice(c * CHUNK, (c + 1) * CHUNK)
            parts = []
            for h in range(A_GROUPS):
                cols = slice(h * GDIM, (h + 1) * GDIM)
                dws_ref[h] += _nt(dmb[rows, cols], vnb[rows, cols])
                parts.append(_mm(wst_ref[h], dmb[rows, cols]))
            dvn_rows.append(jnp.concatenate(parts, axis=1))

        sc = sc_ref[...]
        y = ypre * sc
        sgb, dsgb = _silu_and_grad(bg_ref[...].astype(F32))
        yb = (y * sgb).astype(BF16)
        dy_b = dyb * sgb
        dz_ref[:, 3 * D:4 * D] = jnp.zeros((ts, D), BF16)
        dz_ref[:, 4 * D:5 * D] = (dyb * y * dsgb).astype(BF16)
        dsc_ref[...] += jnp.sum(dy_b * ypre, axis=0, keepdims=True)
        dypre = (dy_b * sc).astype(BF16)
        dps = []
        for g in range(4):
            cols = slice(g * GDIM, (g + 1) * GDIM)
            dwg_ref[g] += _tn(pb[:, cols], dypre[:, cols])
            dps.append(_mm(dypre[:, cols], wgt_ref[g]))

        dbias = dmixed[0:CHUNK, :]
        for c in range(1, ts // CHUNK):
            dbias = dbias + dmixed[c * CHUNK:(c + 1) * CHUNK, :]
        dbias_ref[...] += dbias
        dvn = jnp.concatenate(dvn_rows, axis=0)
        dgv_ref[...] += jnp.sum(dvn * vnorm, axis=0, keepdims=True)
        dxn = dvn * gvw
        dvg = rv * (dxn - vnorm * jnp.mean(dxn * vnorm, axis=1, keepdims=True))
        dz_ref[:, D:2 * D] = (dvg * dvg_dz).astype(BF16)

        dp_ref[...] = jnp.concatenate(dps, axis=1)
        catt_ref[...] = jnp.concatenate([ya, yb], axis=1).T

        @pl.when(i == n_tiles - 1)
        def _():
            for h in range(A_GROUPS):
                tot = jnp.sum(dbias_ref[:, h * GDIM:(h + 1) * GDIM].T, axis=0, keepdims=True)
                db_ref[pl.ds(h * 8, 8), :] = jnp.broadcast_to(tot, (8, CHUNK))

    prev, nxt = _halo_specs(ts, seq, D)
    row = lambda w_: pl.BlockSpec((ts, w_), lambda i: (i, 0))
    acc = lambda shape: pl.BlockSpec(shape, lambda i: (0,) * len(shape))
    return pl.pallas_call(
        body, grid=(n_tiles,), name="l0_mix_bwd",
        out_shape=(jax.ShapeDtypeStruct((seq, MIX0_IN), BF16), jax.ShapeDtypeStruct((seq, D), F32),
                   jax.ShapeDtypeStruct((2 * D, seq), BF16),
                   jax.ShapeDtypeStruct((4, CHUNK, CHUNK), F32), jax.ShapeDtypeStruct((CHUNK, D), F32),
                   jax.ShapeDtypeStruct((1, D), F32), jax.ShapeDtypeStruct((1, D), F32),
                   jax.ShapeDtypeStruct((4, GDIM, GDIM), F32), jax.ShapeDtypeStruct((32, CHUNK), F32)),
        in_specs=[row(D), _resident((2 * D, D)), row(3 * D), row(D), prev, nxt, row(D), _resident((4, CHUNK, CHUNK)),
                  _resident((4, CHUNK, CHUNK)), _resident((CHUNK, D)), _resident((1, D)), _resident((4, GDIM, GDIM)),
                  _resident((4, GDIM, GDIM)), _resident((1, D))],
        out_specs=(row(MIX0_IN), row(D), pl.BlockSpec((2 * D, ts), lambda i: (0, i)),
                   acc((4, CHUNK, CHUNK)), acc((CHUNK, D)), acc((1, D)), acc((1, D)), acc((4, GDIM, GDIM)),
                   acc((32, CHUNK))),
        scratch_shapes=_pool_scratch(ts),
        compiler_params=_params(56),
    )(dx1b, wout, za, bx, bx, bx, bg, ws, ws_t, bias, gv, wg, wg_t, scale)


def _l0_pool_bwd(dp, dz):
    seq = dp.shape[0]
    ts = 512
    n_tiles = seq // ts
    ext = ts + 2 * POOL_HALO

    def body(dp_ref, dpp_ref, dpn_ref, dz_ref, out_ref, qe_ref, *tmp_refs):
        i = pl.program_id(0)
        _fill_halo(qe_ref, dp_ref[...], dpp_ref, dpn_ref, i, n_tiles, ts)
        te = i * ts - POOL_HALO + lax.broadcasted_iota(jnp.int32, (ext, 1), 0)
        for gi, w in enumerate(POOL_WINDOWS):
            hw = w // 2
            cols = slice(gi * GDIM, (gi + 1) * GDIM)
            cnt = jnp.maximum(jnp.minimum(te + hw, seq) - jnp.maximum(te - hw, 0), 1).astype(F32)
            qe_ref[pl.ds(0, ext), cols] = qe_ref[pl.ds(0, ext), cols] / cnt
        outs = []
        for gi, w in enumerate(POOL_WINDOWS):
            cols = slice(gi * GDIM, (gi + 1) * GDIM)
            outs.append(_window_sums(qe_ref, tmp_refs, ts, cols, w, 1) - dp_ref[:, cols])
        out_ref[...] = jnp.concatenate(outs, axis=1).astype(BF16)

    prev, nxt = _halo_specs(ts, seq, D)
    row = pl.BlockSpec((ts, D), lambda i: (i, 0))
    return pl.pallas_call(
        body, grid=(n_tiles,), name="l0_pool_bwd",
        out_shape=jax.ShapeDtypeStruct(dz.shape, BF16),
        in_specs=[row, prev, nxt, pl.BlockSpec(memory_space=pl.ANY)],
        out_specs=pl.BlockSpec((ts, D), lambda i: (i, 3)),
        input_output_aliases={3: 0},
        scratch_shapes=_pool_scratch(ts),
        compiler_params=_params(32),
    )(dp, dp, dp, dz)


def _l0_in_proj_bwd(dz, w, x, g0, dx1):
    seq = x.shape[0]
    tm = 512

    def body(dz_ref, w_ref, x_ref, g_ref, dres_ref, dx_ref, dn_ref):
        @pl.when(pl.program_id(0) == 0)
        def _():
            dn_ref[...] = jnp.zeros_like(dn_ref)

        dh = _nt(dz_ref[...], w_ref[...])
        xf = x_ref[...]
        r = lax.rsqrt(jnp.mean(xf * xf, axis=1, keepdims=True) + EPS)
        xn = xf * r
        dn_ref[...] += jnp.sum(dh * xn, axis=0, keepdims=True)
        dxn = dh * g_ref[...]
        dx_ref[...] = dres_ref[...] + r * (dxn - xn * jnp.mean(dxn * xn, axis=1, keepdims=True))

    row = lambda w_: pl.BlockSpec((tm, w_), lambda i: (i, 0))
    return pl.pallas_call(
        body, grid=(seq // tm,), name="l0_in_proj_bwd",
        out_shape=(jax.ShapeDtypeStruct((seq, D), F32), jax.ShapeDtypeStruct((1, D), F32)),
        in_specs=[row(MIX0_IN), _resident((D, MIX0_IN)), row(D), _resident((1, D)), row(D)],
        out_specs=(row(D), pl.BlockSpec((1, D), lambda i: (0, 0))),
        compiler_params=_params(56),
    )(dz, w, x, g0, dx1)


def _dw_matmul(a_t, b, name, b_transposed=False, tn=1024, ts=1024, col_block=None):
    k, seq = a_t.shape
    n = b.shape[0] if b_transposed else b.shape[1]
    tn = min(n, tn)
    assert seq % ts == 0 and n % tn == 0 and (col_block is None or tn % col_block == 0)
    n_s = seq // ts
    per = 1 if col_block is None else tn // col_block

    def body(a_ref, b_ref, o_ref, ob_ref, acc_ref):
        s = pl.program_id(1)

        @pl.when(s == 0)
        def _():
            acc_ref[...] = jnp.zeros_like(acc_ref)

        acc_ref[...] += _nt(a_ref[...], b_ref[...]) if b_transposed else _mm(a_ref[...], b_ref[...])

        @pl.when(s == n_s - 1)
        def _():
            acc = acc_ref[...]
            if col_block is None:
                o_ref[...] = acc
                ob_ref[...] = acc.astype(BF16)
            else:
                for i in range(per):
                    piece = acc[:, i * col_block:(i + 1) * col_block]
                    o_ref[i] = piece
                    ob_ref[i] = piece.astype(BF16)

    b_spec = (pl.BlockSpec((tn, ts), lambda j, s: (j, s)) if b_transposed else pl.BlockSpec((ts, tn), lambda j, s: (s, j)))
    if col_block is None:
        shape, o_spec = (k, n), pl.BlockSpec((k, tn), lambda j, s: (0, j))
    else:
        shape, o_spec = (n // col_block, k, col_block), pl.BlockSpec((per, k, col_block), lambda j, s: (j, 0, 0))
    return pl.pallas_call(
        body, grid=(n // tn, n_s), name=name,
        out_shape=(jax.ShapeDtypeStruct(shape, F32), jax.ShapeDtypeStruct(shape, BF16)),
        in_specs=[pl.BlockSpec((k, ts), lambda j, s: (0, s)), b_spec],
        out_specs=(o_spec, o_spec),
        scratch_shapes=[pltpu.VMEM((k, tn), F32)],
        compiler_params=_params(56, 2),
    )(a_t, b)


ROW_TILES = 8


def _cast_shards(shards):
    n = len(shards)

    def body(*refs):
        for a in range(n):
            refs[n + a][...] = refs[a][...].astype(BF16)

    vm = pl.BlockSpec(memory_space=pltpu.VMEM)
    return pl.pallas_call(body, name="cast_weights", out_shape=[jax.ShapeDtypeStruct(t.shape, BF16) for t in shards],
                          in_specs=[vm] * n, out_specs=[vm] * n, compiler_params=_params(32, 0))(*shards)


def _adamw_math(w, g, m, v):
    m2 = ADAM_B1 * m + (1.0 - ADAM_B1) * g
    v2 = ADAM_B2 * v + (1.0 - ADAM_B2) * (g * g)
    m_hat = m2 / (1.0 - ADAM_B1 ** ADAM_STEP)
    v_hat = v2 / (1.0 - ADAM_B2 ** ADAM_STEP)
    delta = -ADAM_LR * (m_hat / (jnp.sqrt(v_hat) + ADAM_EPS) + ADAM_WD * w)
    return delta, m2, v2


def _final_sum_adamw(g_list, recv_list, me, w_list, m_list, v_list):
    n = len(w_list)

    def body(me_ref, *refs):
        own, recv, w, m, v = (refs[k * n:(k + 1) * n] for k in range(5))
        outs = [refs[(5 + k) * n:(6 + k) * n] for k in range(4)]
        for a in range(n):
            g = own[a][...]
            for k in range(N_DEV - 1):
                g = g + recv[a][k].astype(F32)
            delta, m2, v2 = _adamw_math(w[a][...], g, m[a][...], v[a][...])
            for o_ref, val in zip((outs[0][a], outs[1][a], outs[2][a], outs[3][a]), (g, delta, m2, v2)):
                o_ref[...] = val

    own_specs, flat, wire, shapes = [], [], [], []
    for t in w_list:
        rows, width = t.shape
        tr = rows // ROW_TILES
        own_specs.append(pl.BlockSpec((None, tr, width), lambda i, me: (me[0], i, 0)))
        flat.append(pl.BlockSpec((tr, width), lambda i, me: (i, 0)))
        wire.append(pl.BlockSpec((N_DEV - 1, tr, width), lambda i, me: (0, i, 0)))
        shapes.append(jax.ShapeDtypeStruct((rows, width), F32))
    out = pl.pallas_call(
        body, name="grad_sum_adamw", out_shape=shapes * 4,
        grid_spec=pltpu.PrefetchScalarGridSpec(
            num_scalar_prefetch=1, grid=(ROW_TILES,), in_specs=own_specs + wire + flat * 3, out_specs=flat * 4),
        compiler_params=_params(40),
    )(me, *g_list, *recv_list, *w_list, *m_list, *v_list)
    return [out[k * n:(k + 1) * n] for k in range(4)]


SMALL_NAMES = ("norm_0", "a_v_norm_0", "b_scale_0", "norm_1", "final_norm", "a_spatial_w_0", "a_spatial_b_0", "sink_1")
SMALL_VIEWS = ((8, LANES),) * 5 + ((4 * CHUNK, LANES), (4, LANES), (1, N_HEADS))
SMALL_ROW0 = (0, 8, 16, 24, 32, 40, 552, 560)
SMALL_ROWS = 568


def _small_sum_adamw(early, late, w_list, m_list, v_list):
    n = len(w_list)

    def body(e_ref, l_ref, *refs):
        gtot, first = e_ref[0], l_ref[0]
        for d in range(1, N_DEV):
            gtot = gtot + e_ref[d]
            first = first + l_ref[d]
        for a, ((rows, width), r0) in enumerate(zip(SMALL_VIEWS, SMALL_ROW0)):
            g = first if SMALL_NAMES[a] == "norm_0" else gtot[r0:r0 + rows, 0:width]
            delta, m2, v2 = _adamw_math(refs[a][...], g, refs[n + a][...], refs[2 * n + a][...])
            for k, val in enumerate((g, delta, m2, v2)):
                refs[(3 + k) * n + a][...] = val
        refs[7 * n][...] = gtot[LOSS_ROW:LOSS_ROW + 1, LOSS_LANE:LOSS_LANE + 1]

    vm = pl.BlockSpec(memory_space=pltpu.VMEM)
    shapes = [jax.ShapeDtypeStruct(s, F32) for s in SMALL_VIEWS]
    out = pl.pallas_call(
        body, name="small_sum_adamw", out_shape=shapes * 4 + [jax.ShapeDtypeStruct((1, 1), F32)],
        in_specs=[vm, vm] + [vm] * (3 * n), out_specs=[vm] * (4 * n + 1),
    )(early, late, *w_list, *m_list, *v_list)
    return [out[k * n:(k + 1) * n] for k in range(4)], out[4 * n]


PEER_FLIPS = tuple((fx, fy, fc) for fx in (0, 1) for fy in (0, 1) for fc in (0, 1))[1:]


def _sequencer_all_gather(blks, name, collective_id, concat_rows=False):
    n = len(blks)

    def body(*refs):
        ins, outs = refs[:n], refs[n:2 * n]
        send_sems, recv_sems, local_sems = refs[2 * n:]
        x, y, c = lax.axis_index("x"), lax.axis_index("y"), lax.axis_index("c")
        peers = [(x ^ fx, y ^ fy, c ^ fc) for fx, fy, fc in PEER_FLIPS]
        barrier = pltpu.get_barrier_semaphore()
        for peer in peers:
            pl.semaphore_signal(barrier, inc=1, device_id=peer, device_id_type=MESH)
        pl.semaphore_wait(barrier, len(peers))
        me = 4 * x + 2 * y + c

        def slot(a):
            rows = blks[a].shape[0]
            return outs[a].at[pl.ds(pl.multiple_of(me * rows, 16), rows)] if concat_rows else outs[a].at[me]

        copies = [pltpu.make_async_remote_copy(
            src_ref=ins[a], dst_ref=slot(a), send_sem=send_sems.at[k, a], recv_sem=recv_sems.at[k, a],
            device_id=peer, device_id_type=MESH) for k, peer in enumerate(peers) for a in range(n)]
        mine = [pltpu.make_async_copy(ins[a], slot(a), local_sems.at[a]) for a in range(n)]
        for cp in copies + mine:
            cp.start()
        for cp in copies + mine:
            cp.wait()

    out_shape = (lambda t: (N_DEV * t.shape[0],) + t.shape[1:]) if concat_rows else (lambda t: (N_DEV,) + t.shape)
    return pl.kernel(
        body, out_type=[jax.ShapeDtypeStruct(out_shape(t), t.dtype) for t in blks],
        mesh=plsc.ScalarSubcoreMesh(axis_name="sequencer", num_cores=1), name=name,
        scratch_types=[pltpu.SemaphoreType.DMA((7, n)), pltpu.SemaphoreType.DMA((7, n)), pltpu.SemaphoreType.DMA((n,))],
        compiler_params=pltpu.CompilerParams(collective_id=collective_id),
    )(*blks)


def _sequencer_scatter(g_list, name, collective_id):
    n = len(g_list)

    def body(*refs):
        ins, outs = refs[:n], refs[n:2 * n]
        send_sems, recv_sems = refs[2 * n:]
        x, y, c = lax.axis_index("x"), lax.axis_index("y"), lax.axis_index("c")
        peers = [(x ^ fx, y ^ fy, c ^ fc) for fx, fy, fc in PEER_FLIPS]
        barrier = pltpu.get_barrier_semaphore()
        for peer in peers:
            pl.semaphore_signal(barrier, inc=1, device_id=peer, device_id_type=MESH)
        pl.semaphore_wait(barrier, len(peers))
        copies = [pltpu.make_async_remote_copy(
            src_ref=ins[a].at[4 * px + 2 * py + pc], dst_ref=outs[a].at[k], send_sem=send_sems.at[k, a],
            recv_sem=recv_sems.at[k, a], device_id=(px, py, pc), device_id_type=MESH)
            for k, (px, py, pc) in enumerate(peers) for a in range(n)]
        for cp in copies:
            cp.start()
        for cp in copies:
            cp.wait()

    return pl.kernel(
        body, out_type=[jax.ShapeDtypeStruct((N_DEV - 1,) + g.shape[1:], g.dtype) for g in g_list],
        mesh=plsc.ScalarSubcoreMesh(axis_name="sequencer", num_cores=1), name=name,
        scratch_types=[pltpu.SemaphoreType.DMA((7, n)), pltpu.SemaphoreType.DMA((7, n))],
        compiler_params=pltpu.CompilerParams(collective_id=collective_id),
    )(*g_list)


def _direct_all_gather(blk, name):
    def body(g_ref, out_ref, send_sems, recv_sems, local_sem):
        x, y, c = lax.axis_index("x"), lax.axis_index("y"), lax.axis_index("c")
        me = 4 * x + 2 * y + c
        copies = [pltpu.make_async_remote_copy(
            src_ref=g_ref, dst_ref=out_ref.at[me], send_sem=send_sems.at[k], recv_sem=recv_sems.at[k],
            device_id=(x ^ fx, y ^ fy, c ^ fc), device_id_type=MESH) for k, (fx, fy, fc) in enumerate(PEER_FLIPS)]
        copies.append(pltpu.make_async_copy(g_ref, out_ref.at[me], local_sem))
        for cp in copies:
            cp.start()
        for cp in copies:
            cp.wait()

    any_spec = pl.BlockSpec(memory_space=pl.ANY)
    return pl.pallas_call(
        body, name=name, out_shape=jax.ShapeDtypeStruct((N_DEV,) + blk.shape, blk.dtype),
        in_specs=[any_spec], out_specs=any_spec,
        scratch_shapes=[pltpu.SemaphoreType.DMA((7,)), pltpu.SemaphoreType.DMA((7,)), pltpu.SemaphoreType.DMA],
    )(blk)


def _shard_views(w_in_0, b_group_w_0, w_out_0, w_in_1, w_out_1):
    return [w_in_0, b_group_w_0.reshape(4 * 32, GDIM), w_out_0, w_in_1, w_out_1]


def _small_views(named):
    return [named[name].reshape(view) for name, view in zip(SMALL_NAMES, SMALL_VIEWS)]


LOSS_ROW, LOSS_LANE = 560, N_HEADS


def _pack_small_grads(named, loss_part):
    rows = []
    for name, (r, w) in zip(SMALL_NAMES, SMALL_VIEWS):
        pad_r = -r % 8
        if name == "sink_1":
            t = jnp.concatenate([named[name].reshape(r, w), loss_part], axis=1)
            rows.append(jnp.pad(t, ((0, pad_r), (0, LANES - w - 1))))
        elif name in named:
            rows.append(jnp.pad(named[name].reshape(r, w), ((0, pad_r), (0, LANES - w))))
        else:
            rows.append(jnp.zeros((r + pad_r, LANES), F32))
    return jnp.concatenate(rows, axis=0)


def _device_blocks(t, axis):
    shape = t.shape
    t = t.reshape(shape[:axis] + (N_DEV, shape[axis] // N_DEV) + shape[axis + 1:])
    t = jnp.moveaxis(t, axis, 0)
    return t.reshape(N_DEV, -1, shape[-1] if axis != len(shape) - 1 else shape[-1] // N_DEV)


def kernel(x, norm_0, w_in_0, a_v_norm_0, a_spatial_w_0, a_spatial_b_0, b_group_w_0, b_scale_0, w_out_0, norm_1, w_in_1, sink_1, w_out_1, final_norm, loss_target, m_norm_0, m_w_in_0, m_a_v_norm_0, m_a_spatial_w_0, m_a_spatial_b_0, m_b_group_w_0, m_b_scale_0, m_w_out_0, m_norm_1, m_w_in_1, m_sink_1, m_w_out_1, m_final_norm, v_norm_0, v_w_in_0, v_a_v_norm_0, v_a_spatial_w_0, v_a_spatial_b_0, v_b_group_w_0, v_b_scale_0, v_w_out_0, v_norm_1, v_w_in_1, v_sink_1, v_w_out_1, v_final_norm):
    seq = x.shape[1]
    xs = x.reshape(seq, D)
    tgt = loss_target.reshape(seq, D)
    ax, ay, ac = lax.axis_index("x"), lax.axis_index("y"), lax.axis_index("c")
    me = jnp.reshape(4 * ax + 2 * ay + ac, (1,)).astype(jnp.int32)

    shards = _shard_views(w_in_0, b_group_w_0, w_out_0, w_in_1, w_out_1)
    cast = _cast_shards([shards[0], shards[1], shards[2], w_in_1.T, shards[4]])

    def l1_weights(after):
        blks, _ = lax.optimization_barrier((cast[3:5], after))
        return _sequencer_all_gather(blks, "weights_gather_l1", 2, concat_rows=True)

    blocks, received, early = {}, {}, {}
    collective_ids = {"l1": 3, "out0": 4, "in0": 5}

    def scatter(tag, own_blocks, wire_blocks):
        blocks[tag] = own_blocks
        received[tag] = _sequencer_scatter(wire_blocks, "grad_scatter_" + tag, collective_ids[tag])

    def small_early(named, loss_part):
        early["small"] = _sequencer_all_gather([_pack_small_grads(named, loss_part)], "small_grad_gather", 6)[0]

    grad_x, d_norm_0 = _local_step(xs, tgt, cast[0], cast[1:3], l1_weights, norm_0, a_v_norm_0, a_spatial_w_0,
                                   a_spatial_b_0, b_scale_0, norm_1, sink_1, final_norm, scatter, small_early)

    order = (("in0", 0), ("in0", 1), ("out0", 0), ("l1", 0), ("l1", 1))
    late = _direct_all_gather(d_norm_0.reshape(8, LANES), "norm_grad_gather")
    shards_late, _ = lax.optimization_barrier((shards, grad_x))
    big = _final_sum_adamw([blocks[t][i] for t, i in order], [received[t][i] for t, i in order], me, shards_late,
                           _shard_views(m_w_in_0, m_b_group_w_0, m_w_out_0, m_w_in_1, m_w_out_1),
                           _shard_views(v_w_in_0, v_b_group_w_0, v_w_out_0, v_w_in_1, v_w_out_1))
    weights = dict(norm_0=norm_0, a_v_norm_0=a_v_norm_0, a_spatial_w_0=a_spatial_w_0, a_spatial_b_0=a_spatial_b_0,
                   b_scale_0=b_scale_0, norm_1=norm_1, sink_1=sink_1, final_norm=final_norm)
    m_small = dict(norm_0=m_norm_0, a_v_norm_0=m_a_v_norm_0, a_spatial_w_0=m_a_spatial_w_0, a_spatial_b_0=m_a_spatial_b_0,
                   b_scale_0=m_b_scale_0, norm_1=m_norm_1, sink_1=m_sink_1, final_norm=m_final_norm)
    v_small = dict(norm_0=v_norm_0, a_v_norm_0=v_a_v_norm_0, a_spatial_w_0=v_a_spatial_w_0, a_spatial_b_0=v_a_spatial_b_0,
                   b_scale_0=v_b_scale_0, norm_1=v_norm_1, sink_1=v_sink_1, final_norm=v_final_norm)
    small, loss = _small_sum_adamw(early["small"], late, _small_views(weights), _small_views(m_small),
                                   _small_views(v_small))

    def in_order(kind):
        b = [b_.reshape(s_.shape) for b_, s_ in zip(big[kind], (w_in_0, b_group_w_0, w_out_0, w_in_1, w_out_1))]
        s = {name: t.reshape(weights[name].shape) for name, t in zip(SMALL_NAMES, small[kind])}
        return [s["norm_0"], b[0], s["a_v_norm_0"], s["a_spatial_w_0"], s["a_spatial_b_0"], b[1], s["b_scale_0"], b[2],
                s["norm_1"], b[3], s["sink_1"], b[4], s["final_norm"]]

    return (loss[0, 0], grad_x.reshape(1, seq, D), *in_order(0), *in_order(1), *in_order(2), *in_order(3))


def _local_step(xs, tgt, win0_shard, l0_shards, l1_weights, norm_0, a_v_norm_0, a_spatial_w_0, a_spatial_b_0, b_scale_0,
                norm_1, sink_1, final_norm, scatter, small_early):
    seq = xs.shape[0]
    ws = a_spatial_w_0.astype(BF16)
    ws_t = jnp.swapaxes(ws, 1, 2)
    bias = jnp.repeat(a_spatial_b_0.T, GDIM, axis=1)
    g0, gv, scale, g1, gf = (t.reshape(1, D) for t in (norm_0, a_v_norm_0, b_scale_0, norm_1, final_norm))
    cos_t, sin_t = _rope_tables_t(seq)

    za, bx, bg, h0_t, win0, g_wg, wout0 = _l0_in_proj(xs, g0, win0_shard, l0_shards)
    win1_t, wout1 = l1_weights(za)
    wg = g_wg.reshape(N_DEV, 4, 32, GDIM).transpose(1, 0, 2, 3).reshape(4, GDIM, GDIM)
    wg_t = jnp.swapaxes(wg, 1, 2)
    x1 = _l0_mix_fwd(za, bx, bg, xs, ws, bias, gv, wg, scale, wout0)
    win1_t, wout1, x1 = lax.optimization_barrier((win1_t, wout1, x1))
    qt, kt, vt, gatet, h1_t = _l1_in_proj(x1, g1, win1_t, cos_t, sin_t)
    dx2, dx2b, att, lse, loss_part, d_gf, d_wout1, d_wout1_wire = _l1_attn_fwd(
        qt, kt, vt, gatet, x1, tgt, wout1, gf, sink_1)

    dq_r, dgate, dk_pad, dv_pad, d_sink = _l1_attn_bwd(dx2b, wout1, qt, kt, vt, gatet, att, lse, sink_1)
    dk_r = dk_pad[:, BLK:BLK + seq]
    dv = dv_pad[:, BLK:BLK + seq]
    dx1, dx1b, dz1_t, d_g1 = _l1_in_proj_bwd(dq_r, dk_r, dv, dgate, cos_t, sin_t, win1_t, x1, g1, dx2)
    d_win1, d_win1_wire = _dw_matmul(h1_t, dz1_t, "dw_in_1", b_transposed=True, tn=1280, col_block=MIX1_IN // N_DEV)
    rows = lambda t: t.reshape(N_DEV, t.shape[0] // N_DEV, t.shape[1])
    scatter("l1", [d_win1, rows(d_wout1)], [d_win1_wire, rows(d_wout1_wire)])

    dz0, dp, cat_t, d_ws, _, d_gv, d_scale, d_wg, d_b = _l0_mix_bwd(
        dx1b, wout0, za, bx, bg, ws, ws_t, bias, gv, wg, wg_t, scale)
    dz0 = _l0_pool_bwd(dp, dz0)
    d_win0, d_win0_wire = _dw_matmul(h0_t, dz0, "dw_in_0", tn=1280, col_block=MIX0_IN // N_DEV)
    d_wg_blocks = _device_blocks(d_wg, 1)
    scatter("in0", [d_win0, d_wg_blocks], [d_win0_wire, d_wg_blocks])
    cat_t, _ = lax.optimization_barrier((cat_t, d_win0))
    d_wout0, d_wout0_wire = _dw_matmul(cat_t, dx1b, "dw_out_0")
    scatter("out0", [rows(d_wout0)], [rows(d_wout0_wire)])
    small_early(dict(a_v_norm_0=d_gv, a_spatial_w_0=d_ws, a_spatial_b_0=d_b.reshape(4, 8, CHUNK)[:, 0, :],
                     b_scale_0=d_scale, norm_1=d_g1, sink_1=d_sink[:, 0], final_norm=d_gf), loss_part)
    dz0, _ = lax.optimization_barrier((dz0, d_wout0))
    return _l0_in_proj_bwd(dz0, win0, xs, g0, dx1)
```

```python
import jax
import jax.numpy as jnp
from jax import lax
from jax.experimental import pallas as pl
from jax.experimental.pallas import tpu as pltpu
from jax.experimental.pallas import tpu_sc as plsc

F32 = jnp.float32
BF16 = jnp.bfloat16

D = 1024
EPS = 1e-6
NEG_INF = -1e30
CHUNK = 128
A_GROUPS = 4
POOL_WINDOWS = (2, 4, 8, 16)
POOL_HALO = 8
GDIM = 256
N_HEADS = 16
N_KV = 4
GQA = 4
HD = 64
BLK = 128
ROT_HALF = 8
ROPE_THETA = 500000.0
SCALE = HD ** -0.5
MIX0_IN = 5 * D
MIX1_IN = 2560
KV_W = N_KV * HD
Q_ROWS, K_ROWS, V_ROWS, G_ROWS = (0, D), (D, D + KV_W), (D + KV_W, D + 2 * KV_W), (D + 2 * KV_W, MIX1_IN)
TQ = 512

ADAM_LR = 0.001
ADAM_B1 = 0.9
ADAM_B2 = 0.999
ADAM_EPS = 1e-08
ADAM_WD = 0.01
ADAM_STEP = 10

N_DEV = 8
LANES = 128
MIB = 2 ** 20
MESH = pl.DeviceIdType.MESH


def _params(limit_mib, n_axes=1):
    return pltpu.CompilerParams(vmem_limit_bytes=limit_mib * MIB, dimension_semantics=("arbitrary",) * n_axes)


def _resident(shape):
    nd = len(shape)
    return pl.BlockSpec(shape, lambda *_: (0,) * nd, pipeline_mode=pl.Buffered(1))


def _gelu(x):
    k = 0.7978845608028654
    return 0.5 * x * (1.0 + jnp.tanh(k * (x + 0.044715 * x * x * x)))


def _gelu_and_grad(x):
    k = 0.7978845608028654
    x2 = x * x
    t = jnp.tanh(k * (x + 0.044715 * x * x2))
    g = 0.5 * x * (1.0 + t)
    dg = 0.5 * (1.0 + t) + 0.5 * x * (1.0 - t * t) * (k * (1.0 + 3.0 * 0.044715 * x2))
    return g, dg


def _silu_and_grad(x):
    s = jax.nn.sigmoid(x)
    return x * s, s * (1.0 + x * (1.0 - s))


def _nt(a, b):
    return lax.dot_general(a, b, (((1,), (1,)), ((), ())), preferred_element_type=F32)


def _tn(a, b):
    return lax.dot_general(a, b, (((0,), (0,)), ((), ())), preferred_element_type=F32)


def _mm(a, b):
    return jnp.dot(a, b, preferred_element_type=F32)


def _rope_tables_t(seq):
    inv = ROPE_THETA ** (-jnp.arange(0, 2 * ROT_HALF, 2, dtype=F32) / (2 * ROT_HALF))
    ang = inv[:, None] * jnp.arange(seq, dtype=F32)[None, :]
    return jnp.cos(ang), jnp.sin(ang)


def _rope_t(z, c, s, n_heads, sign):
    parts = []
    for h in range(n_heads):
        b = h * HD
        x1, x2 = z[b:b + ROT_HALF], z[b + ROT_HALF:b + 2 * ROT_HALF]
        if sign > 0:
            parts += [x1 * c - x2 * s, x2 * c + x1 * s]
        else:
            parts += [x1 * c + x2 * s, x2 * c - x1 * s]
        parts.append(z[b + 2 * ROT_HALF:b + HD])
    return jnp.concatenate(parts, axis=0)


N_CHIPS = 4
CHIP_COLS = MIX0_IN // N_CHIPS
IN_PROJ_ROWS, IN_PROJ_CHUNK = 512, 256
IN_PROJ_SPLIT = 4
IN_PROJ_PIECES = (
    ((0, 0, CHIP_COLS, 0),),
    ((0, CHIP_COLS, CHIP_COLS, 0),),
    ((0, 2 * CHIP_COLS, 3 * D - 2 * CHIP_COLS, 0), (1, 0, 3 * CHIP_COLS - 3 * D, 3 * D - 2 * CHIP_COLS)),
    ((1, 3 * CHIP_COLS - 3 * D, 4 * D - 3 * CHIP_COLS, 0), (2, 0, D, 4 * D - 3 * CHIP_COLS)),
)


def _l0_in_proj(x, g0, w_shard, later_shards):
    seq = x.shape[0]
    tm = 1024
    n = seq // tm
    f32_cols = max(width for pieces in IN_PROJ_PIECES for o, _, width, _ in pieces if o == 1)
    shard_cols = w_shard.shape[1]
    n_arr = 1 + len(later_shards)
    parts = [(0, s * (D // IN_PROJ_SPLIT), D // IN_PROJ_SPLIT) for s in range(IN_PROJ_SPLIT)]
    parts += [(a + 1, 0, t.shape[0]) for a, t in enumerate(later_shards)]
    w_parts, later_parts = range(IN_PROJ_SPLIT), range(IN_PROJ_SPLIT, len(parts))
    assert 2 * shard_cols == CHIP_COLS and seq % tm == 0 and n >= 4

    def body(*refs):
        x_ref, g_ref = refs[:2]
        ins = refs[2:2 + n_arr]
        za_ref, bx_ref, bg_ref, ht_ref = refs[2 + n_arr:6 + n_arr]
        gathered = refs[6 + n_arr:6 + 2 * n_arr]
        h_all, w_buf, z32, z16, send_sems, recv_sems, local_sems, load_sems, out_sems = refs[6 + 2 * n_arr:]
        p, i = pl.program_id(0), pl.program_id(1)
        ax, ay, ac = lax.axis_index("x"), lax.axis_index("y"), lax.axis_index("c")
        me, sibling = (ax, ay, ac), (ax, ay, 1 - ac)
        chips = [(ax, ay), (1 - ax, ay), (ax, 1 - ay), (1 - ax, 1 - ay)]
        outs = (za_ref, bx_ref, bg_ref)

        def source(t):
            arr, r0, rows = parts[t]
            return ins[arr].at[pl.ds(r0, rows)]

        def slot(t, px, py, pc):
            arr, r0, rows = parts[t]
            dev = 4 * px + 2 * py + pc
            if arr == 0:
                return gathered[0].at[pl.ds(r0, rows), pl.ds(pl.multiple_of(dev * shard_cols, LANES), shard_cols)]
            return gathered[arr].at[pl.ds(pl.multiple_of(dev * rows, 16), rows)]

        def copy(k, t, block, to, from_input=False):
            return pltpu.make_async_remote_copy(
                src_ref=source(t) if from_input else slot(t, *block), dst_ref=slot(t, *block),
                send_sem=send_sems.at[k, t], recv_sem=recv_sems.at[k, t], device_id=to, device_id_type=MESH)

        def to_sibling(t):
            return copy(0, t, me, sibling, from_input=True)

        def from_sibling(t):
            return copy(0, t, sibling, me)

        def send(j, t):
            return copy(j, t, me, (*chips[j], ac), from_input=True)

        def landed(j, t):
            return copy(j, t, (*chips[j], ac), me)

        def forward(j, t):
            return copy(3 + j, t, (*chips[j], ac), sibling)

        def forwarded(j, t):
            return copy(3 + j, t, (*chips[j], 1 - ac), me)

        def mine(t):
            return pltpu.make_async_copy(source(t), slot(t, *me), local_sems.at[t])

        def load(chip, q):
            px, py = chip
            cols = pl.ds(pl.multiple_of((2 * px + py) * CHIP_COLS, LANES), CHIP_COLS)
            return pltpu.make_async_copy(gathered[0].at[:, cols], w_buf.at[q % 2], load_sems.at[q % 2])

        def out_copies(q, tile, stage):
            cps = []
            for k, (o, c0, width, z0) in enumerate(IN_PROJ_PIECES[q]):
                src = z32.at[stage, :, pl.ds(0, width)] if o == 1 else z16.at[stage, :, pl.ds(z0, width)]
                dst = outs[o].at[pl.ds(pl.multiple_of(tile * tm, tm), tm), pl.ds(c0, width)]
                cps.append(pltpu.make_async_copy(src, dst, out_sems.at[stage, k]))
            return cps

        @pl.when((p == 0) & (i == 0))
        def _():
            for t in w_parts:
                mine(t).start()
                to_sibling(t).start()
                for j in range(1, N_CHIPS):
                    send(j, t).start()
            for t in later_parts:
                mine(t).start()
                to_sibling(t).start()
            for t in w_parts:
                from_sibling(t).wait_recv()
                mine(t).wait()
            load(chips[0], 0).start()

        for j in range(1, N_CHIPS):
            @pl.when((p == j - 1) & (i == n - 2))
            def _(j=j):
                for t in w_parts:
                    landed(j, t).wait_recv()
                    forward(j, t).start()
                if j == 1:
                    for t in later_parts:
                        for jj in range(1, N_CHIPS):
                            send(jj, t).start()

            @pl.when((p == j - 1) & (i == n - 1))
            def _(j=j):
                for t in w_parts:
                    forwarded(j, t).wait_recv()
                load(chips[j], j).start()

        @pl.when((p == N_CHIPS - 1) & (i == n - 4))
        def _():
            for jj in range(1, N_CHIPS):
                for t in later_parts:
                    landed(jj, t).wait_recv()
                    forward(jj, t).start()

        @pl.when(i == 0)
        def _():
            load(chips[0], p).wait()

        @pl.when(p == 0)
        def _():
            xf = x_ref[...]
            r = lax.rsqrt(jnp.mean(xf * xf, axis=1, keepdims=True) + EPS)
            h = (xf * r * g_ref[...]).astype(BF16)
            ht_ref[...] = h.T
            h_all[pl.ds(pl.multiple_of(i * tm, tm), tm), :] = h

        def chip_of_pass(pp):
            return (2 * ax + ay) ^ ((pp >> 1) | ((pp & 1) << 1))

        step = p * n + i
        stage = step % 2
        for q in range(N_CHIPS):
            @pl.when((step >= 2) & (chip_of_pass((step - 2) // n) == q))
            def _(q=q):
                for cp in out_copies(q, (step - 2) % n, stage):
                    cp.wait()

        for q in range(N_CHIPS):
            @pl.when(chip_of_pass(p) == q)
            def _(q=q):
                f32_from = [(z0, width) for o, _, width, z0 in IN_PROJ_PIECES[q] if o == 1]
                for r0 in range(0, tm, IN_PROJ_ROWS):
                    rows = pl.ds(r0, IN_PROJ_ROWS)
                    h = h_all[pl.ds(pl.multiple_of(i * tm + r0, IN_PROJ_ROWS), IN_PROJ_ROWS), :]
                    for c0 in range(0, CHIP_COLS, IN_PROJ_CHUNK):
                        z = _mm(h, w_buf[p % 2, :, pl.ds(c0, IN_PROJ_CHUNK)])
                        z16[stage, rows, pl.ds(c0, IN_PROJ_CHUNK)] = z.astype(BF16)
                        for z0, width in f32_from:
                            if z0 <= c0 < z0 + width:
                                z32[stage, rows, pl.ds(c0 - z0, IN_PROJ_CHUNK)] = z
                for cp in out_copies(q, i, stage):
                    cp.start()

        last = (p == N_CHIPS - 1) & (i == n - 1)
        for q in range(N_CHIPS):
            @pl.when(last & (chip_of_pass(p) == q))
            def _(q=q):
                for cp in out_copies(q, n - 2, 1 - stage) + out_copies(q, n - 1, stage):
                    cp.wait()

        @pl.when(last)
        def _():
            for t in later_parts:
                from_sibling(t).wait_recv()
                for jj in range(1, N_CHIPS):
                    forwarded(jj, t).wait_recv()
                mine(t).wait()
            for t in range(len(parts)):
                to_sibling(t).wait_send()
                for jj in range(1, N_CHIPS):
                    send(jj, t).wait_send()
                    forward(jj, t).wait_send()

    any_spec = pl.BlockSpec(memory_space=pl.ANY)
    first_pass_tile = lambda p, i: jnp.where(p == 0, i, n - 1)
    return pl.pallas_call(
        body, grid=(N_CHIPS, n), name="l0_in_proj",
        out_shape=[jax.ShapeDtypeStruct((seq, 3 * D), BF16), jax.ShapeDtypeStruct((seq, D), F32),
                   jax.ShapeDtypeStruct((seq, D), BF16), jax.ShapeDtypeStruct((D, seq), BF16),
                   jax.ShapeDtypeStruct((D, MIX0_IN), BF16)]
        + [jax.ShapeDtypeStruct((N_DEV * t.shape[0], t.shape[1]), t.dtype) for t in later_shards],
        in_specs=[pl.BlockSpec((tm, D), lambda p, i: (first_pass_tile(p, i), 0)), _resident((1, D))] + [any_spec] * n_arr,
        out_specs=[any_spec, any_spec, any_spec, pl.BlockSpec((D, tm), lambda p, i: (0, first_pass_tile(p, i)))]
        + [any_spec] * n_arr,
        scratch_shapes=[pltpu.VMEM((seq, D), BF16), pltpu.VMEM((2, D, CHIP_COLS), BF16),
                        pltpu.VMEM((2, tm, f32_cols), F32), pltpu.VMEM((2, tm, CHIP_COLS), BF16),
                        pltpu.SemaphoreType.DMA((7, len(parts))), pltpu.SemaphoreType.DMA((7, len(parts))),
                        pltpu.SemaphoreType.DMA((len(parts),)), pltpu.SemaphoreType.DMA((2,)),
                        pltpu.SemaphoreType.DMA((2, 2))],
        compiler_params=_params(56, 2),
    )(x, g0, w_shard, *later_shards)


POOL_EXT = 40


def _fill_halo(ext_ref, cur, prev_ref, next_ref, i, n_tiles, ts):
    ext_ref[pl.ds(0, POOL_HALO), :] = jnp.where(i > 0, prev_ref[...], 0.0)
    ext_ref[pl.ds(POOL_HALO, ts), :] = cur
    ext_ref[pl.ds(POOL_HALO + ts, POOL_HALO), :] = jnp.where(i < n_tiles - 1, next_ref[...], 0.0)
    ext_ref[pl.ds(2 * POOL_HALO + ts, POOL_EXT - 2 * POOL_HALO), :] = jnp.zeros((POOL_EXT - 2 * POOL_HALO, D), F32)


def _window_sums(src_ref, tmp_refs, ts, cols, w, shift):
    if w == 2:
        return src_ref[pl.ds(POOL_HALO - 1 + shift, ts), cols] + src_ref[pl.ds(POOL_HALO + shift, ts), cols]
    d2, d4, d8 = tmp_refs
    n2, n4, n8 = ts + 32, ts + 24, ts + 16
    d2[pl.ds(0, n2), :] = src_ref[pl.ds(0, n2), cols] + src_ref[pl.ds(1, n2), cols]
    if w == 4:
        return d2[pl.ds(POOL_HALO - 2 + shift, ts), :] + d2[pl.ds(POOL_HALO + shift, ts), :]
    d4[pl.ds(0, n4), :] = d2[pl.ds(0, n4), :] + d2[pl.ds(2, n4), :]
    if w == 8:
        return d4[pl.ds(POOL_HALO - 4 + shift, ts), :] + d4[pl.ds(POOL_HALO + shift, ts), :]
    d8[pl.ds(0, n8), :] = d4[pl.ds(0, n8), :] + d4[pl.ds(4, n8), :]
    return d8[pl.ds(shift, ts), :] + d8[pl.ds(POOL_HALO + shift, ts), :]


def _pool_scratch(ts):
    return [pltpu.VMEM((ts + POOL_EXT, D), F32)] + [pltpu.VMEM((ts + POOL_EXT, GDIM), F32)] * 3


def _pool_forward(xe_ref, tmp_refs, ts, t0, seq):
    tg = t0 + lax.broadcasted_iota(jnp.int32, (ts, 1), 0)
    outs = []
    for gi, w in enumerate(POOL_WINDOWS):
        hw = w // 2
        cols = slice(gi * GDIM, (gi + 1) * GDIM)
        cnt = (jnp.minimum(tg + hw, seq) - jnp.maximum(tg - hw, 0)).astype(F32)
        outs.append(_window_sums(xe_ref, tmp_refs, ts, cols, w, 0) / cnt - xe_ref[pl.ds(POOL_HALO, ts), cols])
    return jnp.concatenate(outs, axis=1)


def _spatial_mix(ws_ref, vnb, bias, ts):
    rows = []
    for c in range(ts // CHUNK):
        vc = vnb[c * CHUNK:(c + 1) * CHUNK, :]
        rows.append(jnp.concatenate(
            [_mm(ws_ref[h], vc[:, h * GDIM:(h + 1) * GDIM]) for h in range(A_GROUPS)], axis=1) + bias)
    return jnp.concatenate(rows, axis=0)


def _halo_specs(ts, seq, width):
    per = ts // POOL_HALO
    last = seq // POOL_HALO - 1
    prev = pl.BlockSpec((POOL_HALO, width), lambda i: (jnp.maximum(i * per - 1, 0), 0))
    nxt = pl.BlockSpec((POOL_HALO, width), lambda i: (jnp.minimum((i + 1) * per, last), 0))
    return prev, nxt


def _l0_mix_fwd(za, bx, bg, x, ws, bias, gv, wg, scale, wout):
    seq = x.shape[0]
    ts = 512
    n_tiles = seq // ts

    def body(za_ref, bx_ref, bxp_ref, bxn_ref, bg_ref, x_ref, ws_ref, bias_ref, gv_ref, wg_ref, sc_ref, wo_ref,
             x1_ref, xe_ref, *tmp_refs):
        i = pl.program_id(0)
        vg = _gelu(za_ref[:, D:2 * D].astype(F32))
        rv = lax.rsqrt(jnp.mean(vg * vg, axis=1, keepdims=True) + EPS)
        vnb = (vg * rv * gv_ref[...]).astype(BF16)
        mixed = _spatial_mix(ws_ref, vnb, bias_ref[...], ts)

        _fill_halo(xe_ref, bx_ref[...], bxp_ref, bxn_ref, i, n_tiles, ts)
        pb = _pool_forward(xe_ref, tmp_refs, ts, i * ts, seq).astype(BF16)
        ypre = jnp.concatenate([_mm(pb[:, g * GDIM:(g + 1) * GDIM], wg_ref[g]) for g in range(4)], axis=1)

        u = _gelu(za_ref[:, 0:D].astype(F32))
        ag = za_ref[:, 2 * D:3 * D].astype(F32)
        ya = (u * mixed * (ag * jax.nn.sigmoid(ag))).astype(BF16)
        out_a = _mm(ya, wo_ref[0:D, :])

        bgf = bg_ref[...].astype(F32)
        yb = (ypre * sc_ref[...] * (bgf * jax.nn.sigmoid(bgf))).astype(BF16)
        x1_ref[...] = x_ref[...] + out_a + _mm(yb, wo_ref[D:2 * D, :])

    prev, nxt = _halo_specs(ts, seq, D)
    row = lambda w: pl.BlockSpec((ts, w), lambda i: (i, 0))
    return pl.pallas_call(
        body, grid=(n_tiles,), name="l0_mix_fwd",
        out_shape=jax.ShapeDtypeStruct((seq, D), F32),
        in_specs=[row(3 * D), row(D), prev, nxt, row(D), row(D), _resident((4, CHUNK, CHUNK)), _resident((CHUNK, D)),
                  _resident((1, D)), _resident((4, GDIM, GDIM)), _resident((1, D)), _resident((2 * D, D))],
        out_specs=row(D),
        scratch_shapes=_pool_scratch(ts),
        compiler_params=_params(56),
    )(za, bx, bx, bx, bg, x, ws, bias, gv, wg, scale, wout)


def _l1_in_proj(x1, g1, w_t, cos_t, sin_t):
    seq = x1.shape[0]
    tm = 512

    def body(x_ref, g_ref, wt_ref, c_ref, s_ref, q_ref, k_ref, v_ref, gate_ref, ht_ref):
        xf = x_ref[...]
        r = lax.rsqrt(jnp.mean(xf * xf, axis=1, keepdims=True) + EPS)
        ht = (xf * r * g_ref[...]).astype(BF16).T
        ht_ref[...] = ht
        c, s = c_ref[...], s_ref[...]
        q_ref[...] = (_rope_t(_mm(wt_ref[Q_ROWS[0]:Q_ROWS[1], :], ht), c, s, N_HEADS, 1) * SCALE).astype(BF16)
        k_ref[...] = _rope_t(_mm(wt_ref[K_ROWS[0]:K_ROWS[1], :], ht), c, s, N_KV, 1).astype(BF16)
        v_ref[...] = _mm(wt_ref[V_ROWS[0]:V_ROWS[1], :], ht).astype(BF16)
        gate_ref[...] = _mm(wt_ref[G_ROWS[0]:G_ROWS[1], :], ht).astype(BF16)

    col = lambda rows: pl.BlockSpec((rows, tm), lambda i: (0, i))
    return pl.pallas_call(
        body, grid=(seq // tm,), name="l1_in_proj",
        out_shape=(jax.ShapeDtypeStruct((D, seq), BF16), jax.ShapeDtypeStruct((KV_W, seq), BF16),
                   jax.ShapeDtypeStruct((KV_W, seq), BF16), jax.ShapeDtypeStruct((D, seq), BF16),
                   jax.ShapeDtypeStruct((D, seq), BF16)),
        in_specs=[pl.BlockSpec((tm, D), lambda i: (i, 0)), _resident((1, D)), _resident((MIX1_IN, D)), col(ROT_HALF),
                  col(ROT_HALF)],
        out_specs=(col(D), col(KV_W), col(KV_W), col(D), col(D)),
        compiler_params=_params(48),
    )(x1, g1, w_t, cos_t, sin_t)


def _band_specs_t(nb, clamp_i):
    per = TQ // BLK
    prev = pl.BlockSpec((KV_W, BLK), lambda i: (0, jnp.maximum(clamp_i(i) * per - 1, 0)))
    cur = pl.BlockSpec((KV_W, TQ), lambda i: (0, clamp_i(i)))
    nxt = pl.BlockSpec((KV_W, BLK), lambda i: (0, jnp.minimum((clamp_i(i) + 1) * per, nb - 1)))
    return [prev, cur, nxt]


def _fill_band(buf, p_ref, c_ref, n_ref):
    buf[:, 0:BLK] = p_ref[...]
    buf[:, BLK:BLK + TQ] = c_ref[...]
    buf[:, BLK + TQ:2 * BLK + TQ] = n_ref[...]


def _band_bias_t(n, nb):
    c = lax.broadcasted_iota(jnp.int32, (BLK, BLK), 0)
    r = lax.broadcasted_iota(jnp.int32, (BLK, BLK), 1)
    first = jnp.where((c >= r) & (n > 0), 0.0, NEG_INF).astype(F32)
    last = jnp.where((c <= r) & (n < nb - 1), 0.0, NEG_INF).astype(F32)
    return jnp.concatenate([first] * HPP, axis=1), jnp.concatenate([last] * HPP, axis=1)


def _masked(st, bias):
    first, last = bias
    return jnp.concatenate([st[0:BLK] + first, st[BLK:2 * BLK], st[2 * BLK:3 * BLK] + last], axis=0)


AUG = 16


def _ones_rows(n_ones, width):
    return (lax.broadcasted_iota(jnp.int32, (AUG, width), 0) < n_ones).astype(BF16)


def _minus_rows(vec):
    hi = vec.astype(BF16).astype(F32)
    lo = vec - hi
    return jnp.concatenate([-hi, -lo, jnp.zeros((AUG - 2, vec.shape[1]), F32)], axis=0).astype(BF16)


HPP = GQA
FWD_GROUP, BWD_GROUP = 2, 1
BWD_AHEAD = 1


def _heads_t(ref, h0, c0):
    return jnp.concatenate([ref[(h0 + g) * HD:(h0 + g + 1) * HD, c0:c0 + BLK] for g in range(HPP)], axis=1)


def _row4(ref, h0, c0):
    return jnp.concatenate([ref[h0 + g:h0 + g + 1, c0:c0 + BLK] for g in range(HPP)], axis=1)


def _sink_row(sink_ref, h0):
    return jnp.concatenate([jnp.full((1, BLK), sink_ref[h0 + g], F32) for g in range(HPP)], axis=1)


def _l1_attn_fwd(qt, kt, vt, gatet, x1, tgt, wout, gf, sink):
    seq = x1.shape[0]
    nq, nb = seq // TQ, seq // BLK

    def body(q_ref, gate_ref, kp_ref, k_ref, kn_ref, vp_ref, v_ref, vn_ref, x1_ref, tgt_ref, wo_ref, gf_ref, sink_ref,
             dx2_ref, dx2b_ref, att_ref, lse_ref, loss_ref, dgf_ref, dwo_ref, dwo_wire_ref, kbuf, vbuf, att_scr):
        i = pl.program_id(0)

        @pl.when(i == 0)
        def _():
            loss_ref[...] = jnp.zeros_like(loss_ref)
            dgf_ref[...] = jnp.zeros_like(dgf_ref)
            dwo_ref[...] = jnp.zeros_like(dwo_ref)

        _fill_band(kbuf, kp_ref, k_ref, kn_ref)
        _fill_band(vbuf, vp_ref, v_ref, vn_ref)
        ones_row = _ones_rows(1, 3 * BLK)
        groups = [list(range(0, N_HEADS, HPP))[g:g + FWD_GROUP] for g in range(0, N_HEADS // HPP, FWD_GROUP)]
        work = [(j, grp) for j in range(TQ // BLK) for grp in groups]

        def scores(j, passes):
            c0 = j * BLK
            bias = _band_bias_t(i * (TQ // BLK) + j, nb)
            st = dict(c0=c0, passes=passes)
            st["kv_rows"] = [slice(h0 // GQA * HD, (h0 // GQA + 1) * HD) for h0 in passes]
            st["sts"] = [_masked(_tn(kbuf[rows, c0:c0 + 3 * BLK], _heads_t(q_ref, h0, c0)), bias)
                         for h0, rows in zip(passes, st["kv_rows"])]
            return st

        def softmaxes(st):
            st["sks"] = [_sink_row(sink_ref, h0) for h0 in st["passes"]]
            st["ms"] = [jnp.maximum(jnp.max(s_, axis=0, keepdims=True), sk) for s_, sk in zip(st["sts"], st["sks"])]
            st["ps"] = [jnp.exp(s_ - m).astype(BF16) for s_, m in zip(st["sts"], st["ms"])]

        def values(st):
            c0, passes = st["c0"], st["passes"]
            pvs = [_mm(jnp.concatenate([vbuf[rows, c0:c0 + 3 * BLK], ones_row], axis=0), p)
                   for rows, p in zip(st["kv_rows"], st["ps"])]
            lse_rows = []
            for h0, pv, m, sk in zip(passes, pvs, st["ms"], st["sks"]):
                den = pv[HD:HD + 1, :] + jnp.exp(sk - m)
                ot = pv[0:HD, :] / den
                lse = m + jnp.log(den)
                for g in range(HPP):
                    h = h0 + g
                    att_scr[h * HD:(h + 1) * HD, c0:c0 + BLK] = ot[:, g * BLK:(g + 1) * BLK]
                    lse_rows.append(lse[:, g * BLK:(g + 1) * BLK])
            lse_ref[passes[0]:passes[0] + len(lse_rows), c0:c0 + BLK] = jnp.concatenate(lse_rows, axis=0)

        state = scores(*work[0])
        for nxt in work[1:] + [None]:
            following = scores(*nxt) if nxt is not None else None
            softmaxes(state)
            values(state)
            state = following

        att = att_scr[...]
        gate = gate_ref[...].astype(F32)
        yt = (att * (gate * jax.nn.sigmoid(gate))).astype(BF16)
        att_ref[...] = att.astype(BF16)
        x2 = x1_ref[...] + _mm(yt.T, wo_ref[...])
        r = lax.rsqrt(jnp.mean(x2 * x2, axis=1, keepdims=True) + EPS)
        xn = x2 * r
        diff = xn * gf_ref[...] - tgt_ref[...]
        loss_ref[...] += 0.5 * jnp.sum(jnp.mean(diff * diff, axis=1, keepdims=True), axis=0, keepdims=True)
        dout = diff * (1.0 / D)
        dgf_ref[...] += jnp.sum(dout * xn, axis=0, keepdims=True)
        dxn = dout * gf_ref[...]
        dx2 = r * (dxn - xn * jnp.mean(dxn * xn, axis=1, keepdims=True))
        dx2_ref[...] = dx2
        dx2b = dx2.astype(BF16)
        dx2b_ref[...] = dx2b
        dwo_ref[...] += _mm(yt, dx2b)

        @pl.when(i == nq - 1)
        def _():
            dwo_wire_ref[...] = dwo_ref[...].astype(BF16)

    ident = lambda i: i
    row = pl.BlockSpec((TQ, D), lambda i: (i, 0))
    col = lambda rows: pl.BlockSpec((rows, TQ), lambda i: (0, i))
    whole = pl.BlockSpec((D, D), lambda i: (0, 0))
    return pl.pallas_call(
        body, grid=(nq,), name="l1_attn_fwd",
        out_shape=(jax.ShapeDtypeStruct((seq, D), F32), jax.ShapeDtypeStruct((seq, D), BF16),
                   jax.ShapeDtypeStruct((D, seq), BF16),
                   jax.ShapeDtypeStruct((N_HEADS, seq), F32), jax.ShapeDtypeStruct((1, 1), F32),
                   jax.ShapeDtypeStruct((1, D), F32), jax.ShapeDtypeStruct((D, D), F32), jax.ShapeDtypeStruct((D, D), BF16)),
        in_specs=[col(D), col(D)] + _band_specs_t(nb, ident) + _band_specs_t(nb, ident) + [
            row, row, _resident((D, D)), _resident((1, D)), pl.BlockSpec(memory_space=pltpu.SMEM)],
        out_specs=(row, row, col(D), col(N_HEADS), pl.BlockSpec((1, 1), lambda i: (0, 0)),
                   pl.BlockSpec((1, D), lambda i: (0, 0)), whole, whole),
        scratch_shapes=[pltpu.VMEM((KV_W, TQ + 2 * BLK), BF16), pltpu.VMEM((KV_W, TQ + 2 * BLK), BF16),
                        pltpu.VMEM((D, TQ), F32)],
        compiler_params=_params(56),
    )(qt, gatet, kt, kt, kt, vt, vt, vt, x1, tgt, wout, gf, sink)


def _l1_attn_bwd(dx2b, wout, qt, kt, vt, gatet, att, lse, sink):
    seq = dx2b.shape[0]
    nq, nb = seq // TQ, seq // BLK

    def body(dx_ref, wo_ref, q_ref, gate_ref, kp_ref, k_ref, kn_ref, vp_ref, v_ref, vn_ref, att_ref, lse_ref, sink_ref,
             dq_ref, dgate_ref, dk_ref, dv_ref, dsink_ref, kbuf, vbuf, dkacc, dvacc, dat_scr, delta_scr, dsacc):
        i = pl.program_id(0)

        @pl.when(i == 0)
        def _():
            dkacc[...] = jnp.zeros_like(dkacc)
            dvacc[...] = jnp.zeros_like(dvacc)
            dsacc[...] = jnp.zeros_like(dsacc)

        @pl.when(i > 0)
        def _():
            for acc in (dkacc, dvacc):
                acc[:, 0:2 * BLK] = acc[:, TQ:TQ + 2 * BLK]
                acc[:, 2 * BLK:2 * BLK + TQ] = jnp.zeros((KV_W, TQ), F32)

        @pl.when(i < nq)
        def _():
            _fill_band(kbuf, kp_ref, k_ref, kn_ref)
            _fill_band(vbuf, vp_ref, v_ref, vn_ref)
            dyt = _nt(wo_ref[...], dx_ref[...])
            sg, dsg = _silu_and_grad(gate_ref[...].astype(F32))
            attf = att_ref[...].astype(F32)
            dat = dyt * sg
            dat_scr[...] = dat.astype(BF16)
            dgate_ref[...] = (dyt * attf * dsg).astype(BF16)
            dl = dat * attf
            delta_scr[...] = jnp.concatenate(
                [jnp.sum(dl[h * HD:(h + 1) * HD, :], axis=0, keepdims=True) for h in range(N_HEADS)], axis=0)
            ones_rows = _ones_rows(2, 3 * BLK)
            groups = [list(range(0, N_HEADS, HPP))[g:g + BWD_GROUP] for g in range(0, N_HEADS // HPP, BWD_GROUP)]
            work = [(j, grp) for j in range(TQ // BLK) for grp in groups]

            def scores(j, passes):
                c0 = j * BLK
                st = dict(c0=c0, passes=passes, bias=_band_bias_t(i * (TQ // BLK) + j, nb))
                st["kv_rows"] = [slice(h0 // GQA * HD, (h0 // GQA + 1) * HD) for h0 in passes]
                st["q4s"] = [_heads_t(q_ref, h0, c0) for h0 in passes]
                st["do4s"] = [_heads_t(dat_scr, h0, c0) for h0 in passes]
                st["lse4s"] = [_row4(lse_ref, h0, c0) for h0 in passes]
                st["delta4s"] = [_row4(delta_scr, h0, c0) for h0 in passes]
                st["kths"] = [kbuf[rows, c0:c0 + 3 * BLK] for rows in st["kv_rows"]]
                st["sts"] = [_tn(jnp.concatenate([kth, ones_rows], axis=0),
                                 jnp.concatenate([q4, _minus_rows(lse4)], axis=0))
                             for kth, q4, lse4 in zip(st["kths"], st["q4s"], st["lse4s"])]
                st["dpds"] = [_tn(jnp.concatenate([vbuf[rows, c0:c0 + 3 * BLK], ones_rows], axis=0),
                                  jnp.concatenate([do4, _minus_rows(delta4)], axis=0))
                              for rows, do4, delta4 in zip(st["kv_rows"], st["do4s"], st["delta4s"])]
                return st

            def elementwise(st):
                st["ps"] = [jnp.exp(_masked(s_, st["bias"])) for s_ in st["sts"]]
                st["dss"] = [(p * dpd).astype(BF16) for p, dpd in zip(st["ps"], st["dpds"])]

            def gradients(st):
                c0 = st["c0"]
                dq4s = [_mm(kth, ds) * SCALE for kth, ds in zip(st["kths"], st["dss"])]
                dks = [_nt(q4, ds) for q4, ds in zip(st["q4s"], st["dss"])]
                dvs = [_nt(do4, p.astype(BF16)) for do4, p in zip(st["do4s"], st["ps"])]
                for h0, rows, dq4, dk, dv, lse4, delta4 in zip(st["passes"], st["kv_rows"], dq4s, dks, dvs, st["lse4s"],
                                                               st["delta4s"]):
                    dkacc[rows, c0:c0 + 3 * BLK] += dk
                    dvacc[rows, c0:c0 + 3 * BLK] += dv
                    dsk = -jnp.exp(_sink_row(sink_ref, h0) - lse4) * delta4
                    for g in range(HPP):
                        h = h0 + g
                        dq_ref[h * HD:(h + 1) * HD, c0:c0 + BLK] = dq4[:, g * BLK:(g + 1) * BLK].astype(BF16)
                        dsacc[h:h + 1, :] += dsk[:, g * BLK:(g + 1) * BLK]

            ahead = [scores(*w) for w in work[:BWD_AHEAD]]
            for n in range(len(work)):
                if n + BWD_AHEAD < len(work):
                    ahead.append(scores(*work[n + BWD_AHEAD]))
                state = ahead.pop(0)
                elementwise(state)
                gradients(state)

        dk_ref[...] = dkacc[:, 0:TQ].astype(BF16)
        dv_ref[...] = dvacc[:, 0:TQ].astype(BF16)

        @pl.when(i == nq)
        def _():
            dsink_ref[...] = jnp.broadcast_to(jnp.sum(dsacc[...], axis=1, keepdims=True), (N_HEADS, LANES))

    clamp = lambda i: jnp.minimum(i, nq - 1)
    row = pl.BlockSpec((TQ, D), lambda i: (clamp(i), 0))
    col = lambda rows: pl.BlockSpec((rows, TQ), lambda i: (0, clamp(i)))
    pad = pl.BlockSpec((KV_W, TQ), lambda i: (0, i))
    return pl.pallas_call(
        body, grid=(nq + 1,), name="l1_attn_bwd",
        out_shape=(jax.ShapeDtypeStruct((D, seq), BF16), jax.ShapeDtypeStruct((D, seq), BF16),
                   jax.ShapeDtypeStruct((KV_W, seq + TQ), BF16), jax.ShapeDtypeStruct((KV_W, seq + TQ), BF16),
                   jax.ShapeDtypeStruct((N_HEADS, LANES), F32)),
        in_specs=[row, _resident((D, D)), col(D), col(D)] + _band_specs_t(nb, clamp) + _band_specs_t(nb, clamp) + [
            col(D), col(N_HEADS), pl.BlockSpec(memory_space=pltpu.SMEM)],
        out_specs=(col(D), col(D), pad, pad, pl.BlockSpec((N_HEADS, LANES), lambda i: (0, 0))),
        scratch_shapes=[pltpu.VMEM((KV_W, TQ + 2 * BLK), BF16), pltpu.VMEM((KV_W, TQ + 2 * BLK), BF16),
                        pltpu.VMEM((KV_W, TQ + 2 * BLK), F32), pltpu.VMEM((KV_W, TQ + 2 * BLK), F32),
                        pltpu.VMEM((D, TQ), BF16), pltpu.VMEM((N_HEADS, TQ), F32), pltpu.VMEM((N_HEADS, LANES), F32)],
        compiler_params=_params(56),
    )(dx2b, wout, qt, gatet, kt, kt, kt, vt, vt, vt, att, lse, sink)


def _l1_in_proj_bwd(dq_r, dk_r, dv, dgate, cos_t, sin_t, w_t, x1, g1, dx2):
    seq = x1.shape[0]
    tm = 512

    def body(dq_ref, dk_ref, dv_ref, dg_ref, c_ref, s_ref, w_ref, x_ref, g_ref, dres_ref,
             dx_ref, dxb_ref, dz_ref, dn_ref):
        @pl.when(pl.program_id(0) == 0)
        def _():
            dn_ref[...] = jnp.zeros_like(dn_ref)

        c, s = c_ref[...], s_ref[...]
        dq = _rope_t(dq_ref[...].astype(F32), c, s, N_HEADS, -1).astype(BF16)
        dk = _rope_t(dk_ref[...].astype(F32), c, s, N_KV, -1).astype(BF16)
        dz = jnp.concatenate([dq, dk, dv_ref[...], dg_ref[...]], axis=0)
        dz_ref[...] = dz
        dh = _tn(dz, w_ref[...])
        xf = x_ref[...]
        r = lax.rsqrt(jnp.mean(xf * xf, axis=1, keepdims=True) + EPS)
        xn = xf * r
        dn_ref[...] += jnp.sum(dh * xn, axis=0, keepdims=True)
        dxn = dh * g_ref[...]
        dx = dres_ref[...] + r * (dxn - xn * jnp.mean(dxn * xn, axis=1, keepdims=True))
        dx_ref[...] = dx
        dxb_ref[...] = dx.astype(BF16)

    row = pl.BlockSpec((tm, D), lambda i: (i, 0))
    col = lambda rows: pl.BlockSpec((rows, tm), lambda i: (0, i))
    return pl.pallas_call(
        body, grid=(seq // tm,), name="l1_in_proj_bwd",
        out_shape=(jax.ShapeDtypeStruct((seq, D), F32), jax.ShapeDtypeStruct((seq, D), BF16),
                   jax.ShapeDtypeStruct((MIX1_IN, seq), BF16), jax.ShapeDtypeStruct((1, D), F32)),
        in_specs=[col(D), col(KV_W), col(KV_W), col(D), col(ROT_HALF), col(ROT_HALF), _resident((MIX1_IN, D)), row,
                  _resident((1, D)), row],
        out_specs=(row, row, col(MIX1_IN), pl.BlockSpec((1, D), lambda i: (0, 0))),
        compiler_params=_params(48),
    )(dq_r, dk_r, dv, dgate, cos_t, sin_t, w_t, x1, g1, dx2)


def _l0_mix_bwd(dx1b, wout, za, bx, bg, ws, ws_t, bias, gv, wg, wg_t, scale):
    seq = dx1b.shape[0]
    ts = 256
    n_tiles = seq // ts

    def body(dx_ref, wo_ref, za_ref, bx_ref, bxp_ref, bxn_ref, bg_ref, ws_ref, wst_ref, bias_ref, gv_ref, wg_ref,
             wgt_ref, sc_ref,
             dz_ref, dp_ref, catt_ref, dws_ref, dbias_ref, dgv_ref, dsc_ref, dwg_ref, db_ref, xe_ref, *tmp_refs):
        i = pl.program_id(0)

        @pl.when(i == 0)
        def _():
            for r_ in (dws_ref, dbias_ref, dgv_ref, dsc_ref, dwg_ref, db_ref):
                r_[...] = jnp.zeros_like(r_)

        dxb = dx_ref[...]
        dya = _nt(dxb, wo_ref[0:D, :])
        dyb = _nt(dxb, wo_ref[D:2 * D, :])

        vg, dvg_dz = _gelu_and_grad(za_ref[:, D:2 * D].astype(F32))
        rv = lax.rsqrt(jnp.mean(vg * vg, axis=1, keepdims=True) + EPS)
        vnorm = vg * rv
        gvw = gv_ref[...]
        vnb = (vnorm * gvw).astype(BF16)
        mixed = _spatial_mix(ws_ref, vnb, bias_ref[...], ts)

        _fill_halo(xe_ref, bx_ref[...], bxp_ref, bxn_ref, i, n_tiles, ts)
        pb = _pool_forward(xe_ref, tmp_refs, ts, i * ts, seq).astype(BF16)
        ypre = jnp.concatenate([_mm(pb[:, g * GDIM:(g + 1) * GDIM], wg_ref[g]) for g in range(4)], axis=1)

        u, du = _gelu_and_grad(za_ref[:, 0:D].astype(F32))
        sga, dsga = _silu_and_grad(za_ref[:, 2 * D:3 * D].astype(F32))
        um = u * mixed
        ya = (um * sga).astype(BF16)
        t = dya * sga
        dz_ref[:, 0:D] = (t * mixed * du).astype(BF16)
        dz_ref[:, 2 * D:3 * D] = (dya * um * dsga).astype(BF16)
        dmixed = t * u
        dmb = dmixed.astype(BF16)
        dvn_rows = []
        for c in range(ts // CHUNK):
            rows = slice(c * CHUNK, (c + 1) * CHUNK)
            parts = []
            for h in range(A_GROUPS):
                cols = slice(h * GDIM, (h + 1) * GDIM)
                dws_ref[h] += _nt(dmb[rows, cols], vnb[rows, cols])
                parts.append(_mm(wst_ref[h], dmb[rows, cols]))
            dvn_rows.append(jnp.concatenate(parts, axis=1))

        sc = sc_ref[...]
        y = ypre * sc
        sgb, dsgb = _silu_and_grad(bg_ref[...].astype(F32))
        yb = (y * sgb).astype(BF16)
        dy_b = dyb * sgb
        dz_ref[:, 3 * D:4 * D] = jnp.zeros((ts, D), BF16)
        dz_ref[:, 4 * D:5 * D] = (dyb * y * dsgb).astype(BF16)
        dsc_ref[...] += jnp.sum(dy_b * ypre, axis=0, keepdims=True)
        dypre = (dy_b * sc).astype(BF16)
        dps = []
        for g in range(4):
            cols = slice(g * GDIM, (g + 1) * GDIM)
            dwg_ref[g] += _tn(pb[:, cols], dypre[:, cols])
            dps.append(_mm(dypre[:, cols], wgt_ref[g]))

        dbias = dmixed[0:CHUNK, :]
        for c in range(1, ts // CHUNK):
            dbias = dbias + dmixed[c * CHUNK:(c + 1) * CHUNK, :]
        dbias_ref[...] += dbias
        dvn = jnp.concatenate(dvn_rows, axis=0)
        dgv_ref[...] += jnp.sum(dvn * vnorm, axis=0, keepdims=True)
        dxn = dvn * gvw
        dvg = rv * (dxn - vnorm * jnp.mean(dxn * vnorm, axis=1, keepdims=True))
        dz_ref[:, D:2 * D] = (dvg * dvg_dz).astype(BF16)

        dp_ref[...] = jnp.concatenate(dps, axis=1)
        catt_ref[...] = jnp.concatenate([ya, yb], axis=1).T

        @pl.when(i == n_tiles - 1)
        def _():
            for h in range(A_GROUPS):
                tot = jnp.sum(dbias_ref[:, h * GDIM:(h + 1) * GDIM].T, axis=0, keepdims=True)
                db_ref[pl.ds(h * 8, 8), :] = jnp.broadcast_to(tot, (8, CHUNK))

    prev, nxt = _halo_specs(ts, seq, D)
    row = lambda w_: pl.BlockSpec((ts, w_), lambda i: (i, 0))
    acc = lambda shape: pl.BlockSpec(shape, lambda i: (0,) * len(shape))
    return pl.pallas_call(
        body, grid=(n_tiles,), name="l0_mix_bwd",
        out_shape=(jax.ShapeDtypeStruct((seq, MIX0_IN), BF16), jax.ShapeDtypeStruct((seq, D), F32),
                   jax.ShapeDtypeStruct((2 * D, seq), BF16),
                   jax.ShapeDtypeStruct((4, CHUNK, CHUNK), F32), jax.ShapeDtypeStruct((CHUNK, D), F32),
                   jax.ShapeDtypeStruct((1, D), F32), jax.ShapeDtypeStruct((1, D), F32),
                   jax.ShapeDtypeStruct((4, GDIM, GDIM), F32), jax.ShapeDtypeStruct((32, CHUNK), F32)),
        in_specs=[row(D), _resident((2 * D, D)), row(3 * D), row(D), prev, nxt, row(D), _resident((4, CHUNK, CHUNK)),
                  _resident((4, CHUNK, CHUNK)), _resident((CHUNK, D)), _resident((1, D)), _resident((4, GDIM, GDIM)),
                  _resident((4, GDIM, GDIM)), _resident((1, D))],
        out_specs=(row(MIX0_IN), row(D), pl.BlockSpec((2 * D, ts), lambda i: (0, i)),
                   acc((4, CHUNK, CHUNK)), acc((CHUNK, D)), acc((1, D)), acc((1, D)), acc((4, GDIM, GDIM)),
                   acc((32, CHUNK))),
        scratch_shapes=_pool_scratch(ts),
        compiler_params=_params(56),
    )(dx1b, wout, za, bx, bx, bx, bg, ws, ws_t, bias, gv, wg, wg_t, scale)


def _l0_pool_bwd(dp, dz):
    seq = dp.shape[0]
    ts = 512
    n_tiles = seq // ts
    ext = ts + 2 * POOL_HALO

    def body(dp_ref, dpp_ref, dpn_ref, dz_ref, out_ref, qe_ref, *tmp_refs):
        i = pl.program_id(0)
        _fill_halo(qe_ref, dp_ref[...], dpp_ref, dpn_ref, i, n_tiles, ts)
        te = i * ts - POOL_HALO + lax.broadcasted_iota(jnp.int32, (ext, 1), 0)
        for gi, w in enumerate(POOL_WINDOWS):
            hw = w // 2
            cols = slice(gi * GDIM, (gi + 1) * GDIM)
            cnt = jnp.maximum(jnp.minimum(te + hw, seq) - jnp.maximum(te - hw, 0), 1).astype(F32)
            qe_ref[pl.ds(0, ext), cols] = qe_ref[pl.ds(0, ext), cols] / cnt
        outs = []
        for gi, w in enumerate(POOL_WINDOWS):
            cols = slice(gi * GDIM, (gi + 1) * GDIM)
            outs.append(_window_sums(qe_ref, tmp_refs, ts, cols, w, 1) - dp_ref[:, cols])
        out_ref[...] = jnp.concatenate(outs, axis=1).astype(BF16)

    prev, nxt = _halo_specs(ts, seq, D)
    row = pl.BlockSpec((ts, D), lambda i: (i, 0))
    return pl.pallas_call(
        body, grid=(n_tiles,), name="l0_pool_bwd",
        out_shape=jax.ShapeDtypeStruct(dz.shape, BF16),
        in_specs=[row, prev, nxt, pl.BlockSpec(memory_space=pl.ANY)],
        out_specs=pl.BlockSpec((ts, D), lambda i: (i, 3)),
        input_output_aliases={3: 0},
        scratch_shapes=_pool_scratch(ts),
        compiler_params=_params(32),
    )(dp, dp, dp, dz)


def _l0_in_proj_bwd(dz, w, x, g0, dx1):
    seq = x.shape[0]
    tm = 512

    def body(dz_ref, w_ref, x_ref, g_ref, dres_ref, dx_ref, dn_ref):
        @pl.when(pl.program_id(0) == 0)
        def _():
            dn_ref[...] = jnp.zeros_like(dn_ref)

        dh = _nt(dz_ref[...], w_ref[...])
        xf = x_ref[...]
        r = lax.rsqrt(jnp.mean(xf * xf, axis=1, keepdims=True) + EPS)
        xn = xf * r
        dn_ref[...] += jnp.sum(dh * xn, axis=0, keepdims=True)
        dxn = dh * g_ref[...]
        dx_ref[...] = dres_ref[...] + r * (dxn - xn * jnp.mean(dxn * xn, axis=1, keepdims=True))

    row = lambda w_: pl.BlockSpec((tm, w_), lambda i: (i, 0))
    return pl.pallas_call(
        body, grid=(seq // tm,), name="l0_in_proj_bwd",
        out_shape=(jax.ShapeDtypeStruct((seq, D), F32), jax.ShapeDtypeStruct((1, D), F32)),
        in_specs=[row(MIX0_IN), _resident((D, MIX0_IN)), row(D), _resident((1, D)), row(D)],
        out_specs=(row(D), pl.BlockSpec((1, D), lambda i: (0, 0))),
        compiler_params=_params(56),
    )(dz, w, x, g0, dx1)


def _dw_matmul(a_t, b, name, b_transposed=False, tn=1024, ts=1024, col_block=None):
    k, seq = a_t.shape
    n = b.shape[0] if b_transposed else b.shape[1]
    tn = min(n, tn)
    assert seq % ts == 0 and n % tn == 0 and (col_block is None or tn % col_block == 0)
    n_s = seq // ts
    per = 1 if col_block is None else tn // col_block

    def body(a_ref, b_ref, o_ref, ob_ref, acc_ref):
        s = pl.program_id(1)

        @pl.when(s == 0)
        def _():
            acc_ref[...] = jnp.zeros_like(acc_ref)

        acc_ref[...] += _nt(a_ref[...], b_ref[...]) if b_transposed else _mm(a_ref[...], b_ref[...])

        @pl.when(s == n_s - 1)
        def _():
            acc = acc_ref[...]
            if col_block is None:
                o_ref[...] = acc
                ob_ref[...] = acc.astype(BF16)
            else:
                for i in range(per):
                    piece = acc[:, i * col_block:(i + 1) * col_block]
                    o_ref[i] = piece
                    ob_ref[i] = piece.astype(BF16)

    b_spec = (pl.BlockSpec((tn, ts), lambda j, s: (j, s)) if b_transposed else pl.BlockSpec((ts, tn), lambda j, s: (s, j)))
    if col_block is None:
        shape, o_spec = (k, n), pl.BlockSpec((k, tn), lambda j, s: (0, j))
    else:
        shape, o_spec = (n // col_block, k, col_block), pl.BlockSpec((per, k, col_block), lambda j, s: (j, 0, 0))
    return pl.pallas_call(
        body, grid=(n // tn, n_s), name=name,
        out_shape=(jax.ShapeDtypeStruct(shape, F32), jax.ShapeDtypeStruct(shape, BF16)),
        in_specs=[pl.BlockSpec((k, ts), lambda j, s: (0, s)), b_spec],
        out_specs=(o_spec, o_spec),
        scratch_shapes=[pltpu.VMEM((k, tn), F32)],
        compiler_params=_params(56, 2),
    )(a_t, b)


ROW_TILES = 8


def _cast_shards(shards):
    n = len(shards)

    def body(*refs):
        for a in range(n):
            refs[n + a][...] = refs[a][...].astype(BF16)

    vm = pl.BlockSpec(memory_space=pltpu.VMEM)
    return pl.pallas_call(body, name="cast_weights", out_shape=[jax.ShapeDtypeStruct(t.shape, BF16) for t in shards],
                          in_specs=[vm] * n, out_specs=[vm] * n, compiler_params=_params(32, 0))(*shards)


def _adamw_math(w, g, m, v):
    m2 = ADAM_B1 * m + (1.0 - ADAM_B1) * g
    v2 = ADAM_B2 * v + (1.0 - ADAM_B2) * (g * g)
    m_hat = m2 / (1.0 - ADAM_B1 ** ADAM_STEP)
    v_hat = v2 / (1.0 - ADAM_B2 ** ADAM_STEP)
    delta = -ADAM_LR * (m_hat / (jnp.sqrt(v_hat) + ADAM_EPS) + ADAM_WD * w)
    return delta, m2, v2


def _final_sum_adamw(g_list, recv_list, me, w_list, m_list, v_list):
    n = len(w_list)

    def body(me_ref, *refs):
        own, recv, w, m, v = (refs[k * n:(k + 1) * n] for k in range(5))
        outs = [refs[(5 + k) * n:(6 + k) * n] for k in range(4)]
        for a in range(n):
            g = own[a][...]
            for k in range(N_DEV - 1):
                g = g + recv[a][k].astype(F32)
            delta, m2, v2 = _adamw_math(w[a][...], g, m[a][...], v[a][...])
            for o_ref, val in zip((outs[0][a], outs[1][a], outs[2][a], outs[3][a]), (g, delta, m2, v2)):
                o_ref[...] = val

    own_specs, flat, wire, shapes = [], [], [], []
    for t in w_list:
        rows, width = t.shape
        tr = rows // ROW_TILES
        own_specs.append(pl.BlockSpec((None, tr, width), lambda i, me: (me[0], i, 0)))
        flat.append(pl.BlockSpec((tr, width), lambda i, me: (i, 0)))
        wire.append(pl.BlockSpec((N_DEV - 1, tr, width), lambda i, me: (0, i, 0)))
        shapes.append(jax.ShapeDtypeStruct((rows, width), F32))
    out = pl.pallas_call(
        body, name="grad_sum_adamw", out_shape=shapes * 4,
        grid_spec=pltpu.PrefetchScalarGridSpec(
            num_scalar_prefetch=1, grid=(ROW_TILES,), in_specs=own_specs + wire + flat * 3, out_specs=flat * 4),
        compiler_params=_params(40),
    )(me, *g_list, *recv_list, *w_list, *m_list, *v_list)
    return [out[k * n:(k + 1) * n] for k in range(4)]


SMALL_NAMES = ("norm_0", "a_v_norm_0", "b_scale_0", "norm_1", "final_norm", "a_spatial_w_0", "a_spatial_b_0", "sink_1")
SMALL_VIEWS = ((8, LANES),) * 5 + ((4 * CHUNK, LANES), (4, LANES), (1, N_HEADS))
SMALL_ROW0 = (0, 8, 16, 24, 32, 40, 552, 560)
SMALL_ROWS = 568


def _small_sum_adamw(early, late, w_list, m_list, v_list):
    n = len(w_list)

    def body(e_ref, l_ref, *refs):
        gtot, first = e_ref[0], l_ref[0]
        for d in range(1, N_DEV):
            gtot = gtot + e_ref[d]
            first = first + l_ref[d]
        for a, ((rows, width), r0) in enumerate(zip(SMALL_VIEWS, SMALL_ROW0)):
            g = first if SMALL_NAMES[a] == "norm_0" else gtot[r0:r0 + rows, 0:width]
            delta, m2, v2 = _adamw_math(refs[a][...], g, refs[n + a][...], refs[2 * n + a][...])
            for k, val in enumerate((g, delta, m2, v2)):
                refs[(3 + k) * n + a][...] = val
        refs[7 * n][...] = gtot[LOSS_ROW:LOSS_ROW + 1, LOSS_LANE:LOSS_LANE + 1]

    vm = pl.BlockSpec(memory_space=pltpu.VMEM)
    shapes = [jax.ShapeDtypeStruct(s, F32) for s in SMALL_VIEWS]
    out = pl.pallas_call(
        body, name="small_sum_adamw", out_shape=shapes * 4 + [jax.ShapeDtypeStruct((1, 1), F32)],
        in_specs=[vm, vm] + [vm] * (3 * n), out_specs=[vm] * (4 * n + 1),
    )(early, late, *w_list, *m_list, *v_list)
    return [out[k * n:(k + 1) * n] for k in range(4)], out[4 * n]


PEER_FLIPS = tuple((fx, fy, fc) for fx in (0, 1) for fy in (0, 1) for fc in (0, 1))[1:]


def _sequencer_all_gather(blks, name, collective_id, concat_rows=False):
    n = len(blks)

    def body(*refs):
        ins, outs = refs[:n], refs[n:2 * n]
        send_sems, recv_sems, local_sems = refs[2 * n:]
        x, y, c = lax.axis_index("x"), lax.axis_index("y"), lax.axis_index("c")
        peers = [(x ^ fx, y ^ fy, c ^ fc) for fx, fy, fc in PEER_FLIPS]
        barrier = pltpu.get_barrier_semaphore()
        for peer in peers:
            pl.semaphore_signal(barrier, inc=1, device_id=peer, device_id_type=MESH)
        pl.semaphore_wait(barrier, len(peers))
        me = 4 * x + 2 * y + c

        def slot(a):
            rows = blks[a].shape[0]
            return outs[a].at[pl.ds(pl.multiple_of(me * rows, 16), rows)] if concat_rows else outs[a].at[me]

        copies = [pltpu.make_async_remote_copy(
            src_ref=ins[a], dst_ref=slot(a), send_sem=send_sems.at[k, a], recv_sem=recv_sems.at[k, a],
            device_id=peer, device_id_type=MESH) for k, peer in enumerate(peers) for a in range(n)]
        mine = [pltpu.make_async_copy(ins[a], slot(a), local_sems.at[a]) for a in range(n)]
        for cp in copies + mine:
            cp.start()
        for cp in copies + mine:
            cp.wait()

    out_shape = (lambda t: (N_DEV * t.shape[0],) + t.shape[1:]) if concat_rows else (lambda t: (N_DEV,) + t.shape)
    return pl.kernel(
        body, out_type=[jax.ShapeDtypeStruct(out_shape(t), t.dtype) for t in blks],
        mesh=plsc.ScalarSubcoreMesh(axis_name="sequencer", num_cores=1), name=name,
        scratch_types=[pltpu.SemaphoreType.DMA((7, n)), pltpu.SemaphoreType.DMA((7, n)), pltpu.SemaphoreType.DMA((n,))],
        compiler_params=pltpu.CompilerParams(collective_id=collective_id),
    )(*blks)


def _sequencer_scatter(g_list, name, collective_id):
    n = len(g_list)

    def body(*refs):
        ins, outs = refs[:n], refs[n:2 * n]
        send_sems, recv_sems = refs[2 * n:]
        x, y, c = lax.axis_index("x"), lax.axis_index("y"), lax.axis_index("c")
        peers = [(x ^ fx, y ^ fy, c ^ fc) for fx, fy, fc in PEER_FLIPS]
        barrier = pltpu.get_barrier_semaphore()
        for peer in peers:
            pl.semaphore_signal(barrier, inc=1, device_id=peer, device_id_type=MESH)
        pl.semaphore_wait(barrier, len(peers))
        copies = [pltpu.make_async_remote_copy(
            src_ref=ins[a].at[4 * px + 2 * py + pc], dst_ref=outs[a].at[k], send_sem=send_sems.at[k, a],
            recv_sem=recv_sems.at[k, a], device_id=(px, py, pc), device_id_type=MESH)
            for k, (px, py, pc) in enumerate(peers) for a in range(n)]
        for cp in copies:
            cp.start()
        for cp in copies:
            cp.wait()

    return pl.kernel(
        body, out_type=[jax.ShapeDtypeStruct((N_DEV - 1,) + g.shape[1:], g.dtype) for g in g_list],
        mesh=plsc.ScalarSubcoreMesh(axis_name="sequencer", num_cores=1), name=name,
        scratch_types=[pltpu.SemaphoreType.DMA((7, n)), pltpu.SemaphoreType.DMA((7, n))],
        compiler_params=pltpu.CompilerParams(collective_id=collective_id),
    )(*g_list)


def _direct_all_gather(blk, name):
    def body(g_ref, out_ref, send_sems, recv_sems, local_sem):
        x, y, c = lax.axis_index("x"), lax.axis_index("y"), lax.axis_index("c")
        me = 4 * x + 2 * y + c
        copies = [pltpu.make_async_remote_copy(
            src_ref=g_ref, dst_ref=out_ref.at[me], send_sem=send_sems.at[k], recv_sem=recv_sems.at[k],
            device_id=(x ^ fx, y ^ fy, c ^ fc), device_id_type=MESH) for k, (fx, fy, fc) in enumerate(PEER_FLIPS)]
        copies.append(pltpu.make_async_copy(g_ref, out_ref.at[me], local_sem))
        for cp in copies:
            cp.start()
        for cp in copies:
            cp.wait()

    any_spec = pl.BlockSpec(memory_space=pl.ANY)
    return pl.pallas_call(
        body, name=name, out_shape=jax.ShapeDtypeStruct((N_DEV,) + blk.shape, blk.dtype),
        in_specs=[any_spec], out_specs=any_spec,
        scratch_shapes=[pltpu.SemaphoreType.DMA((7,)), pltpu.SemaphoreType.DMA((7,)), pltpu.SemaphoreType.DMA],
    )(blk)


def _shard_views(w_in_0, b_group_w_0, w_out_0, w_in_1, w_out_1):
    return [w_in_0, b_group_w_0.reshape(4 * 32, GDIM), w_out_0, w_in_1, w_out_1]


def _small_views(named):
    return [named[name].reshape(view) for name, view in zip(SMALL_NAMES, SMALL_VIEWS)]


LOSS_ROW, LOSS_LANE = 560, N_HEADS


def _pack_small_grads(named, loss_part):
    rows = []
    for name, (r, w) in zip(SMALL_NAMES, SMALL_VIEWS):
        pad_r = -r % 8
        if name == "sink_1":
            t = jnp.concatenate([named[name].reshape(r, w), loss_part], axis=1)
            rows.append(jnp.pad(t, ((0, pad_r), (0, LANES - w - 1))))
        elif name in named:
            rows.append(jnp.pad(named[name].reshape(r, w), ((0, pad_r), (0, LANES - w))))
        else:
            rows.append(jnp.zeros((r + pad_r, LANES), F32))
    return jnp.concatenate(rows, axis=0)


def _device_blocks(t, axis):
    shape = t.shape
    t = t.reshape(shape[:axis] + (N_DEV, shape[axis] // N_DEV) + shape[axis + 1:])
    t = jnp.moveaxis(t, axis, 0)
    return t.reshape(N_DEV, -1, shape[-1] if axis != len(shape) - 1 else shape[-1] // N_DEV)


def kernel(x, norm_0, w_in_0, a_v_norm_0, a_spatial_w_0, a_spatial_b_0, b_group_w_0, b_scale_0, w_out_0, norm_1, w_in_1, sink_1, w_out_1, final_norm, loss_target, m_norm_0, m_w_in_0, m_a_v_norm_0, m_a_spatial_w_0, m_a_spatial_b_0, m_b_group_w_0, m_b_scale_0, m_w_out_0, m_norm_1, m_w_in_1, m_sink_1, m_w_out_1, m_final_norm, v_norm_0, v_w_in_0, v_a_v_norm_0, v_a_spatial_w_0, v_a_spatial_b_0, v_b_group_w_0, v_b_scale_0, v_w_out_0, v_norm_1, v_w_in_1, v_sink_1, v_w_out_1, v_final_norm):
    seq = x.shape[1]
    xs = x.reshape(seq, D)
    tgt = loss_target.reshape(seq, D)
    ax, ay, ac = lax.axis_index("x"), lax.axis_index("y"), lax.axis_index("c")
    me = jnp.reshape(4 * ax + 2 * ay + ac, (1,)).astype(jnp.int32)

    shards = _shard_views(w_in_0, b_group_w_0, w_out_0, w_in_1, w_out_1)
    cast = _cast_shards([shards[0], shards[1], shards[2], w_in_1.T, shards[4]])

    def l1_weights(after):
        blks, _ = lax.optimization_barrier((cast[3:5], after))
        return _sequencer_all_gather(blks, "weights_gather_l1", 2, concat_rows=True)

    blocks, received, early = {}, {}, {}
    collective_ids = {"l1": 3, "out0": 4, "in0": 5}

    def scatter(tag, own_blocks, wire_blocks):
        blocks[tag] = own_blocks
        received[tag] = _sequencer_scatter(wire_blocks, "grad_scatter_" + tag, collective_ids[tag])

    def small_early(named, loss_part):
        early["small"] = _sequencer_all_gather([_pack_small_grads(named, loss_part)], "small_grad_gather", 6)[0]

    grad_x, d_norm_0 = _local_step(xs, tgt, cast[0], cast[1:3], l1_weights, norm_0, a_v_norm_0, a_spatial_w_0,
                                   a_spatial_b_0, b_scale_0, norm_1, sink_1, final_norm, scatter, small_early)

    order = (("in0", 0), ("in0", 1), ("out0", 0), ("l1", 0), ("l1", 1))
    late = _direct_all_gather(d_norm_0.reshape(8, LANES), "norm_grad_gather")
    shards_late, _ = lax.optimization_barrier((shards, grad_x))
    big = _final_sum_adamw([blocks[t][i] for t, i in order], [received[t][i] for t, i in order], me, shards_late,
                           _shard_views(m_w_in_0, m_b_group_w_0, m_w_out_0, m_w_in_1, m_w_out_1),
                           _shard_views(v_w_in_0, v_b_group_w_0, v_w_out_0, v_w_in_1, v_w_out_1))
    weights = dict(norm_0=norm_0, a_v_norm_0=a_v_norm_0, a_spatial_w_0=a_spatial_w_0, a_spatial_b_0=a_spatial_b_0,
                   b_scale_0=b_scale_0, norm_1=norm_1, sink_1=sink_1, final_norm=final_norm)
    m_small = dict(norm_0=m_norm_0, a_v_norm_0=m_a_v_norm_0, a_spatial_w_0=m_a_spatial_w_0, a_spatial_b_0=m_a_spatial_b_0,
                   b_scale_0=m_b_scale_0, norm_1=m_norm_1, sink_1=m_sink_1, final_norm=m_final_norm)
    v_small = dict(norm_0=v_norm_0, a_v_norm_0=v_a_v_norm_0, a_spatial_w_0=v_a_spatial_w_0, a_spatial_b_0=v_a_spatial_b_0,
                   b_scale_0=v_b_scale_0, norm_1=v_norm_1, sink_1=v_sink_1, final_norm=v_final_norm)
    small, loss = _small_sum_adamw(early["small"], late, _small_views(weights), _small_views(m_small),
                                   _small_views(v_small))

    def in_order(kind):
        b = [b_.reshape(s_.shape) for b_, s_ in zip(big[kind], (w_in_0, b_group_w_0, w_out_0, w_in_1, w_out_1))]
        s = {name: t.reshape(weights[name].shape) for name, t in zip(SMALL_NAMES, small[kind])}
        return [s["norm_0"], b[0], s["a_v_norm_0"], s["a_spatial_w_0"], s["a_spatial_b_0"], b[1], s["b_scale_0"], b[2],
                s["norm_1"], b[3], s["sink_1"], b[4], s["final_norm"]]

    return (loss[0, 0], grad_x.reshape(1, seq, D), *in_order(0), *in_order(1), *in_order(2), *in_order(3))


def _local_step(xs, tgt, win0_shard, l0_shards, l1_weights, norm_0, a_v_norm_0, a_spatial_w_0, a_spatial_b_0, b_scale_0,
                norm_1, sink_1, final_norm, scatter, small_early):
    seq = xs.shape[0]
    ws = a_spatial_w_0.astype(BF16)
    ws_t = jnp.swapaxes(ws, 1, 2)
    bias = jnp.repeat(a_spatial_b_0.T, GDIM, axis=1)
    g0, gv, scale, g1, gf = (t.reshape(1, D) for t in (norm_0, a_v_norm_0, b_scale_0, norm_1, final_norm))
    cos_t, sin_t = _rope_tables_t(seq)

    za, bx, bg, h0_t, win0, g_wg, wout0 = _l0_in_proj(xs, g0, win0_shard, l0_shards)
    win1_t, wout1 = l1_weights(za)
    wg = g_wg.reshape(N_DEV, 4, 32, GDIM).transpose(1, 0, 2, 3).reshape(4, GDIM, GDIM)
    wg_t = jnp.swapaxes(wg, 1, 2)
    x1 = _l0_mix_fwd(za, bx, bg, xs, ws, bias, gv, wg, scale, wout0)
    win1_t, wout1, x1 = lax.optimization_barrier((win1_t, wout1, x1))
    qt, kt, vt, gatet, h1_t = _l1_in_proj(x1, g1, win1_t, cos_t, sin_t)
    dx2, dx2b, att, lse, loss_part, d_gf, d_wout1, d_wout1_wire = _l1_attn_fwd(
        qt, kt, vt, gatet, x1, tgt, wout1, gf, sink_1)

    dq_r, dgate, dk_pad, dv_pad, d_sink = _l1_attn_bwd(dx2b, wout1, qt, kt, vt, gatet, att, lse, sink_1)
    dk_r = dk_pad[:, BLK:BLK + seq]
    dv = dv_pad[:, BLK:BLK + seq]
    dx1, dx1b, dz1_t, d_g1 = _l1_in_proj_bwd(dq_r, dk_r, dv, dgate, cos_t, sin_t, win1_t, x1, g1, dx2)
    d_win1, d_win1_wire = _dw_matmul(h1_t, dz1_t, "dw_in_1", b_transposed=True, tn=1280, col_block=MIX1_IN // N_DEV)
    rows = lambda t: t.reshape(N_DEV, t.shape[0] // N_DEV, t.shape[1])
    scatter("l1", [d_win1, rows(d_wout1)], [d_win1_wire, rows(d_wout1_wire)])

    dz0, dp, cat_t, d_ws, _, d_gv, d_scale, d_wg, d_b = _l0_mix_bwd(
        dx1b, wout0, za, bx, bg, ws, ws_t, bias, gv, wg, wg_t, scale)
    dz0 = _l0_pool_bwd(dp, dz0)
    d_win0, d_win0_wire = _dw_matmul(h0_t, dz0, "dw_in_0", tn=1280, col_block=MIX0_IN // N_DEV)
    d_wg_blocks = _device_blocks(d_wg, 1)
    scatter("in0", [d_win0, d_wg_blocks], [d_win0_wire, d_wg_blocks])
    cat_t, _ = lax.optimization_barrier((cat_t, d_win0))
    d_wout0, d_wout0_wire = _dw_matmul(cat_t, dx1b, "dw_out_0")
    scatter("out0", [rows(d_wout0)], [rows(d_wout0_wire)])
    small_early(dict(a_v_norm_0=d_gv, a_spatial_w_0=d_ws, a_spatial_b_0=d_b.reshape(4, 8, CHUNK)[:, 0, :],
                     b_scale_0=d_scale, norm_1=d_g1, sink_1=d_sink[:, 0], final_norm=d_gf), loss_part)
    dz0, _ = lax.optimization_barrier((dz0, d_wout0))
    return _l0_in_proj_bwd(dz0, win0, xs, g0, dx1)
```

```python
import jax
import jax.numpy as jnp
from jax import lax
from jax.experimental import pallas as pl
from jax.experimental.pallas import tpu as pltpu
from jax.experimental.pallas import tpu_sc as plsc

F32 = jnp.float32
BF16 = jnp.bfloat16

D = 1024
EPS = 1e-6
NEG_INF = -1e30
CHUNK = 128
A_GROUPS = 4
POOL_WINDOWS = (2, 4, 8, 16)
POOL_HALO = 8
GDIM = 256
N_HEADS = 16
N_KV = 4
GQA = 4
HD = 64
BLK = 128
ROT_HALF = 8
ROPE_THETA = 500000.0
SCALE = HD ** -0.5
MIX0_IN = 5 * D
MIX1_IN = 2560
KV_W = N_KV * HD
Q_ROWS, K_ROWS, V_ROWS, G_ROWS = (0, D), (D, D + KV_W), (D + KV_W, D + 2 * KV_W), (D + 2 * KV_W, MIX1_IN)
TQ = 512

ADAM_LR = 0.001
ADAM_B1 = 0.9
ADAM_B2 = 0.999
ADAM_EPS = 1e-08
ADAM_WD = 0.01
ADAM_STEP = 10

N_DEV = 8
LANES = 128
MIB = 2 ** 20
MESH = pl.DeviceIdType.MESH


def _params(limit_mib, n_axes=1):
    return pltpu.CompilerParams(vmem_limit_bytes=limit_mib * MIB, dimension_semantics=("arbitrary",) * n_axes)


def _resident(shape):
    nd = len(shape)
    return pl.BlockSpec(shape, lambda *_: (0,) * nd, pipeline_mode=pl.Buffered(1))


def _gelu(x):
    k = 0.7978845608028654
    return 0.5 * x * (1.0 + jnp.tanh(k * (x + 0.044715 * x * x * x)))


def _gelu_and_grad(x):
    k = 0.7978845608028654
    x2 = x * x
    t = jnp.tanh(k * (x + 0.044715 * x * x2))
    g = 0.5 * x * (1.0 + t)
    dg = 0.5 * (1.0 + t) + 0.5 * x * (1.0 - t * t) * (k * (1.0 + 3.0 * 0.044715 * x2))
    return g, dg


def _silu_and_grad(x):
    s = jax.nn.sigmoid(x)
    return x * s, s * (1.0 + x * (1.0 - s))


def _nt(a, b):
    return lax.dot_general(a, b, (((1,), (1,)), ((), ())), preferred_element_type=F32)


def _tn(a, b):
    return lax.dot_general(a, b, (((0,), (0,)), ((), ())), preferred_element_type=F32)


def _mm(a, b):
    return jnp.dot(a, b, preferred_element_type=F32)


def _rope_tables_t(seq):
    inv = ROPE_THETA ** (-jnp.arange(0, 2 * ROT_HALF, 2, dtype=F32) / (2 * ROT_HALF))
    ang = inv[:, None] * jnp.arange(seq, dtype=F32)[None, :]
    return jnp.cos(ang), jnp.sin(ang)


def _rope_t(z, c, s, n_heads, sign):
    parts = []
    for h in range(n_heads):
        b = h * HD
        x1, x2 = z[b:b + ROT_HALF], z[b + ROT_HALF:b + 2 * ROT_HALF]
        if sign > 0:
            parts += [x1 * c - x2 * s, x2 * c + x1 * s]
        else:
            parts += [x1 * c + x2 * s, x2 * c - x1 * s]
        parts.append(z[b + 2 * ROT_HALF:b + HD])
    return jnp.concatenate(parts, axis=0)


N_CHIPS = 4
CHIP_COLS = MIX0_IN // N_CHIPS
IN_PROJ_ROWS, IN_PROJ_CHUNK = 512, 256
IN_PROJ_SPLIT = 4
IN_PROJ_PIECES = (
    ((0, 0, CHIP_COLS, 0),),
    ((0, CHIP_COLS, CHIP_COLS, 0),),
    ((0, 2 * CHIP_COLS, 3 * D - 2 * CHIP_COLS, 0), (1, 0, 3 * CHIP_COLS - 3 * D, 3 * D - 2 * CHIP_COLS)),
    ((1, 3 * CHIP_COLS - 3 * D, 4 * D - 3 * CHIP_COLS, 0), (2, 0, D, 4 * D - 3 * CHIP_COLS)),
)


def _l0_in_proj(x, g0, w_shard, later_shards):
    seq = x.shape[0]
    tm = 1024
    n = seq // tm
    f32_cols = max(width for pieces in IN_PROJ_PIECES for o, _, width, _ in pieces if o == 1)
    shard_cols = w_shard.shape[1]
    n_arr = 1 + len(later_shards)
    parts = [(0, s * (D // IN_PROJ_SPLIT), D // IN_PROJ_SPLIT) for s in range(IN_PROJ_SPLIT)]
    parts += [(a + 1, 0, t.shape[0]) for a, t in enumerate(later_shards)]
    w_parts, later_parts = range(IN_PROJ_SPLIT), range(IN_PROJ_SPLIT, len(parts))
    assert 2 * shard_cols == CHIP_COLS and seq % tm == 0 and n >= 4

    def body(*refs):
        x_ref, g_ref = refs[:2]
        ins = refs[2:2 + n_arr]
        za_ref, bx_ref, bg_ref, ht_ref = refs[2 + n_arr:6 + n_arr]
        gathered = refs[6 + n_arr:6 + 2 * n_arr]
        h_all, w_buf, z32, z16, send_sems, recv_sems, local_sems, load_sems, out_sems = refs[6 + 2 * n_arr:]
        p, i = pl.program_id(0), pl.program_id(1)
        ax, ay, ac = lax.axis_index("x"), lax.axis_index("y"), lax.axis_index("c")
        me, sibling = (ax, ay, ac), (ax, ay, 1 - ac)
        chips = [(ax, ay), (1 - ax, ay), (ax, 1 - ay), (1 - ax, 1 - ay)]
        outs = (za_ref, bx_ref, bg_ref)

        def source(t):
            arr, r0, rows = parts[t]
            return ins[arr].at[pl.ds(r0, rows)]

        def slot(t, px, py, pc):
            arr, r0, rows = parts[t]
            dev = 4 * px + 2 * py + pc
            if arr == 0:
                return gathered[0].at[pl.ds(r0, rows), pl.ds(pl.multiple_of(dev * shard_cols, LANES), shard_cols)]
            return gathered[arr].at[pl.ds(pl.multiple_of(dev * rows, 16), rows)]

        def copy(k, t, block, to, from_input=False):
            return pltpu.make_async_remote_copy(
                src_ref=source(t) if from_input else slot(t, *block), dst_ref=slot(t, *block),
                send_sem=send_sems.at[k, t], recv_sem=recv_sems.at[k, t], device_id=to, device_id_type=MESH)

        def to_sibling(t):
            return copy(0, t, me, sibling, from_input=True)

        def from_sibling(t):
            return copy(0, t, sibling, me)

        def send(j, t):
            return copy(j, t, me, (*chips[j], ac), from_input=True)

        def landed(j, t):
            return copy(j, t, (*chips[j], ac), me)

        def forward(j, t):
            return copy(3 + j, t, (*chips[j], ac), sibling)

        def forwarded(j, t):
            return copy(3 + j, t, (*chips[j], 1 - ac), me)

        def mine(t):
            return pltpu.make_async_copy(source(t), slot(t, *me), local_sems.at[t])

        def load(chip, q):
            px, py = chip
            cols = pl.ds(pl.multiple_of((2 * px + py) * CHIP_COLS, LANES), CHIP_COLS)
            return pltpu.make_async_copy(gathered[0].at[:, cols], w_buf.at[q % 2], load_sems.at[q % 2])

        def out_copies(q, tile, stage):
            cps = []
            for k, (o, c0, width, z0) in enumerate(IN_PROJ_PIECES[q]):
                src = z32.at[stage, :, pl.ds(0, width)] if o == 1 else z16.at[stage, :, pl.ds(z0, width)]
                dst = outs[o].at[pl.ds(pl.multiple_of(tile * tm, tm), tm), pl.ds(c0, width)]
                cps.append(pltpu.make_async_copy(src, dst, out_sems.at[stage, k]))
            return cps

        @pl.when((p == 0) & (i == 0))
        def _():
            for t in w_parts:
                mine(t).start()
                to_sibling(t).start()
                for j in range(1, N_CHIPS):
                    send(j, t).start()
            for t in later_parts:
                mine(t).start()
                to_sibling(t).start()
            for t in w_parts:
                from_sibling(t).wait_recv()
                mine(t).wait()
            load(chips[0], 0).start()

        for j in range(1, N_CHIPS):
            @pl.when((p == j - 1) & (i == n - 2))
            def _(j=j):
                for t in w_parts:
                    landed(j, t).wait_recv()
                    forward(j, t).start()
                if j == 2:
                    for t in later_parts:
                        for jj in range(1, N_CHIPS):
                            send(jj, t).start()

            @pl.when((p == j - 1) & (i == n - 1))
            def _(j=j):
                for t in w_parts:
                    forwarded(j, t).wait_recv()
                load(chips[j], j).start()

        @pl.when((p == N_CHIPS - 1) & (i == n - 4))
        def _():
            for jj in range(1, N_CHIPS):
                for t in later_parts:
                    landed(jj, t).wait_recv()
                    forward(jj, t).start()

        @pl.when(i == 0)
        def _():
            load(chips[0], p).wait()

        @pl.when(p == 0)
        def _():
            xf = x_ref[...]
            r = lax.rsqrt(jnp.mean(xf * xf, axis=1, keepdims=True) + EPS)
            h = (xf * r * g_ref[...]).astype(BF16)
            ht_ref[...] = h.T
            h_all[pl.ds(pl.multiple_of(i * tm, tm), tm), :] = h

        def chip_of_pass(pp):
            return (2 * ax + ay) ^ ((pp >> 1) | ((pp & 1) << 1))

        step = p * n + i
        stage = step % 2
        for q in range(N_CHIPS):
            @pl.when((step >= 2) & (chip_of_pass((step - 2) // n) == q))
            def _(q=q):
                for cp in out_copies(q, (step - 2) % n, stage):
                    cp.wait()

        for q in range(N_CHIPS):
            @pl.when(chip_of_pass(p) == q)
            def _(q=q):
                f32_from = [(z0, width) for o, _, width, z0 in IN_PROJ_PIECES[q] if o == 1]
                for r0 in range(0, tm, IN_PROJ_ROWS):
                    rows = pl.ds(r0, IN_PROJ_ROWS)
                    h = h_all[pl.ds(pl.multiple_of(i * tm + r0, IN_PROJ_ROWS), IN_PROJ_ROWS), :]
                    for c0 in range(0, CHIP_COLS, IN_PROJ_CHUNK):
                        z = _mm(h, w_buf[p % 2, :, pl.ds(c0, IN_PROJ_CHUNK)])
                        z16[stage, rows, pl.ds(c0, IN_PROJ_CHUNK)] = z.astype(BF16)
                        for z0, width in f32_from:
                            if z0 <= c0 < z0 + width:
                                z32[stage, rows, pl.ds(c0 - z0, IN_PROJ_CHUNK)] = z
                for cp in out_copies(q, i, stage):
                    cp.start()

        last = (p == N_CHIPS - 1) & (i == n - 1)
        for q in range(N_CHIPS):
            @pl.when(last & (chip_of_pass(p) == q))
            def _(q=q):
                for cp in out_copies(q, n - 2, 1 - stage) + out_copies(q, n - 1, stage):
                    cp.wait()

        @pl.when(last)
        def _():
            for t in later_parts:
                from_sibling(t).wait_recv()
                for jj in range(1, N_CHIPS):
                    forwarded(jj, t).wait_recv()
                mine(t).wait()
            for t in range(len(parts)):
                to_sibling(t).wait_send()
                for jj in range(1, N_CHIPS):
                    send(jj, t).wait_send()
                    forward(jj, t).wait_send()

    any_spec = pl.BlockSpec(memory_space=pl.ANY)
    first_pass_tile = lambda p, i: jnp.where(p == 0, i, n - 1)
    return pl.pallas_call(
        body, grid=(N_CHIPS, n), name="l0_in_proj",
        out_shape=[jax.ShapeDtypeStruct((seq, 3 * D), BF16), jax.ShapeDtypeStruct((seq, D), F32),
                   jax.ShapeDtypeStruct((seq, D), BF16), jax.ShapeDtypeStruct((D, seq), BF16),
                   jax.ShapeDtypeStruct((D, MIX0_IN), BF16)]
        + [jax.ShapeDtypeStruct((N_DEV * t.shape[0], t.shape[1]), t.dtype) for t in later_shards],
        in_specs=[pl.BlockSpec((tm, D), lambda p, i: (first_pass_tile(p, i), 0)), _resident((1, D))] + [any_spec] * n_arr,
        out_specs=[any_spec, any_spec, any_spec, pl.BlockSpec((D, tm), lambda p, i: (0, first_pass_tile(p, i)))]
        + [any_spec] * n_arr,
        scratch_shapes=[pltpu.VMEM((seq, D), BF16), pltpu.VMEM((2, D, CHIP_COLS), BF16),
                        pltpu.VMEM((2, tm, f32_cols), F32), pltpu.VMEM((2, tm, CHIP_COLS), BF16),
                        pltpu.SemaphoreType.DMA((7, len(parts))), pltpu.SemaphoreType.DMA((7, len(parts))),
                        pltpu.SemaphoreType.DMA((len(parts),)), pltpu.SemaphoreType.DMA((2,)),
                        pltpu.SemaphoreType.DMA((2, 2))],
        compiler_params=_params(56, 2),
    )(x, g0, w_shard, *later_shards)


POOL_EXT = 40


def _fill_halo(ext_ref, cur, prev_ref, next_ref, i, n_tiles, ts):
    ext_ref[pl.ds(0, POOL_HALO), :] = jnp.where(i > 0, prev_ref[...], 0.0)
    ext_ref[pl.ds(POOL_HALO, ts), :] = cur
    ext_ref[pl.ds(POOL_HALO + ts, POOL_HALO), :] = jnp.where(i < n_tiles - 1, next_ref[...], 0.0)
    ext_ref[pl.ds(2 * POOL_HALO + ts, POOL_EXT - 2 * POOL_HALO), :] = jnp.zeros((POOL_EXT - 2 * POOL_HALO, D), F32)


def _window_sums(src_ref, tmp_refs, ts, cols, w, shift):
    if w == 2:
        return src_ref[pl.ds(POOL_HALO - 1 + shift, ts), cols] + src_ref[pl.ds(POOL_HALO + shift, ts), cols]
    d2, d4, d8 = tmp_refs
    n2, n4, n8 = ts + 32, ts + 24, ts + 16
    d2[pl.ds(0, n2), :] = src_ref[pl.ds(0, n2), cols] + src_ref[pl.ds(1, n2), cols]
    if w == 4:
        return d2[pl.ds(POOL_HALO - 2 + shift, ts), :] + d2[pl.ds(POOL_HALO + shift, ts), :]
    d4[pl.ds(0, n4), :] = d2[pl.ds(0, n4), :] + d2[pl.ds(2, n4), :]
    if w == 8:
        return d4[pl.ds(POOL_HALO - 4 + shift, ts), :] + d4[pl.ds(POOL_HALO + shift, ts), :]
    d8[pl.ds(0, n8), :] = d4[pl.ds(0, n8), :] + d4[pl.ds(4, n8), :]
    return d8[pl.ds(shift, ts), :] + d8[pl.ds(POOL_HALO + shift, ts), :]


def _pool_scratch(ts):
    return [pltpu.VMEM((ts + POOL_EXT, D), F32)] + [pltpu.VMEM((ts + POOL_EXT, GDIM), F32)] * 3


def _pool_forward(xe_ref, tmp_refs, ts, t0, seq):
    tg = t0 + lax.broadcasted_iota(jnp.int32, (ts, 1), 0)
    outs = []
    for gi, w in enumerate(POOL_WINDOWS):
        hw = w // 2
        cols = slice(gi * GDIM, (gi + 1) * GDIM)
        cnt = (jnp.minimum(tg + hw, seq) - jnp.maximum(tg - hw, 0)).astype(F32)
        outs.append(_window_sums(xe_ref, tmp_refs, ts, cols, w, 0) / cnt - xe_ref[pl.ds(POOL_HALO, ts), cols])
    return jnp.concatenate(outs, axis=1)


def _spatial_mix(ws_ref, vnb, bias, ts):
    rows = []
    for c in range(ts // CHUNK):
        vc = vnb[c * CHUNK:(c + 1) * CHUNK, :]
        rows.append(jnp.concatenate(
            [_mm(ws_ref[h], vc[:, h * GDIM:(h + 1) * GDIM]) for h in range(A_GROUPS)], axis=1) + bias)
    return jnp.concatenate(rows, axis=0)


def _halo_specs(ts, seq, width):
    per = ts // POOL_HALO
    last = seq // POOL_HALO - 1
    prev = pl.BlockSpec((POOL_HALO, width), lambda i: (jnp.maximum(i * per - 1, 0), 0))
    nxt = pl.BlockSpec((POOL_HALO, width), lambda i: (jnp.minimum((i + 1) * per, last), 0))
    return prev, nxt


def _l0_mix_fwd(za, bx, bg, x, ws, bias, gv, wg, scale, wout):
    seq = x.shape[0]
    ts = 512
    n_tiles = seq // ts

    def body(za_ref, bx_ref, bxp_ref, bxn_ref, bg_ref, x_ref, ws_ref, bias_ref, gv_ref, wg_ref, sc_ref, wo_ref,
             x1_ref, xe_ref, *tmp_refs):
        i = pl.program_id(0)
        vg = _gelu(za_ref[:, D:2 * D].astype(F32))
        rv = lax.rsqrt(jnp.mean(vg * vg, axis=1, keepdims=True) + EPS)
        vnb = (vg * rv * gv_ref[...]).astype(BF16)
        mixed = _spatial_mix(ws_ref, vnb, bias_ref[...], ts)

        _fill_halo(xe_ref, bx_ref[...], bxp_ref, bxn_ref, i, n_tiles, ts)
        pb = _pool_forward(xe_ref, tmp_refs, ts, i * ts, seq).astype(BF16)
        ypre = jnp.concatenate([_mm(pb[:, g * GDIM:(g + 1) * GDIM], wg_ref[g]) for g in range(4)], axis=1)

        u = _gelu(za_ref[:, 0:D].astype(F32))
        ag = za_ref[:, 2 * D:3 * D].astype(F32)
        ya = (u * mixed * (ag * jax.nn.sigmoid(ag))).astype(BF16)
        out_a = _mm(ya, wo_ref[0:D, :])

        bgf = bg_ref[...].astype(F32)
        yb = (ypre * sc_ref[...] * (bgf * jax.nn.sigmoid(bgf))).astype(BF16)
        x1_ref[...] = x_ref[...] + out_a + _mm(yb, wo_ref[D:2 * D, :])

    prev, nxt = _halo_specs(ts, seq, D)
    row = lambda w: pl.BlockSpec((ts, w), lambda i: (i, 0))
    return pl.pallas_call(
        body, grid=(n_tiles,), name="l0_mix_fwd",
        out_shape=jax.ShapeDtypeStruct((seq, D), F32),
        in_specs=[row(3 * D), row(D), prev, nxt, row(D), row(D), _resident((4, CHUNK, CHUNK)), _resident((CHUNK, D)),
                  _resident((1, D)), _resident((4, GDIM, GDIM)), _resident((1, D)), _resident((2 * D, D))],
        out_specs=row(D),
        scratch_shapes=_pool_scratch(ts),
        compiler_params=_params(56),
    )(za, bx, bx, bx, bg, x, ws, bias, gv, wg, scale, wout)


def _l1_in_proj(x1, g1, w_t, cos_t, sin_t):
    seq = x1.shape[0]
    tm = 512

    def body(x_ref, g_ref, wt_ref, c_ref, s_ref, q_ref, k_ref, v_ref, gate_ref, ht_ref):
        xf = x_ref[...]
        r = lax.rsqrt(jnp.mean(xf * xf, axis=1, keepdims=True) + EPS)
        ht = (xf * r * g_ref[...]).astype(BF16).T
        ht_ref[...] = ht
        c, s = c_ref[...], s_ref[...]
        q_ref[...] = (_rope_t(_mm(wt_ref[Q_ROWS[0]:Q_ROWS[1], :], ht), c, s, N_HEADS, 1) * SCALE).astype(BF16)
        k_ref[...] = _rope_t(_mm(wt_ref[K_ROWS[0]:K_ROWS[1], :], ht), c, s, N_KV, 1).astype(BF16)
        v_ref[...] = _mm(wt_ref[V_ROWS[0]:V_ROWS[1], :], ht).astype(BF16)
        gate_ref[...] = _mm(wt_ref[G_ROWS[0]:G_ROWS[1], :], ht).astype(BF16)

    col = lambda rows: pl.BlockSpec((rows, tm), lambda i: (0, i))
    return pl.pallas_call(
        body, grid=(seq // tm,), name="l1_in_proj",
        out_shape=(jax.ShapeDtypeStruct((D, seq), BF16), jax.ShapeDtypeStruct((KV_W, seq), BF16),
                   jax.ShapeDtypeStruct((KV_W, seq), BF16), jax.ShapeDtypeStruct((D, seq), BF16),
                   jax.ShapeDtypeStruct((D, seq), BF16)),
        in_specs=[pl.BlockSpec((tm, D), lambda i: (i, 0)), _resident((1, D)), _resident((MIX1_IN, D)), col(ROT_HALF),
                  col(ROT_HALF)],
        out_specs=(col(D), col(KV_W), col(KV_W), col(D), col(D)),
        compiler_params=_params(48),
    )(x1, g1, w_t, cos_t, sin_t)


def _band_specs_t(nb, clamp_i):
    per = TQ // BLK
    prev = pl.BlockSpec((KV_W, BLK), lambda i: (0, jnp.maximum(clamp_i(i) * per - 1, 0)))
    cur = pl.BlockSpec((KV_W, TQ), lambda i: (0, clamp_i(i)))
    nxt = pl.BlockSpec((KV_W, BLK), lambda i: (0, jnp.minimum((clamp_i(i) + 1) * per, nb - 1)))
    return [prev, cur, nxt]


def _fill_band(buf, p_ref, c_ref, n_ref):
    buf[:, 0:BLK] = p_ref[...]
    buf[:, BLK:BLK + TQ] = c_ref[...]
    buf[:, BLK + TQ:2 * BLK + TQ] = n_ref[...]


def _band_bias_t(n, nb):
    c = lax.broadcasted_iota(jnp.int32, (BLK, BLK), 0)
    r = lax.broadcasted_iota(jnp.int32, (BLK, BLK), 1)
    first = jnp.where((c >= r) & (n > 0), 0.0, NEG_INF).astype(F32)
    last = jnp.where((c <= r) & (n < nb - 1), 0.0, NEG_INF).astype(F32)
    return jnp.concatenate([first] * HPP, axis=1), jnp.concatenate([last] * HPP, axis=1)


def _masked(st, bias):
    first, last = bias
    return jnp.concatenate([st[0:BLK] + first, st[BLK:2 * BLK], st[2 * BLK:3 * BLK] + last], axis=0)


AUG = 16


def _ones_rows(n_ones, width):
    return (lax.broadcasted_iota(jnp.int32, (AUG, width), 0) < n_ones).astype(BF16)


def _minus_rows(vec):
    hi = vec.astype(BF16).astype(F32)
    lo = vec - hi
    return jnp.concatenate([-hi, -lo, jnp.zeros((AUG - 2, vec.shape[1]), F32)], axis=0).astype(BF16)


HPP = GQA
FWD_GROUP, BWD_GROUP = 2, 1
BWD_AHEAD = 1


def _heads_t(ref, h0, c0):
    return jnp.concatenate([ref[(h0 + g) * HD:(h0 + g + 1) * HD, c0:c0 + BLK] for g in range(HPP)], axis=1)


def _row4(ref, h0, c0):
    return jnp.concatenate([ref[h0 + g:h0 + g + 1, c0:c0 + BLK] for g in range(HPP)], axis=1)


def _sink_row(sink_ref, h0):
    return jnp.concatenate([jnp.full((1, BLK), sink_ref[h0 + g], F32) for g in range(HPP)], axis=1)


def _l1_attn_fwd(qt, kt, vt, gatet, x1, tgt, wout, gf, sink):
    seq = x1.shape[0]
    nq, nb = seq // TQ, seq // BLK

    def body(q_ref, gate_ref, kp_ref, k_ref, kn_ref, vp_ref, v_ref, vn_ref, x1_ref, tgt_ref, wo_ref, gf_ref, sink_ref,
             dx2_ref, dx2b_ref, att_ref, lse_ref, loss_ref, dgf_ref, dwo_ref, dwo_wire_ref, kbuf, vbuf, att_scr):
        i = pl.program_id(0)

        @pl.when(i == 0)
        def _():
            loss_ref[...] = jnp.zeros_like(loss_ref)
            dgf_ref[...] = jnp.zeros_like(dgf_ref)
            dwo_ref[...] = jnp.zeros_like(dwo_ref)

        _fill_band(kbuf, kp_ref, k_ref, kn_ref)
        _fill_band(vbuf, vp_ref, v_ref, vn_ref)
        ones_row = _ones_rows(1, 3 * BLK)
        groups = [list(range(0, N_HEADS, HPP))[g:g + FWD_GROUP] for g in range(0, N_HEADS // HPP, FWD_GROUP)]
        work = [(j, grp) for j in range(TQ // BLK) for grp in groups]

        def scores(j, passes):
            c0 = j * BLK
            bias = _band_bias_t(i * (TQ // BLK) + j, nb)
            st = dict(c0=c0, passes=passes)
            st["kv_rows"] = [slice(h0 // GQA * HD, (h0 // GQA + 1) * HD) for h0 in passes]
            st["sts"] = [_masked(_tn(kbuf[rows, c0:c0 + 3 * BLK], _heads_t(q_ref, h0, c0)), bias)
                         for h0, rows in zip(passes, st["kv_rows"])]
            return st

        def softmaxes(st):
            st["sks"] = [_sink_row(sink_ref, h0) for h0 in st["passes"]]
            st["ms"] = [jnp.maximum(jnp.max(s_, axis=0, keepdims=True), sk) for s_, sk in zip(st["sts"], st["sks"])]
            st["ps"] = [jnp.exp(s_ - m).astype(BF16) for s_, m in zip(st["sts"], st["ms"])]

        def values(st):
            c0, passes = st["c0"], st["passes"]
            pvs = [_mm(jnp.concatenate([vbuf[rows, c0:c0 + 3 * BLK], ones_row], axis=0), p)
                   for rows, p in zip(st["kv_rows"], st["ps"])]
            lse_rows = []
            for h0, pv, m, sk in zip(passes, pvs, st["ms"], st["sks"]):
                den = pv[HD:HD + 1, :] + jnp.exp(sk - m)
                ot = pv[0:HD, :] / den
                lse = m + jnp.log(den)
                for g in range(HPP):
                    h = h0 + g
                    att_scr[h * HD:(h + 1) * HD, c0:c0 + BLK] = ot[:, g * BLK:(g + 1) * BLK]
                    lse_rows.append(lse[:, g * BLK:(g + 1) * BLK])
            lse_ref[passes[0]:passes[0] + len(lse_rows), c0:c0 + BLK] = jnp.concatenate(lse_rows, axis=0)

        state = scores(*work[0])
        for nxt in work[1:] + [None]:
            following = scores(*nxt) if nxt is not None else None
            softmaxes(state)
            values(state)
            state = following

        att = att_scr[...]
        gate = gate_ref[...].astype(F32)
        yt = (att * (gate * jax.nn.sigmoid(gate))).astype(BF16)
        att_ref[...] = att.astype(BF16)
        x2 = x1_ref[...] + _mm(yt.T, wo_ref[...])
        r = lax.rsqrt(jnp.mean(x2 * x2, axis=1, keepdims=True) + EPS)
        xn = x2 * r
        diff = xn * gf_ref[...] - tgt_ref[...]
        loss_ref[...] += 0.5 * jnp.sum(jnp.mean(diff * diff, axis=1, keepdims=True), axis=0, keepdims=True)
        dout = diff * (1.0 / D)
        dgf_ref[...] += jnp.sum(dout * xn, axis=0, keepdims=True)
        dxn = dout * gf_ref[...]
        dx2 = r * (dxn - xn * jnp.mean(dxn * xn, axis=1, keepdims=True))
        dx2_ref[...] = dx2
        dx2b = dx2.astype(BF16)
        dx2b_ref[...] = dx2b
        dwo_ref[...] += _mm(yt, dx2b)

        @pl.when(i == nq - 1)
        def _():
            dwo_wire_ref[...] = dwo_ref[...].astype(BF16)

    ident = lambda i: i
    row = pl.BlockSpec((TQ, D), lambda i: (i, 0))
    col = lambda rows: pl.BlockSpec((rows, TQ), lambda i: (0, i))
    whole = pl.BlockSpec((D, D), lambda i: (0, 0))
    return pl.pallas_call(
        body, grid=(nq,), name="l1_attn_fwd",
        out_shape=(jax.ShapeDtypeStruct((seq, D), F32), jax.ShapeDtypeStruct((seq, D), BF16),
                   jax.ShapeDtypeStruct((D, seq), BF16),
                   jax.ShapeDtypeStruct((N_HEADS, seq), F32), jax.ShapeDtypeStruct((1, 1), F32),
                   jax.ShapeDtypeStruct((1, D), F32), jax.ShapeDtypeStruct((D, D), F32), jax.ShapeDtypeStruct((D, D), BF16)),
        in_specs=[col(D), col(D)] + _band_specs_t(nb, ident) + _band_specs_t(nb, ident) + [
            row, row, _resident((D, D)), _resident((1, D)), pl.BlockSpec(memory_space=pltpu.SMEM)],
        out_specs=(row, row, col(D), col(N_HEADS), pl.BlockSpec((1, 1), lambda i: (0, 0)),
                   pl.BlockSpec((1, D), lambda i: (0, 0)), whole, whole),
        scratch_shapes=[pltpu.VMEM((KV_W, TQ + 2 * BLK), BF16), pltpu.VMEM((KV_W, TQ + 2 * BLK), BF16),
                        pltpu.VMEM((D, TQ), F32)],
        compiler_params=_params(56),
    )(qt, gatet, kt, kt, kt, vt, vt, vt, x1, tgt, wout, gf, sink)


def _l1_attn_bwd(dx2b, wout, qt, kt, vt, gatet, att, lse, sink):
    seq = dx2b.shape[0]
    nq, nb = seq // TQ, seq // BLK

    def body(dx_ref, wo_ref, q_ref, gate_ref, kp_ref, k_ref, kn_ref, vp_ref, v_ref, vn_ref, att_ref, lse_ref, sink_ref,
             dq_ref, dgate_ref, dk_ref, dv_ref, dsink_ref, kbuf, vbuf, dkacc, dvacc, dat_scr, delta_scr, dsacc):
        i = pl.program_id(0)

        @pl.when(i == 0)
        def _():
            dkacc[...] = jnp.zeros_like(dkacc)
            dvacc[...] = jnp.zeros_like(dvacc)
            dsacc[...] = jnp.zeros_like(dsacc)

        @pl.when(i > 0)
        def _():
            for acc in (dkacc, dvacc):
                acc[:, 0:2 * BLK] = acc[:, TQ:TQ + 2 * BLK]
                acc[:, 2 * BLK:2 * BLK + TQ] = jnp.zeros((KV_W, TQ), F32)

        @pl.when(i < nq)
        def _():
            _fill_band(kbuf, kp_ref, k_ref, kn_ref)
            _fill_band(vbuf, vp_ref, v_ref, vn_ref)
            dyt = _nt(wo_ref[...], dx_ref[...])
            sg, dsg = _silu_and_grad(gate_ref[...].astype(F32))
            attf = att_ref[...].astype(F32)
            dat = dyt * sg
            dat_scr[...] = dat.astype(BF16)
            dgate_ref[...] = (dyt * attf * dsg).astype(BF16)
            dl = dat * attf
            delta_scr[...] = jnp.concatenate(
                [jnp.sum(dl[h * HD:(h + 1) * HD, :], axis=0, keepdims=True) for h in range(N_HEADS)], axis=0)
            ones_rows = _ones_rows(2, 3 * BLK)
            groups = [list(range(0, N_HEADS, HPP))[g:g + BWD_GROUP] for g in range(0, N_HEADS // HPP, BWD_GROUP)]
            work = [(j, grp) for j in range(TQ // BLK) for grp in groups]

            def scores(j, passes):
                c0 = j * BLK
                st = dict(c0=c0, passes=passes, bias=_band_bias_t(i * (TQ // BLK) + j, nb))
                st["kv_rows"] = [slice(h0 // GQA * HD, (h0 // GQA + 1) * HD) for h0 in passes]
                st["q4s"] = [_heads_t(q_ref, h0, c0) for h0 in passes]
                st["do4s"] = [_heads_t(dat_scr, h0, c0) for h0 in passes]
                st["lse4s"] = [_row4(lse_ref, h0, c0) for h0 in passes]
                st["delta4s"] = [_row4(delta_scr, h0, c0) for h0 in passes]
                st["kths"] = [kbuf[rows, c0:c0 + 3 * BLK] for rows in st["kv_rows"]]
                st["sts"] = [_tn(jnp.concatenate([kth, ones_rows], axis=0),
                                 jnp.concatenate([q4, _minus_rows(lse4)], axis=0))
                             for kth, q4, lse4 in zip(st["kths"], st["q4s"], st["lse4s"])]
                st["dpds"] = [_tn(jnp.concatenate([vbuf[rows, c0:c0 + 3 * BLK], ones_rows], axis=0),
                                  jnp.concatenate([do4, _minus_rows(delta4)], axis=0))
                              for rows, do4, delta4 in zip(st["kv_rows"], st["do4s"], st["delta4s"])]
                return st

            def elementwise(st):
                st["ps"] = [jnp.exp(_masked(s_, st["bias"])) for s_ in st["sts"]]
                st["dss"] = [(p * dpd).astype(BF16) for p, dpd in zip(st["ps"], st["dpds"])]

            def gradients(st):
                c0 = st["c0"]
                dq4s = [_mm(kth, ds) * SCALE for kth, ds in zip(st["kths"], st["dss"])]
                dks = [_nt(q4, ds) for q4, ds in zip(st["q4s"], st["dss"])]
                dvs = [_nt(do4, p.astype(BF16)) for do4, p in zip(st["do4s"], st["ps"])]
                for h0, rows, dq4, dk, dv, lse4, delta4 in zip(st["passes"], st["kv_rows"], dq4s, dks, dvs, st["lse4s"],
                                                               st["delta4s"]):
                    dkacc[rows, c0:c0 + 3 * BLK] += dk
                    dvacc[rows, c0:c0 + 3 * BLK] += dv
                    dsk = -jnp.exp(_sink_row(sink_ref, h0) - lse4) * delta4
                    for g in range(HPP):
                        h = h0 + g
                        dq_ref[h * HD:(h + 1) * HD, c0:c0 + BLK] = dq4[:, g * BLK:(g + 1) * BLK].astype(BF16)
                        dsacc[h:h + 1, :] += dsk[:, g * BLK:(g + 1) * BLK]

            ahead = [scores(*w) for w in work[:BWD_AHEAD]]
            for n in range(len(work)):
                if n + BWD_AHEAD < len(work):
                    ahead.append(scores(*work[n + BWD_AHEAD]))
                state = ahead.pop(0)
                elementwise(state)
                gradients(state)

        dk_ref[...] = dkacc[:, 0:TQ].astype(BF16)
        dv_ref[...] = dvacc[:, 0:TQ].astype(BF16)

        @pl.when(i == nq)
        def _():
            dsink_ref[...] = jnp.broadcast_to(jnp.sum(dsacc[...], axis=1, keepdims=True), (N_HEADS, LANES))

    clamp = lambda i: jnp.minimum(i, nq - 1)
    row = pl.BlockSpec((TQ, D), lambda i: (clamp(i), 0))
    col = lambda rows: pl.BlockSpec((rows, TQ), lambda i: (0, clamp(i)))
    pad = pl.BlockSpec((KV_W, TQ), lambda i: (0, i))
    return pl.pallas_call(
        body, grid=(nq + 1,), name="l1_attn_bwd",
        out_shape=(jax.ShapeDtypeStruct((D, seq), BF16), jax.ShapeDtypeStruct((D, seq), BF16),
                   jax.ShapeDtypeStruct((KV_W, seq + TQ), BF16), jax.ShapeDtypeStruct((KV_W, seq + TQ), BF16),
                   jax.ShapeDtypeStruct((N_HEADS, LANES), F32)),
        in_specs=[row, _resident((D, D)), col(D), col(D)] + _band_specs_t(nb, clamp) + _band_specs_t(nb, clamp) + [
            col(D), col(N_HEADS), pl.BlockSpec(memory_space=pltpu.SMEM)],
        out_specs=(col(D), col(D), pad, pad, pl.BlockSpec((N_HEADS, LANES), lambda i: (0, 0))),
        scratch_shapes=[pltpu.VMEM((KV_W, TQ + 2 * BLK), BF16), pltpu.VMEM((KV_W, TQ + 2 * BLK), BF16),
                        pltpu.VMEM((KV_W, TQ + 2 * BLK), F32), pltpu.VMEM((KV_W, TQ + 2 * BLK), F32),
                        pltpu.VMEM((D, TQ), BF16), pltpu.VMEM((N_HEADS, TQ), F32), pltpu.VMEM((N_HEADS, LANES), F32)],
        compiler_params=_params(56),
    )(dx2b, wout, qt, gatet, kt, kt, kt, vt, vt, vt, att, lse, sink)


def _l1_in_proj_bwd(dq_r, dk_r, dv, dgate, cos_t, sin_t, w_t, x1, g1, dx2):
    seq = x1.shape[0]
    tm = 512

    def body(dq_ref, dk_ref, dv_ref, dg_ref, c_ref, s_ref, w_ref, x_ref, g_ref, dres_ref,
             dx_ref, dxb_ref, dz_ref, dn_ref):
        @pl.when(pl.program_id(0) == 0)
        def _():
            dn_ref[...] = jnp.zeros_like(dn_ref)

        c, s = c_ref[...], s_ref[...]
        dq = _rope_t(dq_ref[...].astype(F32), c, s, N_HEADS, -1).astype(BF16)
        dk = _rope_t(dk_ref[...].astype(F32), c, s, N_KV, -1).astype(BF16)
        dz = jnp.concatenate([dq, dk, dv_ref[...], dg_ref[...]], axis=0)
        dz_ref[...] = dz
        dh = _tn(dz, w_ref[...])
        xf = x_ref[...]
        r = lax.rsqrt(jnp.mean(xf * xf, axis=1, keepdims=True) + EPS)
        xn = xf * r
        dn_ref[...] += jnp.sum(dh * xn, axis=0, keepdims=True)
        dxn = dh * g_ref[...]
        dx = dres_ref[...] + r * (dxn - xn * jnp.mean(dxn * xn, axis=1, keepdims=True))
        dx_ref[...] = dx
        dxb_ref[...] = dx.astype(BF16)

    row = pl.BlockSpec((tm, D), lambda i: (i, 0))
    col = lambda rows: pl.BlockSpec((rows, tm), lambda i: (0, i))
    return pl.pallas_call(
        body, grid=(seq // tm,), name="l1_in_proj_bwd",
        out_shape=(jax.ShapeDtypeStruct((seq, D), F32), jax.ShapeDtypeStruct((seq, D), BF16),
                   jax.ShapeDtypeStruct((MIX1_IN, seq), BF16), jax.ShapeDtypeStruct((1, D), F32)),
        in_specs=[col(D), col(KV_W), col(KV_W), col(D), col(ROT_HALF), col(ROT_HALF), _resident((MIX1_IN, D)), row,
                  _resident((1, D)), row],
        out_specs=(row, row, col(MIX1_IN), pl.BlockSpec((1, D), lambda i: (0, 0))),
        compiler_params=_params(48),
    )(dq_r, dk_r, dv, dgate, cos_t, sin_t, w_t, x1, g1, dx2)


def _l0_mix_bwd(dx1b, wout, za, bx, bg, ws, ws_t, bias, gv, wg, wg_t, scale):
    seq = dx1b.shape[0]
    ts = 256
    n_tiles = seq // ts

    def body(dx_ref, wo_ref, za_ref, bx_ref, bxp_ref, bxn_ref, bg_ref, ws_ref, wst_ref, bias_ref, gv_ref, wg_ref,
             wgt_ref, sc_ref,
             dz_ref, dp_ref, catt_ref, dws_ref, dbias_ref, dgv_ref, dsc_ref, dwg_ref, db_ref, xe_ref, *tmp_refs):
        i = pl.program_id(0)

        @pl.when(i == 0)
        def _():
            for r_ in (dws_ref, dbias_ref, dgv_ref, dsc_ref, dwg_ref, db_ref):
                r_[...] = jnp.zeros_like(r_)

        dxb = dx_ref[...]
        dya = _nt(dxb, wo_ref[0:D, :])
        dyb = _nt(dxb, wo_ref[D:2 * D, :])

        vg, dvg_dz = _gelu_and_grad(za_ref[:, D:2 * D].astype(F32))
        rv = lax.rsqrt(jnp.mean(vg * vg, axis=1, keepdims=True) + EPS)
        vnorm = vg * rv
        gvw = gv_ref[...]
        vnb = (vnorm * gvw).astype(BF16)
        mixed = _spatial_mix(ws_ref, vnb, bias_ref[...], ts)

        _fill_halo(xe_ref, bx_ref[...], bxp_ref, bxn_ref, i, n_tiles, ts)
        pb = _pool_forward(xe_ref, tmp_refs, ts, i * ts, seq).astype(BF16)
        ypre = jnp.concatenate([_mm(pb[:, g * GDIM:(g + 1) * GDIM], wg_ref[g]) for g in range(4)], axis=1)

        u, du = _gelu_and_grad(za_ref[:, 0:D].astype(F32))
        sga, dsga = _silu_and_grad(za_ref[:, 2 * D:3 * D].astype(F32))
        um = u * mixed
        ya = (um * sga).astype(BF16)
        t = dya * sga
        dz_ref[:, 0:D] = (t * mixed * du).astype(BF16)
        dz_ref[:, 2 * D:3 * D] = (dya * um * dsga).astype(BF16)
        dmixed = t * u
        dmb = dmixed.astype(BF16)
        dvn_rows = []
        for c in range(ts // CHUNK):
            rows = slice(c * CHUNK, (c + 1) * CHUNK)
            parts = []
            for h in range(A_GROUPS):
                cols = slice(h * GDIM, (h + 1) * GDIM)
                dws_ref[h] += _nt(dmb[rows, cols], vnb[rows, cols])
                parts.append(_mm(wst_ref[h], dmb[rows, cols]))
            dvn_rows.append(jnp.concatenate(parts, axis=1))

        sc = sc_ref[...]
        y = ypre * sc
        sgb, dsgb = _silu_and_grad(bg_ref[...].astype(F32))
        yb = (y * sgb).astype(BF16)
        dy_b = dyb * sgb
        dz_ref[:, 3 * D:4 * D] = jnp.zeros((ts, D), BF16)
        dz_ref[:, 4 * D:5 * D] = (dyb * y * dsgb).astype(BF16)
        dsc_ref[...] += jnp.sum(dy_b * ypre, axis=0, keepdims=True)
        dypre = (dy_b * sc).astype(BF16)
        dps = []
        for g in range(4):
            cols = slice(g * GDIM, (g + 1) * GDIM)
            dwg_ref[g] += _tn(pb[:, cols], dypre[:, cols])
            dps.append(_mm(dypre[:, cols], wgt_ref[g]))

        dbias = dmixed[0:CHUNK, :]
        for c in range(1, ts // CHUNK):
            dbias = dbias + dmixed[c * CHUNK:(c + 1) * CHUNK, :]
        dbias_ref[...] += dbias
        dvn = jnp.concatenate(dvn_rows, axis=0)
        dgv_ref[...] += jnp.sum(dvn * vnorm, axis=0, keepdims=True)
        dxn = dvn * gvw
        dvg = rv * (dxn - vnorm * jnp.mean(dxn * vnorm, axis=1, keepdims=True))
        dz_ref[:, D:2 * D] = (dvg * dvg_dz).astype(BF16)

        dp_ref[...] = jnp.concatenate(dps, axis=1)
        catt_ref[...] = jnp.concatenate([ya, yb], axis=1).T

        @pl.when(i == n_tiles - 1)
        def _():
            for h in range(A_GROUPS):
                tot = jnp.sum(dbias_ref[:, h * GDIM:(h + 1) * GDIM].T, axis=0, keepdims=True)
                db_ref[pl.ds(h * 8, 8), :] = jnp.broadcast_to(tot, (8, CHUNK))

    prev, nxt = _halo_specs(ts, seq, D)
    row = lambda w_: pl.BlockSpec((ts, w_), lambda i: (i, 0))
    acc = lambda shape: pl.BlockSpec(shape, lambda i: (0,) * len(shape))
    return pl.pallas_call(
        body, grid=(n_tiles,), name="l0_mix_bwd",
        out_shape=(jax.ShapeDtypeStruct((seq, MIX0_IN), BF16), jax.ShapeDtypeStruct((seq, D), F32),
                   jax.ShapeDtypeStruct((2 * D, seq), BF16),
                   jax.ShapeDtypeStruct((4, CHUNK, CHUNK), F32), jax.ShapeDtypeStruct((CHUNK, D), F32),
                   jax.ShapeDtypeStruct((1, D), F32), jax.ShapeDtypeStruct((1, D), F32),
                   jax.ShapeDtypeStruct((4, GDIM, GDIM), F32), jax.ShapeDtypeStruct((32, CHUNK), F32)),
        in_specs=[row(D), _resident((2 * D, D)), row(3 * D), row(D), prev, nxt, row(D), _resident((4, CHUNK, CHUNK)),
                  _resident((4, CHUNK, CHUNK)), _resident((CHUNK, D)), _resident((1, D)), _resident((4, GDIM, GDIM)),
                  _resident((4, GDIM, GDIM)), _resident((1, D))],
        out_specs=(row(MIX0_IN), row(D), pl.BlockSpec((2 * D, ts), lambda i: (0, i)),
                   acc((4, CHUNK, CHUNK)), acc((CHUNK, D)), acc((1, D)), acc((1, D)), acc((4, GDIM, GDIM)),
                   acc((32, CHUNK))),
        scratch_shapes=_pool_scratch(ts),
        compiler_params=_params(56),
    )(dx1b, wout, za, bx, bx, bx, bg, ws, ws_t, bias, gv, wg, wg_t, scale)


def _l0_pool_bwd(dp, dz):
    seq = dp.shape[0]
    ts = 512
    n_tiles = seq // ts
    ext = ts + 2 * POOL_HALO

    def body(dp_ref, dpp_ref, dpn_ref, dz_ref, out_ref, qe_ref, *tmp_refs):
        i = pl.program_id(0)
        _fill_halo(qe_ref, dp_ref[...], dpp_ref, dpn_ref, i, n_tiles, ts)
        te = i * ts - POOL_HALO + lax.broadcasted_iota(jnp.int32, (ext, 1), 0)
        for gi, w in enumerate(POOL_WINDOWS):
            hw = w // 2
            cols = slice(gi * GDIM, (gi + 1) * GDIM)
            cnt = jnp.maximum(jnp.minimum(te + hw, seq) - jnp.maximum(te - hw, 0), 1).astype(F32)
            qe_ref[pl.ds(0, ext), cols] = qe_ref[pl.ds(0, ext), cols] / cnt
        outs = []
        for gi, w in enumerate(POOL_WINDOWS):
            cols = slice(gi * GDIM, (gi + 1) * GDIM)
            outs.append(_window_sums(qe_ref, tmp_refs, ts, cols, w, 1) - dp_ref[:, cols])
        out_ref[...] = jnp.concatenate(outs, axis=1).astype(BF16)

    prev, nxt = _halo_specs(ts, seq, D)
    row = pl.BlockSpec((ts, D), lambda i: (i, 0))
    return pl.pallas_call(
        body, grid=(n_tiles,), name="l0_pool_bwd",
        out_shape=jax.ShapeDtypeStruct(dz.shape, BF16),
        in_specs=[row, prev, nxt, pl.BlockSpec(memory_space=pl.ANY)],
        out_specs=pl.BlockSpec((ts, D), lambda i: (i, 3)),
        input_output_aliases={3: 0},
        scratch_shapes=_pool_scratch(ts),
        compiler_params=_params(32),
    )(dp, dp, dp, dz)


def _l0_in_proj_bwd(dz, w, x, g0, dx1):
    seq = x.shape[0]
    tm = 512

    def body(dz_ref, w_ref, x_ref, g_ref, dres_ref, dx_ref, dn_ref):
        @pl.when(pl.program_id(0) == 0)
        def _():
            dn_ref[...] = jnp.zeros_like(dn_ref)

        dh = _nt(dz_ref[...], w_ref[...])
        xf = x_ref[...]
        r = lax.rsqrt(jnp.mean(xf * xf, axis=1, keepdims=True) + EPS)
        xn = xf * r
        dn_ref[...] += jnp.sum(dh * xn, axis=0, keepdims=True)
        dxn = dh * g_ref[...]
        dx_ref[...] = dres_ref[...] + r * (dxn - xn * jnp.mean(dxn * xn, axis=1, keepdims=True))

    row = lambda w_: pl.BlockSpec((tm, w_), lambda i: (i, 0))
    return pl.pallas_call(
        body, grid=(seq // tm,), name="l0_in_proj_bwd",
        out_shape=(jax.ShapeDtypeStruct((seq, D), F32), jax.ShapeDtypeStruct((1, D), F32)),
        in_specs=[row(MIX0_IN), _resident((D, MIX0_IN)), row(D), _resident((1, D)), row(D)],
        out_specs=(row(D), pl.BlockSpec((1, D), lambda i: (0, 0))),
        compiler_params=_params(56),
    )(dz, w, x, g0, dx1)


def _dw_matmul(a_t, b, name, b_transposed=False, tn=1024, ts=1024, col_block=None):
    k, seq = a_t.shape
    n = b.shape[0] if b_transposed else b.shape[1]
    tn = min(n, tn)
    assert seq % ts == 0 and n % tn == 0 and (col_block is None or tn % col_block == 0)
    n_s = seq // ts
    per = 1 if col_block is None else tn // col_block

    def body(a_ref, b_ref, o_ref, ob_ref, acc_ref):
        s = pl.program_id(1)

        @pl.when(s == 0)
        def _():
            acc_ref[...] = jnp.zeros_like(acc_ref)

        acc_ref[...] += _nt(a_ref[...], b_ref[...]) if b_transposed else _mm(a_ref[...], b_ref[...])

        @pl.when(s == n_s - 1)
        def _():
            acc = acc_ref[...]
            if col_block is None:
                o_ref[...] = acc
                ob_ref[...] = acc.astype(BF16)
            else:
                for i in range(per):
                    piece = acc[:, i * col_block:(i + 1) * col_block]
                    o_ref[i] = piece
                    ob_ref[i] = piece.astype(BF16)

    b_spec = (pl.BlockSpec((tn, ts), lambda j, s: (j, s)) if b_transposed else pl.BlockSpec((ts, tn), lambda j, s: (s, j)))
    if col_block is None:
        shape, o_spec = (k, n), pl.BlockSpec((k, tn), lambda j, s: (0, j))
    else:
        shape, o_spec = (n // col_block, k, col_block), pl.BlockSpec((per, k, col_block), lambda j, s: (j, 0, 0))
    return pl.pallas_call(
        body, grid=(n // tn, n_s), name=name,
        out_shape=(jax.ShapeDtypeStruct(shape, F32), jax.ShapeDtypeStruct(shape, BF16)),
        in_specs=[pl.BlockSpec((k, ts), lambda j, s: (0, s)), b_spec],
        out_specs=(o_spec, o_spec),
        scratch_shapes=[pltpu.VMEM((k, tn), F32)],
        compiler_params=_params(56, 2),
    )(a_t, b)


ROW_TILES = 8


def _cast_shards(shards):
    n = len(shards)

    def body(*refs):
        for a in range(n):
            refs[n + a][...] = refs[a][...].astype(BF16)

    vm = pl.BlockSpec(memory_space=pltpu.VMEM)
    return pl.pallas_call(body, name="cast_weights", out_shape=[jax.ShapeDtypeStruct(t.shape, BF16) for t in shards],
                          in_specs=[vm] * n, out_specs=[vm] * n, compiler_params=_params(32, 0))(*shards)


def _adamw_math(w, g, m, v):
    m2 = ADAM_B1 * m + (1.0 - ADAM_B1) * g
    v2 = ADAM_B2 * v + (1.0 - ADAM_B2) * (g * g)
    m_hat = m2 / (1.0 - ADAM_B1 ** ADAM_STEP)
    v_hat = v2 / (1.0 - ADAM_B2 ** ADAM_STEP)
    delta = -ADAM_LR * (m_hat / (jnp.sqrt(v_hat) + ADAM_EPS) + ADAM_WD * w)
    return delta, m2, v2


def _final_sum_adamw(g_list, recv_list, me, w_list, m_list, v_list):
    n = len(w_list)

    def body(me_ref, *refs):
        own, recv, w, m, v = (refs[k * n:(k + 1) * n] for k in range(5))
        outs = [refs[(5 + k) * n:(6 + k) * n] for k in range(4)]
        for a in range(n):
            g = own[a][...]
            for k in range(N_DEV - 1):
                g = g + recv[a][k].astype(F32)
            delta, m2, v2 = _adamw_math(w[a][...], g, m[a][...], v[a][...])
            for o_ref, val in zip((outs[0][a], outs[1][a], outs[2][a], outs[3][a]), (g, delta, m2, v2)):
                o_ref[...] = val

    own_specs, flat, wire, shapes = [], [], [], []
    for t in w_list:
        rows, width = t.shape
        tr = rows // ROW_TILES
        own_specs.append(pl.BlockSpec((None, tr, width), lambda i, me: (me[0], i, 0)))
        flat.append(pl.BlockSpec((tr, width), lambda i, me: (i, 0)))
        wire.append(pl.BlockSpec((N_DEV - 1, tr, width), lambda i, me: (0, i, 0)))
        shapes.append(jax.ShapeDtypeStruct((rows, width), F32))
    out = pl.pallas_call(
        body, name="grad_sum_adamw", out_shape=shapes * 4,
        grid_spec=pltpu.PrefetchScalarGridSpec(
            num_scalar_prefetch=1, grid=(ROW_TILES,), in_specs=own_specs + wire + flat * 3, out_specs=flat * 4),
        compiler_params=_params(40),
    )(me, *g_list, *recv_list, *w_list, *m_list, *v_list)
    return [out[k * n:(k + 1) * n] for k in range(4)]


SMALL_NAMES = ("norm_0", "a_v_norm_0", "b_scale_0", "norm_1", "final_norm", "a_spatial_w_0", "a_spatial_b_0", "sink_1")
SMALL_VIEWS = ((8, LANES),) * 5 + ((4 * CHUNK, LANES), (4, LANES), (1, N_HEADS))
SMALL_ROW0 = (0, 8, 16, 24, 32, 40, 552, 560)
SMALL_ROWS = 568


def _small_sum_adamw(early, late, w_list, m_list, v_list):
    n = len(w_list)

    def body(e_ref, l_ref, *refs):
        gtot, first = e_ref[0], l_ref[0]
        for d in range(1, N_DEV):
            gtot = gtot + e_ref[d]
            first = first + l_ref[d]
        for a, ((rows, width), r0) in enumerate(zip(SMALL_VIEWS, SMALL_ROW0)):
            g = first if SMALL_NAMES[a] == "norm_0" else gtot[r0:r0 + rows, 0:width]
            delta, m2, v2 = _adamw_math(refs[a][...], g, refs[n + a][...], refs[2 * n + a][...])
            for k, val in enumerate((g, delta, m2, v2)):
                refs[(3 + k) * n + a][...] = val
        refs[7 * n][...] = gtot[LOSS_ROW:LOSS_ROW + 1, LOSS_LANE:LOSS_LANE + 1]

    vm = pl.BlockSpec(memory_space=pltpu.VMEM)
    shapes = [jax.ShapeDtypeStruct(s, F32) for s in SMALL_VIEWS]
    out = pl.pallas_call(
        body, name="small_sum_adamw", out_shape=shapes * 4 + [jax.ShapeDtypeStruct((1, 1), F32)],
        in_specs=[vm, vm] + [vm] * (3 * n), out_specs=[vm] * (4 * n + 1),
    )(early, late, *w_list, *m_list, *v_list)
    return [out[k * n:(k + 1) * n] for k in range(4)], out[4 * n]


PEER_FLIPS = tuple((fx, fy, fc) for fx in (0, 1) for fy in (0, 1) for fc in (0, 1))[1:]


def _sequencer_all_gather(blks, name, collective_id, concat_rows=False):
    n = len(blks)

    def body(*refs):
        ins, outs = refs[:n], refs[n:2 * n]
        send_sems, recv_sems, local_sems = refs[2 * n:]
        x, y, c = lax.axis_index("x"), lax.axis_index("y"), lax.axis_index("c")
        peers = [(x ^ fx, y ^ fy, c ^ fc) for fx, fy, fc in PEER_FLIPS]
        barrier = pltpu.get_barrier_semaphore()
        for peer in peers:
            pl.semaphore_signal(barrier, inc=1, device_id=peer, device_id_type=MESH)
        pl.semaphore_wait(barrier, len(peers))
        me = 4 * x + 2 * y + c

        def slot(a):
            rows = blks[a].shape[0]
            return outs[a].at[pl.ds(pl.multiple_of(me * rows, 16), rows)] if concat_rows else outs[a].at[me]

        copies = [pltpu.make_async_remote_copy(
            src_ref=ins[a], dst_ref=slot(a), send_sem=send_sems.at[k, a], recv_sem=recv_sems.at[k, a],
            device_id=peer, device_id_type=MESH) for k, peer in enumerate(peers) for a in range(n)]
        mine = [pltpu.make_async_copy(ins[a], slot(a), local_sems.at[a]) for a in range(n)]
        for cp in copies + mine:
            cp.start()
        for cp in copies + mine:
            cp.wait()

    out_shape = (lambda t: (N_DEV * t.shape[0],) + t.shape[1:]) if concat_rows else (lambda t: (N_DEV,) + t.shape)
    return pl.kernel(
        body, out_type=[jax.ShapeDtypeStruct(out_shape(t), t.dtype) for t in blks],
        mesh=plsc.ScalarSubcoreMesh(axis_name="sequencer", num_cores=1), name=name,
        scratch_types=[pltpu.SemaphoreType.DMA((7, n)), pltpu.SemaphoreType.DMA((7, n)), pltpu.SemaphoreType.DMA((n,))],
        compiler_params=pltpu.CompilerParams(collective_id=collective_id),
    )(*blks)


def _sequencer_scatter(g_list, name, collective_id):
    n = len(g_list)

    def body(*refs):
        ins, outs = refs[:n], refs[n:2 * n]
        send_sems, recv_sems = refs[2 * n:]
        x, y, c = lax.axis_index("x"), lax.axis_index("y"), lax.axis_index("c")
        peers = [(x ^ fx, y ^ fy, c ^ fc) for fx, fy, fc in PEER_FLIPS]
        barrier = pltpu.get_barrier_semaphore()
        for peer in peers:
            pl.semaphore_signal(barrier, inc=1, device_id=peer, device_id_type=MESH)
        pl.semaphore_wait(barrier, len(peers))
        copies = [pltpu.make_async_remote_copy(
            src_ref=ins[a].at[4 * px + 2 * py + pc], dst_ref=outs[a].at[k], send_sem=send_sems.at[k, a],
            recv_sem=recv_sems.at[k, a], device_id=(px, py, pc), device_id_type=MESH)
            for k, (px, py, pc) in enumerate(peers) for a in range(n)]
        for cp in copies:
            cp.start()
        for cp in copies:
            cp.wait()

    return pl.kernel(
        body, out_type=[jax.ShapeDtypeStruct((N_DEV - 1,) + g.shape[1:], g.dtype) for g in g_list],
        mesh=plsc.ScalarSubcoreMesh(axis_name="sequencer", num_cores=1), name=name,
        scratch_types=[pltpu.SemaphoreType.DMA((7, n)), pltpu.SemaphoreType.DMA((7, n))],
        compiler_params=pltpu.CompilerParams(collective_id=collective_id),
    )(*g_list)


def _direct_all_gather(blk, name):
    def body(g_ref, out_ref, send_sems, recv_sems, local_sem):
        x, y, c = lax.axis_index("x"), lax.axis_index("y"), lax.axis_index("c")
        me = 4 * x + 2 * y + c
        copies = [pltpu.make_async_remote_copy(
            src_ref=g_ref, dst_ref=out_ref.at[me], send_sem=send_sems.at[k], recv_sem=recv_sems.at[k],
            device_id=(x ^ fx, y ^ fy, c ^ fc), device_id_type=MESH) for k, (fx, fy, fc) in enumerate(PEER_FLIPS)]
        copies.append(pltpu.make_async_copy(g_ref, out_ref.at[me], local_sem))
        for cp in copies:
            cp.start()
        for cp in copies:
            cp.wait()

    any_spec = pl.BlockSpec(memory_space=pl.ANY)
    return pl.pallas_call(
        body, name=name, out_shape=jax.ShapeDtypeStruct((N_DEV,) + blk.shape, blk.dtype),
        in_specs=[any_spec], out_specs=any_spec,
        scratch_shapes=[pltpu.SemaphoreType.DMA((7,)), pltpu.SemaphoreType.DMA((7,)), pltpu.SemaphoreType.DMA],
    )(blk)


def _shard_views(w_in_0, b_group_w_0, w_out_0, w_in_1, w_out_1):
    return [w_in_0, b_group_w_0.reshape(4 * 32, GDIM), w_out_0, w_in_1, w_out_1]


def _small_views(named):
    return [named[name].reshape(view) for name, view in zip(SMALL_NAMES, SMALL_VIEWS)]


LOSS_ROW, LOSS_LANE = 560, N_HEADS


def _pack_small_grads(named, loss_part):
    rows = []
    for name, (r, w) in zip(SMALL_NAMES, SMALL_VIEWS):
        pad_r = -r % 8
        if name == "sink_1":
            t = jnp.concatenate([named[name].reshape(r, w), loss_part], axis=1)
            rows.append(jnp.pad(t, ((0, pad_r), (0, LANES - w - 1))))
        elif name in named:
            rows.append(jnp.pad(named[name].reshape(r, w), ((0, pad_r), (0, LANES - w))))
        else:
            rows.append(jnp.zeros((r + pad_r, LANES), F32))
    return jnp.concatenate(rows, axis=0)


def _device_blocks(t, axis):
    shape = t.shape
    t = t.reshape(shape[:axis] + (N_DEV, shape[axis] // N_DEV) + shape[axis + 1:])
    t = jnp.moveaxis(t, axis, 0)
    return t.reshape(N_DEV, -1, shape[-1] if axis != len(shape) - 1 else shape[-1] // N_DEV)


def kernel(x, norm_0, w_in_0, a_v_norm_0, a_spatial_w_0, a_spatial_b_0, b_group_w_0, b_scale_0, w_out_0, norm_1, w_in_1, sink_1, w_out_1, final_norm, loss_target, m_norm_0, m_w_in_0, m_a_v_norm_0, m_a_spatial_w_0, m_a_spatial_b_0, m_b_group_w_0, m_b_scale_0, m_w_out_0, m_norm_1, m_w_in_1, m_sink_1, m_w_out_1, m_final_norm, v_norm_0, v_w_in_0, v_a_v_norm_0, v_a_spatial_w_0, v_a_spatial_b_0, v_b_group_w_0, v_b_scale_0, v_w_out_0, v_norm_1, v_w_in_1, v_sink_1, v_w_out_1, v_final_norm):
    seq = x.shape[1]
    xs = x.reshape(seq, D)
    tgt = loss_target.reshape(seq, D)
    ax, ay, ac = lax.axis_index("x"), lax.axis_index("y"), lax.axis_index("c")
    me = jnp.reshape(4 * ax + 2 * ay + ac, (1,)).astype(jnp.int32)

    shards = _shard_views(w_in_0, b_group_w_0, w_out_0, w_in_1, w_out_1)
    cast = _cast_shards([shards[0], shards[1], shards[2], w_in_1.T, shards[4]])

    def l1_weights(after):
        blks, _ = lax.optimization_barrier((cast[3:5], after))
        return _sequencer_all_gather(blks, "weights_gather_l1", 2, concat_rows=True)

    blocks, received, early = {}, {}, {}
    collective_ids = {"l1": 3, "out0": 4, "in0": 5}

    def scatter(tag, own_blocks, wire_blocks):
        blocks[tag] = own_blocks
        received[tag] = _sequencer_scatter(wire_blocks, "grad_scatter_" + tag, collective_ids[tag])

    def small_early(named, loss_part):
        early["small"] = _sequencer_all_gather([_pack_small_grads(named, loss_part)], "small_grad_gather", 6)[0]

    grad_x, d_norm_0 = _local_step(xs, tgt, cast[0], cast[1:3], l1_weights, norm_0, a_v_norm_0, a_spatial_w_0,
                                   a_spatial_b_0, b_scale_0, norm_1, sink_1, final_norm, scatter, small_early)

    order = (("in0", 0), ("in0", 1), ("out0", 0), ("l1", 0), ("l1", 1))
    late = _direct_all_gather(d_norm_0.reshape(8, LANES), "norm_grad_gather")
    shards_late, _ = lax.optimization_barrier((shards, grad_x))
    big = _final_sum_adamw([blocks[t][i] for t, i in order], [received[t][i] for t, i in order], me, shards_late,
                           _shard_views(m_w_in_0, m_b_group_w_0, m_w_out_0, m_w_in_1, m_w_out_1),
                           _shard_views(v_w_in_0, v_b_group_w_0, v_w_out_0, v_w_in_1, v_w_out_1))
    weights = dict(norm_0=norm_0, a_v_norm_0=a_v_norm_0, a_spatial_w_0=a_spatial_w_0, a_spatial_b_0=a_spatial_b_0,
                   b_scale_0=b_scale_0, norm_1=norm_1, sink_1=sink_1, final_norm=final_norm)
    m_small = dict(norm_0=m_norm_0, a_v_norm_0=m_a_v_norm_0, a_spatial_w_0=m_a_spatial_w_0, a_spatial_b_0=m_a_spatial_b_0,
                   b_scale_0=m_b_scale_0, norm_1=m_norm_1, sink_1=m_sink_1, final_norm=m_final_norm)
    v_small = dict(norm_0=v_norm_0, a_v_norm_0=v_a_v_norm_0, a_spatial_w_0=v_a_spatial_w_0, a_spatial_b_0=v_a_spatial_b_0,
                   b_scale_0=v_b_scale_0, norm_1=v_norm_1, sink_1=v_sink_1, final_norm=v_final_norm)
    small, loss = _small_sum_adamw(early["small"], late, _small_views(weights), _small_views(m_small),
                                   _small_views(v_small))

    def in_order(kind):
        b = [b_.reshape(s_.shape) for b_, s_ in zip(big[kind], (w_in_0, b_group_w_0, w_out_0, w_in_1, w_out_1))]
        s = {name: t.reshape(weights[name].shape) for name, t in zip(SMALL_NAMES, small[kind])}
        return [s["norm_0"], b[0], s["a_v_norm_0"], s["a_spatial_w_0"], s["a_spatial_b_0"], b[1], s["b_scale_0"], b[2],
                s["norm_1"], b[3], s["sink_1"], b[4], s["final_norm"]]

    return (loss[0, 0], grad_x.reshape(1, seq, D), *in_order(0), *in_order(1), *in_order(2), *in_order(3))


def _local_step(xs, tgt, win0_shard, l0_shards, l1_weights, norm_0, a_v_norm_0, a_spatial_w_0, a_spatial_b_0, b_scale_0,
                norm_1, sink_1, final_norm, scatter, small_early):
    seq = xs.shape[0]
    ws = a_spatial_w_0.astype(BF16)
    ws_t = jnp.swapaxes(ws, 1, 2)
    bias = jnp.repeat(a_spatial_b_0.T, GDIM, axis=1)
    g0, gv, scale, g1, gf = (t.reshape(1, D) for t in (norm_0, a_v_norm_0, b_scale_0, norm_1, final_norm))
    cos_t, sin_t = _rope_tables_t(seq)

    za, bx, bg, h0_t, win0, g_wg, wout0 = _l0_in_proj(xs, g0, win0_shard, l0_shards)
    win1_t, wout1 = l1_weights(za)
    wg = g_wg.reshape(N_DEV, 4, 32, GDIM).transpose(1, 0, 2, 3).reshape(4, GDIM, GDIM)
    wg_t = jnp.swapaxes(wg, 1, 2)
    x1 = _l0_mix_fwd(za, bx, bg, xs, ws, bias, gv, wg, scale, wout0)
    win1_t, wout1, x1 = lax.optimization_barrier((win1_t, wout1, x1))
    qt, kt, vt, gatet, h1_t = _l1_in_proj(x1, g1, win1_t, cos_t, sin_t)
    dx2, dx2b, att, lse, loss_part, d_gf, d_wout1, d_wout1_wire = _l1_attn_fwd(
        qt, kt, vt, gatet, x1, tgt, wout1, gf, sink_1)

    dq_r, dgate, dk_pad, dv_pad, d_sink = _l1_attn_bwd(dx2b, wout1, qt, kt, vt, gatet, att, lse, sink_1)
    dk_r = dk_pad[:, BLK:BLK + seq]
    dv = dv_pad[:, BLK:BLK + seq]
    dx1, dx1b, dz1_t, d_g1 = _l1_in_proj_bwd(dq_r, dk_r, dv, dgate, cos_t, sin_t, win1_t, x1, g1, dx2)
    d_win1, d_win1_wire = _dw_matmul(h1_t, dz1_t, "dw_in_1", b_transposed=True, tn=1280, col_block=MIX1_IN // N_DEV)
    rows = lambda t: t.reshape(N_DEV, t.shape[0] // N_DEV, t.shape[1])
    scatter("l1", [d_win1, rows(d_wout1)], [d_win1_wire, rows(d_wout1_wire)])

    dz0, dp, cat_t, d_ws, _, d_gv, d_scale, d_wg, d_b = _l0_mix_bwd(
        dx1b, wout0, za, bx, bg, ws, ws_t, bias, gv, wg, wg_t, scale)
    dz0 = _l0_pool_bwd(dp, dz0)
    d_win0, d_win0_wire = _dw_matmul(h0_t, dz0, "dw_in_0", tn=1280, col_block=MIX0_IN // N_DEV)
    d_wg_blocks = _device_blocks(d_wg, 1)
    scatter("in0", [d_win0, d_wg_blocks], [d_win0_wire, d_wg_blocks])
    cat_t, _ = lax.optimization_barrier((cat_t, d_win0))
    d_wout0, d_wout0_wire = _dw_matmul(cat_t, dx1b, "dw_out_0")
    scatter("out0", [rows(d_wout0)], [rows(d_wout0_wire)])
    small_early(dict(a_v_norm_0=d_gv, a_spatial_w_0=d_ws, a_spatial_b_0=d_b.reshape(4, 8, CHUNK)[:, 0, :],
                     b_scale_0=d_scale, norm_1=d_g1, sink_1=d_sink[:, 0], final_norm=d_gf), loss_part)
    dz0, _ = lax.optimization_barrier((dz0, d_wout0))
    return _l0_in_proj_bwd(dz0, win0, xs, g0, dx1)
```

```python
import jax
import jax.numpy as jnp
from jax import lax
from jax.experimental import pallas as pl
from jax.experimental.pallas import tpu as pltpu
from jax.experimental.pallas import tpu_sc as plsc

F32 = jnp.float32
BF16 = jnp.bfloat16

D = 1024
EPS = 1e-6
NEG_INF = -1e30
CHUNK = 128
A_GROUPS = 4
POOL_WINDOWS = (2, 4, 8, 16)
POOL_HALO = 8
GDIM = 256
N_HEADS = 16
N_KV = 4
GQA = 4
HD = 64
BLK = 128
ROT_HALF = 8
ROPE_THETA = 500000.0
SCALE = HD ** -0.5
MIX0_IN = 5 * D
MIX1_IN = 2560
KV_W = N_KV * HD
Q_ROWS, K_ROWS, V_ROWS, G_ROWS = (0, D), (D, D + KV_W), (D + KV_W, D + 2 * KV_W), (D + 2 * KV_W, MIX1_IN)
TQ = 512

ADAM_LR = 0.001
ADAM_B1 = 0.9
ADAM_B2 = 0.999
ADAM_EPS = 1e-08
ADAM_WD = 0.01
ADAM_STEP = 10

N_DEV = 8
LANES = 128
MIB = 2 ** 20
MESH = pl.DeviceIdType.MESH


def _params(limit_mib, n_axes=1):
    return pltpu.CompilerParams(vmem_limit_bytes=limit_mib * MIB, dimension_semantics=("arbitrary",) * n_axes)


def _resident(shape):
    nd = len(shape)
    return pl.BlockSpec(shape, lambda *_: (0,) * nd, pipeline_mode=pl.Buffered(1))


def _gelu(x):
    k = 0.7978845608028654
    return 0.5 * x * (1.0 + jnp.tanh(k * (x + 0.044715 * x * x * x)))


def _gelu_and_grad(x):
    k = 0.7978845608028654
    x2 = x * x
    t = jnp.tanh(k * (x + 0.044715 * x * x2))
    g = 0.5 * x * (1.0 + t)
    dg = 0.5 * (1.0 + t) + 0.5 * x * (1.0 - t * t) * (k * (1.0 + 3.0 * 0.044715 * x2))
    return g, dg


def _silu_and_grad(x):
    s = jax.nn.sigmoid(x)
    return x * s, s * (1.0 + x * (1.0 - s))


def _nt(a, b):
    return lax.dot_general(a, b, (((1,), (1,)), ((), ())), preferred_element_type=F32)


def _tn(a, b):
    return lax.dot_general(a, b, (((0,), (0,)), ((), ())), preferred_element_type=F32)


def _mm(a, b):
    return jnp.dot(a, b, preferred_element_type=F32)


def _rope_tables_t(seq):
    inv = ROPE_THETA ** (-jnp.arange(0, 2 * ROT_HALF, 2, dtype=F32) / (2 * ROT_HALF))
    ang = inv[:, None] * jnp.arange(seq, dtype=F32)[None, :]
    return jnp.cos(ang), jnp.sin(ang)


def _rope_t(z, c, s, n_heads, sign):
    parts = []
    for h in range(n_heads):
        b = h * HD
        x1, x2 = z[b:b + ROT_HALF], z[b + ROT_HALF:b + 2 * ROT_HALF]
        if sign > 0:
            parts += [x1 * c - x2 * s, x2 * c + x1 * s]
        else:
            parts += [x1 * c + x2 * s, x2 * c - x1 * s]
        parts.append(z[b + 2 * ROT_HALF:b + HD])
    return jnp.concatenate(parts, axis=0)


N_CHIPS = 4
CHIP_COLS = MIX0_IN // N_CHIPS
IN_PROJ_ROWS, IN_PROJ_CHUNK = 512, 256
IN_PROJ_SPLIT = 4
IN_PROJ_PIECES = (
    ((0, 0, CHIP_COLS, 0),),
    ((0, CHIP_COLS, CHIP_COLS, 0),),
    ((0, 2 * CHIP_COLS, 3 * D - 2 * CHIP_COLS, 0), (1, 0, 3 * CHIP_COLS - 3 * D, 3 * D - 2 * CHIP_COLS)),
    ((1, 3 * CHIP_COLS - 3 * D, 4 * D - 3 * CHIP_COLS, 0), (2, 0, D, 4 * D - 3 * CHIP_COLS)),
)


def _l0_in_proj(x, g0, w_shard, later_shards):
    seq = x.shape[0]
    tm = 512
    n = seq // tm
    f32_cols = max(width for pieces in IN_PROJ_PIECES for o, _, width, _ in pieces if o == 1)
    shard_cols = w_shard.shape[1]
    n_arr = 1 + len(later_shards)
    parts = [(0, s * (D // IN_PROJ_SPLIT), D // IN_PROJ_SPLIT) for s in range(IN_PROJ_SPLIT)]
    parts += [(a + 1, 0, t.shape[0]) for a, t in enumerate(later_shards)]
    w_parts, later_parts = range(IN_PROJ_SPLIT), range(IN_PROJ_SPLIT, len(parts))
    assert 2 * shard_cols == CHIP_COLS and seq % tm == 0 and n >= 4

    def body(*refs):
        x_ref, g_ref = refs[:2]
        ins = refs[2:2 + n_arr]
        za_ref, bx_ref, bg_ref, ht_ref = refs[2 + n_arr:6 + n_arr]
        gathered = refs[6 + n_arr:6 + 2 * n_arr]
        h_all, w_buf, z32, z16, send_sems, recv_sems, local_sems, load_sems, out_sems = refs[6 + 2 * n_arr:]
        p, i = pl.program_id(0), pl.program_id(1)
        ax, ay, ac = lax.axis_index("x"), lax.axis_index("y"), lax.axis_index("c")
        me, sibling = (ax, ay, ac), (ax, ay, 1 - ac)
        chips = [(ax, ay), (1 - ax, ay), (ax, 1 - ay), (1 - ax, 1 - ay)]
        outs = (za_ref, bx_ref, bg_ref)

        def source(t):
            arr, r0, rows = parts[t]
            return ins[arr].at[pl.ds(r0, rows)]

        def slot(t, px, py, pc):
            arr, r0, rows = parts[t]
            dev = 4 * px + 2 * py + pc
            if arr == 0:
                return gathered[0].at[pl.ds(r0, rows), pl.ds(pl.multiple_of(dev * shard_cols, LANES), shard_cols)]
            return gathered[arr].at[pl.ds(pl.multiple_of(dev * rows, 16), rows)]

        def copy(k, t, block, to, from_input=False):
            return pltpu.make_async_remote_copy(
                src_ref=source(t) if from_input else slot(t, *block), dst_ref=slot(t, *block),
                send_sem=send_sems.at[k, t], recv_sem=recv_sems.at[k, t], device_id=to, device_id_type=MESH)

        def to_sibling(t):
            return copy(0, t, me, sibling, from_input=True)

        def from_sibling(t):
            return copy(0, t, sibling, me)

        def send(j, t):
            return copy(j, t, me, (*chips[j], ac), from_input=True)

        def landed(j, t):
            return copy(j, t, (*chips[j], ac), me)

        def forward(j, t):
            return copy(3 + j, t, (*chips[j], ac), sibling)

        def forwarded(j, t):
            return copy(3 + j, t, (*chips[j], 1 - ac), me)

        def mine(t):
            return pltpu.make_async_copy(source(t), slot(t, *me), local_sems.at[t])

        def load(chip, q):
            px, py = chip
            cols = pl.ds(pl.multiple_of((2 * px + py) * CHIP_COLS, LANES), CHIP_COLS)
            return pltpu.make_async_copy(gathered[0].at[:, cols], w_buf.at[q % 2], load_sems.at[q % 2])

        def out_copies(q, tile, stage):
            cps = []
            for k, (o, c0, width, z0) in enumerate(IN_PROJ_PIECES[q]):
                src = z32.at[stage, :, pl.ds(0, width)] if o == 1 else z16.at[stage, :, pl.ds(z0, width)]
                dst = outs[o].at[pl.ds(pl.multiple_of(tile * tm, tm), tm), pl.ds(c0, width)]
                cps.append(pltpu.make_async_copy(src, dst, out_sems.at[stage, k]))
            return cps

        @pl.when((p == 0) & (i == 0))
        def _():
            for t in w_parts:
                mine(t).start()
                to_sibling(t).start()
                for j in range(1, N_CHIPS):
                    send(j, t).start()
            for t in later_parts:
                mine(t).start()
                to_sibling(t).start()
            for t in w_parts:
                from_sibling(t).wait_recv()
                mine(t).wait()
            load(chips[0], 0).start()

        for j in range(1, N_CHIPS):
            @pl.when((p == j - 1) & (i == n - 2))
            def _(j=j):
                for t in w_parts:
                    landed(j, t).wait_recv()
                    forward(j, t).start()
                if j == 2:
                    for t in later_parts:
                        for jj in range(1, N_CHIPS):
                            send(jj, t).start()

            @pl.when((p == j - 1) & (i == n - 1))
            def _(j=j):
                for t in w_parts:
                    forwarded(j, t).wait_recv()
                load(chips[j], j).start()

        @pl.when((p == N_CHIPS - 1) & (i == n - 4))
        def _():
            for jj in range(1, N_CHIPS):
                for t in later_parts:
                    landed(jj, t).wait_recv()
                    forward(jj, t).start()

        @pl.when(i == 0)
        def _():
            load(chips[0], p).wait()

        @pl.when(p == 0)
        def _():
            xf = x_ref[...]
            r = lax.rsqrt(jnp.mean(xf * xf, axis=1, keepdims=True) + EPS)
            h = (xf * r * g_ref[...]).astype(BF16)
            ht_ref[...] = h.T
            h_all[pl.ds(pl.multiple_of(i * tm, tm), tm), :] = h

        def chip_of_pass(pp):
            return (2 * ax + ay) ^ ((pp >> 1) | ((pp & 1) << 1))

        step = p * n + i
        stage = step % 2
        for q in range(N_CHIPS):
            @pl.when((step >= 2) & (chip_of_pass((step - 2) // n) == q))
            def _(q=q):
                for cp in out_copies(q, (step - 2) % n, stage):
                    cp.wait()

        for q in range(N_CHIPS):
            @pl.when(chip_of_pass(p) == q)
            def _(q=q):
                f32_from = [(z0, width) for o, _, width, z0 in IN_PROJ_PIECES[q] if o == 1]
                for r0 in range(0, tm, IN_PROJ_ROWS):
                    rows = pl.ds(r0, IN_PROJ_ROWS)
                    h = h_all[pl.ds(pl.multiple_of(i * tm + r0, IN_PROJ_ROWS), IN_PROJ_ROWS), :]
                    for c0 in range(0, CHIP_COLS, IN_PROJ_CHUNK):
                        z = _mm(h, w_buf[p % 2, :, pl.ds(c0, IN_PROJ_CHUNK)])
                        z16[stage, rows, pl.ds(c0, IN_PROJ_CHUNK)] = z.astype(BF16)
                        for z0, width in f32_from:
                            if z0 <= c0 < z0 + width:
                                z32[stage, rows, pl.ds(c0 - z0, IN_PROJ_CHUNK)] = z
                for cp in out_copies(q, i, stage):
                    cp.start()

        last = (p == N_CHIPS - 1) & (i == n - 1)
        for q in range(N_CHIPS):
            @pl.when(last & (chip_of_pass(p) == q))
            def _(q=q):
                for cp in out_copies(q, n - 2, 1 - stage) + out_copies(q, n - 1, stage):
                    cp.wait()

        @pl.when(last)
        def _():
            for t in later_parts:
                from_sibling(t).wait_recv()
                for jj in range(1, N_CHIPS):
                    forwarded(jj, t).wait_recv()
                mine(t).wait()
            for t in range(len(parts)):
                to_sibling(t).wait_send()
                for jj in range(1, N_CHIPS):
                    send(jj, t).wait_send()
                    forward(jj, t).wait_send()

    any_spec = pl.BlockSpec(memory_space=pl.ANY)
    first_pass_tile = lambda p, i: jnp.where(p == 0, i, n - 1)
    return pl.pallas_call(
        body, grid=(N_CHIPS, n), name="l0_in_proj",
        out_shape=[jax.ShapeDtypeStruct((seq, 3 * D), BF16), jax.ShapeDtypeStruct((seq, D), F32),
                   jax.ShapeDtypeStruct((seq, D), BF16), jax.ShapeDtypeStruct((D, seq), BF16),
                   jax.ShapeDtypeStruct((D, MIX0_IN), BF16)]
        + [jax.ShapeDtypeStruct((N_DEV * t.shape[0], t.shape[1]), t.dtype) for t in later_shards],
        in_specs=[pl.BlockSpec((tm, D), lambda p, i: (first_pass_tile(p, i), 0)), _resident((1, D))] + [any_spec] * n_arr,
        out_specs=[any_spec, any_spec, any_spec, pl.BlockSpec((D, tm), lambda p, i: (0, first_pass_tile(p, i)))]
        + [any_spec] * n_arr,
        scratch_shapes=[pltpu.VMEM((seq, D), BF16), pltpu.VMEM((2, D, CHIP_COLS), BF16),
                        pltpu.VMEM((2, tm, f32_cols), F32), pltpu.VMEM((2, tm, CHIP_COLS), BF16),
                        pltpu.SemaphoreType.DMA((7, len(parts))), pltpu.SemaphoreType.DMA((7, len(parts))),
                        pltpu.SemaphoreType.DMA((len(parts),)), pltpu.SemaphoreType.DMA((2,)),
                        pltpu.SemaphoreType.DMA((2, 2))],
        compiler_params=_params(56, 2),
    )(x, g0, w_shard, *later_shards)


POOL_EXT = 40


def _fill_halo(ext_ref, cur, prev_ref, next_ref, i, n_tiles, ts):
    ext_ref[pl.ds(0, POOL_HALO), :] = jnp.where(i > 0, prev_ref[...], 0.0)
    ext_ref[pl.ds(POOL_HALO, ts), :] = cur
    ext_ref[pl.ds(POOL_HALO + ts, POOL_HALO), :] = jnp.where(i < n_tiles - 1, next_ref[...], 0.0)
    ext_ref[pl.ds(2 * POOL_HALO + ts, POOL_EXT - 2 * POOL_HALO), :] = jnp.zeros((POOL_EXT - 2 * POOL_HALO, D), F32)


def _window_sums(src_ref, tmp_refs, ts, cols, w, shift):
    if w == 2:
        return src_ref[pl.ds(POOL_HALO - 1 + shift, ts), cols] + src_ref[pl.ds(POOL_HALO + shift, ts), cols]
    d2, d4, d8 = tmp_refs
    n2, n4, n8 = ts + 32, ts + 24, ts + 16
    d2[pl.ds(0, n2), :] = src_ref[pl.ds(0, n2), cols] + src_ref[pl.ds(1, n2), cols]
    if w == 4:
        return d2[pl.ds(POOL_HALO - 2 + shift, ts), :] + d2[pl.ds(POOL_HALO + shift, ts), :]
    d4[pl.ds(0, n4), :] = d2[pl.ds(0, n4), :] + d2[pl.ds(2, n4), :]
    if w == 8:
        return d4[pl.ds(POOL_HALO - 4 + shift, ts), :] + d4[pl.ds(POOL_HALO + shift, ts), :]
    d8[pl.ds(0, n8), :] = d4[pl.ds(0, n8), :] + d4[pl.ds(4, n8), :]
    return d8[pl.ds(shift, ts), :] + d8[pl.ds(POOL_HALO + shift, ts), :]


def _pool_scratch(ts):
    return [pltpu.VMEM((ts + POOL_EXT, D), F32)] + [pltpu.VMEM((ts + POOL_EXT, GDIM), F32)] * 3


def _pool_forward(xe_ref, tmp_refs, ts, t0, seq):
    tg = t0 + lax.broadcasted_iota(jnp.int32, (ts, 1), 0)
    outs = []
    for gi, w in enumerate(POOL_WINDOWS):
        hw = w // 2
        cols = slice(gi * GDIM, (gi + 1) * GDIM)
        cnt = (jnp.minimum(tg + hw, seq) - jnp.maximum(tg - hw, 0)).astype(F32)
        outs.append(_window_sums(xe_ref, tmp_refs, ts, cols, w, 0) / cnt - xe_ref[pl.ds(POOL_HALO, ts), cols])
    return jnp.concatenate(outs, axis=1)


def _spatial_mix(ws_ref, vnb, bias, ts):
    rows = []
    for c in range(ts // CHUNK):
        vc = vnb[c * CHUNK:(c + 1) * CHUNK, :]
        rows.append(jnp.concatenate(
            [_mm(ws_ref[h], vc[:, h * GDIM:(h + 1) * GDIM]) for h in range(A_GROUPS)], axis=1) + bias)
    return jnp.concatenate(rows, axis=0)


def _halo_specs(ts, seq, width):
    per = ts // POOL_HALO
    last = seq // POOL_HALO - 1
    prev = pl.BlockSpec((POOL_HALO, width), lambda i: (jnp.maximum(i * per - 1, 0), 0))
    nxt = pl.BlockSpec((POOL_HALO, width), lambda i: (jnp.minimum((i + 1) * per, last), 0))
    return prev, nxt


def _l0_mix_fwd(za, bx, bg, x, ws, bias, gv, wg, scale, wout):
    seq = x.shape[0]
    ts = 512
    n_tiles = seq // ts

    def body(za_ref, bx_ref, bxp_ref, bxn_ref, bg_ref, x_ref, ws_ref, bias_ref, gv_ref, wg_ref, sc_ref, wo_ref,
             x1_ref, xe_ref, *tmp_refs):
        i = pl.program_id(0)
        vg = _gelu(za_ref[:, D:2 * D].astype(F32))
        rv = lax.rsqrt(jnp.mean(vg * vg, axis=1, keepdims=True) + EPS)
        vnb = (vg * rv * gv_ref[...]).astype(BF16)
        mixed = _spatial_mix(ws_ref, vnb, bias_ref[...], ts)

        _fill_halo(xe_ref, bx_ref[...], bxp_ref, bxn_ref, i, n_tiles, ts)
        pb = _pool_forward(xe_ref, tmp_refs, ts, i * ts, seq).astype(BF16)
        ypre = jnp.concatenate([_mm(pb[:, g * GDIM:(g + 1) * GDIM], wg_ref[g]) for g in range(4)], axis=1)

        u = _gelu(za_ref[:, 0:D].astype(F32))
        ag = za_ref[:, 2 * D:3 * D].astype(F32)
        ya = (u * mixed * (ag * jax.nn.sigmoid(ag))).astype(BF16)
        out_a = _mm(ya, wo_ref[0:D, :])

        bgf = bg_ref[...].astype(F32)
        yb = (ypre * sc_ref[...] * (bgf * jax.nn.sigmoid(bgf))).astype(BF16)
        x1_ref[...] = x_ref[...] + out_a + _mm(yb, wo_ref[D:2 * D, :])

    prev, nxt = _halo_specs(ts, seq, D)
    row = lambda w: pl.BlockSpec((ts, w), lambda i: (i, 0))
    return pl.pallas_call(
        body, grid=(n_tiles,), name="l0_mix_fwd",
        out_shape=jax.ShapeDtypeStruct((seq, D), F32),
        in_specs=[row(3 * D), row(D), prev, nxt, row(D), row(D), _resident((4, CHUNK, CHUNK)), _resident((CHUNK, D)),
                  _resident((1, D)), _resident((4, GDIM, GDIM)), _resident((1, D)), _resident((2 * D, D))],
        out_specs=row(D),
        scratch_shapes=_pool_scratch(ts),
        compiler_params=_params(56),
    )(za, bx, bx, bx, bg, x, ws, bias, gv, wg, scale, wout)


def _l1_in_proj(x1, g1, w_t, cos_t, sin_t):
    seq = x1.shape[0]
    tm = 512

    def body(x_ref, g_ref, wt_ref, c_ref, s_ref, q_ref, k_ref, v_ref, gate_ref, ht_ref):
        xf = x_ref[...]
        r = lax.rsqrt(jnp.mean(xf * xf, axis=1, keepdims=True) + EPS)
        ht = (xf * r * g_ref[...]).astype(BF16).T
        ht_ref[...] = ht
        c, s = c_ref[...], s_ref[...]
        q_ref[...] = (_rope_t(_mm(wt_ref[Q_ROWS[0]:Q_ROWS[1], :], ht), c, s, N_HEADS, 1) * SCALE).astype(BF16)
        k_ref[...] = _rope_t(_mm(wt_ref[K_ROWS[0]:K_ROWS[1], :], ht), c, s, N_KV, 1).astype(BF16)
        v_ref[...] = _mm(wt_ref[V_ROWS[0]:V_ROWS[1], :], ht).astype(BF16)
        gate_ref[...] = _mm(wt_ref[G_ROWS[0]:G_ROWS[1], :], ht).astype(BF16)

    col = lambda rows: pl.BlockSpec((rows, tm), lambda i: (0, i))
    return pl.pallas_call(
        body, grid=(seq // tm,), name="l1_in_proj",
        out_shape=(jax.ShapeDtypeStruct((D, seq), BF16), jax.ShapeDtypeStruct((KV_W, seq), BF16),
                   jax.ShapeDtypeStruct((KV_W, seq), BF16), jax.ShapeDtypeStruct((D, seq), BF16),
                   jax.ShapeDtypeStruct((D, seq), BF16)),
        in_specs=[pl.BlockSpec((tm, D), lambda i: (i, 0)), _resident((1, D)), _resident((MIX1_IN, D)), col(ROT_HALF),
                  col(ROT_HALF)],
        out_specs=(col(D), col(KV_W), col(KV_W), col(D), col(D)),
        compiler_params=_params(48),
    )(x1, g1, w_t, cos_t, sin_t)


def _band_specs_t(nb, clamp_i):
    per = TQ // BLK
    prev = pl.BlockSpec((KV_W, BLK), lambda i: (0, jnp.maximum(clamp_i(i) * per - 1, 0)))
    cur = pl.BlockSpec((KV_W, TQ), lambda i: (0, clamp_i(i)))
    nxt = pl.BlockSpec((KV_W, BLK), lambda i: (0, jnp.minimum((clamp_i(i) + 1) * per, nb - 1)))
    return [prev, cur, nxt]


def _fill_band(buf, p_ref, c_ref, n_ref):
    buf[:, 0:BLK] = p_ref[...]
    buf[:, BLK:BLK + TQ] = c_ref[...]
    buf[:, BLK + TQ:2 * BLK + TQ] = n_ref[...]


def _band_bias_t(n, nb):
    c = lax.broadcasted_iota(jnp.int32, (BLK, BLK), 0)
    r = lax.broadcasted_iota(jnp.int32, (BLK, BLK), 1)
    first = jnp.where((c >= r) & (n > 0), 0.0, NEG_INF).astype(F32)
    last = jnp.where((c <= r) & (n < nb - 1), 0.0, NEG_INF).astype(F32)
    return jnp.concatenate([first] * HPP, axis=1), jnp.concatenate([last] * HPP, axis=1)


def _masked(st, bias):
    first, last = bias
    return jnp.concatenate([st[0:BLK] + first, st[BLK:2 * BLK], st[2 * BLK:3 * BLK] + last], axis=0)


AUG = 16


def _ones_rows(n_ones, width):
    return (lax.broadcasted_iota(jnp.int32, (AUG, width), 0) < n_ones).astype(BF16)


def _minus_rows(vec):
    hi = vec.astype(BF16).astype(F32)
    lo = vec - hi
    return jnp.concatenate([-hi, -lo, jnp.zeros((AUG - 2, vec.shape[1]), F32)], axis=0).astype(BF16)


HPP = GQA
FWD_GROUP, BWD_GROUP = 2, 1
BWD_AHEAD = 1


def _heads_t(ref, h0, c0):
    return jnp.concatenate([ref[(h0 + g) * HD:(h0 + g + 1) * HD, c0:c0 + BLK] for g in range(HPP)], axis=1)


def _row4(ref, h0, c0):
    return jnp.concatenate([ref[h0 + g:h0 + g + 1, c0:c0 + BLK] for g in range(HPP)], axis=1)


def _sink_row(sink_ref, h0):
    return jnp.concatenate([jnp.full((1, BLK), sink_ref[h0 + g], F32) for g in range(HPP)], axis=1)


def _l1_attn_fwd(qt, kt, vt, gatet, x1, tgt, wout, gf, sink):
    seq = x1.shape[0]
    nq, nb = seq // TQ, seq // BLK

    def body(q_ref, gate_ref, kp_ref, k_ref, kn_ref, vp_ref, v_ref, vn_ref, x1_ref, tgt_ref, wo_ref, gf_ref, sink_ref,
             dx2_ref, dx2b_ref, att_ref, lse_ref, loss_ref, dgf_ref, dwo_ref, dwo_wire_ref, kbuf, vbuf, att_scr):
        i = pl.program_id(0)

        @pl.when(i == 0)
        def _():
            loss_ref[...] = jnp.zeros_like(loss_ref)
            dgf_ref[...] = jnp.zeros_like(dgf_ref)
            dwo_ref[...] = jnp.zeros_like(dwo_ref)

        _fill_band(kbuf, kp_ref, k_ref, kn_ref)
        _fill_band(vbuf, vp_ref, v_ref, vn_ref)
        ones_row = _ones_rows(1, 3 * BLK)
        groups = [list(range(0, N_HEADS, HPP))[g:g + FWD_GROUP] for g in range(0, N_HEADS // HPP, FWD_GROUP)]
        work = [(j, grp) for j in range(TQ // BLK) for grp in groups]

        def scores(j, passes):
            c0 = j * BLK
            bias = _band_bias_t(i * (TQ // BLK) + j, nb)
            st = dict(c0=c0, passes=passes)
            st["kv_rows"] = [slice(h0 // GQA * HD, (h0 // GQA + 1) * HD) for h0 in passes]
            st["sts"] = [_masked(_tn(kbuf[rows, c0:c0 + 3 * BLK], _heads_t(q_ref, h0, c0)), bias)
                         for h0, rows in zip(passes, st["kv_rows"])]
            return st

        def softmaxes(st):
            st["sks"] = [_sink_row(sink_ref, h0) for h0 in st["passes"]]
            st["ms"] = [jnp.maximum(jnp.max(s_, axis=0, keepdims=True), sk) for s_, sk in zip(st["sts"], st["sks"])]
            st["ps"] = [jnp.exp(s_ - m).astype(BF16) for s_, m in zip(st["sts"], st["ms"])]

        def values(st):
            c0, passes = st["c0"], st["passes"]
            pvs = [_mm(jnp.concatenate([vbuf[rows, c0:c0 + 3 * BLK], ones_row], axis=0), p)
                   for rows, p in zip(st["kv_rows"], st["ps"])]
            lse_rows = []
            for h0, pv, m, sk in zip(passes, pvs, st["ms"], st["sks"]):
                den = pv[HD:HD + 1, :] + jnp.exp(sk - m)
                ot = pv[0:HD, :] / den
                lse = m + jnp.log(den)
                for g in range(HPP):
                    h = h0 + g
                    att_scr[h * HD:(h + 1) * HD, c0:c0 + BLK] = ot[:, g * BLK:(g + 1) * BLK]
                    lse_rows.append(lse[:, g * BLK:(g + 1) * BLK])
            lse_ref[passes[0]:passes[0] + len(lse_rows), c0:c0 + BLK] = jnp.concatenate(lse_rows, axis=0)

        state = scores(*work[0])
        for nxt in work[1:] + [None]:
            following = scores(*nxt) if nxt is not None else None
            softmaxes(state)
            values(state)
            state = following

        att = att_scr[...]
        gate = gate_ref[...].astype(F32)
        yt = (att * (gate * jax.nn.sigmoid(gate))).astype(BF16)
        att_ref[...] = att.astype(BF16)
        x2 = x1_ref[...] + _mm(yt.T, wo_ref[...])
        r = lax.rsqrt(jnp.mean(x2 * x2, axis=1, keepdims=True) + EPS)
        xn = x2 * r
        diff = xn * gf_ref[...] - tgt_ref[...]
        loss_ref[...] += 0.5 * jnp.sum(jnp.mean(diff * diff, axis=1, keepdims=True), axis=0, keepdims=True)
        dout = diff * (1.0 / D)
        dgf_ref[...] += jnp.sum(dout * xn, axis=0, keepdims=True)
        dxn = dout * gf_ref[...]
        dx2 = r * (dxn - xn * jnp.mean(dxn * xn, axis=1, keepdims=True))
        dx2_ref[...] = dx2
        dx2b = dx2.astype(BF16)
        dx2b_ref[...] = dx2b
        dwo_ref[...] += _mm(yt, dx2b)

        @pl.when(i == nq - 1)
        def _():
            dwo_wire_ref[...] = dwo_ref[...].astype(BF16)

    ident = lambda i: i
    row = pl.BlockSpec((TQ, D), lambda i: (i, 0))
    col = lambda rows: pl.BlockSpec((rows, TQ), lambda i: (0, i))
    whole = pl.BlockSpec((D, D), lambda i: (0, 0))
    return pl.pallas_call(
        body, grid=(nq,), name="l1_attn_fwd",
        out_shape=(jax.ShapeDtypeStruct((seq, D), F32), jax.ShapeDtypeStruct((seq, D), BF16),
                   jax.ShapeDtypeStruct((D, seq), BF16),
                   jax.ShapeDtypeStruct((N_HEADS, seq), F32), jax.ShapeDtypeStruct((1, 1), F32),
                   jax.ShapeDtypeStruct((1, D), F32), jax.ShapeDtypeStruct((D, D), F32), jax.ShapeDtypeStruct((D, D), BF16)),
        in_specs=[col(D), col(D)] + _band_specs_t(nb, ident) + _band_specs_t(nb, ident) + [
            row, row, _resident((D, D)), _resident((1, D)), pl.BlockSpec(memory_space=pltpu.SMEM)],
        out_specs=(row, row, col(D), col(N_HEADS), pl.BlockSpec((1, 1), lambda i: (0, 0)),
                   pl.BlockSpec((1, D), lambda i: (0, 0)), whole, whole),
        scratch_shapes=[pltpu.VMEM((KV_W, TQ + 2 * BLK), BF16), pltpu.VMEM((KV_W, TQ + 2 * BLK), BF16),
                        pltpu.VMEM((D, TQ), F32)],
        compiler_params=_params(56),
    )(qt, gatet, kt, kt, kt, vt, vt, vt, x1, tgt, wout, gf, sink)


def _l1_attn_bwd(dx2b, wout, qt, kt, vt, gatet, att, lse, sink):
    seq = dx2b.shape[0]
    nq, nb = seq // TQ, seq // BLK

    def body(dx_ref, wo_ref, q_ref, gate_ref, kp_ref, k_ref, kn_ref, vp_ref, v_ref, vn_ref, att_ref, lse_ref, sink_ref,
             dq_ref, dgate_ref, dk_ref, dv_ref, dsink_ref, kbuf, vbuf, dkacc, dvacc, dat_scr, delta_scr, dsacc):
        i = pl.program_id(0)

        @pl.when(i == 0)
        def _():
            dkacc[...] = jnp.zeros_like(dkacc)
            dvacc[...] = jnp.zeros_like(dvacc)
            dsacc[...] = jnp.zeros_like(dsacc)

        @pl.when(i > 0)
        def _():
            for acc in (dkacc, dvacc):
                acc[:, 0:2 * BLK] = acc[:, TQ:TQ + 2 * BLK]
                acc[:, 2 * BLK:2 * BLK + TQ] = jnp.zeros((KV_W, TQ), F32)

        @pl.when(i < nq)
        def _():
            _fill_band(kbuf, kp_ref, k_ref, kn_ref)
            _fill_band(vbuf, vp_ref, v_ref, vn_ref)
            dyt = _nt(wo_ref[...], dx_ref[...])
            sg, dsg = _silu_and_grad(gate_ref[...].astype(F32))
            attf = att_ref[...].astype(F32)
            dat = dyt * sg
            dat_scr[...] = dat.astype(BF16)
            dgate_ref[...] = (dyt * attf * dsg).astype(BF16)
            dl = dat * attf
            delta_scr[...] = jnp.concatenate(
                [jnp.sum(dl[h * HD:(h + 1) * HD, :], axis=0, keepdims=True) for h in range(N_HEADS)], axis=0)
            ones_rows = _ones_rows(2, 3 * BLK)
            groups = [list(range(0, N_HEADS, HPP))[g:g + BWD_GROUP] for g in range(0, N_HEADS // HPP, BWD_GROUP)]
            work = [(j, grp) for j in range(TQ // BLK) for grp in groups]

            def scores(j, passes):
                c0 = j * BLK
                st = dict(c0=c0, passes=passes, bias=_band_bias_t(i * (TQ // BLK) + j, nb))
                st["kv_rows"] = [slice(h0 // GQA * HD, (h0 // GQA + 1) * HD) for h0 in passes]
                st["q4s"] = [_heads_t(q_ref, h0, c0) for h0 in passes]
                st["do4s"] = [_heads_t(dat_scr, h0, c0) for h0 in passes]
                st["lse4s"] = [_row4(lse_ref, h0, c0) for h0 in passes]
                st["delta4s"] = [_row4(delta_scr, h0, c0) for h0 in passes]
                st["kths"] = [kbuf[rows, c0:c0 + 3 * BLK] for rows in st["kv_rows"]]
                st["sts"] = [_tn(jnp.concatenate([kth, ones_rows], axis=0),
                                 jnp.concatenate([q4, _minus_rows(lse4)], axis=0))
                             for kth, q4, lse4 in zip(st["kths"], st["q4s"], st["lse4s"])]
                st["dpds"] = [_tn(jnp.concatenate([vbuf[rows, c0:c0 + 3 * BLK], ones_rows], axis=0),
                                  jnp.concatenate([do4, _minus_rows(delta4)], axis=0))
                              for rows, do4, delta4 in zip(st["kv_rows"], st["do4s"], st["delta4s"])]
                return st

            def elementwise(st):
                st["ps"] = [jnp.exp(_masked(s_, st["bias"])) for s_ in st["sts"]]
                st["dss"] = [(p * dpd).astype(BF16) for p, dpd in zip(st["ps"], st["dpds"])]

            def gradients(st):
                c0 = st["c0"]
                dq4s = [_mm(kth, ds) * SCALE for kth, ds in zip(st["kths"], st["dss"])]
                dks = [_nt(q4, ds) for q4, ds in zip(st["q4s"], st["dss"])]
                dvs = [_nt(do4, p.astype(BF16)) for do4, p in zip(st["do4s"], st["ps"])]
                for h0, rows, dq4, dk, dv, lse4, delta4 in zip(st["passes"], st["kv_rows"], dq4s, dks, dvs, st["lse4s"],
                                                               st["delta4s"]):
                    dkacc[rows, c0:c0 + 3 * BLK] += dk
                    dvacc[rows, c0:c0 + 3 * BLK] += dv
                    dsk = -jnp.exp(_sink_row(sink_ref, h0) - lse4) * delta4
                    for g in range(HPP):
                        h = h0 + g
                        dq_ref[h * HD:(h + 1) * HD, c0:c0 + BLK] = dq4[:, g * BLK:(g + 1) * BLK].astype(BF16)
                        dsacc[h:h + 1, :] += dsk[:, g * BLK:(g + 1) * BLK]

            ahead = [scores(*w) for w in work[:BWD_AHEAD]]
            for n in range(len(work)):
                if n + BWD_AHEAD < len(work):
                    ahead.append(scores(*work[n + BWD_AHEAD]))
                state = ahead.pop(0)
                elementwise(state)
                gradients(state)

        dk_ref[...] = dkacc[:, 0:TQ].astype(BF16)
        dv_ref[...] = dvacc[:, 0:TQ].astype(BF16)

        @pl.when(i == nq)
        def _():
            dsink_ref[...] = jnp.broadcast_to(jnp.sum(dsacc[...], axis=1, keepdims=True), (N_HEADS, LANES))

    clamp = lambda i: jnp.minimum(i, nq - 1)
    row = pl.BlockSpec((TQ, D), lambda i: (clamp(i), 0))
    col = lambda rows: pl.BlockSpec((rows, TQ), lambda i: (0, clamp(i)))
    pad = pl.BlockSpec((KV_W, TQ), lambda i: (0, i))
    return pl.pallas_call(
        body, grid=(nq + 1,), name="l1_attn_bwd",
        out_shape=(jax.ShapeDtypeStruct((D, seq), BF16), jax.ShapeDtypeStruct((D, seq), BF16),
                   jax.ShapeDtypeStruct((KV_W, seq + TQ), BF16), jax.ShapeDtypeStruct((KV_W, seq + TQ), BF16),
                   jax.ShapeDtypeStruct((N_HEADS, LANES), F32)),
        in_specs=[row, _resident((D, D)), col(D), col(D)] + _band_specs_t(nb, clamp) + _band_specs_t(nb, clamp) + [
            col(D), col(N_HEADS), pl.BlockSpec(memory_space=pltpu.SMEM)],
        out_specs=(col(D), col(D), pad, pad, pl.BlockSpec((N_HEADS, LANES), lambda i: (0, 0))),
        scratch_shapes=[pltpu.VMEM((KV_W, TQ + 2 * BLK), BF16), pltpu.VMEM((KV_W, TQ + 2 * BLK), BF16),
                        pltpu.VMEM((KV_W, TQ + 2 * BLK), F32), pltpu.VMEM((KV_W, TQ + 2 * BLK), F32),
                        pltpu.VMEM((D, TQ), BF16), pltpu.VMEM((N_HEADS, TQ), F32), pltpu.VMEM((N_HEADS, LANES), F32)],
        compiler_params=_params(56),
    )(dx2b, wout, qt, gatet, kt, kt, kt, vt, vt, vt, att, lse, sink)


def _l1_in_proj_bwd(dq_r, dk_r, dv, dgate, cos_t, sin_t, w_t, x1, g1, dx2):
    seq = x1.shape[0]
    tm = 512

    def body(dq_ref, dk_ref, dv_ref, dg_ref, c_ref, s_ref, w_ref, x_ref, g_ref, dres_ref,
             dx_ref, dxb_ref, dz_ref, dn_ref):
        @pl.when(pl.program_id(0) == 0)
        def _():
            dn_ref[...] = jnp.zeros_like(dn_ref)

        c, s = c_ref[...], s_ref[...]
        dq = _rope_t(dq_ref[...].astype(F32), c, s, N_HEADS, -1).astype(BF16)
        dk = _rope_t(dk_ref[...].astype(F32), c, s, N_KV, -1).astype(BF16)
        dz = jnp.concatenate([dq, dk, dv_ref[...], dg_ref[...]], axis=0)
        dz_ref[...] = dz
        dh = _tn(dz, w_ref[...])
        xf = x_ref[...]
        r = lax.rsqrt(jnp.mean(xf * xf, axis=1, keepdims=True) + EPS)
        xn = xf * r
        dn_ref[...] += jnp.sum(dh * xn, axis=0, keepdims=True)
        dxn = dh * g_ref[...]
        dx = dres_ref[...] + r * (dxn - xn * jnp.mean(dxn * xn, axis=1, keepdims=True))
        dx_ref[...] = dx
        dxb_ref[...] = dx.astype(BF16)

    row = pl.BlockSpec((tm, D), lambda i: (i, 0))
    col = lambda rows: pl.BlockSpec((rows, tm), lambda i: (0, i))
    return pl.pallas_call(
        body, grid=(seq // tm,), name="l1_in_proj_bwd",
        out_shape=(jax.ShapeDtypeStruct((seq, D), F32), jax.ShapeDtypeStruct((seq, D), BF16),
                   jax.ShapeDtypeStruct((MIX1_IN, seq), BF16), jax.ShapeDtypeStruct((1, D), F32)),
        in_specs=[col(D), col(KV_W), col(KV_W), col(D), col(ROT_HALF), col(ROT_HALF), _resident((MIX1_IN, D)), row,
                  _resident((1, D)), row],
        out_specs=(row, row, col(MIX1_IN), pl.BlockSpec((1, D), lambda i: (0, 0))),
        compiler_params=_params(48),
    )(dq_r, dk_r, dv, dgate, cos_t, sin_t, w_t, x1, g1, dx2)


def _l0_mix_bwd(dx1b, wout, za, bx, bg, ws, ws_t, bias, gv, wg, wg_t, scale):
    seq = dx1b.shape[0]
    ts = 256
    n_tiles = seq // ts

    def body(dx_ref, wo_ref, za_ref, bx_ref, bxp_ref, bxn_ref, bg_ref, ws_ref, wst_ref, bias_ref, gv_ref, wg_ref,
             wgt_ref, sc_ref,
             dz_ref, dp_ref, catt_ref, dws_ref, dbias_ref, dgv_ref, dsc_ref, dwg_ref, db_ref, xe_ref, *tmp_refs):
        i = pl.program_id(0)

        @pl.when(i == 0)
        def _():
            for r_ in (dws_ref, dbias_ref, dgv_ref, dsc_ref, dwg_ref, db_ref):
                r_[...] = jnp.zeros_like(r_)

        dxb = dx_ref[...]
        dya = _nt(dxb, wo_ref[0:D, :])
        dyb = _nt(dxb, wo_ref[D:2 * D, :])

        vg, dvg_dz = _gelu_and_grad(za_ref[:, D:2 * D].astype(F32))
        rv = lax.rsqrt(jnp.mean(vg * vg, axis=1, keepdims=True) + EPS)
        vnorm = vg * rv
        gvw = gv_ref[...]
        vnb = (vnorm * gvw).astype(BF16)
        mixed = _spatial_mix(ws_ref, vnb, bias_ref[...], ts)

        _fill_halo(xe_ref, bx_ref[...], bxp_ref, bxn_ref, i, n_tiles, ts)
        pb = _pool_forward(xe_ref, tmp_refs, ts, i * ts, seq).astype(BF16)
        ypre = jnp.concatenate([_mm(pb[:, g * GDIM:(g + 1) * GDIM], wg_ref[g]) for g in range(4)], axis=1)

        u, du = _gelu_and_grad(za_ref[:, 0:D].astype(F32))
        sga, dsga = _silu_and_grad(za_ref[:, 2 * D:3 * D].astype(F32))
        um = u * mixed
        ya = (um * sga).astype(BF16)
        t = dya * sga
        dz_ref[:, 0:D] = (t * mixed * du).astype(BF16)
        dz_ref[:, 2 * D:3 * D] = (dya * um * dsga).astype(BF16)
        dmixed = t * u
        dmb = dmixed.astype(BF16)
        dvn_rows = []
        for c in range(ts // CHUNK):
            rows = slice(c * CHUNK, (c + 1) * CHUNK)
            parts = []
            for h in range(A_GROUPS):
                cols = slice(h * GDIM, (h + 1) * GDIM)
                dws_ref[h] += _nt(dmb[rows, cols], vnb[rows, cols])
                parts.append(_mm(wst_ref[h], dmb[rows, cols]))
            dvn_rows.append(jnp.concatenate(parts, axis=1))

        sc = sc_ref[...]
        y = ypre * sc
        sgb, dsgb = _silu_and_grad(bg_ref[...].astype(F32))
        yb = (y * sgb).astype(BF16)
        dy_b = dyb * sgb
        dz_ref[:, 3 * D:4 * D] = jnp.zeros((ts, D), BF16)
        dz_ref[:, 4 * D:5 * D] = (dyb * y * dsgb).astype(BF16)
        dsc_ref[...] += jnp.sum(dy_b * ypre, axis=0, keepdims=True)
        dypre = (dy_b * sc).astype(BF16)
        dps = []
        for g in range(4):
            cols = slice(g * GDIM, (g + 1) * GDIM)
            dwg_ref[g] += _tn(pb[:, cols], dypre[:, cols])
            dps.append(_mm(dypre[:, cols], wgt_ref[g]))

        dbias = dmixed[0:CHUNK, :]
        for c in range(1, ts // CHUNK):
            dbias = dbias + dmixed[c * CHUNK:(c + 1) * CHUNK, :]
        dbias_ref[...] += dbias
        dvn = jnp.concatenate(dvn_rows, axis=0)
        dgv_ref[...] += jnp.sum(dvn * vnorm, axis=0, keepdims=True)
        dxn = dvn * gvw
        dvg = rv * (dxn - vnorm * jnp.mean(dxn * vnorm, axis=1, keepdims=True))
        dz_ref[:, D:2 * D] = (dvg * dvg_dz).astype(BF16)

        dp_ref[...] = jnp.concatenate(dps, axis=1)
        catt_ref[...] = jnp.concatenate([ya, yb], axis=1).T

        @pl.when(i == n_tiles - 1)
        def _():
            for h in range(A_GROUPS):
                tot = jnp.sum(dbias_ref[:, h * GDIM:(h + 1) * GDIM].T, axis=0, keepdims=True)
                db_ref[pl.ds(h * 8, 8), :] = jnp.broadcast_to(tot, (8, CHUNK))

    prev, nxt = _halo_specs(ts, seq, D)
    row = lambda w_: pl.BlockSpec((ts, w_), lambda i: (i, 0))
    acc = lambda shape: pl.BlockSpec(shape, lambda i: (0,) * len(shape))
    return pl.pallas_call(
        body, grid=(n_tiles,), name="l0_mix_bwd",
        out_shape=(jax.ShapeDtypeStruct((seq, MIX0_IN), BF16), jax.ShapeDtypeStruct((seq, D), F32),
                   jax.ShapeDtypeStruct((2 * D, seq), BF16),
                   jax.ShapeDtypeStruct((4, CHUNK, CHUNK), F32), jax.ShapeDtypeStruct((CHUNK, D), F32),
                   jax.ShapeDtypeStruct((1, D), F32), jax.ShapeDtypeStruct((1, D), F32),
                   jax.ShapeDtypeStruct((4, GDIM, GDIM), F32), jax.ShapeDtypeStruct((32, CHUNK), F32)),
        in_specs=[row(D), _resident((2 * D, D)), row(3 * D), row(D), prev, nxt, row(D), _resident((4, CHUNK, CHUNK)),
                  _resident((4, CHUNK, CHUNK)), _resident((CHUNK, D)), _resident((1, D)), _resident((4, GDIM, GDIM)),
                  _resident((4, GDIM, GDIM)), _resident((1, D))],
        out_specs=(row(MIX0_IN), row(D), pl.BlockSpec((2 * D, ts), lambda i: (0, i)),
                   acc((4, CHUNK, CHUNK)), acc((CHUNK, D)), acc((1, D)), acc((1, D)), acc((4, GDIM, GDIM)),
                   acc((32, CHUNK))),
        scratch_shapes=_pool_scratch(ts),
        compiler_params=_params(56),
    )(dx1b, wout, za, bx, bx, bx, bg, ws, ws_t, bias, gv, wg, wg_t, scale)


def _l0_pool_bwd(dp, dz):
    seq = dp.shape[0]
    ts = 512
    n_tiles = seq // ts
    ext = ts + 2 * POOL_HALO

    def body(dp_ref, dpp_ref, dpn_ref, dz_ref, out_ref, qe_ref, *tmp_refs):
        i = pl.program_id(0)
        _fill_halo(qe_ref, dp_ref[...], dpp_ref, dpn_ref, i, n_tiles, ts)
        te = i * ts - POOL_HALO + lax.broadcasted_iota(jnp.int32, (ext, 1), 0)
        for gi, w in enumerate(POOL_WINDOWS):
            hw = w // 2
            cols = slice(gi * GDIM, (gi + 1) * GDIM)
            cnt = jnp.maximum(jnp.minimum(te + hw, seq) - jnp.maximum(te - hw, 0), 1).astype(F32)
            qe_ref[pl.ds(0, ext), cols] = qe_ref[pl.ds(0, ext), cols] / cnt
        outs = []
        for gi, w in enumerate(POOL_WINDOWS):
            cols = slice(gi * GDIM, (gi + 1) * GDIM)
            outs.append(_window_sums(qe_ref, tmp_refs, ts, cols, w, 1) - dp_ref[:, cols])
        out_ref[...] = jnp.concatenate(outs, axis=1).astype(BF16)

    prev, nxt = _halo_specs(ts, seq, D)
    row = pl.BlockSpec((ts, D), lambda i: (i, 0))
    return pl.pallas_call(
        body, grid=(n_tiles,), name="l0_pool_bwd",
        out_shape=jax.ShapeDtypeStruct(dz.shape, BF16),
        in_specs=[row, prev, nxt, pl.BlockSpec(memory_space=pl.ANY)],
        out_specs=pl.BlockSpec((ts, D), lambda i: (i, 3)),
        input_output_aliases={3: 0},
        scratch_shapes=_pool_scratch(ts),
        compiler_params=_params(32),
    )(dp, dp, dp, dz)


def _l0_in_proj_bwd(dz, w, x, g0, dx1):
    seq = x.shape[0]
    tm = 512

    def body(dz_ref, w_ref, x_ref, g_ref, dres_ref, dx_ref, dn_ref):
        @pl.when(pl.program_id(0) == 0)
        def _():
            dn_ref[...] = jnp.zeros_like(dn_ref)

        dh = _nt(dz_ref[...], w_ref[...])
        xf = x_ref[...]
        r = lax.rsqrt(jnp.mean(xf * xf, axis=1, keepdims=True) + EPS)
        xn = xf * r
        dn_ref[...] += jnp.sum(dh * xn, axis=0, keepdims=True)
        dxn = dh * g_ref[...]
        dx_ref[...] = dres_ref[...] + r * (dxn - xn * jnp.mean(dxn * xn, axis=1, keepdims=True))

    row = lambda w_: pl.BlockSpec((tm, w_), lambda i: (i, 0))
    return pl.pallas_call(
        body, grid=(seq // tm,), name="l0_in_proj_bwd",
        out_shape=(jax.ShapeDtypeStruct((seq, D), F32), jax.ShapeDtypeStruct((1, D), F32)),
        in_specs=[row(MIX0_IN), _resident((D, MIX0_IN)), row(D), _resident((1, D)), row(D)],
        out_specs=(row(D), pl.BlockSpec((1, D), lambda i: (0, 0))),
        compiler_params=_params(56),
    )(dz, w, x, g0, dx1)


def _dw_matmul(a_t, b, name, b_transposed=False, tn=1024, ts=1024, col_block=None):
    k, seq = a_t.shape
    n = b.shape[0] if b_transposed else b.shape[1]
    tn = min(n, tn)
    assert seq % ts == 0 and n % tn == 0 and (col_block is None or tn % col_block == 0)
    n_s = seq // ts
    per = 1 if col_block is None else tn // col_block

    def body(a_ref, b_ref, o_ref, ob_ref, acc_ref):
        s = pl.program_id(1)

        @pl.when(s == 0)
        def _():
            acc_ref[...] = jnp.zeros_like(acc_ref)

        acc_ref[...] += _nt(a_ref[...], b_ref[...]) if b_transposed else _mm(a_ref[...], b_ref[...])

        @pl.when(s == n_s - 1)
        def _():
            acc = acc_ref[...]
            if col_block is None:
                o_ref[...] = acc
                ob_ref[...] = acc.astype(BF16)
            else:
                for i in range(per):
                    piece = acc[:, i * col_block:(i + 1) * col_block]
                    o_ref[i] = piece
                    ob_ref[i] = piece.astype(BF16)

    b_spec = (pl.BlockSpec((tn, ts), lambda j, s: (j, s)) if b_transposed else pl.BlockSpec((ts, tn), lambda j, s: (s, j)))
    if col_block is None:
        shape, o_spec = (k, n), pl.BlockSpec((k, tn), lambda j, s: (0, j))
    else:
        shape, o_spec = (n // col_block, k, col_block), pl.BlockSpec((per, k, col_block), lambda j, s: (j, 0, 0))
    return pl.pallas_call(
        body, grid=(n // tn, n_s), name=name,
        out_shape=(jax.ShapeDtypeStruct(shape, F32), jax.ShapeDtypeStruct(shape, BF16)),
        in_specs=[pl.BlockSpec((k, ts), lambda j, s: (0, s)), b_spec],
        out_specs=(o_spec, o_spec),
        scratch_shapes=[pltpu.VMEM((k, tn), F32)],
        compiler_params=_params(56, 2),
    )(a_t, b)


ROW_TILES = 8


def _cast_shards(shards):
    n = len(shards)

    def body(*refs):
        for a in range(n):
            refs[n + a][...] = refs[a][...].astype(BF16)

    vm = pl.BlockSpec(memory_space=pltpu.VMEM)
    return pl.pallas_call(body, name="cast_weights", out_shape=[jax.ShapeDtypeStruct(t.shape, BF16) for t in shards],
                          in_specs=[vm] * n, out_specs=[vm] * n, compiler_params=_params(32, 0))(*shards)


def _adamw_math(w, g, m, v):
    m2 = ADAM_B1 * m + (1.0 - ADAM_B1) * g
    v2 = ADAM_B2 * v + (1.0 - ADAM_B2) * (g * g)
    m_hat = m2 / (1.0 - ADAM_B1 ** ADAM_STEP)
    v_hat = v2 / (1.0 - ADAM_B2 ** ADAM_STEP)
    delta = -ADAM_LR * (m_hat / (jnp.sqrt(v_hat) + ADAM_EPS) + ADAM_WD * w)
    return delta, m2, v2


def _final_sum_adamw(g_list, recv_list, me, w_list, m_list, v_list):
    n = len(w_list)

    def body(me_ref, *refs):
        own, recv, w, m, v = (refs[k * n:(k + 1) * n] for k in range(5))
        outs = [refs[(5 + k) * n:(6 + k) * n] for k in range(4)]
        for a in range(n):
            g = own[a][...]
            for k in range(N_DEV - 1):
                g = g + recv[a][k].astype(F32)
            delta, m2, v2 = _adamw_math(w[a][...], g, m[a][...], v[a][...])
            for o_ref, val in zip((outs[0][a], outs[1][a], outs[2][a], outs[3][a]), (g, delta, m2, v2)):
                o_ref[...] = val

    own_specs, flat, wire, shapes = [], [], [], []
    for t in w_list:
        rows, width = t.shape
        tr = rows // ROW_TILES
        own_specs.append(pl.BlockSpec((None, tr, width), lambda i, me: (me[0], i, 0)))
        flat.append(pl.BlockSpec((tr, width), lambda i, me: (i, 0)))
        wire.append(pl.BlockSpec((N_DEV - 1, tr, width), lambda i, me: (0, i, 0)))
        shapes.append(jax.ShapeDtypeStruct((rows, width), F32))
    out = pl.pallas_call(
        body, name="grad_sum_adamw", out_shape=shapes * 4,
        grid_spec=pltpu.PrefetchScalarGridSpec(
            num_scalar_prefetch=1, grid=(ROW_TILES,), in_specs=own_specs + wire + flat * 3, out_specs=flat * 4),
        compiler_params=_params(40),
    )(me, *g_list, *recv_list, *w_list, *m_list, *v_list)
    return [out[k * n:(k + 1) * n] for k in range(4)]


SMALL_NAMES = ("norm_0", "a_v_norm_0", "b_scale_0", "norm_1", "final_norm", "a_spatial_w_0", "a_spatial_b_0", "sink_1")
SMALL_VIEWS = ((8, LANES),) * 5 + ((4 * CHUNK, LANES), (4, LANES), (1, N_HEADS))
SMALL_ROW0 = (0, 8, 16, 24, 32, 40, 552, 560)
SMALL_ROWS = 568


def _small_sum_adamw(early, late, w_list, m_list, v_list):
    n = len(w_list)

    def body(e_ref, l_ref, *refs):
        gtot, first = e_ref[0], l_ref[0]
        for d in range(1, N_DEV):
            gtot = gtot + e_ref[d]
            first = first + l_ref[d]
        for a, ((rows, width), r0) in enumerate(zip(SMALL_VIEWS, SMALL_ROW0)):
            g = first if SMALL_NAMES[a] == "norm_0" else gtot[r0:r0 + rows, 0:width]
            delta, m2, v2 = _adamw_math(refs[a][...], g, refs[n + a][...], refs[2 * n + a][...])
            for k, val in enumerate((g, delta, m2, v2)):
                refs[(3 + k) * n + a][...] = val
        refs[7 * n][...] = gtot[LOSS_ROW:LOSS_ROW + 1, LOSS_LANE:LOSS_LANE + 1]

    vm = pl.BlockSpec(memory_space=pltpu.VMEM)
    shapes = [jax.ShapeDtypeStruct(s, F32) for s in SMALL_VIEWS]
    out = pl.pallas_call(
        body, name="small_sum_adamw", out_shape=shapes * 4 + [jax.ShapeDtypeStruct((1, 1), F32)],
        in_specs=[vm, vm] + [vm] * (3 * n), out_specs=[vm] * (4 * n + 1),
    )(early, late, *w_list, *m_list, *v_list)
    return [out[k * n:(k + 1) * n] for k in range(4)], out[4 * n]


PEER_FLIPS = tuple((fx, fy, fc) for fx in (0, 1) for fy in (0, 1) for fc in (0, 1))[1:]


def _sequencer_all_gather(blks, name, collective_id, concat_rows=False):
    n = len(blks)

    def body(*refs):
        ins, outs = refs[:n], refs[n:2 * n]
        send_sems, recv_sems, local_sems = refs[2 * n:]
        x, y, c = lax.axis_index("x"), lax.axis_index("y"), lax.axis_index("c")
        peers = [(x ^ fx, y ^ fy, c ^ fc) for fx, fy, fc in PEER_FLIPS]
        barrier = pltpu.get_barrier_semaphore()
        for peer in peers:
            pl.semaphore_signal(barrier, inc=1, device_id=peer, device_id_type=MESH)
        pl.semaphore_wait(barrier, len(peers))
        me = 4 * x + 2 * y + c

        def slot(a):
            rows = blks[a].shape[0]
            return outs[a].at[pl.ds(pl.multiple_of(me * rows, 16), rows)] if concat_rows else outs[a].at[me]

        copies = [pltpu.make_async_remote_copy(
            src_ref=ins[a], dst_ref=slot(a), send_sem=send_sems.at[k, a], recv_sem=recv_sems.at[k, a],
            device_id=peer, device_id_type=MESH) for k, peer in enumerate(peers) for a in range(n)]
        mine = [pltpu.make_async_copy(ins[a], slot(a), local_sems.at[a]) for a in range(n)]
        for cp in copies + mine:
            cp.start()
        for cp in copies + mine:
            cp.wait()

    out_shape = (lambda t: (N_DEV * t.shape[0],) + t.shape[1:]) if concat_rows else (lambda t: (N_DEV,) + t.shape)
    return pl.kernel(
        body, out_type=[jax.ShapeDtypeStruct(out_shape(t), t.dtype) for t in blks],
        mesh=plsc.ScalarSubcoreMesh(axis_name="sequencer", num_cores=1), name=name,
        scratch_types=[pltpu.SemaphoreType.DMA((7, n)), pltpu.SemaphoreType.DMA((7, n)), pltpu.SemaphoreType.DMA((n,))],
        compiler_params=pltpu.CompilerParams(collective_id=collective_id),
    )(*blks)


def _sequencer_scatter(g_list, name, collective_id):
    n = len(g_list)

    def body(*refs):
        ins, outs = refs[:n], refs[n:2 * n]
        send_sems, recv_sems = refs[2 * n:]
        x, y, c = lax.axis_index("x"), lax.axis_index("y"), lax.axis_index("c")
        peers = [(x ^ fx, y ^ fy, c ^ fc) for fx, fy, fc in PEER_FLIPS]
        barrier = pltpu.get_barrier_semaphore()
        for peer in peers:
            pl.semaphore_signal(barrier, inc=1, device_id=peer, device_id_type=MESH)
        pl.semaphore_wait(barrier, len(peers))
        copies = [pltpu.make_async_remote_copy(
            src_ref=ins[a].at[4 * px + 2 * py + pc], dst_ref=outs[a].at[k], send_sem=send_sems.at[k, a],
            recv_sem=recv_sems.at[k, a], device_id=(px, py, pc), device_id_type=MESH)
            for k, (px, py, pc) in enumerate(peers) for a in range(n)]
        for cp in copies:
            cp.start()
        for cp in copies:
            cp.wait()

    return pl.kernel(
        body, out_type=[jax.ShapeDtypeStruct((N_DEV - 1,) + g.shape[1:], g.dtype) for g in g_list],
        mesh=plsc.ScalarSubcoreMesh(axis_name="sequencer", num_cores=1), name=name,
        scratch_types=[pltpu.SemaphoreType.DMA((7, n)), pltpu.SemaphoreType.DMA((7, n))],
        compiler_params=pltpu.CompilerParams(collective_id=collective_id),
    )(*g_list)


def _direct_all_gather(blk, name):
    def body(g_ref, out_ref, send_sems, recv_sems, local_sem):
        x, y, c = lax.axis_index("x"), lax.axis_index("y"), lax.axis_index("c")
        me = 4 * x + 2 * y + c
        copies = [pltpu.make_async_remote_copy(
            src_ref=g_ref, dst_ref=out_ref.at[me], send_sem=send_sems.at[k], recv_sem=recv_sems.at[k],
            device_id=(x ^ fx, y ^ fy, c ^ fc), device_id_type=MESH) for k, (fx, fy, fc) in enumerate(PEER_FLIPS)]
        copies.append(pltpu.make_async_copy(g_ref, out_ref.at[me], local_sem))
        for cp in copies:
            cp.start()
        for cp in copies:
            cp.wait()

    any_spec = pl.BlockSpec(memory_space=pl.ANY)
    return pl.pallas_call(
        body, name=name, out_shape=jax.ShapeDtypeStruct((N_DEV,) + blk.shape, blk.dtype),
        in_specs=[any_spec], out_specs=any_spec,
        scratch_shapes=[pltpu.SemaphoreType.DMA((7,)), pltpu.SemaphoreType.DMA((7,)), pltpu.SemaphoreType.DMA],
    )(blk)


def _shard_views(w_in_0, b_group_w_0, w_out_0, w_in_1, w_out_1):
    return [w_in_0, b_group_w_0.reshape(4 * 32, GDIM), w_out_0, w_in_1, w_out_1]


def _small_views(named):
    return [named[name].reshape(view) for name, view in zip(SMALL_NAMES, SMALL_VIEWS)]


LOSS_ROW, LOSS_LANE = 560, N_HEADS


def _pack_small_grads(named, loss_part):
    rows = []
    for name, (r, w) in zip(SMALL_NAMES, SMALL_VIEWS):
        pad_r = -r % 8
        if name == "sink_1":
            t = jnp.concatenate([named[name].reshape(r, w), loss_part], axis=1)
            rows.append(jnp.pad(t, ((0, pad_r), (0, LANES - w - 1))))
        elif name in named:
            rows.append(jnp.pad(named[name].reshape(r, w), ((0, pad_r), (0, LANES - w))))
        else:
            rows.append(jnp.zeros((r + pad_r, LANES), F32))
    return jnp.concatenate(rows, axis=0)


def _device_blocks(t, axis):
    shape = t.shape
    t = t.reshape(shape[:axis] + (N_DEV, shape[axis] // N_DEV) + shape[axis + 1:])
    t = jnp.moveaxis(t, axis, 0)
    return t.reshape(N_DEV, -1, shape[-1] if axis != len(shape) - 1 else shape[-1] // N_DEV)


def kernel(x, norm_0, w_in_0, a_v_norm_0, a_spatial_w_0, a_spatial_b_0, b_group_w_0, b_scale_0, w_out_0, norm_1, w_in_1, sink_1, w_out_1, final_norm, loss_target, m_norm_0, m_w_in_0, m_a_v_norm_0, m_a_spatial_w_0, m_a_spatial_b_0, m_b_group_w_0, m_b_scale_0, m_w_out_0, m_norm_1, m_w_in_1, m_sink_1, m_w_out_1, m_final_norm, v_norm_0, v_w_in_0, v_a_v_norm_0, v_a_spatial_w_0, v_a_spatial_b_0, v_b_group_w_0, v_b_scale_0, v_w_out_0, v_norm_1, v_w_in_1, v_sink_1, v_w_out_1, v_final_norm):
    seq = x.shape[1]
    xs = x.reshape(seq, D)
    tgt = loss_target.reshape(seq, D)
    ax, ay, ac = lax.axis_index("x"), lax.axis_index("y"), lax.axis_index("c")
    me = jnp.reshape(4 * ax + 2 * ay + ac, (1,)).astype(jnp.int32)

    shards = _shard_views(w_in_0, b_group_w_0, w_out_0, w_in_1, w_out_1)
    cast = _cast_shards([shards[0], shards[1], shards[2], w_in_1.T, shards[4]])

    def l1_weights(after):
        blks, _ = lax.optimization_barrier((cast[3:5], after))
        return _sequencer_all_gather(blks, "weights_gather_l1", 2, concat_rows=True)

    blocks, received, early = {}, {}, {}
    collective_ids = {"l1": 3, "out0": 4, "in0": 5}

    def scatter(tag, own_blocks, wire_blocks):
        blocks[tag] = own_blocks
        received[tag] = _sequencer_scatter(wire_blocks, "grad_scatter_" + tag, collective_ids[tag])

    def small_early(named, loss_part):
        early["small"] = _sequencer_all_gather([_pack_small_grads(named, loss_part)], "small_grad_gather", 6)[0]

    grad_x, d_norm_0 = _local_step(xs, tgt, cast[0], cast[1:3], l1_weights, norm_0, a_v_norm_0, a_spatial_w_0,
                                   a_spatial_b_0, b_scale_0, norm_1, sink_1, final_norm, scatter, small_early)

    order = (("in0", 0), ("in0", 1), ("out0", 0), ("l1", 0), ("l1", 1))
    late = _direct_all_gather(d_norm_0.reshape(8, LANES), "norm_grad_gather")
    shards_late, _ = lax.optimization_barrier((shards, grad_x))
    big = _final_sum_adamw([blocks[t][i] for t, i in order], [received[t][i] for t, i in order], me, shards_late,
                           _shard_views(m_w_in_0, m_b_group_w_0, m_w_out_0, m_w_in_1, m_w_out_1),
                           _shard_views(v_w_in_0, v_b_group_w_0, v_w_out_0, v_w_in_1, v_w_out_1))
    weights = dict(norm_0=norm_0, a_v_norm_0=a_v_norm_0, a_spatial_w_0=a_spatial_w_0, a_spatial_b_0=a_spatial_b_0,
                   b_scale_0=b_scale_0, norm_1=norm_1, sink_1=sink_1, final_norm=final_norm)
    m_small = dict(norm_0=m_norm_0, a_v_norm_0=m_a_v_norm_0, a_spatial_w_0=m_a_spatial_w_0, a_spatial_b_0=m_a_spatial_b_0,
                   b_scale_0=m_b_scale_0, norm_1=m_norm_1, sink_1=m_sink_1, final_norm=m_final_norm)
    v_small = dict(norm_0=v_norm_0, a_v_norm_0=v_a_v_norm_0, a_spatial_w_0=v_a_spatial_w_0, a_spatial_b_0=v_a_spatial_b_0,
                   b_scale_0=v_b_scale_0, norm_1=v_norm_1, sink_1=v_sink_1, final_norm=v_final_norm)
    small, loss = _small_sum_adamw(early["small"], late, _small_views(weights), _small_views(m_small),
                                   _small_views(v_small))

    def in_order(kind):
        b = [b_.reshape(s_.shape) for b_, s_ in zip(big[kind], (w_in_0, b_group_w_0, w_out_0, w_in_1, w_out_1))]
        s = {name: t.reshape(weights[name].shape) for name, t in zip(SMALL_NAMES, small[kind])}
        return [s["norm_0"], b[0], s["a_v_norm_0"], s["a_spatial_w_0"], s["a_spatial_b_0"], b[1], s["b_scale_0"], b[2],
                s["norm_1"], b[3], s["sink_1"], b[4], s["final_norm"]]

    return (loss[0, 0], grad_x.reshape(1, seq, D), *in_order(0), *in_order(1), *in_order(2), *in_order(3))


def _local_step(xs, tgt, win0_shard, l0_shards, l1_weights, norm_0, a_v_norm_0, a_spatial_w_0, a_spatial_b_0, b_scale_0,
                norm_1, sink_1, final_norm, scatter, small_early):
    seq = xs.shape[0]
    ws = a_spatial_w_0.astype(BF16)
    ws_t = jnp.swapaxes(ws, 1, 2)
    bias = jnp.repeat(a_spatial_b_0.T, GDIM, axis=1)
    g0, gv, scale, g1, gf = (t.reshape(1, D) for t in (norm_0, a_v_norm_0, b_scale_0, norm_1, final_norm))
    cos_t, sin_t = _rope_tables_t(seq)

    za, bx, bg, h0_t, win0, g_wg, wout0 = _l0_in_proj(xs, g0, win0_shard, l0_shards)
    win1_t, wout1 = l1_weights(za)
    wg = g_wg.reshape(N_DEV, 4, 32, GDIM).transpose(1, 0, 2, 3).reshape(4, GDIM, GDIM)
    wg_t = jnp.swapaxes(wg, 1, 2)
    x1 = _l0_mix_fwd(za, bx, bg, xs, ws, bias, gv, wg, scale, wout0)
    win1_t, wout1, x1 = lax.optimization_barrier((win1_t, wout1, x1))
    qt, kt, vt, gatet, h1_t = _l1_in_proj(x1, g1, win1_t, cos_t, sin_t)
    dx2, dx2b, att, lse, loss_part, d_gf, d_wout1, d_wout1_wire = _l1_attn_fwd(
        qt, kt, vt, gatet, x1, tgt, wout1, gf, sink_1)

    dq_r, dgate, dk_pad, dv_pad, d_sink = _l1_attn_bwd(dx2b, wout1, qt, kt, vt, gatet, att, lse, sink_1)
    dk_r = dk_pad[:, BLK:BLK + seq]
    dv = dv_pad[:, BLK:BLK + seq]
    dx1, dx1b, dz1_t, d_g1 = _l1_in_proj_bwd(dq_r, dk_r, dv, dgate, cos_t, sin_t, win1_t, x1, g1, dx2)
    d_win1, d_win1_wire = _dw_matmul(h1_t, dz1_t, "dw_in_1", b_transposed=True, tn=1280, col_block=MIX1_IN // N_DEV)
    rows = lambda t: t.reshape(N_DEV, t.shape[0] // N_DEV, t.shape[1])
    scatter("l1", [d_win1, rows(d_wout1)], [d_win1_wire, rows(d_wout1_wire)])

    dz0, dp, cat_t, d_ws, _, d_gv, d_scale, d_wg, d_b = _l0_mix_bwd(
        dx1b, wout0, za, bx, bg, ws, ws_t, bias, gv, wg, wg_t, scale)
    dz0 = _l0_pool_bwd(dp, dz0)
    d_win0, d_win0_wire = _dw_matmul(h0_t, dz0, "dw_in_0", tn=1280, col_block=MIX0_IN // N_DEV)
    d_wg_blocks = _device_blocks(d_wg, 1)
    scatter("in0", [d_win0, d_wg_blocks], [d_win0_wire, d_wg_blocks])
    cat_t, _ = lax.optimization_barrier((cat_t, d_win0))
    d_wout0, d_wout0_wire = _dw_matmul(cat_t, dx1b, "dw_out_0")
    scatter("out0", [rows(d_wout0)], [rows(d_wout0_wire)])
    small_early(dict(a_v_norm_0=d_gv, a_spatial_w_0=d_ws, a_spatial_b_0=d_b.reshape(4, 8, CHUNK)[:, 0, :],
                     b_scale_0=d_scale, norm_1=d_g1, sink_1=d_sink[:, 0], final_norm=d_gf), loss_part)
    dz0, _ = lax.optimization_barrier((dz0, d_wout0))
    return _l0_in_proj_bwd(dz0, win0, xs, g0, dx1)
```

```python
import jax
import jax.numpy as jnp
from jax import lax
from jax.experimental import pallas as pl
from jax.experimental.pallas import tpu as pltpu
from jax.experimental.pallas import tpu_sc as plsc

F32 = jnp.float32
BF16 = jnp.bfloat16

D = 1024
EPS = 1e-6
NEG_INF = -1e30
CHUNK = 128
A_GROUPS = 4
POOL_WINDOWS = (2, 4, 8, 16)
POOL_HALO = 8
GDIM = 256
N_HEADS = 16
N_KV = 4
GQA = 4
HD = 64
BLK = 128
ROT_HALF = 8
ROPE_THETA = 500000.0
SCALE = HD ** -0.5
MIX0_IN = 5 * D
MIX1_IN = 2560
KV_W = N_KV * HD
Q_ROWS, K_ROWS, V_ROWS, G_ROWS = (0, D), (D, D + KV_W), (D + KV_W, D + 2 * KV_W), (D + 2 * KV_W, MIX1_IN)
TQ = 512

ADAM_LR = 0.001
ADAM_B1 = 0.9
ADAM_B2 = 0.999
ADAM_EPS = 1e-08
ADAM_WD = 0.01
ADAM_STEP = 10

N_DEV = 8
LANES = 128
MIB = 2 ** 20
MESH = pl.DeviceIdType.MESH


def _params(limit_mib, n_axes=1):
    return pltpu.CompilerParams(vmem_limit_bytes=limit_mib * MIB, dimension_semantics=("arbitrary",) * n_axes)


def _resident(shape):
    nd = len(shape)
    return pl.BlockSpec(shape, lambda *_: (0,) * nd, pipeline_mode=pl.Buffered(1))


def _whole(shape):
    nd = len(shape)
    return pl.BlockSpec(shape, lambda *_: (0,) * nd)


def _gelu(x):
    k = 0.7978845608028654
    return 0.5 * x * (1.0 + jnp.tanh(k * (x + 0.044715 * x * x * x)))


def _gelu_and_grad(x):
    k = 0.7978845608028654
    x2 = x * x
    t = jnp.tanh(k * (x + 0.044715 * x * x2))
    g = 0.5 * x * (1.0 + t)
    dg = 0.5 * (1.0 + t) + 0.5 * x * (1.0 - t * t) * (k * (1.0 + 3.0 * 0.044715 * x2))
    return g, dg


def _silu_and_grad(x):
    s = jax.nn.sigmoid(x)
    return x * s, s * (1.0 + x * (1.0 - s))


def _nt(a, b):
    return lax.dot_general(a, b, (((1,), (1,)), ((), ())), preferred_element_type=F32)


def _tn(a, b):
    return lax.dot_general(a, b, (((0,), (0,)), ((), ())), preferred_element_type=F32)


def _mm(a, b):
    return jnp.dot(a, b, preferred_element_type=F32)


def _rope_tables_t(seq):
    inv = ROPE_THETA ** (-jnp.arange(0, 2 * ROT_HALF, 2, dtype=F32) / (2 * ROT_HALF))
    ang = inv[:, None] * jnp.arange(seq, dtype=F32)[None, :]
    return jnp.cos(ang), jnp.sin(ang)


def _rope_t(z, c, s, n_heads, sign):
    parts = []
    for h in range(n_heads):
        b = h * HD
        x1, x2 = z[b:b + ROT_HALF], z[b + ROT_HALF:b + 2 * ROT_HALF]
        if sign > 0:
            parts += [x1 * c - x2 * s, x2 * c + x1 * s]
        else:
            parts += [x1 * c + x2 * s, x2 * c - x1 * s]
        parts.append(z[b + 2 * ROT_HALF:b + HD])
    return jnp.concatenate(parts, axis=0)


N_CHIPS = 4
CHIP_COLS = MIX0_IN // N_CHIPS
IN_PROJ_PIECES = (
    ((0, 0, CHIP_COLS, 0),),
    ((0, CHIP_COLS, CHIP_COLS, 0),),
    ((0, 2 * CHIP_COLS, 3 * D - 2 * CHIP_COLS, 0), (1, 0, 3 * CHIP_COLS - 3 * D, 3 * D - 2 * CHIP_COLS)),
    ((1, 3 * CHIP_COLS - 3 * D, 4 * D - 3 * CHIP_COLS, 0), (2, 0, D, 4 * D - 3 * CHIP_COLS)),
)


def _l0_in_proj(x, g0, w_shard, later_shards):
    seq = x.shape[0]
    tm = 512
    n = seq // tm
    shard_cols = w_shard.shape[1]
    n_arr = 1 + len(later_shards)
    later = range(1, n_arr)
    assert 2 * shard_cols == CHIP_COLS and seq % tm == 0 and n >= 4

    def body(*refs):
        x_ref, g_ref = refs[:2]
        ins = refs[2:2 + n_arr]
        za_ref, bx_ref, bg_ref, ht_ref = refs[2 + n_arr:6 + n_arr]
        gathered = refs[6 + n_arr:6 + 2 * n_arr]
        h_all, w_buf, z32, z16, send_sems, recv_sems, local_sems, load_sems, out_sems = refs[6 + 2 * n_arr:]
        p, i = pl.program_id(0), pl.program_id(1)
        ax, ay, ac = lax.axis_index("x"), lax.axis_index("y"), lax.axis_index("c")
        me, sibling = (ax, ay, ac), (ax, ay, 1 - ac)
        chips = [(ax, ay), (1 - ax, ay), (ax, 1 - ay), (1 - ax, 1 - ay)]
        outs = (za_ref, bx_ref, bg_ref)

        def slot(a, px, py, pc):
            dev = 4 * px + 2 * py + pc
            if a == 0:
                return gathered[0].at[:, pl.ds(pl.multiple_of(dev * shard_cols, LANES), shard_cols)]
            rows = later_shards[a - 1].shape[0]
            return gathered[a].at[pl.ds(pl.multiple_of(dev * rows, 16), rows)]

        def copy(k, a, block, to, from_input=False):
            return pltpu.make_async_remote_copy(
                src_ref=ins[a] if from_input else slot(a, *block), dst_ref=slot(a, *block),
                send_sem=send_sems.at[k, a], recv_sem=recv_sems.at[k, a], device_id=to, device_id_type=MESH)

        def to_sibling(a):
            return copy(0, a, me, sibling, from_input=True)

        def send(j, a):
            return copy(j, a, me, (*chips[j], ac), from_input=True)

        def landed(j, a):
            return copy(j, a, (*chips[j], ac), me)

        def forward(j, a):
            return copy(3 + j, a, (*chips[j], ac), sibling)

        def forwarded(j, a):
            return copy(3 + j, a, (*chips[j], 1 - ac), me)

        def mine(a):
            return pltpu.make_async_copy(ins[a], slot(a, *me), local_sems.at[a])

        def load(chip, q):
            px, py = chip
            cols = pl.ds(pl.multiple_of((2 * px + py) * CHIP_COLS, LANES), CHIP_COLS)
            return pltpu.make_async_copy(gathered[0].at[:, cols], w_buf.at[q % 2], load_sems.at[q % 2])

        def out_copies(q, tile, stage):
            cps = []
            for k, (o, c0, width, z0) in enumerate(IN_PROJ_PIECES[q]):
                src = z32.at[stage, :, pl.ds(z0, width)] if o == 1 else z16.at[stage, :, pl.ds(z0, width)]
                dst = outs[o].at[pl.ds(pl.multiple_of(tile * tm, tm), tm), pl.ds(c0, width)]
                cps.append(pltpu.make_async_copy(src, dst, out_sems.at[stage, k]))
            return cps

        @pl.when((p == 0) & (i == 0))
        def _():
            for a in range(n_arr):
                mine(a).start()
                to_sibling(a).start()
            send(1, 0).start()
            send(2, 0).start()
            copy(0, 0, sibling, me).wait_recv()
            mine(0).wait()
            load(chips[0], 0).start()

        for j in range(1, N_CHIPS):
            @pl.when((p == j - 1) & (i == n - 2))
            def _(j=j):
                landed(j, 0).wait_recv()
                forward(j, 0).start()
                forwarded(j, 0).wait_recv()
                if j == 1:
                    send(1, 0).wait_send()
                    send(2, 0).wait_send()
                    send(3, 0).start()
                    for a in later:
                        for jj in range(1, N_CHIPS):
                            send(jj, a).start()
                load(chips[j], j).start()

        @pl.when((p == N_CHIPS - 1) & (i == n - 4))
        def _():
            for jj in range(1, N_CHIPS):
                for a in later:
                    landed(jj, a).wait_recv()
                    forward(jj, a).start()

        @pl.when(i == 0)
        def _():
            load(chips[0], p).wait()

        @pl.when(p == 0)
        def _():
            xf = x_ref[...]
            r = lax.rsqrt(jnp.mean(xf * xf, axis=1, keepdims=True) + EPS)
            h = (xf * r * g_ref[...]).astype(BF16)
            ht_ref[...] = h.T
            h_all[pl.ds(pl.multiple_of(i * tm, tm), tm), :] = h

        def chip_of_pass(pp):
            return (2 * ax + ay) ^ ((pp >> 1) | ((pp & 1) << 1))

        step = p * n + i
        stage = step % 2
        for q in range(N_CHIPS):
            @pl.when((step >= 2) & (chip_of_pass((step - 2) // n) == q))
            def _(q=q):
                for cp in out_copies(q, (step - 2) % n, stage):
                    cp.wait()

        z32[stage] = _mm(h_all[pl.ds(pl.multiple_of(i * tm, tm), tm), :], w_buf[p % 2])
        z16[stage] = z32[stage].astype(BF16)
        for q in range(N_CHIPS):
            @pl.when(chip_of_pass(p) == q)
            def _(q=q):
                for cp in out_copies(q, i, stage):
                    cp.start()

        last = (p == N_CHIPS - 1) & (i == n - 1)
        for q in range(N_CHIPS):
            @pl.when(last & (chip_of_pass(p) == q))
            def _(q=q):
                for cp in out_copies(q, n - 2, 1 - stage) + out_copies(q, n - 1, stage):
                    cp.wait()

        @pl.when(last)
        def _():
            for a in later:
                copy(0, a, sibling, me).wait_recv()
                for jj in range(1, N_CHIPS):
                    forwarded(jj, a).wait_recv()
                    send(jj, a).wait_send()
                mine(a).wait()
            send(3, 0).wait_send()
            for a in range(n_arr):
                to_sibling(a).wait_send()
                for jj in range(1, N_CHIPS):
                    forward(jj, a).wait_send()

    any_spec = pl.BlockSpec(memory_space=pl.ANY)
    first_pass_tile = lambda p, i: jnp.where(p == 0, i, n - 1)
    return pl.pallas_call(
        body, grid=(N_CHIPS, n), name="l0_in_proj",
        out_shape=[jax.ShapeDtypeStruct((seq, 3 * D), BF16), jax.ShapeDtypeStruct((seq, D), F32),
                   jax.ShapeDtypeStruct((seq, D), BF16), jax.ShapeDtypeStruct((D, seq), BF16),
                   jax.ShapeDtypeStruct((D, MIX0_IN), BF16)]
        + [jax.ShapeDtypeStruct((N_DEV * t.shape[0], t.shape[1]), t.dtype) for t in later_shards],
        in_specs=[pl.BlockSpec((tm, D), lambda p, i: (first_pass_tile(p, i), 0)), _resident((1, D))] + [any_spec] * n_arr,
        out_specs=[any_spec, any_spec, any_spec, pl.BlockSpec((D, tm), lambda p, i: (0, first_pass_tile(p, i)))]
        + [any_spec] * n_arr,
        scratch_shapes=[pltpu.VMEM((seq, D), BF16), pltpu.VMEM((2, D, CHIP_COLS), BF16),
                        pltpu.VMEM((2, tm, CHIP_COLS), F32), pltpu.VMEM((2, tm, CHIP_COLS), BF16),
                        pltpu.SemaphoreType.DMA((7, n_arr)), pltpu.SemaphoreType.DMA((7, n_arr)),
                        pltpu.SemaphoreType.DMA((n_arr,)), pltpu.SemaphoreType.DMA((2,)), pltpu.SemaphoreType.DMA((2, 2))],
        compiler_params=_params(48, 2),
    )(x, g0, w_shard, *later_shards)


POOL_EXT = 40


def _fill_halo(ext_ref, cur, prev_ref, next_ref, i, n_tiles, ts):
    ext_ref[pl.ds(0, POOL_HALO), :] = jnp.where(i > 0, prev_ref[...], 0.0)
    ext_ref[pl.ds(POOL_HALO, ts), :] = cur
    ext_ref[pl.ds(POOL_HALO + ts, POOL_HALO), :] = jnp.where(i < n_tiles - 1, next_ref[...], 0.0)
    ext_ref[pl.ds(2 * POOL_HALO + ts, POOL_EXT - 2 * POOL_HALO), :] = jnp.zeros((POOL_EXT - 2 * POOL_HALO, D), F32)


def _window_sums(src_ref, tmp_refs, ts, cols, w, shift):
    if w == 2:
        return src_ref[pl.ds(POOL_HALO - 1 + shift, ts), cols] + src_ref[pl.ds(POOL_HALO + shift, ts), cols]
    d2, d4, d8 = tmp_refs
    n2, n4, n8 = ts + 32, ts + 24, ts + 16
    d2[pl.ds(0, n2), :] = src_ref[pl.ds(0, n2), cols] + src_ref[pl.ds(1, n2), cols]
    if w == 4:
        return d2[pl.ds(POOL_HALO - 2 + shift, ts), :] + d2[pl.ds(POOL_HALO + shift, ts), :]
    d4[pl.ds(0, n4), :] = d2[pl.ds(0, n4), :] + d2[pl.ds(2, n4), :]
    if w == 8:
        return d4[pl.ds(POOL_HALO - 4 + shift, ts), :] + d4[pl.ds(POOL_HALO + shift, ts), :]
    d8[pl.ds(0, n8), :] = d4[pl.ds(0, n8), :] + d4[pl.ds(4, n8), :]
    return d8[pl.ds(shift, ts), :] + d8[pl.ds(POOL_HALO + shift, ts), :]


def _pool_scratch(ts):
    return [pltpu.VMEM((ts + POOL_EXT, D), F32)] + [pltpu.VMEM((ts + POOL_EXT, GDIM), F32)] * 3


def _pool_forward(xe_ref, tmp_refs, ts, t0, seq):
    tg = t0 + lax.broadcasted_iota(jnp.int32, (ts, 1), 0)
    outs = []
    for gi, w in enumerate(POOL_WINDOWS):
        hw = w // 2
        cols = slice(gi * GDIM, (gi + 1) * GDIM)
        cnt = (jnp.minimum(tg + hw, seq) - jnp.maximum(tg - hw, 0)).astype(F32)
        outs.append(_window_sums(xe_ref, tmp_refs, ts, cols, w, 0) / cnt - xe_ref[pl.ds(POOL_HALO, ts), cols])
    return jnp.concatenate(outs, axis=1)


def _spatial_mix(ws_ref, vnb, bias, ts):
    rows = []
    for c in range(ts // CHUNK):
        vc = vnb[c * CHUNK:(c + 1) * CHUNK, :]
        rows.append(jnp.concatenate(
            [_mm(ws_ref[h], vc[:, h * GDIM:(h + 1) * GDIM]) for h in range(A_GROUPS)], axis=1) + bias)
    return jnp.concatenate(rows, axis=0)


def _halo_specs(ts, seq, width):
    per = ts // POOL_HALO
    last = seq // POOL_HALO - 1
    prev = pl.BlockSpec((POOL_HALO, width), lambda i: (jnp.maximum(i * per - 1, 0), 0))
    nxt = pl.BlockSpec((POOL_HALO, width), lambda i: (jnp.minimum((i + 1) * per, last), 0))
    return prev, nxt


def _l0_mix_fwd(za, bx, bg, x, ws, bias, gv, wg, scale, wout):
    seq = x.shape[0]
    ts = 512
    n_tiles = seq // ts

    def body(za_ref, bx_ref, bxp_ref, bxn_ref, bg_ref, x_ref, ws_ref, bias_ref, gv_ref, wg_ref, sc_ref, wo_ref,
             x1_ref, xe_ref, *tmp_refs):
        i = pl.program_id(0)
        vg = _gelu(za_ref[:, D:2 * D].astype(F32))
        rv = lax.rsqrt(jnp.mean(vg * vg, axis=1, keepdims=True) + EPS)
        vnb = (vg * rv * gv_ref[...]).astype(BF16)
        mixed = _spatial_mix(ws_ref, vnb, bias_ref[...], ts)

        _fill_halo(xe_ref, bx_ref[...], bxp_ref, bxn_ref, i, n_tiles, ts)
        pb = _pool_forward(xe_ref, tmp_refs, ts, i * ts, seq).astype(BF16)
        ypre = jnp.concatenate([_mm(pb[:, g * GDIM:(g + 1) * GDIM], wg_ref[g]) for g in range(4)], axis=1)

        u = _gelu(za_ref[:, 0:D].astype(F32))
        ag = za_ref[:, 2 * D:3 * D].astype(F32)
        ya = (u * mixed * (ag * jax.nn.sigmoid(ag))).astype(BF16)
        out_a = _mm(ya, wo_ref[0:D, :])

        bgf = bg_ref[...].astype(F32)
        yb = (ypre * sc_ref[...] * (bgf * jax.nn.sigmoid(bgf))).astype(BF16)
        x1_ref[...] = x_ref[...] + out_a + _mm(yb, wo_ref[D:2 * D, :])

    prev, nxt = _halo_specs(ts, seq, D)
    row = lambda w: pl.BlockSpec((ts, w), lambda i: (i, 0))
    return pl.pallas_call(
        body, grid=(n_tiles,), name="l0_mix_fwd",
        out_shape=jax.ShapeDtypeStruct((seq, D), F32),
        in_specs=[row(3 * D), row(D), prev, nxt, row(D), row(D), _resident((4, CHUNK, CHUNK)), _resident((CHUNK, D)),
                  _resident((1, D)), _resident((4, GDIM, GDIM)), _resident((1, D)), _resident((2 * D, D))],
        out_specs=row(D),
        scratch_shapes=_pool_scratch(ts),
        compiler_params=_params(56),
    )(za, bx, bx, bx, bg, x, ws, bias, gv, wg, scale, wout)


def _l1_in_proj(x1, g1, w_t, cos_t, sin_t):
    seq = x1.shape[0]
    tm = 512

    def body(x_ref, g_ref, wt_ref, c_ref, s_ref, q_ref, k_ref, v_ref, gate_ref, ht_ref):
        xf = x_ref[...]
        r = lax.rsqrt(jnp.mean(xf * xf, axis=1, keepdims=True) + EPS)
        ht = (xf * r * g_ref[...]).astype(BF16).T
        ht_ref[...] = ht
        c, s = c_ref[...], s_ref[...]
        q_ref[...] = (_rope_t(_mm(wt_ref[Q_ROWS[0]:Q_ROWS[1], :], ht), c, s, N_HEADS, 1) * SCALE).astype(BF16)
        k_ref[...] = _rope_t(_mm(wt_ref[K_ROWS[0]:K_ROWS[1], :], ht), c, s, N_KV, 1).astype(BF16)
        v_ref[...] = _mm(wt_ref[V_ROWS[0]:V_ROWS[1], :], ht).astype(BF16)
        gate_ref[...] = _mm(wt_ref[G_ROWS[0]:G_ROWS[1], :], ht).astype(BF16)

    col = lambda rows: pl.BlockSpec((rows, tm), lambda i: (0, i))
    return pl.pallas_call(
        body, grid=(seq // tm,), name="l1_in_proj",
        out_shape=(jax.ShapeDtypeStruct((D, seq), BF16), jax.ShapeDtypeStruct((KV_W, seq), BF16),
                   jax.ShapeDtypeStruct((KV_W, seq), BF16), jax.ShapeDtypeStruct((D, seq), BF16),
                   jax.ShapeDtypeStruct((D, seq), BF16)),
        in_specs=[pl.BlockSpec((tm, D), lambda i: (i, 0)), _resident((1, D)), _resident((MIX1_IN, D)), col(ROT_HALF),
                  col(ROT_HALF)],
        out_specs=(col(D), col(KV_W), col(KV_W), col(D), col(D)),
        compiler_params=_params(48),
    )(x1, g1, w_t, cos_t, sin_t)


def _band_specs_t(nb, clamp_i):
    per = TQ // BLK
    prev = pl.BlockSpec((KV_W, BLK), lambda i: (0, jnp.maximum(clamp_i(i) * per - 1, 0)))
    cur = pl.BlockSpec((KV_W, TQ), lambda i: (0, clamp_i(i)))
    nxt = pl.BlockSpec((KV_W, BLK), lambda i: (0, jnp.minimum((clamp_i(i) + 1) * per, nb - 1)))
    return [prev, cur, nxt]


def _fill_band(buf, p_ref, c_ref, n_ref):
    buf[:, 0:BLK] = p_ref[...]
    buf[:, BLK:BLK + TQ] = c_ref[...]
    buf[:, BLK + TQ:2 * BLK + TQ] = n_ref[...]


def _band_bias_t(n, nb):
    c = lax.broadcasted_iota(jnp.int32, (BLK, BLK), 0)
    r = lax.broadcasted_iota(jnp.int32, (BLK, BLK), 1)
    first = jnp.where((c >= r) & (n > 0), 0.0, NEG_INF).astype(F32)
    last = jnp.where((c <= r) & (n < nb - 1), 0.0, NEG_INF).astype(F32)
    return jnp.concatenate([first] * HPP, axis=1), jnp.concatenate([last] * HPP, axis=1)


def _masked(st, bias):
    first, last = bias
    return jnp.concatenate([st[0:BLK] + first, st[BLK:2 * BLK], st[2 * BLK:3 * BLK] + last], axis=0)


AUG = 16


def _ones_rows(n_ones, width):
    return (lax.broadcasted_iota(jnp.int32, (AUG, width), 0) < n_ones).astype(BF16)


def _minus_rows(vec):
    hi = vec.astype(BF16).astype(F32)
    lo = vec - hi
    return jnp.concatenate([-hi, -lo, jnp.zeros((AUG - 2, vec.shape[1]), F32)], axis=0).astype(BF16)


HPP = GQA
FWD_GROUP, BWD_GROUP = 2, 1
BWD_AHEAD = 1


def _heads_t(ref, h0, c0):
    return jnp.concatenate([ref[(h0 + g) * HD:(h0 + g + 1) * HD, c0:c0 + BLK] for g in range(HPP)], axis=1)


def _row4(ref, h0, c0):
    return jnp.concatenate([ref[h0 + g:h0 + g + 1, c0:c0 + BLK] for g in range(HPP)], axis=1)


def _sink_row(sink_ref, h0):
    return jnp.concatenate([jnp.full((1, BLK), sink_ref[h0 + g], F32) for g in range(HPP)], axis=1)


def _l1_attn_fwd(qt, kt, vt, gatet, x1, tgt, wout, gf, sink):
    seq = x1.shape[0]
    nq, nb = seq // TQ, seq // BLK

    def body(q_ref, gate_ref, kp_ref, k_ref, kn_ref, vp_ref, v_ref, vn_ref, x1_ref, tgt_ref, wo_ref, gf_ref, sink_ref,
             dx2_ref, dx2b_ref, att_ref, lse_ref, loss_ref, dgf_ref, dwo_ref, dwo_wire_ref, kbuf, vbuf, att_scr):
        i = pl.program_id(0)

        @pl.when(i == 0)
        def _():
            loss_ref[...] = jnp.zeros_like(loss_ref)
            dgf_ref[...] = jnp.zeros_like(dgf_ref)
            dwo_ref[...] = jnp.zeros_like(dwo_ref)

        _fill_band(kbuf, kp_ref, k_ref, kn_ref)
        _fill_band(vbuf, vp_ref, v_ref, vn_ref)
        ones_row = _ones_rows(1, 3 * BLK)
        groups = [list(range(0, N_HEADS, HPP))[g:g + FWD_GROUP] for g in range(0, N_HEADS // HPP, FWD_GROUP)]
        work = [(j, grp) for j in range(TQ // BLK) for grp in groups]

        def scores(j, passes):
            c0 = j * BLK
            bias = _band_bias_t(i * (TQ // BLK) + j, nb)
            st = dict(c0=c0, passes=passes)
            st["kv_rows"] = [slice(h0 // GQA * HD, (h0 // GQA + 1) * HD) for h0 in passes]
            st["sts"] = [_masked(_tn(kbuf[rows, c0:c0 + 3 * BLK], _heads_t(q_ref, h0, c0)), bias)
                         for h0, rows in zip(passes, st["kv_rows"])]
            return st

        def softmaxes(st):
            st["sks"] = [_sink_row(sink_ref, h0) for h0 in st["passes"]]
            st["ms"] = [jnp.maximum(jnp.max(s_, axis=0, keepdims=True), sk) for s_, sk in zip(st["sts"], st["sks"])]
            st["ps"] = [jnp.exp(s_ - m).astype(BF16) for s_, m in zip(st["sts"], st["ms"])]

        def values(st):
            c0, passes = st["c0"], st["passes"]
            pvs = [_mm(jnp.concatenate([vbuf[rows, c0:c0 + 3 * BLK], ones_row], axis=0), p)
                   for rows, p in zip(st["kv_rows"], st["ps"])]
            lse_rows = []
            for h0, pv, m, sk in zip(passes, pvs, st["ms"], st["sks"]):
                den = pv[HD:HD + 1, :] + jnp.exp(sk - m)
                ot = pv[0:HD, :] / den
                lse = m + jnp.log(den)
                for g in range(HPP):
                    h = h0 + g
                    att_scr[h * HD:(h + 1) * HD, c0:c0 + BLK] = ot[:, g * BLK:(g + 1) * BLK]
                    lse_rows.append(lse[:, g * BLK:(g + 1) * BLK])
            lse_ref[passes[0]:passes[0] + len(lse_rows), c0:c0 + BLK] = jnp.concatenate(lse_rows, axis=0)

        state = scores(*work[0])
        for nxt in work[1:] + [None]:
            following = scores(*nxt) if nxt is not None else None
            softmaxes(state)
            values(state)
            state = following

        att = att_scr[...]
        gate = gate_ref[...].astype(F32)
        yt = (att * (gate * jax.nn.sigmoid(gate))).astype(BF16)
        att_ref[...] = att.astype(BF16)
        x2 = x1_ref[...] + _mm(yt.T, wo_ref[...])
        r = lax.rsqrt(jnp.mean(x2 * x2, axis=1, keepdims=True) + EPS)
        xn = x2 * r
        diff = xn * gf_ref[...] - tgt_ref[...]
        loss_ref[...] += 0.5 * jnp.sum(jnp.mean(diff * diff, axis=1, keepdims=True), axis=0, keepdims=True)
        dout = diff * (1.0 / D)
        dgf_ref[...] += jnp.sum(dout * xn, axis=0, keepdims=True)
        dxn = dout * gf_ref[...]
        dx2 = r * (dxn - xn * jnp.mean(dxn * xn, axis=1, keepdims=True))
        dx2_ref[...] = dx2
        dx2b = dx2.astype(BF16)
        dx2b_ref[...] = dx2b
        dwo_ref[...] += _mm(yt, dx2b)

        @pl.when(i == nq - 1)
        def _():
            dwo_wire_ref[...] = dwo_ref[...].astype(BF16)

    ident = lambda i: i
    row = pl.BlockSpec((TQ, D), lambda i: (i, 0))
    col = lambda rows: pl.BlockSpec((rows, TQ), lambda i: (0, i))
    whole = pl.BlockSpec((D, D), lambda i: (0, 0))
    return pl.pallas_call(
        body, grid=(nq,), name="l1_attn_fwd",
        out_shape=(jax.ShapeDtypeStruct((seq, D), F32), jax.ShapeDtypeStruct((seq, D), BF16),
                   jax.ShapeDtypeStruct((D, seq), BF16),
                   jax.ShapeDtypeStruct((N_HEADS, seq), F32), jax.ShapeDtypeStruct((1, 1), F32),
                   jax.ShapeDtypeStruct((1, D), F32), jax.ShapeDtypeStruct((D, D), F32), jax.ShapeDtypeStruct((D, D), BF16)),
        in_specs=[col(D), col(D)] + _band_specs_t(nb, ident) + _band_specs_t(nb, ident) + [
            row, row, _resident((D, D)), _resident((1, D)), pl.BlockSpec(memory_space=pltpu.SMEM)],
        out_specs=(row, row, col(D), col(N_HEADS), pl.BlockSpec((1, 1), lambda i: (0, 0)),
                   pl.BlockSpec((1, D), lambda i: (0, 0)), whole, whole),
        scratch_shapes=[pltpu.VMEM((KV_W, TQ + 2 * BLK), BF16), pltpu.VMEM((KV_W, TQ + 2 * BLK), BF16),
                        pltpu.VMEM((D, TQ), F32)],
        compiler_params=_params(56),
    )(qt, gatet, kt, kt, kt, vt, vt, vt, x1, tgt, wout, gf, sink)


def _l1_attn_bwd(dx2b, wout, qt, kt, vt, gatet, att, lse, sink):
    seq = dx2b.shape[0]
    nq, nb = seq // TQ, seq // BLK

    def body(dx_ref, wo_ref, q_ref, gate_ref, kp_ref, k_ref, kn_ref, vp_ref, v_ref, vn_ref, att_ref, lse_ref, sink_ref,
             dq_ref, dgate_ref, dk_ref, dv_ref, dsink_ref, kbuf, vbuf, dkacc, dvacc, dat_scr, delta_scr, dsacc):
        i = pl.program_id(0)

        @pl.when(i == 0)
        def _():
            dkacc[...] = jnp.zeros_like(dkacc)
            dvacc[...] = jnp.zeros_like(dvacc)
            dsacc[...] = jnp.zeros_like(dsacc)

        @pl.when(i > 0)
        def _():
            for acc in (dkacc, dvacc):
                acc[:, 0:2 * BLK] = acc[:, TQ:TQ + 2 * BLK]
                acc[:, 2 * BLK:2 * BLK + TQ] = jnp.zeros((KV_W, TQ), F32)

        @pl.when(i < nq)
        def _():
            _fill_band(kbuf, kp_ref, k_ref, kn_ref)
            _fill_band(vbuf, vp_ref, v_ref, vn_ref)
            dyt = _nt(wo_ref[...], dx_ref[...])
            sg, dsg = _silu_and_grad(gate_ref[...].astype(F32))
            attf = att_ref[...].astype(F32)
            dat = dyt * sg
            dat_scr[...] = dat.astype(BF16)
            dgate_ref[...] = (dyt * attf * dsg).astype(BF16)
            dl = dat * attf
            delta_scr[...] = jnp.concatenate(
                [jnp.sum(dl[h * HD:(h + 1) * HD, :], axis=0, keepdims=True) for h in range(N_HEADS)], axis=0)
            ones_rows = _ones_rows(2, 3 * BLK)
            groups = [list(range(0, N_HEADS, HPP))[g:g + BWD_GROUP] for g in range(0, N_HEADS // HPP, BWD_GROUP)]
            work = [(j, grp) for j in range(TQ // BLK) for grp in groups]

            def scores(j, passes):
                c0 = j * BLK
                st = dict(c0=c0, passes=passes, bias=_band_bias_t(i * (TQ // BLK) + j, nb))
                st["kv_rows"] = [slice(h0 // GQA * HD, (h0 // GQA + 1) * HD) for h0 in passes]
                st["q4s"] = [_heads_t(q_ref, h0, c0) for h0 in passes]
                st["do4s"] = [_heads_t(dat_scr, h0, c0) for h0 in passes]
                st["lse4s"] = [_row4(lse_ref, h0, c0) for h0 in passes]
                st["delta4s"] = [_row4(delta_scr, h0, c0) for h0 in passes]
                st["kths"] = [kbuf[rows, c0:c0 + 3 * BLK] for rows in st["kv_rows"]]
                st["sts"] = [_tn(jnp.concatenate([kth, ones_rows], axis=0),
                                 jnp.concatenate([q4, _minus_rows(lse4)], axis=0))
                             for kth, q4, lse4 in zip(st["kths"], st["q4s"], st["lse4s"])]
                st["dpds"] = [_tn(jnp.concatenate([vbuf[rows, c0:c0 + 3 * BLK], ones_rows], axis=0),
                                  jnp.concatenate([do4, _minus_rows(delta4)], axis=0))
                              for rows, do4, delta4 in zip(st["kv_rows"], st["do4s"], st["delta4s"])]
                return st

            def elementwise(st):
                st["ps"] = [jnp.exp(_masked(s_, st["bias"])) for s_ in st["sts"]]
                st["dss"] = [(p * dpd).astype(BF16) for p, dpd in zip(st["ps"], st["dpds"])]

            def gradients(st):
                c0 = st["c0"]
                dq4s = [_mm(kth, ds) * SCALE for kth, ds in zip(st["kths"], st["dss"])]
                dks = [_nt(q4, ds) for q4, ds in zip(st["q4s"], st["dss"])]
                dvs = [_nt(do4, p.astype(BF16)) for do4, p in zip(st["do4s"], st["ps"])]
                for h0, rows, dq4, dk, dv, lse4, delta4 in zip(st["passes"], st["kv_rows"], dq4s, dks, dvs, st["lse4s"],
                                                               st["delta4s"]):
                    dkacc[rows, c0:c0 + 3 * BLK] += dk
                    dvacc[rows, c0:c0 + 3 * BLK] += dv
                    dsk = -jnp.exp(_sink_row(sink_ref, h0) - lse4) * delta4
                    for g in range(HPP):
                        h = h0 + g
                        dq_ref[h * HD:(h + 1) * HD, c0:c0 + BLK] = dq4[:, g * BLK:(g + 1) * BLK].astype(BF16)
                        dsacc[h:h + 1, :] += dsk[:, g * BLK:(g + 1) * BLK]

            ahead = [scores(*w) for w in work[:BWD_AHEAD]]
            for n in range(len(work)):
                if n + BWD_AHEAD < len(work):
                    ahead.append(scores(*work[n + BWD_AHEAD]))
                state = ahead.pop(0)
                elementwise(state)
                gradients(state)

        dk_ref[...] = dkacc[:, 0:TQ].astype(BF16)
        dv_ref[...] = dvacc[:, 0:TQ].astype(BF16)

        @pl.when(i == nq)
        def _():
            dsink_ref[...] = jnp.broadcast_to(jnp.sum(dsacc[...], axis=1, keepdims=True), (N_HEADS, LANES))

    clamp = lambda i: jnp.minimum(i, nq - 1)
    row = pl.BlockSpec((TQ, D), lambda i: (clamp(i), 0))
    col = lambda rows: pl.BlockSpec((rows, TQ), lambda i: (0, clamp(i)))
    pad = pl.BlockSpec((KV_W, TQ), lambda i: (0, i))
    return pl.pallas_call(
        body, grid=(nq + 1,), name="l1_attn_bwd",
        out_shape=(jax.ShapeDtypeStruct((D, seq), BF16), jax.ShapeDtypeStruct((D, seq), BF16),
                   jax.ShapeDtypeStruct((KV_W, seq + TQ), BF16), jax.ShapeDtypeStruct((KV_W, seq + TQ), BF16),
                   jax.ShapeDtypeStruct((N_HEADS, LANES), F32)),
        in_specs=[row, _resident((D, D)), col(D), col(D)] + _band_specs_t(nb, clamp) + _band_specs_t(nb, clamp) + [
            col(D), col(N_HEADS), pl.BlockSpec(memory_space=pltpu.SMEM)],
        out_specs=(col(D), col(D), pad, pad, pl.BlockSpec((N_HEADS, LANES), lambda i: (0, 0))),
        scratch_shapes=[pltpu.VMEM((KV_W, TQ + 2 * BLK), BF16), pltpu.VMEM((KV_W, TQ + 2 * BLK), BF16),
                        pltpu.VMEM((KV_W, TQ + 2 * BLK), F32), pltpu.VMEM((KV_W, TQ + 2 * BLK), F32),
                        pltpu.VMEM((D, TQ), BF16), pltpu.VMEM((N_HEADS, TQ), F32), pltpu.VMEM((N_HEADS, LANES), F32)],
        compiler_params=_params(56),
    )(dx2b, wout, qt, gatet, kt, kt, kt, vt, vt, vt, att, lse, sink)


def _l1_in_proj_bwd(dq_r, dk_r, dv, dgate, cos_t, sin_t, w_t, x1, g1, dx2):
    seq = x1.shape[0]
    tm = 512

    def body(dq_ref, dk_ref, dv_ref, dg_ref, c_ref, s_ref, w_ref, x_ref, g_ref, dres_ref,
             dx_ref, dxb_ref, dz_ref, dn_ref):
        @pl.when(pl.program_id(0) == 0)
        def _():
            dn_ref[...] = jnp.zeros_like(dn_ref)

        c, s = c_ref[...], s_ref[...]
        dq = _rope_t(dq_ref[...].astype(F32), c, s, N_HEADS, -1).astype(BF16)
        dk = _rope_t(dk_ref[...].astype(F32), c, s, N_KV, -1).astype(BF16)
        dz = jnp.concatenate([dq, dk, dv_ref[...], dg_ref[...]], axis=0)
        dz_ref[...] = dz
        dh = _tn(dz, w_ref[...])
        xf = x_ref[...]
        r = lax.rsqrt(jnp.mean(xf * xf, axis=1, keepdims=True) + EPS)
        xn = xf * r
        dn_ref[...] += jnp.sum(dh * xn, axis=0, keepdims=True)
        dxn = dh * g_ref[...]
        dx = dres_ref[...] + r * (dxn - xn * jnp.mean(dxn * xn, axis=1, keepdims=True))
        dx_ref[...] = dx
        dxb_ref[...] = dx.astype(BF16)

    row = pl.BlockSpec((tm, D), lambda i: (i, 0))
    col = lambda rows: pl.BlockSpec((rows, tm), lambda i: (0, i))
    return pl.pallas_call(
        body, grid=(seq // tm,), name="l1_in_proj_bwd",
        out_shape=(jax.ShapeDtypeStruct((seq, D), F32), jax.ShapeDtypeStruct((seq, D), BF16),
                   jax.ShapeDtypeStruct((MIX1_IN, seq), BF16), jax.ShapeDtypeStruct((1, D), F32)),
        in_specs=[col(D), col(KV_W), col(KV_W), col(D), col(ROT_HALF), col(ROT_HALF), _resident((MIX1_IN, D)), row,
                  _resident((1, D)), row],
        out_specs=(row, row, col(MIX1_IN), pl.BlockSpec((1, D), lambda i: (0, 0))),
        compiler_params=_params(48),
    )(dq_r, dk_r, dv, dgate, cos_t, sin_t, w_t, x1, g1, dx2)


def _l0_mix_bwd(dx1b, wout, za, bx, bg, ws, ws_t, bias, gv, wg, wg_t, scale):
    seq = dx1b.shape[0]
    ts = 256
    n_tiles = seq // ts

    def body(dx_ref, wo_ref, za_ref, bx_ref, bxp_ref, bxn_ref, bg_ref, ws_ref, wst_ref, bias_ref, gv_ref, wg_ref,
             wgt_ref, sc_ref,
             dz_ref, dp_ref, catt_ref, dws_ref, dbias_ref, dgv_ref, dsc_ref, dwg_ref, db_ref, xe_ref, *tmp_refs):
        i = pl.program_id(0)

        @pl.when(i == 0)
        def _():
            for r_ in (dws_ref, dbias_ref, dgv_ref, dsc_ref, dwg_ref, db_ref):
                r_[...] = jnp.zeros_like(r_)

        dxb = dx_ref[...]
        dya = _nt(dxb, wo_ref[0:D, :])
        dyb = _nt(dxb, wo_ref[D:2 * D, :])

        vg, dvg_dz = _gelu_and_grad(za_ref[:, D:2 * D].astype(F32))
        rv = lax.rsqrt(jnp.mean(vg * vg, axis=1, keepdims=True) + EPS)
        vnorm = vg * rv
        gvw = gv_ref[...]
        vnb = (vnorm * gvw).astype(BF16)
        mixed = _spatial_mix(ws_ref, vnb, bias_ref[...], ts)

        _fill_halo(xe_ref, bx_ref[...], bxp_ref, bxn_ref, i, n_tiles, ts)
        pb = _pool_forward(xe_ref, tmp_refs, ts, i * ts, seq).astype(BF16)
        ypre = jnp.concatenate([_mm(pb[:, g * GDIM:(g + 1) * GDIM], wg_ref[g]) for g in range(4)], axis=1)

        u, du = _gelu_and_grad(za_ref[:, 0:D].astype(F32))
        sga, dsga = _silu_and_grad(za_ref[:, 2 * D:3 * D].astype(F32))
        um = u * mixed
        ya = (um * sga).astype(BF16)
        t = dya * sga
        dz_ref[:, 0:D] = (t * mixed * du).astype(BF16)
        dz_ref[:, 2 * D:3 * D] = (dya * um * dsga).astype(BF16)
        dmixed = t * u
        dmb = dmixed.astype(BF16)
        dvn_rows = []
        for c in range(ts // CHUNK):
            rows = slice(c * CHUNK, (c + 1) * CHUNK)
            parts = []
            for h in range(A_GROUPS):
                cols = slice(h * GDIM, (h + 1) * GDIM)
                dws_ref[h] += _nt(dmb[rows, cols], vnb[rows, cols])
                parts.append(_mm(wst_ref[h], dmb[rows, cols]))
            dvn_rows.append(jnp.concatenate(parts, axis=1))

        sc = sc_ref[...]
        y = ypre * sc
        sgb, dsgb = _silu_and_grad(bg_ref[...].astype(F32))
        yb = (y * sgb).astype(BF16)
        dy_b = dyb * sgb
        dz_ref[:, 3 * D:4 * D] = jnp.zeros((ts, D), BF16)
        dz_ref[:, 4 * D:5 * D] = (dyb * y * dsgb).astype(BF16)
        dsc_ref[...] += jnp.sum(dy_b * ypre, axis=0, keepdims=True)
        dypre = (dy_b * sc).astype(BF16)
        dps = []
        for g in range(4):
            cols = slice(g * GDIM, (g + 1) * GDIM)
            dwg_ref[g] += _tn(pb[:, cols], dypre[:, cols])
            dps.append(_mm(dypre[:, cols], wgt_ref[g]))

        dbias = dmixed[0:CHUNK, :]
        for c in range(1, ts // CHUNK):
            dbias = dbias + dmixed[c * CHUNK:(c + 1) * CHUNK, :]
        dbias_ref[...] += dbias
        dvn = jnp.concatenate(dvn_rows, axis=0)
        dgv_ref[...] += jnp.sum(dvn * vnorm, axis=0, keepdims=True)
        dxn = dvn * gvw
        dvg = rv * (dxn - vnorm * jnp.mean(dxn * vnorm, axis=1, keepdims=True))
        dz_ref[:, D:2 * D] = (dvg * dvg_dz).astype(BF16)

        dp_ref[...] = jnp.concatenate(dps, axis=1)
        catt_ref[...] = jnp.concatenate([ya, yb], axis=1).T

        @pl.when(i == n_tiles - 1)
        def _():
            for h in range(A_GROUPS):
                tot = jnp.sum(dbias_ref[:, h * GDIM:(h + 1) * GDIM].T, axis=0, keepdims=True)
                db_ref[pl.ds(h * 8, 8), :] = jnp.broadcast_to(tot, (8, CHUNK))

    prev, nxt = _halo_specs(ts, seq, D)
    row = lambda w_: pl.BlockSpec((ts, w_), lambda i: (i, 0))
    acc = lambda shape: pl.BlockSpec(shape, lambda i: (0,) * len(shape))
    return pl.pallas_call(
        body, grid=(n_tiles,), name="l0_mix_bwd",
        out_shape=(jax.ShapeDtypeStruct((seq, MIX0_IN), BF16), jax.ShapeDtypeStruct((seq, D), F32),
                   jax.ShapeDtypeStruct((2 * D, seq), BF16),
                   jax.ShapeDtypeStruct((4, CHUNK, CHUNK), F32), jax.ShapeDtypeStruct((CHUNK, D), F32),
                   jax.ShapeDtypeStruct((1, D), F32), jax.ShapeDtypeStruct((1, D), F32),
                   jax.ShapeDtypeStruct((4, GDIM, GDIM), F32), jax.ShapeDtypeStruct((32, CHUNK), F32)),
        in_specs=[row(D), _resident((2 * D, D)), row(3 * D), row(D), prev, nxt, row(D), _resident((4, CHUNK, CHUNK)),
                  _resident((4, CHUNK, CHUNK)), _resident((CHUNK, D)), _resident((1, D)), _resident((4, GDIM, GDIM)),
                  _resident((4, GDIM, GDIM)), _resident((1, D))],
        out_specs=(row(MIX0_IN), row(D), pl.BlockSpec((2 * D, ts), lambda i: (0, i)),
                   acc((4, CHUNK, CHUNK)), acc((CHUNK, D)), acc((1, D)), acc((1, D)), acc((4, GDIM, GDIM)),
                   acc((32, CHUNK))),
        scratch_shapes=_pool_scratch(ts),
        compiler_params=_params(56),
    )(dx1b, wout, za, bx, bx, bx, bg, ws, ws_t, bias, gv, wg, wg_t, scale)


def _l0_pool_bwd(dp, dz):
    seq = dp.shape[0]
    ts = 512
    n_tiles = seq // ts
    ext = ts + 2 * POOL_HALO

    def body(dp_ref, dpp_ref, dpn_ref, dz_ref, out_ref, qe_ref, *tmp_refs):
        i = pl.program_id(0)
        _fill_halo(qe_ref, dp_ref[...], dpp_ref, dpn_ref, i, n_tiles, ts)
        te = i * ts - POOL_HALO + lax.broadcasted_iota(jnp.int32, (ext, 1), 0)
        for gi, w in enumerate(POOL_WINDOWS):
            hw = w // 2
            cols = slice(gi * GDIM, (gi + 1) * GDIM)
            cnt = jnp.maximum(jnp.minimum(te + hw, seq) - jnp.maximum(te - hw, 0), 1).astype(F32)
            qe_ref[pl.ds(0, ext), cols] = qe_ref[pl.ds(0, ext), cols] / cnt
        outs = []
        for gi, w in enumerate(POOL_WINDOWS):
            cols = slice(gi * GDIM, (gi + 1) * GDIM)
            outs.append(_window_sums(qe_ref, tmp_refs, ts, cols, w, 1) - dp_ref[:, cols])
        out_ref[...] = jnp.concatenate(outs, axis=1).astype(BF16)

    prev, nxt = _halo_specs(ts, seq, D)
    row = pl.BlockSpec((ts, D), lambda i: (i, 0))
    return pl.pallas_call(
        body, grid=(n_tiles,), name="l0_pool_bwd",
        out_shape=jax.ShapeDtypeStruct(dz.shape, BF16),
        in_specs=[row, prev, nxt, pl.BlockSpec(memory_space=pl.ANY)],
        out_specs=pl.BlockSpec((ts, D), lambda i: (i, 3)),
        input_output_aliases={3: 0},
        scratch_shapes=_pool_scratch(ts),
        compiler_params=_params(32),
    )(dp, dp, dp, dz)


def _l0_in_proj_bwd(dz, w, x, g0, dx1):
    seq = x.shape[0]
    tm = 512

    def body(dz_ref, w_ref, x_ref, g_ref, dres_ref, dx_ref, dn_ref):
        @pl.when(pl.program_id(0) == 0)
        def _():
            dn_ref[...] = jnp.zeros_like(dn_ref)

        dh = _nt(dz_ref[...], w_ref[...])
        xf = x_ref[...]
        r = lax.rsqrt(jnp.mean(xf * xf, axis=1, keepdims=True) + EPS)
        xn = xf * r
        dn_ref[...] += jnp.sum(dh * xn, axis=0, keepdims=True)
        dxn = dh * g_ref[...]
        dx_ref[...] = dres_ref[...] + r * (dxn - xn * jnp.mean(dxn * xn, axis=1, keepdims=True))

    row = lambda w_: pl.BlockSpec((tm, w_), lambda i: (i, 0))
    return pl.pallas_call(
        body, grid=(seq // tm,), name="l0_in_proj_bwd",
        out_shape=(jax.ShapeDtypeStruct((seq, D), F32), jax.ShapeDtypeStruct((1, D), F32)),
        in_specs=[row(MIX0_IN), _resident((D, MIX0_IN)), row(D), _resident((1, D)), row(D)],
        out_specs=(row(D), pl.BlockSpec((1, D), lambda i: (0, 0))),
        compiler_params=_params(56),
    )(dz, w, x, g0, dx1)


def _dw_matmul(a_t, b, name, b_transposed=False, tn=1024, ts=1024, col_block=None):
    k, seq = a_t.shape
    n = b.shape[0] if b_transposed else b.shape[1]
    tn = min(n, tn)
    assert seq % ts == 0 and n % tn == 0 and (col_block is None or tn % col_block == 0)
    n_s = seq // ts
    per = 1 if col_block is None else tn // col_block

    def body(a_ref, b_ref, o_ref, ob_ref, acc_ref):
        s = pl.program_id(1)

        @pl.when(s == 0)
        def _():
            acc_ref[...] = jnp.zeros_like(acc_ref)

        acc_ref[...] += _nt(a_ref[...], b_ref[...]) if b_transposed else _mm(a_ref[...], b_ref[...])

        @pl.when(s == n_s - 1)
        def _():
            acc = acc_ref[...]
            if col_block is None:
                o_ref[...] = acc
                ob_ref[...] = acc.astype(BF16)
            else:
                for i in range(per):
                    piece = acc[:, i * col_block:(i + 1) * col_block]
                    o_ref[i] = piece
                    ob_ref[i] = piece.astype(BF16)

    b_spec = (pl.BlockSpec((tn, ts), lambda j, s: (j, s)) if b_transposed else pl.BlockSpec((ts, tn), lambda j, s: (s, j)))
    if col_block is None:
        shape, o_spec = (k, n), pl.BlockSpec((k, tn), lambda j, s: (0, j))
    else:
        shape, o_spec = (n // col_block, k, col_block), pl.BlockSpec((per, k, col_block), lambda j, s: (j, 0, 0))
    return pl.pallas_call(
        body, grid=(n // tn, n_s), name=name,
        out_shape=(jax.ShapeDtypeStruct(shape, F32), jax.ShapeDtypeStruct(shape, BF16)),
        in_specs=[pl.BlockSpec((k, ts), lambda j, s: (0, s)), b_spec],
        out_specs=(o_spec, o_spec),
        scratch_shapes=[pltpu.VMEM((k, tn), F32)],
        compiler_params=_params(56, 2),
    )(a_t, b)


ROW_TILES = 8


def _cast_shards(shards):
    n = len(shards)

    def body(*refs):
        for a in range(n):
            refs[n + a][...] = refs[a][...].astype(BF16)

    specs = [_whole(t.shape) for t in shards]
    return pl.pallas_call(body, grid=(1,), name="cast_weights",
                          out_shape=[jax.ShapeDtypeStruct(t.shape, BF16) for t in shards],
                          in_specs=specs, out_specs=specs, compiler_params=_params(32))(*shards)


def _adamw_math(w, g, m, v):
    m2 = ADAM_B1 * m + (1.0 - ADAM_B1) * g
    v2 = ADAM_B2 * v + (1.0 - ADAM_B2) * (g * g)
    m_hat = m2 / (1.0 - ADAM_B1 ** ADAM_STEP)
    v_hat = v2 / (1.0 - ADAM_B2 ** ADAM_STEP)
    delta = -ADAM_LR * (m_hat / (jnp.sqrt(v_hat) + ADAM_EPS) + ADAM_WD * w)
    return delta, m2, v2


def _final_sum_adamw(g_list, recv_list, me, w_list, m_list, v_list):
    n = len(w_list)

    def body(me_ref, *refs):
        own, recv, w, m, v = (refs[k * n:(k + 1) * n] for k in range(5))
        outs = [refs[(5 + k) * n:(6 + k) * n] for k in range(4)]
        for a in range(n):
            g = own[a][...]
            for k in range(N_DEV - 1):
                g = g + recv[a][k].astype(F32)
            delta, m2, v2 = _adamw_math(w[a][...], g, m[a][...], v[a][...])
            for o_ref, val in zip((outs[0][a], outs[1][a], outs[2][a], outs[3][a]), (g, delta, m2, v2)):
                o_ref[...] = val

    own_specs, flat, wire, shapes = [], [], [], []
    for t in w_list:
        rows, width = t.shape
        tr = rows // ROW_TILES
        own_specs.append(pl.BlockSpec((None, tr, width), lambda i, me: (me[0], i, 0)))
        flat.append(pl.BlockSpec((tr, width), lambda i, me: (i, 0)))
        wire.append(pl.BlockSpec((N_DEV - 1, tr, width), lambda i, me: (0, i, 0)))
        shapes.append(jax.ShapeDtypeStruct((rows, width), F32))
    out = pl.pallas_call(
        body, name="grad_sum_adamw", out_shape=shapes * 4,
        grid_spec=pltpu.PrefetchScalarGridSpec(
            num_scalar_prefetch=1, grid=(ROW_TILES,), in_specs=own_specs + wire + flat * 3, out_specs=flat * 4),
        compiler_params=_params(40),
    )(me, *g_list, *recv_list, *w_list, *m_list, *v_list)
    return [out[k * n:(k + 1) * n] for k in range(4)]


SMALL_NAMES = ("norm_0", "a_v_norm_0", "b_scale_0", "norm_1", "final_norm", "a_spatial_w_0", "a_spatial_b_0", "sink_1")
SMALL_VIEWS = ((8, LANES),) * 5 + ((4 * CHUNK, LANES), (4, LANES), (1, N_HEADS))
SMALL_ROW0 = (0, 8, 16, 24, 32, 40, 552, 560)
SMALL_ROWS = 568


def _small_sum_adamw(early, late, w_list, m_list, v_list):
    n = len(w_list)

    def body(e_ref, l_ref, *refs):
        gtot, first = e_ref[0], l_ref[0]
        for d in range(1, N_DEV):
            gtot = gtot + e_ref[d]
            first = first + l_ref[d]
        for a, ((rows, width), r0) in enumerate(zip(SMALL_VIEWS, SMALL_ROW0)):
            g = first if SMALL_NAMES[a] == "norm_0" else gtot[r0:r0 + rows, 0:width]
            delta, m2, v2 = _adamw_math(refs[a][...], g, refs[n + a][...], refs[2 * n + a][...])
            for k, val in enumerate((g, delta, m2, v2)):
                refs[(3 + k) * n + a][...] = val
        refs[7 * n][...] = gtot[LOSS_ROW:LOSS_ROW + 1, LOSS_LANE:LOSS_LANE + 1]

    shapes = [jax.ShapeDtypeStruct(s, F32) for s in SMALL_VIEWS]
    views = [_whole(s) for s in SMALL_VIEWS]
    out = pl.pallas_call(
        body, grid=(1,), name="small_sum_adamw", out_shape=shapes * 4 + [jax.ShapeDtypeStruct((1, 1), F32)],
        in_specs=[_whole(early.shape), _whole(late.shape)] + views * 3, out_specs=views * 4 + [_whole((1, 1))],
        compiler_params=_params(32),
    )(early, late, *w_list, *m_list, *v_list)
    return [out[k * n:(k + 1) * n] for k in range(4)], out[4 * n]


PEER_FLIPS = tuple((fx, fy, fc) for fx in (0, 1) for fy in (0, 1) for fc in (0, 1))[1:]


def _sequencer_all_gather(blks, name, collective_id, concat_rows=False):
    n = len(blks)

    def body(*refs):
        ins, outs = refs[:n], refs[n:2 * n]
        send_sems, recv_sems, local_sems = refs[2 * n:]
        x, y, c = lax.axis_index("x"), lax.axis_index("y"), lax.axis_index("c")
        peers = [(x ^ fx, y ^ fy, c ^ fc) for fx, fy, fc in PEER_FLIPS]
        barrier = pltpu.get_barrier_semaphore()
        for peer in peers:
            pl.semaphore_signal(barrier, inc=1, device_id=peer, device_id_type=MESH)
        pl.semaphore_wait(barrier, len(peers))
        me = 4 * x + 2 * y + c

        def slot(a):
            rows = blks[a].shape[0]
            return outs[a].at[pl.ds(pl.multiple_of(me * rows, 16), rows)] if concat_rows else outs[a].at[me]

        copies = [pltpu.make_async_remote_copy(
            src_ref=ins[a], dst_ref=slot(a), send_sem=send_sems.at[k, a], recv_sem=recv_sems.at[k, a],
            device_id=peer, device_id_type=MESH) for k, peer in enumerate(peers) for a in range(n)]
        mine = [pltpu.make_async_copy(ins[a], slot(a), local_sems.at[a]) for a in range(n)]
        for cp in copies + mine:
            cp.start()
        for cp in copies + mine:
            cp.wait()

    out_shape = (lambda t: (N_DEV * t.shape[0],) + t.shape[1:]) if concat_rows else (lambda t: (N_DEV,) + t.shape)
    return pl.kernel(
        body, out_type=[jax.ShapeDtypeStruct(out_shape(t), t.dtype) for t in blks],
        mesh=plsc.ScalarSubcoreMesh(axis_name="sequencer", num_cores=1), name=name,
        scratch_types=[pltpu.SemaphoreType.DMA((7, n)), pltpu.SemaphoreType.DMA((7, n)), pltpu.SemaphoreType.DMA((n,))],
        compiler_params=pltpu.CompilerParams(collective_id=collective_id),
    )(*blks)


def _sequencer_scatter(g_list, name, collective_id):
    n = len(g_list)

    def body(*refs):
        ins, outs = refs[:n], refs[n:2 * n]
        send_sems, recv_sems = refs[2 * n:]
        x, y, c = lax.axis_index("x"), lax.axis_index("y"), lax.axis_index("c")
        peers = [(x ^ fx, y ^ fy, c ^ fc) for fx, fy, fc in PEER_FLIPS]
        barrier = pltpu.get_barrier_semaphore()
        for peer in peers:
            pl.semaphore_signal(barrier, inc=1, device_id=peer, device_id_type=MESH)
        pl.semaphore_wait(barrier, len(peers))
        copies = [pltpu.make_async_remote_copy(
            src_ref=ins[a].at[4 * px + 2 * py + pc], dst_ref=outs[a].at[k], send_sem=send_sems.at[k, a],
            recv_sem=recv_sems.at[k, a], device_id=(px, py, pc), device_id_type=MESH)
            for k, (px, py, pc) in enumerate(peers) for a in range(n)]
        for cp in copies:
            cp.start()
        for cp in copies:
            cp.wait()

    return pl.kernel(
        body, out_type=[jax.ShapeDtypeStruct((N_DEV - 1,) + g.shape[1:], g.dtype) for g in g_list],
        mesh=plsc.ScalarSubcoreMesh(axis_name="sequencer", num_cores=1), name=name,
        scratch_types=[pltpu.SemaphoreType.DMA((7, n)), pltpu.SemaphoreType.DMA((7, n))],
        compiler_params=pltpu.CompilerParams(collective_id=collective_id),
    )(*g_list)


def _direct_all_gather(blk, name):
    def body(g_ref, out_ref, send_sems, recv_sems, local_sem):
        x, y, c = lax.axis_index("x"), lax.axis_index("y"), lax.axis_index("c")
        me = 4 * x + 2 * y + c
        copies = [pltpu.make_async_remote_copy(
            src_ref=g_ref, dst_ref=out_ref.at[me], send_sem=send_sems.at[k], recv_sem=recv_sems.at[k],
            device_id=(x ^ fx, y ^ fy, c ^ fc), device_id_type=MESH) for k, (fx, fy, fc) in enumerate(PEER_FLIPS)]
        copies.append(pltpu.make_async_copy(g_ref, out_ref.at[me], local_sem))
        for cp in copies:
            cp.start()
        for cp in copies:
            cp.wait()

    any_spec = pl.BlockSpec(memory_space=pl.ANY)
    return pl.pallas_call(
        body, name=name, out_shape=jax.ShapeDtypeStruct((N_DEV,) + blk.shape, blk.dtype),
        in_specs=[any_spec], out_specs=any_spec,
        scratch_shapes=[pltpu.SemaphoreType.DMA((7,)), pltpu.SemaphoreType.DMA((7,)), pltpu.SemaphoreType.DMA],
    )(blk)


def _shard_views(w_in_0, b_group_w_0, w_out_0, w_in_1, w_out_1):
    return [w_in_0, b_group_w_0.reshape(4 * 32, GDIM), w_out_0, w_in_1, w_out_1]


def _small_views(named):
    return [named[name].reshape(view) for name, view in zip(SMALL_NAMES, SMALL_VIEWS)]


LOSS_ROW, LOSS_LANE = 560, N_HEADS


def _pack_small_grads(named, loss_part):
    rows = []
    for name, (r, w) in zip(SMALL_NAMES, SMALL_VIEWS):
        pad_r = -r % 8
        if name == "sink_1":
            t = jnp.concatenate([named[name].reshape(r, w), loss_part], axis=1)
            rows.append(jnp.pad(t, ((0, pad_r), (0, LANES - w - 1))))
        elif name in named:
            rows.append(jnp.pad(named[name].reshape(r, w), ((0, pad_r), (0, LANES - w))))
        else:
            rows.append(jnp.zeros((r + pad_r, LANES), F32))
    return jnp.concatenate(rows, axis=0)


def _device_blocks(t, axis):
    shape = t.shape
    t = t.reshape(shape[:axis] + (N_DEV, shape[axis] // N_DEV) + shape[axis + 1:])
    t = jnp.moveaxis(t, axis, 0)
    return t.reshape(N_DEV, -1, shape[-1] if axis != len(shape) - 1 else shape[-1] // N_DEV)


def kernel(x, norm_0, w_in_0, a_v_norm_0, a_spatial_w_0, a_spatial_b_0, b_group_w_0, b_scale_0, w_out_0, norm_1, w_in_1, sink_1, w_out_1, final_norm, loss_target, m_norm_0, m_w_in_0, m_a_v_norm_0, m_a_spatial_w_0, m_a_spatial_b_0, m_b_group_w_0, m_b_scale_0, m_w_out_0, m_norm_1, m_w_in_1, m_sink_1, m_w_out_1, m_final_norm, v_norm_0, v_w_in_0, v_a_v_norm_0, v_a_spatial_w_0, v_a_spatial_b_0, v_b_group_w_0, v_b_scale_0, v_w_out_0, v_norm_1, v_w_in_1, v_sink_1, v_w_out_1, v_final_norm):
    seq = x.shape[1]
    xs = x.reshape(seq, D)
    tgt = loss_target.reshape(seq, D)
    ax, ay, ac = lax.axis_index("x"), lax.axis_index("y"), lax.axis_index("c")
    me = jnp.reshape(4 * ax + 2 * ay + ac, (1,)).astype(jnp.int32)

    shards = _shard_views(w_in_0, b_group_w_0, w_out_0, w_in_1, w_out_1)
    cast = _cast_shards([shards[0], shards[1], shards[2], w_in_1.T, shards[4]])

    def l1_weights(after):
        blks, _ = lax.optimization_barrier((cast[3:5], after))
        return _sequencer_all_gather(blks, "weights_gather_l1", 2, concat_rows=True)

    blocks, received, early = {}, {}, {}
    collective_ids = {"l1": 3, "out0": 4, "in0": 5}

    def scatter(tag, own_blocks, wire_blocks):
        blocks[tag] = own_blocks
        received[tag] = _sequencer_scatter(wire_blocks, "grad_scatter_" + tag, collective_ids[tag])

    def small_early(named, loss_part):
        early["small"] = _sequencer_all_gather([_pack_small_grads(named, loss_part)], "small_grad_gather", 6)[0]

    grad_x, d_norm_0 = _local_step(xs, tgt, cast[0], cast[1:3], l1_weights, norm_0, a_v_norm_0, a_spatial_w_0,
                                   a_spatial_b_0, b_scale_0, norm_1, sink_1, final_norm, scatter, small_early)

    order = (("in0", 0), ("in0", 1), ("out0", 0), ("l1", 0), ("l1", 1))
    late = _direct_all_gather(d_norm_0.reshape(8, LANES), "norm_grad_gather")
    shards_late, _ = lax.optimization_barrier((shards, grad_x))
    big = _final_sum_adamw([blocks[t][i] for t, i in order], [received[t][i] for t, i in order], me, shards_late,
                           _shard_views(m_w_in_0, m_b_group_w_0, m_w_out_0, m_w_in_1, m_w_out_1),
                           _shard_views(v_w_in_0, v_b_group_w_0, v_w_out_0, v_w_in_1, v_w_out_1))
    weights = dict(norm_0=norm_0, a_v_norm_0=a_v_norm_0, a_spatial_w_0=a_spatial_w_0, a_spatial_b_0=a_spatial_b_0,
                   b_scale_0=b_scale_0, norm_1=norm_1, sink_1=sink_1, final_norm=final_norm)
    m_small = dict(norm_0=m_norm_0, a_v_norm_0=m_a_v_norm_0, a_spatial_w_0=m_a_spatial_w_0, a_spatial_b_0=m_a_spatial_b_0,
                   b_scale_0=m_b_scale_0, norm_1=m_norm_1, sink_1=m_sink_1, final_norm=m_final_norm)
    v_small = dict(norm_0=v_norm_0, a_v_norm_0=v_a_v_norm_0, a_spatial_w_0=v_a_spatial_w_0, a_spatial_b_0=v_a_spatial_b_0,
                   b_scale_0=v_b_scale_0, norm_1=v_norm_1, sink_1=v_sink_1, final_norm=v_final_norm)
    small, loss = _small_sum_adamw(early["small"], late, _small_views(weights), _small_views(m_small),
                                   _small_views(v_small))

    def in_order(kind):
        b = [b_.reshape(s_.shape) for b_, s_ in zip(big[kind], (w_in_0, b_group_w_0, w_out_0, w_in_1, w_out_1))]
        s = {name: t.reshape(weights[name].shape) for name, t in zip(SMALL_NAMES, small[kind])}
        return [s["norm_0"], b[0], s["a_v_norm_0"], s["a_spatial_w_0"], s["a_spatial_b_0"], b[1], s["b_scale_0"], b[2],
                s["norm_1"], b[3], s["sink_1"], b[4], s["final_norm"]]

    return (loss[0, 0], grad_x.reshape(1, seq, D), *in_order(0), *in_order(1), *in_order(2), *in_order(3))


def _local_step(xs, tgt, win0_shard, l0_shards, l1_weights, norm_0, a_v_norm_0, a_spatial_w_0, a_spatial_b_0, b_scale_0,
                norm_1, sink_1, final_norm, scatter, small_early):
    seq = xs.shape[0]
    ws = a_spatial_w_0.astype(BF16)
    ws_t = jnp.swapaxes(ws, 1, 2)
    bias = jnp.repeat(a_spatial_b_0.T, GDIM, axis=1)
    g0, gv, scale, g1, gf = (t.reshape(1, D) for t in (norm_0, a_v_norm_0, b_scale_0, norm_1, final_norm))
    cos_t, sin_t = _rope_tables_t(seq)

    za, bx, bg, h0_t, win0, g_wg, wout0 = _l0_in_proj(xs, g0, win0_shard, l0_shards)
    win1_t, wout1 = l1_weights(za)
    wg = g_wg.reshape(N_DEV, 4, 32, GDIM).transpose(1, 0, 2, 3).reshape(4, GDIM, GDIM)
    wg_t = jnp.swapaxes(wg, 1, 2)
    x1 = _l0_mix_fwd(za, bx, bg, xs, ws, bias, gv, wg, scale, wout0)
    win1_t, wout1, x1 = lax.optimization_barrier((win1_t, wout1, x1))
    qt, kt, vt, gatet, h1_t = _l1_in_proj(x1, g1, win1_t, cos_t, sin_t)
    dx2, dx2b, att, lse, loss_part, d_gf, d_wout1, d_wout1_wire = _l1_attn_fwd(
        qt, kt, vt, gatet, x1, tgt, wout1, gf, sink_1)

    dq_r, dgate, dk_pad, dv_pad, d_sink = _l1_attn_bwd(dx2b, wout1, qt, kt, vt, gatet, att, lse, sink_1)
    dk_r = dk_pad[:, BLK:BLK + seq]
    dv = dv_pad[:, BLK:BLK + seq]
    dx1, dx1b, dz1_t, d_g1 = _l1_in_proj_bwd(dq_r, dk_r, dv, dgate, cos_t, sin_t, win1_t, x1, g1, dx2)
    d_win1, d_win1_wire = _dw_matmul(h1_t, dz1_t, "dw_in_1", b_transposed=True, tn=1280, col_block=MIX1_IN // N_DEV)
    rows = lambda t: t.reshape(N_DEV, t.shape[0] // N_DEV, t.shape[1])
    scatter("l1", [d_win1, rows(d_wout1)], [d_win1_wire, rows(d_wout1_wire)])

    dz0, dp, cat_t, d_ws, _, d_gv, d_scale, d_wg, d_b = _l0_mix_bwd(
        dx1b, wout0, za, bx, bg, ws, ws_t, bias, gv, wg, wg_t, scale)
    dz0 = _l0_pool_bwd(dp, dz0)
    d_win0, d_win0_wire = _dw_matmul(h0_t, dz0, "dw_in_0", tn=1280, col_block=MIX0_IN // N_DEV)
    d_wg_blocks = _device_blocks(d_wg, 1)
    scatter("in0", [d_win0, d_wg_blocks], [d_win0_wire, d_wg_blocks])
    cat_t, _ = lax.optimization_barrier((cat_t, d_win0))
    d_wout0, d_wout0_wire = _dw_matmul(cat_t, dx1b, "dw_out_0")
    scatter("out0", [rows(d_wout0)], [rows(d_wout0_wire)])
    small_early(dict(a_v_norm_0=d_gv, a_spatial_w_0=d_ws, a_spatial_b_0=d_b.reshape(4, 8, CHUNK)[:, 0, :],
                     b_scale_0=d_scale, norm_1=d_g1, sink_1=d_sink[:, 0], final_norm=d_gf), loss_part)
    dz0, _ = lax.optimization_barrier((dz0, d_wout0))
    return _l0_in_proj_bwd(dz0, win0, xs, g0, dx1)
```

```python
import jax
import jax.numpy as jnp
from jax import lax
from jax.experimental import pallas as pl
from jax.experimental.pallas import tpu as pltpu
from jax.experimental.pallas import tpu_sc as plsc

F32 = jnp.float32
BF16 = jnp.bfloat16

D = 1024
EPS = 1e-6
NEG_INF = -1e30
CHUNK = 128
A_GROUPS = 4
POOL_WINDOWS = (2, 4, 8, 16)
POOL_HALO = 8
GDIM = 256
N_HEADS = 16
N_KV = 4
GQA = 4
HD = 64
BLK = 128
ROT_HALF = 8
ROPE_THETA = 500000.0
SCALE = HD ** -0.5
MIX0_IN = 5 * D
MIX1_IN = 2560
KV_W = N_KV * HD
Q_ROWS, K_ROWS, V_ROWS, G_ROWS = (0, D), (D, D + KV_W), (D + KV_W, D + 2 * KV_W), (D + 2 * KV_W, MIX1_IN)
TQ = 512

ADAM_LR = 0.001
ADAM_B1 = 0.9
ADAM_B2 = 0.999
ADAM_EPS = 1e-08
ADAM_WD = 0.01
ADAM_STEP = 10

N_DEV = 8
LANES = 128
MIB = 2 ** 20
MESH = pl.DeviceIdType.MESH


def _params(limit_mib, n_axes=1):
    return pltpu.CompilerParams(vmem_limit_bytes=limit_mib * MIB, dimension_semantics=("arbitrary",) * n_axes)


def _resident(shape):
    nd = len(shape)
    return pl.BlockSpec(shape, lambda *_: (0,) * nd, pipeline_mode=pl.Buffered(1))


def _gelu(x):
    k = 0.7978845608028654
    return 0.5 * x * (1.0 + jnp.tanh(k * (x + 0.044715 * x * x * x)))


def _gelu_and_grad(x):
    k = 0.7978845608028654
    x2 = x * x
    t = jnp.tanh(k * (x + 0.044715 * x * x2))
    g = 0.5 * x * (1.0 + t)
    dg = 0.5 * (1.0 + t) + 0.5 * x * (1.0 - t * t) * (k * (1.0 + 3.0 * 0.044715 * x2))
    return g, dg


def _silu_and_grad(x):
    s = jax.nn.sigmoid(x)
    return x * s, s * (1.0 + x * (1.0 - s))


def _nt(a, b):
    return lax.dot_general(a, b, (((1,), (1,)), ((), ())), preferred_element_type=F32)


def _tn(a, b):
    return lax.dot_general(a, b, (((0,), (0,)), ((), ())), preferred_element_type=F32)


def _mm(a, b):
    return jnp.dot(a, b, preferred_element_type=F32)


def _rope_tables_t(seq):
    inv = ROPE_THETA ** (-jnp.arange(0, 2 * ROT_HALF, 2, dtype=F32) / (2 * ROT_HALF))
    ang = inv[:, None] * jnp.arange(seq, dtype=F32)[None, :]
    return jnp.cos(ang), jnp.sin(ang)


def _rope_t(z, c, s, n_heads, sign):
    parts = []
    for h in range(n_heads):
        b = h * HD
        x1, x2 = z[b:b + ROT_HALF], z[b + ROT_HALF:b + 2 * ROT_HALF]
        if sign > 0:
            parts += [x1 * c - x2 * s, x2 * c + x1 * s]
        else:
            parts += [x1 * c + x2 * s, x2 * c - x1 * s]
        parts.append(z[b + 2 * ROT_HALF:b + HD])
    return jnp.concatenate(parts, axis=0)


N_CHIPS = 4
CHIP_COLS = MIX0_IN // N_CHIPS
IN_PROJ_ROWS = 512
IN_PROJ_PIECES = (
    ((0, 0, CHIP_COLS, 0),),
    ((0, CHIP_COLS, CHIP_COLS, 0),),
    ((0, 2 * CHIP_COLS, 3 * D - 2 * CHIP_COLS, 0), (1, 0, 3 * CHIP_COLS - 3 * D, 3 * D - 2 * CHIP_COLS)),
    ((1, 3 * CHIP_COLS - 3 * D, 4 * D - 3 * CHIP_COLS, 0), (2, 0, D, 4 * D - 3 * CHIP_COLS)),
)


def _l0_in_proj(x, g0, w_shard, later_shards):
    seq = x.shape[0]
    tm = 1024
    n = seq // tm
    shard_cols = w_shard.shape[1]
    n_arr = 1 + len(later_shards)
    later = range(1, n_arr)
    assert 2 * shard_cols == CHIP_COLS and seq % tm == 0 and n >= 4

    def body(*refs):
        x_ref, g_ref = refs[:2]
        ins = refs[2:2 + n_arr]
        za_ref, bx_ref, bg_ref, ht_ref = refs[2 + n_arr:6 + n_arr]
        gathered = refs[6 + n_arr:6 + 2 * n_arr]
        h_all, w_buf, z32, z16, send_sems, recv_sems, local_sems, load_sems, out_sems = refs[6 + 2 * n_arr:]
        p, i = pl.program_id(0), pl.program_id(1)
        ax, ay, ac = lax.axis_index("x"), lax.axis_index("y"), lax.axis_index("c")
        me, sibling = (ax, ay, ac), (ax, ay, 1 - ac)
        chips = [(ax, ay), (1 - ax, ay), (ax, 1 - ay), (1 - ax, 1 - ay)]
        outs = (za_ref, bx_ref, bg_ref)

        def slot(a, px, py, pc):
            dev = 4 * px + 2 * py + pc
            if a == 0:
                return gathered[0].at[:, pl.ds(pl.multiple_of(dev * shard_cols, LANES), shard_cols)]
            rows = later_shards[a - 1].shape[0]
            return gathered[a].at[pl.ds(pl.multiple_of(dev * rows, 16), rows)]

        def copy(k, a, block, to, from_input=False):
            return pltpu.make_async_remote_copy(
                src_ref=ins[a] if from_input else slot(a, *block), dst_ref=slot(a, *block),
                send_sem=send_sems.at[k, a], recv_sem=recv_sems.at[k, a], device_id=to, device_id_type=MESH)

        def to_sibling(a):
            return copy(0, a, me, sibling, from_input=True)

        def send(j, a):
            return copy(j, a, me, (*chips[j], ac), from_input=True)

        def landed(j, a):
            return copy(j, a, (*chips[j], ac), me)

        def forward(j, a):
            return copy(3 + j, a, (*chips[j], ac), sibling)

        def forwarded(j, a):
            return copy(3 + j, a, (*chips[j], 1 - ac), me)

        def mine(a):
            return pltpu.make_async_copy(ins[a], slot(a, *me), local_sems.at[a])

        def load(chip, q):
            px, py = chip
            cols = pl.ds(pl.multiple_of((2 * px + py) * CHIP_COLS, LANES), CHIP_COLS)
            return pltpu.make_async_copy(gathered[0].at[:, cols], w_buf.at[q % 2], load_sems.at[q % 2])

        def out_copies(q, tile, stage):
            cps = []
            for k, (o, c0, width, z0) in enumerate(IN_PROJ_PIECES[q]):
                src = z32.at[stage, :, pl.ds(z0, width)] if o == 1 else z16.at[stage, :, pl.ds(z0, width)]
                dst = outs[o].at[pl.ds(pl.multiple_of(tile * tm, tm), tm), pl.ds(c0, width)]
                cps.append(pltpu.make_async_copy(src, dst, out_sems.at[stage, k]))
            return cps

        @pl.when((p == 0) & (i == 0))
        def _():
            for a in range(n_arr):
                mine(a).start()
                to_sibling(a).start()
            send(1, 0).start()
            send(2, 0).start()
            copy(0, 0, sibling, me).wait_recv()
            mine(0).wait()
            load(chips[0], 0).start()

        for j in range(1, N_CHIPS):
            @pl.when((p == j - 1) & (i == n - 1))
            def _(j=j):
                landed(j, 0).wait_recv()
                forward(j, 0).start()
                forwarded(j, 0).wait_recv()
                if j == 1:
                    send(1, 0).wait_send()
                    send(2, 0).wait_send()
                    send(3, 0).start()
                    for a in later:
                        for jj in range(1, N_CHIPS):
                            send(jj, a).start()
                load(chips[j], j).start()

        @pl.when((p == N_CHIPS - 1) & (i == n - 4))
        def _():
            for jj in range(1, N_CHIPS):
                for a in later:
                    landed(jj, a).wait_recv()
                    forward(jj, a).start()

        @pl.when(i == 0)
        def _():
            load(chips[0], p).wait()

        @pl.when(p == 0)
        def _():
            xf = x_ref[...]
            r = lax.rsqrt(jnp.mean(xf * xf, axis=1, keepdims=True) + EPS)
            h = (xf * r * g_ref[...]).astype(BF16)
            ht_ref[...] = h.T
            h_all[pl.ds(pl.multiple_of(i * tm, tm), tm), :] = h

        def chip_of_pass(pp):
            return (2 * ax + ay) ^ ((pp >> 1) | ((pp & 1) << 1))

        step = p * n + i
        stage = step % 2
        for q in range(N_CHIPS):
            @pl.when((step >= 2) & (chip_of_pass((step - 2) // n) == q))
            def _(q=q):
                for cp in out_copies(q, (step - 2) % n, stage):
                    cp.wait()

        for r0 in range(0, tm, IN_PROJ_ROWS):
            rows = pl.ds(r0, IN_PROJ_ROWS)
            z32[stage, rows] = _mm(h_all[pl.ds(pl.multiple_of(i * tm + r0, IN_PROJ_ROWS), IN_PROJ_ROWS), :], w_buf[p % 2])
            z16[stage, rows] = z32[stage, rows].astype(BF16)
        for q in range(N_CHIPS):
            @pl.when(chip_of_pass(p) == q)
            def _(q=q):
                for cp in out_copies(q, i, stage):
                    cp.start()

        last = (p == N_CHIPS - 1) & (i == n - 1)
        for q in range(N_CHIPS):
            @pl.when(last & (chip_of_pass(p) == q))
            def _(q=q):
                for cp in out_copies(q, n - 2, 1 - stage) + out_copies(q, n - 1, stage):
                    cp.wait()

        @pl.when(last)
        def _():
            for a in later:
                copy(0, a, sibling, me).wait_recv()
                for jj in range(1, N_CHIPS):
                    forwarded(jj, a).wait_recv()
                    send(jj, a).wait_send()
                mine(a).wait()
            send(3, 0).wait_send()
            for a in range(n_arr):
                to_sibling(a).wait_send()
                for jj in range(1, N_CHIPS):
                    forward(jj, a).wait_send()

    any_spec = pl.BlockSpec(memory_space=pl.ANY)
    first_pass_tile = lambda p, i: jnp.where(p == 0, i, n - 1)
    return pl.pallas_call(
        body, grid=(N_CHIPS, n), name="l0_in_proj",
        out_shape=[jax.ShapeDtypeStruct((seq, 3 * D), BF16), jax.ShapeDtypeStruct((seq, D), F32),
                   jax.ShapeDtypeStruct((seq, D), BF16), jax.ShapeDtypeStruct((D, seq), BF16),
                   jax.ShapeDtypeStruct((D, MIX0_IN), BF16)]
        + [jax.ShapeDtypeStruct((N_DEV * t.shape[0], t.shape[1]), t.dtype) for t in later_shards],
        in_specs=[pl.BlockSpec((tm, D), lambda p, i: (first_pass_tile(p, i), 0)), _resident((1, D))] + [any_spec] * n_arr,
        out_specs=[any_spec, any_spec, any_spec, pl.BlockSpec((D, tm), lambda p, i: (0, first_pass_tile(p, i)))]
        + [any_spec] * n_arr,
        scratch_shapes=[pltpu.VMEM((seq, D), BF16), pltpu.VMEM((2, D, CHIP_COLS), BF16),
                        pltpu.VMEM((2, tm, CHIP_COLS), F32), pltpu.VMEM((2, tm, CHIP_COLS), BF16),
                        pltpu.SemaphoreType.DMA((7, n_arr)), pltpu.SemaphoreType.DMA((7, n_arr)),
                        pltpu.SemaphoreType.DMA((n_arr,)), pltpu.SemaphoreType.DMA((2,)), pltpu.SemaphoreType.DMA((2, 2))],
        compiler_params=_params(56, 2),
    )(x, g0, w_shard, *later_shards)


POOL_EXT = 40


def _fill_halo(ext_ref, cur, prev_ref, next_ref, i, n_tiles, ts):
    ext_ref[pl.ds(0, POOL_HALO), :] = jnp.where(i > 0, prev_ref[...], 0.0)
    ext_ref[pl.ds(POOL_HALO, ts), :] = cur
    ext_ref[pl.ds(POOL_HALO + ts, POOL_HALO), :] = jnp.where(i < n_tiles - 1, next_ref[...], 0.0)
    ext_ref[pl.ds(2 * POOL_HALO + ts, POOL_EXT - 2 * POOL_HALO), :] = jnp.zeros((POOL_EXT - 2 * POOL_HALO, D), F32)


def _window_sums(src_ref, tmp_refs, ts, cols, w, shift):
    if w == 2:
        return src_ref[pl.ds(POOL_HALO - 1 + shift, ts), cols] + src_ref[pl.ds(POOL_HALO + shift, ts), cols]
    d2, d4, d8 = tmp_refs
    n2, n4, n8 = ts + 32, ts + 24, ts + 16
    d2[pl.ds(0, n2), :] = src_ref[pl.ds(0, n2), cols] + src_ref[pl.ds(1, n2), cols]
    if w == 4:
        return d2[pl.ds(POOL_HALO - 2 + shift, ts), :] + d2[pl.ds(POOL_HALO + shift, ts), :]
    d4[pl.ds(0, n4), :] = d2[pl.ds(0, n4), :] + d2[pl.ds(2, n4), :]
    if w == 8:
        return d4[pl.ds(POOL_HALO - 4 + shift, ts), :] + d4[pl.ds(POOL_HALO + shift, ts), :]
    d8[pl.ds(0, n8), :] = d4[pl.ds(0, n8), :] + d4[pl.ds(4, n8), :]
    return d8[pl.ds(shift, ts), :] + d8[pl.ds(POOL_HALO + shift, ts), :]


def _pool_scratch(ts):
    return [pltpu.VMEM((ts + POOL_EXT, D), F32)] + [pltpu.VMEM((ts + POOL_EXT, GDIM), F32)] * 3


def _pool_forward(xe_ref, tmp_refs, ts, t0, seq):
    tg = t0 + lax.broadcasted_iota(jnp.int32, (ts, 1), 0)
    outs = []
    for gi, w in enumerate(POOL_WINDOWS):
        hw = w // 2
        cols = slice(gi * GDIM, (gi + 1) * GDIM)
        cnt = (jnp.minimum(tg + hw, seq) - jnp.maximum(tg - hw, 0)).astype(F32)
        outs.append(_window_sums(xe_ref, tmp_refs, ts, cols, w, 0) / cnt - xe_ref[pl.ds(POOL_HALO, ts), cols])
    return jnp.concatenate(outs, axis=1)


def _spatial_mix(ws_ref, vnb, bias, ts):
    rows = []
    for c in range(ts // CHUNK):
        vc = vnb[c * CHUNK:(c + 1) * CHUNK, :]
        rows.append(jnp.concatenate(
            [_mm(ws_ref[h], vc[:, h * GDIM:(h + 1) * GDIM]) for h in range(A_GROUPS)], axis=1) + bias)
    return jnp.concatenate(rows, axis=0)


def _halo_specs(ts, seq, width):
    per = ts // POOL_HALO
    last = seq // POOL_HALO - 1
    prev = pl.BlockSpec((POOL_HALO, width), lambda i: (jnp.maximum(i * per - 1, 0), 0))
    nxt = pl.BlockSpec((POOL_HALO, width), lambda i: (jnp.minimum((i + 1) * per, last), 0))
    return prev, nxt


def _l0_mix_fwd(za, bx, bg, x, ws, bias, gv, wg, scale, wout):
    seq = x.shape[0]
    ts = 512
    n_tiles = seq // ts

    def body(za_ref, bx_ref, bxp_ref, bxn_ref, bg_ref, x_ref, ws_ref, bias_ref, gv_ref, wg_ref, sc_ref, wo_ref,
             x1_ref, xe_ref, *tmp_refs):
        i = pl.program_id(0)
        vg = _gelu(za_ref[:, D:2 * D].astype(F32))
        rv = lax.rsqrt(jnp.mean(vg * vg, axis=1, keepdims=True) + EPS)
        vnb = (vg * rv * gv_ref[...]).astype(BF16)
        mixed = _spatial_mix(ws_ref, vnb, bias_ref[...], ts)

        _fill_halo(xe_ref, bx_ref[...], bxp_ref, bxn_ref, i, n_tiles, ts)
        pb = _pool_forward(xe_ref, tmp_refs, ts, i * ts, seq).astype(BF16)
        ypre = jnp.concatenate([_mm(pb[:, g * GDIM:(g + 1) * GDIM], wg_ref[g]) for g in range(4)], axis=1)

        u = _gelu(za_ref[:, 0:D].astype(F32))
        ag = za_ref[:, 2 * D:3 * D].astype(F32)
        ya = (u * mixed * (ag * jax.nn.sigmoid(ag))).astype(BF16)
        out_a = _mm(ya, wo_ref[0:D, :])

        bgf = bg_ref[...].astype(F32)
        yb = (ypre * sc_ref[...] * (bgf * jax.nn.sigmoid(bgf))).astype(BF16)
        x1_ref[...] = x_ref[...] + out_a + _mm(yb, wo_ref[D:2 * D, :])

    prev, nxt = _halo_specs(ts, seq, D)
    row = lambda w: pl.BlockSpec((ts, w), lambda i: (i, 0))
    return pl.pallas_call(
        body, grid=(n_tiles,), name="l0_mix_fwd",
        out_shape=jax.ShapeDtypeStruct((seq, D), F32),
        in_specs=[row(3 * D), row(D), prev, nxt, row(D), row(D), _resident((4, CHUNK, CHUNK)), _resident((CHUNK, D)),
                  _resident((1, D)), _resident((4, GDIM, GDIM)), _resident((1, D)), _resident((2 * D, D))],
        out_specs=row(D),
        scratch_shapes=_pool_scratch(ts),
        compiler_params=_params(56),
    )(za, bx, bx, bx, bg, x, ws, bias, gv, wg, scale, wout)


def _l1_in_proj(x1, g1, w_t, cos_t, sin_t):
    seq = x1.shape[0]
    tm = 512

    def body(x_ref, g_ref, wt_ref, c_ref, s_ref, q_ref, k_ref, v_ref, gate_ref, ht_ref):
        xf = x_ref[...]
        r = lax.rsqrt(jnp.mean(xf * xf, axis=1, keepdims=True) + EPS)
        ht = (xf * r * g_ref[...]).astype(BF16).T
        ht_ref[...] = ht
        c, s = c_ref[...], s_ref[...]
        q_ref[...] = (_rope_t(_mm(wt_ref[Q_ROWS[0]:Q_ROWS[1], :], ht), c, s, N_HEADS, 1) * SCALE).astype(BF16)
        k_ref[...] = _rope_t(_mm(wt_ref[K_ROWS[0]:K_ROWS[1], :], ht), c, s, N_KV, 1).astype(BF16)
        v_ref[...] = _mm(wt_ref[V_ROWS[0]:V_ROWS[1], :], ht).astype(BF16)
        gate_ref[...] = _mm(wt_ref[G_ROWS[0]:G_ROWS[1], :], ht).astype(BF16)

    col = lambda rows: pl.BlockSpec((rows, tm), lambda i: (0, i))
    return pl.pallas_call(
        body, grid=(seq // tm,), name="l1_in_proj",
        out_shape=(jax.ShapeDtypeStruct((D, seq), BF16), jax.ShapeDtypeStruct((KV_W, seq), BF16),
                   jax.ShapeDtypeStruct((KV_W, seq), BF16), jax.ShapeDtypeStruct((D, seq), BF16),
                   jax.ShapeDtypeStruct((D, seq), BF16)),
        in_specs=[pl.BlockSpec((tm, D), lambda i: (i, 0)), _resident((1, D)), _resident((MIX1_IN, D)), col(ROT_HALF),
                  col(ROT_HALF)],
        out_specs=(col(D), col(KV_W), col(KV_W), col(D), col(D)),
        compiler_params=_params(48),
    )(x1, g1, w_t, cos_t, sin_t)


def _band_specs_t(nb, clamp_i):
    per = TQ // BLK
    prev = pl.BlockSpec((KV_W, BLK), lambda i: (0, jnp.maximum(clamp_i(i) * per - 1, 0)))
    cur = pl.BlockSpec((KV_W, TQ), lambda i: (0, clamp_i(i)))
    nxt = pl.BlockSpec((KV_W, BLK), lambda i: (0, jnp.minimum((clamp_i(i) + 1) * per, nb - 1)))
    return [prev, cur, nxt]


def _fill_band(buf, p_ref, c_ref, n_ref):
    buf[:, 0:BLK] = p_ref[...]
    buf[:, BLK:BLK + TQ] = c_ref[...]
    buf[:, BLK + TQ:2 * BLK + TQ] = n_ref[...]


def _band_bias_t(n, nb):
    c = lax.broadcasted_iota(jnp.int32, (BLK, BLK), 0)
    r = lax.broadcasted_iota(jnp.int32, (BLK, BLK), 1)
    first = jnp.where((c >= r) & (n > 0), 0.0, NEG_INF).astype(F32)
    last = jnp.where((c <= r) & (n < nb - 1), 0.0, NEG_INF).astype(F32)
    return jnp.concatenate([first] * HPP, axis=1), jnp.concatenate([last] * HPP, axis=1)


def _masked(st, bias):
    first, last = bias
    return jnp.concatenate([st[0:BLK] + first, st[BLK:2 * BLK], st[2 * BLK:3 * BLK] + last], axis=0)


AUG = 16


def _ones_rows(n_ones, width):
    return (lax.broadcasted_iota(jnp.int32, (AUG, width), 0) < n_ones).astype(BF16)


def _minus_rows(vec):
    hi = vec.astype(BF16).astype(F32)
    lo = vec - hi
    return jnp.concatenate([-hi, -lo, jnp.zeros((AUG - 2, vec.shape[1]), F32)], axis=0).astype(BF16)


HPP = GQA
FWD_GROUP, BWD_GROUP = 2, 1
BWD_AHEAD = 1


def _heads_t(ref, h0, c0):
    return jnp.concatenate([ref[(h0 + g) * HD:(h0 + g + 1) * HD, c0:c0 + BLK] for g in range(HPP)], axis=1)


def _row4(ref, h0, c0):
    return jnp.concatenate([ref[h0 + g:h0 + g + 1, c0:c0 + BLK] for g in range(HPP)], axis=1)


def _sink_row(sink_ref, h0):
    return jnp.concatenate([jnp.full((1, BLK), sink_ref[h0 + g], F32) for g in range(HPP)], axis=1)


def _l1_attn_fwd(qt, kt, vt, gatet, x1, tgt, wout, gf, sink):
    seq = x1.shape[0]
    nq, nb = seq // TQ, seq // BLK

    def body(q_ref, gate_ref, kp_ref, k_ref, kn_ref, vp_ref, v_ref, vn_ref, x1_ref, tgt_ref, wo_ref, gf_ref, sink_ref,
             dx2_ref, dx2b_ref, att_ref, lse_ref, loss_ref, dgf_ref, dwo_ref, dwo_wire_ref, kbuf, vbuf, att_scr):
        i = pl.program_id(0)

        @pl.when(i == 0)
        def _():
            loss_ref[...] = jnp.zeros_like(loss_ref)
            dgf_ref[...] = jnp.zeros_like(dgf_ref)
            dwo_ref[...] = jnp.zeros_like(dwo_ref)

        _fill_band(kbuf, kp_ref, k_ref, kn_ref)
        _fill_band(vbuf, vp_ref, v_ref, vn_ref)
        ones_row = _ones_rows(1, 3 * BLK)
        groups = [list(range(0, N_HEADS, HPP))[g:g + FWD_GROUP] for g in range(0, N_HEADS // HPP, FWD_GROUP)]
        work = [(j, grp) for j in range(TQ // BLK) for grp in groups]

        def scores(j, passes):
            c0 = j * BLK
            bias = _band_bias_t(i * (TQ // BLK) + j, nb)
            st = dict(c0=c0, passes=passes)
            st["kv_rows"] = [slice(h0 // GQA * HD, (h0 // GQA + 1) * HD) for h0 in passes]
            st["sts"] = [_masked(_tn(kbuf[rows, c0:c0 + 3 * BLK], _heads_t(q_ref, h0, c0)), bias)
                         for h0, rows in zip(passes, st["kv_rows"])]
            return st

        def softmaxes(st):
            st["sks"] = [_sink_row(sink_ref, h0) for h0 in st["passes"]]
            st["ms"] = [jnp.maximum(jnp.max(s_, axis=0, keepdims=True), sk) for s_, sk in zip(st["sts"], st["sks"])]
            st["ps"] = [jnp.exp(s_ - m).astype(BF16) for s_, m in zip(st["sts"], st["ms"])]

        def values(st):
            c0, passes = st["c0"], st["passes"]
            pvs = [_mm(jnp.concatenate([vbuf[rows, c0:c0 + 3 * BLK], ones_row], axis=0), p)
                   for rows, p in zip(st["kv_rows"], st["ps"])]
            lse_rows = []
            for h0, pv, m, sk in zip(passes, pvs, st["ms"], st["sks"]):
                den = pv[HD:HD + 1, :] + jnp.exp(sk - m)
                ot = pv[0:HD, :] / den
                lse = m + jnp.log(den)
                for g in range(HPP):
                    h = h0 + g
                    att_scr[h * HD:(h + 1) * HD, c0:c0 + BLK] = ot[:, g * BLK:(g + 1) * BLK]
                    lse_rows.append(lse[:, g * BLK:(g + 1) * BLK])
            lse_ref[passes[0]:passes[0] + len(lse_rows), c0:c0 + BLK] = jnp.concatenate(lse_rows, axis=0)

        state = scores(*work[0])
        for nxt in work[1:] + [None]:
            following = scores(*nxt) if nxt is not None else None
            softmaxes(state)
            values(state)
            state = following

        att = att_scr[...]
        gate = gate_ref[...].astype(F32)
        yt = (att * (gate * jax.nn.sigmoid(gate))).astype(BF16)
        att_ref[...] = att.astype(BF16)
        x2 = x1_ref[...] + _mm(yt.T, wo_ref[...])
        r = lax.rsqrt(jnp.mean(x2 * x2, axis=1, keepdims=True) + EPS)
        xn = x2 * r
        diff = xn * gf_ref[...] - tgt_ref[...]
        loss_ref[...] += 0.5 * jnp.sum(jnp.mean(diff * diff, axis=1, keepdims=True), axis=0, keepdims=True)
        dout = diff * (1.0 / D)
        dgf_ref[...] += jnp.sum(dout * xn, axis=0, keepdims=True)
        dxn = dout * gf_ref[...]
        dx2 = r * (dxn - xn * jnp.mean(dxn * xn, axis=1, keepdims=True))
        dx2_ref[...] = dx2
        dx2b = dx2.astype(BF16)
        dx2b_ref[...] = dx2b
        dwo_ref[...] += _mm(yt, dx2b)

        @pl.when(i == nq - 1)
        def _():
            dwo_wire_ref[...] = dwo_ref[...].astype(BF16)

    ident = lambda i: i
    row = pl.BlockSpec((TQ, D), lambda i: (i, 0))
    col = lambda rows: pl.BlockSpec((rows, TQ), lambda i: (0, i))
    whole = pl.BlockSpec((D, D), lambda i: (0, 0))
    return pl.pallas_call(
        body, grid=(nq,), name="l1_attn_fwd",
        out_shape=(jax.ShapeDtypeStruct((seq, D), F32), jax.ShapeDtypeStruct((seq, D), BF16),
                   jax.ShapeDtypeStruct((D, seq), BF16),
                   jax.ShapeDtypeStruct((N_HEADS, seq), F32), jax.ShapeDtypeStruct((1, 1), F32),
                   jax.ShapeDtypeStruct((1, D), F32), jax.ShapeDtypeStruct((D, D), F32), jax.ShapeDtypeStruct((D, D), BF16)),
        in_specs=[col(D), col(D)] + _band_specs_t(nb, ident) + _band_specs_t(nb, ident) + [
            row, row, _resident((D, D)), _resident((1, D)), pl.BlockSpec(memory_space=pltpu.SMEM)],
        out_specs=(row, row, col(D), col(N_HEADS), pl.BlockSpec((1, 1), lambda i: (0, 0)),
                   pl.BlockSpec((1, D), lambda i: (0, 0)), whole, whole),
        scratch_shapes=[pltpu.VMEM((KV_W, TQ + 2 * BLK), BF16), pltpu.VMEM((KV_W, TQ + 2 * BLK), BF16),
                        pltpu.VMEM((D, TQ), F32)],
        compiler_params=_params(56),
    )(qt, gatet, kt, kt, kt, vt, vt, vt, x1, tgt, wout, gf, sink)


def _l1_attn_bwd(dx2b, wout, qt, kt, vt, gatet, att, lse, sink):
    seq = dx2b.shape[0]
    nq, nb = seq // TQ, seq // BLK

    def body(dx_ref, wo_ref, q_ref, gate_ref, kp_ref, k_ref, kn_ref, vp_ref, v_ref, vn_ref, att_ref, lse_ref, sink_ref,
             dq_ref, dgate_ref, dk_ref, dv_ref, dsink_ref, kbuf, vbuf, dkacc, dvacc, dat_scr, delta_scr, dsacc):
        i = pl.program_id(0)

        @pl.when(i == 0)
        def _():
            dkacc[...] = jnp.zeros_like(dkacc)
            dvacc[...] = jnp.zeros_like(dvacc)
            dsacc[...] = jnp.zeros_like(dsacc)

        @pl.when(i > 0)
        def _():
            for acc in (dkacc, dvacc):
                acc[:, 0:2 * BLK] = acc[:, TQ:TQ + 2 * BLK]
                acc[:, 2 * BLK:2 * BLK + TQ] = jnp.zeros((KV_W, TQ), F32)

        @pl.when(i < nq)
        def _():
            _fill_band(kbuf, kp_ref, k_ref, kn_ref)
            _fill_band(vbuf, vp_ref, v_ref, vn_ref)
            dyt = _nt(wo_ref[...], dx_ref[...])
            sg, dsg = _silu_and_grad(gate_ref[...].astype(F32))
            attf = att_ref[...].astype(F32)
            dat = dyt * sg
            dat_scr[...] = dat.astype(BF16)
            dgate_ref[...] = (dyt * attf * dsg).astype(BF16)
            dl = dat * attf
            delta_scr[...] = jnp.concatenate(
                [jnp.sum(dl[h * HD:(h + 1) * HD, :], axis=0, keepdims=True) for h in range(N_HEADS)], axis=0)
            ones_rows = _ones_rows(2, 3 * BLK)
            groups = [list(range(0, N_HEADS, HPP))[g:g + BWD_GROUP] for g in range(0, N_HEADS // HPP, BWD_GROUP)]
            work = [(j, grp) for j in range(TQ // BLK) for grp in groups]

            def scores(j, passes):
                c0 = j * BLK
                st = dict(c0=c0, passes=passes, bias=_band_bias_t(i * (TQ // BLK) + j, nb))
                st["kv_rows"] = [slice(h0 // GQA * HD, (h0 // GQA + 1) * HD) for h0 in passes]
                st["q4s"] = [_heads_t(q_ref, h0, c0) for h0 in passes]
                st["do4s"] = [_heads_t(dat_scr, h0, c0) for h0 in passes]
                st["lse4s"] = [_row4(lse_ref, h0, c0) for h0 in passes]
                st["delta4s"] = [_row4(delta_scr, h0, c0) for h0 in passes]
                st["kths"] = [kbuf[rows, c0:c0 + 3 * BLK] for rows in st["kv_rows"]]
                st["sts"] = [_tn(jnp.concatenate([kth, ones_rows], axis=0),
                                 jnp.concatenate([q4, _minus_rows(lse4)], axis=0))
                             for kth, q4, lse4 in zip(st["kths"], st["q4s"], st["lse4s"])]
                st["dpds"] = [_tn(jnp.concatenate([vbuf[rows, c0:c0 + 3 * BLK], ones_rows], axis=0),
                                  jnp.concatenate([do4, _minus_rows(delta4)], axis=0))
                              for rows, do4, delta4 in zip(st["kv_rows"], st["do4s"], st["delta4s"])]
                return st

            def elementwise(st):
                st["ps"] = [jnp.exp(_masked(s_, st["bias"])) for s_ in st["sts"]]
                st["dss"] = [(p * dpd).astype(BF16) for p, dpd in zip(st["ps"], st["dpds"])]

            def gradients(st):
                c0 = st["c0"]
                dq4s = [_mm(kth, ds) * SCALE for kth, ds in zip(st["kths"], st["dss"])]
                dks = [_nt(q4, ds) for q4, ds in zip(st["q4s"], st["dss"])]
                dvs = [_nt(do4, p.astype(BF16)) for do4, p in zip(st["do4s"], st["ps"])]
                for h0, rows, dq4, dk, dv, lse4, delta4 in zip(st["passes"], st["kv_rows"], dq4s, dks, dvs, st["lse4s"],
                                                               st["delta4s"]):
                    dkacc[rows, c0:c0 + 3 * BLK] += dk
                    dvacc[rows, c0:c0 + 3 * BLK] += dv
                    dsk = -jnp.exp(_sink_row(sink_ref, h0) - lse4) * delta4
                    for g in range(HPP):
                        h = h0 + g
                        dq_ref[h * HD:(h + 1) * HD, c0:c0 + BLK] = dq4[:, g * BLK:(g + 1) * BLK].astype(BF16)
                        dsacc[h:h + 1, :] += dsk[:, g * BLK:(g + 1) * BLK]

            ahead = [scores(*w) for w in work[:BWD_AHEAD]]
            for n in range(len(work)):
                if n + BWD_AHEAD < len(work):
                    ahead.append(scores(*work[n + BWD_AHEAD]))
                state = ahead.pop(0)
                elementwise(state)
                gradients(state)

        dk_ref[...] = dkacc[:, 0:TQ].astype(BF16)
        dv_ref[...] = dvacc[:, 0:TQ].astype(BF16)

        @pl.when(i == nq)
        def _():
            dsink_ref[...] = jnp.broadcast_to(jnp.sum(dsacc[...], axis=1, keepdims=True), (N_HEADS, LANES))

    clamp = lambda i: jnp.minimum(i, nq - 1)
    row = pl.BlockSpec((TQ, D), lambda i: (clamp(i), 0))
    col = lambda rows: pl.BlockSpec((rows, TQ), lambda i: (0, clamp(i)))
    pad = pl.BlockSpec((KV_W, TQ), lambda i: (0, i))
    return pl.pallas_call(
        body, grid=(nq + 1,), name="l1_attn_bwd",
        out_shape=(jax.ShapeDtypeStruct((D, seq), BF16), jax.ShapeDtypeStruct((D, seq), BF16),
                   jax.ShapeDtypeStruct((KV_W, seq + TQ), BF16), jax.ShapeDtypeStruct((KV_W, seq + TQ), BF16),
                   jax.ShapeDtypeStruct((N_HEADS, LANES), F32)),
        in_specs=[row, _resident((D, D)), col(D), col(D)] + _band_specs_t(nb, clamp) + _band_specs_t(nb, clamp) + [
            col(D), col(N_HEADS), pl.BlockSpec(memory_space=pltpu.SMEM)],
        out_specs=(col(D), col(D), pad, pad, pl.BlockSpec((N_HEADS, LANES), lambda i: (0, 0))),
        scratch_shapes=[pltpu.VMEM((KV_W, TQ + 2 * BLK), BF16), pltpu.VMEM((KV_W, TQ + 2 * BLK), BF16),
                        pltpu.VMEM((KV_W, TQ + 2 * BLK), F32), pltpu.VMEM((KV_W, TQ + 2 * BLK), F32),
                        pltpu.VMEM((D, TQ), BF16), pltpu.VMEM((N_HEADS, TQ), F32), pltpu.VMEM((N_HEADS, LANES), F32)],
        compiler_params=_params(56),
    )(dx2b, wout, qt, gatet, kt, kt, kt, vt, vt, vt, att, lse, sink)


def _l1_in_proj_bwd(dq_r, dk_r, dv, dgate, cos_t, sin_t, w_t, x1, g1, dx2):
    seq = x1.shape[0]
    tm = 512

    def body(dq_ref, dk_ref, dv_ref, dg_ref, c_ref, s_ref, w_ref, x_ref, g_ref, dres_ref,
             dx_ref, dxb_ref, dz_ref, dn_ref):
        @pl.when(pl.program_id(0) == 0)
        def _():
            dn_ref[...] = jnp.zeros_like(dn_ref)

        c, s = c_ref[...], s_ref[...]
        dq = _rope_t(dq_ref[...].astype(F32), c, s, N_HEADS, -1).astype(BF16)
        dk = _rope_t(dk_ref[...].astype(F32), c, s, N_KV, -1).astype(BF16)
        dz = jnp.concatenate([dq, dk, dv_ref[...], dg_ref[...]], axis=0)
        dz_ref[...] = dz
        dh = _tn(dz, w_ref[...])
        xf = x_ref[...]
        r = lax.rsqrt(jnp.mean(xf * xf, axis=1, keepdims=True) + EPS)
        xn = xf * r
        dn_ref[...] += jnp.sum(dh * xn, axis=0, keepdims=True)
        dxn = dh * g_ref[...]
        dx = dres_ref[...] + r * (dxn - xn * jnp.mean(dxn * xn, axis=1, keepdims=True))
        dx_ref[...] = dx
        dxb_ref[...] = dx.astype(BF16)

    row = pl.BlockSpec((tm, D), lambda i: (i, 0))
    col = lambda rows: pl.BlockSpec((rows, tm), lambda i: (0, i))
    return pl.pallas_call(
        body, grid=(seq // tm,), name="l1_in_proj_bwd",
        out_shape=(jax.ShapeDtypeStruct((seq, D), F32), jax.ShapeDtypeStruct((seq, D), BF16),
                   jax.ShapeDtypeStruct((MIX1_IN, seq), BF16), jax.ShapeDtypeStruct((1, D), F32)),
        in_specs=[col(D), col(KV_W), col(KV_W), col(D), col(ROT_HALF), col(ROT_HALF), _resident((MIX1_IN, D)), row,
                  _resident((1, D)), row],
        out_specs=(row, row, col(MIX1_IN), pl.BlockSpec((1, D), lambda i: (0, 0))),
        compiler_params=_params(48),
    )(dq_r, dk_r, dv, dgate, cos_t, sin_t, w_t, x1, g1, dx2)


def _l0_mix_bwd(dx1b, wout, za, bx, bg, ws, ws_t, bias, gv, wg, wg_t, scale):
    seq = dx1b.shape[0]
    ts = 256
    n_tiles = seq // ts

    def body(dx_ref, wo_ref, za_ref, bx_ref, bxp_ref, bxn_ref, bg_ref, ws_ref, wst_ref, bias_ref, gv_ref, wg_ref,
             wgt_ref, sc_ref,
             dz_ref, dp_ref, catt_ref, dws_ref, dbias_ref, dgv_ref, dsc_ref, dwg_ref, db_ref, xe_ref, *tmp_refs):
        i = pl.program_id(0)

        @pl.when(i == 0)
        def _():
            for r_ in (dws_ref, dbias_ref, dgv_ref, dsc_ref, dwg_ref, db_ref):
                r_[...] = jnp.zeros_like(r_)

        dxb = dx_ref[...]
        dya = _nt(dxb, wo_ref[0:D, :])
        dyb = _nt(dxb, wo_ref[D:2 * D, :])

        vg, dvg_dz = _gelu_and_grad(za_ref[:, D:2 * D].astype(F32))
        rv = lax.rsqrt(jnp.mean(vg * vg, axis=1, keepdims=True) + EPS)
        vnorm = vg * rv
        gvw = gv_ref[...]
        vnb = (vnorm * gvw).astype(BF16)
        mixed = _spatial_mix(ws_ref, vnb, bias_ref[...], ts)

        _fill_halo(xe_ref, bx_ref[...], bxp_ref, bxn_ref, i, n_tiles, ts)
        pb = _pool_forward(xe_ref, tmp_refs, ts, i * ts, seq).astype(BF16)
        ypre = jnp.concatenate([_mm(pb[:, g * GDIM:(g + 1) * GDIM], wg_ref[g]) for g in range(4)], axis=1)

        u, du = _gelu_and_grad(za_ref[:, 0:D].astype(F32))
        sga, dsga = _silu_and_grad(za_ref[:, 2 * D:3 * D].astype(F32))
        um = u * mixed
        ya = (um * sga).astype(BF16)
        t = dya * sga
        dz_ref[:, 0:D] = (t * mixed * du).astype(BF16)
        dz_ref[:, 2 * D:3 * D] = (dya * um * dsga).astype(BF16)
        dmixed = t * u
        dmb = dmixed.astype(BF16)
        dvn_rows = []
        for c in range(ts // CHUNK):
            rows = slice(c * CHUNK, (c + 1) * CHUNK)
            parts = []
            for h in range(A_GROUPS):
                cols = slice(h * GDIM, (h + 1) * GDIM)
                dws_ref[h] += _nt(dmb[rows, cols], vnb[rows, cols])
                parts.append(_mm(wst_ref[h], dmb[rows, cols]))
            dvn_rows.append(jnp.concatenate(parts, axis=1))

        sc = sc_ref[...]
        y = ypre * sc
        sgb, dsgb = _silu_and_grad(bg_ref[...].astype(F32))
        yb = (y * sgb).astype(BF16)
        dy_b = dyb * sgb
        dz_ref[:, 3 * D:4 * D] = jnp.zeros((ts, D), BF16)
        dz_ref[:, 4 * D:5 * D] = (dyb * y * dsgb).astype(BF16)
        dsc_ref[...] += jnp.sum(dy_b * ypre, axis=0, keepdims=True)
        dypre = (dy_b * sc).astype(BF16)
        dps = []
        for g in range(4):
            cols = slice(g * GDIM, (g + 1) * GDIM)
            dwg_ref[g] += _tn(pb[:, cols], dypre[:, cols])
            dps.append(_mm(dypre[:, cols], wgt_ref[g]))

        dbias = dmixed[0:CHUNK, :]
        for c in range(1, ts // CHUNK):
            dbias = dbias + dmixed[c * CHUNK:(c + 1) * CHUNK, :]
        dbias_ref[...] += dbias
        dvn = jnp.concatenate(dvn_rows, axis=0)
        dgv_ref[...] += jnp.sum(dvn * vnorm, axis=0, keepdims=True)
        dxn = dvn * gvw
        dvg = rv * (dxn - vnorm * jnp.mean(dxn * vnorm, axis=1, keepdims=True))
        dz_ref[:, D:2 * D] = (dvg * dvg_dz).astype(BF16)

        dp_ref[...] = jnp.concatenate(dps, axis=1)
        catt_ref[...] = jnp.concatenate([ya, yb], axis=1).T

        @pl.when(i == n_tiles - 1)
        def _():
            for h in range(A_GROUPS):
                tot = jnp.sum(dbias_ref[:, h * GDIM:(h + 1) * GDIM].T, axis=0, keepdims=True)
                db_ref[pl.ds(h * 8, 8), :] = jnp.broadcast_to(tot, (8, CHUNK))

    prev, nxt = _halo_specs(ts, seq, D)
    row = lambda w_: pl.BlockSpec((ts, w_), lambda i: (i, 0))
    acc = lambda shape: pl.BlockSpec(shape, lambda i: (0,) * len(shape))
    return pl.pallas_call(
        body, grid=(n_tiles,), name="l0_mix_bwd",
        out_shape=(jax.ShapeDtypeStruct((seq, MIX0_IN), BF16), jax.ShapeDtypeStruct((seq, D), F32),
                   jax.ShapeDtypeStruct((2 * D, seq), BF16),
                   jax.ShapeDtypeStruct((4, CHUNK, CHUNK), F32), jax.ShapeDtypeStruct((CHUNK, D), F32),
                   jax.ShapeDtypeStruct((1, D), F32), jax.ShapeDtypeStruct((1, D), F32),
                   jax.ShapeDtypeStruct((4, GDIM, GDIM), F32), jax.ShapeDtypeStruct((32, CHUNK), F32)),
        in_specs=[row(D), _resident((2 * D, D)), row(3 * D), row(D), prev, nxt, row(D), _resident((4, CHUNK, CHUNK)),
                  _resident((4, CHUNK, CHUNK)), _resident((CHUNK, D)), _resident((1, D)), _resident((4, GDIM, GDIM)),
                  _resident((4, GDIM, GDIM)), _resident((1, D))],
        out_specs=(row(MIX0_IN), row(D), pl.BlockSpec((2 * D, ts), lambda i: (0, i)),
                   acc((4, CHUNK, CHUNK)), acc((CHUNK, D)), acc((1, D)), acc((1, D)), acc((4, GDIM, GDIM)),
                   acc((32, CHUNK))),
        scratch_shapes=_pool_scratch(ts),
        compiler_params=_params(56),
    )(dx1b, wout, za, bx, bx, bx, bg, ws, ws_t, bias, gv, wg, wg_t, scale)


def _l0_pool_bwd(dp, dz):
    seq = dp.shape[0]
    ts = 512
    n_tiles = seq // ts
    ext = ts + 2 * POOL_HALO

    def body(dp_ref, dpp_ref, dpn_ref, dz_ref, out_ref, qe_ref, *tmp_refs):
        i = pl.program_id(0)
        _fill_halo(qe_ref, dp_ref[...], dpp_ref, dpn_ref, i, n_tiles, ts)
        te = i * ts - POOL_HALO + lax.broadcasted_iota(jnp.int32, (ext, 1), 0)
        for gi, w in enumerate(POOL_WINDOWS):
            hw = w // 2
            cols = slice(gi * GDIM, (gi + 1) * GDIM)
            cnt = jnp.maximum(jnp.minimum(te + hw, seq) - jnp.maximum(te - hw, 0), 1).astype(F32)
            qe_ref[pl.ds(0, ext), cols] = qe_ref[pl.ds(0, ext), cols] / cnt
        outs = []
        for gi, w in enumerate(POOL_WINDOWS):
            cols = slice(gi * GDIM, (gi + 1) * GDIM)
            outs.append(_window_sums(qe_ref, tmp_refs, ts, cols, w, 1) - dp_ref[:, cols])
        out_ref[...] = jnp.concatenate(outs, axis=1).astype(BF16)

    prev, nxt = _halo_specs(ts, seq, D)
    row = pl.BlockSpec((ts, D), lambda i: (i, 0))
    return pl.pallas_call(
        body, grid=(n_tiles,), name="l0_pool_bwd",
        out_shape=jax.ShapeDtypeStruct(dz.shape, BF16),
        in_specs=[row, prev, nxt, pl.BlockSpec(memory_space=pl.ANY)],
        out_specs=pl.BlockSpec((ts, D), lambda i: (i, 3)),
        input_output_aliases={3: 0},
        scratch_shapes=_pool_scratch(ts),
        compiler_params=_params(32),
    )(dp, dp, dp, dz)


def _l0_in_proj_bwd(dz, w, x, g0, dx1):
    seq = x.shape[0]
    tm = 512

    def body(dz_ref, w_ref, x_ref, g_ref, dres_ref, dx_ref, dn_ref):
        @pl.when(pl.program_id(0) == 0)
        def _():
            dn_ref[...] = jnp.zeros_like(dn_ref)

        dh = _nt(dz_ref[...], w_ref[...])
        xf = x_ref[...]
        r = lax.rsqrt(jnp.mean(xf * xf, axis=1, keepdims=True) + EPS)
        xn = xf * r
        dn_ref[...] += jnp.sum(dh * xn, axis=0, keepdims=True)
        dxn = dh * g_ref[...]
        dx_ref[...] = dres_ref[...] + r * (dxn - xn * jnp.mean(dxn * xn, axis=1, keepdims=True))

    row = lambda w_: pl.BlockSpec((tm, w_), lambda i: (i, 0))
    return pl.pallas_call(
        body, grid=(seq // tm,), name="l0_in_proj_bwd",
        out_shape=(jax.ShapeDtypeStruct((seq, D), F32), jax.ShapeDtypeStruct((1, D), F32)),
        in_specs=[row(MIX0_IN), _resident((D, MIX0_IN)), row(D), _resident((1, D)), row(D)],
        out_specs=(row(D), pl.BlockSpec((1, D), lambda i: (0, 0))),
        compiler_params=_params(56),
    )(dz, w, x, g0, dx1)


def _dw_matmul(a_t, b, name, b_transposed=False, tn=1024, ts=1024, col_block=None):
    k, seq = a_t.shape
    n = b.shape[0] if b_transposed else b.shape[1]
    tn = min(n, tn)
    assert seq % ts == 0 and n % tn == 0 and (col_block is None or tn % col_block == 0)
    n_s = seq // ts
    per = 1 if col_block is None else tn // col_block

    def body(a_ref, b_ref, o_ref, ob_ref, acc_ref):
        s = pl.program_id(1)

        @pl.when(s == 0)
        def _():
            acc_ref[...] = jnp.zeros_like(acc_ref)

        acc_ref[...] += _nt(a_ref[...], b_ref[...]) if b_transposed else _mm(a_ref[...], b_ref[...])

        @pl.when(s == n_s - 1)
        def _():
            acc = acc_ref[...]
            if col_block is None:
                o_ref[...] = acc
                ob_ref[...] = acc.astype(BF16)
            else:
                for i in range(per):
                    piece = acc[:, i * col_block:(i + 1) * col_block]
                    o_ref[i] = piece
                    ob_ref[i] = piece.astype(BF16)

    b_spec = (pl.BlockSpec((tn, ts), lambda j, s: (j, s)) if b_transposed else pl.BlockSpec((ts, tn), lambda j, s: (s, j)))
    if col_block is None:
        shape, o_spec = (k, n), pl.BlockSpec((k, tn), lambda j, s: (0, j))
    else:
        shape, o_spec = (n // col_block, k, col_block), pl.BlockSpec((per, k, col_block), lambda j, s: (j, 0, 0))
    return pl.pallas_call(
        body, grid=(n // tn, n_s), name=name,
        out_shape=(jax.ShapeDtypeStruct(shape, F32), jax.ShapeDtypeStruct(shape, BF16)),
        in_specs=[pl.BlockSpec((k, ts), lambda j, s: (0, s)), b_spec],
        out_specs=(o_spec, o_spec),
        scratch_shapes=[pltpu.VMEM((k, tn), F32)],
        compiler_params=_params(56, 2),
    )(a_t, b)


ROW_TILES = 8


def _cast_shards(shards):
    n = len(shards)

    def body(*refs):
        for a in range(n):
            refs[n + a][...] = refs[a][...].astype(BF16)

    vm = pl.BlockSpec(memory_space=pltpu.VMEM)
    return pl.pallas_call(body, name="cast_weights", out_shape=[jax.ShapeDtypeStruct(t.shape, BF16) for t in shards],
                          in_specs=[vm] * n, out_specs=[vm] * n, compiler_params=_params(32, 0))(*shards)


def _adamw_math(w, g, m, v):
    m2 = ADAM_B1 * m + (1.0 - ADAM_B1) * g
    v2 = ADAM_B2 * v + (1.0 - ADAM_B2) * (g * g)
    m_hat = m2 / (1.0 - ADAM_B1 ** ADAM_STEP)
    v_hat = v2 / (1.0 - ADAM_B2 ** ADAM_STEP)
    delta = -ADAM_LR * (m_hat / (jnp.sqrt(v_hat) + ADAM_EPS) + ADAM_WD * w)
    return delta, m2, v2


def _final_sum_adamw(g_list, recv_list, me, w_list, m_list, v_list):
    n = len(w_list)

    def body(me_ref, *refs):
        own, recv, w, m, v = (refs[k * n:(k + 1) * n] for k in range(5))
        outs = [refs[(5 + k) * n:(6 + k) * n] for k in range(4)]
        for a in range(n):
            g = own[a][...]
            for k in range(N_DEV - 1):
                g = g + recv[a][k].astype(F32)
            delta, m2, v2 = _adamw_math(w[a][...], g, m[a][...], v[a][...])
            for o_ref, val in zip((outs[0][a], outs[1][a], outs[2][a], outs[3][a]), (g, delta, m2, v2)):
                o_ref[...] = val

    own_specs, flat, wire, shapes = [], [], [], []
    for t in w_list:
        rows, width = t.shape
        tr = rows // ROW_TILES
        own_specs.append(pl.BlockSpec((None, tr, width), lambda i, me: (me[0], i, 0)))
        flat.append(pl.BlockSpec((tr, width), lambda i, me: (i, 0)))
        wire.append(pl.BlockSpec((N_DEV - 1, tr, width), lambda i, me: (0, i, 0)))
        shapes.append(jax.ShapeDtypeStruct((rows, width), F32))
    out = pl.pallas_call(
        body, name="grad_sum_adamw", out_shape=shapes * 4,
        grid_spec=pltpu.PrefetchScalarGridSpec(
            num_scalar_prefetch=1, grid=(ROW_TILES,), in_specs=own_specs + wire + flat * 3, out_specs=flat * 4),
        compiler_params=_params(40),
    )(me, *g_list, *recv_list, *w_list, *m_list, *v_list)
    return [out[k * n:(k + 1) * n] for k in range(4)]


SMALL_NAMES = ("norm_0", "a_v_norm_0", "b_scale_0", "norm_1", "final_norm", "a_spatial_w_0", "a_spatial_b_0", "sink_1")
SMALL_VIEWS = ((8, LANES),) * 5 + ((4 * CHUNK, LANES), (4, LANES), (1, N_HEADS))
SMALL_ROW0 = (0, 8, 16, 24, 32, 40, 552, 560)
SMALL_ROWS = 568


def _small_sum_adamw(early, late, w_list, m_list, v_list):
    n = len(w_list)

    def body(e_ref, l_ref, *refs):
        gtot, first = e_ref[0], l_ref[0]
        for d in range(1, N_DEV):
            gtot = gtot + e_ref[d]
            first = first + l_ref[d]
        for a, ((rows, width), r0) in enumerate(zip(SMALL_VIEWS, SMALL_ROW0)):
            g = first if SMALL_NAMES[a] == "norm_0" else gtot[r0:r0 + rows, 0:width]
            delta, m2, v2 = _adamw_math(refs[a][...], g, refs[n + a][...], refs[2 * n + a][...])
            for k, val in enumerate((g, delta, m2, v2)):
                refs[(3 + k) * n + a][...] = val
        refs[7 * n][...] = gtot[LOSS_ROW:LOSS_ROW + 1, LOSS_LANE:LOSS_LANE + 1]

    vm = pl.BlockSpec(memory_space=pltpu.VMEM)
    shapes = [jax.ShapeDtypeStruct(s, F32) for s in SMALL_VIEWS]
    out = pl.pallas_call(
        body, name="small_sum_adamw", out_shape=shapes * 4 + [jax.ShapeDtypeStruct((1, 1), F32)],
        in_specs=[vm, vm] + [vm] * (3 * n), out_specs=[vm] * (4 * n + 1),
    )(early, late, *w_list, *m_list, *v_list)
    return [out[k * n:(k + 1) * n] for k in range(4)], out[4 * n]


PEER_FLIPS = tuple((fx, fy, fc) for fx in (0, 1) for fy in (0, 1) for fc in (0, 1))[1:]


def _sequencer_all_gather(blks, name, collective_id, concat_rows=False):
    n = len(blks)

    def body(*refs):
        ins, outs = refs[:n], refs[n:2 * n]
        send_sems, recv_sems, local_sems = refs[2 * n:]
        x, y, c = lax.axis_index("x"), lax.axis_index("y"), lax.axis_index("c")
        peers = [(x ^ fx, y ^ fy, c ^ fc) for fx, fy, fc in PEER_FLIPS]
        barrier = pltpu.get_barrier_semaphore()
        for peer in peers:
            pl.semaphore_signal(barrier, inc=1, device_id=peer, device_id_type=MESH)
        pl.semaphore_wait(barrier, len(peers))
        me = 4 * x + 2 * y + c

        def slot(a):
            rows = blks[a].shape[0]
            return outs[a].at[pl.ds(pl.multiple_of(me * rows, 16), rows)] if concat_rows else outs[a].at[me]

        copies = [pltpu.make_async_remote_copy(
            src_ref=ins[a], dst_ref=slot(a), send_sem=send_sems.at[k, a], recv_sem=recv_sems.at[k, a],
            device_id=peer, device_id_type=MESH) for k, peer in enumerate(peers) for a in range(n)]
        mine = [pltpu.make_async_copy(ins[a], slot(a), local_sems.at[a]) for a in range(n)]
        for cp in copies + mine:
            cp.start()
        for cp in copies + mine:
            cp.wait()

    out_shape = (lambda t: (N_DEV * t.shape[0],) + t.shape[1:]) if concat_rows else (lambda t: (N_DEV,) + t.shape)
    return pl.kernel(
        body, out_type=[jax.ShapeDtypeStruct(out_shape(t), t.dtype) for t in blks],
        mesh=plsc.ScalarSubcoreMesh(axis_name="sequencer", num_cores=1), name=name,
        scratch_types=[pltpu.SemaphoreType.DMA((7, n)), pltpu.SemaphoreType.DMA((7, n)), pltpu.SemaphoreType.DMA((n,))],
        compiler_params=pltpu.CompilerParams(collective_id=collective_id),
    )(*blks)


def _sequencer_scatter(g_list, name, collective_id):
    n = len(g_list)

    def body(*refs):
        ins, outs = refs[:n], refs[n:2 * n]
        send_sems, recv_sems = refs[2 * n:]
        x, y, c = lax.axis_index("x"), lax.axis_index("y"), lax.axis_index("c")
        peers = [(x ^ fx, y ^ fy, c ^ fc) for fx, fy, fc in PEER_FLIPS]
        barrier = pltpu.get_barrier_semaphore()
        for peer in peers:
            pl.semaphore_signal(barrier, inc=1, device_id=peer, device_id_type=MESH)
        pl.semaphore_wait(barrier, len(peers))
        copies = [pltpu.make_async_remote_copy(
            src_ref=ins[a].at[4 * px + 2 * py + pc], dst_ref=outs[a].at[k], send_sem=send_sems.at[k, a],
            recv_sem=recv_sems.at[k, a], device_id=(px, py, pc), device_id_type=MESH)
            for k, (px, py, pc) in enumerate(peers) for a in range(n)]
        for cp in copies:
            cp.start()
        for cp in copies:
            cp.wait()

    return pl.kernel(
        body, out_type=[jax.ShapeDtypeStruct((N_DEV - 1,) + g.shape[1:], g.dtype) for g in g_list],
        mesh=plsc.ScalarSubcoreMesh(axis_name="sequencer", num_cores=1), name=name,
        scratch_types=[pltpu.SemaphoreType.DMA((7, n)), pltpu.SemaphoreType.DMA((7, n))],
        compiler_params=pltpu.CompilerParams(collective_id=collective_id),
    )(*g_list)


def _direct_all_gather(blk, name):
    def body(g_ref, out_ref, send_sems, recv_sems, local_sem):
        x, y, c = lax.axis_index("x"), lax.axis_index("y"), lax.axis_index("c")
        me = 4 * x + 2 * y + c
        copies = [pltpu.make_async_remote_copy(
            src_ref=g_ref, dst_ref=out_ref.at[me], send_sem=send_sems.at[k], recv_sem=recv_sems.at[k],
            device_id=(x ^ fx, y ^ fy, c ^ fc), device_id_type=MESH) for k, (fx, fy, fc) in enumerate(PEER_FLIPS)]
        copies.append(pltpu.make_async_copy(g_ref, out_ref.at[me], local_sem))
        for cp in copies:
            cp.start()
        for cp in copies:
            cp.wait()

    any_spec = pl.BlockSpec(memory_space=pl.ANY)
    return pl.pallas_call(
        body, name=name, out_shape=jax.ShapeDtypeStruct((N_DEV,) + blk.shape, blk.dtype),
        in_specs=[any_spec], out_specs=any_spec,
        scratch_shapes=[pltpu.SemaphoreType.DMA((7,)), pltpu.SemaphoreType.DMA((7,)), pltpu.SemaphoreType.DMA],
    )(blk)


def _shard_views(w_in_0, b_group_w_0, w_out_0, w_in_1, w_out_1):
    return [w_in_0, b_group_w_0.reshape(4 * 32, GDIM), w_out_0, w_in_1, w_out_1]


def _small_views(named):
    return [named[name].reshape(view) for name, view in zip(SMALL_NAMES, SMALL_VIEWS)]


LOSS_ROW, LOSS_LANE = 560, N_HEADS


def _pack_small_grads(named, loss_part):
    rows = []
    for name, (r, w) in zip(SMALL_NAMES, SMALL_VIEWS):
        pad_r = -r % 8
        if name == "sink_1":
            t = jnp.concatenate([named[name].reshape(r, w), loss_part], axis=1)
            rows.append(jnp.pad(t, ((0, pad_r), (0, LANES - w - 1))))
        elif name in named:
            rows.append(jnp.pad(named[name].reshape(r, w), ((0, pad_r), (0, LANES - w))))
        else:
            rows.append(jnp.zeros((r + pad_r, LANES), F32))
    return jnp.concatenate(rows, axis=0)


def _device_blocks(t, axis):
    shape = t.shape
    t = t.reshape(shape[:axis] + (N_DEV, shape[axis] // N_DEV) + shape[axis + 1:])
    t = jnp.moveaxis(t, axis, 0)
    return t.reshape(N_DEV, -1, shape[-1] if axis != len(shape) - 1 else shape[-1] // N_DEV)


def kernel(x, norm_0, w_in_0, a_v_norm_0, a_spatial_w_0, a_spatial_b_0, b_group_w_0, b_scale_0, w_out_0, norm_1, w_in_1, sink_1, w_out_1, final_norm, loss_target, m_norm_0, m_w_in_0, m_a_v_norm_0, m_a_spatial_w_0, m_a_spatial_b_0, m_b_group_w_0, m_b_scale_0, m_w_out_0, m_norm_1, m_w_in_1, m_sink_1, m_w_out_1, m_final_norm, v_norm_0, v_w_in_0, v_a_v_norm_0, v_a_spatial_w_0, v_a_spatial_b_0, v_b_group_w_0, v_b_scale_0, v_w_out_0, v_norm_1, v_w_in_1, v_sink_1, v_w_out_1, v_final_norm):
    seq = x.shape[1]
    xs = x.reshape(seq, D)
    tgt = loss_target.reshape(seq, D)
    ax, ay, ac = lax.axis_index("x"), lax.axis_index("y"), lax.axis_index("c")
    me = jnp.reshape(4 * ax + 2 * ay + ac, (1,)).astype(jnp.int32)

    shards = _shard_views(w_in_0, b_group_w_0, w_out_0, w_in_1, w_out_1)
    cast = _cast_shards([shards[0], shards[1], shards[2], w_in_1.T, shards[4]])

    def l1_weights(after):
        blks, _ = lax.optimization_barrier((cast[3:5], after))
        return _sequencer_all_gather(blks, "weights_gather_l1", 2, concat_rows=True)

    blocks, received, early = {}, {}, {}
    collective_ids = {"l1": 3, "out0": 4, "in0": 5}

    def scatter(tag, own_blocks, wire_blocks):
        blocks[tag] = own_blocks
        received[tag] = _sequencer_scatter(wire_blocks, "grad_scatter_" + tag, collective_ids[tag])

    def small_early(named, loss_part):
        early["small"] = _sequencer_all_gather([_pack_small_grads(named, loss_part)], "small_grad_gather", 6)[0]

    grad_x, d_norm_0 = _local_step(xs, tgt, cast[0], cast[1:3], l1_weights, norm_0, a_v_norm_0, a_spatial_w_0,
                                   a_spatial_b_0, b_scale_0, norm_1, sink_1, final_norm, scatter, small_early)

    order = (("in0", 0), ("in0", 1), ("out0", 0), ("l1", 0), ("l1", 1))
    late = _direct_all_gather(d_norm_0.reshape(8, LANES), "norm_grad_gather")
    shards_late, _ = lax.optimization_barrier((shards, grad_x))
    big = _final_sum_adamw([blocks[t][i] for t, i in order], [received[t][i] for t, i in order], me, shards_late,
                           _shard_views(m_w_in_0, m_b_group_w_0, m_w_out_0, m_w_in_1, m_w_out_1),
                           _shard_views(v_w_in_0, v_b_group_w_0, v_w_out_0, v_w_in_1, v_w_out_1))
    weights = dict(norm_0=norm_0, a_v_norm_0=a_v_norm_0, a_spatial_w_0=a_spatial_w_0, a_spatial_b_0=a_spatial_b_0,
                   b_scale_0=b_scale_0, norm_1=norm_1, sink_1=sink_1, final_norm=final_norm)
    m_small = dict(norm_0=m_norm_0, a_v_norm_0=m_a_v_norm_0, a_spatial_w_0=m_a_spatial_w_0, a_spatial_b_0=m_a_spatial_b_0,
                   b_scale_0=m_b_scale_0, norm_1=m_norm_1, sink_1=m_sink_1, final_norm=m_final_norm)
    v_small = dict(norm_0=v_norm_0, a_v_norm_0=v_a_v_norm_0, a_spatial_w_0=v_a_spatial_w_0, a_spatial_b_0=v_a_spatial_b_0,
                   b_scale_0=v_b_scale_0, norm_1=v_norm_1, sink_1=v_sink_1, final_norm=v_final_norm)
    small, loss = _small_sum_adamw(early["small"], late, _small_views(weights), _small_views(m_small),
                                   _small_views(v_small))

    def in_order(kind):
        b = [b_.reshape(s_.shape) for b_, s_ in zip(big[kind], (w_in_0, b_group_w_0, w_out_0, w_in_1, w_out_1))]
        s = {name: t.reshape(weights[name].shape) for name, t in zip(SMALL_NAMES, small[kind])}
        return [s["norm_0"], b[0], s["a_v_norm_0"], s["a_spatial_w_0"], s["a_spatial_b_0"], b[1], s["b_scale_0"], b[2],
                s["norm_1"], b[3], s["sink_1"], b[4], s["final_norm"]]

    return (loss[0, 0], grad_x.reshape(1, seq, D), *in_order(0), *in_order(1), *in_order(2), *in_order(3))


def _local_step(xs, tgt, win0_shard, l0_shards, l1_weights, norm_0, a_v_norm_0, a_spatial_w_0, a_spatial_b_0, b_scale_0,
                norm_1, sink_1, final_norm, scatter, small_early):
    seq = xs.shape[0]
    ws = a_spatial_w_0.astype(BF16)
    ws_t = jnp.swapaxes(ws, 1, 2)
    bias = jnp.repeat(a_spatial_b_0.T, GDIM, axis=1)
    g0, gv, scale, g1, gf = (t.reshape(1, D) for t in (norm_0, a_v_norm_0, b_scale_0, norm_1, final_norm))
    cos_t, sin_t = _rope_tables_t(seq)

    za, bx, bg, h0_t, win0, g_wg, wout0 = _l0_in_proj(xs, g0, win0_shard, l0_shards)
    win1_t, wout1 = l1_weights(za)
    wg = g_wg.reshape(N_DEV, 4, 32, GDIM).transpose(1, 0, 2, 3).reshape(4, GDIM, GDIM)
    wg_t = jnp.swapaxes(wg, 1, 2)
    x1 = _l0_mix_fwd(za, bx, bg, xs, ws, bias, gv, wg, scale, wout0)
    win1_t, wout1, x1 = lax.optimization_barrier((win1_t, wout1, x1))
    qt, kt, vt, gatet, h1_t = _l1_in_proj(x1, g1, win1_t, cos_t, sin_t)
    dx2, dx2b, att, lse, loss_part, d_gf, d_wout1, d_wout1_wire = _l1_attn_fwd(
        qt, kt, vt, gatet, x1, tgt, wout1, gf, sink_1)

    dq_r, dgate, dk_pad, dv_pad, d_sink = _l1_attn_bwd(dx2b, wout1, qt, kt, vt, gatet, att, lse, sink_1)
    dk_r = dk_pad[:, BLK:BLK + seq]
    dv = dv_pad[:, BLK:BLK + seq]
    dx1, dx1b, dz1_t, d_g1 = _l1_in_proj_bwd(dq_r, dk_r, dv, dgate, cos_t, sin_t, win1_t, x1, g1, dx2)
    d_win1, d_win1_wire = _dw_matmul(h1_t, dz1_t, "dw_in_1", b_transposed=True, tn=1280, col_block=MIX1_IN // N_DEV)
    rows = lambda t: t.reshape(N_DEV, t.shape[0] // N_DEV, t.shape[1])
    scatter("l1", [d_win1, rows(d_wout1)], [d_win1_wire, rows(d_wout1_wire)])

    dz0, dp, cat_t, d_ws, _, d_gv, d_scale, d_wg, d_b = _l0_mix_bwd(
        dx1b, wout0, za, bx, bg, ws, ws_t, bias, gv, wg, wg_t, scale)
    dz0 = _l0_pool_bwd(dp, dz0)
    d_win0, d_win0_wire = _dw_matmul(h0_t, dz0, "dw_in_0", tn=1280, col_block=MIX0_IN // N_DEV)
    d_wg_blocks = _device_blocks(d_wg, 1)
    scatter("in0", [d_win0, d_wg_blocks], [d_win0_wire, d_wg_blocks])
    cat_t, _ = lax.optimization_barrier((cat_t, d_win0))
    d_wout0, d_wout0_wire = _dw_matmul(cat_t, dx1b, "dw_out_0")
    scatter("out0", [rows(d_wout0)], [rows(d_wout0_wire)])
    small_early(dict(a_v_norm_0=d_gv, a_spatial_w_0=d_ws, a_spatial_b_0=d_b.reshape(4, 8, CHUNK)[:, 0, :],
                     b_scale_0=d_scale, norm_1=d_g1, sink_1=d_sink[:, 0], final_norm=d_gf), loss_part)
    dz0, _ = lax.optimization_barrier((dz0, d_wout0))
    return _l0_in_proj_bwd(dz0, win0, xs, g0, dx1)
```

```python
import jax
import jax.numpy as jnp
from jax import lax
from jax.experimental import pallas as pl
from jax.experimental.pallas import tpu as pltpu
from jax.experimental.pallas import tpu_sc as plsc

F32 = jnp.float32
BF16 = jnp.bfloat16

D = 1024
EPS = 1e-6
NEG_INF = -1e30
CHUNK = 128
A_GROUPS = 4
POOL_WINDOWS = (2, 4, 8, 16)
POOL_HALO = 8
GDIM = 256
N_HEADS = 16
N_KV = 4
GQA = 4
HD = 64
BLK = 128
ROT_HALF = 8
ROPE_THETA = 500000.0
SCALE = HD ** -0.5
MIX0_IN = 5 * D
MIX1_IN = 2560
KV_W = N_KV * HD
Q_ROWS, K_ROWS, V_ROWS, G_ROWS = (0, D), (D, D + KV_W), (D + KV_W, D + 2 * KV_W), (D + 2 * KV_W, MIX1_IN)
TQ = 512

ADAM_LR = 0.001
ADAM_B1 = 0.9
ADAM_B2 = 0.999
ADAM_EPS = 1e-08
ADAM_WD = 0.01
ADAM_STEP = 10

N_DEV = 8
LANES = 128
MIB = 2 ** 20
MESH = pl.DeviceIdType.MESH


def _params(limit_mib, n_axes=1):
    return pltpu.CompilerParams(vmem_limit_bytes=limit_mib * MIB, dimension_semantics=("arbitrary",) * n_axes)


def _resident(shape):
    nd = len(shape)
    return pl.BlockSpec(shape, lambda *_: (0,) * nd, pipeline_mode=pl.Buffered(1))


def _gelu(x):
    k = 0.7978845608028654
    return 0.5 * x * (1.0 + jnp.tanh(k * (x + 0.044715 * x * x * x)))


def _gelu_and_grad(x):
    k = 0.7978845608028654
    x2 = x * x
    t = jnp.tanh(k * (x + 0.044715 * x * x2))
    g = 0.5 * x * (1.0 + t)
    dg = 0.5 * (1.0 + t) + 0.5 * x * (1.0 - t * t) * (k * (1.0 + 3.0 * 0.044715 * x2))
    return g, dg


def _silu_and_grad(x):
    s = jax.nn.sigmoid(x)
    return x * s, s * (1.0 + x * (1.0 - s))


def _nt(a, b):
    return lax.dot_general(a, b, (((1,), (1,)), ((), ())), preferred_element_type=F32)


def _tn(a, b):
    return lax.dot_general(a, b, (((0,), (0,)), ((), ())), preferred_element_type=F32)


def _mm(a, b):
    return jnp.dot(a, b, preferred_element_type=F32)


def _rope_tables_t(seq):
    inv = ROPE_THETA ** (-jnp.arange(0, 2 * ROT_HALF, 2, dtype=F32) / (2 * ROT_HALF))
    ang = inv[:, None] * jnp.arange(seq, dtype=F32)[None, :]
    return jnp.cos(ang), jnp.sin(ang)


def _rope_t(z, c, s, n_heads, sign):
    parts = []
    for h in range(n_heads):
        b = h * HD
        x1, x2 = z[b:b + ROT_HALF], z[b + ROT_HALF:b + 2 * ROT_HALF]
        if sign > 0:
            parts += [x1 * c - x2 * s, x2 * c + x1 * s]
        else:
            parts += [x1 * c + x2 * s, x2 * c - x1 * s]
        parts.append(z[b + 2 * ROT_HALF:b + HD])
    return jnp.concatenate(parts, axis=0)


N_CHIPS = 4
CHIP_COLS = MIX0_IN // N_CHIPS
IN_PROJ_PIECES = (
    ((0, 0, CHIP_COLS, 0),),
    ((0, CHIP_COLS, CHIP_COLS, 0),),
    ((0, 2 * CHIP_COLS, 3 * D - 2 * CHIP_COLS, 0), (1, 0, 3 * CHIP_COLS - 3 * D, 3 * D - 2 * CHIP_COLS)),
    ((1, 3 * CHIP_COLS - 3 * D, 4 * D - 3 * CHIP_COLS, 0), (2, 0, D, 4 * D - 3 * CHIP_COLS)),
)


def _l0_in_proj(x, g0, w_shard, later_shards):
    seq = x.shape[0]
    tm = 512
    n = seq // tm
    shard_cols = w_shard.shape[1]
    n_arr = 1 + len(later_shards)
    later = range(1, n_arr)
    assert 2 * shard_cols == CHIP_COLS and seq % tm == 0 and n >= 4

    def body(*refs):
        x_ref, g_ref = refs[:2]
        ins = refs[2:2 + n_arr]
        za_ref, bx_ref, bg_ref, ht_ref = refs[2 + n_arr:6 + n_arr]
        gathered = refs[6 + n_arr:6 + 2 * n_arr]
        h_all, w_buf, z32, z16, send_sems, recv_sems, local_sems, load_sems, out_sems = refs[6 + 2 * n_arr:]
        p, i = pl.program_id(0), pl.program_id(1)
        ax, ay, ac = lax.axis_index("x"), lax.axis_index("y"), lax.axis_index("c")
        me, sibling = (ax, ay, ac), (ax, ay, 1 - ac)
        chips = [(ax, ay), (1 - ax, ay), (ax, 1 - ay), (1 - ax, 1 - ay)]
        outs = (za_ref, bx_ref, bg_ref)

        def slot(a, px, py, pc):
            dev = 4 * px + 2 * py + pc
            if a == 0:
                return gathered[0].at[:, pl.ds(pl.multiple_of(dev * shard_cols, LANES), shard_cols)]
            rows = later_shards[a - 1].shape[0]
            return gathered[a].at[pl.ds(pl.multiple_of(dev * rows, 16), rows)]

        def copy(k, a, block, to, from_input=False):
            return pltpu.make_async_remote_copy(
                src_ref=ins[a] if from_input else slot(a, *block), dst_ref=slot(a, *block),
                send_sem=send_sems.at[k, a], recv_sem=recv_sems.at[k, a], device_id=to, device_id_type=MESH)

        def to_sibling(a):
            return copy(0, a, me, sibling, from_input=True)

        def send(j, a):
            return copy(j, a, me, (*chips[j], ac), from_input=True)

        def landed(j, a):
            return copy(j, a, (*chips[j], ac), me)

        def forward(j, a):
            return copy(3 + j, a, (*chips[j], ac), sibling)

        def forwarded(j, a):
            return copy(3 + j, a, (*chips[j], 1 - ac), me)

        def mine(a):
            return pltpu.make_async_copy(ins[a], slot(a, *me), local_sems.at[a])

        def load(chip, q):
            px, py = chip
            cols = pl.ds(pl.multiple_of((2 * px + py) * CHIP_COLS, LANES), CHIP_COLS)
            return pltpu.make_async_copy(gathered[0].at[:, cols], w_buf.at[q % 2], load_sems.at[q % 2])

        def out_copies(q, tile, stage):
            cps = []
            for k, (o, c0, width, z0) in enumerate(IN_PROJ_PIECES[q]):
                src = z32.at[stage, :, pl.ds(z0, width)] if o == 1 else z16.at[stage, :, pl.ds(z0, width)]
                dst = outs[o].at[pl.ds(pl.multiple_of(tile * tm, tm), tm), pl.ds(c0, width)]
                cps.append(pltpu.make_async_copy(src, dst, out_sems.at[stage, k]))
            return cps

        @pl.when((p == 0) & (i == 0))
        def _():
            for a in range(n_arr):
                mine(a).start()
                to_sibling(a).start()
            send(1, 0).start()
            send(2, 0).start()
            copy(0, 0, sibling, me).wait_recv()
            mine(0).wait()
            load(chips[0], 0).start()

        for j in range(1, N_CHIPS):
            @pl.when((p == j - 1) & (i == n - 2))
            def _(j=j):
                landed(j, 0).wait_recv()
                forward(j, 0).start()
                forwarded(j, 0).wait_recv()
                if j == 1:
                    send(1, 0).wait_send()
                    send(2, 0).wait_send()
                    send(3, 0).start()
                    for a in later:
                        for jj in range(1, N_CHIPS):
                            send(jj, a).start()
                load(chips[j], j).start()

        @pl.when((p == N_CHIPS - 1) & (i == n - 4))
        def _():
            for jj in range(1, N_CHIPS):
                for a in later:
                    landed(jj, a).wait_recv()
                    forward(jj, a).start()

        @pl.when(i == 0)
        def _():
            load(chips[0], p).wait()

        @pl.when(p == 0)
        def _():
            xf = x_ref[...]
            r = lax.rsqrt(jnp.mean(xf * xf, axis=1, keepdims=True) + EPS)
            h = (xf * r * g_ref[...]).astype(BF16)
            ht_ref[...] = h.T
            h_all[pl.ds(pl.multiple_of(i * tm, tm), tm), :] = h

        def chip_of_pass(pp):
            return (2 * ax + ay) ^ ((pp >> 1) | ((pp & 1) << 1))

        step = p * n + i
        stage = step % 2
        for q in range(N_CHIPS):
            @pl.when((step >= 2) & (chip_of_pass((step - 2) // n) == q))
            def _(q=q):
                for cp in out_copies(q, (step - 2) % n, stage):
                    cp.wait()

        z32[stage] = _mm(h_all[pl.ds(pl.multiple_of(i * tm, tm), tm), :], w_buf[p % 2])
        z16[stage] = z32[stage].astype(BF16)
        for q in range(N_CHIPS):
            @pl.when(chip_of_pass(p) == q)
            def _(q=q):
                for cp in out_copies(q, i, stage):
                    cp.start()

        last = (p == N_CHIPS - 1) & (i == n - 1)
        for q in range(N_CHIPS):
            @pl.when(last & (chip_of_pass(p) == q))
            def _(q=q):
                for cp in out_copies(q, n - 2, 1 - stage) + out_copies(q, n - 1, stage):
                    cp.wait()

        @pl.when(last)
        def _():
            for a in later:
                copy(0, a, sibling, me).wait_recv()
                for jj in range(1, N_CHIPS):
                    forwarded(jj, a).wait_recv()
                    send(jj, a).wait_send()
                mine(a).wait()
            send(3, 0).wait_send()
            for a in range(n_arr):
                to_sibling(a).wait_send()
                for jj in range(1, N_CHIPS):
                    forward(jj, a).wait_send()

    any_spec = pl.BlockSpec(memory_space=pl.ANY)
    first_pass_tile = lambda p, i: jnp.where(p == 0, i, n - 1)
    return pl.pallas_call(
        body, grid=(N_CHIPS, n), name="l0_in_proj",
        out_shape=[jax.ShapeDtypeStruct((seq, 3 * D), BF16), jax.ShapeDtypeStruct((seq, D), F32),
                   jax.ShapeDtypeStruct((seq, D), BF16), jax.ShapeDtypeStruct((D, seq), BF16),
                   jax.ShapeDtypeStruct((D, MIX0_IN), BF16)]
        + [jax.ShapeDtypeStruct((N_DEV * t.shape[0], t.shape[1]), t.dtype) for t in later_shards],
        in_specs=[pl.BlockSpec((tm, D), lambda p, i: (first_pass_tile(p, i), 0)), _resident((1, D))] + [any_spec] * n_arr,
        out_specs=[any_spec, any_spec, any_spec, pl.BlockSpec((D, tm), lambda p, i: (0, first_pass_tile(p, i)))]
        + [any_spec] * n_arr,
        scratch_shapes=[pltpu.VMEM((seq, D), BF16), pltpu.VMEM((2, D, CHIP_COLS), BF16),
                        pltpu.VMEM((2, tm, CHIP_COLS), F32), pltpu.VMEM((2, tm, CHIP_COLS), BF16),
                        pltpu.SemaphoreType.DMA((7, n_arr)), pltpu.SemaphoreType.DMA((7, n_arr)),
                        pltpu.SemaphoreType.DMA((n_arr,)), pltpu.SemaphoreType.DMA((2,)), pltpu.SemaphoreType.DMA((2, 2))],
        compiler_params=_params(48, 2),
    )(x, g0, w_shard, *later_shards)


POOL_EXT = 40


def _fill_halo(ext_ref, cur, prev_ref, next_ref, i, n_tiles, ts):
    ext_ref[pl.ds(0, POOL_HALO), :] = jnp.where(i > 0, prev_ref[...], 0.0)
    ext_ref[pl.ds(POOL_HALO, ts), :] = cur
    ext_ref[pl.ds(POOL_HALO + ts, POOL_HALO), :] = jnp.where(i < n_tiles - 1, next_ref[...], 0.0)
    ext_ref[pl.ds(2 * POOL_HALO + ts, POOL_EXT - 2 * POOL_HALO), :] = jnp.zeros((POOL_EXT - 2 * POOL_HALO, D), F32)


def _window_sums(src_ref, tmp_refs, ts, cols, w, shift):
    if w == 2:
        return src_ref[pl.ds(POOL_HALO - 1 + shift, ts), cols] + src_ref[pl.ds(POOL_HALO + shift, ts), cols]
    d2, d4, d8 = tmp_refs
    n2, n4, n8 = ts + 32, ts + 24, ts + 16
    d2[pl.ds(0, n2), :] = src_ref[pl.ds(0, n2), cols] + src_ref[pl.ds(1, n2), cols]
    if w == 4:
        return d2[pl.ds(POOL_HALO - 2 + shift, ts), :] + d2[pl.ds(POOL_HALO + shift, ts), :]
    d4[pl.ds(0, n4), :] = d2[pl.ds(0, n4), :] + d2[pl.ds(2, n4), :]
    if w == 8:
        return d4[pl.ds(POOL_HALO - 4 + shift, ts), :] + d4[pl.ds(POOL_HALO + shift, ts), :]
    d8[pl.ds(0, n8), :] = d4[pl.ds(0, n8), :] + d4[pl.ds(4, n8), :]
    return d8[pl.ds(shift, ts), :] + d8[pl.ds(POOL_HALO + shift, ts), :]


def _pool_scratch(ts):
    return [pltpu.VMEM((ts + POOL_EXT, D), F32)] + [pltpu.VMEM((ts + POOL_EXT, GDIM), F32)] * 3


def _pool_forward(xe_ref, tmp_refs, ts, t0, seq):
    tg = t0 + lax.broadcasted_iota(jnp.int32, (ts, 1), 0)
    outs = []
    for gi, w in enumerate(POOL_WINDOWS):
        hw = w // 2
        cols = slice(gi * GDIM, (gi + 1) * GDIM)
        cnt = (jnp.minimum(tg + hw, seq) - jnp.maximum(tg - hw, 0)).astype(F32)
        outs.append(_window_sums(xe_ref, tmp_refs, ts, cols, w, 0) / cnt - xe_ref[pl.ds(POOL_HALO, ts), cols])
    return jnp.concatenate(outs, axis=1)


def _spatial_mix(ws_ref, vnb, bias, ts):
    rows = []
    for c in range(ts // CHUNK):
        vc = vnb[c * CHUNK:(c + 1) * CHUNK, :]
        rows.append(jnp.concatenate(
            [_mm(ws_ref[h], vc[:, h * GDIM:(h + 1) * GDIM]) for h in range(A_GROUPS)], axis=1) + bias)
    return jnp.concatenate(rows, axis=0)


def _halo_specs(ts, seq, width):
    per = ts // POOL_HALO
    last = seq // POOL_HALO - 1
    prev = pl.BlockSpec((POOL_HALO, width), lambda i: (jnp.maximum(i * per - 1, 0), 0))
    nxt = pl.BlockSpec((POOL_HALO, width), lambda i: (jnp.minimum((i + 1) * per, last), 0))
    return prev, nxt


def _l0_mix_fwd(za, bx, bg, x, ws, bias, gv, wg, scale, wout):
    seq = x.shape[0]
    ts = 512
    n_tiles = seq // ts

    def body(za_ref, bx_ref, bxp_ref, bxn_ref, bg_ref, x_ref, ws_ref, bias_ref, gv_ref, wg_ref, sc_ref, wo_ref,
             x1_ref, xe_ref, *tmp_refs):
        i = pl.program_id(0)
        vg = _gelu(za_ref[:, D:2 * D].astype(F32))
        rv = lax.rsqrt(jnp.mean(vg * vg, axis=1, keepdims=True) + EPS)
        vnb = (vg * rv * gv_ref[...]).astype(BF16)
        mixed = _spatial_mix(ws_ref, vnb, bias_ref[...], ts)

        _fill_halo(xe_ref, bx_ref[...], bxp_ref, bxn_ref, i, n_tiles, ts)
        pb = _pool_forward(xe_ref, tmp_refs, ts, i * ts, seq).astype(BF16)
        ypre = jnp.concatenate([_mm(pb[:, g * GDIM:(g + 1) * GDIM], wg_ref[g]) for g in range(4)], axis=1)

        u = _gelu(za_ref[:, 0:D].astype(F32))
        ag = za_ref[:, 2 * D:3 * D].astype(F32)
        ya = (u * mixed * (ag * jax.nn.sigmoid(ag))).astype(BF16)
        out_a = _mm(ya, wo_ref[0:D, :])

        bgf = bg_ref[...].astype(F32)
        yb = (ypre * sc_ref[...] * (bgf * jax.nn.sigmoid(bgf))).astype(BF16)
        x1_ref[...] = x_ref[...] + out_a + _mm(yb, wo_ref[D:2 * D, :])

    prev, nxt = _halo_specs(ts, seq, D)
    row = lambda w: pl.BlockSpec((ts, w), lambda i: (i, 0))
    return pl.pallas_call(
        body, grid=(n_tiles,), name="l0_mix_fwd",
        out_shape=jax.ShapeDtypeStruct((seq, D), F32),
        in_specs=[row(3 * D), row(D), prev, nxt, row(D), row(D), _resident((4, CHUNK, CHUNK)), _resident((CHUNK, D)),
                  _resident((1, D)), _resident((4, GDIM, GDIM)), _resident((1, D)), _resident((2 * D, D))],
        out_specs=row(D),
        scratch_shapes=_pool_scratch(ts),
        compiler_params=_params(56),
    )(za, bx, bx, bx, bg, x, ws, bias, gv, wg, scale, wout)


def _l1_in_proj(x1, g1, w_t, cos_t, sin_t):
    seq = x1.shape[0]
    tm = 512

    def body(x_ref, g_ref, wt_ref, c_ref, s_ref, q_ref, k_ref, v_ref, gate_ref, ht_ref):
        xf = x_ref[...]
        r = lax.rsqrt(jnp.mean(xf * xf, axis=1, keepdims=True) + EPS)
        ht = (xf * r * g_ref[...]).astype(BF16).T
        ht_ref[...] = ht
        c, s = c_ref[...], s_ref[...]
        q_ref[...] = (_rope_t(_mm(wt_ref[Q_ROWS[0]:Q_ROWS[1], :], ht), c, s, N_HEADS, 1) * SCALE).astype(BF16)
        k_ref[...] = _rope_t(_mm(wt_ref[K_ROWS[0]:K_ROWS[1], :], ht), c, s, N_KV, 1).astype(BF16)
        v_ref[...] = _mm(wt_ref[V_ROWS[0]:V_ROWS[1], :], ht).astype(BF16)
        gate_ref[...] = _mm(wt_ref[G_ROWS[0]:G_ROWS[1], :], ht).astype(BF16)

    col = lambda rows: pl.BlockSpec((rows, tm), lambda i: (0, i))
    return pl.pallas_call(
        body, grid=(seq // tm,), name="l1_in_proj",
        out_shape=(jax.ShapeDtypeStruct((D, seq), BF16), jax.ShapeDtypeStruct((KV_W, seq), BF16),
                   jax.ShapeDtypeStruct((KV_W, seq), BF16), jax.ShapeDtypeStruct((D, seq), BF16),
                   jax.ShapeDtypeStruct((D, seq), BF16)),
        in_specs=[pl.BlockSpec((tm, D), lambda i: (i, 0)), _resident((1, D)), _resident((MIX1_IN, D)), col(ROT_HALF),
                  col(ROT_HALF)],
        out_specs=(col(D), col(KV_W), col(KV_W), col(D), col(D)),
        compiler_params=_params(48),
    )(x1, g1, w_t, cos_t, sin_t)


def _band_specs_t(nb, clamp_i):
    per = TQ // BLK
    prev = pl.BlockSpec((KV_W, BLK), lambda i: (0, jnp.maximum(clamp_i(i) * per - 1, 0)))
    cur = pl.BlockSpec((KV_W, TQ), lambda i: (0, clamp_i(i)))
    nxt = pl.BlockSpec((KV_W, BLK), lambda i: (0, jnp.minimum((clamp_i(i) + 1) * per, nb - 1)))
    return [prev, cur, nxt]


def _fill_band(buf, p_ref, c_ref, n_ref):
    buf[:, 0:BLK] = p_ref[...]
    buf[:, BLK:BLK + TQ] = c_ref[...]
    buf[:, BLK + TQ:2 * BLK + TQ] = n_ref[...]


def _band_bias_t(n, nb):
    c = lax.broadcasted_iota(jnp.int32, (BLK, BLK), 0)
    r = lax.broadcasted_iota(jnp.int32, (BLK, BLK), 1)
    first = jnp.where((c >= r) & (n > 0), 0.0, NEG_INF).astype(F32)
    last = jnp.where((c <= r) & (n < nb - 1), 0.0, NEG_INF).astype(F32)
    return jnp.concatenate([first] * HPP, axis=1), jnp.concatenate([last] * HPP, axis=1)


def _masked(st, bias):
    first, last = bias
    return jnp.concatenate([st[0:BLK] + first, st[BLK:2 * BLK], st[2 * BLK:3 * BLK] + last], axis=0)


AUG = 16


def _ones_rows(n_ones, width):
    return (lax.broadcasted_iota(jnp.int32, (AUG, width), 0) < n_ones).astype(BF16)


def _minus_rows(vec):
    hi = vec.astype(BF16).astype(F32)
    lo = vec - hi
    return jnp.concatenate([-hi, -lo, jnp.zeros((AUG - 2, vec.shape[1]), F32)], axis=0).astype(BF16)


HPP = GQA
FWD_GROUP, BWD_GROUP = 2, 1
BWD_AHEAD = 1


def _heads_t(ref, h0, c0):
    return jnp.concatenate([ref[(h0 + g) * HD:(h0 + g + 1) * HD, c0:c0 + BLK] for g in range(HPP)], axis=1)


def _row4(ref, h0, c0):
    return jnp.concatenate([ref[h0 + g:h0 + g + 1, c0:c0 + BLK] for g in range(HPP)], axis=1)


def _sink_row(sink_ref, h0):
    return jnp.concatenate([jnp.full((1, BLK), sink_ref[h0 + g], F32) for g in range(HPP)], axis=1)


def _l1_attn_fwd(qt, kt, vt, gatet, x1, tgt, wout, gf, sink):
    seq = x1.shape[0]
    nq, nb = seq // TQ, seq // BLK

    def body(q_ref, gate_ref, kp_ref, k_ref, kn_ref, vp_ref, v_ref, vn_ref, x1_ref, tgt_ref, wo_ref, gf_ref, sink_ref,
             dx2_ref, dx2b_ref, att_ref, lse_ref, loss_ref, dgf_ref, dwo_ref, dwo_wire_ref, kbuf, vbuf, att_scr):
        i = pl.program_id(0)

        @pl.when(i == 0)
        def _():
            loss_ref[...] = jnp.zeros_like(loss_ref)
            dgf_ref[...] = jnp.zeros_like(dgf_ref)
            dwo_ref[...] = jnp.zeros_like(dwo_ref)

        _fill_band(kbuf, kp_ref, k_ref, kn_ref)
        _fill_band(vbuf, vp_ref, v_ref, vn_ref)
        ones_row = _ones_rows(1, 3 * BLK)
        groups = [list(range(0, N_HEADS, HPP))[g:g + FWD_GROUP] for g in range(0, N_HEADS // HPP, FWD_GROUP)]
        work = [(j, grp) for j in range(TQ // BLK) for grp in groups]

        def scores(j, passes):
            c0 = j * BLK
            bias = _band_bias_t(i * (TQ // BLK) + j, nb)
            st = dict(c0=c0, passes=passes)
            st["kv_rows"] = [slice(h0 // GQA * HD, (h0 // GQA + 1) * HD) for h0 in passes]
            st["sts"] = [_masked(_tn(kbuf[rows, c0:c0 + 3 * BLK], _heads_t(q_ref, h0, c0)), bias)
                         for h0, rows in zip(passes, st["kv_rows"])]
            return st

        def softmaxes(st):
            st["sks"] = [_sink_row(sink_ref, h0) for h0 in st["passes"]]
            st["ms"] = [jnp.maximum(jnp.max(s_, axis=0, keepdims=True), sk) for s_, sk in zip(st["sts"], st["sks"])]
            st["ps"] = [jnp.exp(s_ - m).astype(BF16) for s_, m in zip(st["sts"], st["ms"])]

        def values(st):
            c0, passes = st["c0"], st["passes"]
            pvs = [_mm(jnp.concatenate([vbuf[rows, c0:c0 + 3 * BLK], ones_row], axis=0), p)
                   for rows, p in zip(st["kv_rows"], st["ps"])]
            lse_rows = []
            for h0, pv, m, sk in zip(passes, pvs, st["ms"], st["sks"]):
                den = pv[HD:HD + 1, :] + jnp.exp(sk - m)
                ot = pv[0:HD, :] / den
                lse = m + jnp.log(den)
                for g in range(HPP):
                    h = h0 + g
                    att_scr[h * HD:(h + 1) * HD, c0:c0 + BLK] = ot[:, g * BLK:(g + 1) * BLK]
                    lse_rows.append(lse[:, g * BLK:(g + 1) * BLK])
            lse_ref[passes[0]:passes[0] + len(lse_rows), c0:c0 + BLK] = jnp.concatenate(lse_rows, axis=0)

        state = scores(*work[0])
        for nxt in work[1:] + [None]:
            following = scores(*nxt) if nxt is not None else None
            softmaxes(state)
            values(state)
            state = following

        att = att_scr[...]
        gate = gate_ref[...].astype(F32)
        yt = (att * (gate * jax.nn.sigmoid(gate))).astype(BF16)
        att_ref[...] = att.astype(BF16)
        x2 = x1_ref[...] + _mm(yt.T, wo_ref[...])
        r = lax.rsqrt(jnp.mean(x2 * x2, axis=1, keepdims=True) + EPS)
        xn = x2 * r
        diff = xn * gf_ref[...] - tgt_ref[...]
        loss_ref[...] += 0.5 * jnp.sum(jnp.mean(diff * diff, axis=1, keepdims=True), axis=0, keepdims=True)
        dout = diff * (1.0 / D)
        dgf_ref[...] += jnp.sum(dout * xn, axis=0, keepdims=True)
        dxn = dout * gf_ref[...]
        dx2 = r * (dxn - xn * jnp.mean(dxn * xn, axis=1, keepdims=True))
        dx2_ref[...] = dx2
        dx2b = dx2.astype(BF16)
        dx2b_ref[...] = dx2b
        dwo_ref[...] += _mm(yt, dx2b)

        @pl.when(i == nq - 1)
        def _():
            dwo_wire_ref[...] = dwo_ref[...].astype(BF16)

    ident = lambda i: i
    row = pl.BlockSpec((TQ, D), lambda i: (i, 0))
    col = lambda rows: pl.BlockSpec((rows, TQ), lambda i: (0, i))
    whole = pl.BlockSpec((D, D), lambda i: (0, 0))
    return pl.pallas_call(
        body, grid=(nq,), name="l1_attn_fwd",
        out_shape=(jax.ShapeDtypeStruct((seq, D), F32), jax.ShapeDtypeStruct((seq, D), BF16),
                   jax.ShapeDtypeStruct((D, seq), BF16),
                   jax.ShapeDtypeStruct((N_HEADS, seq), F32), jax.ShapeDtypeStruct((1, 1), F32),
                   jax.ShapeDtypeStruct((1, D), F32), jax.ShapeDtypeStruct((D, D), F32), jax.ShapeDtypeStruct((D, D), BF16)),
        in_specs=[col(D), col(D)] + _band_specs_t(nb, ident) + _band_specs_t(nb, ident) + [
            row, row, _resident((D, D)), _resident((1, D)), pl.BlockSpec(memory_space=pltpu.SMEM)],
        out_specs=(row, row, col(D), col(N_HEADS), pl.BlockSpec((1, 1), lambda i: (0, 0)),
                   pl.BlockSpec((1, D), lambda i: (0, 0)), whole, whole),
        scratch_shapes=[pltpu.VMEM((KV_W, TQ + 2 * BLK), BF16), pltpu.VMEM((KV_W, TQ + 2 * BLK), BF16),
                        pltpu.VMEM((D, TQ), F32)],
        compiler_params=_params(56),
    )(qt, gatet, kt, kt, kt, vt, vt, vt, x1, tgt, wout, gf, sink)


def _l1_attn_bwd(dx2b, wout, qt, kt, vt, gatet, att, lse, sink):
    seq = dx2b.shape[0]
    nq, nb = seq // TQ, seq // BLK

    def body(dx_ref, wo_ref, q_ref, gate_ref, kp_ref, k_ref, kn_ref, vp_ref, v_ref, vn_ref, att_ref, lse_ref, sink_ref,
             dq_ref, dgate_ref, dk_ref, dv_ref, dsink_ref, kbuf, vbuf, dkacc, dvacc, dat_scr, delta_scr, dsacc):
        i = pl.program_id(0)

        @pl.when(i == 0)
        def _():
            dkacc[...] = jnp.zeros_like(dkacc)
            dvacc[...] = jnp.zeros_like(dvacc)
            dsacc[...] = jnp.zeros_like(dsacc)

        @pl.when(i > 0)
        def _():
            for acc in (dkacc, dvacc):
                acc[:, 0:2 * BLK] = acc[:, TQ:TQ + 2 * BLK]
                acc[:, 2 * BLK:2 * BLK + TQ] = jnp.zeros((KV_W, TQ), F32)

        @pl.when(i < nq)
        def _():
            _fill_band(kbuf, kp_ref, k_ref, kn_ref)
            _fill_band(vbuf, vp_ref, v_ref, vn_ref)
            dyt = _nt(wo_ref[...], dx_ref[...])
            sg, dsg = _silu_and_grad(gate_ref[...].astype(F32))
            attf = att_ref[...].astype(F32)
            dat = dyt * sg
            dat_scr[...] = dat.astype(BF16)
            dgate_ref[...] = (dyt * attf * dsg).astype(BF16)
            dl = dat * attf
            delta_scr[...] = jnp.concatenate(
                [jnp.sum(dl[h * HD:(h + 1) * HD, :], axis=0, keepdims=True) for h in range(N_HEADS)], axis=0)
            ones_rows = _ones_rows(2, 3 * BLK)
            groups = [list(range(0, N_HEADS, HPP))[g:g + BWD_GROUP] for g in range(0, N_HEADS // HPP, BWD_GROUP)]
            work = [(j, grp) for j in range(TQ // BLK) for grp in groups]

            def scores(j, passes):
                c0 = j * BLK
                st = dict(c0=c0, passes=passes, bias=_band_bias_t(i * (TQ // BLK) + j, nb))
                st["kv_rows"] = [slice(h0 // GQA * HD, (h0 // GQA + 1) * HD) for h0 in passes]
                st["q4s"] = [_heads_t(q_ref, h0, c0) for h0 in passes]
                st["do4s"] = [_heads_t(dat_scr, h0, c0) for h0 in passes]
                st["lse4s"] = [_row4(lse_ref, h0, c0) for h0 in passes]
                st["delta4s"] = [_row4(delta_scr, h0, c0) for h0 in passes]
                st["kths"] = [kbuf[rows, c0:c0 + 3 * BLK] for rows in st["kv_rows"]]
                st["sts"] = [_tn(jnp.concatenate([kth, ones_rows], axis=0),
                                 jnp.concatenate([q4, _minus_rows(lse4)], axis=0))
                             for kth, q4, lse4 in zip(st["kths"], st["q4s"], st["lse4s"])]
                st["dpds"] = [_tn(jnp.concatenate([vbuf[rows, c0:c0 + 3 * BLK], ones_rows], axis=0),
                                  jnp.concatenate([do4, _minus_rows(delta4)], axis=0))
                              for rows, do4, delta4 in zip(st["kv_rows"], st["do4s"], st["delta4s"])]
                return st

            def elementwise(st):
                st["ps"] = [jnp.exp(_masked(s_, st["bias"])) for s_ in st["sts"]]
                st["dss"] = [(p * dpd).astype(BF16) for p, dpd in zip(st["ps"], st["dpds"])]

            def gradients(st):
                c0 = st["c0"]
                dq4s = [_mm(kth, ds) * SCALE for kth, ds in zip(st["kths"], st["dss"])]
                dks = [_nt(q4, ds) for q4, ds in zip(st["q4s"], st["dss"])]
                dvs = [_nt(do4, p.astype(BF16)) for do4, p in zip(st["do4s"], st["ps"])]
                for h0, rows, dq4, dk, dv, lse4, delta4 in zip(st["passes"], st["kv_rows"], dq4s, dks, dvs, st["lse4s"],
                                                               st["delta4s"]):
                    dkacc[rows, c0:c0 + 3 * BLK] += dk
                    dvacc[rows, c0:c0 + 3 * BLK] += dv
                    dsk = -jnp.exp(_sink_row(sink_ref, h0) - lse4) * delta4
                    for g in range(HPP):
                        h = h0 + g
                        dq_ref[h * HD:(h + 1) * HD, c0:c0 + BLK] = dq4[:, g * BLK:(g + 1) * BLK].astype(BF16)
                        dsacc[h:h + 1, :] += dsk[:, g * BLK:(g + 1) * BLK]

            ahead = [scores(*w) for w in work[:BWD_AHEAD]]
            for n in range(len(work)):
                if n + BWD_AHEAD < len(work):
                    ahead.append(scores(*work[n + BWD_AHEAD]))
                state = ahead.pop(0)
                elementwise(state)
                gradients(state)

        dk_ref[...] = dkacc[:, 0:TQ].astype(BF16)
        dv_ref[...] = dvacc[:, 0:TQ].astype(BF16)

        @pl.when(i == nq)
        def _():
            dsink_ref[...] = jnp.broadcast_to(jnp.sum(dsacc[...], axis=1, keepdims=True), (N_HEADS, LANES))

    clamp = lambda i: jnp.minimum(i, nq - 1)
    row = pl.BlockSpec((TQ, D), lambda i: (clamp(i), 0))
    col = lambda rows: pl.BlockSpec((rows, TQ), lambda i: (0, clamp(i)))
    pad = pl.BlockSpec((KV_W, TQ), lambda i: (0, i))
    return pl.pallas_call(
        body, grid=(nq + 1,), name="l1_attn_bwd",
        out_shape=(jax.ShapeDtypeStruct((D, seq), BF16), jax.ShapeDtypeStruct((D, seq), BF16),
                   jax.ShapeDtypeStruct((KV_W, seq + TQ), BF16), jax.ShapeDtypeStruct((KV_W, seq + TQ), BF16),
                   jax.ShapeDtypeStruct((N_HEADS, LANES), F32)),
        in_specs=[row, _resident((D, D)), col(D), col(D)] + _band_specs_t(nb, clamp) + _band_specs_t(nb, clamp) + [
            col(D), col(N_HEADS), pl.BlockSpec(memory_space=pltpu.SMEM)],
        out_specs=(col(D), col(D), pad, pad, pl.BlockSpec((N_HEADS, LANES), lambda i: (0, 0))),
        scratch_shapes=[pltpu.VMEM((KV_W, TQ + 2 * BLK), BF16), pltpu.VMEM((KV_W, TQ + 2 * BLK), BF16),
                        pltpu.VMEM((KV_W, TQ + 2 * BLK), F32), pltpu.VMEM((KV_W, TQ + 2 * BLK), F32),
                        pltpu.VMEM((D, TQ), BF16), pltpu.VMEM((N_HEADS, TQ), F32), pltpu.VMEM((N_HEADS, LANES), F32)],
        compiler_params=_params(56),
    )(dx2b, wout, qt, gatet, kt, kt, kt, vt, vt, vt, att, lse, sink)


def _l1_in_proj_bwd(dq_r, dk_r, dv, dgate, cos_t, sin_t, w_t, x1, g1, dx2):
    seq = x1.shape[0]
    tm = 512

    def body(dq_ref, dk_ref, dv_ref, dg_ref, c_ref, s_ref, w_ref, x_ref, g_ref, dres_ref,
             dx_ref, dxb_ref, dz_ref, dn_ref):
        @pl.when(pl.program_id(0) == 0)
        def _():
            dn_ref[...] = jnp.zeros_like(dn_ref)

        c, s = c_ref[...], s_ref[...]
        dq = _rope_t(dq_ref[...].astype(F32), c, s, N_HEADS, -1).astype(BF16)
        dk = _rope_t(dk_ref[...].astype(F32), c, s, N_KV, -1).astype(BF16)
        dz = jnp.concatenate([dq, dk, dv_ref[...], dg_ref[...]], axis=0)
        dz_ref[...] = dz
        dh = _tn(dz, w_ref[...])
        xf = x_ref[...]
        r = lax.rsqrt(jnp.mean(xf * xf, axis=1, keepdims=True) + EPS)
        xn = xf * r
        dn_ref[...] += jnp.sum(dh * xn, axis=0, keepdims=True)
        dxn = dh * g_ref[...]
        dx = dres_ref[...] + r * (dxn - xn * jnp.mean(dxn * xn, axis=1, keepdims=True))
        dx_ref[...] = dx
        dxb_ref[...] = dx.astype(BF16)

    row = pl.BlockSpec((tm, D), lambda i: (i, 0))
    col = lambda rows: pl.BlockSpec((rows, tm), lambda i: (0, i))
    return pl.pallas_call(
        body, grid=(seq // tm,), name="l1_in_proj_bwd",
        out_shape=(jax.ShapeDtypeStruct((seq, D), F32), jax.ShapeDtypeStruct((seq, D), BF16),
                   jax.ShapeDtypeStruct((MIX1_IN, seq), BF16), jax.ShapeDtypeStruct((1, D), F32)),
        in_specs=[col(D), col(KV_W), col(KV_W), col(D), col(ROT_HALF), col(ROT_HALF), _resident((MIX1_IN, D)), row,
                  _resident((1, D)), row],
        out_specs=(row, row, col(MIX1_IN), pl.BlockSpec((1, D), lambda i: (0, 0))),
        compiler_params=_params(48),
    )(dq_r, dk_r, dv, dgate, cos_t, sin_t, w_t, x1, g1, dx2)


def _l0_mix_bwd(dx1b, wout, za, bx, bg, ws, ws_t, bias, gv, wg, wg_t, scale):
    seq = dx1b.shape[0]
    ts = 256
    n_tiles = seq // ts

    def body(dx_ref, wo_ref, za_ref, bx_ref, bxp_ref, bxn_ref, bg_ref, ws_ref, wst_ref, bias_ref, gv_ref, wg_ref,
             wgt_ref, sc_ref,
             dz_ref, dp_ref, catt_ref, dws_ref, dbias_ref, dgv_ref, dsc_ref, dwg_ref, db_ref, xe_ref, *tmp_refs):
        i = pl.program_id(0)

        @pl.when(i == 0)
        def _():
            for r_ in (dws_ref, dbias_ref, dgv_ref, dsc_ref, dwg_ref, db_ref):
                r_[...] = jnp.zeros_like(r_)

        dxb = dx_ref[...]
        dya = _nt(dxb, wo_ref[0:D, :])
        dyb = _nt(dxb, wo_ref[D:2 * D, :])

        vg, dvg_dz = _gelu_and_grad(za_ref[:, D:2 * D].astype(F32))
        rv = lax.rsqrt(jnp.mean(vg * vg, axis=1, keepdims=True) + EPS)
        vnorm = vg * rv
        gvw = gv_ref[...]
        vnb = (vnorm * gvw).astype(BF16)
        mixed = _spatial_mix(ws_ref, vnb, bias_ref[...], ts)

        _fill_halo(xe_ref, bx_ref[...], bxp_ref, bxn_ref, i, n_tiles, ts)
        pb = _pool_forward(xe_ref, tmp_refs, ts, i * ts, seq).astype(BF16)
        ypre = jnp.concatenate([_mm(pb[:, g * GDIM:(g + 1) * GDIM], wg_ref[g]) for g in range(4)], axis=1)

        u, du = _gelu_and_grad(za_ref[:, 0:D].astype(F32))
        sga, dsga = _silu_and_grad(za_ref[:, 2 * D:3 * D].astype(F32))
        um = u * mixed
        ya = (um * sga).astype(BF16)
        t = dya * sga
        dz_ref[:, 0:D] = (t * mixed * du).astype(BF16)
        dz_ref[:, 2 * D:3 * D] = (dya * um * dsga).astype(BF16)
        dmixed = t * u
        dmb = dmixed.astype(BF16)
        dvn_rows = []
        for c in range(ts // CHUNK):
            rows = slice(c * CHUNK, (c + 1) * CHUNK)
            parts = []
            for h in range(A_GROUPS):
                cols = slice(h * GDIM, (h + 1) * GDIM)
                dws_ref[h] += _nt(dmb[rows, cols], vnb[rows, cols])
                parts.append(_mm(wst_ref[h], dmb[rows, cols]))
            dvn_rows.append(jnp.concatenate(parts, axis=1))

        sc = sc_ref[...]
        y = ypre * sc
        sgb, dsgb = _silu_and_grad(bg_ref[...].astype(F32))
        yb = (y * sgb).astype(BF16)
        dy_b = dyb * sgb
        dz_ref[:, 3 * D:4 * D] = jnp.zeros((ts, D), BF16)
        dz_ref[:, 4 * D:5 * D] = (dyb * y * dsgb).astype(BF16)
        dsc_ref[...] += jnp.sum(dy_b * ypre, axis=0, keepdims=True)
        dypre = (dy_b * sc).astype(BF16)
        dps = []
        for g in range(4):
            cols = slice(g * GDIM, (g + 1) * GDIM)
            dwg_ref[g] += _tn(pb[:, cols], dypre[:, cols])
            dps.append(_mm(dypre[:, cols], wgt_ref[g]))

        dbias = dmixed[0:CHUNK, :]
        for c in range(1, ts // CHUNK):
            dbias = dbias + dmixed[c * CHUNK:(c + 1) * CHUNK, :]
        dbias_ref[...] += dbias
        dvn = jnp.concatenate(dvn_rows, axis=0)
        dgv_ref[...] += jnp.sum(dvn * vnorm, axis=0, keepdims=True)
        dxn = dvn * gvw
        dvg = rv * (dxn - vnorm * jnp.mean(dxn * vnorm, axis=1, keepdims=True))
        dz_ref[:, D:2 * D] = (dvg * dvg_dz).astype(BF16)

        dp_ref[...] = jnp.concatenate(dps, axis=1)
        catt_ref[...] = jnp.concatenate([ya, yb], axis=1).T

        @pl.when(i == n_tiles - 1)
        def _():
            for h in range(A_GROUPS):
                tot = jnp.sum(dbias_ref[:, h * GDIM:(h + 1) * GDIM].T, axis=0, keepdims=True)
                db_ref[pl.ds(h * 8, 8), :] = jnp.broadcast_to(tot, (8, CHUNK))

    prev, nxt = _halo_specs(ts, seq, D)
    row = lambda w_: pl.BlockSpec((ts, w_), lambda i: (i, 0))
    acc = lambda shape: pl.BlockSpec(shape, lambda i: (0,) * len(shape))
    return pl.pallas_call(
        body, grid=(n_tiles,), name="l0_mix_bwd",
        out_shape=(jax.ShapeDtypeStruct((seq, MIX0_IN), BF16), jax.ShapeDtypeStruct((seq, D), F32),
                   jax.ShapeDtypeStruct((2 * D, seq), BF16),
                   jax.ShapeDtypeStruct((4, CHUNK, CHUNK), F32), jax.ShapeDtypeStruct((CHUNK, D), F32),
                   jax.ShapeDtypeStruct((1, D), F32), jax.ShapeDtypeStruct((1, D), F32),
                   jax.ShapeDtypeStruct((4, GDIM, GDIM), F32), jax.ShapeDtypeStruct((32, CHUNK), F32)),
        in_specs=[row(D), _resident((2 * D, D)), row(3 * D), row(D), prev, nxt, row(D), _resident((4, CHUNK, CHUNK)),
                  _resident((4, CHUNK, CHUNK)), _resident((CHUNK, D)), _resident((1, D)), _resident((4, GDIM, GDIM)),
                  _resident((4, GDIM, GDIM)), _resident((1, D))],
        out_specs=(row(MIX0_IN), row(D), pl.BlockSpec((2 * D, ts), lambda i: (0, i)),
                   acc((4, CHUNK, CHUNK)), acc((CHUNK, D)), acc((1, D)), acc((1, D)), acc((4, GDIM, GDIM)),
                   acc((32, CHUNK))),
        scratch_shapes=_pool_scratch(ts),
        compiler_params=_params(56),
    )(dx1b, wout, za, bx, bx, bx, bg, ws, ws_t, bias, gv, wg, wg_t, scale)


def _l0_pool_bwd(dp, dz):
    seq = dp.shape[0]
    ts = 512
    n_tiles = seq // ts
    ext = ts + 2 * POOL_HALO

    def body(dp_ref, dpp_ref, dpn_ref, dz_ref, out_ref, qe_ref, *tmp_refs):
        i = pl.program_id(0)
        _fill_halo(qe_ref, dp_ref[...], dpp_ref, dpn_ref, i, n_tiles, ts)
        te = i * ts - POOL_HALO + lax.broadcasted_iota(jnp.int32, (ext, 1), 0)
        for gi, w in enumerate(POOL_WINDOWS):
            hw = w // 2
            cols = slice(gi * GDIM, (gi + 1) * GDIM)
            cnt = jnp.maximum(jnp.minimum(te + hw, seq) - jnp.maximum(te - hw, 0), 1).astype(F32)
            qe_ref[pl.ds(0, ext), cols] = qe_ref[pl.ds(0, ext), cols] / cnt
        outs = []
        for gi, w in enumerate(POOL_WINDOWS):
            cols = slice(gi * GDIM, (gi + 1) * GDIM)
            outs.append(_window_sums(qe_ref, tmp_refs, ts, cols, w, 1) - dp_ref[:, cols])
        out_ref[...] = jnp.concatenate(outs, axis=1).astype(BF16)

    prev, nxt = _halo_specs(ts, seq, D)
    row = pl.BlockSpec((ts, D), lambda i: (i, 0))
    return pl.pallas_call(
        body, grid=(n_tiles,), name="l0_pool_bwd",
        out_shape=jax.ShapeDtypeStruct(dz.shape, BF16),
        in_specs=[row, prev, nxt, pl.BlockSpec(memory_space=pl.ANY)],
        out_specs=pl.BlockSpec((ts, D), lambda i: (i, 3)),
        input_output_aliases={3: 0},
        scratch_shapes=_pool_scratch(ts),
        compiler_params=_params(32),
    )(dp, dp, dp, dz)


def _l0_in_proj_bwd(dz, w, x, g0, dx1):
    seq = x.shape[0]
    tm = 512

    def body(dz_ref, w_ref, x_ref, g_ref, dres_ref, dx_ref, dn_ref):
        @pl.when(pl.program_id(0) == 0)
        def _():
            dn_ref[...] = jnp.zeros_like(dn_ref)

        dh = _nt(dz_ref[...], w_ref[...])
        xf = x_ref[...]
        r = lax.rsqrt(jnp.mean(xf * xf, axis=1, keepdims=True) + EPS)
        xn = xf * r
        dn_ref[...] += jnp.sum(dh * xn, axis=0, keepdims=True)
        dxn = dh * g_ref[...]
        dx_ref[...] = dres_ref[...] + r * (dxn - xn * jnp.mean(dxn * xn, axis=1, keepdims=True))

    row = lambda w_: pl.BlockSpec((tm, w_), lambda i: (i, 0))
    return pl.pallas_call(
        body, grid=(seq // tm,), name="l0_in_proj_bwd",
        out_shape=(jax.ShapeDtypeStruct((seq, D), F32), jax.ShapeDtypeStruct((1, D), F32)),
        in_specs=[row(MIX0_IN), _resident((D, MIX0_IN)), row(D), _resident((1, D)), row(D)],
        out_specs=(row(D), pl.BlockSpec((1, D), lambda i: (0, 0))),
        compiler_params=_params(56),
    )(dz, w, x, g0, dx1)


def _dw_matmul(a_t, b, name, b_transposed=False, tn=1024, ts=1024, col_block=None):
    k, seq = a_t.shape
    n = b.shape[0] if b_transposed else b.shape[1]
    tn = min(n, tn)
    assert seq % ts == 0 and n % tn == 0 and (col_block is None or tn % col_block == 0)
    n_s = seq // ts
    per = 1 if col_block is None else tn // col_block

    def body(a_ref, b_ref, o_ref, ob_ref, acc_ref):
        s = pl.program_id(1)

        @pl.when(s == 0)
        def _():
            acc_ref[...] = jnp.zeros_like(acc_ref)

        acc_ref[...] += _nt(a_ref[...], b_ref[...]) if b_transposed else _mm(a_ref[...], b_ref[...])

        @pl.when(s == n_s - 1)
        def _():
            acc = acc_ref[...]
            if col_block is None:
                o_ref[...] = acc
                ob_ref[...] = acc.astype(BF16)
            else:
                for i in range(per):
                    piece = acc[:, i * col_block:(i + 1) * col_block]
                    o_ref[i] = piece
                    ob_ref[i] = piece.astype(BF16)

    b_spec = (pl.BlockSpec((tn, ts), lambda j, s: (j, s)) if b_transposed else pl.BlockSpec((ts, tn), lambda j, s: (s, j)))
    if col_block is None:
        shape, o_spec = (k, n), pl.BlockSpec((k, tn), lambda j, s: (0, j))
    else:
        shape, o_spec = (n // col_block, k, col_block), pl.BlockSpec((per, k, col_block), lambda j, s: (j, 0, 0))
    return pl.pallas_call(
        body, grid=(n // tn, n_s), name=name,
        out_shape=(jax.ShapeDtypeStruct(shape, F32), jax.ShapeDtypeStruct(shape, BF16)),
        in_specs=[pl.BlockSpec((k, ts), lambda j, s: (0, s)), b_spec],
        out_specs=(o_spec, o_spec),
        scratch_shapes=[pltpu.VMEM((k, tn), F32)],
        compiler_params=_params(56, 2),
    )(a_t, b)


ROW_TILES = 8


def _cast_shards(shards):
    n = len(shards)

    def body(*refs):
        for a in range(n):
            refs[n + a][...] = refs[a][...].astype(BF16)

    vm = pl.BlockSpec(memory_space=pltpu.VMEM)
    return pl.pallas_call(body, name="cast_weights", out_shape=[jax.ShapeDtypeStruct(t.shape, BF16) for t in shards],
                          in_specs=[vm] * n, out_specs=[vm] * n, compiler_params=_params(32, 0))(*shards)


def _adamw_math(w, g, m, v):
    m2 = ADAM_B1 * m + (1.0 - ADAM_B1) * g
    v2 = ADAM_B2 * v + (1.0 - ADAM_B2) * (g * g)
    m_hat = m2 / (1.0 - ADAM_B1 ** ADAM_STEP)
    v_hat = v2 / (1.0 - ADAM_B2 ** ADAM_STEP)
    delta = -ADAM_LR * (m_hat / (jnp.sqrt(v_hat) + ADAM_EPS) + ADAM_WD * w)
    return delta, m2, v2


def _final_sum_adamw(g_list, recv_list, me, w_list, m_list, v_list):
    n = len(w_list)

    def body(me_ref, *refs):
        own, recv, w, m, v = (refs[k * n:(k + 1) * n] for k in range(5))
        outs = [refs[(5 + k) * n:(6 + k) * n] for k in range(4)]
        for a in range(n):
            g = own[a][...]
            for k in range(N_DEV - 1):
                g = g + recv[a][k].astype(F32)
            delta, m2, v2 = _adamw_math(w[a][...], g, m[a][...], v[a][...])
            for o_ref, val in zip((outs[0][a], outs[1][a], outs[2][a], outs[3][a]), (g, delta, m2, v2)):
                o_ref[...] = val

    own_specs, flat, wire, shapes = [], [], [], []
    for t in w_list:
        rows, width = t.shape
        tr = rows // ROW_TILES
        own_specs.append(pl.BlockSpec((None, tr, width), lambda i, me: (me[0], i, 0)))
        flat.append(pl.BlockSpec((tr, width), lambda i, me: (i, 0)))
        wire.append(pl.BlockSpec((N_DEV - 1, tr, width), lambda i, me: (0, i, 0)))
        shapes.append(jax.ShapeDtypeStruct((rows, width), F32))
    out = pl.pallas_call(
        body, name="grad_sum_adamw", out_shape=shapes * 4,
        grid_spec=pltpu.PrefetchScalarGridSpec(
            num_scalar_prefetch=1, grid=(ROW_TILES,), in_specs=own_specs + wire + flat * 3, out_specs=flat * 4),
        compiler_params=_params(40),
    )(me, *g_list, *recv_list, *w_list, *m_list, *v_list)
    return [out[k * n:(k + 1) * n] for k in range(4)]


SMALL_NAMES = ("norm_0", "a_v_norm_0", "b_scale_0", "norm_1", "final_norm", "a_spatial_w_0", "a_spatial_b_0", "sink_1")
SMALL_VIEWS = ((8, LANES),) * 5 + ((4 * CHUNK, LANES), (4, LANES), (1, N_HEADS))
SMALL_ROW0 = (0, 8, 16, 24, 32, 40, 552, 560)
SMALL_ROWS = 568


def _small_sum_adamw(early, late, w_list, m_list, v_list):
    n = len(w_list)

    def body(e_ref, l_ref, *refs):
        gtot, first = e_ref[0], l_ref[0]
        for d in range(1, N_DEV):
            gtot = gtot + e_ref[d]
            first = first + l_ref[d]
        for a, ((rows, width), r0) in enumerate(zip(SMALL_VIEWS, SMALL_ROW0)):
            g = first if SMALL_NAMES[a] == "norm_0" else gtot[r0:r0 + rows, 0:width]
            delta, m2, v2 = _adamw_math(refs[a][...], g, refs[n + a][...], refs[2 * n + a][...])
            for k, val in enumerate((g, delta, m2, v2)):
                refs[(3 + k) * n + a][...] = val
        refs[7 * n][...] = gtot[LOSS_ROW:LOSS_ROW + 1, LOSS_LANE:LOSS_LANE + 1]

    vm = pl.BlockSpec(memory_space=pltpu.VMEM)
    shapes = [jax.ShapeDtypeStruct(s, F32) for s in SMALL_VIEWS]
    out = pl.pallas_call(
        body, name="small_sum_adamw", out_shape=shapes * 4 + [jax.ShapeDtypeStruct((1, 1), F32)],
        in_specs=[vm, vm] + [vm] * (3 * n), out_specs=[vm] * (4 * n + 1),
    )(early, late, *w_list, *m_list, *v_list)
    return [out[k * n:(k + 1) * n] for k in range(4)], out[4 * n]


PEER_FLIPS = tuple((fx, fy, fc) for fx in (0, 1) for fy in (0, 1) for fc in (0, 1))[1:]


def _sequencer_all_gather(blks, name, collective_id, concat_rows=False):
    n = len(blks)

    def body(*refs):
        ins, outs = refs[:n], refs[n:2 * n]
        send_sems, recv_sems, local_sems = refs[2 * n:]
        x, y, c = lax.axis_index("x"), lax.axis_index("y"), lax.axis_index("c")
        peers = [(x ^ fx, y ^ fy, c ^ fc) for fx, fy, fc in PEER_FLIPS]
        barrier = pltpu.get_barrier_semaphore()
        for peer in peers:
            pl.semaphore_signal(barrier, inc=1, device_id=peer, device_id_type=MESH)
        pl.semaphore_wait(barrier, len(peers))
        me = 4 * x + 2 * y + c

        def slot(a):
            rows = blks[a].shape[0]
            return outs[a].at[pl.ds(pl.multiple_of(me * rows, 16), rows)] if concat_rows else outs[a].at[me]

        copies = [pltpu.make_async_remote_copy(
            src_ref=ins[a], dst_ref=slot(a), send_sem=send_sems.at[k, a], recv_sem=recv_sems.at[k, a],
            device_id=peer, device_id_type=MESH) for k, peer in enumerate(peers) for a in range(n)]
        mine = [pltpu.make_async_copy(ins[a], slot(a), local_sems.at[a]) for a in range(n)]
        for cp in copies + mine:
            cp.start()
        for cp in copies + mine:
            cp.wait()

    out_shape = (lambda t: (N_DEV * t.shape[0],) + t.shape[1:]) if concat_rows else (lambda t: (N_DEV,) + t.shape)
    return pl.kernel(
        body, out_type=[jax.ShapeDtypeStruct(out_shape(t), t.dtype) for t in blks],
        mesh=plsc.ScalarSubcoreMesh(axis_name="sequencer", num_cores=1), name=name,
        scratch_types=[pltpu.SemaphoreType.DMA((7, n)), pltpu.SemaphoreType.DMA((7, n)), pltpu.SemaphoreType.DMA((n,))],
        compiler_params=pltpu.CompilerParams(collective_id=collective_id),
    )(*blks)


def _sequencer_scatter(g_list, name, collective_id):
    n = len(g_list)

    def body(*refs):
        ins, outs = refs[:n], refs[n:2 * n]
        send_sems, recv_sems = refs[2 * n:]
        x, y, c = lax.axis_index("x"), lax.axis_index("y"), lax.axis_index("c")
        peers = [(x ^ fx, y ^ fy, c ^ fc) for fx, fy, fc in PEER_FLIPS]
        barrier = pltpu.get_barrier_semaphore()
        for peer in peers:
            pl.semaphore_signal(barrier, inc=1, device_id=peer, device_id_type=MESH)
        pl.semaphore_wait(barrier, len(peers))
        copies = [pltpu.make_async_remote_copy(
            src_ref=ins[a].at[4 * px + 2 * py + pc], dst_ref=outs[a].at[k], send_sem=send_sems.at[k, a],
            recv_sem=recv_sems.at[k, a], device_id=(px, py, pc), device_id_type=MESH)
            for k, (px, py, pc) in enumerate(peers) for a in range(n)]
        for cp in copies:
            cp.start()
        for cp in copies:
            cp.wait()

    return pl.kernel(
        body, out_type=[jax.ShapeDtypeStruct((N_DEV - 1,) + g.shape[1:], g.dtype) for g in g_list],
        mesh=plsc.ScalarSubcoreMesh(axis_name="sequencer", num_cores=1), name=name,
        scratch_types=[pltpu.SemaphoreType.DMA((7, n)), pltpu.SemaphoreType.DMA((7, n))],
        compiler_params=pltpu.CompilerParams(collective_id=collective_id),
    )(*g_list)


def _direct_all_gather(blk, name):
    def body(g_ref, out_ref, send_sems, recv_sems, local_sem):
        x, y, c = lax.axis_index("x"), lax.axis_index("y"), lax.axis_index("c")
        me = 4 * x + 2 * y + c
        copies = [pltpu.make_async_remote_copy(
            src_ref=g_ref, dst_ref=out_ref.at[me], send_sem=send_sems.at[k], recv_sem=recv_sems.at[k],
            device_id=(x ^ fx, y ^ fy, c ^ fc), device_id_type=MESH) for k, (fx, fy, fc) in enumerate(PEER_FLIPS)]
        copies.append(pltpu.make_async_copy(g_ref, out_ref.at[me], local_sem))
        for cp in copies:
            cp.start()
        for cp in copies:
            cp.wait()

    any_spec = pl.BlockSpec(memory_space=pl.ANY)
    return pl.pallas_call(
        body, name=name, out_shape=jax.ShapeDtypeStruct((N_DEV,) + blk.shape, blk.dtype),
        in_specs=[any_spec], out_specs=any_spec,
        scratch_shapes=[pltpu.SemaphoreType.DMA((7,)), pltpu.SemaphoreType.DMA((7,)), pltpu.SemaphoreType.DMA],
    )(blk)


def _shard_views(w_in_0, b_group_w_0, w_out_0, w_in_1, w_out_1):
    return [w_in_0, b_group_w_0.reshape(4 * 32, GDIM), w_out_0, w_in_1, w_out_1]


def _small_views(named):
    return [named[name].reshape(view) for name, view in zip(SMALL_NAMES, SMALL_VIEWS)]


LOSS_ROW, LOSS_LANE = 560, N_HEADS


def _pack_small_grads(named, loss_part):
    rows = []
    for name, (r, w) in zip(SMALL_NAMES, SMALL_VIEWS):
        pad_r = -r % 8
        if name == "sink_1":
            t = jnp.concatenate([named[name].reshape(r, w), loss_part], axis=1)
            rows.append(jnp.pad(t, ((0, pad_r), (0, LANES - w - 1))))
        elif name in named:
            rows.append(jnp.pad(named[name].reshape(r, w), ((0, pad_r), (0, LANES - w))))
        else:
            rows.append(jnp.zeros((r + pad_r, LANES), F32))
    return jnp.concatenate(rows, axis=0)


def _device_blocks(t, axis):
    shape = t.shape
    t = t.reshape(shape[:axis] + (N_DEV, shape[axis] // N_DEV) + shape[axis + 1:])
    t = jnp.moveaxis(t, axis, 0)
    return t.reshape(N_DEV, -1, shape[-1] if axis != len(shape) - 1 else shape[-1] // N_DEV)


def kernel(x, norm_0, w_in_0, a_v_norm_0, a_spatial_w_0, a_spatial_b_0, b_group_w_0, b_scale_0, w_out_0, norm_1, w_in_1, sink_1, w_out_1, final_norm, loss_target, m_norm_0, m_w_in_0, m_a_v_norm_0, m_a_spatial_w_0, m_a_spatial_b_0, m_b_group_w_0, m_b_scale_0, m_w_out_0, m_norm_1, m_w_in_1, m_sink_1, m_w_out_1, m_final_norm, v_norm_0, v_w_in_0, v_a_v_norm_0, v_a_spatial_w_0, v_a_spatial_b_0, v_b_group_w_0, v_b_scale_0, v_w_out_0, v_norm_1, v_w_in_1, v_sink_1, v_w_out_1, v_final_norm):
    seq = x.shape[1]
    xs = x.reshape(seq, D)
    tgt = loss_target.reshape(seq, D)
    ax, ay, ac = lax.axis_index("x"), lax.axis_index("y"), lax.axis_index("c")
    me = jnp.reshape(4 * ax + 2 * ay + ac, (1,)).astype(jnp.int32)

    shards = _shard_views(w_in_0, b_group_w_0, w_out_0, w_in_1, w_out_1)
    cast = _cast_shards([shards[0], shards[1], shards[2], w_in_1.T, shards[4]])

    def l1_weights(after):
        blks, _ = lax.optimization_barrier((cast[3:5], after))
        return _sequencer_all_gather(blks, "weights_gather_l1", 2, concat_rows=True)

    blocks, received, early = {}, {}, {}
    collective_ids = {"l1": 3, "out0": 4, "in0": 5}

    def scatter(tag, own_blocks, wire_blocks):
        blocks[tag] = own_blocks
        received[tag] = _sequencer_scatter(wire_blocks, "grad_scatter_" + tag, collective_ids[tag])

    def small_early(named, loss_part):
        early["small"] = _sequencer_all_gather([_pack_small_grads(named, loss_part)], "small_grad_gather", 6)[0]

    grad_x, d_norm_0 = _local_step(xs, tgt, cast[0], cast[1:3], l1_weights, norm_0, a_v_norm_0, a_spatial_w_0,
                                   a_spatial_b_0, b_scale_0, norm_1, sink_1, final_norm, scatter, small_early)

    order = (("in0", 0), ("in0", 1), ("out0", 0), ("l1", 0), ("l1", 1))
    late = _sequencer_all_gather([d_norm_0.reshape(8, LANES)], "norm_grad_gather", 7)[0]
    shards_late, _ = lax.optimization_barrier((shards, grad_x))
    big = _final_sum_adamw([blocks[t][i] for t, i in order], [received[t][i] for t, i in order], me, shards_late,
                           _shard_views(m_w_in_0, m_b_group_w_0, m_w_out_0, m_w_in_1, m_w_out_1),
                           _shard_views(v_w_in_0, v_b_group_w_0, v_w_out_0, v_w_in_1, v_w_out_1))
    weights = dict(norm_0=norm_0, a_v_norm_0=a_v_norm_0, a_spatial_w_0=a_spatial_w_0, a_spatial_b_0=a_spatial_b_0,
                   b_scale_0=b_scale_0, norm_1=norm_1, sink_1=sink_1, final_norm=final_norm)
    m_small = dict(norm_0=m_norm_0, a_v_norm_0=m_a_v_norm_0, a_spatial_w_0=m_a_spatial_w_0, a_spatial_b_0=m_a_spatial_b_0,
                   b_scale_0=m_b_scale_0, norm_1=m_norm_1, sink_1=m_sink_1, final_norm=m_final_norm)
    v_small = dict(norm_0=v_norm_0, a_v_norm_0=v_a_v_norm_0, a_spatial_w_0=v_a_spatial_w_0, a_spatial_b_0=v_a_spatial_b_0,
                   b_scale_0=v_b_scale_0, norm_1=v_norm_1, sink_1=v_sink_1, final_norm=v_final_norm)
    small, loss = _small_sum_adamw(early["small"], late, _small_views(weights), _small_views(m_small),
                                   _small_views(v_small))

    def in_order(kind):
        b = [b_.reshape(s_.shape) for b_, s_ in zip(big[kind], (w_in_0, b_group_w_0, w_out_0, w_in_1, w_out_1))]
        s = {name: t.reshape(weights[name].shape) for name, t in zip(SMALL_NAMES, small[kind])}
        return [s["norm_0"], b[0], s["a_v_norm_0"], s["a_spatial_w_0"], s["a_spatial_b_0"], b[1], s["b_scale_0"], b[2],
                s["norm_1"], b[3], s["sink_1"], b[4], s["final_norm"]]

    return (loss[0, 0], grad_x.reshape(1, seq, D), *in_order(0), *in_order(1), *in_order(2), *in_order(3))


def _local_step(xs, tgt, win0_shard, l0_shards, l1_weights, norm_0, a_v_norm_0, a_spatial_w_0, a_spatial_b_0, b_scale_0,
                norm_1, sink_1, final_norm, scatter, small_early):
    seq = xs.shape[0]
    ws = a_spatial_w_0.astype(BF16)
    ws_t = jnp.swapaxes(ws, 1, 2)
    bias = jnp.repeat(a_spatial_b_0.T, GDIM, axis=1)
    g0, gv, scale, g1, gf = (t.reshape(1, D) for t in (norm_0, a_v_norm_0, b_scale_0, norm_1, final_norm))
    cos_t, sin_t = _rope_tables_t(seq)

    za, bx, bg, h0_t, win0, g_wg, wout0 = _l0_in_proj(xs, g0, win0_shard, l0_shards)
    win1_t, wout1 = l1_weights(za)
    wg = g_wg.reshape(N_DEV, 4, 32, GDIM).transpose(1, 0, 2, 3).reshape(4, GDIM, GDIM)
    wg_t = jnp.swapaxes(wg, 1, 2)
    x1 = _l0_mix_fwd(za, bx, bg, xs, ws, bias, gv, wg, scale, wout0)
    win1_t, wout1, x1 = lax.optimization_barrier((win1_t, wout1, x1))
    qt, kt, vt, gatet, h1_t = _l1_in_proj(x1, g1, win1_t, cos_t, sin_t)
    dx2, dx2b, att, lse, loss_part, d_gf, d_wout1, d_wout1_wire = _l1_attn_fwd(
        qt, kt, vt, gatet, x1, tgt, wout1, gf, sink_1)

    dq_r, dgate, dk_pad, dv_pad, d_sink = _l1_attn_bwd(dx2b, wout1, qt, kt, vt, gatet, att, lse, sink_1)
    dk_r = dk_pad[:, BLK:BLK + seq]
    dv = dv_pad[:, BLK:BLK + seq]
    dx1, dx1b, dz1_t, d_g1 = _l1_in_proj_bwd(dq_r, dk_r, dv, dgate, cos_t, sin_t, win1_t, x1, g1, dx2)
    d_win1, d_win1_wire = _dw_matmul(h1_t, dz1_t, "dw_in_1", b_transposed=True, tn=1280, col_block=MIX1_IN // N_DEV)
    rows = lambda t: t.reshape(N_DEV, t.shape[0] // N_DEV, t.shape[1])
    scatter("l1", [d_win1, rows(d_wout1)], [d_win1_wire, rows(d_wout1_wire)])

    dz0, dp, cat_t, d_ws, _, d_gv, d_scale, d_wg, d_b = _l0_mix_bwd(
        dx1b, wout0, za, bx, bg, ws, ws_t, bias, gv, wg, wg_t, scale)
    dz0 = _l0_pool_bwd(dp, dz0)
    d_win0, d_win0_wire = _dw_matmul(h0_t, dz0, "dw_in_0", tn=1280, col_block=MIX0_IN // N_DEV)
    d_wg_blocks = _device_blocks(d_wg, 1)
    scatter("in0", [d_win0, d_wg_blocks], [d_win0_wire, d_wg_blocks])
    cat_t, _ = lax.optimization_barrier((cat_t, d_win0))
    d_wout0, d_wout0_wire = _dw_matmul(cat_t, dx1b, "dw_out_0")
    scatter("out0", [rows(d_wout0)], [rows(d_wout0_wire)])
    small_early(dict(a_v_norm_0=d_gv, a_spatial_w_0=d_ws, a_spatial_b_0=d_b.reshape(4, 8, CHUNK)[:, 0, :],
                     b_scale_0=d_scale, norm_1=d_g1, sink_1=d_sink[:, 0], final_norm=d_gf), loss_part)
    dz0, _ = lax.optimization_barrier((dz0, d_wout0))
    return _l0_in_proj_bwd(dz0, win0, xs, g0, dx1)
```

```python
import jax
import jax.numpy as jnp
from jax import lax
from jax.experimental import pallas as pl
from jax.experimental.pallas import tpu as pltpu
from jax.experimental.pallas import tpu_sc as plsc

F32 = jnp.float32
BF16 = jnp.bfloat16

D = 1024
EPS = 1e-6
NEG_INF = -1e30
CHUNK = 128
A_GROUPS = 4
POOL_WINDOWS = (2, 4, 8, 16)
POOL_HALO = 8
GDIM = 256
N_HEADS = 16
N_KV = 4
GQA = 4
HD = 64
BLK = 128
ROT_HALF = 8
ROPE_THETA = 500000.0
SCALE = HD ** -0.5
MIX0_IN = 5 * D
MIX1_IN = 2560
KV_W = N_KV * HD
Q_ROWS, K_ROWS, V_ROWS, G_ROWS = (0, D), (D, D + KV_W), (D + KV_W, D + 2 * KV_W), (D + 2 * KV_W, MIX1_IN)
TQ = 512

ADAM_LR = 0.001
ADAM_B1 = 0.9
ADAM_B2 = 0.999
ADAM_EPS = 1e-08
ADAM_WD = 0.01
ADAM_STEP = 10

N_DEV = 8
LANES = 128
MIB = 2 ** 20
MESH = pl.DeviceIdType.MESH


def _params(limit_mib, n_axes=1):
    return pltpu.CompilerParams(vmem_limit_bytes=limit_mib * MIB, dimension_semantics=("arbitrary",) * n_axes)


def _resident(shape):
    nd = len(shape)
    return pl.BlockSpec(shape, lambda *_: (0,) * nd, pipeline_mode=pl.Buffered(1))


def _gelu(x):
    k = 0.7978845608028654
    return 0.5 * x * (1.0 + jnp.tanh(k * (x + 0.044715 * x * x * x)))


def _gelu_and_grad(x):
    k = 0.7978845608028654
    x2 = x * x
    t = jnp.tanh(k * (x + 0.044715 * x * x2))
    g = 0.5 * x * (1.0 + t)
    dg = 0.5 * (1.0 + t) + 0.5 * x * (1.0 - t * t) * (k * (1.0 + 3.0 * 0.044715 * x2))
    return g, dg


def _silu_and_grad(x):
    s = jax.nn.sigmoid(x)
    return x * s, s * (1.0 + x * (1.0 - s))


def _nt(a, b):
    return lax.dot_general(a, b, (((1,), (1,)), ((), ())), preferred_element_type=F32)


def _tn(a, b):
    return lax.dot_general(a, b, (((0,), (0,)), ((), ())), preferred_element_type=F32)


def _mm(a, b):
    return jnp.dot(a, b, preferred_element_type=F32)


def _rope_tables_t(seq):
    inv = ROPE_THETA ** (-jnp.arange(0, 2 * ROT_HALF, 2, dtype=F32) / (2 * ROT_HALF))
    ang = inv[:, None] * jnp.arange(seq, dtype=F32)[None, :]
    return jnp.cos(ang), jnp.sin(ang)


def _rope_t(z, c, s, n_heads, sign):
    parts = []
    for h in range(n_heads):
        b = h * HD
        x1, x2 = z[b:b + ROT_HALF], z[b + ROT_HALF:b + 2 * ROT_HALF]
        if sign > 0:
            parts += [x1 * c - x2 * s, x2 * c + x1 * s]
        else:
            parts += [x1 * c + x2 * s, x2 * c - x1 * s]
        parts.append(z[b + 2 * ROT_HALF:b + HD])
    return jnp.concatenate(parts, axis=0)


N_CHIPS = 4
CHIP_COLS = MIX0_IN // N_CHIPS
IN_PROJ_PIECES = (
    ((0, 0, CHIP_COLS, 0),),
    ((0, CHIP_COLS, CHIP_COLS, 0),),
    ((0, 2 * CHIP_COLS, 3 * D - 2 * CHIP_COLS, 0), (1, 0, 3 * CHIP_COLS - 3 * D, 3 * D - 2 * CHIP_COLS)),
    ((1, 3 * CHIP_COLS - 3 * D, 4 * D - 3 * CHIP_COLS, 0), (2, 0, D, 4 * D - 3 * CHIP_COLS)),
)


def _l0_in_proj(x, g0, w_shard, later_shards):
    seq = x.shape[0]
    tm = 512
    n = seq // tm
    shard_cols = w_shard.shape[1]
    n_arr = 1 + len(later_shards)
    later = range(1, n_arr)
    assert 2 * shard_cols == CHIP_COLS and seq % tm == 0 and n >= 4

    def body(*refs):
        x_ref, g_ref = refs[:2]
        ins = refs[2:2 + n_arr]
        za_ref, bx_ref, bg_ref, ht_ref = refs[2 + n_arr:6 + n_arr]
        gathered = refs[6 + n_arr:6 + 2 * n_arr]
        h_all, w_buf, z32, z16, send_sems, recv_sems, local_sems, load_sems, out_sems = refs[6 + 2 * n_arr:]
        p, i = pl.program_id(0), pl.program_id(1)
        ax, ay, ac = lax.axis_index("x"), lax.axis_index("y"), lax.axis_index("c")
        me, sibling = (ax, ay, ac), (ax, ay, 1 - ac)
        chips = [(ax, ay), (1 - ax, ay), (ax, 1 - ay), (1 - ax, 1 - ay)]
        outs = (za_ref, bx_ref, bg_ref)

        def slot(a, px, py, pc):
            dev = 4 * px + 2 * py + pc
            if a == 0:
                return gathered[0].at[:, pl.ds(pl.multiple_of(dev * shard_cols, LANES), shard_cols)]
            rows = later_shards[a - 1].shape[0]
            return gathered[a].at[pl.ds(pl.multiple_of(dev * rows, 16), rows)]

        def copy(k, a, block, to, from_input=False):
            return pltpu.make_async_remote_copy(
                src_ref=ins[a] if from_input else slot(a, *block), dst_ref=slot(a, *block),
                send_sem=send_sems.at[k, a], recv_sem=recv_sems.at[k, a], device_id=to, device_id_type=MESH)

        def to_sibling(a):
            return copy(0, a, me, sibling, from_input=True)

        def send(j, a):
            return copy(j, a, me, (*chips[j], ac), from_input=True)

        def landed(j, a):
            return copy(j, a, (*chips[j], ac), me)

        def forward(j, a):
            return copy(3 + j, a, (*chips[j], ac), sibling)

        def forwarded(j, a):
            return copy(3 + j, a, (*chips[j], 1 - ac), me)

        def mine(a):
            return pltpu.make_async_copy(ins[a], slot(a, *me), local_sems.at[a])

        def load(chip, q):
            px, py = chip
            cols = pl.ds(pl.multiple_of((2 * px + py) * CHIP_COLS, LANES), CHIP_COLS)
            return pltpu.make_async_copy(gathered[0].at[:, cols], w_buf.at[q % 2], load_sems.at[q % 2])

        def out_copies(q, tile, stage):
            cps = []
            for k, (o, c0, width, z0) in enumerate(IN_PROJ_PIECES[q]):
                src = z32.at[stage, :, pl.ds(z0, width)] if o == 1 else z16.at[stage, :, pl.ds(z0, width)]
                dst = outs[o].at[pl.ds(pl.multiple_of(tile * tm, tm), tm), pl.ds(c0, width)]
                cps.append(pltpu.make_async_copy(src, dst, out_sems.at[stage, k]))
            return cps

        @pl.when((p == 0) & (i == 0))
        def _():
            for a in range(n_arr):
                mine(a).start()
                to_sibling(a).start()
            send(1, 0).start()
            send(2, 0).start()
            copy(0, 0, sibling, me).wait_recv()
            mine(0).wait()
            load(chips[0], 0).start()

        for j in range(1, N_CHIPS):
            @pl.when((p == j - 1) & (i == n - 2))
            def _(j=j):
                landed(j, 0).wait_recv()
                forward(j, 0).start()
                if j == 1:
                    send(1, 0).wait_send()
                    send(2, 0).wait_send()
                    send(3, 0).start()
                    for a in later:
                        for jj in range(1, N_CHIPS):
                            send(jj, a).start()

            @pl.when((p == j - 1) & (i == n - 1))
            def _(j=j):
                forwarded(j, 0).wait_recv()
                load(chips[j], j).start()

        @pl.when((p == N_CHIPS - 1) & (i == n - 4))
        def _():
            for jj in range(1, N_CHIPS):
                for a in later:
                    landed(jj, a).wait_recv()
                    forward(jj, a).start()

        @pl.when(i == 0)
        def _():
            load(chips[0], p).wait()

        @pl.when(p == 0)
        def _():
            xf = x_ref[...]
            r = lax.rsqrt(jnp.mean(xf * xf, axis=1, keepdims=True) + EPS)
            h = (xf * r * g_ref[...]).astype(BF16)
            ht_ref[...] = h.T
            h_all[pl.ds(pl.multiple_of(i * tm, tm), tm), :] = h

        def chip_of_pass(pp):
            return (2 * ax + ay) ^ ((pp >> 1) | ((pp & 1) << 1))

        step = p * n + i
        stage = step % 2
        for q in range(N_CHIPS):
            @pl.when((step >= 2) & (chip_of_pass((step - 2) // n) == q))
            def _(q=q):
                for cp in out_copies(q, (step - 2) % n, stage):
                    cp.wait()

        z32[stage] = _mm(h_all[pl.ds(pl.multiple_of(i * tm, tm), tm), :], w_buf[p % 2])
        z16[stage] = z32[stage].astype(BF16)
        for q in range(N_CHIPS):
            @pl.when(chip_of_pass(p) == q)
            def _(q=q):
                for cp in out_copies(q, i, stage):
                    cp.start()

        last = (p == N_CHIPS - 1) & (i == n - 1)
        for q in range(N_CHIPS):
            @pl.when(last & (chip_of_pass(p) == q))
            def _(q=q):
                for cp in out_copies(q, n - 2, 1 - stage) + out_copies(q, n - 1, stage):
                    cp.wait()

        @pl.when(last)
        def _():
            for a in later:
                copy(0, a, sibling, me).wait_recv()
                for jj in range(1, N_CHIPS):
                    forwarded(jj, a).wait_recv()
                    send(jj, a).wait_send()
                mine(a).wait()
            send(3, 0).wait_send()
            for a in range(n_arr):
                to_sibling(a).wait_send()
                for jj in range(1, N_CHIPS):
                    forward(jj, a).wait_send()

    any_spec = pl.BlockSpec(memory_space=pl.ANY)
    first_pass_tile = lambda p, i: jnp.where(p == 0, i, n - 1)
    return pl.pallas_call(
        body, grid=(N_CHIPS, n), name="l0_in_proj",
        out_shape=[jax.ShapeDtypeStruct((seq, 3 * D), BF16), jax.ShapeDtypeStruct((seq, D), F32),
                   jax.ShapeDtypeStruct((seq, D), BF16), jax.ShapeDtypeStruct((D, seq), BF16),
                   jax.ShapeDtypeStruct((D, MIX0_IN), BF16)]
        + [jax.ShapeDtypeStruct((N_DEV * t.shape[0], t.shape[1]), t.dtype) for t in later_shards],
        in_specs=[pl.BlockSpec((tm, D), lambda p, i: (first_pass_tile(p, i), 0)), _resident((1, D))] + [any_spec] * n_arr,
        out_specs=[any_spec, any_spec, any_spec, pl.BlockSpec((D, tm), lambda p, i: (0, first_pass_tile(p, i)))]
        + [any_spec] * n_arr,
        scratch_shapes=[pltpu.VMEM((seq, D), BF16), pltpu.VMEM((2, D, CHIP_COLS), BF16),
                        pltpu.VMEM((2, tm, CHIP_COLS), F32), pltpu.VMEM((2, tm, CHIP_COLS), BF16),
                        pltpu.SemaphoreType.DMA((7, n_arr)), pltpu.SemaphoreType.DMA((7, n_arr)),
                        pltpu.SemaphoreType.DMA((n_arr,)), pltpu.SemaphoreType.DMA((2,)), pltpu.SemaphoreType.DMA((2, 2))],
        compiler_params=_params(48, 2),
    )(x, g0, w_shard, *later_shards)


POOL_EXT = 40


def _fill_halo(ext_ref, cur, prev_ref, next_ref, i, n_tiles, ts):
    ext_ref[pl.ds(0, POOL_HALO), :] = jnp.where(i > 0, prev_ref[...], 0.0)
    ext_ref[pl.ds(POOL_HALO, ts), :] = cur
    ext_ref[pl.ds(POOL_HALO + ts, POOL_HALO), :] = jnp.where(i < n_tiles - 1, next_ref[...], 0.0)
    ext_ref[pl.ds(2 * POOL_HALO + ts, POOL_EXT - 2 * POOL_HALO), :] = jnp.zeros((POOL_EXT - 2 * POOL_HALO, D), F32)


def _window_sums(src_ref, tmp_refs, ts, cols, w, shift):
    if w == 2:
        return src_ref[pl.ds(POOL_HALO - 1 + shift, ts), cols] + src_ref[pl.ds(POOL_HALO + shift, ts), cols]
    d2, d4, d8 = tmp_refs
    n2, n4, n8 = ts + 32, ts + 24, ts + 16
    d2[pl.ds(0, n2), :] = src_ref[pl.ds(0, n2), cols] + src_ref[pl.ds(1, n2), cols]
    if w == 4:
        return d2[pl.ds(POOL_HALO - 2 + shift, ts), :] + d2[pl.ds(POOL_HALO + shift, ts), :]
    d4[pl.ds(0, n4), :] = d2[pl.ds(0, n4), :] + d2[pl.ds(2, n4), :]
    if w == 8:
        return d4[pl.ds(POOL_HALO - 4 + shift, ts), :] + d4[pl.ds(POOL_HALO + shift, ts), :]
    d8[pl.ds(0, n8), :] = d4[pl.ds(0, n8), :] + d4[pl.ds(4, n8), :]
    return d8[pl.ds(shift, ts), :] + d8[pl.ds(POOL_HALO + shift, ts), :]


def _pool_scratch(ts):
    return [pltpu.VMEM((ts + POOL_EXT, D), F32)] + [pltpu.VMEM((ts + POOL_EXT, GDIM), F32)] * 3


def _pool_forward(xe_ref, tmp_refs, ts, t0, seq):
    tg = t0 + lax.broadcasted_iota(jnp.int32, (ts, 1), 0)
    outs = []
    for gi, w in enumerate(POOL_WINDOWS):
        hw = w // 2
        cols = slice(gi * GDIM, (gi + 1) * GDIM)
        cnt = (jnp.minimum(tg + hw, seq) - jnp.maximum(tg - hw, 0)).astype(F32)
        outs.append(_window_sums(xe_ref, tmp_refs, ts, cols, w, 0) / cnt - xe_ref[pl.ds(POOL_HALO, ts), cols])
    return jnp.concatenate(outs, axis=1)


def _spatial_mix(ws_ref, vnb, bias, ts):
    rows = []
    for c in range(ts // CHUNK):
        vc = vnb[c * CHUNK:(c + 1) * CHUNK, :]
        rows.append(jnp.concatenate(
            [_mm(ws_ref[h], vc[:, h * GDIM:(h + 1) * GDIM]) for h in range(A_GROUPS)], axis=1) + bias)
    return jnp.concatenate(rows, axis=0)


def _halo_specs(ts, seq, width):
    per = ts // POOL_HALO
    last = seq // POOL_HALO - 1
    prev = pl.BlockSpec((POOL_HALO, width), lambda i: (jnp.maximum(i * per - 1, 0), 0))
    nxt = pl.BlockSpec((POOL_HALO, width), lambda i: (jnp.minimum((i + 1) * per, last), 0))
    return prev, nxt


def _l0_mix_fwd(za, bx, bg, x, ws, bias, gv, wg, scale, wout):
    seq = x.shape[0]
    ts = 512
    n_tiles = seq // ts

    def body(za_ref, bx_ref, bxp_ref, bxn_ref, bg_ref, x_ref, ws_ref, bias_ref, gv_ref, wg_ref, sc_ref, wo_ref,
             x1_ref, xe_ref, *tmp_refs):
        i = pl.program_id(0)
        vg = _gelu(za_ref[:, D:2 * D].astype(F32))
        rv = lax.rsqrt(jnp.mean(vg * vg, axis=1, keepdims=True) + EPS)
        vnb = (vg * rv * gv_ref[...]).astype(BF16)
        mixed = _spatial_mix(ws_ref, vnb, bias_ref[...], ts)

        _fill_halo(xe_ref, bx_ref[...], bxp_ref, bxn_ref, i, n_tiles, ts)
        pb = _pool_forward(xe_ref, tmp_refs, ts, i * ts, seq).astype(BF16)
        ypre = jnp.concatenate([_mm(pb[:, g * GDIM:(g + 1) * GDIM], wg_ref[g]) for g in range(4)], axis=1)

        u = _gelu(za_ref[:, 0:D].astype(F32))
        ag = za_ref[:, 2 * D:3 * D].astype(F32)
        ya = (u * mixed * (ag * jax.nn.sigmoid(ag))).astype(BF16)
        out_a = _mm(ya, wo_ref[0:D, :])

        bgf = bg_ref[...].astype(F32)
        yb = (ypre * sc_ref[...] * (bgf * jax.nn.sigmoid(bgf))).astype(BF16)
        x1_ref[...] = x_ref[...] + out_a + _mm(yb, wo_ref[D:2 * D, :])

    prev, nxt = _halo_specs(ts, seq, D)
    row = lambda w: pl.BlockSpec((ts, w), lambda i: (i, 0))
    return pl.pallas_call(
        body, grid=(n_tiles,), name="l0_mix_fwd",
        out_shape=jax.ShapeDtypeStruct((seq, D), F32),
        in_specs=[row(3 * D), row(D), prev, nxt, row(D), row(D), _resident((4, CHUNK, CHUNK)), _resident((CHUNK, D)),
                  _resident((1, D)), _resident((4, GDIM, GDIM)), _resident((1, D)), _resident((2 * D, D))],
        out_specs=row(D),
        scratch_shapes=_pool_scratch(ts),
        compiler_params=_params(56),
    )(za, bx, bx, bx, bg, x, ws, bias, gv, wg, scale, wout)


def _l1_in_proj(x1, g1, w_t, cos_t, sin_t):
    seq = x1.shape[0]
    tm = 512

    def body(x_ref, g_ref, wt_ref, c_ref, s_ref, q_ref, k_ref, v_ref, gate_ref, ht_ref):
        xf = x_ref[...]
        r = lax.rsqrt(jnp.mean(xf * xf, axis=1, keepdims=True) + EPS)
        ht = (xf * r * g_ref[...]).astype(BF16).T
        ht_ref[...] = ht
        c, s = c_ref[...], s_ref[...]
        q_ref[...] = (_rope_t(_mm(wt_ref[Q_ROWS[0]:Q_ROWS[1], :], ht), c, s, N_HEADS, 1) * SCALE).astype(BF16)
        k_ref[...] = _rope_t(_mm(wt_ref[K_ROWS[0]:K_ROWS[1], :], ht), c, s, N_KV, 1).astype(BF16)
        v_ref[...] = _mm(wt_ref[V_ROWS[0]:V_ROWS[1], :], ht).astype(BF16)
        gate_ref[...] = _mm(wt_ref[G_ROWS[0]:G_ROWS[1], :], ht).astype(BF16)

    col = lambda rows: pl.BlockSpec((rows, tm), lambda i: (0, i))
    return pl.pallas_call(
        body, grid=(seq // tm,), name="l1_in_proj",
        out_shape=(jax.ShapeDtypeStruct((D, seq), BF16), jax.ShapeDtypeStruct((KV_W, seq), BF16),
                   jax.ShapeDtypeStruct((KV_W, seq), BF16), jax.ShapeDtypeStruct((D, seq), BF16),
                   jax.ShapeDtypeStruct((D, seq), BF16)),
        in_specs=[pl.BlockSpec((tm, D), lambda i: (i, 0)), _resident((1, D)), _resident((MIX1_IN, D)), col(ROT_HALF),
                  col(ROT_HALF)],
        out_specs=(col(D), col(KV_W), col(KV_W), col(D), col(D)),
        compiler_params=_params(48),
    )(x1, g1, w_t, cos_t, sin_t)


def _band_specs_t(nb, clamp_i):
    per = TQ // BLK
    prev = pl.BlockSpec((KV_W, BLK), lambda i: (0, jnp.maximum(clamp_i(i) * per - 1, 0)))
    cur = pl.BlockSpec((KV_W, TQ), lambda i: (0, clamp_i(i)))
    nxt = pl.BlockSpec((KV_W, BLK), lambda i: (0, jnp.minimum((clamp_i(i) + 1) * per, nb - 1)))
    return [prev, cur, nxt]


def _fill_band(buf, p_ref, c_ref, n_ref):
    buf[:, 0:BLK] = p_ref[...]
    buf[:, BLK:BLK + TQ] = c_ref[...]
    buf[:, BLK + TQ:2 * BLK + TQ] = n_ref[...]


def _band_bias_t(n, nb):
    c = lax.broadcasted_iota(jnp.int32, (BLK, BLK), 0)
    r = lax.broadcasted_iota(jnp.int32, (BLK, BLK), 1)
    first = jnp.where((c >= r) & (n > 0), 0.0, NEG_INF).astype(F32)
    last = jnp.where((c <= r) & (n < nb - 1), 0.0, NEG_INF).astype(F32)
    return jnp.concatenate([first] * HPP, axis=1), jnp.concatenate([last] * HPP, axis=1)


def _masked(st, bias):
    first, last = bias
    return jnp.concatenate([st[0:BLK] + first, st[BLK:2 * BLK], st[2 * BLK:3 * BLK] + last], axis=0)


AUG = 16


def _ones_rows(n_ones, width):
    return (lax.broadcasted_iota(jnp.int32, (AUG, width), 0) < n_ones).astype(BF16)


def _minus_rows(vec):
    hi = vec.astype(BF16).astype(F32)
    lo = vec - hi
    return jnp.concatenate([-hi, -lo, jnp.zeros((AUG - 2, vec.shape[1]), F32)], axis=0).astype(BF16)


HPP = GQA
FWD_GROUP, BWD_GROUP = 2, 1
BWD_AHEAD = 1


def _heads_t(ref, h0, c0):
    return jnp.concatenate([ref[(h0 + g) * HD:(h0 + g + 1) * HD, c0:c0 + BLK] for g in range(HPP)], axis=1)


def _row4(ref, h0, c0):
    return jnp.concatenate([ref[h0 + g:h0 + g + 1, c0:c0 + BLK] for g in range(HPP)], axis=1)


def _sink_row(sink_ref, h0):
    return jnp.concatenate([jnp.full((1, BLK), sink_ref[h0 + g], F32) for g in range(HPP)], axis=1)


def _l1_attn_fwd(qt, kt, vt, gatet, x1, tgt, wout, gf, sink):
    seq = x1.shape[0]
    nq, nb = seq // TQ, seq // BLK

    def body(q_ref, gate_ref, kp_ref, k_ref, kn_ref, vp_ref, v_ref, vn_ref, x1_ref, tgt_ref, wo_ref, gf_ref, sink_ref,
             dx2_ref, dx2b_ref, att_ref, lse_ref, loss_ref, dgf_ref, dwo_ref, dwo_wire_ref, kbuf, vbuf, att_scr):
        i = pl.program_id(0)

        @pl.when(i == 0)
        def _():
            loss_ref[...] = jnp.zeros_like(loss_ref)
            dgf_ref[...] = jnp.zeros_like(dgf_ref)
            dwo_ref[...] = jnp.zeros_like(dwo_ref)

        _fill_band(kbuf, kp_ref, k_ref, kn_ref)
        _fill_band(vbuf, vp_ref, v_ref, vn_ref)
        ones_row = _ones_rows(1, 3 * BLK)
        groups = [list(range(0, N_HEADS, HPP))[g:g + FWD_GROUP] for g in range(0, N_HEADS // HPP, FWD_GROUP)]
        work = [(j, grp) for j in range(TQ // BLK) for grp in groups]

        def scores(j, passes):
            c0 = j * BLK
            bias = _band_bias_t(i * (TQ // BLK) + j, nb)
            st = dict(c0=c0, passes=passes)
            st["kv_rows"] = [slice(h0 // GQA * HD, (h0 // GQA + 1) * HD) for h0 in passes]
            st["sts"] = [_masked(_tn(kbuf[rows, c0:c0 + 3 * BLK], _heads_t(q_ref, h0, c0)), bias)
                         for h0, rows in zip(passes, st["kv_rows"])]
            return st

        def softmaxes(st):
            st["sks"] = [_sink_row(sink_ref, h0) for h0 in st["passes"]]
            st["ms"] = [jnp.maximum(jnp.max(s_, axis=0, keepdims=True), sk) for s_, sk in zip(st["sts"], st["sks"])]
            st["ps"] = [jnp.exp(s_ - m).astype(BF16) for s_, m in zip(st["sts"], st["ms"])]

        def values(st):
            c0, passes = st["c0"], st["passes"]
            pvs = [_mm(jnp.concatenate([vbuf[rows, c0:c0 + 3 * BLK], ones_row], axis=0), p)
                   for rows, p in zip(st["kv_rows"], st["ps"])]
            lse_rows = []
            for h0, pv, m, sk in zip(passes, pvs, st["ms"], st["sks"]):
                den = pv[HD:HD + 1, :] + jnp.exp(sk - m)
                ot = pv[0:HD, :] / den
                lse = m + jnp.log(den)
                for g in range(HPP):
                    h = h0 + g
                    att_scr[h * HD:(h + 1) * HD, c0:c0 + BLK] = ot[:, g * BLK:(g + 1) * BLK]
                    lse_rows.append(lse[:, g * BLK:(g + 1) * BLK])
            lse_ref[passes[0]:passes[0] + len(lse_rows), c0:c0 + BLK] = jnp.concatenate(lse_rows, axis=0)

        state = scores(*work[0])
        for nxt in work[1:] + [None]:
            following = scores(*nxt) if nxt is not None else None
            softmaxes(state)
            values(state)
            state = following

        att = att_scr[...]
        gate = gate_ref[...].astype(F32)
        yt = (att * (gate * jax.nn.sigmoid(gate))).astype(BF16)
        att_ref[...] = att.astype(BF16)
        x2 = x1_ref[...] + _mm(yt.T, wo_ref[...])
        r = lax.rsqrt(jnp.mean(x2 * x2, axis=1, keepdims=True) + EPS)
        xn = x2 * r
        diff = xn * gf_ref[...] - tgt_ref[...]
        loss_ref[...] += 0.5 * jnp.sum(jnp.mean(diff * diff, axis=1, keepdims=True), axis=0, keepdims=True)
        dout = diff * (1.0 / D)
        dgf_ref[...] += jnp.sum(dout * xn, axis=0, keepdims=True)
        dxn = dout * gf_ref[...]
        dx2 = r * (dxn - xn * jnp.mean(dxn * xn, axis=1, keepdims=True))
        dx2_ref[...] = dx2
        dx2b = dx2.astype(BF16)
        dx2b_ref[...] = dx2b
        dwo_ref[...] += _mm(yt, dx2b)

        @pl.when(i == nq - 1)
        def _():
            dwo_wire_ref[...] = dwo_ref[...].astype(BF16)

    ident = lambda i: i
    row = pl.BlockSpec((TQ, D), lambda i: (i, 0))
    col = lambda rows: pl.BlockSpec((rows, TQ), lambda i: (0, i))
    whole = pl.BlockSpec((D, D), lambda i: (0, 0))
    return pl.pallas_call(
        body, grid=(nq,), name="l1_attn_fwd",
        out_shape=(jax.ShapeDtypeStruct((seq, D), F32), jax.ShapeDtypeStruct((seq, D), BF16),
                   jax.ShapeDtypeStruct((D, seq), BF16),
                   jax.ShapeDtypeStruct((N_HEADS, seq), F32), jax.ShapeDtypeStruct((1, 1), F32),
                   jax.ShapeDtypeStruct((1, D), F32), jax.ShapeDtypeStruct((D, D), F32), jax.ShapeDtypeStruct((D, D), BF16)),
        in_specs=[col(D), col(D)] + _band_specs_t(nb, ident) + _band_specs_t(nb, ident) + [
            row, row, _resident((D, D)), _resident((1, D)), pl.BlockSpec(memory_space=pltpu.SMEM)],
        out_specs=(row, row, col(D), col(N_HEADS), pl.BlockSpec((1, 1), lambda i: (0, 0)),
                   pl.BlockSpec((1, D), lambda i: (0, 0)), whole, whole),
        scratch_shapes=[pltpu.VMEM((KV_W, TQ + 2 * BLK), BF16), pltpu.VMEM((KV_W, TQ + 2 * BLK), BF16),
                        pltpu.VMEM((D, TQ), F32)],
        compiler_params=_params(56),
    )(qt, gatet, kt, kt, kt, vt, vt, vt, x1, tgt, wout, gf, sink)


def _l1_attn_bwd(dx2b, wout, qt, kt, vt, gatet, att, lse, sink):
    seq = dx2b.shape[0]
    nq, nb = seq // TQ, seq // BLK

    def body(dx_ref, wo_ref, q_ref, gate_ref, kp_ref, k_ref, kn_ref, vp_ref, v_ref, vn_ref, att_ref, lse_ref, sink_ref,
             dq_ref, dgate_ref, dk_ref, dv_ref, dsink_ref, kbuf, vbuf, dkacc, dvacc, dat_scr, delta_scr, dsacc):
        i = pl.program_id(0)

        @pl.when(i == 0)
        def _():
            dkacc[...] = jnp.zeros_like(dkacc)
            dvacc[...] = jnp.zeros_like(dvacc)
            dsacc[...] = jnp.zeros_like(dsacc)

        @pl.when(i > 0)
        def _():
            for acc in (dkacc, dvacc):
                acc[:, 0:2 * BLK] = acc[:, TQ:TQ + 2 * BLK]
                acc[:, 2 * BLK:2 * BLK + TQ] = jnp.zeros((KV_W, TQ), F32)

        @pl.when(i < nq)
        def _():
            _fill_band(kbuf, kp_ref, k_ref, kn_ref)
            _fill_band(vbuf, vp_ref, v_ref, vn_ref)
            dyt = _nt(wo_ref[...], dx_ref[...])
            sg, dsg = _silu_and_grad(gate_ref[...].astype(F32))
            attf = att_ref[...].astype(F32)
            dat = dyt * sg
            dat_scr[...] = dat.astype(BF16)
            dgate_ref[...] = (dyt * attf * dsg).astype(BF16)
            dl = dat * attf
            delta_scr[...] = jnp.concatenate(
                [jnp.sum(dl[h * HD:(h + 1) * HD, :], axis=0, keepdims=True) for h in range(N_HEADS)], axis=0)
            ones_rows = _ones_rows(2, 3 * BLK)
            groups = [list(range(0, N_HEADS, HPP))[g:g + BWD_GROUP] for g in range(0, N_HEADS // HPP, BWD_GROUP)]
            work = [(j, grp) for j in range(TQ // BLK) for grp in groups]

            def scores(j, passes):
                c0 = j * BLK
                st = dict(c0=c0, passes=passes, bias=_band_bias_t(i * (TQ // BLK) + j, nb))
                st["kv_rows"] = [slice(h0 // GQA * HD, (h0 // GQA + 1) * HD) for h0 in passes]
                st["q4s"] = [_heads_t(q_ref, h0, c0) for h0 in passes]
                st["do4s"] = [_heads_t(dat_scr, h0, c0) for h0 in passes]
                st["lse4s"] = [_row4(lse_ref, h0, c0) for h0 in passes]
                st["delta4s"] = [_row4(delta_scr, h0, c0) for h0 in passes]
                st["kths"] = [kbuf[rows, c0:c0 + 3 * BLK] for rows in st["kv_rows"]]
                st["sts"] = [_tn(jnp.concatenate([kth, ones_rows], axis=0),
                                 jnp.concatenate([q4, _minus_rows(lse4)], axis=0))
                             for kth, q4, lse4 in zip(st["kths"], st["q4s"], st["lse4s"])]
                st["dpds"] = [_tn(jnp.concatenate([vbuf[rows, c0:c0 + 3 * BLK], ones_rows], axis=0),
                                  jnp.concatenate([do4, _minus_rows(delta4)], axis=0))
                              for rows, do4, delta4 in zip(st["kv_rows"], st["do4s"], st["delta4s"])]
                return st

            def elementwise(st):
                st["ps"] = [jnp.exp(_masked(s_, st["bias"])) for s_ in st["sts"]]
                st["dss"] = [(p * dpd).astype(BF16) for p, dpd in zip(st["ps"], st["dpds"])]

            def gradients(st):
                c0 = st["c0"]
                dq4s = [_mm(kth, ds) * SCALE for kth, ds in zip(st["kths"], st["dss"])]
                dks = [_nt(q4, ds) for q4, ds in zip(st["q4s"], st["dss"])]
                dvs = [_nt(do4, p.astype(BF16)) for do4, p in zip(st["do4s"], st["ps"])]
                for h0, rows, dq4, dk, dv, lse4, delta4 in zip(st["passes"], st["kv_rows"], dq4s, dks, dvs, st["lse4s"],
                                                               st["delta4s"]):
                    dkacc[rows, c0:c0 + 3 * BLK] += dk
                    dvacc[rows, c0:c0 + 3 * BLK] += dv
                    dsk = -jnp.exp(_sink_row(sink_ref, h0) - lse4) * delta4
                    for g in range(HPP):
                        h = h0 + g
                        dq_ref[h * HD:(h + 1) * HD, c0:c0 + BLK] = dq4[:, g * BLK:(g + 1) * BLK].astype(BF16)
                        dsacc[h:h + 1, :] += dsk[:, g * BLK:(g + 1) * BLK]

            ahead = [scores(*w) for w in work[:BWD_AHEAD]]
            for n in range(len(work)):
                if n + BWD_AHEAD < len(work):
                    ahead.append(scores(*work[n + BWD_AHEAD]))
                state = ahead.pop(0)
                elementwise(state)
                gradients(state)

        dk_ref[...] = dkacc[:, 0:TQ].astype(BF16)
        dv_ref[...] = dvacc[:, 0:TQ].astype(BF16)

        @pl.when(i == nq)
        def _():
            dsink_ref[...] = jnp.broadcast_to(jnp.sum(dsacc[...], axis=1, keepdims=True), (N_HEADS, LANES))

    clamp = lambda i: jnp.minimum(i, nq - 1)
    row = pl.BlockSpec((TQ, D), lambda i: (clamp(i), 0))
    col = lambda rows: pl.BlockSpec((rows, TQ), lambda i: (0, clamp(i)))
    pad = pl.BlockSpec((KV_W, TQ), lambda i: (0, i))
    return pl.pallas_call(
        body, grid=(nq + 1,), name="l1_attn_bwd",
        out_shape=(jax.ShapeDtypeStruct((D, seq), BF16), jax.ShapeDtypeStruct((D, seq), BF16),
                   jax.ShapeDtypeStruct((KV_W, seq + TQ), BF16), jax.ShapeDtypeStruct((KV_W, seq + TQ), BF16),
                   jax.ShapeDtypeStruct((N_HEADS, LANES), F32)),
        in_specs=[row, _resident((D, D)), col(D), col(D)] + _band_specs_t(nb, clamp) + _band_specs_t(nb, clamp) + [
            col(D), col(N_HEADS), pl.BlockSpec(memory_space=pltpu.SMEM)],
        out_specs=(col(D), col(D), pad, pad, pl.BlockSpec((N_HEADS, LANES), lambda i: (0, 0))),
        scratch_shapes=[pltpu.VMEM((KV_W, TQ + 2 * BLK), BF16), pltpu.VMEM((KV_W, TQ + 2 * BLK), BF16),
                        pltpu.VMEM((KV_W, TQ + 2 * BLK), F32), pltpu.VMEM((KV_W, TQ + 2 * BLK), F32),
                        pltpu.VMEM((D, TQ), BF16), pltpu.VMEM((N_HEADS, TQ), F32), pltpu.VMEM((N_HEADS, LANES), F32)],
        compiler_params=_params(56),
    )(dx2b, wout, qt, gatet, kt, kt, kt, vt, vt, vt, att, lse, sink)


def _l1_in_proj_bwd(dq_r, dk_r, dv, dgate, cos_t, sin_t, w_t, x1, g1, dx2):
    seq = x1.shape[0]
    tm = 512

    def body(dq_ref, dk_ref, dv_ref, dg_ref, c_ref, s_ref, w_ref, x_ref, g_ref, dres_ref,
             dx_ref, dxb_ref, dz_ref, dn_ref):
        @pl.when(pl.program_id(0) == 0)
        def _():
            dn_ref[...] = jnp.zeros_like(dn_ref)

        c, s = c_ref[...], s_ref[...]
        dq = _rope_t(dq_ref[...].astype(F32), c, s, N_HEADS, -1).astype(BF16)
        dk = _rope_t(dk_ref[...].astype(F32), c, s, N_KV, -1).astype(BF16)
        dz = jnp.concatenate([dq, dk, dv_ref[...], dg_ref[...]], axis=0)
        dz_ref[...] = dz
        dh = _tn(dz, w_ref[...])
        xf = x_ref[...]
        r = lax.rsqrt(jnp.mean(xf * xf, axis=1, keepdims=True) + EPS)
        xn = xf * r
        dn_ref[...] += jnp.sum(dh * xn, axis=0, keepdims=True)
        dxn = dh * g_ref[...]
        dx = dres_ref[...] + r * (dxn - xn * jnp.mean(dxn * xn, axis=1, keepdims=True))
        dx_ref[...] = dx
        dxb_ref[...] = dx.astype(BF16)

    row = pl.BlockSpec((tm, D), lambda i: (i, 0))
    col = lambda rows: pl.BlockSpec((rows, tm), lambda i: (0, i))
    return pl.pallas_call(
        body, grid=(seq // tm,), name="l1_in_proj_bwd",
        out_shape=(jax.ShapeDtypeStruct((seq, D), F32), jax.ShapeDtypeStruct((seq, D), BF16),
                   jax.ShapeDtypeStruct((MIX1_IN, seq), BF16), jax.ShapeDtypeStruct((1, D), F32)),
        in_specs=[col(D), col(KV_W), col(KV_W), col(D), col(ROT_HALF), col(ROT_HALF), _resident((MIX1_IN, D)), row,
                  _resident((1, D)), row],
        out_specs=(row, row, col(MIX1_IN), pl.BlockSpec((1, D), lambda i: (0, 0))),
        compiler_params=_params(48),
    )(dq_r, dk_r, dv, dgate, cos_t, sin_t, w_t, x1, g1, dx2)


def _l0_mix_bwd(dx1b, wout, za, bx, bg, ws, ws_t, bias, gv, wg, wg_t, scale):
    seq = dx1b.shape[0]
    ts = 256
    n_tiles = seq // ts

    def body(dx_ref, wo_ref, za_ref, bx_ref, bxp_ref, bxn_ref, bg_ref, ws_ref, wst_ref, bias_ref, gv_ref, wg_ref,
             wgt_ref, sc_ref,
             dz_ref, dp_ref, catt_ref, dws_ref, dbias_ref, dgv_ref, dsc_ref, dwg_ref, db_ref, xe_ref, *tmp_refs):
        i = pl.program_id(0)

        @pl.when(i == 0)
        def _():
            for r_ in (dws_ref, dbias_ref, dgv_ref, dsc_ref, dwg_ref, db_ref):
                r_[...] = jnp.zeros_like(r_)

        dxb = dx_ref[...]
        dya = _nt(dxb, wo_ref[0:D, :])
        dyb = _nt(dxb, wo_ref[D:2 * D, :])

        vg, dvg_dz = _gelu_and_grad(za_ref[:, D:2 * D].astype(F32))
        rv = lax.rsqrt(jnp.mean(vg * vg, axis=1, keepdims=True) + EPS)
        vnorm = vg * rv
        gvw = gv_ref[...]
        vnb = (vnorm * gvw).astype(BF16)
        mixed = _spatial_mix(ws_ref, vnb, bias_ref[...], ts)

        _fill_halo(xe_ref, bx_ref[...], bxp_ref, bxn_ref, i, n_tiles, ts)
        pb = _pool_forward(xe_ref, tmp_refs, ts, i * ts, seq).astype(BF16)
        ypre = jnp.concatenate([_mm(pb[:, g * GDIM:(g + 1) * GDIM], wg_ref[g]) for g in range(4)], axis=1)

        u, du = _gelu_and_grad(za_ref[:, 0:D].astype(F32))
        sga, dsga = _silu_and_grad(za_ref[:, 2 * D:3 * D].astype(F32))
        um = u * mixed
        ya = (um * sga).astype(BF16)
        t = dya * sga
        dz_ref[:, 0:D] = (t * mixed * du).astype(BF16)
        dz_ref[:, 2 * D:3 * D] = (dya * um * dsga).astype(BF16)
        dmixed = t * u
        dmb = dmixed.astype(BF16)
        dvn_rows = []
        for c in range(ts // CHUNK):
            rows = slice(c * CHUNK, (c + 1) * CHUNK)
            parts = []
            for h in range(A_GROUPS):
                cols = slice(h * GDIM, (h + 1) * GDIM)
                dws_ref[h] += _nt(dmb[rows, cols], vnb[rows, cols])
                parts.append(_mm(wst_ref[h], dmb[rows, cols]))
            dvn_rows.append(jnp.concatenate(parts, axis=1))

        sc = sc_ref[...]
        y = ypre * sc
        sgb, dsgb = _silu_and_grad(bg_ref[...].astype(F32))
        yb = (y * sgb).astype(BF16)
        dy_b = dyb * sgb
        dz_ref[:, 3 * D:4 * D] = jnp.zeros((ts, D), BF16)
        dz_ref[:, 4 * D:5 * D] = (dyb * y * dsgb).astype(BF16)
        dsc_ref[...] += jnp.sum(dy_b * ypre, axis=0, keepdims=True)
        dypre = (dy_b * sc).astype(BF16)
        dps = []
        for g in range(4):
            cols = slice(g * GDIM, (g + 1) * GDIM)
            dwg_ref[g] += _tn(pb[:, cols], dypre[:, cols])
            dps.append(_mm(dypre[:, cols], wgt_ref[g]))

        dbias = dmixed[0:CHUNK, :]
        for c in range(1, ts // CHUNK):
            dbias = dbias + dmixed[c * CHUNK:(c + 1) * CHUNK, :]
        dbias_ref[...] += dbias
        dvn = jnp.concatenate(dvn_rows, axis=0)
        dgv_ref[...] += jnp.sum(dvn * vnorm, axis=0, keepdims=True)
        dxn = dvn * gvw
        dvg = rv * (dxn - vnorm * jnp.mean(dxn * vnorm, axis=1, keepdims=True))
        dz_ref[:, D:2 * D] = (dvg * dvg_dz).astype(BF16)

        dp_ref[...] = jnp.concatenate(dps, axis=1)
        catt_ref[...] = jnp.concatenate([ya, yb], axis=1).T

        @pl.when(i == n_tiles - 1)
        def _():
            for h in range(A_GROUPS):
                tot = jnp.sum(dbias_ref[:, h * GDIM:(h + 1) * GDIM].T, axis=0, keepdims=True)
                db_ref[pl.ds(h * 8, 8), :] = jnp.broadcast_to(tot, (8, CHUNK))

    prev, nxt = _halo_specs(ts, seq, D)
    row = lambda w_: pl.BlockSpec((ts, w_), lambda i: (i, 0))
    acc = lambda shape: pl.BlockSpec(shape, lambda i: (0,) * len(shape))
    return pl.pallas_call(
        body, grid=(n_tiles,), name="l0_mix_bwd",
        out_shape=(jax.ShapeDtypeStruct((seq, MIX0_IN), BF16), jax.ShapeDtypeStruct((seq, D), F32),
                   jax.ShapeDtypeStruct((2 * D, seq), BF16),
                   jax.ShapeDtypeStruct((4, CHUNK, CHUNK), F32), jax.ShapeDtypeStruct((CHUNK, D), F32),
                   jax.ShapeDtypeStruct((1, D), F32), jax.ShapeDtypeStruct((1, D), F32),
                   jax.ShapeDtypeStruct((4, GDIM, GDIM), F32), jax.ShapeDtypeStruct((32, CHUNK), F32)),
        in_specs=[row(D), _resident((2 * D, D)), row(3 * D), row(D), prev, nxt, row(D), _resident((4, CHUNK, CHUNK)),
                  _resident((4, CHUNK, CHUNK)), _resident((CHUNK, D)), _resident((1, D)), _resident((4, GDIM, GDIM)),
                  _resident((4, GDIM, GDIM)), _resident((1, D))],
        out_specs=(row(MIX0_IN), row(D), pl.BlockSpec((2 * D, ts), lambda i: (0, i)),
                   acc((4, CHUNK, CHUNK)), acc((CHUNK, D)), acc((1, D)), acc((1, D)), acc((4, GDIM, GDIM)),
                   acc((32, CHUNK))),
        scratch_shapes=_pool_scratch(ts),
        compiler_params=_params(56),
    )(dx1b, wout, za, bx, bx, bx, bg, ws, ws_t, bias, gv, wg, wg_t, scale)


def _l0_pool_bwd(dp, dz):
    seq = dp.shape[0]
    ts = 512
    n_tiles = seq // ts
    ext = ts + 2 * POOL_HALO

    def body(dp_ref, dpp_ref, dpn_ref, dz_ref, out_ref, qe_ref, *tmp_refs):
        i = pl.program_id(0)
        _fill_halo(qe_ref, dp_ref[...], dpp_ref, dpn_ref, i, n_tiles, ts)
        te = i * ts - POOL_HALO + lax.broadcasted_iota(jnp.int32, (ext, 1), 0)
        for gi, w in enumerate(POOL_WINDOWS):
            hw = w // 2
            cols = slice(gi * GDIM, (gi + 1) * GDIM)
            cnt = jnp.maximum(jnp.minimum(te + hw, seq) - jnp.maximum(te - hw, 0), 1).astype(F32)
            qe_ref[pl.ds(0, ext), cols] = qe_ref[pl.ds(0, ext), cols] / cnt
        outs = []
        for gi, w in enumerate(POOL_WINDOWS):
            cols = slice(gi * GDIM, (gi + 1) * GDIM)
            outs.append(_window_sums(qe_ref, tmp_refs, ts, cols, w, 1) - dp_ref[:, cols])
        out_ref[...] = jnp.concatenate(outs, axis=1).astype(BF16)

    prev, nxt = _halo_specs(ts, seq, D)
    row = pl.BlockSpec((ts, D), lambda i: (i, 0))
    return pl.pallas_call(
        body, grid=(n_tiles,), name="l0_pool_bwd",
        out_shape=jax.ShapeDtypeStruct(dz.shape, BF16),
        in_specs=[row, prev, nxt, pl.BlockSpec(memory_space=pl.ANY)],
        out_specs=pl.BlockSpec((ts, D), lambda i: (i, 3)),
        input_output_aliases={3: 0},
        scratch_shapes=_pool_scratch(ts),
        compiler_params=_params(32),
    )(dp, dp, dp, dz)


def _l0_in_proj_bwd(dz, w, x, g0, dx1):
    seq = x.shape[0]
    tm = 512

    def body(dz_ref, w_ref, x_ref, g_ref, dres_ref, dx_ref, dn_ref):
        @pl.when(pl.program_id(0) == 0)
        def _():
            dn_ref[...] = jnp.zeros_like(dn_ref)

        dh = _nt(dz_ref[...], w_ref[...])
        xf = x_ref[...]
        r = lax.rsqrt(jnp.mean(xf * xf, axis=1, keepdims=True) + EPS)
        xn = xf * r
        dn_ref[...] += jnp.sum(dh * xn, axis=0, keepdims=True)
        dxn = dh * g_ref[...]
        dx_ref[...] = dres_ref[...] + r * (dxn - xn * jnp.mean(dxn * xn, axis=1, keepdims=True))

    row = lambda w_: pl.BlockSpec((tm, w_), lambda i: (i, 0))
    return pl.pallas_call(
        body, grid=(seq // tm,), name="l0_in_proj_bwd",
        out_shape=(jax.ShapeDtypeStruct((seq, D), F32), jax.ShapeDtypeStruct((1, D), F32)),
        in_specs=[row(MIX0_IN), _resident((D, MIX0_IN)), row(D), _resident((1, D)), row(D)],
        out_specs=(row(D), pl.BlockSpec((1, D), lambda i: (0, 0))),
        compiler_params=_params(56),
    )(dz, w, x, g0, dx1)


def _dw_matmul(a_t, b, name, b_transposed=False, tn=1024, ts=1024, col_block=None):
    k, seq = a_t.shape
    n = b.shape[0] if b_transposed else b.shape[1]
    tn = min(n, tn)
    assert seq % ts == 0 and n % tn == 0 and (col_block is None or tn % col_block == 0)
    n_s = seq // ts
    per = 1 if col_block is None else tn // col_block

    def body(a_ref, b_ref, o_ref, ob_ref, acc_ref):
        s = pl.program_id(1)

        @pl.when(s == 0)
        def _():
            acc_ref[...] = jnp.zeros_like(acc_ref)

        acc_ref[...] += _nt(a_ref[...], b_ref[...]) if b_transposed else _mm(a_ref[...], b_ref[...])

        @pl.when(s == n_s - 1)
        def _():
            acc = acc_ref[...]
            if col_block is None:
                o_ref[...] = acc
                ob_ref[...] = acc.astype(BF16)
            else:
                for i in range(per):
                    piece = acc[:, i * col_block:(i + 1) * col_block]
                    o_ref[i] = piece
                    ob_ref[i] = piece.astype(BF16)

    b_spec = (pl.BlockSpec((tn, ts), lambda j, s: (j, s)) if b_transposed else pl.BlockSpec((ts, tn), lambda j, s: (s, j)))
    if col_block is None:
        shape, o_spec = (k, n), pl.BlockSpec((k, tn), lambda j, s: (0, j))
    else:
        shape, o_spec = (n // col_block, k, col_block), pl.BlockSpec((per, k, col_block), lambda j, s: (j, 0, 0))
    return pl.pallas_call(
        body, grid=(n // tn, n_s), name=name,
        out_shape=(jax.ShapeDtypeStruct(shape, F32), jax.ShapeDtypeStruct(shape, BF16)),
        in_specs=[pl.BlockSpec((k, ts), lambda j, s: (0, s)), b_spec],
        out_specs=(o_spec, o_spec),
        scratch_shapes=[pltpu.VMEM((k, tn), F32)],
        compiler_params=_params(56, 2),
    )(a_t, b)


ROW_TILES = 8


def _cast_shards(shards):
    n = len(shards)

    def body(*refs):
        for a in range(n):
            refs[n + a][...] = refs[a][...].astype(BF16)

    vm = pl.BlockSpec(memory_space=pltpu.VMEM)
    return pl.pallas_call(body, name="cast_weights", out_shape=[jax.ShapeDtypeStruct(t.shape, BF16) for t in shards],
                          in_specs=[vm] * n, out_specs=[vm] * n, compiler_params=_params(32, 0))(*shards)


def _adamw_math(w, g, m, v):
    m2 = ADAM_B1 * m + (1.0 - ADAM_B1) * g
    v2 = ADAM_B2 * v + (1.0 - ADAM_B2) * (g * g)
    m_hat = m2 / (1.0 - ADAM_B1 ** ADAM_STEP)
    v_hat = v2 / (1.0 - ADAM_B2 ** ADAM_STEP)
    delta = -ADAM_LR * (m_hat / (jnp.sqrt(v_hat) + ADAM_EPS) + ADAM_WD * w)
    return delta, m2, v2


def _final_sum_adamw(g_list, recv_list, me, w_list, m_list, v_list):
    n = len(w_list)

    def body(me_ref, *refs):
        own, recv, w, m, v = (refs[k * n:(k + 1) * n] for k in range(5))
        outs = [refs[(5 + k) * n:(6 + k) * n] for k in range(4)]
        for a in range(n):
            g = own[a][...]
            for k in range(N_DEV - 1):
                g = g + recv[a][k].astype(F32)
            delta, m2, v2 = _adamw_math(w[a][...], g, m[a][...], v[a][...])
            for o_ref, val in zip((outs[0][a], outs[1][a], outs[2][a], outs[3][a]), (g, delta, m2, v2)):
                o_ref[...] = val

    own_specs, flat, wire, shapes = [], [], [], []
    for t in w_list:
        rows, width = t.shape
        tr = rows // ROW_TILES
        own_specs.append(pl.BlockSpec((None, tr, width), lambda i, me: (me[0], i, 0)))
        flat.append(pl.BlockSpec((tr, width), lambda i, me: (i, 0)))
        wire.append(pl.BlockSpec((N_DEV - 1, tr, width), lambda i, me: (0, i, 0)))
        shapes.append(jax.ShapeDtypeStruct((rows, width), F32))
    out = pl.pallas_call(
        body, name="grad_sum_adamw", out_shape=shapes * 4,
        grid_spec=pltpu.PrefetchScalarGridSpec(
            num_scalar_prefetch=1, grid=(ROW_TILES,), in_specs=own_specs + wire + flat * 3, out_specs=flat * 4),
        compiler_params=_params(40),
    )(me, *g_list, *recv_list, *w_list, *m_list, *v_list)
    return [out[k * n:(k + 1) * n] for k in range(4)]


SMALL_NAMES = ("norm_0", "a_v_norm_0", "b_scale_0", "norm_1", "final_norm", "a_spatial_w_0", "a_spatial_b_0", "sink_1")
SMALL_VIEWS = ((8, LANES),) * 5 + ((4 * CHUNK, LANES), (4, LANES), (1, N_HEADS))
SMALL_ROW0 = (0, 8, 16, 24, 32, 40, 552, 560)
SMALL_ROWS = 568


def _small_sum_adamw(early, late, w_list, m_list, v_list):
    n = len(w_list)

    def body(e_ref, l_ref, *refs):
        gtot, first = e_ref[0], l_ref[0]
        for d in range(1, N_DEV):
            gtot = gtot + e_ref[d]
            first = first + l_ref[d]
        for a, ((rows, width), r0) in enumerate(zip(SMALL_VIEWS, SMALL_ROW0)):
            g = first if SMALL_NAMES[a] == "norm_0" else gtot[r0:r0 + rows, 0:width]
            delta, m2, v2 = _adamw_math(refs[a][...], g, refs[n + a][...], refs[2 * n + a][...])
            for k, val in enumerate((g, delta, m2, v2)):
                refs[(3 + k) * n + a][...] = val
        refs[7 * n][...] = gtot[LOSS_ROW:LOSS_ROW + 1, LOSS_LANE:LOSS_LANE + 1]

    vm = pl.BlockSpec(memory_space=pltpu.VMEM)
    shapes = [jax.ShapeDtypeStruct(s, F32) for s in SMALL_VIEWS]
    out = pl.pallas_call(
        body, name="small_sum_adamw", out_shape=shapes * 4 + [jax.ShapeDtypeStruct((1, 1), F32)],
        in_specs=[vm, vm] + [vm] * (3 * n), out_specs=[vm] * (4 * n + 1),
    )(early, late, *w_list, *m_list, *v_list)
    return [out[k * n:(k + 1) * n] for k in range(4)], out[4 * n]


PEER_FLIPS = tuple((fx, fy, fc) for fx in (0, 1) for fy in (0, 1) for fc in (0, 1))[1:]


def _sequencer_all_gather(blks, name, collective_id, concat_rows=False):
    n = len(blks)

    def body(*refs):
        ins, outs = refs[:n], refs[n:2 * n]
        send_sems, recv_sems, local_sems = refs[2 * n:]
        x, y, c = lax.axis_index("x"), lax.axis_index("y"), lax.axis_index("c")
        peers = [(x ^ fx, y ^ fy, c ^ fc) for fx, fy, fc in PEER_FLIPS]
        barrier = pltpu.get_barrier_semaphore()
        for peer in peers:
            pl.semaphore_signal(barrier, inc=1, device_id=peer, device_id_type=MESH)
        pl.semaphore_wait(barrier, len(peers))
        me = 4 * x + 2 * y + c

        def slot(a):
            rows = blks[a].shape[0]
            return outs[a].at[pl.ds(pl.multiple_of(me * rows, 16), rows)] if concat_rows else outs[a].at[me]

        copies = [pltpu.make_async_remote_copy(
            src_ref=ins[a], dst_ref=slot(a), send_sem=send_sems.at[k, a], recv_sem=recv_sems.at[k, a],
            device_id=peer, device_id_type=MESH) for k, peer in enumerate(peers) for a in range(n)]
        mine = [pltpu.make_async_copy(ins[a], slot(a), local_sems.at[a]) for a in range(n)]
        for cp in copies + mine:
            cp.start()
        for cp in copies + mine:
            cp.wait()

    out_shape = (lambda t: (N_DEV * t.shape[0],) + t.shape[1:]) if concat_rows else (lambda t: (N_DEV,) + t.shape)
    return pl.kernel(
        body, out_type=[jax.ShapeDtypeStruct(out_shape(t), t.dtype) for t in blks],
        mesh=plsc.ScalarSubcoreMesh(axis_name="sequencer", num_cores=1), name=name,
        scratch_types=[pltpu.SemaphoreType.DMA((7, n)), pltpu.SemaphoreType.DMA((7, n)), pltpu.SemaphoreType.DMA((n,))],
        compiler_params=pltpu.CompilerParams(collective_id=collective_id),
    )(*blks)


def _sequencer_scatter(g_list, name, collective_id):
    n = len(g_list)

    def body(*refs):
        ins, outs = refs[:n], refs[n:2 * n]
        send_sems, recv_sems = refs[2 * n:]
        x, y, c = lax.axis_index("x"), lax.axis_index("y"), lax.axis_index("c")
        peers = [(x ^ fx, y ^ fy, c ^ fc) for fx, fy, fc in PEER_FLIPS]
        barrier = pltpu.get_barrier_semaphore()
        for peer in peers:
            pl.semaphore_signal(barrier, inc=1, device_id=peer, device_id_type=MESH)
        pl.semaphore_wait(barrier, len(peers))
        copies = [pltpu.make_async_remote_copy(
            src_ref=ins[a].at[4 * px + 2 * py + pc], dst_ref=outs[a].at[k], send_sem=send_sems.at[k, a],
            recv_sem=recv_sems.at[k, a], device_id=(px, py, pc), device_id_type=MESH)
            for k, (px, py, pc) in enumerate(peers) for a in range(n)]
        for cp in copies:
            cp.start()
        for cp in copies:
            cp.wait()

    return pl.kernel(
        body, out_type=[jax.ShapeDtypeStruct((N_DEV - 1,) + g.shape[1:], g.dtype) for g in g_list],
        mesh=plsc.ScalarSubcoreMesh(axis_name="sequencer", num_cores=1), name=name,
        scratch_types=[pltpu.SemaphoreType.DMA((7, n)), pltpu.SemaphoreType.DMA((7, n))],
        compiler_params=pltpu.CompilerParams(collective_id=collective_id),
    )(*g_list)


def _direct_all_gather(blk, name):
    def body(g_ref, out_ref, send_sems, recv_sems, local_sem):
        x, y, c = lax.axis_index("x"), lax.axis_index("y"), lax.axis_index("c")
        me = 4 * x + 2 * y + c
        copies = [pltpu.make_async_remote_copy(
            src_ref=g_ref, dst_ref=out_ref.at[me], send_sem=send_sems.at[k], recv_sem=recv_sems.at[k],
            device_id=(x ^ fx, y ^ fy, c ^ fc), device_id_type=MESH) for k, (fx, fy, fc) in enumerate(PEER_FLIPS)]
        copies.append(pltpu.make_async_copy(g_ref, out_ref.at[me], local_sem))
        for cp in copies:
            cp.start()
        for cp in copies:
            cp.wait()

    any_spec = pl.BlockSpec(memory_space=pl.ANY)
    return pl.pallas_call(
        body, name=name, out_shape=jax.ShapeDtypeStruct((N_DEV,) + blk.shape, blk.dtype),
        in_specs=[any_spec], out_specs=any_spec,
        scratch_shapes=[pltpu.SemaphoreType.DMA((7,)), pltpu.SemaphoreType.DMA((7,)), pltpu.SemaphoreType.DMA],
    )(blk)


def _shard_views(w_in_0, b_group_w_0, w_out_0, w_in_1, w_out_1):
    return [w_in_0, b_group_w_0.reshape(4 * 32, GDIM), w_out_0, w_in_1, w_out_1]


def _small_views(named):
    return [named[name].reshape(view) for name, view in zip(SMALL_NAMES, SMALL_VIEWS)]


LOSS_ROW, LOSS_LANE = 560, N_HEADS


def _pack_small_grads(named, loss_part):
    rows = []
    for name, (r, w) in zip(SMALL_NAMES, SMALL_VIEWS):
        pad_r = -r % 8
        if name == "sink_1":
            t = jnp.concatenate([named[name].reshape(r, w), loss_part], axis=1)
            rows.append(jnp.pad(t, ((0, pad_r), (0, LANES - w - 1))))
        elif name in named:
            rows.append(jnp.pad(named[name].reshape(r, w), ((0, pad_r), (0, LANES - w))))
        else:
            rows.append(jnp.zeros((r + pad_r, LANES), F32))
    return jnp.concatenate(rows, axis=0)


def _device_blocks(t, axis):
    shape = t.shape
    t = t.reshape(shape[:axis] + (N_DEV, shape[axis] // N_DEV) + shape[axis + 1:])
    t = jnp.moveaxis(t, axis, 0)
    return t.reshape(N_DEV, -1, shape[-1] if axis != len(shape) - 1 else shape[-1] // N_DEV)


def kernel(x, norm_0, w_in_0, a_v_norm_0, a_spatial_w_0, a_spatial_b_0, b_group_w_0, b_scale_0, w_out_0, norm_1, w_in_1, sink_1, w_out_1, final_norm, loss_target, m_norm_0, m_w_in_0, m_a_v_norm_0, m_a_spatial_w_0, m_a_spatial_b_0, m_b_group_w_0, m_b_scale_0, m_w_out_0, m_norm_1, m_w_in_1, m_sink_1, m_w_out_1, m_final_norm, v_norm_0, v_w_in_0, v_a_v_norm_0, v_a_spatial_w_0, v_a_spatial_b_0, v_b_group_w_0, v_b_scale_0, v_w_out_0, v_norm_1, v_w_in_1, v_sink_1, v_w_out_1, v_final_norm):
    seq = x.shape[1]
    xs = x.reshape(seq, D)
    tgt = loss_target.reshape(seq, D)
    ax, ay, ac = lax.axis_index("x"), lax.axis_index("y"), lax.axis_index("c")
    me = jnp.reshape(4 * ax + 2 * ay + ac, (1,)).astype(jnp.int32)

    shards = _shard_views(w_in_0, b_group_w_0, w_out_0, w_in_1, w_out_1)
    cast = _cast_shards([shards[0], shards[1], shards[2], w_in_1.T, shards[4]])

    def l1_weights(after):
        blks, _ = lax.optimization_barrier((cast[3:5], after))
        return _sequencer_all_gather(blks, "weights_gather_l1", 2, concat_rows=True)

    blocks, received, early = {}, {}, {}
    collective_ids = {"l1": 3, "out0": 4, "in0": 5}

    def scatter(tag, own_blocks, wire_blocks):
        blocks[tag] = own_blocks
        received[tag] = _sequencer_scatter(wire_blocks, "grad_scatter_" + tag, collective_ids[tag])

    def small_early(named, loss_part):
        early["small"] = _sequencer_all_gather([_pack_small_grads(named, loss_part)], "small_grad_gather", 6)[0]

    grad_x, d_norm_0 = _local_step(xs, tgt, cast[0], cast[1:3], l1_weights, norm_0, a_v_norm_0, a_spatial_w_0,
                                   a_spatial_b_0, b_scale_0, norm_1, sink_1, final_norm, scatter, small_early)

    order = (("in0", 0), ("in0", 1), ("out0", 0), ("l1", 0), ("l1", 1))
    late = _direct_all_gather(d_norm_0.reshape(8, LANES), "norm_grad_gather")
    shards_late, _ = lax.optimization_barrier((shards, grad_x))
    big = _final_sum_adamw([blocks[t][i] for t, i in order], [received[t][i] for t, i in order], me, shards_late,
                           _shard_views(m_w_in_0, m_b_group_w_0, m_w_out_0, m_w_in_1, m_w_out_1),
                           _shard_views(v_w_in_0, v_b_group_w_0, v_w_out_0, v_w_in_1, v_w_out_1))
    weights = dict(norm_0=norm_0, a_v_norm_0=a_v_norm_0, a_spatial_w_0=a_spatial_w_0, a_spatial_b_0=a_spatial_b_0,
                   b_scale_0=b_scale_0, norm_1=norm_1, sink_1=sink_1, final_norm=final_norm)
    m_small = dict(norm_0=m_norm_0, a_v_norm_0=m_a_v_norm_0, a_spatial_w_0=m_a_spatial_w_0, a_spatial_b_0=m_a_spatial_b_0,
                   b_scale_0=m_b_scale_0, norm_1=m_norm_1, sink_1=m_sink_1, final_norm=m_final_norm)
    v_small = dict(norm_0=v_norm_0, a_v_norm_0=v_a_v_norm_0, a_spatial_w_0=v_a_spatial_w_0, a_spatial_b_0=v_a_spatial_b_0,
                   b_scale_0=v_b_scale_0, norm_1=v_norm_1, sink_1=v_sink_1, final_norm=v_final_norm)
    small, loss = _small_sum_adamw(early["small"], late, _small_views(weights), _small_views(m_small),
                                   _small_views(v_small))

    def in_order(kind):
        b = [b_.reshape(s_.shape) for b_, s_ in zip(big[kind], (w_in_0, b_group_w_0, w_out_0, w_in_1, w_out_1))]
        s = {name: t.reshape(weights[name].shape) for name, t in zip(SMALL_NAMES, small[kind])}
        return [s["norm_0"], b[0], s["a_v_norm_0"], s["a_spatial_w_0"], s["a_spatial_b_0"], b[1], s["b_scale_0"], b[2],
                s["norm_1"], b[3], s["sink_1"], b[4], s["final_norm"]]

    return (loss[0, 0], grad_x.reshape(1, seq, D), *in_order(0), *in_order(1), *in_order(2), *in_order(3))


def _local_step(xs, tgt, win0_shard, l0_shards, l1_weights, norm_0, a_v_norm_0, a_spatial_w_0, a_spatial_b_0, b_scale_0,
                norm_1, sink_1, final_norm, scatter, small_early):
    seq = xs.shape[0]
    ws = a_spatial_w_0.astype(BF16)
    ws_t = jnp.swapaxes(ws, 1, 2)
    bias = jnp.repeat(a_spatial_b_0.T, GDIM, axis=1)
    g0, gv, scale, g1, gf = (t.reshape(1, D) for t in (norm_0, a_v_norm_0, b_scale_0, norm_1, final_norm))
    cos_t, sin_t = _rope_tables_t(seq)

    za, bx, bg, h0_t, win0, g_wg, wout0 = _l0_in_proj(xs, g0, win0_shard, l0_shards)
    win1_t, wout1 = l1_weights(za)
    wg = g_wg.reshape(N_DEV, 4, 32, GDIM).transpose(1, 0, 2, 3).reshape(4, GDIM, GDIM)
    wg_t = jnp.swapaxes(wg, 1, 2)
    x1 = _l0_mix_fwd(za, bx, bg, xs, ws, bias, gv, wg, scale, wout0)
    win1_t, wout1, x1 = lax.optimization_barrier((win1_t, wout1, x1))
    qt, kt, vt, gatet, h1_t = _l1_in_proj(x1, g1, win1_t, cos_t, sin_t)
    dx2, dx2b, att, lse, loss_part, d_gf, d_wout1, d_wout1_wire = _l1_attn_fwd(
        qt, kt, vt, gatet, x1, tgt, wout1, gf, sink_1)

    dq_r, dgate, dk_pad, dv_pad, d_sink = _l1_attn_bwd(dx2b, wout1, qt, kt, vt, gatet, att, lse, sink_1)
    dk_r = dk_pad[:, BLK:BLK + seq]
    dv = dv_pad[:, BLK:BLK + seq]
    dx1, dx1b, dz1_t, d_g1 = _l1_in_proj_bwd(dq_r, dk_r, dv, dgate, cos_t, sin_t, win1_t, x1, g1, dx2)
    d_win1, d_win1_wire = _dw_matmul(h1_t, dz1_t, "dw_in_1", b_transposed=True, tn=1280, col_block=MIX1_IN // N_DEV)
    rows = lambda t: t.reshape(N_DEV, t.shape[0] // N_DEV, t.shape[1])
    scatter("l1", [d_win1, rows(d_wout1)], [d_win1_wire, rows(d_wout1_wire)])

    dz0, dp, cat_t, d_ws, _, d_gv, d_scale, d_wg, d_b = _l0_mix_bwd(
        dx1b, wout0, za, bx, bg, ws, ws_t, bias, gv, wg, wg_t, scale)
    dz0 = _l0_pool_bwd(dp, dz0)
    d_win0, d_win0_wire = _dw_matmul(h0_t, dz0, "dw_in_0", tn=1280, col_block=MIX0_IN // N_DEV)
    d_wg_blocks = _device_blocks(d_wg, 1)
    scatter("in0", [d_win0, d_wg_blocks], [d_win0_wire, d_wg_blocks])
    cat_t, _ = lax.optimization_barrier((cat_t, d_win0))
    d_wout0, d_wout0_wire = _dw_matmul(cat_t, dx1b, "dw_out_0")
    scatter("out0", [rows(d_wout0)], [rows(d_wout0_wire)])
    small_early(dict(a_v_norm_0=d_gv, a_spatial_w_0=d_ws, a_spatial_b_0=d_b.reshape(4, 8, CHUNK)[:, 0, :],
                     b_scale_0=d_scale, norm_1=d_g1, sink_1=d_sink[:, 0], final_norm=d_gf), loss_part)
    dz0, _ = lax.optimization_barrier((dz0, d_wout0))
    return _l0_in_proj_bwd(dz0, win0, xs, g0, dx1)
```

```python
import jax
import jax.numpy as jnp
from jax import lax
from jax.experimental import pallas as pl
from jax.experimental.pallas import tpu as pltpu
from jax.experimental.pallas import tpu_sc as plsc

F32 = jnp.float32
BF16 = jnp.bfloat16

D = 1024
EPS = 1e-6
NEG_INF = -1e30
CHUNK = 128
A_GROUPS = 4
POOL_WINDOWS = (2, 4, 8, 16)
POOL_HALO = 8
GDIM = 256
N_HEADS = 16
N_KV = 4
GQA = 4
HD = 64
BLK = 128
ROT_HALF = 8
ROPE_THETA = 500000.0
SCALE = HD ** -0.5
MIX0_IN = 5 * D
MIX1_IN = 2560
KV_W = N_KV * HD
Q_ROWS, K_ROWS, V_ROWS, G_ROWS = (0, D), (D, D + KV_W), (D + KV_W, D + 2 * KV_W), (D + 2 * KV_W, MIX1_IN)
TQ = 512

ADAM_LR = 0.001
ADAM_B1 = 0.9
ADAM_B2 = 0.999
ADAM_EPS = 1e-08
ADAM_WD = 0.01
ADAM_STEP = 10

N_DEV = 8
LANES = 128
MIB = 2 ** 20
MESH = pl.DeviceIdType.MESH


def _params(limit_mib, n_axes=1):
    return pltpu.CompilerParams(vmem_limit_bytes=limit_mib * MIB, dimension_semantics=("arbitrary",) * n_axes)


def _resident(shape):
    nd = len(shape)
    return pl.BlockSpec(shape, lambda *_: (0,) * nd, pipeline_mode=pl.Buffered(1))


def _gelu(x):
    k = 0.7978845608028654
    return 0.5 * x * (1.0 + jnp.tanh(k * (x + 0.044715 * x * x * x)))


def _gelu_and_grad(x):
    k = 0.7978845608028654
    x2 = x * x
    t = jnp.tanh(k * (x + 0.044715 * x * x2))
    g = 0.5 * x * (1.0 + t)
    dg = 0.5 * (1.0 + t) + 0.5 * x * (1.0 - t * t) * (k * (1.0 + 3.0 * 0.044715 * x2))
    return g, dg


def _silu_and_grad(x):
    s = jax.nn.sigmoid(x)
    return x * s, s * (1.0 + x * (1.0 - s))


def _nt(a, b):
    return lax.dot_general(a, b, (((1,), (1,)), ((), ())), preferred_element_type=F32)


def _tn(a, b):
    return lax.dot_general(a, b, (((0,), (0,)), ((), ())), preferred_element_type=F32)


def _mm(a, b):
    return jnp.dot(a, b, preferred_element_type=F32)


def _rope_tables_t(seq):
    inv = ROPE_THETA ** (-jnp.arange(0, 2 * ROT_HALF, 2, dtype=F32) / (2 * ROT_HALF))
    ang = inv[:, None] * jnp.arange(seq, dtype=F32)[None, :]
    return jnp.cos(ang), jnp.sin(ang)


def _rope_t(z, c, s, n_heads, sign):
    parts = []
    for h in range(n_heads):
        b = h * HD
        x1, x2 = z[b:b + ROT_HALF], z[b + ROT_HALF:b + 2 * ROT_HALF]
        if sign > 0:
            parts += [x1 * c - x2 * s, x2 * c + x1 * s]
        else:
            parts += [x1 * c + x2 * s, x2 * c - x1 * s]
        parts.append(z[b + 2 * ROT_HALF:b + HD])
    return jnp.concatenate(parts, axis=0)


N_CHIPS = 4
CHIP_COLS = MIX0_IN // N_CHIPS
LOCAL_DMA_PRIORITY = 1
IN_PROJ_PIECES = (
    ((0, 0, CHIP_COLS, 0),),
    ((0, CHIP_COLS, CHIP_COLS, 0),),
    ((0, 2 * CHIP_COLS, 3 * D - 2 * CHIP_COLS, 0), (1, 0, 3 * CHIP_COLS - 3 * D, 3 * D - 2 * CHIP_COLS)),
    ((1, 3 * CHIP_COLS - 3 * D, 4 * D - 3 * CHIP_COLS, 0), (2, 0, D, 4 * D - 3 * CHIP_COLS)),
)


def _l0_in_proj(x, g0, w_shard, later_shards):
    seq = x.shape[0]
    tm = 512
    n = seq // tm
    shard_cols = w_shard.shape[1]
    n_arr = 1 + len(later_shards)
    later = range(1, n_arr)
    assert 2 * shard_cols == CHIP_COLS and seq % tm == 0 and n >= 4

    def body(*refs):
        x_ref, g_ref = refs[:2]
        ins = refs[2:2 + n_arr]
        za_ref, bx_ref, bg_ref, ht_ref = refs[2 + n_arr:6 + n_arr]
        gathered = refs[6 + n_arr:6 + 2 * n_arr]
        h_all, w_buf, z32, z16, send_sems, recv_sems, local_sems, load_sems, out_sems = refs[6 + 2 * n_arr:]
        p, i = pl.program_id(0), pl.program_id(1)
        ax, ay, ac = lax.axis_index("x"), lax.axis_index("y"), lax.axis_index("c")
        me, sibling = (ax, ay, ac), (ax, ay, 1 - ac)
        chips = [(ax, ay), (1 - ax, ay), (ax, 1 - ay), (1 - ax, 1 - ay)]
        outs = (za_ref, bx_ref, bg_ref)

        def slot(a, px, py, pc):
            dev = 4 * px + 2 * py + pc
            if a == 0:
                return gathered[0].at[:, pl.ds(pl.multiple_of(dev * shard_cols, LANES), shard_cols)]
            rows = later_shards[a - 1].shape[0]
            return gathered[a].at[pl.ds(pl.multiple_of(dev * rows, 16), rows)]

        def copy(k, a, block, to, from_input=False):
            return pltpu.make_async_remote_copy(
                src_ref=ins[a] if from_input else slot(a, *block), dst_ref=slot(a, *block),
                send_sem=send_sems.at[k, a], recv_sem=recv_sems.at[k, a], device_id=to, device_id_type=MESH)

        def to_sibling(a):
            return copy(0, a, me, sibling, from_input=True)

        def send(j, a):
            return copy(j, a, me, (*chips[j], ac), from_input=True)

        def landed(j, a):
            return copy(j, a, (*chips[j], ac), me)

        def forward(j, a):
            return copy(3 + j, a, (*chips[j], ac), sibling)

        def forwarded(j, a):
            return copy(3 + j, a, (*chips[j], 1 - ac), me)

        def mine(a):
            return pltpu.make_async_copy(ins[a], slot(a, *me), local_sems.at[a])

        def load(chip, q):
            px, py = chip
            cols = pl.ds(pl.multiple_of((2 * px + py) * CHIP_COLS, LANES), CHIP_COLS)
            return pltpu.make_async_copy(gathered[0].at[:, cols], w_buf.at[q % 2], load_sems.at[q % 2])

        def out_copies(q, tile, stage):
            cps = []
            for k, (o, c0, width, z0) in enumerate(IN_PROJ_PIECES[q]):
                src = z32.at[stage, :, pl.ds(z0, width)] if o == 1 else z16.at[stage, :, pl.ds(z0, width)]
                dst = outs[o].at[pl.ds(pl.multiple_of(tile * tm, tm), tm), pl.ds(c0, width)]
                cps.append(pltpu.make_async_copy(src, dst, out_sems.at[stage, k]))
            return cps

        @pl.when((p == 0) & (i == 0))
        def _():
            for a in range(n_arr):
                mine(a).start()
                to_sibling(a).start()
            send(1, 0).start()
            send(2, 0).start()
            copy(0, 0, sibling, me).wait_recv()
            mine(0).wait()
            load(chips[0], 0).start(priority=LOCAL_DMA_PRIORITY)

        for j in range(1, N_CHIPS):
            @pl.when((p == j - 1) & (i == n - 2))
            def _(j=j):
                landed(j, 0).wait_recv()
                forward(j, 0).start()
                if j == 1:
                    send(1, 0).wait_send()
                    send(2, 0).wait_send()
                    send(3, 0).start()
                    for a in later:
                        for jj in range(1, N_CHIPS):
                            send(jj, a).start()

            @pl.when((p == j - 1) & (i == n - 1))
            def _(j=j):
                forwarded(j, 0).wait_recv()
                load(chips[j], j).start(priority=LOCAL_DMA_PRIORITY)

        @pl.when((p == N_CHIPS - 1) & (i == n - 4))
        def _():
            for jj in range(1, N_CHIPS):
                for a in later:
                    landed(jj, a).wait_recv()
                    forward(jj, a).start()

        @pl.when(i == 0)
        def _():
            load(chips[0], p).wait()

        @pl.when(p == 0)
        def _():
            xf = x_ref[...]
            r = lax.rsqrt(jnp.mean(xf * xf, axis=1, keepdims=True) + EPS)
            h = (xf * r * g_ref[...]).astype(BF16)
            ht_ref[...] = h.T
            h_all[pl.ds(pl.multiple_of(i * tm, tm), tm), :] = h

        def chip_of_pass(pp):
            return (2 * ax + ay) ^ ((pp >> 1) | ((pp & 1) << 1))

        step = p * n + i
        stage = step % 2
        for q in range(N_CHIPS):
            @pl.when((step >= 2) & (chip_of_pass((step - 2) // n) == q))
            def _(q=q):
                for cp in out_copies(q, (step - 2) % n, stage):
                    cp.wait()

        z32[stage] = _mm(h_all[pl.ds(pl.multiple_of(i * tm, tm), tm), :], w_buf[p % 2])
        z16[stage] = z32[stage].astype(BF16)
        for q in range(N_CHIPS):
            @pl.when(chip_of_pass(p) == q)
            def _(q=q):
                for cp in out_copies(q, i, stage):
                    cp.start(priority=LOCAL_DMA_PRIORITY)

        last = (p == N_CHIPS - 1) & (i == n - 1)
        for q in range(N_CHIPS):
            @pl.when(last & (chip_of_pass(p) == q))
            def _(q=q):
                for cp in out_copies(q, n - 2, 1 - stage) + out_copies(q, n - 1, stage):
                    cp.wait()

        @pl.when(last)
        def _():
            for a in later:
                copy(0, a, sibling, me).wait_recv()
                for jj in range(1, N_CHIPS):
                    forwarded(jj, a).wait_recv()
                    send(jj, a).wait_send()
                mine(a).wait()
            send(3, 0).wait_send()
            for a in range(n_arr):
                to_sibling(a).wait_send()
                for jj in range(1, N_CHIPS):
                    forward(jj, a).wait_send()

    any_spec = pl.BlockSpec(memory_space=pl.ANY)
    first_pass_tile = lambda p, i: jnp.where(p == 0, i, n - 1)
    return pl.pallas_call(
        body, grid=(N_CHIPS, n), name="l0_in_proj",
        out_shape=[jax.ShapeDtypeStruct((seq, 3 * D), BF16), jax.ShapeDtypeStruct((seq, D), F32),
                   jax.ShapeDtypeStruct((seq, D), BF16), jax.ShapeDtypeStruct((D, seq), BF16),
                   jax.ShapeDtypeStruct((D, MIX0_IN), BF16)]
        + [jax.ShapeDtypeStruct((N_DEV * t.shape[0], t.shape[1]), t.dtype) for t in later_shards],
        in_specs=[pl.BlockSpec((tm, D), lambda p, i: (first_pass_tile(p, i), 0)), _resident((1, D))] + [any_spec] * n_arr,
        out_specs=[any_spec, any_spec, any_spec, pl.BlockSpec((D, tm), lambda p, i: (0, first_pass_tile(p, i)))]
        + [any_spec] * n_arr,
        scratch_shapes=[pltpu.VMEM((seq, D), BF16), pltpu.VMEM((2, D, CHIP_COLS), BF16),
                        pltpu.VMEM((2, tm, CHIP_COLS), F32), pltpu.VMEM((2, tm, CHIP_COLS), BF16),
                        pltpu.SemaphoreType.DMA((7, n_arr)), pltpu.SemaphoreType.DMA((7, n_arr)),
                        pltpu.SemaphoreType.DMA((n_arr,)), pltpu.SemaphoreType.DMA((2,)), pltpu.SemaphoreType.DMA((2, 2))],
        compiler_params=_params(48, 2),
    )(x, g0, w_shard, *later_shards)


POOL_EXT = 40


def _fill_halo(ext_ref, cur, prev_ref, next_ref, i, n_tiles, ts):
    ext_ref[pl.ds(0, POOL_HALO), :] = jnp.where(i > 0, prev_ref[...], 0.0)
    ext_ref[pl.ds(POOL_HALO, ts), :] = cur
    ext_ref[pl.ds(POOL_HALO + ts, POOL_HALO), :] = jnp.where(i < n_tiles - 1, next_ref[...], 0.0)
    ext_ref[pl.ds(2 * POOL_HALO + ts, POOL_EXT - 2 * POOL_HALO), :] = jnp.zeros((POOL_EXT - 2 * POOL_HALO, D), F32)


def _window_sums(src_ref, tmp_refs, ts, cols, w, shift):
    if w == 2:
        return src_ref[pl.ds(POOL_HALO - 1 + shift, ts), cols] + src_ref[pl.ds(POOL_HALO + shift, ts), cols]
    d2, d4, d8 = tmp_refs
    n2, n4, n8 = ts + 32, ts + 24, ts + 16
    d2[pl.ds(0, n2), :] = src_ref[pl.ds(0, n2), cols] + src_ref[pl.ds(1, n2), cols]
    if w == 4:
        return d2[pl.ds(POOL_HALO - 2 + shift, ts), :] + d2[pl.ds(POOL_HALO + shift, ts), :]
    d4[pl.ds(0, n4), :] = d2[pl.ds(0, n4), :] + d2[pl.ds(2, n4), :]
    if w == 8:
        return d4[pl.ds(POOL_HALO - 4 + shift, ts), :] + d4[pl.ds(POOL_HALO + shift, ts), :]
    d8[pl.ds(0, n8), :] = d4[pl.ds(0, n8), :] + d4[pl.ds(4, n8), :]
    return d8[pl.ds(shift, ts), :] + d8[pl.ds(POOL_HALO + shift, ts), :]


def _pool_scratch(ts):
    return [pltpu.VMEM((ts + POOL_EXT, D), F32)] + [pltpu.VMEM((ts + POOL_EXT, GDIM), F32)] * 3


def _pool_forward(xe_ref, tmp_refs, ts, t0, seq):
    tg = t0 + lax.broadcasted_iota(jnp.int32, (ts, 1), 0)
    outs = []
    for gi, w in enumerate(POOL_WINDOWS):
        hw = w // 2
        cols = slice(gi * GDIM, (gi + 1) * GDIM)
        cnt = (jnp.minimum(tg + hw, seq) - jnp.maximum(tg - hw, 0)).astype(F32)
        outs.append(_window_sums(xe_ref, tmp_refs, ts, cols, w, 0) / cnt - xe_ref[pl.ds(POOL_HALO, ts), cols])
    return jnp.concatenate(outs, axis=1)


def _spatial_mix(ws_ref, vnb, bias, ts):
    rows = []
    for c in range(ts // CHUNK):
        vc = vnb[c * CHUNK:(c + 1) * CHUNK, :]
        rows.append(jnp.concatenate(
            [_mm(ws_ref[h], vc[:, h * GDIM:(h + 1) * GDIM]) for h in range(A_GROUPS)], axis=1) + bias)
    return jnp.concatenate(rows, axis=0)


def _halo_specs(ts, seq, width):
    per = ts // POOL_HALO
    last = seq // POOL_HALO - 1
    prev = pl.BlockSpec((POOL_HALO, width), lambda i: (jnp.maximum(i * per - 1, 0), 0))
    nxt = pl.BlockSpec((POOL_HALO, width), lambda i: (jnp.minimum((i + 1) * per, last), 0))
    return prev, nxt


def _l0_mix_fwd(za, bx, bg, x, ws, bias, gv, wg, scale, wout):
    seq = x.shape[0]
    ts = 512
    n_tiles = seq // ts

    def body(za_ref, bx_ref, bxp_ref, bxn_ref, bg_ref, x_ref, ws_ref, bias_ref, gv_ref, wg_ref, sc_ref, wo_ref,
             x1_ref, xe_ref, *tmp_refs):
        i = pl.program_id(0)
        vg = _gelu(za_ref[:, D:2 * D].astype(F32))
        rv = lax.rsqrt(jnp.mean(vg * vg, axis=1, keepdims=True) + EPS)
        vnb = (vg * rv * gv_ref[...]).astype(BF16)
        mixed = _spatial_mix(ws_ref, vnb, bias_ref[...], ts)

        _fill_halo(xe_ref, bx_ref[...], bxp_ref, bxn_ref, i, n_tiles, ts)
        pb = _pool_forward(xe_ref, tmp_refs, ts, i * ts, seq).astype(BF16)
        ypre = jnp.concatenate([_mm(pb[:, g * GDIM:(g + 1) * GDIM], wg_ref[g]) for g in range(4)], axis=1)

        u = _gelu(za_ref[:, 0:D].astype(F32))
        ag = za_ref[:, 2 * D:3 * D].astype(F32)
        ya = (u * mixed * (ag * jax.nn.sigmoid(ag))).astype(BF16)
        out_a = _mm(ya, wo_ref[0:D, :])

        bgf = bg_ref[...].astype(F32)
        yb = (ypre * sc_ref[...] * (bgf * jax.nn.sigmoid(bgf))).astype(BF16)
        x1_ref[...] = x_ref[...] + out_a + _mm(yb, wo_ref[D:2 * D, :])

    prev, nxt = _halo_specs(ts, seq, D)
    row = lambda w: pl.BlockSpec((ts, w), lambda i: (i, 0))
    return pl.pallas_call(
        body, grid=(n_tiles,), name="l0_mix_fwd",
        out_shape=jax.ShapeDtypeStruct((seq, D), F32),
        in_specs=[row(3 * D), row(D), prev, nxt, row(D), row(D), _resident((4, CHUNK, CHUNK)), _resident((CHUNK, D)),
                  _resident((1, D)), _resident((4, GDIM, GDIM)), _resident((1, D)), _resident((2 * D, D))],
        out_specs=row(D),
        scratch_shapes=_pool_scratch(ts),
        compiler_params=_params(56),
    )(za, bx, bx, bx, bg, x, ws, bias, gv, wg, scale, wout)


def _l1_in_proj(x1, g1, w_t, cos_t, sin_t):
    seq = x1.shape[0]
    tm = 512

    def body(x_ref, g_ref, wt_ref, c_ref, s_ref, q_ref, k_ref, v_ref, gate_ref, ht_ref):
        xf = x_ref[...]
        r = lax.rsqrt(jnp.mean(xf * xf, axis=1, keepdims=True) + EPS)
        ht = (xf * r * g_ref[...]).astype(BF16).T
        ht_ref[...] = ht
        c, s = c_ref[...], s_ref[...]
        q_ref[...] = (_rope_t(_mm(wt_ref[Q_ROWS[0]:Q_ROWS[1], :], ht), c, s, N_HEADS, 1) * SCALE).astype(BF16)
        k_ref[...] = _rope_t(_mm(wt_ref[K_ROWS[0]:K_ROWS[1], :], ht), c, s, N_KV, 1).astype(BF16)
        v_ref[...] = _mm(wt_ref[V_ROWS[0]:V_ROWS[1], :], ht).astype(BF16)
        gate_ref[...] = _mm(wt_ref[G_ROWS[0]:G_ROWS[1], :], ht).astype(BF16)

    col = lambda rows: pl.BlockSpec((rows, tm), lambda i: (0, i))
    return pl.pallas_call(
        body, grid=(seq // tm,), name="l1_in_proj",
        out_shape=(jax.ShapeDtypeStruct((D, seq), BF16), jax.ShapeDtypeStruct((KV_W, seq), BF16),
                   jax.ShapeDtypeStruct((KV_W, seq), BF16), jax.ShapeDtypeStruct((D, seq), BF16),
                   jax.ShapeDtypeStruct((D, seq), BF16)),
        in_specs=[pl.BlockSpec((tm, D), lambda i: (i, 0)), _resident((1, D)), _resident((MIX1_IN, D)), col(ROT_HALF),
                  col(ROT_HALF)],
        out_specs=(col(D), col(KV_W), col(KV_W), col(D), col(D)),
        compiler_params=_params(48),
    )(x1, g1, w_t, cos_t, sin_t)


def _band_specs_t(nb, clamp_i):
    per = TQ // BLK
    prev = pl.BlockSpec((KV_W, BLK), lambda i: (0, jnp.maximum(clamp_i(i) * per - 1, 0)))
    cur = pl.BlockSpec((KV_W, TQ), lambda i: (0, clamp_i(i)))
    nxt = pl.BlockSpec((KV_W, BLK), lambda i: (0, jnp.minimum((clamp_i(i) + 1) * per, nb - 1)))
    return [prev, cur, nxt]


def _fill_band(buf, p_ref, c_ref, n_ref):
    buf[:, 0:BLK] = p_ref[...]
    buf[:, BLK:BLK + TQ] = c_ref[...]
    buf[:, BLK + TQ:2 * BLK + TQ] = n_ref[...]


def _band_bias_t(n, nb):
    c = lax.broadcasted_iota(jnp.int32, (BLK, BLK), 0)
    r = lax.broadcasted_iota(jnp.int32, (BLK, BLK), 1)
    first = jnp.where((c >= r) & (n > 0), 0.0, NEG_INF).astype(F32)
    last = jnp.where((c <= r) & (n < nb - 1), 0.0, NEG_INF).astype(F32)
    return jnp.concatenate([first] * HPP, axis=1), jnp.concatenate([last] * HPP, axis=1)


def _masked(st, bias):
    first, last = bias
    return jnp.concatenate([st[0:BLK] + first, st[BLK:2 * BLK], st[2 * BLK:3 * BLK] + last], axis=0)


AUG = 16


def _ones_rows(n_ones, width):
    return (lax.broadcasted_iota(jnp.int32, (AUG, width), 0) < n_ones).astype(BF16)


def _minus_rows(vec):
    hi = vec.astype(BF16).astype(F32)
    lo = vec - hi
    return jnp.concatenate([-hi, -lo, jnp.zeros((AUG - 2, vec.shape[1]), F32)], axis=0).astype(BF16)


HPP = GQA
FWD_GROUP, BWD_GROUP = 2, 1
BWD_AHEAD = 1


def _heads_t(ref, h0, c0):
    return jnp.concatenate([ref[(h0 + g) * HD:(h0 + g + 1) * HD, c0:c0 + BLK] for g in range(HPP)], axis=1)


def _row4(ref, h0, c0):
    return jnp.concatenate([ref[h0 + g:h0 + g + 1, c0:c0 + BLK] for g in range(HPP)], axis=1)


def _sink_row(sink_ref, h0):
    return jnp.concatenate([jnp.full((1, BLK), sink_ref[h0 + g], F32) for g in range(HPP)], axis=1)


def _l1_attn_fwd(qt, kt, vt, gatet, x1, tgt, wout, gf, sink):
    seq = x1.shape[0]
    nq, nb = seq // TQ, seq // BLK

    def body(q_ref, gate_ref, kp_ref, k_ref, kn_ref, vp_ref, v_ref, vn_ref, x1_ref, tgt_ref, wo_ref, gf_ref, sink_ref,
             dx2_ref, dx2b_ref, att_ref, lse_ref, loss_ref, dgf_ref, dwo_ref, dwo_wire_ref, kbuf, vbuf, att_scr):
        i = pl.program_id(0)

        @pl.when(i == 0)
        def _():
            loss_ref[...] = jnp.zeros_like(loss_ref)
            dgf_ref[...] = jnp.zeros_like(dgf_ref)
            dwo_ref[...] = jnp.zeros_like(dwo_ref)

        _fill_band(kbuf, kp_ref, k_ref, kn_ref)
        _fill_band(vbuf, vp_ref, v_ref, vn_ref)
        ones_row = _ones_rows(1, 3 * BLK)
        groups = [list(range(0, N_HEADS, HPP))[g:g + FWD_GROUP] for g in range(0, N_HEADS // HPP, FWD_GROUP)]
        work = [(j, grp) for j in range(TQ // BLK) for grp in groups]

        def scores(j, passes):
            c0 = j * BLK
            bias = _band_bias_t(i * (TQ // BLK) + j, nb)
            st = dict(c0=c0, passes=passes)
            st["kv_rows"] = [slice(h0 // GQA * HD, (h0 // GQA + 1) * HD) for h0 in passes]
            st["sts"] = [_masked(_tn(kbuf[rows, c0:c0 + 3 * BLK], _heads_t(q_ref, h0, c0)), bias)
                         for h0, rows in zip(passes, st["kv_rows"])]
            return st

        def softmaxes(st):
            st["sks"] = [_sink_row(sink_ref, h0) for h0 in st["passes"]]
            st["ms"] = [jnp.maximum(jnp.max(s_, axis=0, keepdims=True), sk) for s_, sk in zip(st["sts"], st["sks"])]
            st["ps"] = [jnp.exp(s_ - m).astype(BF16) for s_, m in zip(st["sts"], st["ms"])]

        def values(st):
            c0, passes = st["c0"], st["passes"]
            pvs = [_mm(jnp.concatenate([vbuf[rows, c0:c0 + 3 * BLK], ones_row], axis=0), p)
                   for rows, p in zip(st["kv_rows"], st["ps"])]
            lse_rows = []
            for h0, pv, m, sk in zip(passes, pvs, st["ms"], st["sks"]):
                den = pv[HD:HD + 1, :] + jnp.exp(sk - m)
                ot = pv[0:HD, :] / den
                lse = m + jnp.log(den)
                for g in range(HPP):
                    h = h0 + g
                    att_scr[h * HD:(h + 1) * HD, c0:c0 + BLK] = ot[:, g * BLK:(g + 1) * BLK]
                    lse_rows.append(lse[:, g * BLK:(g + 1) * BLK])
            lse_ref[passes[0]:passes[0] + len(lse_rows), c0:c0 + BLK] = jnp.concatenate(lse_rows, axis=0)

        state = scores(*work[0])
        for nxt in work[1:] + [None]:
            following = scores(*nxt) if nxt is not None else None
            softmaxes(state)
            values(state)
            state = following

        att = att_scr[...]
        gate = gate_ref[...].astype(F32)
        yt = (att * (gate * jax.nn.sigmoid(gate))).astype(BF16)
        att_ref[...] = att.astype(BF16)
        x2 = x1_ref[...] + _mm(yt.T, wo_ref[...])
        r = lax.rsqrt(jnp.mean(x2 * x2, axis=1, keepdims=True) + EPS)
        xn = x2 * r
        diff = xn * gf_ref[...] - tgt_ref[...]
        loss_ref[...] += 0.5 * jnp.sum(jnp.mean(diff * diff, axis=1, keepdims=True), axis=0, keepdims=True)
        dout = diff * (1.0 / D)
        dgf_ref[...] += jnp.sum(dout * xn, axis=0, keepdims=True)
        dxn = dout * gf_ref[...]
        dx2 = r * (dxn - xn * jnp.mean(dxn * xn, axis=1, keepdims=True))
        dx2_ref[...] = dx2
        dx2b = dx2.astype(BF16)
        dx2b_ref[...] = dx2b
        dwo_ref[...] += _mm(yt, dx2b)

        @pl.when(i == nq - 1)
        def _():
            dwo_wire_ref[...] = dwo_ref[...].astype(BF16)

    ident = lambda i: i
    row = pl.BlockSpec((TQ, D), lambda i: (i, 0))
    col = lambda rows: pl.BlockSpec((rows, TQ), lambda i: (0, i))
    whole = pl.BlockSpec((D, D), lambda i: (0, 0))
    return pl.pallas_call(
        body, grid=(nq,), name="l1_attn_fwd",
        out_shape=(jax.ShapeDtypeStruct((seq, D), F32), jax.ShapeDtypeStruct((seq, D), BF16),
                   jax.ShapeDtypeStruct((D, seq), BF16),
                   jax.ShapeDtypeStruct((N_HEADS, seq), F32), jax.ShapeDtypeStruct((1, 1), F32),
                   jax.ShapeDtypeStruct((1, D), F32), jax.ShapeDtypeStruct((D, D), F32), jax.ShapeDtypeStruct((D, D), BF16)),
        in_specs=[col(D), col(D)] + _band_specs_t(nb, ident) + _band_specs_t(nb, ident) + [
            row, row, _resident((D, D)), _resident((1, D)), pl.BlockSpec(memory_space=pltpu.SMEM)],
        out_specs=(row, row, col(D), col(N_HEADS), pl.BlockSpec((1, 1), lambda i: (0, 0)),
                   pl.BlockSpec((1, D), lambda i: (0, 0)), whole, whole),
        scratch_shapes=[pltpu.VMEM((KV_W, TQ + 2 * BLK), BF16), pltpu.VMEM((KV_W, TQ + 2 * BLK), BF16),
                        pltpu.VMEM((D, TQ), F32)],
        compiler_params=_params(56),
    )(qt, gatet, kt, kt, kt, vt, vt, vt, x1, tgt, wout, gf, sink)


def _l1_attn_bwd(dx2b, wout, qt, kt, vt, gatet, att, lse, sink):
    seq = dx2b.shape[0]
    nq, nb = seq // TQ, seq // BLK

    def body(dx_ref, wo_ref, q_ref, gate_ref, kp_ref, k_ref, kn_ref, vp_ref, v_ref, vn_ref, att_ref, lse_ref, sink_ref,
             dq_ref, dgate_ref, dk_ref, dv_ref, dsink_ref, kbuf, vbuf, dkacc, dvacc, dat_scr, delta_scr, dsacc):
        i = pl.program_id(0)

        @pl.when(i == 0)
        def _():
            dkacc[...] = jnp.zeros_like(dkacc)
            dvacc[...] = jnp.zeros_like(dvacc)
            dsacc[...] = jnp.zeros_like(dsacc)

        @pl.when(i > 0)
        def _():
            for acc in (dkacc, dvacc):
                acc[:, 0:2 * BLK] = acc[:, TQ:TQ + 2 * BLK]
                acc[:, 2 * BLK:2 * BLK + TQ] = jnp.zeros((KV_W, TQ), F32)

        @pl.when(i < nq)
        def _():
            _fill_band(kbuf, kp_ref, k_ref, kn_ref)
            _fill_band(vbuf, vp_ref, v_ref, vn_ref)
            dyt = _nt(wo_ref[...], dx_ref[...])
            sg, dsg = _silu_and_grad(gate_ref[...].astype(F32))
            attf = att_ref[...].astype(F32)
            dat = dyt * sg
            dat_scr[...] = dat.astype(BF16)
            dgate_ref[...] = (dyt * attf * dsg).astype(BF16)
            dl = dat * attf
            delta_scr[...] = jnp.concatenate(
                [jnp.sum(dl[h * HD:(h + 1) * HD, :], axis=0, keepdims=True) for h in range(N_HEADS)], axis=0)
            ones_rows = _ones_rows(2, 3 * BLK)
            groups = [list(range(0, N_HEADS, HPP))[g:g + BWD_GROUP] for g in range(0, N_HEADS // HPP, BWD_GROUP)]
            work = [(j, grp) for j in range(TQ // BLK) for grp in groups]

            def scores(j, passes):
                c0 = j * BLK
                st = dict(c0=c0, passes=passes, bias=_band_bias_t(i * (TQ // BLK) + j, nb))
                st["kv_rows"] = [slice(h0 // GQA * HD, (h0 // GQA + 1) * HD) for h0 in passes]
                st["q4s"] = [_heads_t(q_ref, h0, c0) for h0 in passes]
                st["do4s"] = [_heads_t(dat_scr, h0, c0) for h0 in passes]
                st["lse4s"] = [_row4(lse_ref, h0, c0) for h0 in passes]
                st["delta4s"] = [_row4(delta_scr, h0, c0) for h0 in passes]
                st["kths"] = [kbuf[rows, c0:c0 + 3 * BLK] for rows in st["kv_rows"]]
                st["sts"] = [_tn(jnp.concatenate([kth, ones_rows], axis=0),
                                 jnp.concatenate([q4, _minus_rows(lse4)], axis=0))
                             for kth, q4, lse4 in zip(st["kths"], st["q4s"], st["lse4s"])]
                st["dpds"] = [_tn(jnp.concatenate([vbuf[rows, c0:c0 + 3 * BLK], ones_rows], axis=0),
                                  jnp.concatenate([do4, _minus_rows(delta4)], axis=0))
                              for rows, do4, delta4 in zip(st["kv_rows"], st["do4s"], st["delta4s"])]
                return st

            def elementwise(st):
                st["ps"] = [jnp.exp(_masked(s_, st["bias"])) for s_ in st["sts"]]
                st["dss"] = [(p * dpd).astype(BF16) for p, dpd in zip(st["ps"], st["dpds"])]

            def gradients(st):
                c0 = st["c0"]
                dq4s = [_mm(kth, ds) * SCALE for kth, ds in zip(st["kths"], st["dss"])]
                dks = [_nt(q4, ds) for q4, ds in zip(st["q4s"], st["dss"])]
                dvs = [_nt(do4, p.astype(BF16)) for do4, p in zip(st["do4s"], st["ps"])]
                for h0, rows, dq4, dk, dv, lse4, delta4 in zip(st["passes"], st["kv_rows"], dq4s, dks, dvs, st["lse4s"],
                                                               st["delta4s"]):
                    dkacc[rows, c0:c0 + 3 * BLK] += dk
                    dvacc[rows, c0:c0 + 3 * BLK] += dv
                    dsk = -jnp.exp(_sink_row(sink_ref, h0) - lse4) * delta4
                    for g in range(HPP):
                        h = h0 + g
                        dq_ref[h * HD:(h + 1) * HD, c0:c0 + BLK] = dq4[:, g * BLK:(g + 1) * BLK].astype(BF16)
                        dsacc[h:h + 1, :] += dsk[:, g * BLK:(g + 1) * BLK]

            ahead = [scores(*w) for w in work[:BWD_AHEAD]]
            for n in range(len(work)):
                if n + BWD_AHEAD < len(work):
                    ahead.append(scores(*work[n + BWD_AHEAD]))
                state = ahead.pop(0)
                elementwise(state)
                gradients(state)

        dk_ref[...] = dkacc[:, 0:TQ].astype(BF16)
        dv_ref[...] = dvacc[:, 0:TQ].astype(BF16)

        @pl.when(i == nq)
        def _():
            dsink_ref[...] = jnp.broadcast_to(jnp.sum(dsacc[...], axis=1, keepdims=True), (N_HEADS, LANES))

    clamp = lambda i: jnp.minimum(i, nq - 1)
    row = pl.BlockSpec((TQ, D), lambda i: (clamp(i), 0))
    col = lambda rows: pl.BlockSpec((rows, TQ), lambda i: (0, clamp(i)))
    pad = pl.BlockSpec((KV_W, TQ), lambda i: (0, i))
    return pl.pallas_call(
        body, grid=(nq + 1,), name="l1_attn_bwd",
        out_shape=(jax.ShapeDtypeStruct((D, seq), BF16), jax.ShapeDtypeStruct((D, seq), BF16),
                   jax.ShapeDtypeStruct((KV_W, seq + TQ), BF16), jax.ShapeDtypeStruct((KV_W, seq + TQ), BF16),
                   jax.ShapeDtypeStruct((N_HEADS, LANES), F32)),
        in_specs=[row, _resident((D, D)), col(D), col(D)] + _band_specs_t(nb, clamp) + _band_specs_t(nb, clamp) + [
            col(D), col(N_HEADS), pl.BlockSpec(memory_space=pltpu.SMEM)],
        out_specs=(col(D), col(D), pad, pad, pl.BlockSpec((N_HEADS, LANES), lambda i: (0, 0))),
        scratch_shapes=[pltpu.VMEM((KV_W, TQ + 2 * BLK), BF16), pltpu.VMEM((KV_W, TQ + 2 * BLK), BF16),
                        pltpu.VMEM((KV_W, TQ + 2 * BLK), F32), pltpu.VMEM((KV_W, TQ + 2 * BLK), F32),
                        pltpu.VMEM((D, TQ), BF16), pltpu.VMEM((N_HEADS, TQ), F32), pltpu.VMEM((N_HEADS, LANES), F32)],
        compiler_params=_params(56),
    )(dx2b, wout, qt, gatet, kt, kt, kt, vt, vt, vt, att, lse, sink)


def _l1_in_proj_bwd(dq_r, dk_r, dv, dgate, cos_t, sin_t, w_t, x1, g1, dx2):
    seq = x1.shape[0]
    tm = 512

    def body(dq_ref, dk_ref, dv_ref, dg_ref, c_ref, s_ref, w_ref, x_ref, g_ref, dres_ref,
             dx_ref, dxb_ref, dz_ref, dn_ref):
        @pl.when(pl.program_id(0) == 0)
        def _():
            dn_ref[...] = jnp.zeros_like(dn_ref)

        c, s = c_ref[...], s_ref[...]
        dq = _rope_t(dq_ref[...].astype(F32), c, s, N_HEADS, -1).astype(BF16)
        dk = _rope_t(dk_ref[...].astype(F32), c, s, N_KV, -1).astype(BF16)
        dz = jnp.concatenate([dq, dk, dv_ref[...], dg_ref[...]], axis=0)
        dz_ref[...] = dz
        dh = _tn(dz, w_ref[...])
        xf = x_ref[...]
        r = lax.rsqrt(jnp.mean(xf * xf, axis=1, keepdims=True) + EPS)
        xn = xf * r
        dn_ref[...] += jnp.sum(dh * xn, axis=0, keepdims=True)
        dxn = dh * g_ref[...]
        dx = dres_ref[...] + r * (dxn - xn * jnp.mean(dxn * xn, axis=1, keepdims=True))
        dx_ref[...] = dx
        dxb_ref[...] = dx.astype(BF16)

    row = pl.BlockSpec((tm, D), lambda i: (i, 0))
    col = lambda rows: pl.BlockSpec((rows, tm), lambda i: (0, i))
    return pl.pallas_call(
        body, grid=(seq // tm,), name="l1_in_proj_bwd",
        out_shape=(jax.ShapeDtypeStruct((seq, D), F32), jax.ShapeDtypeStruct((seq, D), BF16),
                   jax.ShapeDtypeStruct((MIX1_IN, seq), BF16), jax.ShapeDtypeStruct((1, D), F32)),
        in_specs=[col(D), col(KV_W), col(KV_W), col(D), col(ROT_HALF), col(ROT_HALF), _resident((MIX1_IN, D)), row,
                  _resident((1, D)), row],
        out_specs=(row, row, col(MIX1_IN), pl.BlockSpec((1, D), lambda i: (0, 0))),
        compiler_params=_params(48),
    )(dq_r, dk_r, dv, dgate, cos_t, sin_t, w_t, x1, g1, dx2)


def _l0_mix_bwd(dx1b, wout, za, bx, bg, ws, ws_t, bias, gv, wg, wg_t, scale):
    seq = dx1b.shape[0]
    ts = 256
    n_tiles = seq // ts

    def body(dx_ref, wo_ref, za_ref, bx_ref, bxp_ref, bxn_ref, bg_ref, ws_ref, wst_ref, bias_ref, gv_ref, wg_ref,
             wgt_ref, sc_ref,
             dz_ref, dp_ref, catt_ref, dws_ref, dbias_ref, dgv_ref, dsc_ref, dwg_ref, db_ref, xe_ref, *tmp_refs):
        i = pl.program_id(0)

        @pl.when(i == 0)
        def _():
            for r_ in (dws_ref, dbias_ref, dgv_ref, dsc_ref, dwg_ref, db_ref):
                r_[...] = jnp.zeros_like(r_)

        dxb = dx_ref[...]
        dya = _nt(dxb, wo_ref[0:D, :])
        dyb = _nt(dxb, wo_ref[D:2 * D, :])

        vg, dvg_dz = _gelu_and_grad(za_ref[:, D:2 * D].astype(F32))
        rv = lax.rsqrt(jnp.mean(vg * vg, axis=1, keepdims=True) + EPS)
        vnorm = vg * rv
        gvw = gv_ref[...]
        vnb = (vnorm * gvw).astype(BF16)
        mixed = _spatial_mix(ws_ref, vnb, bias_ref[...], ts)

        _fill_halo(xe_ref, bx_ref[...], bxp_ref, bxn_ref, i, n_tiles, ts)
        pb = _pool_forward(xe_ref, tmp_refs, ts, i * ts, seq).astype(BF16)
        ypre = jnp.concatenate([_mm(pb[:, g * GDIM:(g + 1) * GDIM], wg_ref[g]) for g in range(4)], axis=1)

        u, du = _gelu_and_grad(za_ref[:, 0:D].astype(F32))
        sga, dsga = _silu_and_grad(za_ref[:, 2 * D:3 * D].astype(F32))
        um = u * mixed
        ya = (um * sga).astype(BF16)
        t = dya * sga
        dz_ref[:, 0:D] = (t * mixed * du).astype(BF16)
        dz_ref[:, 2 * D:3 * D] = (dya * um * dsga).astype(BF16)
        dmixed = t * u
        dmb = dmixed.astype(BF16)
        dvn_rows = []
        for c in range(ts // CHUNK):
            rows = slice(c * CHUNK, (c + 1) * CHUNK)
            parts = []
            for h in range(A_GROUPS):
                cols = slice(h * GDIM, (h + 1) * GDIM)
                dws_ref[h] += _nt(dmb[rows, cols], vnb[rows, cols])
                parts.append(_mm(wst_ref[h], dmb[rows, cols]))
            dvn_rows.append(jnp.concatenate(parts, axis=1))

        sc = sc_ref[...]
        y = ypre * sc
        sgb, dsgb = _silu_and_grad(bg_ref[...].astype(F32))
        yb = (y * sgb).astype(BF16)
        dy_b = dyb * sgb
        dz_ref[:, 3 * D:4 * D] = jnp.zeros((ts, D), BF16)
        dz_ref[:, 4 * D:5 * D] = (dyb * y * dsgb).astype(BF16)
        dsc_ref[...] += jnp.sum(dy_b * ypre, axis=0, keepdims=True)
        dypre = (dy_b * sc).astype(BF16)
        dps = []
        for g in range(4):
            cols = slice(g * GDIM, (g + 1) * GDIM)
            dwg_ref[g] += _tn(pb[:, cols], dypre[:, cols])
            dps.append(_mm(dypre[:, cols], wgt_ref[g]))

        dbias = dmixed[0:CHUNK, :]
        for c in range(1, ts // CHUNK):
            dbias = dbias + dmixed[c * CHUNK:(c + 1) * CHUNK, :]
        dbias_ref[...] += dbias
        dvn = jnp.concatenate(dvn_rows, axis=0)
        dgv_ref[...] += jnp.sum(dvn * vnorm, axis=0, keepdims=True)
        dxn = dvn * gvw
        dvg = rv * (dxn - vnorm * jnp.mean(dxn * vnorm, axis=1, keepdims=True))
        dz_ref[:, D:2 * D] = (dvg * dvg_dz).astype(BF16)

        dp_ref[...] = jnp.concatenate(dps, axis=1)
        catt_ref[...] = jnp.concatenate([ya, yb], axis=1).T

        @pl.when(i == n_tiles - 1)
        def _():
            for h in range(A_GROUPS):
                tot = jnp.sum(dbias_ref[:, h * GDIM:(h + 1) * GDIM].T, axis=0, keepdims=True)
                db_ref[pl.ds(h * 8, 8), :] = jnp.broadcast_to(tot, (8, CHUNK))

    prev, nxt = _halo_specs(ts, seq, D)
    row = lambda w_: pl.BlockSpec((ts, w_), lambda i: (i, 0))
    acc = lambda shape: pl.BlockSpec(shape, lambda i: (0,) * len(shape))
    return pl.pallas_call(
        body, grid=(n_tiles,), name="l0_mix_bwd",
        out_shape=(jax.ShapeDtypeStruct((seq, MIX0_IN), BF16), jax.ShapeDtypeStruct((seq, D), F32),
                   jax.ShapeDtypeStruct((2 * D, seq), BF16),
                   jax.ShapeDtypeStruct((4, CHUNK, CHUNK), F32), jax.ShapeDtypeStruct((CHUNK, D), F32),
                   jax.ShapeDtypeStruct((1, D), F32), jax.ShapeDtypeStruct((1, D), F32),
                   jax.ShapeDtypeStruct((4, GDIM, GDIM), F32), jax.ShapeDtypeStruct((32, CHUNK), F32)),
        in_specs=[row(D), _resident((2 * D, D)), row(3 * D), row(D), prev, nxt, row(D), _resident((4, CHUNK, CHUNK)),
                  _resident((4, CHUNK, CHUNK)), _resident((CHUNK, D)), _resident((1, D)), _resident((4, GDIM, GDIM)),
                  _resident((4, GDIM, GDIM)), _resident((1, D))],
        out_specs=(row(MIX0_IN), row(D), pl.BlockSpec((2 * D, ts), lambda i: (0, i)),
                   acc((4, CHUNK, CHUNK)), acc((CHUNK, D)), acc((1, D)), acc((1, D)), acc((4, GDIM, GDIM)),
                   acc((32, CHUNK))),
        scratch_shapes=_pool_scratch(ts),
        compiler_params=_params(56),
    )(dx1b, wout, za, bx, bx, bx, bg, ws, ws_t, bias, gv, wg, wg_t, scale)


def _l0_pool_bwd(dp, dz):
    seq = dp.shape[0]
    ts = 512
    n_tiles = seq // ts
    ext = ts + 2 * POOL_HALO

    def body(dp_ref, dpp_ref, dpn_ref, dz_ref, out_ref, qe_ref, *tmp_refs):
        i = pl.program_id(0)
        _fill_halo(qe_ref, dp_ref[...], dpp_ref, dpn_ref, i, n_tiles, ts)
        te = i * ts - POOL_HALO + lax.broadcasted_iota(jnp.int32, (ext, 1), 0)
        for gi, w in enumerate(POOL_WINDOWS):
            hw = w // 2
            cols = slice(gi * GDIM, (gi + 1) * GDIM)
            cnt = jnp.maximum(jnp.minimum(te + hw, seq) - jnp.maximum(te - hw, 0), 1).astype(F32)
            qe_ref[pl.ds(0, ext), cols] = qe_ref[pl.ds(0, ext), cols] / cnt
        outs = []
        for gi, w in enumerate(POOL_WINDOWS):
            cols = slice(gi * GDIM, (gi + 1) * GDIM)
            outs.append(_window_sums(qe_ref, tmp_refs, ts, cols, w, 1) - dp_ref[:, cols])
        out_ref[...] = jnp.concatenate(outs, axis=1).astype(BF16)

    prev, nxt = _halo_specs(ts, seq, D)
    row = pl.BlockSpec((ts, D), lambda i: (i, 0))
    return pl.pallas_call(
        body, grid=(n_tiles,), name="l0_pool_bwd",
        out_shape=jax.ShapeDtypeStruct(dz.shape, BF16),
        in_specs=[row, prev, nxt, pl.BlockSpec(memory_space=pl.ANY)],
        out_specs=pl.BlockSpec((ts, D), lambda i: (i, 3)),
        input_output_aliases={3: 0},
        scratch_shapes=_pool_scratch(ts),
        compiler_params=_params(32),
    )(dp, dp, dp, dz)


def _l0_in_proj_bwd(dz, w, x, g0, dx1):
    seq = x.shape[0]
    tm = 512

    def body(dz_ref, w_ref, x_ref, g_ref, dres_ref, dx_ref, dn_ref):
        @pl.when(pl.program_id(0) == 0)
        def _():
            dn_ref[...] = jnp.zeros_like(dn_ref)

        dh = _nt(dz_ref[...], w_ref[...])
        xf = x_ref[...]
        r = lax.rsqrt(jnp.mean(xf * xf, axis=1, keepdims=True) + EPS)
        xn = xf * r
        dn_ref[...] += jnp.sum(dh * xn, axis=0, keepdims=True)
        dxn = dh * g_ref[...]
        dx_ref[...] = dres_ref[...] + r * (dxn - xn * jnp.mean(dxn * xn, axis=1, keepdims=True))

    row = lambda w_: pl.BlockSpec((tm, w_), lambda i: (i, 0))
    return pl.pallas_call(
        body, grid=(seq // tm,), name="l0_in_proj_bwd",
        out_shape=(jax.ShapeDtypeStruct((seq, D), F32), jax.ShapeDtypeStruct((1, D), F32)),
        in_specs=[row(MIX0_IN), _resident((D, MIX0_IN)), row(D), _resident((1, D)), row(D)],
        out_specs=(row(D), pl.BlockSpec((1, D), lambda i: (0, 0))),
        compiler_params=_params(56),
    )(dz, w, x, g0, dx1)


def _dw_matmul(a_t, b, name, b_transposed=False, tn=1024, ts=1024, col_block=None):
    k, seq = a_t.shape
    n = b.shape[0] if b_transposed else b.shape[1]
    tn = min(n, tn)
    assert seq % ts == 0 and n % tn == 0 and (col_block is None or tn % col_block == 0)
    n_s = seq // ts
    per = 1 if col_block is None else tn // col_block

    def body(a_ref, b_ref, o_ref, ob_ref, acc_ref):
        s = pl.program_id(1)

        @pl.when(s == 0)
        def _():
            acc_ref[...] = jnp.zeros_like(acc_ref)

        acc_ref[...] += _nt(a_ref[...], b_ref[...]) if b_transposed else _mm(a_ref[...], b_ref[...])

        @pl.when(s == n_s - 1)
        def _():
            acc = acc_ref[...]
            if col_block is None:
                o_ref[...] = acc
                ob_ref[...] = acc.astype(BF16)
            else:
                for i in range(per):
                    piece = acc[:, i * col_block:(i + 1) * col_block]
                    o_ref[i] = piece
                    ob_ref[i] = piece.astype(BF16)

    b_spec = (pl.BlockSpec((tn, ts), lambda j, s: (j, s)) if b_transposed else pl.BlockSpec((ts, tn), lambda j, s: (s, j)))
    if col_block is None:
        shape, o_spec = (k, n), pl.BlockSpec((k, tn), lambda j, s: (0, j))
    else:
        shape, o_spec = (n // col_block, k, col_block), pl.BlockSpec((per, k, col_block), lambda j, s: (j, 0, 0))
    return pl.pallas_call(
        body, grid=(n // tn, n_s), name=name,
        out_shape=(jax.ShapeDtypeStruct(shape, F32), jax.ShapeDtypeStruct(shape, BF16)),
        in_specs=[pl.BlockSpec((k, ts), lambda j, s: (0, s)), b_spec],
        out_specs=(o_spec, o_spec),
        scratch_shapes=[pltpu.VMEM((k, tn), F32)],
        compiler_params=_params(56, 2),
    )(a_t, b)


ROW_TILES = 8


def _cast_shards(shards):
    n = len(shards)

    def body(*refs):
        for a in range(n):
            refs[n + a][...] = refs[a][...].astype(BF16)

    vm = pl.BlockSpec(memory_space=pltpu.VMEM)
    return pl.pallas_call(body, name="cast_weights", out_shape=[jax.ShapeDtypeStruct(t.shape, BF16) for t in shards],
                          in_specs=[vm] * n, out_specs=[vm] * n, compiler_params=_params(32, 0))(*shards)


def _adamw_math(w, g, m, v):
    m2 = ADAM_B1 * m + (1.0 - ADAM_B1) * g
    v2 = ADAM_B2 * v + (1.0 - ADAM_B2) * (g * g)
    m_hat = m2 / (1.0 - ADAM_B1 ** ADAM_STEP)
    v_hat = v2 / (1.0 - ADAM_B2 ** ADAM_STEP)
    delta = -ADAM_LR * (m_hat / (jnp.sqrt(v_hat) + ADAM_EPS) + ADAM_WD * w)
    return delta, m2, v2


def _final_sum_adamw(g_list, recv_list, me, w_list, m_list, v_list):
    n = len(w_list)

    def body(me_ref, *refs):
        own, recv, w, m, v = (refs[k * n:(k + 1) * n] for k in range(5))
        outs = [refs[(5 + k) * n:(6 + k) * n] for k in range(4)]
        for a in range(n):
            g = own[a][...]
            for k in range(N_DEV - 1):
                g = g + recv[a][k].astype(F32)
            delta, m2, v2 = _adamw_math(w[a][...], g, m[a][...], v[a][...])
            for o_ref, val in zip((outs[0][a], outs[1][a], outs[2][a], outs[3][a]), (g, delta, m2, v2)):
                o_ref[...] = val

    own_specs, flat, wire, shapes = [], [], [], []
    for t in w_list:
        rows, width = t.shape
        tr = rows // ROW_TILES
        own_specs.append(pl.BlockSpec((None, tr, width), lambda i, me: (me[0], i, 0)))
        flat.append(pl.BlockSpec((tr, width), lambda i, me: (i, 0)))
        wire.append(pl.BlockSpec((N_DEV - 1, tr, width), lambda i, me: (0, i, 0)))
        shapes.append(jax.ShapeDtypeStruct((rows, width), F32))
    out = pl.pallas_call(
        body, name="grad_sum_adamw", out_shape=shapes * 4,
        grid_spec=pltpu.PrefetchScalarGridSpec(
            num_scalar_prefetch=1, grid=(ROW_TILES,), in_specs=own_specs + wire + flat * 3, out_specs=flat * 4),
        compiler_params=_params(40),
    )(me, *g_list, *recv_list, *w_list, *m_list, *v_list)
    return [out[k * n:(k + 1) * n] for k in range(4)]


SMALL_NAMES = ("norm_0", "a_v_norm_0", "b_scale_0", "norm_1", "final_norm", "a_spatial_w_0", "a_spatial_b_0", "sink_1")
SMALL_VIEWS = ((8, LANES),) * 5 + ((4 * CHUNK, LANES), (4, LANES), (1, N_HEADS))
SMALL_ROW0 = (0, 8, 16, 24, 32, 40, 552, 560)
SMALL_ROWS = 568


def _small_sum_adamw(early, late, w_list, m_list, v_list):
    n = len(w_list)

    def body(e_ref, l_ref, *refs):
        gtot, first = e_ref[0], l_ref[0]
        for d in range(1, N_DEV):
            gtot = gtot + e_ref[d]
            first = first + l_ref[d]
        for a, ((rows, width), r0) in enumerate(zip(SMALL_VIEWS, SMALL_ROW0)):
            g = first if SMALL_NAMES[a] == "norm_0" else gtot[r0:r0 + rows, 0:width]
            delta, m2, v2 = _adamw_math(refs[a][...], g, refs[n + a][...], refs[2 * n + a][...])
            for k, val in enumerate((g, delta, m2, v2)):
                refs[(3 + k) * n + a][...] = val
        refs[7 * n][...] = gtot[LOSS_ROW:LOSS_ROW + 1, LOSS_LANE:LOSS_LANE + 1]

    vm = pl.BlockSpec(memory_space=pltpu.VMEM)
    shapes = [jax.ShapeDtypeStruct(s, F32) for s in SMALL_VIEWS]
    out = pl.pallas_call(
        body, name="small_sum_adamw", out_shape=shapes * 4 + [jax.ShapeDtypeStruct((1, 1), F32)],
        in_specs=[vm, vm] + [vm] * (3 * n), out_specs=[vm] * (4 * n + 1),
    )(early, late, *w_list, *m_list, *v_list)
    return [out[k * n:(k + 1) * n] for k in range(4)], out[4 * n]


PEER_FLIPS = tuple((fx, fy, fc) for fx in (0, 1) for fy in (0, 1) for fc in (0, 1))[1:]


def _sequencer_all_gather(blks, name, collective_id, concat_rows=False):
    n = len(blks)

    def body(*refs):
        ins, outs = refs[:n], refs[n:2 * n]
        send_sems, recv_sems, local_sems = refs[2 * n:]
        x, y, c = lax.axis_index("x"), lax.axis_index("y"), lax.axis_index("c")
        peers = [(x ^ fx, y ^ fy, c ^ fc) for fx, fy, fc in PEER_FLIPS]
        barrier = pltpu.get_barrier_semaphore()
        for peer in peers:
            pl.semaphore_signal(barrier, inc=1, device_id=peer, device_id_type=MESH)
        pl.semaphore_wait(barrier, len(peers))
        me = 4 * x + 2 * y + c

        def slot(a):
            rows = blks[a].shape[0]
            return outs[a].at[pl.ds(pl.multiple_of(me * rows, 16), rows)] if concat_rows else outs[a].at[me]

        copies = [pltpu.make_async_remote_copy(
            src_ref=ins[a], dst_ref=slot(a), send_sem=send_sems.at[k, a], recv_sem=recv_sems.at[k, a],
            device_id=peer, device_id_type=MESH) for k, peer in enumerate(peers) for a in range(n)]
        mine = [pltpu.make_async_copy(ins[a], slot(a), local_sems.at[a]) for a in range(n)]
        for cp in copies + mine:
            cp.start()
        for cp in copies + mine:
            cp.wait()

    out_shape = (lambda t: (N_DEV * t.shape[0],) + t.shape[1:]) if concat_rows else (lambda t: (N_DEV,) + t.shape)
    return pl.kernel(
        body, out_type=[jax.ShapeDtypeStruct(out_shape(t), t.dtype) for t in blks],
        mesh=plsc.ScalarSubcoreMesh(axis_name="sequencer", num_cores=1), name=name,
        scratch_types=[pltpu.SemaphoreType.DMA((7, n)), pltpu.SemaphoreType.DMA((7, n)), pltpu.SemaphoreType.DMA((n,))],
        compiler_params=pltpu.CompilerParams(collective_id=collective_id),
    )(*blks)


def _sequencer_scatter(g_list, name, collective_id):
    n = len(g_list)

    def body(*refs):
        ins, outs = refs[:n], refs[n:2 * n]
        send_sems, recv_sems = refs[2 * n:]
        x, y, c = lax.axis_index("x"), lax.axis_index("y"), lax.axis_index("c")
        peers = [(x ^ fx, y ^ fy, c ^ fc) for fx, fy, fc in PEER_FLIPS]
        barrier = pltpu.get_barrier_semaphore()
        for peer in peers:
            pl.semaphore_signal(barrier, inc=1, device_id=peer, device_id_type=MESH)
        pl.semaphore_wait(barrier, len(peers))
        copies = [pltpu.make_async_remote_copy(
            src_ref=ins[a].at[4 * px + 2 * py + pc], dst_ref=outs[a].at[k], send_sem=send_sems.at[k, a],
            recv_sem=recv_sems.at[k, a], device_id=(px, py, pc), device_id_type=MESH)
            for k, (px, py, pc) in enumerate(peers) for a in range(n)]
        for cp in copies:
            cp.start()
        for cp in copies:
            cp.wait()

    return pl.kernel(
        body, out_type=[jax.ShapeDtypeStruct((N_DEV - 1,) + g.shape[1:], g.dtype) for g in g_list],
        mesh=plsc.ScalarSubcoreMesh(axis_name="sequencer", num_cores=1), name=name,
        scratch_types=[pltpu.SemaphoreType.DMA((7, n)), pltpu.SemaphoreType.DMA((7, n))],
        compiler_params=pltpu.CompilerParams(collective_id=collective_id),
    )(*g_list)


def _direct_all_gather(blk, name):
    def body(g_ref, out_ref, send_sems, recv_sems, local_sem):
        x, y, c = lax.axis_index("x"), lax.axis_index("y"), lax.axis_index("c")
        me = 4 * x + 2 * y + c
        copies = [pltpu.make_async_remote_copy(
            src_ref=g_ref, dst_ref=out_ref.at[me], send_sem=send_sems.at[k], recv_sem=recv_sems.at[k],
            device_id=(x ^ fx, y ^ fy, c ^ fc), device_id_type=MESH) for k, (fx, fy, fc) in enumerate(PEER_FLIPS)]
        copies.append(pltpu.make_async_copy(g_ref, out_ref.at[me], local_sem))
        for cp in copies:
            cp.start()
        for cp in copies:
            cp.wait()

    any_spec = pl.BlockSpec(memory_space=pl.ANY)
    return pl.pallas_call(
        body, name=name, out_shape=jax.ShapeDtypeStruct((N_DEV,) + blk.shape, blk.dtype),
        in_specs=[any_spec], out_specs=any_spec,
        scratch_shapes=[pltpu.SemaphoreType.DMA((7,)), pltpu.SemaphoreType.DMA((7,)), pltpu.SemaphoreType.DMA],
    )(blk)


def _shard_views(w_in_0, b_group_w_0, w_out_0, w_in_1, w_out_1):
    return [w_in_0, b_group_w_0.reshape(4 * 32, GDIM), w_out_0, w_in_1, w_out_1]


def _small_views(named):
    return [named[name].reshape(view) for name, view in zip(SMALL_NAMES, SMALL_VIEWS)]


LOSS_ROW, LOSS_LANE = 560, N_HEADS


def _pack_small_grads(named, loss_part):
    rows = []
    for name, (r, w) in zip(SMALL_NAMES, SMALL_VIEWS):
        pad_r = -r % 8
        if name == "sink_1":
            t = jnp.concatenate([named[name].reshape(r, w), loss_part], axis=1)
            rows.append(jnp.pad(t, ((0, pad_r), (0, LANES - w - 1))))
        elif name in named:
            rows.append(jnp.pad(named[name].reshape(r, w), ((0, pad_r), (0, LANES - w))))
        else:
            rows.append(jnp.zeros((r + pad_r, LANES), F32))
    return jnp.concatenate(rows, axis=0)


def _device_blocks(t, axis):
    shape = t.shape
    t = t.reshape(shape[:axis] + (N_DEV, shape[axis] // N_DEV) + shape[axis + 1:])
    t = jnp.moveaxis(t, axis, 0)
    return t.reshape(N_DEV, -1, shape[-1] if axis != len(shape) - 1 else shape[-1] // N_DEV)


def kernel(x, norm_0, w_in_0, a_v_norm_0, a_spatial_w_0, a_spatial_b_0, b_group_w_0, b_scale_0, w_out_0, norm_1, w_in_1, sink_1, w_out_1, final_norm, loss_target, m_norm_0, m_w_in_0, m_a_v_norm_0, m_a_spatial_w_0, m_a_spatial_b_0, m_b_group_w_0, m_b_scale_0, m_w_out_0, m_norm_1, m_w_in_1, m_sink_1, m_w_out_1, m_final_norm, v_norm_0, v_w_in_0, v_a_v_norm_0, v_a_spatial_w_0, v_a_spatial_b_0, v_b_group_w_0, v_b_scale_0, v_w_out_0, v_norm_1, v_w_in_1, v_sink_1, v_w_out_1, v_final_norm):
    seq = x.shape[1]
    xs = x.reshape(seq, D)
    tgt = loss_target.reshape(seq, D)
    ax, ay, ac = lax.axis_index("x"), lax.axis_index("y"), lax.axis_index("c")
    me = jnp.reshape(4 * ax + 2 * ay + ac, (1,)).astype(jnp.int32)

    shards = _shard_views(w_in_0, b_group_w_0, w_out_0, w_in_1, w_out_1)
    cast = _cast_shards([shards[0], shards[1], shards[2], w_in_1.T, shards[4]])

    def l1_weights(after):
        blks, _ = lax.optimization_barrier((cast[3:5], after))
        return _sequencer_all_gather(blks, "weights_gather_l1", 2, concat_rows=True)

    blocks, received, early = {}, {}, {}
    collective_ids = {"l1": 3, "out0": 4, "in0": 5}

    def scatter(tag, own_blocks, wire_blocks):
        blocks[tag] = own_blocks
        received[tag] = _sequencer_scatter(wire_blocks, "grad_scatter_" + tag, collective_ids[tag])

    def small_early(named, loss_part):
        early["small"] = _sequencer_all_gather([_pack_small_grads(named, loss_part)], "small_grad_gather", 6)[0]

    grad_x, d_norm_0 = _local_step(xs, tgt, cast[0], cast[1:3], l1_weights, norm_0, a_v_norm_0, a_spatial_w_0,
                                   a_spatial_b_0, b_scale_0, norm_1, sink_1, final_norm, scatter, small_early)

    order = (("in0", 0), ("in0", 1), ("out0", 0), ("l1", 0), ("l1", 1))
    late = _direct_all_gather(d_norm_0.reshape(8, LANES), "norm_grad_gather")
    shards_late, _ = lax.optimization_barrier((shards, grad_x))
    big = _final_sum_adamw([blocks[t][i] for t, i in order], [received[t][i] for t, i in order], me, shards_late,
                           _shard_views(m_w_in_0, m_b_group_w_0, m_w_out_0, m_w_in_1, m_w_out_1),
                           _shard_views(v_w_in_0, v_b_group_w_0, v_w_out_0, v_w_in_1, v_w_out_1))
    weights = dict(norm_0=norm_0, a_v_norm_0=a_v_norm_0, a_spatial_w_0=a_spatial_w_0, a_spatial_b_0=a_spatial_b_0,
                   b_scale_0=b_scale_0, norm_1=norm_1, sink_1=sink_1, final_norm=final_norm)
    m_small = dict(norm_0=m_norm_0, a_v_norm_0=m_a_v_norm_0, a_spatial_w_0=m_a_spatial_w_0, a_spatial_b_0=m_a_spatial_b_0,
                   b_scale_0=m_b_scale_0, norm_1=m_norm_1, sink_1=m_sink_1, final_norm=m_final_norm)
    v_small = dict(norm_0=v_norm_0, a_v_norm_0=v_a_v_norm_0, a_spatial_w_0=v_a_spatial_w_0, a_spatial_b_0=v_a_spatial_b_0,
                   b_scale_0=v_b_scale_0, norm_1=v_norm_1, sink_1=v_sink_1, final_norm=v_final_norm)
    small, loss = _small_sum_adamw(early["small"], late, _small_views(weights), _small_views(m_small),
                                   _small_views(v_small))

    def in_order(kind):
        b = [b_.reshape(s_.shape) for b_, s_ in zip(big[kind], (w_in_0, b_group_w_0, w_out_0, w_in_1, w_out_1))]
        s = {name: t.reshape(weights[name].shape) for name, t in zip(SMALL_NAMES, small[kind])}
        return [s["norm_0"], b[0], s["a_v_norm_0"], s["a_spatial_w_0"], s["a_spatial_b_0"], b[1], s["b_scale_0"], b[2],
                s["norm_1"], b[3], s["sink_1"], b[4], s["final_norm"]]

    return (loss[0, 0], grad_x.reshape(1, seq, D), *in_order(0), *in_order(1), *in_order(2), *in_order(3))


def _local_step(xs, tgt, win0_shard, l0_shards, l1_weights, norm_0, a_v_norm_0, a_spatial_w_0, a_spatial_b_0, b_scale_0,
                norm_1, sink_1, final_norm, scatter, small_early):
    seq = xs.shape[0]
    ws = a_spatial_w_0.astype(BF16)
    ws_t = jnp.swapaxes(ws, 1, 2)
    bias = jnp.repeat(a_spatial_b_0.T, GDIM, axis=1)
    g0, gv, scale, g1, gf = (t.reshape(1, D) for t in (norm_0, a_v_norm_0, b_scale_0, norm_1, final_norm))
    cos_t, sin_t = _rope_tables_t(seq)

    za, bx, bg, h0_t, win0, g_wg, wout0 = _l0_in_proj(xs, g0, win0_shard, l0_shards)
    win1_t, wout1 = l1_weights(za)
    wg = g_wg.reshape(N_DEV, 4, 32, GDIM).transpose(1, 0, 2, 3).reshape(4, GDIM, GDIM)
    wg_t = jnp.swapaxes(wg, 1, 2)
    x1 = _l0_mix_fwd(za, bx, bg, xs, ws, bias, gv, wg, scale, wout0)
    win1_t, wout1, x1 = lax.optimization_barrier((win1_t, wout1, x1))
    qt, kt, vt, gatet, h1_t = _l1_in_proj(x1, g1, win1_t, cos_t, sin_t)
    dx2, dx2b, att, lse, loss_part, d_gf, d_wout1, d_wout1_wire = _l1_attn_fwd(
        qt, kt, vt, gatet, x1, tgt, wout1, gf, sink_1)

    dq_r, dgate, dk_pad, dv_pad, d_sink = _l1_attn_bwd(dx2b, wout1, qt, kt, vt, gatet, att, lse, sink_1)
    dk_r = dk_pad[:, BLK:BLK + seq]
    dv = dv_pad[:, BLK:BLK + seq]
    dx1, dx1b, dz1_t, d_g1 = _l1_in_proj_bwd(dq_r, dk_r, dv, dgate, cos_t, sin_t, win1_t, x1, g1, dx2)
    d_win1, d_win1_wire = _dw_matmul(h1_t, dz1_t, "dw_in_1", b_transposed=True, tn=1280, col_block=MIX1_IN // N_DEV)
    rows = lambda t: t.reshape(N_DEV, t.shape[0] // N_DEV, t.shape[1])
    scatter("l1", [d_win1, rows(d_wout1)], [d_win1_wire, rows(d_wout1_wire)])

    dz0, dp, cat_t, d_ws, _, d_gv, d_scale, d_wg, d_b = _l0_mix_bwd(
        dx1b, wout0, za, bx, bg, ws, ws_t, bias, gv, wg, wg_t, scale)
    dz0 = _l0_pool_bwd(dp, dz0)
    d_win0, d_win0_wire = _dw_matmul(h0_t, dz0, "dw_in_0", tn=1280, col_block=MIX0_IN // N_DEV)
    d_wg_blocks = _device_blocks(d_wg, 1)
    scatter("in0", [d_win0, d_wg_blocks], [d_win0_wire, d_wg_blocks])
    cat_t, _ = lax.optimization_barrier((cat_t, d_win0))
    d_wout0, d_wout0_wire = _dw_matmul(cat_t, dx1b, "dw_out_0")
    scatter("out0", [rows(d_wout0)], [rows(d_wout0_wire)])
    small_early(dict(a_v_norm_0=d_gv, a_spatial_w_0=d_ws, a_spatial_b_0=d_b.reshape(4, 8, CHUNK)[:, 0, :],
                     b_scale_0=d_scale, norm_1=d_g1, sink_1=d_sink[:, 0], final_norm=d_gf), loss_part)
    dz0, _ = lax.optimization_barrier((dz0, d_wout0))
    return _l0_in_proj_bwd(dz0, win0, xs, g0, dx1)
```

```python
import jax
import jax.numpy as jnp
from jax import lax
from jax.experimental import pallas as pl
from jax.experimental.pallas import tpu as pltpu
from jax.experimental.pallas import tpu_sc as plsc

F32 = jnp.float32
BF16 = jnp.bfloat16

D = 1024
EPS = 1e-6
NEG_INF = -1e30
CHUNK = 128
A_GROUPS = 4
POOL_WINDOWS = (2, 4, 8, 16)
POOL_HALO = 8
GDIM = 256
N_HEADS = 16
N_KV = 4
GQA = 4
HD = 64
BLK = 128
ROT_HALF = 8
ROPE_THETA = 500000.0
SCALE = HD ** -0.5
MIX0_IN = 5 * D
MIX1_IN = 2560
KV_W = N_KV * HD
Q_ROWS, K_ROWS, V_ROWS, G_ROWS = (0, D), (D, D + KV_W), (D + KV_W, D + 2 * KV_W), (D + 2 * KV_W, MIX1_IN)
TQ = 512

ADAM_LR = 0.001
ADAM_B1 = 0.9
ADAM_B2 = 0.999
ADAM_EPS = 1e-08
ADAM_WD = 0.01
ADAM_STEP = 10

N_DEV = 8
LANES = 128
MIB = 2 ** 20
MESH = pl.DeviceIdType.MESH


def _params(limit_mib, n_axes=1):
    return pltpu.CompilerParams(vmem_limit_bytes=limit_mib * MIB, dimension_semantics=("arbitrary",) * n_axes)


def _resident(shape):
    nd = len(shape)
    return pl.BlockSpec(shape, lambda *_: (0,) * nd, pipeline_mode=pl.Buffered(1))


def _gelu(x):
    k = 0.7978845608028654
    return 0.5 * x * (1.0 + jnp.tanh(k * (x + 0.044715 * x * x * x)))


def _gelu_and_grad(x):
    k = 0.7978845608028654
    x2 = x * x
    t = jnp.tanh(k * (x + 0.044715 * x * x2))
    g = 0.5 * x * (1.0 + t)
    dg = 0.5 * (1.0 + t) + 0.5 * x * (1.0 - t * t) * (k * (1.0 + 3.0 * 0.044715 * x2))
    return g, dg


def _silu_and_grad(x):
    s = jax.nn.sigmoid(x)
    return x * s, s * (1.0 + x * (1.0 - s))


def _nt(a, b):
    return lax.dot_general(a, b, (((1,), (1,)), ((), ())), preferred_element_type=F32)


def _tn(a, b):
    return lax.dot_general(a, b, (((0,), (0,)), ((), ())), preferred_element_type=F32)


def _mm(a, b):
    return jnp.dot(a, b, preferred_element_type=F32)


def _rope_tables_t(seq):
    inv = ROPE_THETA ** (-jnp.arange(0, 2 * ROT_HALF, 2, dtype=F32) / (2 * ROT_HALF))
    ang = inv[:, None] * jnp.arange(seq, dtype=F32)[None, :]
    return jnp.cos(ang), jnp.sin(ang)


def _rope_t(z, c, s, n_heads, sign):
    parts = []
    for h in range(n_heads):
        b = h * HD
        x1, x2 = z[b:b + ROT_HALF], z[b + ROT_HALF:b + 2 * ROT_HALF]
        if sign > 0:
            parts += [x1 * c - x2 * s, x2 * c + x1 * s]
        else:
            parts += [x1 * c + x2 * s, x2 * c - x1 * s]
        parts.append(z[b + 2 * ROT_HALF:b + HD])
    return jnp.concatenate(parts, axis=0)


N_CHIPS = 4
CHIP_COLS = MIX0_IN // N_CHIPS
IN_PROJ_PIECES = (
    ((0, 0, CHIP_COLS, 0),),
    ((0, CHIP_COLS, CHIP_COLS, 0),),
    ((0, 2 * CHIP_COLS, 3 * D - 2 * CHIP_COLS, 0), (1, 0, 3 * CHIP_COLS - 3 * D, 3 * D - 2 * CHIP_COLS)),
    ((1, 3 * CHIP_COLS - 3 * D, 4 * D - 3 * CHIP_COLS, 0), (2, 0, D, 4 * D - 3 * CHIP_COLS)),
)


def _l0_in_proj(x, g0, w_shard, later_shards):
    seq = x.shape[0]
    tm = 512
    n = seq // tm
    shard_cols = w_shard.shape[1]
    n_arr = 1 + len(later_shards)
    later = range(1, n_arr)
    assert 2 * shard_cols == CHIP_COLS and seq % tm == 0 and n >= 4

    def body(*refs):
        x_ref, g_ref = refs[:2]
        ins = refs[2:2 + n_arr]
        za_ref, bx_ref, bg_ref, ht_ref = refs[2 + n_arr:6 + n_arr]
        gathered = refs[6 + n_arr:6 + 2 * n_arr]
        h_all, w_buf, z32, z16, send_sems, recv_sems, local_sems, load_sems, out_sems = refs[6 + 2 * n_arr:]
        p, i = pl.program_id(0), pl.program_id(1)
        ax, ay, ac = lax.axis_index("x"), lax.axis_index("y"), lax.axis_index("c")
        me, sibling = (ax, ay, ac), (ax, ay, 1 - ac)
        chips = [(ax, ay), (1 - ax, ay), (ax, 1 - ay), (1 - ax, 1 - ay)]
        outs = (za_ref, bx_ref, bg_ref)

        def slot(a, px, py, pc):
            dev = 4 * px + 2 * py + pc
            if a == 0:
                return gathered[0].at[:, pl.ds(pl.multiple_of(dev * shard_cols, LANES), shard_cols)]
            rows = later_shards[a - 1].shape[0]
            return gathered[a].at[pl.ds(pl.multiple_of(dev * rows, 16), rows)]

        def copy(k, a, block, to, from_input=False):
            return pltpu.make_async_remote_copy(
                src_ref=ins[a] if from_input else slot(a, *block), dst_ref=slot(a, *block),
                send_sem=send_sems.at[k, a], recv_sem=recv_sems.at[k, a], device_id=to, device_id_type=MESH)

        def to_sibling(a):
            return copy(0, a, me, sibling, from_input=True)

        def send(j, a):
            return copy(j, a, me, (*chips[j], ac), from_input=True)

        def landed(j, a):
            return copy(j, a, (*chips[j], ac), me)

        def forward(j, a):
            return copy(3 + j, a, (*chips[j], ac), sibling)

        def forwarded(j, a):
            return copy(3 + j, a, (*chips[j], 1 - ac), me)

        def mine(a):
            return pltpu.make_async_copy(ins[a], slot(a, *me), local_sems.at[a])

        def load(chip, q):
            px, py = chip
            cols = pl.ds(pl.multiple_of((2 * px + py) * CHIP_COLS, LANES), CHIP_COLS)
            return pltpu.make_async_copy(gathered[0].at[:, cols], w_buf.at[q % 2], load_sems.at[q % 2])

        def out_copies(q, tile, stage):
            cps = []
            for k, (o, c0, width, z0) in enumerate(IN_PROJ_PIECES[q]):
                src = z32.at[stage, :, pl.ds(z0, width)] if o == 1 else z16.at[stage, :, pl.ds(z0, width)]
                dst = outs[o].at[pl.ds(pl.multiple_of(tile * tm, tm), tm), pl.ds(c0, width)]
                cps.append(pltpu.make_async_copy(src, dst, out_sems.at[stage, k]))
            return cps

        @pl.when((p == 0) & (i == 0))
        def _():
            for a in range(n_arr):
                mine(a).start()
                to_sibling(a).start()
            for jj in range(1, N_CHIPS):
                send(jj, 0).start()
            copy(0, 0, sibling, me).wait_recv()
            mine(0).wait()
            load(chips[0], 0).start()

        for j in range(1, N_CHIPS):
            @pl.when((p == j - 1) & (i == n - 2))
            def _(j=j):
                landed(j, 0).wait_recv()
                forward(j, 0).start()
                if j == 1:
                    for a in later:
                        for jj in range(1, N_CHIPS):
                            send(jj, a).start()

            @pl.when((p == j - 1) & (i == n - 1))
            def _(j=j):
                forwarded(j, 0).wait_recv()
                load(chips[j], j).start()

        @pl.when((p == N_CHIPS - 1) & (i == n - 4))
        def _():
            for jj in range(1, N_CHIPS):
                for a in later:
                    landed(jj, a).wait_recv()
                    forward(jj, a).start()

        @pl.when(i == 0)
        def _():
            load(chips[0], p).wait()

        @pl.when(p == 0)
        def _():
            xf = x_ref[...]
            r = lax.rsqrt(jnp.mean(xf * xf, axis=1, keepdims=True) + EPS)
            h = (xf * r * g_ref[...]).astype(BF16)
            ht_ref[...] = h.T
            h_all[pl.ds(pl.multiple_of(i * tm, tm), tm), :] = h

        def chip_of_pass(pp):
            return (2 * ax + ay) ^ ((pp >> 1) | ((pp & 1) << 1))

        step = p * n + i
        stage = step % 2
        for q in range(N_CHIPS):
            @pl.when((step >= 2) & (chip_of_pass((step - 2) // n) == q))
            def _(q=q):
                for cp in out_copies(q, (step - 2) % n, stage):
                    cp.wait()

        z32[stage] = _mm(h_all[pl.ds(pl.multiple_of(i * tm, tm), tm), :], w_buf[p % 2])
        z16[stage] = z32[stage].astype(BF16)
        for q in range(N_CHIPS):
            @pl.when(chip_of_pass(p) == q)
            def _(q=q):
                for cp in out_copies(q, i, stage):
                    cp.start()

        last = (p == N_CHIPS - 1) & (i == n - 1)
        for q in range(N_CHIPS):
            @pl.when(last & (chip_of_pass(p) == q))
            def _(q=q):
                for cp in out_copies(q, n - 2, 1 - stage) + out_copies(q, n - 1, stage):
                    cp.wait()

        @pl.when(last)
        def _():
            for a in later:
                copy(0, a, sibling, me).wait_recv()
                for jj in range(1, N_CHIPS):
                    forwarded(jj, a).wait_recv()
                    send(jj, a).wait_send()
                mine(a).wait()
            for jj in range(1, N_CHIPS):
                send(jj, 0).wait_send()
            for a in range(n_arr):
                to_sibling(a).wait_send()
                for jj in range(1, N_CHIPS):
                    forward(jj, a).wait_send()

    any_spec = pl.BlockSpec(memory_space=pl.ANY)
    first_pass_tile = lambda p, i: jnp.where(p == 0, i, n - 1)
    return pl.pallas_call(
        body, grid=(N_CHIPS, n), name="l0_in_proj",
        out_shape=[jax.ShapeDtypeStruct((seq, 3 * D), BF16), jax.ShapeDtypeStruct((seq, D), F32),
                   jax.ShapeDtypeStruct((seq, D), BF16), jax.ShapeDtypeStruct((D, seq), BF16),
                   jax.ShapeDtypeStruct((D, MIX0_IN), BF16)]
        + [jax.ShapeDtypeStruct((N_DEV * t.shape[0], t.shape[1]), t.dtype) for t in later_shards],
        in_specs=[pl.BlockSpec((tm, D), lambda p, i: (first_pass_tile(p, i), 0)), _resident((1, D))] + [any_spec] * n_arr,
        out_specs=[any_spec, any_spec, any_spec, pl.BlockSpec((D, tm), lambda p, i: (0, first_pass_tile(p, i)))]
        + [any_spec] * n_arr,
        scratch_shapes=[pltpu.VMEM((seq, D), BF16), pltpu.VMEM((2, D, CHIP_COLS), BF16),
                        pltpu.VMEM((2, tm, CHIP_COLS), F32), pltpu.VMEM((2, tm, CHIP_COLS), BF16),
                        pltpu.SemaphoreType.DMA((7, n_arr)), pltpu.SemaphoreType.DMA((7, n_arr)),
                        pltpu.SemaphoreType.DMA((n_arr,)), pltpu.SemaphoreType.DMA((2,)), pltpu.SemaphoreType.DMA((2, 2))],
        compiler_params=_params(48, 2),
    )(x, g0, w_shard, *later_shards)


POOL_EXT = 40


def _fill_halo(ext_ref, cur, prev_ref, next_ref, i, n_tiles, ts):
    ext_ref[pl.ds(0, POOL_HALO), :] = jnp.where(i > 0, prev_ref[...], 0.0)
    ext_ref[pl.ds(POOL_HALO, ts), :] = cur
    ext_ref[pl.ds(POOL_HALO + ts, POOL_HALO), :] = jnp.where(i < n_tiles - 1, next_ref[...], 0.0)
    ext_ref[pl.ds(2 * POOL_HALO + ts, POOL_EXT - 2 * POOL_HALO), :] = jnp.zeros((POOL_EXT - 2 * POOL_HALO, D), F32)


def _window_sums(src_ref, tmp_refs, ts, cols, w, shift):
    if w == 2:
        return src_ref[pl.ds(POOL_HALO - 1 + shift, ts), cols] + src_ref[pl.ds(POOL_HALO + shift, ts), cols]
    d2, d4, d8 = tmp_refs
    n2, n4, n8 = ts + 32, ts + 24, ts + 16
    d2[pl.ds(0, n2), :] = src_ref[pl.ds(0, n2), cols] + src_ref[pl.ds(1, n2), cols]
    if w == 4:
        return d2[pl.ds(POOL_HALO - 2 + shift, ts), :] + d2[pl.ds(POOL_HALO + shift, ts), :]
    d4[pl.ds(0, n4), :] = d2[pl.ds(0, n4), :] + d2[pl.ds(2, n4), :]
    if w == 8:
        return d4[pl.ds(POOL_HALO - 4 + shift, ts), :] + d4[pl.ds(POOL_HALO + shift, ts), :]
    d8[pl.ds(0, n8), :] = d4[pl.ds(0, n8), :] + d4[pl.ds(4, n8), :]
    return d8[pl.ds(shift, ts), :] + d8[pl.ds(POOL_HALO + shift, ts), :]


def _pool_scratch(ts):
    return [pltpu.VMEM((ts + POOL_EXT, D), F32)] + [pltpu.VMEM((ts + POOL_EXT, GDIM), F32)] * 3


def _pool_forward(xe_ref, tmp_refs, ts, t0, seq):
    tg = t0 + lax.broadcasted_iota(jnp.int32, (ts, 1), 0)
    outs = []
    for gi, w in enumerate(POOL_WINDOWS):
        hw = w // 2
        cols = slice(gi * GDIM, (gi + 1) * GDIM)
        cnt = (jnp.minimum(tg + hw, seq) - jnp.maximum(tg - hw, 0)).astype(F32)
        outs.append(_window_sums(xe_ref, tmp_refs, ts, cols, w, 0) / cnt - xe_ref[pl.ds(POOL_HALO, ts), cols])
    return jnp.concatenate(outs, axis=1)


def _spatial_mix(ws_ref, vnb, bias, ts):
    rows = []
    for c in range(ts // CHUNK):
        vc = vnb[c * CHUNK:(c + 1) * CHUNK, :]
        rows.append(jnp.concatenate(
            [_mm(ws_ref[h], vc[:, h * GDIM:(h + 1) * GDIM]) for h in range(A_GROUPS)], axis=1) + bias)
    return jnp.concatenate(rows, axis=0)


def _halo_specs(ts, seq, width):
    per = ts // POOL_HALO
    last = seq // POOL_HALO - 1
    prev = pl.BlockSpec((POOL_HALO, width), lambda i: (jnp.maximum(i * per - 1, 0), 0))
    nxt = pl.BlockSpec((POOL_HALO, width), lambda i: (jnp.minimum((i + 1) * per, last), 0))
    return prev, nxt


def _l0_mix_fwd(za, bx, bg, x, ws, bias, gv, wg, scale, wout):
    seq = x.shape[0]
    ts = 512
    n_tiles = seq // ts

    def body(za_ref, bx_ref, bxp_ref, bxn_ref, bg_ref, x_ref, ws_ref, bias_ref, gv_ref, wg_ref, sc_ref, wo_ref,
             x1_ref, xe_ref, *tmp_refs):
        i = pl.program_id(0)
        vg = _gelu(za_ref[:, D:2 * D].astype(F32))
        rv = lax.rsqrt(jnp.mean(vg * vg, axis=1, keepdims=True) + EPS)
        vnb = (vg * rv * gv_ref[...]).astype(BF16)
        mixed = _spatial_mix(ws_ref, vnb, bias_ref[...], ts)

        _fill_halo(xe_ref, bx_ref[...], bxp_ref, bxn_ref, i, n_tiles, ts)
        pb = _pool_forward(xe_ref, tmp_refs, ts, i * ts, seq).astype(BF16)
        ypre = jnp.concatenate([_mm(pb[:, g * GDIM:(g + 1) * GDIM], wg_ref[g]) for g in range(4)], axis=1)

        u = _gelu(za_ref[:, 0:D].astype(F32))
        ag = za_ref[:, 2 * D:3 * D].astype(F32)
        ya = (u * mixed * (ag * jax.nn.sigmoid(ag))).astype(BF16)
        out_a = _mm(ya, wo_ref[0:D, :])

        bgf = bg_ref[...].astype(F32)
        yb = (ypre * sc_ref[...] * (bgf * jax.nn.sigmoid(bgf))).astype(BF16)
        x1_ref[...] = x_ref[...] + out_a + _mm(yb, wo_ref[D:2 * D, :])

    prev, nxt = _halo_specs(ts, seq, D)
    row = lambda w: pl.BlockSpec((ts, w), lambda i: (i, 0))
    return pl.pallas_call(
        body, grid=(n_tiles,), name="l0_mix_fwd",
        out_shape=jax.ShapeDtypeStruct((seq, D), F32),
        in_specs=[row(3 * D), row(D), prev, nxt, row(D), row(D), _resident((4, CHUNK, CHUNK)), _resident((CHUNK, D)),
                  _resident((1, D)), _resident((4, GDIM, GDIM)), _resident((1, D)), _resident((2 * D, D))],
        out_specs=row(D),
        scratch_shapes=_pool_scratch(ts),
        compiler_params=_params(56),
    )(za, bx, bx, bx, bg, x, ws, bias, gv, wg, scale, wout)


def _l1_in_proj(x1, g1, w_t, cos_t, sin_t):
    seq = x1.shape[0]
    tm = 512

    def body(x_ref, g_ref, wt_ref, c_ref, s_ref, q_ref, k_ref, v_ref, gate_ref, ht_ref):
        xf = x_ref[...]
        r = lax.rsqrt(jnp.mean(xf * xf, axis=1, keepdims=True) + EPS)
        ht = (xf * r * g_ref[...]).astype(BF16).T
        ht_ref[...] = ht
        c, s = c_ref[...], s_ref[...]
        q_ref[...] = (_rope_t(_mm(wt_ref[Q_ROWS[0]:Q_ROWS[1], :], ht), c, s, N_HEADS, 1) * SCALE).astype(BF16)
        k_ref[...] = _rope_t(_mm(wt_ref[K_ROWS[0]:K_ROWS[1], :], ht), c, s, N_KV, 1).astype(BF16)
        v_ref[...] = _mm(wt_ref[V_ROWS[0]:V_ROWS[1], :], ht).astype(BF16)
        gate_ref[...] = _mm(wt_ref[G_ROWS[0]:G_ROWS[1], :], ht).astype(BF16)

    col = lambda rows: pl.BlockSpec((rows, tm), lambda i: (0, i))
    return pl.pallas_call(
        body, grid=(seq // tm,), name="l1_in_proj",
        out_shape=(jax.ShapeDtypeStruct((D, seq), BF16), jax.ShapeDtypeStruct((KV_W, seq), BF16),
                   jax.ShapeDtypeStruct((KV_W, seq), BF16), jax.ShapeDtypeStruct((D, seq), BF16),
                   jax.ShapeDtypeStruct((D, seq), BF16)),
        in_specs=[pl.BlockSpec((tm, D), lambda i: (i, 0)), _resident((1, D)), _resident((MIX1_IN, D)), col(ROT_HALF),
                  col(ROT_HALF)],
        out_specs=(col(D), col(KV_W), col(KV_W), col(D), col(D)),
        compiler_params=_params(48),
    )(x1, g1, w_t, cos_t, sin_t)


def _band_specs_t(nb, clamp_i):
    per = TQ // BLK
    prev = pl.BlockSpec((KV_W, BLK), lambda i: (0, jnp.maximum(clamp_i(i) * per - 1, 0)))
    cur = pl.BlockSpec((KV_W, TQ), lambda i: (0, clamp_i(i)))
    nxt = pl.BlockSpec((KV_W, BLK), lambda i: (0, jnp.minimum((clamp_i(i) + 1) * per, nb - 1)))
    return [prev, cur, nxt]


def _fill_band(buf, p_ref, c_ref, n_ref):
    buf[:, 0:BLK] = p_ref[...]
    buf[:, BLK:BLK + TQ] = c_ref[...]
    buf[:, BLK + TQ:2 * BLK + TQ] = n_ref[...]


def _band_bias_t(n, nb):
    c = lax.broadcasted_iota(jnp.int32, (BLK, BLK), 0)
    r = lax.broadcasted_iota(jnp.int32, (BLK, BLK), 1)
    first = jnp.where((c >= r) & (n > 0), 0.0, NEG_INF).astype(F32)
    last = jnp.where((c <= r) & (n < nb - 1), 0.0, NEG_INF).astype(F32)
    return jnp.concatenate([first] * HPP, axis=1), jnp.concatenate([last] * HPP, axis=1)


def _masked(st, bias):
    first, last = bias
    return jnp.concatenate([st[0:BLK] + first, st[BLK:2 * BLK], st[2 * BLK:3 * BLK] + last], axis=0)


AUG = 16


def _ones_rows(n_ones, width):
    return (lax.broadcasted_iota(jnp.int32, (AUG, width), 0) < n_ones).astype(BF16)


def _minus_rows(vec):
    hi = vec.astype(BF16).astype(F32)
    lo = vec - hi
    return jnp.concatenate([-hi, -lo, jnp.zeros((AUG - 2, vec.shape[1]), F32)], axis=0).astype(BF16)


HPP = GQA
FWD_GROUP, BWD_GROUP = 2, 1
BWD_AHEAD = 1


def _heads_t(ref, h0, c0):
    return jnp.concatenate([ref[(h0 + g) * HD:(h0 + g + 1) * HD, c0:c0 + BLK] for g in range(HPP)], axis=1)


def _row4(ref, h0, c0):
    return jnp.concatenate([ref[h0 + g:h0 + g + 1, c0:c0 + BLK] for g in range(HPP)], axis=1)


def _sink_row(sink_ref, h0):
    return jnp.concatenate([jnp.full((1, BLK), sink_ref[h0 + g], F32) for g in range(HPP)], axis=1)


def _l1_attn_fwd(qt, kt, vt, gatet, x1, tgt, wout, gf, sink):
    seq = x1.shape[0]
    nq, nb = seq // TQ, seq // BLK

    def body(q_ref, gate_ref, kp_ref, k_ref, kn_ref, vp_ref, v_ref, vn_ref, x1_ref, tgt_ref, wo_ref, gf_ref, sink_ref,
             dx2_ref, dx2b_ref, att_ref, lse_ref, loss_ref, dgf_ref, dwo_ref, dwo_wire_ref, kbuf, vbuf, att_scr):
        i = pl.program_id(0)

        @pl.when(i == 0)
        def _():
            loss_ref[...] = jnp.zeros_like(loss_ref)
            dgf_ref[...] = jnp.zeros_like(dgf_ref)
            dwo_ref[...] = jnp.zeros_like(dwo_ref)

        _fill_band(kbuf, kp_ref, k_ref, kn_ref)
        _fill_band(vbuf, vp_ref, v_ref, vn_ref)
        ones_row = _ones_rows(1, 3 * BLK)
        groups = [list(range(0, N_HEADS, HPP))[g:g + FWD_GROUP] for g in range(0, N_HEADS // HPP, FWD_GROUP)]
        work = [(j, grp) for j in range(TQ // BLK) for grp in groups]

        def scores(j, passes):
            c0 = j * BLK
            bias = _band_bias_t(i * (TQ // BLK) + j, nb)
            st = dict(c0=c0, passes=passes)
            st["kv_rows"] = [slice(h0 // GQA * HD, (h0 // GQA + 1) * HD) for h0 in passes]
            st["sts"] = [_masked(_tn(kbuf[rows, c0:c0 + 3 * BLK], _heads_t(q_ref, h0, c0)), bias)
                         for h0, rows in zip(passes, st["kv_rows"])]
            return st

        def softmaxes(st):
            st["sks"] = [_sink_row(sink_ref, h0) for h0 in st["passes"]]
            st["ms"] = [jnp.maximum(jnp.max(s_, axis=0, keepdims=True), sk) for s_, sk in zip(st["sts"], st["sks"])]
            st["ps"] = [jnp.exp(s_ - m).astype(BF16) for s_, m in zip(st["sts"], st["ms"])]

        def values(st):
            c0, passes = st["c0"], st["passes"]
            pvs = [_mm(jnp.concatenate([vbuf[rows, c0:c0 + 3 * BLK], ones_row], axis=0), p)
                   for rows, p in zip(st["kv_rows"], st["ps"])]
            lse_rows = []
            for h0, pv, m, sk in zip(passes, pvs, st["ms"], st["sks"]):
                den = pv[HD:HD + 1, :] + jnp.exp(sk - m)
                ot = pv[0:HD, :] / den
                lse = m + jnp.log(den)
                for g in range(HPP):
                    h = h0 + g
                    att_scr[h * HD:(h + 1) * HD, c0:c0 + BLK] = ot[:, g * BLK:(g + 1) * BLK]
                    lse_rows.append(lse[:, g * BLK:(g + 1) * BLK])
            lse_ref[passes[0]:passes[0] + len(lse_rows), c0:c0 + BLK] = jnp.concatenate(lse_rows, axis=0)

        state = scores(*work[0])
        for nxt in work[1:] + [None]:
            following = scores(*nxt) if nxt is not None else None
            softmaxes(state)
            values(state)
            state = following

        att = att_scr[...]
        gate = gate_ref[...].astype(F32)
        yt = (att * (gate * jax.nn.sigmoid(gate))).astype(BF16)
        att_ref[...] = att.astype(BF16)
        x2 = x1_ref[...] + _mm(yt.T, wo_ref[...])
        r = lax.rsqrt(jnp.mean(x2 * x2, axis=1, keepdims=True) + EPS)
        xn = x2 * r
        diff = xn * gf_ref[...] - tgt_ref[...]
        loss_ref[...] += 0.5 * jnp.sum(jnp.mean(diff * diff, axis=1, keepdims=True), axis=0, keepdims=True)
        dout = diff * (1.0 / D)
        dgf_ref[...] += jnp.sum(dout * xn, axis=0, keepdims=True)
        dxn = dout * gf_ref[...]
        dx2 = r * (dxn - xn * jnp.mean(dxn * xn, axis=1, keepdims=True))
        dx2_ref[...] = dx2
        dx2b = dx2.astype(BF16)
        dx2b_ref[...] = dx2b
        dwo_ref[...] += _mm(yt, dx2b)

        @pl.when(i == nq - 1)
        def _():
            dwo_wire_ref[...] = dwo_ref[...].astype(BF16)

    ident = lambda i: i
    row = pl.BlockSpec((TQ, D), lambda i: (i, 0))
    col = lambda rows: pl.BlockSpec((rows, TQ), lambda i: (0, i))
    whole = pl.BlockSpec((D, D), lambda i: (0, 0))
    return pl.pallas_call(
        body, grid=(nq,), name="l1_attn_fwd",
        out_shape=(jax.ShapeDtypeStruct((seq, D), F32), jax.ShapeDtypeStruct((seq, D), BF16),
                   jax.ShapeDtypeStruct((D, seq), BF16),
                   jax.ShapeDtypeStruct((N_HEADS, seq), F32), jax.ShapeDtypeStruct((1, 1), F32),
                   jax.ShapeDtypeStruct((1, D), F32), jax.ShapeDtypeStruct((D, D), F32), jax.ShapeDtypeStruct((D, D), BF16)),
        in_specs=[col(D), col(D)] + _band_specs_t(nb, ident) + _band_specs_t(nb, ident) + [
            row, row, _resident((D, D)), _resident((1, D)), pl.BlockSpec(memory_space=pltpu.SMEM)],
        out_specs=(row, row, col(D), col(N_HEADS), pl.BlockSpec((1, 1), lambda i: (0, 0)),
                   pl.BlockSpec((1, D), lambda i: (0, 0)), whole, whole),
        scratch_shapes=[pltpu.VMEM((KV_W, TQ + 2 * BLK), BF16), pltpu.VMEM((KV_W, TQ + 2 * BLK), BF16),
                        pltpu.VMEM((D, TQ), F32)],
        compiler_params=_params(56),
    )(qt, gatet, kt, kt, kt, vt, vt, vt, x1, tgt, wout, gf, sink)


def _l1_attn_bwd(dx2b, wout, qt, kt, vt, gatet, att, lse, sink):
    seq = dx2b.shape[0]
    nq, nb = seq // TQ, seq // BLK

    def body(dx_ref, wo_ref, q_ref, gate_ref, kp_ref, k_ref, kn_ref, vp_ref, v_ref, vn_ref, att_ref, lse_ref, sink_ref,
             dq_ref, dgate_ref, dk_ref, dv_ref, dsink_ref, kbuf, vbuf, dkacc, dvacc, dat_scr, delta_scr, dsacc):
        i = pl.program_id(0)

        @pl.when(i == 0)
        def _():
            dkacc[...] = jnp.zeros_like(dkacc)
            dvacc[...] = jnp.zeros_like(dvacc)
            dsacc[...] = jnp.zeros_like(dsacc)

        @pl.when(i > 0)
        def _():
            for acc in (dkacc, dvacc):
                acc[:, 0:2 * BLK] = acc[:, TQ:TQ + 2 * BLK]
                acc[:, 2 * BLK:2 * BLK + TQ] = jnp.zeros((KV_W, TQ), F32)

        @pl.when(i < nq)
        def _():
            _fill_band(kbuf, kp_ref, k_ref, kn_ref)
            _fill_band(vbuf, vp_ref, v_ref, vn_ref)
            dyt = _nt(wo_ref[...], dx_ref[...])
            sg, dsg = _silu_and_grad(gate_ref[...].astype(F32))
            attf = att_ref[...].astype(F32)
            dat = dyt * sg
            dat_scr[...] = dat.astype(BF16)
            dgate_ref[...] = (dyt * attf * dsg).astype(BF16)
            dl = dat * attf
            delta_scr[...] = jnp.concatenate(
                [jnp.sum(dl[h * HD:(h + 1) * HD, :], axis=0, keepdims=True) for h in range(N_HEADS)], axis=0)
            ones_rows = _ones_rows(2, 3 * BLK)
            groups = [list(range(0, N_HEADS, HPP))[g:g + BWD_GROUP] for g in range(0, N_HEADS // HPP, BWD_GROUP)]
            work = [(j, grp) for j in range(TQ // BLK) for grp in groups]

            def scores(j, passes):
                c0 = j * BLK
                st = dict(c0=c0, passes=passes, bias=_band_bias_t(i * (TQ // BLK) + j, nb))
                st["kv_rows"] = [slice(h0 // GQA * HD, (h0 // GQA + 1) * HD) for h0 in passes]
                st["q4s"] = [_heads_t(q_ref, h0, c0) for h0 in passes]
                st["do4s"] = [_heads_t(dat_scr, h0, c0) for h0 in passes]
                st["lse4s"] = [_row4(lse_ref, h0, c0) for h0 in passes]
                st["delta4s"] = [_row4(delta_scr, h0, c0) for h0 in passes]
                st["kths"] = [kbuf[rows, c0:c0 + 3 * BLK] for rows in st["kv_rows"]]
                st["sts"] = [_tn(jnp.concatenate([kth, ones_rows], axis=0),
                                 jnp.concatenate([q4, _minus_rows(lse4)], axis=0))
                             for kth, q4, lse4 in zip(st["kths"], st["q4s"], st["lse4s"])]
                st["dpds"] = [_tn(jnp.concatenate([vbuf[rows, c0:c0 + 3 * BLK], ones_rows], axis=0),
                                  jnp.concatenate([do4, _minus_rows(delta4)], axis=0))
                              for rows, do4, delta4 in zip(st["kv_rows"], st["do4s"], st["delta4s"])]
                return st

            def elementwise(st):
                st["ps"] = [jnp.exp(_masked(s_, st["bias"])) for s_ in st["sts"]]
                st["dss"] = [(p * dpd).astype(BF16) for p, dpd in zip(st["ps"], st["dpds"])]

            def gradients(st):
                c0 = st["c0"]
                dq4s = [_mm(kth, ds) * SCALE for kth, ds in zip(st["kths"], st["dss"])]
                dks = [_nt(q4, ds) for q4, ds in zip(st["q4s"], st["dss"])]
                dvs = [_nt(do4, p.astype(BF16)) for do4, p in zip(st["do4s"], st["ps"])]
                for h0, rows, dq4, dk, dv, lse4, delta4 in zip(st["passes"], st["kv_rows"], dq4s, dks, dvs, st["lse4s"],
                                                               st["delta4s"]):
                    dkacc[rows, c0:c0 + 3 * BLK] += dk
                    dvacc[rows, c0:c0 + 3 * BLK] += dv
                    dsk = -jnp.exp(_sink_row(sink_ref, h0) - lse4) * delta4
                    for g in range(HPP):
                        h = h0 + g
                        dq_ref[h * HD:(h + 1) * HD, c0:c0 + BLK] = dq4[:, g * BLK:(g + 1) * BLK].astype(BF16)
                        dsacc[h:h + 1, :] += dsk[:, g * BLK:(g + 1) * BLK]

            ahead = [scores(*w) for w in work[:BWD_AHEAD]]
            for n in range(len(work)):
                if n + BWD_AHEAD < len(work):
                    ahead.append(scores(*work[n + BWD_AHEAD]))
                state = ahead.pop(0)
                elementwise(state)
                gradients(state)

        dk_ref[...] = dkacc[:, 0:TQ].astype(BF16)
        dv_ref[...] = dvacc[:, 0:TQ].astype(BF16)

        @pl.when(i == nq)
        def _():
            dsink_ref[...] = jnp.broadcast_to(jnp.sum(dsacc[...], axis=1, keepdims=True), (N_HEADS, LANES))

    clamp = lambda i: jnp.minimum(i, nq - 1)
    row = pl.BlockSpec((TQ, D), lambda i: (clamp(i), 0))
    col = lambda rows: pl.BlockSpec((rows, TQ), lambda i: (0, clamp(i)))
    pad = pl.BlockSpec((KV_W, TQ), lambda i: (0, i))
    return pl.pallas_call(
        body, grid=(nq + 1,), name="l1_attn_bwd",
        out_shape=(jax.ShapeDtypeStruct((D, seq), BF16), jax.ShapeDtypeStruct((D, seq), BF16),
                   jax.ShapeDtypeStruct((KV_W, seq + TQ), BF16), jax.ShapeDtypeStruct((KV_W, seq + TQ), BF16),
                   jax.ShapeDtypeStruct((N_HEADS, LANES), F32)),
        in_specs=[row, _resident((D, D)), col(D), col(D)] + _band_specs_t(nb, clamp) + _band_specs_t(nb, clamp) + [
            col(D), col(N_HEADS), pl.BlockSpec(memory_space=pltpu.SMEM)],
        out_specs=(col(D), col(D), pad, pad, pl.BlockSpec((N_HEADS, LANES), lambda i: (0, 0))),
        scratch_shapes=[pltpu.VMEM((KV_W, TQ + 2 * BLK), BF16), pltpu.VMEM((KV_W, TQ + 2 * BLK), BF16),
                        pltpu.VMEM((KV_W, TQ + 2 * BLK), F32), pltpu.VMEM((KV_W, TQ + 2 * BLK), F32),
                        pltpu.VMEM((D, TQ), BF16), pltpu.VMEM((N_HEADS, TQ), F32), pltpu.VMEM((N_HEADS, LANES), F32)],
        compiler_params=_params(56),
    )(dx2b, wout, qt, gatet, kt, kt, kt, vt, vt, vt, att, lse, sink)


def _l1_in_proj_bwd(dq_r, dk_r, dv, dgate, cos_t, sin_t, w_t, x1, g1, dx2):
    seq = x1.shape[0]
    tm = 512

    def body(dq_ref, dk_ref, dv_ref, dg_ref, c_ref, s_ref, w_ref, x_ref, g_ref, dres_ref,
             dx_ref, dxb_ref, dz_ref, dn_ref):
        @pl.when(pl.program_id(0) == 0)
        def _():
            dn_ref[...] = jnp.zeros_like(dn_ref)

        c, s = c_ref[...], s_ref[...]
        dq = _rope_t(dq_ref[...].astype(F32), c, s, N_HEADS, -1).astype(BF16)
        dk = _rope_t(dk_ref[...].astype(F32), c, s, N_KV, -1).astype(BF16)
        dz = jnp.concatenate([dq, dk, dv_ref[...], dg_ref[...]], axis=0)
        dz_ref[...] = dz
        dh = _tn(dz, w_ref[...])
        xf = x_ref[...]
        r = lax.rsqrt(jnp.mean(xf * xf, axis=1, keepdims=True) + EPS)
        xn = xf * r
        dn_ref[...] += jnp.sum(dh * xn, axis=0, keepdims=True)
        dxn = dh * g_ref[...]
        dx = dres_ref[...] + r * (dxn - xn * jnp.mean(dxn * xn, axis=1, keepdims=True))
        dx_ref[...] = dx
        dxb_ref[...] = dx.astype(BF16)

    row = pl.BlockSpec((tm, D), lambda i: (i, 0))
    col = lambda rows: pl.BlockSpec((rows, tm), lambda i: (0, i))
    return pl.pallas_call(
        body, grid=(seq // tm,), name="l1_in_proj_bwd",
        out_shape=(jax.ShapeDtypeStruct((seq, D), F32), jax.ShapeDtypeStruct((seq, D), BF16),
                   jax.ShapeDtypeStruct((MIX1_IN, seq), BF16), jax.ShapeDtypeStruct((1, D), F32)),
        in_specs=[col(D), col(KV_W), col(KV_W), col(D), col(ROT_HALF), col(ROT_HALF), _resident((MIX1_IN, D)), row,
                  _resident((1, D)), row],
        out_specs=(row, row, col(MIX1_IN), pl.BlockSpec((1, D), lambda i: (0, 0))),
        compiler_params=_params(48),
    )(dq_r, dk_r, dv, dgate, cos_t, sin_t, w_t, x1, g1, dx2)


def _l0_mix_bwd(dx1b, wout, za, bx, bg, ws, ws_t, bias, gv, wg, wg_t, scale):
    seq = dx1b.shape[0]
    ts = 256
    n_tiles = seq // ts

    def body(dx_ref, wo_ref, za_ref, bx_ref, bxp_ref, bxn_ref, bg_ref, ws_ref, wst_ref, bias_ref, gv_ref, wg_ref,
             wgt_ref, sc_ref,
             dz_ref, dp_ref, catt_ref, dws_ref, dbias_ref, dgv_ref, dsc_ref, dwg_ref, db_ref, xe_ref, *tmp_refs):
        i = pl.program_id(0)

        @pl.when(i == 0)
        def _():
            for r_ in (dws_ref, dbias_ref, dgv_ref, dsc_ref, dwg_ref, db_ref):
                r_[...] = jnp.zeros_like(r_)

        dxb = dx_ref[...]
        dya = _nt(dxb, wo_ref[0:D, :])
        dyb = _nt(dxb, wo_ref[D:2 * D, :])

        vg, dvg_dz = _gelu_and_grad(za_ref[:, D:2 * D].astype(F32))
        rv = lax.rsqrt(jnp.mean(vg * vg, axis=1, keepdims=True) + EPS)
        vnorm = vg * rv
        gvw = gv_ref[...]
        vnb = (vnorm * gvw).astype(BF16)
        mixed = _spatial_mix(ws_ref, vnb, bias_ref[...], ts)

        _fill_halo(xe_ref, bx_ref[...], bxp_ref, bxn_ref, i, n_tiles, ts)
        pb = _pool_forward(xe_ref, tmp_refs, ts, i * ts, seq).astype(BF16)
        ypre = jnp.concatenate([_mm(pb[:, g * GDIM:(g + 1) * GDIM], wg_ref[g]) for g in range(4)], axis=1)

        u, du = _gelu_and_grad(za_ref[:, 0:D].astype(F32))
        sga, dsga = _silu_and_grad(za_ref[:, 2 * D:3 * D].astype(F32))
        um = u * mixed
        ya = (um * sga).astype(BF16)
        t = dya * sga
        dz_ref[:, 0:D] = (t * mixed * du).astype(BF16)
        dz_ref[:, 2 * D:3 * D] = (dya * um * dsga).astype(BF16)
        dmixed = t * u
        dmb = dmixed.astype(BF16)
        dvn_rows = []
        for c in range(ts // CHUNK):
            rows = slice(c * CHUNK, (c + 1) * CHUNK)
            parts = []
            for h in range(A_GROUPS):
                cols = slice(h * GDIM, (h + 1) * GDIM)
                dws_ref[h] += _nt(dmb[rows, cols], vnb[rows, cols])
                parts.append(_mm(wst_ref[h], dmb[rows, cols]))
            dvn_rows.append(jnp.concatenate(parts, axis=1))

        sc = sc_ref[...]
        y = ypre * sc
        sgb, dsgb = _silu_and_grad(bg_ref[...].astype(F32))
        yb = (y * sgb).astype(BF16)
        dy_b = dyb * sgb
        dz_ref[:, 3 * D:4 * D] = jnp.zeros((ts, D), BF16)
        dz_ref[:, 4 * D:5 * D] = (dyb * y * dsgb).astype(BF16)
        dsc_ref[...] += jnp.sum(dy_b * ypre, axis=0, keepdims=True)
        dypre = (dy_b * sc).astype(BF16)
        dps = []
        for g in range(4):
            cols = slice(g * GDIM, (g + 1) * GDIM)
            dwg_ref[g] += _tn(pb[:, cols], dypre[:, cols])
            dps.append(_mm(dypre[:, cols], wgt_ref[g]))

        dbias = dmixed[0:CHUNK, :]
        for c in range(1, ts // CHUNK):
            dbias = dbias + dmixed[c * CHUNK:(c + 1) * CHUNK, :]
        dbias_ref[...] += dbias
        dvn = jnp.concatenate(dvn_rows, axis=0)
        dgv_ref[...] += jnp.sum(dvn * vnorm, axis=0, keepdims=True)
        dxn = dvn * gvw
        dvg = rv * (dxn - vnorm * jnp.mean(dxn * vnorm, axis=1, keepdims=True))
        dz_ref[:, D:2 * D] = (dvg * dvg_dz).astype(BF16)

        dp_ref[...] = jnp.concatenate(dps, axis=1)
        catt_ref[...] = jnp.concatenate([ya, yb], axis=1).T

        @pl.when(i == n_tiles - 1)
        def _():
            for h in range(A_GROUPS):
                tot = jnp.sum(dbias_ref[:, h * GDIM:(h + 1) * GDIM].T, axis=0, keepdims=True)
                db_ref[pl.ds(h * 8, 8), :] = jnp.broadcast_to(tot, (8, CHUNK))

    prev, nxt = _halo_specs(ts, seq, D)
    row = lambda w_: pl.BlockSpec((ts, w_), lambda i: (i, 0))
    acc = lambda shape: pl.BlockSpec(shape, lambda i: (0,) * len(shape))
    return pl.pallas_call(
        body, grid=(n_tiles,), name="l0_mix_bwd",
        out_shape=(jax.ShapeDtypeStruct((seq, MIX0_IN), BF16), jax.ShapeDtypeStruct((seq, D), F32),
                   jax.ShapeDtypeStruct((2 * D, seq), BF16),
                   jax.ShapeDtypeStruct((4, CHUNK, CHUNK), F32), jax.ShapeDtypeStruct((CHUNK, D), F32),
                   jax.ShapeDtypeStruct((1, D), F32), jax.ShapeDtypeStruct((1, D), F32),
                   jax.ShapeDtypeStruct((4, GDIM, GDIM), F32), jax.ShapeDtypeStruct((32, CHUNK), F32)),
        in_specs=[row(D), _resident((2 * D, D)), row(3 * D), row(D), prev, nxt, row(D), _resident((4, CHUNK, CHUNK)),
                  _resident((4, CHUNK, CHUNK)), _resident((CHUNK, D)), _resident((1, D)), _resident((4, GDIM, GDIM)),
                  _resident((4, GDIM, GDIM)), _resident((1, D))],
        out_specs=(row(MIX0_IN), row(D), pl.BlockSpec((2 * D, ts), lambda i: (0, i)),
                   acc((4, CHUNK, CHUNK)), acc((CHUNK, D)), acc((1, D)), acc((1, D)), acc((4, GDIM, GDIM)),
                   acc((32, CHUNK))),
        scratch_shapes=_pool_scratch(ts),
        compiler_params=_params(56),
    )(dx1b, wout, za, bx, bx, bx, bg, ws, ws_t, bias, gv, wg, wg_t, scale)


def _l0_pool_bwd(dp, dz):
    seq = dp.shape[0]
    ts = 512
    n_tiles = seq // ts
    ext = ts + 2 * POOL_HALO

    def body(dp_ref, dpp_ref, dpn_ref, dz_ref, out_ref, qe_ref, *tmp_refs):
        i = pl.program_id(0)
        _fill_halo(qe_ref, dp_ref[...], dpp_ref, dpn_ref, i, n_tiles, ts)
        te = i * ts - POOL_HALO + lax.broadcasted_iota(jnp.int32, (ext, 1), 0)
        for gi, w in enumerate(POOL_WINDOWS):
            hw = w // 2
            cols = slice(gi * GDIM, (gi + 1) * GDIM)
            cnt = jnp.maximum(jnp.minimum(te + hw, seq) - jnp.maximum(te - hw, 0), 1).astype(F32)
            qe_ref[pl.ds(0, ext), cols] = qe_ref[pl.ds(0, ext), cols] / cnt
        outs = []
        for gi, w in enumerate(POOL_WINDOWS):
            cols = slice(gi * GDIM, (gi + 1) * GDIM)
            outs.append(_window_sums(qe_ref, tmp_refs, ts, cols, w, 1) - dp_ref[:, cols])
        out_ref[...] = jnp.concatenate(outs, axis=1).astype(BF16)

    prev, nxt = _halo_specs(ts, seq, D)
    row = pl.BlockSpec((ts, D), lambda i: (i, 0))
    return pl.pallas_call(
        body, grid=(n_tiles,), name="l0_pool_bwd",
        out_shape=jax.ShapeDtypeStruct(dz.shape, BF16),
        in_specs=[row, prev, nxt, pl.BlockSpec(memory_space=pl.ANY)],
        out_specs=pl.BlockSpec((ts, D), lambda i: (i, 3)),
        input_output_aliases={3: 0},
        scratch_shapes=_pool_scratch(ts),
        compiler_params=_params(32),
    )(dp, dp, dp, dz)


def _l0_in_proj_bwd(dz, w, x, g0, dx1):
    seq = x.shape[0]
    tm = 512

    def body(dz_ref, w_ref, x_ref, g_ref, dres_ref, dx_ref, dn_ref):
        @pl.when(pl.program_id(0) == 0)
        def _():
            dn_ref[...] = jnp.zeros_like(dn_ref)

        dh = _nt(dz_ref[...], w_ref[...])
        xf = x_ref[...]
        r = lax.rsqrt(jnp.mean(xf * xf, axis=1, keepdims=True) + EPS)
        xn = xf * r
        dn_ref[...] += jnp.sum(dh * xn, axis=0, keepdims=True)
        dxn = dh * g_ref[...]
        dx_ref[...] = dres_ref[...] + r * (dxn - xn * jnp.mean(dxn * xn, axis=1, keepdims=True))

    row = lambda w_: pl.BlockSpec((tm, w_), lambda i: (i, 0))
    return pl.pallas_call(
        body, grid=(seq // tm,), name="l0_in_proj_bwd",
        out_shape=(jax.ShapeDtypeStruct((seq, D), F32), jax.ShapeDtypeStruct((1, D), F32)),
        in_specs=[row(MIX0_IN), _resident((D, MIX0_IN)), row(D), _resident((1, D)), row(D)],
        out_specs=(row(D), pl.BlockSpec((1, D), lambda i: (0, 0))),
        compiler_params=_params(56),
    )(dz, w, x, g0, dx1)


def _dw_matmul(a_t, b, name, b_transposed=False, tn=1024, ts=1024, col_block=None):
    k, seq = a_t.shape
    n = b.shape[0] if b_transposed else b.shape[1]
    tn = min(n, tn)
    assert seq % ts == 0 and n % tn == 0 and (col_block is None or tn % col_block == 0)
    n_s = seq // ts
    per = 1 if col_block is None else tn // col_block

    def body(a_ref, b_ref, o_ref, ob_ref, acc_ref):
        s = pl.program_id(1)

        @pl.when(s == 0)
        def _():
            acc_ref[...] = jnp.zeros_like(acc_ref)

        acc_ref[...] += _nt(a_ref[...], b_ref[...]) if b_transposed else _mm(a_ref[...], b_ref[...])

        @pl.when(s == n_s - 1)
        def _():
            acc = acc_ref[...]
            if col_block is None:
                o_ref[...] = acc
                ob_ref[...] = acc.astype(BF16)
            else:
                for i in range(per):
                    piece = acc[:, i * col_block:(i + 1) * col_block]
                    o_ref[i] = piece
                    ob_ref[i] = piece.astype(BF16)

    b_spec = (pl.BlockSpec((tn, ts), lambda j, s: (j, s)) if b_transposed else pl.BlockSpec((ts, tn), lambda j, s: (s, j)))
    if col_block is None:
        shape, o_spec = (k, n), pl.BlockSpec((k, tn), lambda j, s: (0, j))
    else:
        shape, o_spec = (n // col_block, k, col_block), pl.BlockSpec((per, k, col_block), lambda j, s: (j, 0, 0))
    return pl.pallas_call(
        body, grid=(n // tn, n_s), name=name,
        out_shape=(jax.ShapeDtypeStruct(shape, F32), jax.ShapeDtypeStruct(shape, BF16)),
        in_specs=[pl.BlockSpec((k, ts), lambda j, s: (0, s)), b_spec],
        out_specs=(o_spec, o_spec),
        scratch_shapes=[pltpu.VMEM((k, tn), F32)],
        compiler_params=_params(56, 2),
    )(a_t, b)


ROW_TILES = 8


def _cast_shards(shards):
    n = len(shards)

    def body(*refs):
        for a in range(n):
            refs[n + a][...] = refs[a][...].astype(BF16)

    vm = pl.BlockSpec(memory_space=pltpu.VMEM)
    return pl.pallas_call(body, name="cast_weights", out_shape=[jax.ShapeDtypeStruct(t.shape, BF16) for t in shards],
                          in_specs=[vm] * n, out_specs=[vm] * n, compiler_params=_params(32, 0))(*shards)


def _adamw_math(w, g, m, v):
    m2 = ADAM_B1 * m + (1.0 - ADAM_B1) * g
    v2 = ADAM_B2 * v + (1.0 - ADAM_B2) * (g * g)
    m_hat = m2 / (1.0 - ADAM_B1 ** ADAM_STEP)
    v_hat = v2 / (1.0 - ADAM_B2 ** ADAM_STEP)
    delta = -ADAM_LR * (m_hat / (jnp.sqrt(v_hat) + ADAM_EPS) + ADAM_WD * w)
    return delta, m2, v2


def _final_sum_adamw(g_list, recv_list, me, w_list, m_list, v_list):
    n = len(w_list)

    def body(me_ref, *refs):
        own, recv, w, m, v = (refs[k * n:(k + 1) * n] for k in range(5))
        outs = [refs[(5 + k) * n:(6 + k) * n] for k in range(4)]
        for a in range(n):
            g = own[a][...]
            for k in range(N_DEV - 1):
                g = g + recv[a][k].astype(F32)
            delta, m2, v2 = _adamw_math(w[a][...], g, m[a][...], v[a][...])
            for o_ref, val in zip((outs[0][a], outs[1][a], outs[2][a], outs[3][a]), (g, delta, m2, v2)):
                o_ref[...] = val

    own_specs, flat, wire, shapes = [], [], [], []
    for t in w_list:
        rows, width = t.shape
        tr = rows // ROW_TILES
        own_specs.append(pl.BlockSpec((None, tr, width), lambda i, me: (me[0], i, 0)))
        flat.append(pl.BlockSpec((tr, width), lambda i, me: (i, 0)))
        wire.append(pl.BlockSpec((N_DEV - 1, tr, width), lambda i, me: (0, i, 0)))
        shapes.append(jax.ShapeDtypeStruct((rows, width), F32))
    out = pl.pallas_call(
        body, name="grad_sum_adamw", out_shape=shapes * 4,
        grid_spec=pltpu.PrefetchScalarGridSpec(
            num_scalar_prefetch=1, grid=(ROW_TILES,), in_specs=own_specs + wire + flat * 3, out_specs=flat * 4),
        compiler_params=_params(40),
    )(me, *g_list, *recv_list, *w_list, *m_list, *v_list)
    return [out[k * n:(k + 1) * n] for k in range(4)]


SMALL_NAMES = ("norm_0", "a_v_norm_0", "b_scale_0", "norm_1", "final_norm", "a_spatial_w_0", "a_spatial_b_0", "sink_1")
SMALL_VIEWS = ((8, LANES),) * 5 + ((4 * CHUNK, LANES), (4, LANES), (1, N_HEADS))
SMALL_ROW0 = (0, 8, 16, 24, 32, 40, 552, 560)
SMALL_ROWS = 568


def _small_sum_adamw(early, late, w_list, m_list, v_list):
    n = len(w_list)

    def body(e_ref, l_ref, *refs):
        gtot, first = e_ref[0], l_ref[0]
        for d in range(1, N_DEV):
            gtot = gtot + e_ref[d]
            first = first + l_ref[d]
        for a, ((rows, width), r0) in enumerate(zip(SMALL_VIEWS, SMALL_ROW0)):
            g = first if SMALL_NAMES[a] == "norm_0" else gtot[r0:r0 + rows, 0:width]
            delta, m2, v2 = _adamw_math(refs[a][...], g, refs[n + a][...], refs[2 * n + a][...])
            for k, val in enumerate((g, delta, m2, v2)):
                refs[(3 + k) * n + a][...] = val
        refs[7 * n][...] = gtot[LOSS_ROW:LOSS_ROW + 1, LOSS_LANE:LOSS_LANE + 1]

    vm = pl.BlockSpec(memory_space=pltpu.VMEM)
    shapes = [jax.ShapeDtypeStruct(s, F32) for s in SMALL_VIEWS]
    out = pl.pallas_call(
        body, name="small_sum_adamw", out_shape=shapes * 4 + [jax.ShapeDtypeStruct((1, 1), F32)],
        in_specs=[vm, vm] + [vm] * (3 * n), out_specs=[vm] * (4 * n + 1),
    )(early, late, *w_list, *m_list, *v_list)
    return [out[k * n:(k + 1) * n] for k in range(4)], out[4 * n]


PEER_FLIPS = tuple((fx, fy, fc) for fx in (0, 1) for fy in (0, 1) for fc in (0, 1))[1:]


def _sequencer_all_gather(blks, name, collective_id, concat_rows=False):
    n = len(blks)

    def body(*refs):
        ins, outs = refs[:n], refs[n:2 * n]
        send_sems, recv_sems, local_sems = refs[2 * n:]
        x, y, c = lax.axis_index("x"), lax.axis_index("y"), lax.axis_index("c")
        peers = [(x ^ fx, y ^ fy, c ^ fc) for fx, fy, fc in PEER_FLIPS]
        barrier = pltpu.get_barrier_semaphore()
        for peer in peers:
            pl.semaphore_signal(barrier, inc=1, device_id=peer, device_id_type=MESH)
        pl.semaphore_wait(barrier, len(peers))
        me = 4 * x + 2 * y + c

        def slot(a):
            rows = blks[a].shape[0]
            return outs[a].at[pl.ds(pl.multiple_of(me * rows, 16), rows)] if concat_rows else outs[a].at[me]

        copies = [pltpu.make_async_remote_copy(
            src_ref=ins[a], dst_ref=slot(a), send_sem=send_sems.at[k, a], recv_sem=recv_sems.at[k, a],
            device_id=peer, device_id_type=MESH) for k, peer in enumerate(peers) for a in range(n)]
        mine = [pltpu.make_async_copy(ins[a], slot(a), local_sems.at[a]) for a in range(n)]
        for cp in copies + mine:
            cp.start()
        for cp in copies + mine:
            cp.wait()

    out_shape = (lambda t: (N_DEV * t.shape[0],) + t.shape[1:]) if concat_rows else (lambda t: (N_DEV,) + t.shape)
    return pl.kernel(
        body, out_type=[jax.ShapeDtypeStruct(out_shape(t), t.dtype) for t in blks],
        mesh=plsc.ScalarSubcoreMesh(axis_name="sequencer", num_cores=1), name=name,
        scratch_types=[pltpu.SemaphoreType.DMA((7, n)), pltpu.SemaphoreType.DMA((7, n)), pltpu.SemaphoreType.DMA((n,))],
        compiler_params=pltpu.CompilerParams(collective_id=collective_id),
    )(*blks)


def _sequencer_scatter(g_list, name, collective_id):
    n = len(g_list)

    def body(*refs):
        ins, outs = refs[:n], refs[n:2 * n]
        send_sems, recv_sems = refs[2 * n:]
        x, y, c = lax.axis_index("x"), lax.axis_index("y"), lax.axis_index("c")
        peers = [(x ^ fx, y ^ fy, c ^ fc) for fx, fy, fc in PEER_FLIPS]
        barrier = pltpu.get_barrier_semaphore()
        for peer in peers:
            pl.semaphore_signal(barrier, inc=1, device_id=peer, device_id_type=MESH)
        pl.semaphore_wait(barrier, len(peers))
        copies = [pltpu.make_async_remote_copy(
            src_ref=ins[a].at[4 * px + 2 * py + pc], dst_ref=outs[a].at[k], send_sem=send_sems.at[k, a],
            recv_sem=recv_sems.at[k, a], device_id=(px, py, pc), device_id_type=MESH)
            for k, (px, py, pc) in enumerate(peers) for a in range(n)]
        for cp in copies:
            cp.start()
        for cp in copies:
            cp.wait()

    return pl.kernel(
        body, out_type=[jax.ShapeDtypeStruct((N_DEV - 1,) + g.shape[1:], g.dtype) for g in g_list],
        mesh=plsc.ScalarSubcoreMesh(axis_name="sequencer", num_cores=1), name=name,
        scratch_types=[pltpu.SemaphoreType.DMA((7, n)), pltpu.SemaphoreType.DMA((7, n))],
        compiler_params=pltpu.CompilerParams(collective_id=collective_id),
    )(*g_list)


def _direct_all_gather(blk, name):
    def body(g_ref, out_ref, send_sems, recv_sems, local_sem):
        x, y, c = lax.axis_index("x"), lax.axis_index("y"), lax.axis_index("c")
        me = 4 * x + 2 * y + c
        copies = [pltpu.make_async_remote_copy(
            src_ref=g_ref, dst_ref=out_ref.at[me], send_sem=send_sems.at[k], recv_sem=recv_sems.at[k],
            device_id=(x ^ fx, y ^ fy, c ^ fc), device_id_type=MESH) for k, (fx, fy, fc) in enumerate(PEER_FLIPS)]
        copies.append(pltpu.make_async_copy(g_ref, out_ref.at[me], local_sem))
        for cp in copies:
            cp.start()
        for cp in copies:
            cp.wait()

    any_spec = pl.BlockSpec(memory_space=pl.ANY)
    return pl.pallas_call(
        body, name=name, out_shape=jax.ShapeDtypeStruct((N_DEV,) + blk.shape, blk.dtype),
        in_specs=[any_spec], out_specs=any_spec,
        scratch_shapes=[pltpu.SemaphoreType.DMA((7,)), pltpu.SemaphoreType.DMA((7,)), pltpu.SemaphoreType.DMA],
    )(blk)


def _shard_views(w_in_0, b_group_w_0, w_out_0, w_in_1, w_out_1):
    return [w_in_0, b_group_w_0.reshape(4 * 32, GDIM), w_out_0, w_in_1, w_out_1]


def _small_views(named):
    return [named[name].reshape(view) for name, view in zip(SMALL_NAMES, SMALL_VIEWS)]


LOSS_ROW, LOSS_LANE = 560, N_HEADS


def _pack_small_grads(named, loss_part):
    rows = []
    for name, (r, w) in zip(SMALL_NAMES, SMALL_VIEWS):
        pad_r = -r % 8
        if name == "sink_1":
            t = jnp.concatenate([named[name].reshape(r, w), loss_part], axis=1)
            rows.append(jnp.pad(t, ((0, pad_r), (0, LANES - w - 1))))
        elif name in named:
            rows.append(jnp.pad(named[name].reshape(r, w), ((0, pad_r), (0, LANES - w))))
        else:
            rows.append(jnp.zeros((r + pad_r, LANES), F32))
    return jnp.concatenate(rows, axis=0)


def _device_blocks(t, axis):
    shape = t.shape
    t = t.reshape(shape[:axis] + (N_DEV, shape[axis] // N_DEV) + shape[axis + 1:])
    t = jnp.moveaxis(t, axis, 0)
    return t.reshape(N_DEV, -1, shape[-1] if axis != len(shape) - 1 else shape[-1] // N_DEV)


def kernel(x, norm_0, w_in_0, a_v_norm_0, a_spatial_w_0, a_spatial_b_0, b_group_w_0, b_scale_0, w_out_0, norm_1, w_in_1, sink_1, w_out_1, final_norm, loss_target, m_norm_0, m_w_in_0, m_a_v_norm_0, m_a_spatial_w_0, m_a_spatial_b_0, m_b_group_w_0, m_b_scale_0, m_w_out_0, m_norm_1, m_w_in_1, m_sink_1, m_w_out_1, m_final_norm, v_norm_0, v_w_in_0, v_a_v_norm_0, v_a_spatial_w_0, v_a_spatial_b_0, v_b_group_w_0, v_b_scale_0, v_w_out_0, v_norm_1, v_w_in_1, v_sink_1, v_w_out_1, v_final_norm):
    seq = x.shape[1]
    xs = x.reshape(seq, D)
    tgt = loss_target.reshape(seq, D)
    ax, ay, ac = lax.axis_index("x"), lax.axis_index("y"), lax.axis_index("c")
    me = jnp.reshape(4 * ax + 2 * ay + ac, (1,)).astype(jnp.int32)

    shards = _shard_views(w_in_0, b_group_w_0, w_out_0, w_in_1, w_out_1)
    cast = _cast_shards([shards[0], shards[1], shards[2], w_in_1.T, shards[4]])

    def l1_weights(after):
        blks, _ = lax.optimization_barrier((cast[3:5], after))
        return _sequencer_all_gather(blks, "weights_gather_l1", 2, concat_rows=True)

    blocks, received, early = {}, {}, {}
    collective_ids = {"l1": 3, "out0": 4, "in0": 5}

    def scatter(tag, own_blocks, wire_blocks):
        blocks[tag] = own_blocks
        received[tag] = _sequencer_scatter(wire_blocks, "grad_scatter_" + tag, collective_ids[tag])

    def small_early(named, loss_part):
        early["small"] = _sequencer_all_gather([_pack_small_grads(named, loss_part)], "small_grad_gather", 6)[0]

    grad_x, d_norm_0 = _local_step(xs, tgt, cast[0], cast[1:3], l1_weights, norm_0, a_v_norm_0, a_spatial_w_0,
                                   a_spatial_b_0, b_scale_0, norm_1, sink_1, final_norm, scatter, small_early)

    order = (("in0", 0), ("in0", 1), ("out0", 0), ("l1", 0), ("l1", 1))
    late = _direct_all_gather(d_norm_0.reshape(8, LANES), "norm_grad_gather")
    shards_late, _ = lax.optimization_barrier((shards, grad_x))
    big = _final_sum_adamw([blocks[t][i] for t, i in order], [received[t][i] for t, i in order], me, shards_late,
                           _shard_views(m_w_in_0, m_b_group_w_0, m_w_out_0, m_w_in_1, m_w_out_1),
                           _shard_views(v_w_in_0, v_b_group_w_0, v_w_out_0, v_w_in_1, v_w_out_1))
    weights = dict(norm_0=norm_0, a_v_norm_0=a_v_norm_0, a_spatial_w_0=a_spatial_w_0, a_spatial_b_0=a_spatial_b_0,
                   b_scale_0=b_scale_0, norm_1=norm_1, sink_1=sink_1, final_norm=final_norm)
    m_small = dict(norm_0=m_norm_0, a_v_norm_0=m_a_v_norm_0, a_spatial_w_0=m_a_spatial_w_0, a_spatial_b_0=m_a_spatial_b_0,
                   b_scale_0=m_b_scale_0, norm_1=m_norm_1, sink_1=m_sink_1, final_norm=m_final_norm)
    v_small = dict(norm_0=v_norm_0, a_v_norm_0=v_a_v_norm_0, a_spatial_w_0=v_a_spatial_w_0, a_spatial_b_0=v_a_spatial_b_0,
                   b_scale_0=v_b_scale_0, norm_1=v_norm_1, sink_1=v_sink_1, final_norm=v_final_norm)
    small, loss = _small_sum_adamw(early["small"], late, _small_views(weights), _small_views(m_small),
                                   _small_views(v_small))

    def in_order(kind):
        b = [b_.reshape(s_.shape) for b_, s_ in zip(big[kind], (w_in_0, b_group_w_0, w_out_0, w_in_1, w_out_1))]
        s = {name: t.reshape(weights[name].shape) for name, t in zip(SMALL_NAMES, small[kind])}
        return [s["norm_0"], b[0], s["a_v_norm_0"], s["a_spatial_w_0"], s["a_spatial_b_0"], b[1], s["b_scale_0"], b[2],
                s["norm_1"], b[3], s["sink_1"], b[4], s["final_norm"]]

    return (loss[0, 0], grad_x.reshape(1, seq, D), *in_order(0), *in_order(1), *in_order(2), *in_order(3))


def _local_step(xs, tgt, win0_shard, l0_shards, l1_weights, norm_0, a_v_norm_0, a_spatial_w_0, a_spatial_b_0, b_scale_0,
                norm_1, sink_1, final_norm, scatter, small_early):
    seq = xs.shape[0]
    ws = a_spatial_w_0.astype(BF16)
    ws_t = jnp.swapaxes(ws, 1, 2)
    bias = jnp.repeat(a_spatial_b_0.T, GDIM, axis=1)
    g0, gv, scale, g1, gf = (t.reshape(1, D) for t in (norm_0, a_v_norm_0, b_scale_0, norm_1, final_norm))
    cos_t, sin_t = _rope_tables_t(seq)

    za, bx, bg, h0_t, win0, g_wg, wout0 = _l0_in_proj(xs, g0, win0_shard, l0_shards)
    win1_t, wout1 = l1_weights(za)
    wg = g_wg.reshape(N_DEV, 4, 32, GDIM).transpose(1, 0, 2, 3).reshape(4, GDIM, GDIM)
    wg_t = jnp.swapaxes(wg, 1, 2)
    x1 = _l0_mix_fwd(za, bx, bg, xs, ws, bias, gv, wg, scale, wout0)
    win1_t, wout1, x1 = lax.optimization_barrier((win1_t, wout1, x1))
    qt, kt, vt, gatet, h1_t = _l1_in_proj(x1, g1, win1_t, cos_t, sin_t)
    dx2, dx2b, att, lse, loss_part, d_gf, d_wout1, d_wout1_wire = _l1_attn_fwd(
        qt, kt, vt, gatet, x1, tgt, wout1, gf, sink_1)

    dq_r, dgate, dk_pad, dv_pad, d_sink = _l1_attn_bwd(dx2b, wout1, qt, kt, vt, gatet, att, lse, sink_1)
    dk_r = dk_pad[:, BLK:BLK + seq]
    dv = dv_pad[:, BLK:BLK + seq]
    dx1, dx1b, dz1_t, d_g1 = _l1_in_proj_bwd(dq_r, dk_r, dv, dgate, cos_t, sin_t, win1_t, x1, g1, dx2)
    d_win1, d_win1_wire = _dw_matmul(h1_t, dz1_t, "dw_in_1", b_transposed=True, tn=1280, col_block=MIX1_IN // N_DEV)
    rows = lambda t: t.reshape(N_DEV, t.shape[0] // N_DEV, t.shape[1])
    scatter("l1", [d_win1, rows(d_wout1)], [d_win1_wire, rows(d_wout1_wire)])

    dz0, dp, cat_t, d_ws, _, d_gv, d_scale, d_wg, d_b = _l0_mix_bwd(
        dx1b, wout0, za, bx, bg, ws, ws_t, bias, gv, wg, wg_t, scale)
    dz0 = _l0_pool_bwd(dp, dz0)
    d_win0, d_win0_wire = _dw_matmul(h0_t, dz0, "dw_in_0", tn=1280, col_block=MIX0_IN // N_DEV)
    d_wg_blocks = _device_blocks(d_wg, 1)
    scatter("in0", [d_win0, d_wg_blocks], [d_win0_wire, d_wg_blocks])
    cat_t, _ = lax.optimization_barrier((cat_t, d_win0))
    d_wout0, d_wout0_wire = _dw_matmul(cat_t, dx1b, "dw_out_0")
    scatter("out0", [rows(d_wout0)], [rows(d_wout0_wire)])
    small_early(dict(a_v_norm_0=d_gv, a_spatial_w_0=d_ws, a_spatial_b_0=d_b.reshape(4, 8, CHUNK)[:, 0, :],
                     b_scale_0=d_scale, norm_1=d_g1, sink_1=d_sink[:, 0], final_norm=d_gf), loss_part)
    dz0, _ = lax.optimization_barrier((dz0, d_wout0))
    return _l0_in_proj_bwd(dz0, win0, xs, g0, dx1)
```
